```python
import math
import jax, jax.numpy as jnp
from jax import lax
import numpy as np

D_MODEL = 1024
BATCH = 4
SEQ = 4096
DEPTH = 1

EPS = 1e-6
MEM_LEN = 256
SSM_WIDTH = D_MODEL // 2
SSM_GROUP = 16
SSM_GROUPS = SSM_WIDTH // SSM_GROUP
SSM_STATE = 64
SGU_WIDTH = D_MODEL // 2
SGU_HEADS = 4
SGU_HEAD_DIM = SGU_WIDTH // SGU_HEADS
CHUNK = 128
XA_HEADS = 4
XA_HEAD_DIM = 128
XA_WIDTH = XA_HEADS * XA_HEAD_DIM
N_BRANCHES = 3
BRANCH_WIDTH = 512
IN_COLS = SSM_WIDTH + 2 * SGU_WIDTH + XA_WIDTH
N_GROUPS = 8
EXPERTS_PER_GROUP = 8
N_EXPERTS = N_GROUPS * EXPERTS_PER_GROUP
TOP_K = 2
D_FF_EXPERT = D_MODEL // 2
MOE_BLOCK = 128

kernel_name = "hybrid_s5_gmlp_memxattn_hmoe"


def rmsnorm(x, g):
    xf = x.astype(jnp.float32)
    y = xf * lax.rsqrt(jnp.mean(xf * xf, axis=-1, keepdims=True) + EPS)
    return (y * g.astype(jnp.float32)).astype(x.dtype)


def s5_ssm(u, lam_re, lam_im, log_dt, b_re, b_im, c_re, c_im, d_skip):
    f32 = jnp.float32
    bsz, s, _ = u.shape
    uf = u.astype(f32).reshape(bsz, s, SSM_GROUPS, SSM_GROUP)
    lam = lax.complex(lam_re.astype(f32), lam_im.astype(f32))
    dt = jnp.exp(log_dt.astype(f32))[:, None]
    lam_bar = jnp.exp(lam * dt)
    b = lax.complex(b_re.astype(f32), b_im.astype(f32))
    b_bar = ((lam_bar - 1.0) / lam)[..., None] * b
    bu = lax.complex(jnp.einsum('bsgh,gph->bsgp', uf, jnp.real(b_bar)),
                     jnp.einsum('bsgh,gph->bsgp', uf, jnp.imag(b_bar)))
    a = jnp.broadcast_to(lam_bar, bu.shape)

    def combine(left, right):
        a_l, x_l = left
        a_r, x_r = right
        return a_r * a_l, a_r * x_l + x_r

    _, states = lax.associative_scan(combine, (a, bu), axis=1)
    y = (jnp.einsum('bsgp,ghp->bsgh', jnp.real(states), c_re.astype(f32))
         - jnp.einsum('bsgp,ghp->bsgh', jnp.imag(states), c_im.astype(f32)))
    y = y + d_skip.astype(f32).reshape(SSM_GROUPS, SSM_GROUP) * uf
    return y.reshape(bsz, s, SSM_WIDTH).astype(u.dtype)


def spatial_gating(u, v, g_sgu, w_spatial, b_spatial):
    bsz, s, _ = v.shape
    v = rmsnorm(v, g_sgu)
    vc = v.reshape(bsz, s // CHUNK, CHUNK, SGU_HEADS, SGU_HEAD_DIM)
    mask = jnp.tril(jnp.ones((CHUNK, CHUNK), dtype=bool))
    w = jnp.where(mask, w_spatial, jnp.zeros_like(w_spatial))
    sv = jnp.einsum('hts,bcshd->bcthd', w, vc) + b_spatial.T[None, None, :, :, None]
    return u * sv.reshape(bsz, s, SGU_WIDTH)


def memory_cross_attention(q, mem_n, w_kv):
    bsz, s, _ = q.shape
    k, v = jnp.split(mem_n @ w_kv, 2, axis=-1)
    qh = q.reshape(bsz, s, XA_HEADS, XA_HEAD_DIM)
    kh = k.reshape(bsz, -1, XA_HEADS, XA_HEAD_DIM)
    vh = v.reshape(bsz, -1, XA_HEADS, XA_HEAD_DIM)
    scores = jnp.einsum('bshd,bmhd->bhsm', qh, kh).astype(jnp.float32) * (XA_HEAD_DIM ** -0.5)
    p = jax.nn.softmax(scores, axis=-1).astype(vh.dtype)
    o = jnp.einsum('bhsm,bmhd->bshd', p, vh)
    return o.reshape(bsz, s, XA_WIDTH)


def hybrid_mixer(h, mem, g_mix, g_mem, w_in, w_gate, b_gate, lam_re, lam_im, log_dt,
                 b_re, b_im, c_re, c_im, d_skip, w_glu, b_glu, g_sgu, w_spatial, b_spatial,
                 w_kv, w_branch, w_out):
    bsz, s, d = h.shape
    n = rmsnorm(h, g_mix)
    proj = n @ w_in
    x_ssm = proj[..., :SSM_WIDTH]
    uv = jax.nn.gelu(proj[..., SSM_WIDTH:SSM_WIDTH + 2 * SGU_WIDTH])
    q = proj[..., SSM_WIDTH + 2 * SGU_WIDTH:]
    y_s = jax.nn.gelu(s5_ssm(x_ssm, lam_re, lam_im, log_dt, b_re, b_im, c_re, c_im, d_skip))
    glu_a, glu_b = jnp.split(y_s @ w_glu + b_glu, 2, axis=-1)
    y_a = glu_a * jax.nn.sigmoid(glu_b)
    u, v = jnp.split(uv, 2, axis=-1)
    y_b = spatial_gating(u, v, g_sgu, w_spatial, b_spatial)
    y_c = memory_cross_attention(q, rmsnorm(mem, g_mem), w_kv)
    ys = jnp.stack([y_a, y_b, y_c], axis=2)
    br = jnp.einsum('bsic,icd->bsid', ys, w_branch)
    gates = jax.nn.sigmoid(n @ w_gate + b_gate).reshape(bsz, s, N_BRANCHES, d)
    merged = jnp.sum(gates * br, axis=2)
    return merged @ w_out


def hierarchical_moe(x, w_group, b_group, w_router, b_router, w1, w3, w2):
    bsz, s, d = x.shape
    t = bsz * s
    xf = x.reshape(t, d)
    group_logits = (xf @ w_group).astype(jnp.float32) + b_group.astype(jnp.float32)
    group_probs = jax.nn.softmax(group_logits, axis=-1)
    g_idx = jnp.argmax(group_logits, axis=-1)
    g_w = jnp.take_along_axis(group_probs, g_idx[:, None], axis=-1)[:, 0]
    all_logits = ((xf @ w_router).astype(jnp.float32) + b_router.astype(jnp.float32))
    all_logits = all_logits.reshape(t, N_GROUPS, EXPERTS_PER_GROUP)
    e_logits = jnp.take_along_axis(all_logits, g_idx[:, None, None], axis=1)[:, 0]
    top_vals, top_loc = lax.top_k(e_logits, TOP_K)
    e_w = jax.nn.softmax(top_vals, axis=-1)
    expert_ids = g_idx[:, None] * EXPERTS_PER_GROUP + top_loc
    weights = g_w[:, None] * e_w

    flat_e = expert_ids.reshape(-1)
    flat_tok = jnp.repeat(jnp.arange(t), TOP_K)
    flat_w = weights.reshape(-1)
    order = jnp.argsort(flat_e)
    sorted_e = flat_e[order]
    sorted_tok = flat_tok[order]
    sorted_w = flat_w[order]
    counts = jnp.bincount(flat_e, length=N_EXPERTS)
    padded = (counts + MOE_BLOCK - 1) // MOE_BLOCK * MOE_BLOCK
    start = jnp.cumsum(counts) - counts
    block_end = jnp.cumsum(padded)
    pstart = block_end - padded
    dest = pstart[sorted_e] + (jnp.arange(t * TOP_K) - start[sorted_e])
    n_blocks = -(-(t * TOP_K) // MOE_BLOCK) + N_EXPERTS
    buf = jnp.zeros((n_blocks * MOE_BLOCK, d), x.dtype).at[dest].set(xf[sorted_tok])
    block_expert = jnp.searchsorted(block_end, jnp.arange(n_blocks) * MOE_BLOCK, side='right')
    block_expert = jnp.minimum(block_expert, N_EXPERTS - 1)

    def expert_block(args):
        xb, e = args
        return (jax.nn.silu(xb @ w1[e]) * (xb @ w3[e])) @ w2[e]

    yb = lax.map(expert_block, (buf.reshape(n_blocks, MOE_BLOCK, d), block_expert))
    y_sorted = yb.reshape(-1, d)[dest] * sorted_w[:, None].astype(x.dtype)
    y = jnp.zeros((t, d), x.dtype).at[sorted_tok].add(y_sorted)
    return y.reshape(bsz, s, d)


def setup_inputs(seed: int = 0) -> dict:
    key = jax.random.key(seed)
    ks = iter(jax.random.split(key, 32))
    L = DEPTH
    f32 = jnp.float32

    def nrm(shape, scale):
        return jax.random.normal(next(ks), shape, f32) * scale

    def gain(shape):
        return 1.0 + nrm(shape, 0.01)

    x = nrm((BATCH, SEQ, D_MODEL), 1.0)
    mem = nrm((BATCH, MEM_LEN, D_MODEL), 1.0)
    g_mix = gain((L, D_MODEL))
    g_mem = gain((L, D_MODEL))
    w_in = nrm((L, D_MODEL, IN_COLS), D_MODEL ** -0.5)
    w_gate = nrm((L, D_MODEL, N_BRANCHES * D_MODEL), D_MODEL ** -0.5)
    b_gate = nrm((L, N_BRANCHES * D_MODEL), 0.01)
    lam_re = -0.5 + nrm((L, SSM_GROUPS, SSM_STATE), 0.01)
    lam_im = (jnp.pi * jnp.arange(SSM_STATE, dtype=f32))[None, None, :] + nrm((L, SSM_GROUPS, SSM_STATE), 0.01)
    log_dt = jax.random.uniform(next(ks), (L, SSM_GROUPS), f32, minval=math.log(1e-3), maxval=math.log(1e-1))
    b_re = nrm((L, SSM_GROUPS, SSM_STATE, SSM_GROUP), (2 * SSM_GROUP) ** -0.5)
    b_im = nrm((L, SSM_GROUPS, SSM_STATE, SSM_GROUP), (2 * SSM_GROUP) ** -0.5)
    c_re = nrm((L, SSM_GROUPS, SSM_GROUP, SSM_STATE), (2 * SSM_STATE) ** -0.5)
    c_im = nrm((L, SSM_GROUPS, SSM_GROUP, SSM_STATE), (2 * SSM_STATE) ** -0.5)
    d_skip = nrm((L, SSM_WIDTH), 1.0)
    w_glu = nrm((L, SSM_WIDTH, 2 * SSM_WIDTH), SSM_WIDTH ** -0.5)
    b_glu = nrm((L, 2 * SSM_WIDTH), 0.01)
    g_sgu = gain((L, SGU_WIDTH))
    w_spatial = nrm((L, SGU_HEADS, CHUNK, CHUNK), 0.5 * CHUNK ** -0.5)
    b_spatial = gain((L, SGU_HEADS, CHUNK))
    w_kv = nrm((L, D_MODEL, 2 * XA_WIDTH), D_MODEL ** -0.5)
    w_branch = nrm((L, N_BRANCHES, BRANCH_WIDTH, D_MODEL), BRANCH_WIDTH ** -0.5)
    w_out = nrm((L, D_MODEL, D_MODEL), D_MODEL ** -0.5)
    g_ffn = gain((L, D_MODEL))
    w_group = nrm((L, D_MODEL, N_GROUPS), D_MODEL ** -0.5)
    b_group = nrm((L, N_GROUPS), 0.01)
    w_router = nrm((L, D_MODEL, N_EXPERTS), D_MODEL ** -0.5)
    b_router = nrm((L, N_EXPERTS), 0.01)
    w1 = nrm((L, N_EXPERTS, D_MODEL, D_FF_EXPERT), D_MODEL ** -0.5)
    w3 = nrm((L, N_EXPERTS, D_MODEL, D_FF_EXPERT), D_MODEL ** -0.5)
    w2 = nrm((L, N_EXPERTS, D_FF_EXPERT, D_MODEL), D_FF_EXPERT ** -0.5)
    g_final = gain((D_MODEL,))
    return {"x": x, "mem": mem, "g_mix": g_mix, "g_mem": g_mem, "w_in": w_in,
            "w_gate": w_gate, "b_gate": b_gate, "lam_re": lam_re, "lam_im": lam_im,
            "log_dt": log_dt, "b_re": b_re, "b_im": b_im, "c_re": c_re, "c_im": c_im,
            "d_skip": d_skip, "w_glu": w_glu, "b_glu": b_glu, "g_sgu": g_sgu,
            "w_spatial": w_spatial, "b_spatial": b_spatial, "w_kv": w_kv,
            "w_branch": w_branch, "w_out": w_out, "g_ffn": g_ffn, "w_group": w_group,
            "b_group": b_group, "w_router": w_router, "b_router": b_router,
            "w1": w1, "w3": w3, "w2": w2, "g_final": g_final}


def reference(x, mem, g_mix, g_mem, w_in, w_gate, b_gate, lam_re, lam_im, log_dt, b_re, b_im,
              c_re, c_im, d_skip, w_glu, b_glu, g_sgu, w_spatial, b_spatial, w_kv, w_branch,
              w_out, g_ffn, w_group, b_group, w_router, b_router, w1, w3, w2, g_final):
    h = x
    for l in range(DEPTH):
        h = h + hybrid_mixer(h, mem, g_mix[l], g_mem[l], w_in[l], w_gate[l], b_gate[l],
                             lam_re[l], lam_im[l], log_dt[l], b_re[l], b_im[l], c_re[l],
                             c_im[l], d_skip[l], w_glu[l], b_glu[l], g_sgu[l], w_spatial[l],
                             b_spatial[l], w_kv[l], w_branch[l], w_out[l])
        h = h + hierarchical_moe(rmsnorm(h, g_ffn[l]), w_group[l], b_group[l], w_router[l],
                                 b_router[l], w1[l], w3[l], w2[l])
    return rmsnorm(h, g_final)
```

```python
import functools
import math

import jax
import jax.numpy as jnp
from jax import lax
from jax.experimental import pallas as pl
from jax.experimental.pallas import tpu as pltpu

F32 = jnp.float32
BF16 = jnp.bfloat16

EPS = 1e-6
D_MODEL = 1024
SSM_WIDTH = 512
SSM_GROUP = 16
SSM_GROUPS = 32
SSM_STATE = 64
SSM_CHUNK = 16
SGU_WIDTH = 512
SGU_HEADS = 4
SGU_HEAD_DIM = 128
CHUNK = 128
XA_HEADS = 4
XA_HEAD_DIM = 128
N_GROUPS = 8
EXPERTS_PER_GROUP = 8
N_EXPERTS = 64
TOP_K = 2
D_FF = 512
LANES = 128
ROUTE_LANE0 = N_GROUPS

TM_IN = 512
TM_MERGE = 256
TM_OUT = 512
BM = 256
VMEM_LIMIT = 56 * 1024 * 1024


def _rms(x, g):
    return x * lax.rsqrt(jnp.mean(x * x, axis=-1, keepdims=True) + EPS) * g


def _sigmoid(x):
    return 0.5 * (1.0 + jnp.tanh(0.5 * x))


def _gelu(x):
    c = math.sqrt(2.0 / math.pi)
    return 0.5 * x * (1.0 + jnp.tanh(c * (x + 0.044715 * (x * x * x))))


def _dot(a, b):
    return jnp.dot(a, b, preferred_element_type=F32)


def _pack_bf16_pair(x):
    n = x.shape[1] // 2
    lo = lax.bitcast_convert_type(x[:, :n].astype(BF16).astype(F32), jnp.uint32)
    hi = lax.bitcast_convert_type(x[:, n:].astype(BF16).astype(F32), jnp.uint32)
    return hi | (lo >> 16)


def _unpack_bf16_pair(p):
    lo = lax.bitcast_convert_type(p << 16, F32)
    hi = lax.bitcast_convert_type(p & jnp.uint32(0xFFFF0000), F32)
    return jnp.concatenate([lo, hi], axis=1)


def _kv_body(mem_ref, g_ref, w_ref, k_ref, v_ref):
    n = _rms(mem_ref[0], g_ref[...]).astype(BF16)
    kv = _dot(n, w_ref[...])
    k_ref[0] = kv[:, :SGU_WIDTH].astype(BF16)
    v_ref[0] = kv[:, SGU_WIDTH:].astype(BF16)


def _kv_proj(mem, g_mem, w_kv):
    b, m, d = mem.shape
    return pl.pallas_call(
        _kv_body,
        grid=(b,),
        in_specs=[pl.BlockSpec((1, m, d), lambda i: (i, 0, 0)),
                  pl.BlockSpec((1, d), lambda i: (0, 0)),
                  pl.BlockSpec((d, 2 * SGU_WIDTH), lambda i: (0, 0))],
        out_specs=[pl.BlockSpec((1, m, SGU_WIDTH), lambda i: (i, 0, 0)),
                   pl.BlockSpec((1, m, SGU_WIDTH), lambda i: (i, 0, 0))],
        out_shape=[jax.ShapeDtypeStruct((b, m, SGU_WIDTH), BF16),
                   jax.ShapeDtypeStruct((b, m, SGU_WIDTH), BF16)],
        compiler_params=pltpu.CompilerParams(dimension_semantics=("arbitrary",),
                                             vmem_limit_bytes=VMEM_LIMIT),
        name="kv_proj",
    )(mem, g_mem, w_kv)


def _in_body(x_ref, gmix_ref, win_ref, gsgu_ref, wsp_ref, bsp_ref, k_ref, v_ref,
             xssm_ref, yb_ref, yc_ref):
    n = _rms(x_ref[0], gmix_ref[...]).astype(BF16)
    proj = _dot(n, win_ref[...])
    xssm_ref[0] = proj[:, :SSM_WIDTH].astype(BF16)

    uv = _gelu(proj[:, SSM_WIDTH:SSM_WIDTH + 2 * SGU_WIDTH])
    u = uv[:, :SGU_WIDTH]
    v = _rms(uv[:, SGU_WIDTH:], gsgu_ref[...]).astype(BF16)
    tm = u.shape[0]
    rows = []
    for c in range(tm // CHUNK):
        vc = v[c * CHUNK:(c + 1) * CHUNK]
        heads = []
        for h in range(SGU_HEADS):
            sl = slice(h * SGU_HEAD_DIM, (h + 1) * SGU_HEAD_DIM)
            heads.append(_dot(wsp_ref[h], vc[:, sl]) + bsp_ref[h])
        rows.append(jnp.concatenate(heads, axis=1))
    sv = jnp.concatenate(rows, axis=0)
    yb_ref[0] = (u * sv).astype(BF16)

    q = proj[:, SSM_WIDTH + 2 * SGU_WIDTH:].astype(BF16)
    kk = k_ref[0]
    vv = v_ref[0]
    outs = []
    for h in range(XA_HEADS):
        sl = slice(h * XA_HEAD_DIM, (h + 1) * XA_HEAD_DIM)
        s = lax.dot_general(q[:, sl], kk[:, sl], (((1,), (1,)), ((), ())),
                            preferred_element_type=F32) * (XA_HEAD_DIM ** -0.5)
        e = jnp.exp(s - jnp.max(s, axis=-1, keepdims=True))
        l = jnp.sum(e, axis=-1, keepdims=True)
        outs.append(_dot(e.astype(BF16), vv[:, sl]) / l)
    yc_ref[0] = jnp.concatenate(outs, axis=1).astype(BF16)


def _in_proj(x, g_mix, w_in, g_sgu, w_sp, b_sp, k, v):
    b, s, d = x.shape
    m = k.shape[1]
    const2 = lambda i, j: (0, 0)
    const3 = lambda i, j: (0, 0, 0)
    tok = lambda i, j: (i, j, 0)
    per_b = lambda i, j: (i, 0, 0)
    out = jax.ShapeDtypeStruct((b, s, SSM_WIDTH), BF16)
    return pl.pallas_call(
        _in_body,
        grid=(b, s // TM_IN),
        in_specs=[pl.BlockSpec((1, TM_IN, d), tok),
                  pl.BlockSpec((1, d), const2),
                  pl.BlockSpec(w_in.shape, const2),
                  pl.BlockSpec((1, SGU_WIDTH), const2),
                  pl.BlockSpec(w_sp.shape, const3),
                  pl.BlockSpec(b_sp.shape, const3),
                  pl.BlockSpec((1, m, SGU_WIDTH), per_b),
                  pl.BlockSpec((1, m, SGU_WIDTH), per_b)],
        out_specs=[pl.BlockSpec((1, TM_IN, SSM_WIDTH), tok)] * 3,
        out_shape=[out, out, out],
        compiler_params=pltpu.CompilerParams(dimension_semantics=("arbitrary", "arbitrary"),
                                             vmem_limit_bytes=VMEM_LIMIT),
        name="in_proj",
    )(x, g_mix, w_in, g_sgu, w_sp, b_sp, k, v)


def _ssm_params(lam_re, lam_im, log_dt, b_re, b_im, c_re, c_im, d_skip):
    hp = lax.Precision.HIGHEST
    g, p = lam_re.shape
    dt = jnp.exp(log_dt)[:, None]
    ar = lam_re * dt
    ai = lam_im * dt

    def cpow(j):
        mag = jnp.exp(ar[:, None, :] * j[None, :, None])
        ph = ai[:, None, :] * j[None, :, None]
        return mag * jnp.cos(ph), mag * jnp.sin(ph)

    pr, pi = cpow(jnp.arange(SSM_CHUNK + 1, dtype=F32))
    nr = pr[:, 1] - 1.0
    ni = pi[:, 1]
    den = lam_re * lam_re + lam_im * lam_im
    fr = (nr * lam_re + ni * lam_im) / den
    fi = (ni * lam_re - nr * lam_im) / den
    bbr = fr[..., None] * b_re - fi[..., None] * b_im
    bbi = fr[..., None] * b_im + fi[..., None] * b_re

    L = SSM_CHUNK
    cr = c_re[:, None]
    ci = c_im[:, None]
    cpr = cr * pr[:, :L, None, :] - ci * pi[:, :L, None, :]
    cpi = cr * pi[:, :L, None, :] + ci * pr[:, :L, None, :]
    kern = (jnp.einsum('glop,gpi->gloi', cpr, bbr, precision=hp)
            - jnp.einsum('glop,gpi->gloi', cpi, bbi, precision=hp))
    t_idx = jnp.arange(L)[None, :] - jnp.arange(L)[:, None]
    kt = kern[:, jnp.clip(t_idx, 0, L - 1)]
    kt = jnp.where((t_idx >= 0)[None, :, :, None, None], kt, 0.0)
    toep = kt.transpose(0, 1, 4, 2, 3).reshape(g, L * SSM_GROUP, L * SSM_GROUP)

    rr = pr[:, L - 1::-1][:, :L]
    ri = pi[:, L - 1::-1][:, :L]
    bbr_t = bbr.transpose(0, 2, 1)[:, None]
    bbi_t = bbi.transpose(0, 2, 1)[:, None]
    n_re = rr[:, :, None, :] * bbr_t - ri[:, :, None, :] * bbi_t
    n_im = rr[:, :, None, :] * bbi_t + ri[:, :, None, :] * bbr_t
    nmat = jnp.concatenate([n_re, n_im], axis=-1).reshape(g, L * SSM_GROUP, 2 * p)

    clr = cr * pr[:, 1:, None, :] - ci * pi[:, 1:, None, :]
    cli = cr * pi[:, 1:, None, :] + ci * pr[:, 1:, None, :]
    mmat = jnp.concatenate([clr, -cli], axis=-1)
    mmat = mmat.transpose(0, 3, 1, 2).reshape(g, 2 * p, L * SSM_GROUP)

    qr, qi = cpow(jnp.asarray([float(L * 2 ** k) for k in range(8)], F32))
    lr = jnp.concatenate([qr, qr], axis=-1)
    li = jnp.concatenate([-qi, qi], axis=-1)
    d2 = jnp.tile(d_skip.reshape(g, 1, SSM_GROUP), (1, 1, L))
    return toep.astype(BF16), nmat.astype(BF16), mmat.astype(BF16), lr, li, d2


def _ssm_body(u_ref, toep_ref, n_ref, m_ref, lr_ref, li_ref, d2_ref, y_ref, *, n_seq):
    u = u_ref[0]
    rows = u.shape[0]
    per = rows // n_seq
    y = _dot(u, toep_ref[0])
    st = _dot(u, n_ref[0])
    row = lax.broadcasted_iota(jnp.int32, (per, LANES), 0)
    prev = []
    for b in range(n_seq):
        x = st[b * per:(b + 1) * per]
        k = 0
        while (1 << k) < per:
            d = 1 << k
            sh = jnp.where(row >= d, pltpu.roll(x, d, 0), 0.0)
            x = x + sh * lr_ref[0, k:k + 1, :] + pltpu.roll(sh, SSM_STATE, 1) * li_ref[0, k:k + 1, :]
            k += 1
        prev.append(jnp.where(row >= 1, pltpu.roll(x, 1, 0), 0.0))
    xp = jnp.concatenate(prev, axis=0).astype(BF16)
    y = y + _dot(xp, m_ref[0]) + d2_ref[0] * u.astype(F32)
    y_ref[0] = _gelu(y).astype(BF16)


def _ssm(u2, toep, nmat, mmat, lr, li, d2, n_seq):
    g, rows, w = u2.shape
    blk = lambda a: pl.BlockSpec((1,) + a.shape[1:], lambda i: (i, 0, 0))
    return pl.pallas_call(
        functools.partial(_ssm_body, n_seq=n_seq),
        grid=(g,),
        in_specs=[blk(u2), blk(toep), blk(nmat), blk(mmat), blk(lr), blk(li), blk(d2)],
        out_specs=blk(u2),
        out_shape=jax.ShapeDtypeStruct(u2.shape, BF16),
        compiler_params=pltpu.CompilerParams(dimension_semantics=("arbitrary",),
                                             vmem_limit_bytes=VMEM_LIMIT),
        name="ssm",
    )(u2, toep, nmat, mmat, lr, li, d2)


def _merge_body(x_ref, ys_ref, yb_ref, yc_ref, gmix_ref, wgate_ref, bgate_ref, wglu_ref, bglu_ref,
                wbr_ref, wout_ref, gffn_ref, wrt_ref, brt_ref,
                h_ref, xnp_ref, rt_ref, cnt_ref, carry_ref):
    i = pl.program_id(0)

    @pl.when(i == 0)
    def _():
        carry_ref[...] = jnp.zeros_like(carry_ref)

    x = x_ref[...]
    n = _rms(x, gmix_ref[...]).astype(BF16)
    gates = _sigmoid(_dot(n, wgate_ref[...]) + bgate_ref[...])
    glu = _dot(ys_ref[...], wglu_ref[...]) + bglu_ref[...]
    ya = (glu[:, :SSM_WIDTH] * _sigmoid(glu[:, SSM_WIDTH:])).astype(BF16)
    merged = (gates[:, :D_MODEL] * _dot(ya, wbr_ref[0])
              + gates[:, D_MODEL:2 * D_MODEL] * _dot(yb_ref[...], wbr_ref[1])
              + gates[:, 2 * D_MODEL:] * _dot(yc_ref[...], wbr_ref[2]))
    h = x + _dot(merged.astype(BF16), wout_ref[...])
    h_ref[...] = h

    xn = _rms(h, gffn_ref[...])
    xnp_ref[...] = _pack_bf16_pair(xn)

    logits = jnp.dot(xn, wrt_ref[...], precision=lax.Precision.HIGHEST,
                     preferred_element_type=F32) + brt_ref[...]
    tm = logits.shape[0]
    lane_i = lax.broadcasted_iota(jnp.int32, (tm, LANES), 1)
    lane = lane_i.astype(F32)
    neg = jnp.float32(-3.0e38)
    big = jnp.float32(LANES)
    gmask = lane_i < N_GROUPS
    gl = jnp.where(gmask, logits, neg)
    gmax = jnp.max(gl, axis=-1, keepdims=True)
    gidx = jnp.min(jnp.where(gl == gmax, lane, big), axis=-1, keepdims=True)
    gsum = jnp.sum(jnp.where(gmask, jnp.exp(gl - gmax), 0.0), axis=-1, keepdims=True)
    g_w = 1.0 / gsum
    e_lane = lane_i - ROUTE_LANE0
    lane_group = (e_lane >> 3).astype(F32)
    emask = (e_lane >= 0) & (e_lane < N_EXPERTS) & (lane_group == gidx)
    el = jnp.where(emask, logits, neg)
    m1 = jnp.max(el, axis=-1, keepdims=True)
    i1 = jnp.min(jnp.where(el == m1, lane, big), axis=-1, keepdims=True)
    el2 = jnp.where(lane == i1, neg, el)
    m2 = jnp.max(el2, axis=-1, keepdims=True)
    i2 = jnp.min(jnp.where(el2 == m2, lane, big), axis=-1, keepdims=True)
    t = jnp.exp(m2 - m1)
    w1 = g_w / (1.0 + t)
    w2 = g_w * t / (1.0 + t)

    sel1 = lane == i1
    sel2 = lane == i2
    onehot = jnp.where(sel1 | sel2, 1.0, 0.0)
    r_i = lax.broadcasted_iota(jnp.int32, (tm, tm), 0)
    c_i = lax.broadcasted_iota(jnp.int32, (tm, tm), 1)
    stril = jnp.where(c_i < r_i, 1.0, 0.0).astype(BF16)
    cum = _dot(stril, onehot.astype(BF16)) + carry_ref[0:1, :]
    rank1 = jnp.sum(jnp.where(sel1, cum, 0.0), axis=-1, keepdims=True)
    rank2 = jnp.sum(jnp.where(sel2, cum, 0.0), axis=-1, keepdims=True)
    carry_ref[...] = carry_ref[...] + jnp.sum(onehot, axis=0, keepdims=True)
    cnt_ref[...] = carry_ref[...]

    cols = (i1 - ROUTE_LANE0, i2 - ROUTE_LANE0, rank1, rank2, w1, w2)
    rt = jnp.zeros((tm, LANES), F32)
    for c, val in enumerate(cols):
        rt = jnp.where(lane_i == c, val, rt)
    rt_ref[...] = rt


def _merge_route(x, ys, yb, yc, g_mix, w_gate, b_gate, w_glu, b_glu, w_br, w_out, g_ffn, w_rt, b_rt):
    t, d = x.shape
    tm = TM_MERGE
    tok = lambda i: (i, 0)
    c2 = lambda i: (0, 0)
    c3 = lambda i: (0, 0, 0)
    full = lambda a: pl.BlockSpec(a.shape, c2 if a.ndim == 2 else c3)
    return pl.pallas_call(
        _merge_body,
        grid=(t // tm,),
        in_specs=[pl.BlockSpec((tm, d), tok),
                  pl.BlockSpec((tm, SSM_WIDTH), tok),
                  pl.BlockSpec((tm, SSM_WIDTH), tok),
                  pl.BlockSpec((tm, SSM_WIDTH), tok),
                  full(g_mix), full(w_gate), full(b_gate), full(w_glu), full(b_glu),
                  full(w_br), full(w_out), full(g_ffn), full(w_rt), full(b_rt)],
        out_specs=[pl.BlockSpec((tm, d), tok),
                   pl.BlockSpec((tm, d // 2), tok),
                   pl.BlockSpec((tm, LANES), tok),
                   pl.BlockSpec((8, LANES), c2)],
        out_shape=[jax.ShapeDtypeStruct((t, d), F32),
                   jax.ShapeDtypeStruct((t, d // 2), jnp.uint32),
                   jax.ShapeDtypeStruct((t, LANES), F32),
                   jax.ShapeDtypeStruct((8, LANES), F32)],
        scratch_shapes=[pltpu.VMEM((8, LANES), F32)],
        compiler_params=pltpu.CompilerParams(dimension_semantics=("arbitrary",),
                                             vmem_limit_bytes=VMEM_LIMIT),
        name="merge_route",
    )(x, ys, yb, yc, g_mix, w_gate, b_gate, w_glu, b_glu, w_br, w_out, g_ffn, w_rt, b_rt)


def _expert_body(be_ref, bv_ref, buf_ref, w1_ref, w3_ref, w2_ref, out_ref):
    i = pl.program_id(0)
    valid = bv_ref[i]

    @pl.when(valid > 0)
    def _():
        x = _unpack_bf16_pair(buf_ref[...])
        row = lax.broadcasted_iota(jnp.int32, x.shape, 0)
        x = jnp.where(row < valid, x, 0.0).astype(BF16)
        h1 = _dot(x, w1_ref[0].astype(BF16))
        h3 = _dot(x, w3_ref[0].astype(BF16))
        a = (h1 * _sigmoid(h1) * h3).astype(BF16)
        out_ref[...] = _pack_bf16_pair(_dot(a, w2_ref[0].astype(BF16)))

    @pl.when(valid <= 0)
    def _():
        out_ref[...] = jnp.zeros_like(out_ref)


def _experts(block_expert, block_valid, buf, w1, w3, w2):
    nslots, dh = buf.shape
    nb = nslots // BM
    d = w1.shape[1]
    grid_spec = pltpu.PrefetchScalarGridSpec(
        num_scalar_prefetch=2,
        grid=(nb,),
        in_specs=[pl.BlockSpec((BM, dh), lambda i, be, bv: (i, 0)),
                  pl.BlockSpec((1, d, D_FF), lambda i, be, bv: (be[i], 0, 0)),
                  pl.BlockSpec((1, d, D_FF), lambda i, be, bv: (be[i], 0, 0)),
                  pl.BlockSpec((1, D_FF, d), lambda i, be, bv: (be[i], 0, 0))],
        out_specs=pl.BlockSpec((BM, dh), lambda i, be, bv: (i, 0)),
    )
    return pl.pallas_call(
        _expert_body,
        grid_spec=grid_spec,
        out_shape=jax.ShapeDtypeStruct((nslots, dh), jnp.uint32),
        compiler_params=pltpu.CompilerParams(dimension_semantics=("arbitrary",),
                                             vmem_limit_bytes=VMEM_LIMIT),
        name="experts",
    )(block_expert, block_valid, buf, w1, w3, w2)


def _combine_body(h_ref, g0_ref, g1_ref, rt_ref, gfin_ref, out_ref):
    rt = rt_ref[...]
    y = (h_ref[...] + rt[:, 4:5] * _unpack_bf16_pair(g0_ref[...])
         + rt[:, 5:6] * _unpack_bf16_pair(g1_ref[...]))
    out_ref[...] = _rms(y, gfin_ref[...])


def _combine(h, g0, g1, rt, g_final):
    t, d = h.shape
    tm = TM_OUT
    tok = lambda i: (i, 0)
    return pl.pallas_call(
        _combine_body,
        grid=(t // tm,),
        in_specs=[pl.BlockSpec((tm, d), tok),
                  pl.BlockSpec((tm, d // 2), tok),
                  pl.BlockSpec((tm, d // 2), tok),
                  pl.BlockSpec((tm, LANES), tok),
                  pl.BlockSpec((1, d), lambda i: (0, 0))],
        out_specs=pl.BlockSpec((tm, d), tok),
        out_shape=jax.ShapeDtypeStruct((t, d), F32),
        compiler_params=pltpu.CompilerParams(dimension_semantics=("arbitrary",),
                                             vmem_limit_bytes=VMEM_LIMIT),
        name="combine",
    )(h, g0, g1, rt, g_final)


def _layer(h, mem, g_mix, g_mem, w_in, w_gate, b_gate, lam_re, lam_im, log_dt, b_re, b_im,
           c_re, c_im, d_skip, w_glu, b_glu, g_sgu, w_spatial, b_spatial, w_kv, w_branch,
           w_out, g_ffn, w_group, b_group, w_router, b_router, w1, w3, w2, g_out):
    bsz, s, d = h.shape
    t = bsz * s
    row = lambda a: a.reshape(1, -1)

    k, v = _kv_proj(mem, row(g_mem), w_kv.astype(BF16))

    tril = jnp.tril(jnp.ones((CHUNK, CHUNK), dtype=bool))
    w_sp = jnp.where(tril, w_spatial, 0.0).astype(BF16)
    b_sp = jnp.broadcast_to(b_spatial[:, :, None], (SGU_HEADS, CHUNK, SGU_HEAD_DIM))
    x_ssm, y_b, y_c = _in_proj(h, row(g_mix), w_in.astype(BF16), row(g_sgu), w_sp, b_sp, k, v)

    nch = t // SSM_CHUNK
    u2 = x_ssm.reshape(nch, SSM_CHUNK, SSM_GROUPS, SSM_GROUP).transpose(2, 0, 1, 3)
    u2 = u2.reshape(SSM_GROUPS, nch, SSM_CHUNK * SSM_GROUP)
    y2 = _ssm(u2, *_ssm_params(lam_re, lam_im, log_dt, b_re, b_im, c_re, c_im, d_skip), n_seq=bsz)
    y_s = y2.reshape(SSM_GROUPS, nch, SSM_CHUNK, SSM_GROUP).transpose(1, 2, 0, 3).reshape(t, SSM_WIDTH)

    pad = LANES - N_GROUPS - N_EXPERTS
    w_rt = jnp.concatenate([w_group, w_router, jnp.zeros((d, pad), F32)], axis=1)
    b_rt = jnp.concatenate([b_group, b_router, jnp.zeros((pad,), F32)]).reshape(1, LANES)
    h2, xnp, rt, cnt = _merge_route(
        h.reshape(t, d), y_s, y_b.reshape(t, -1), y_c.reshape(t, -1), row(g_mix),
        w_gate.astype(BF16), row(b_gate), w_glu.astype(BF16), row(b_glu),
        w_branch.astype(BF16), w_out.astype(BF16), row(g_ffn), w_rt, b_rt)

    counts = cnt[0, ROUTE_LANE0:ROUTE_LANE0 + N_EXPERTS].astype(jnp.int32)
    padded = (counts + BM - 1) // BM * BM
    block_end = jnp.cumsum(padded)
    pstart = block_end - padded
    nb = (t * TOP_K) // BM + N_EXPERTS
    blk_row0 = jnp.arange(nb, dtype=jnp.int32) * BM
    block_expert = jnp.minimum(jnp.searchsorted(block_end, blk_row0, side='right'),
                               N_EXPERTS - 1).astype(jnp.int32)
    block_valid = jnp.clip(counts[block_expert] - (blk_row0 - pstart[block_expert]), 0, BM)
    block_valid = jnp.where(blk_row0 < block_end[-1], block_valid, 0).astype(jnp.int32)
    e_ids = rt[:, 0:2].astype(jnp.int32)
    dest = pstart[e_ids] + rt[:, 2:4].astype(jnp.int32)

    buf = jnp.zeros((nb * BM, d // 2), jnp.uint32)
    buf = buf.at[dest[:, 0]].set(xnp).at[dest[:, 1]].set(xnp)
    yb = _experts(block_expert, block_valid, buf, w1, w3, w2)
    g0 = yb[dest[:, 0]]
    g1 = yb[dest[:, 1]]
    out = _combine(h2, g0, g1, rt, row(g_out))
    return out.reshape(bsz, s, d)


def kernel(x, mem, g_mix, g_mem, w_in, w_gate, b_gate, lam_re, lam_im, log_dt, b_re, b_im, c_re,
           c_im, d_skip, w_glu, b_glu, g_sgu, w_spatial, b_spatial, w_kv, w_branch, w_out, g_ffn,
           w_group, b_group, w_router, b_router, w1, w3, w2, g_final):
    assert g_mix.shape[0] == 1, "single-layer stack"
    return _layer(x, mem, g_mix[0], g_mem[0], w_in[0], w_gate[0], b_gate[0], lam_re[0], lam_im[0],
                  log_dt[0], b_re[0], b_im[0], c_re[0], c_im[0], d_skip[0], w_glu[0], b_glu[0],
                  g_sgu[0], w_spatial[0], b_spatial[0], w_kv[0], w_branch[0], w_out[0], g_ffn[0],
                  w_group[0], b_group[0], w_router[0], b_router[0], w1[0], w3[0], w2[0], g_final)
```

```python
import functools
import math

import jax
import jax.numpy as jnp
from jax import lax
from jax.experimental import pallas as pl
from jax.experimental.pallas import tpu as pltpu
from jax.experimental.pallas import tpu_sc as plsc

F32 = jnp.float32
BF16 = jnp.bfloat16

EPS = 1e-6
D_MODEL = 1024
SSM_WIDTH = 512
SSM_GROUP = 16
SSM_GROUPS = 32
SSM_STATE = 64
SSM_CHUNK = 16
SGU_WIDTH = 512
SGU_HEADS = 4
SGU_HEAD_DIM = 128
CHUNK = 128
XA_HEADS = 4
XA_HEAD_DIM = 128
N_GROUPS = 8
EXPERTS_PER_GROUP = 8
N_EXPERTS = 64
TOP_K = 2
D_FF = 512
LANES = 128
ROUTE_LANE0 = N_GROUPS

TM_IN = 512
TM_MERGE = 256
TM_OUT = 512
BM = 256
SC_WINDOW = 128
SC_ROW = 256
SC_SPLIT = (D_MODEL // 2) // SC_ROW
VMEM_LIMIT = 56 * 1024 * 1024


def _rms(x, g):
    return x * lax.rsqrt(jnp.mean(x * x, axis=-1, keepdims=True) + EPS) * g


def _sigmoid(x):
    return 0.5 * (1.0 + jnp.tanh(0.5 * x))


def _gelu(x):
    c = math.sqrt(2.0 / math.pi)
    return 0.5 * x * (1.0 + jnp.tanh(c * (x + 0.044715 * (x * x * x))))


def _dot(a, b):
    return jnp.dot(a, b, preferred_element_type=F32)


def _pack_bf16_pair(x):
    n = x.shape[1] // 2
    lo = lax.bitcast_convert_type(x[:, :n].astype(BF16).astype(F32), jnp.uint32)
    hi = lax.bitcast_convert_type(x[:, n:].astype(BF16).astype(F32), jnp.uint32)
    return hi | (lo >> 16)


def _unpack_bf16_pair(p):
    lo = lax.bitcast_convert_type(p << 16, F32)
    hi = lax.bitcast_convert_type(p & jnp.uint32(0xFFFF0000), F32)
    return jnp.concatenate([lo, hi], axis=1)


def _kv_body(mem_ref, g_ref, w_ref, k_ref, v_ref):
    n = _rms(mem_ref[0], g_ref[...]).astype(BF16)
    kv = _dot(n, w_ref[...])
    k_ref[0] = kv[:, :SGU_WIDTH].astype(BF16)
    v_ref[0] = kv[:, SGU_WIDTH:].astype(BF16)


def _kv_proj(mem, g_mem, w_kv):
    b, m, d = mem.shape
    return pl.pallas_call(
        _kv_body,
        grid=(b,),
        in_specs=[pl.BlockSpec((1, m, d), lambda i: (i, 0, 0)),
                  pl.BlockSpec((1, d), lambda i: (0, 0)),
                  pl.BlockSpec((d, 2 * SGU_WIDTH), lambda i: (0, 0))],
        out_specs=[pl.BlockSpec((1, m, SGU_WIDTH), lambda i: (i, 0, 0)),
                   pl.BlockSpec((1, m, SGU_WIDTH), lambda i: (i, 0, 0))],
        out_shape=[jax.ShapeDtypeStruct((b, m, SGU_WIDTH), BF16),
                   jax.ShapeDtypeStruct((b, m, SGU_WIDTH), BF16)],
        compiler_params=pltpu.CompilerParams(dimension_semantics=("arbitrary",),
                                             vmem_limit_bytes=VMEM_LIMIT),
        name="kv_proj",
    )(mem, g_mem, w_kv)


def _in_body(x_ref, gmix_ref, win_ref, gsgu_ref, wsp_ref, bsp_ref, k_ref, v_ref,
             xssm_ref, yb_ref, yc_ref):
    n = _rms(x_ref[0], gmix_ref[...]).astype(BF16)
    proj = _dot(n, win_ref[...])
    xssm_ref[0] = proj[:, :SSM_WIDTH].astype(BF16)

    uv = _gelu(proj[:, SSM_WIDTH:SSM_WIDTH + 2 * SGU_WIDTH])
    u = uv[:, :SGU_WIDTH]
    v = _rms(uv[:, SGU_WIDTH:], gsgu_ref[...]).astype(BF16)
    tm = u.shape[0]
    rows = []
    for c in range(tm // CHUNK):
        vc = v[c * CHUNK:(c + 1) * CHUNK]
        heads = []
        for h in range(SGU_HEADS):
            sl = slice(h * SGU_HEAD_DIM, (h + 1) * SGU_HEAD_DIM)
            heads.append(_dot(wsp_ref[h], vc[:, sl]) + bsp_ref[h])
        rows.append(jnp.concatenate(heads, axis=1))
    sv = jnp.concatenate(rows, axis=0)
    yb_ref[0] = (u * sv).astype(BF16)

    q = proj[:, SSM_WIDTH + 2 * SGU_WIDTH:].astype(BF16)
    kk = k_ref[0]
    vv = v_ref[0]
    outs = []
    for h in range(XA_HEADS):
        sl = slice(h * XA_HEAD_DIM, (h + 1) * XA_HEAD_DIM)
        s = lax.dot_general(q[:, sl], kk[:, sl], (((1,), (1,)), ((), ())),
                            preferred_element_type=F32) * (XA_HEAD_DIM ** -0.5)
        e = jnp.exp(s - jnp.max(s, axis=-1, keepdims=True))
        l = jnp.sum(e, axis=-1, keepdims=True)
        outs.append(_dot(e.astype(BF16), vv[:, sl]) / l)
    yc_ref[0] = jnp.concatenate(outs, axis=1).astype(BF16)


def _in_proj(x, g_mix, w_in, g_sgu, w_sp, b_sp, k, v):
    b, s, d = x.shape
    m = k.shape[1]
    const2 = lambda i, j: (0, 0)
    const3 = lambda i, j: (0, 0, 0)
    tok = lambda i, j: (i, j, 0)
    per_b = lambda i, j: (i, 0, 0)
    out = jax.ShapeDtypeStruct((b, s, SSM_WIDTH), BF16)
    return pl.pallas_call(
        _in_body,
        grid=(b, s // TM_IN),
        in_specs=[pl.BlockSpec((1, TM_IN, d), tok),
                  pl.BlockSpec((1, d), const2),
                  pl.BlockSpec(w_in.shape, const2),
                  pl.BlockSpec((1, SGU_WIDTH), const2),
                  pl.BlockSpec(w_sp.shape, const3),
                  pl.BlockSpec(b_sp.shape, const3),
                  pl.BlockSpec((1, m, SGU_WIDTH), per_b),
                  pl.BlockSpec((1, m, SGU_WIDTH), per_b)],
        out_specs=[pl.BlockSpec((1, TM_IN, SSM_WIDTH), tok)] * 3,
        out_shape=[out, out, out],
        compiler_params=pltpu.CompilerParams(dimension_semantics=("arbitrary", "arbitrary"),
                                             vmem_limit_bytes=VMEM_LIMIT),
        name="in_proj",
    )(x, g_mix, w_in, g_sgu, w_sp, b_sp, k, v)


def _ssm_params(lam_re, lam_im, log_dt, b_re, b_im, c_re, c_im, d_skip):
    hp = lax.Precision.HIGHEST
    g, p = lam_re.shape
    dt = jnp.exp(log_dt)[:, None]
    ar = lam_re * dt
    ai = lam_im * dt

    def cpow(j):
        mag = jnp.exp(ar[:, None, :] * j[None, :, None])
        ph = ai[:, None, :] * j[None, :, None]
        return mag * jnp.cos(ph), mag * jnp.sin(ph)

    pr, pi = cpow(jnp.arange(SSM_CHUNK + 1, dtype=F32))
    nr = pr[:, 1] - 1.0
    ni = pi[:, 1]
    den = lam_re * lam_re + lam_im * lam_im
    fr = (nr * lam_re + ni * lam_im) / den
    fi = (ni * lam_re - nr * lam_im) / den
    bbr = fr[..., None] * b_re - fi[..., None] * b_im
    bbi = fr[..., None] * b_im + fi[..., None] * b_re

    L = SSM_CHUNK
    cr = c_re[:, None]
    ci = c_im[:, None]
    cpr = cr * pr[:, :L, None, :] - ci * pi[:, :L, None, :]
    cpi = cr * pi[:, :L, None, :] + ci * pr[:, :L, None, :]
    kern = (jnp.einsum('glop,gpi->gloi', cpr, bbr, precision=hp)
            - jnp.einsum('glop,gpi->gloi', cpi, bbi, precision=hp))
    t_idx = jnp.arange(L)[None, :] - jnp.arange(L)[:, None]
    kt = kern[:, jnp.clip(t_idx, 0, L - 1)]
    kt = jnp.where((t_idx >= 0)[None, :, :, None, None], kt, 0.0)
    toep = kt.transpose(0, 1, 4, 2, 3).reshape(g, L * SSM_GROUP, L * SSM_GROUP)

    rr = pr[:, L - 1::-1][:, :L]
    ri = pi[:, L - 1::-1][:, :L]
    bbr_t = bbr.transpose(0, 2, 1)[:, None]
    bbi_t = bbi.transpose(0, 2, 1)[:, None]
    n_re = rr[:, :, None, :] * bbr_t - ri[:, :, None, :] * bbi_t
    n_im = rr[:, :, None, :] * bbi_t + ri[:, :, None, :] * bbr_t
    nmat = jnp.concatenate([n_re, n_im], axis=-1).reshape(g, L * SSM_GROUP, 2 * p)

    clr = cr * pr[:, 1:, None, :] - ci * pi[:, 1:, None, :]
    cli = cr * pi[:, 1:, None, :] + ci * pr[:, 1:, None, :]
    mmat = jnp.concatenate([clr, -cli], axis=-1)
    mmat = mmat.transpose(0, 3, 1, 2).reshape(g, 2 * p, L * SSM_GROUP)

    qr, qi = cpow(jnp.asarray([float(L * 2 ** k) for k in range(8)], F32))
    lr = jnp.concatenate([qr, qr], axis=-1)
    li = jnp.concatenate([-qi, qi], axis=-1)
    d2 = jnp.tile(d_skip.reshape(g, 1, SSM_GROUP), (1, 1, L))
    return toep.astype(BF16), nmat.astype(BF16), mmat.astype(BF16), lr, li, d2


def _ssm_body(u_ref, toep_ref, n_ref, m_ref, lr_ref, li_ref, d2_ref, y_ref, *, n_seq):
    u = u_ref[0]
    rows = u.shape[0]
    per = rows // n_seq
    y = _dot(u, toep_ref[0])
    st = _dot(u, n_ref[0])
    row = lax.broadcasted_iota(jnp.int32, (per, LANES), 0)
    prev = []
    for b in range(n_seq):
        x = st[b * per:(b + 1) * per]
        k = 0
        while (1 << k) < per:
            d = 1 << k
            sh = jnp.where(row >= d, pltpu.roll(x, d, 0), 0.0)
            x = x + sh * lr_ref[0, k:k + 1, :] + pltpu.roll(sh, SSM_STATE, 1) * li_ref[0, k:k + 1, :]
            k += 1
        prev.append(jnp.where(row >= 1, pltpu.roll(x, 1, 0), 0.0))
    xp = jnp.concatenate(prev, axis=0).astype(BF16)
    y = y + _dot(xp, m_ref[0]) + d2_ref[0] * u.astype(F32)
    y_ref[0] = _gelu(y).astype(BF16)


def _ssm(u2, toep, nmat, mmat, lr, li, d2, n_seq):
    g, rows, w = u2.shape
    blk = lambda a: pl.BlockSpec((1,) + a.shape[1:], lambda i: (i, 0, 0))
    return pl.pallas_call(
        functools.partial(_ssm_body, n_seq=n_seq),
        grid=(g,),
        in_specs=[blk(u2), blk(toep), blk(nmat), blk(mmat), blk(lr), blk(li), blk(d2)],
        out_specs=blk(u2),
        out_shape=jax.ShapeDtypeStruct(u2.shape, BF16),
        compiler_params=pltpu.CompilerParams(dimension_semantics=("arbitrary",),
                                             vmem_limit_bytes=VMEM_LIMIT),
        name="ssm",
    )(u2, toep, nmat, mmat, lr, li, d2)


def _merge_body(x_ref, ys_ref, yb_ref, yc_ref, gmix_ref, wgate_ref, bgate_ref, wglu_ref, bglu_ref,
                wbr_ref, wout_ref, gffn_ref, wrt_ref, brt_ref,
                h_ref, xnp_ref, rt_ref, cnt_ref, carry_ref):
    i = pl.program_id(0)

    @pl.when(i == 0)
    def _():
        carry_ref[...] = jnp.zeros_like(carry_ref)

    x = x_ref[...]
    n = _rms(x, gmix_ref[...]).astype(BF16)
    gates = _sigmoid(_dot(n, wgate_ref[...]) + bgate_ref[...])
    glu = _dot(ys_ref[...], wglu_ref[...]) + bglu_ref[...]
    ya = (glu[:, :SSM_WIDTH] * _sigmoid(glu[:, SSM_WIDTH:])).astype(BF16)
    merged = (gates[:, :D_MODEL] * _dot(ya, wbr_ref[0])
              + gates[:, D_MODEL:2 * D_MODEL] * _dot(yb_ref[...], wbr_ref[1])
              + gates[:, 2 * D_MODEL:] * _dot(yc_ref[...], wbr_ref[2]))
    h = x + _dot(merged.astype(BF16), wout_ref[...])
    h_ref[...] = h

    xn = _rms(h, gffn_ref[...])
    xnp_ref[...] = _pack_bf16_pair(xn)

    logits = jnp.dot(xn, wrt_ref[...], precision=lax.Precision.HIGHEST,
                     preferred_element_type=F32) + brt_ref[...]
    tm = logits.shape[0]
    lane_i = lax.broadcasted_iota(jnp.int32, (tm, LANES), 1)
    lane = lane_i.astype(F32)
    neg = jnp.float32(-3.0e38)
    big = jnp.float32(LANES)
    gmask = lane_i < N_GROUPS
    gl = jnp.where(gmask, logits, neg)
    gmax = jnp.max(gl, axis=-1, keepdims=True)
    gidx = jnp.min(jnp.where(gl == gmax, lane, big), axis=-1, keepdims=True)
    gsum = jnp.sum(jnp.where(gmask, jnp.exp(gl - gmax), 0.0), axis=-1, keepdims=True)
    g_w = 1.0 / gsum
    e_lane = lane_i - ROUTE_LANE0
    lane_group = (e_lane >> 3).astype(F32)
    emask = (e_lane >= 0) & (e_lane < N_EXPERTS) & (lane_group == gidx)
    el = jnp.where(emask, logits, neg)
    m1 = jnp.max(el, axis=-1, keepdims=True)
    i1 = jnp.min(jnp.where(el == m1, lane, big), axis=-1, keepdims=True)
    el2 = jnp.where(lane == i1, neg, el)
    m2 = jnp.max(el2, axis=-1, keepdims=True)
    i2 = jnp.min(jnp.where(el2 == m2, lane, big), axis=-1, keepdims=True)
    t = jnp.exp(m2 - m1)
    w1 = g_w / (1.0 + t)
    w2 = g_w * t / (1.0 + t)

    sel1 = lane == i1
    sel2 = lane == i2
    onehot = jnp.where(sel1 | sel2, 1.0, 0.0)
    r_i = lax.broadcasted_iota(jnp.int32, (tm, tm), 0)
    c_i = lax.broadcasted_iota(jnp.int32, (tm, tm), 1)
    stril = jnp.where(c_i < r_i, 1.0, 0.0).astype(BF16)
    cum = _dot(stril, onehot.astype(BF16)) + carry_ref[0:1, :]
    rank1 = jnp.sum(jnp.where(sel1, cum, 0.0), axis=-1, keepdims=True)
    rank2 = jnp.sum(jnp.where(sel2, cum, 0.0), axis=-1, keepdims=True)
    carry_ref[...] = carry_ref[...] + jnp.sum(onehot, axis=0, keepdims=True)
    cnt_ref[...] = carry_ref[...]

    cols = (i1 - ROUTE_LANE0, i2 - ROUTE_LANE0, rank1, rank2, w1, w2)
    rt = jnp.zeros((tm, LANES), F32)
    for c, val in enumerate(cols):
        rt = jnp.where(lane_i == c, val, rt)
    rt_ref[...] = rt


def _merge_route(x, ys, yb, yc, g_mix, w_gate, b_gate, w_glu, b_glu, w_br, w_out, g_ffn, w_rt, b_rt):
    t, d = x.shape
    tm = TM_MERGE
    tok = lambda i: (i, 0)
    c2 = lambda i: (0, 0)
    c3 = lambda i: (0, 0, 0)
    full = lambda a: pl.BlockSpec(a.shape, c2 if a.ndim == 2 else c3)
    return pl.pallas_call(
        _merge_body,
        grid=(t // tm,),
        in_specs=[pl.BlockSpec((tm, d), tok),
                  pl.BlockSpec((tm, SSM_WIDTH), tok),
                  pl.BlockSpec((tm, SSM_WIDTH), tok),
                  pl.BlockSpec((tm, SSM_WIDTH), tok),
                  full(g_mix), full(w_gate), full(b_gate), full(w_glu), full(b_glu),
                  full(w_br), full(w_out), full(g_ffn), full(w_rt), full(b_rt)],
        out_specs=[pl.BlockSpec((tm, d), tok),
                   pl.BlockSpec((tm, d // 2), tok),
                   pl.BlockSpec((tm, LANES), tok),
                   pl.BlockSpec((8, LANES), c2)],
        out_shape=[jax.ShapeDtypeStruct((t, d), F32),
                   jax.ShapeDtypeStruct((t, d // 2), jnp.uint32),
                   jax.ShapeDtypeStruct((t, LANES), F32),
                   jax.ShapeDtypeStruct((8, LANES), F32)],
        scratch_shapes=[pltpu.VMEM((8, LANES), F32)],
        compiler_params=pltpu.CompilerParams(dimension_semantics=("arbitrary",),
                                             vmem_limit_bytes=VMEM_LIMIT),
        name="merge_route",
    )(x, ys, yb, yc, g_mix, w_gate, b_gate, w_glu, b_glu, w_br, w_out, g_ffn, w_rt, b_rt)


def _expert_body(be_ref, bv_ref, buf_ref, w1_ref, w3_ref, w2_ref, out_ref):
    i = pl.program_id(0)
    valid = bv_ref[i]

    @pl.when(valid > 0)
    def _():
        x = _unpack_bf16_pair(buf_ref[...])
        row = lax.broadcasted_iota(jnp.int32, x.shape, 0)
        x = jnp.where(row < valid, x, 0.0).astype(BF16)
        h1 = _dot(x, w1_ref[0].astype(BF16))
        h3 = _dot(x, w3_ref[0].astype(BF16))
        a = (h1 * _sigmoid(h1) * h3).astype(BF16)
        out_ref[...] = _pack_bf16_pair(_dot(a, w2_ref[0].astype(BF16)))

    @pl.when(valid <= 0)
    def _():
        out_ref[...] = jnp.zeros_like(out_ref)


def _experts(block_expert, block_valid, buf, w1, w3, w2):
    nslots, dh = buf.shape
    nb = nslots // BM
    d = w1.shape[1]
    grid_spec = pltpu.PrefetchScalarGridSpec(
        num_scalar_prefetch=2,
        grid=(nb,),
        in_specs=[pl.BlockSpec((BM, dh), lambda i, be, bv: (i, 0)),
                  pl.BlockSpec((1, d, D_FF), lambda i, be, bv: (be[i], 0, 0)),
                  pl.BlockSpec((1, d, D_FF), lambda i, be, bv: (be[i], 0, 0)),
                  pl.BlockSpec((1, D_FF, d), lambda i, be, bv: (be[i], 0, 0))],
        out_specs=pl.BlockSpec((BM, dh), lambda i, be, bv: (i, 0)),
    )
    return pl.pallas_call(
        _expert_body,
        grid_spec=grid_spec,
        out_shape=jax.ShapeDtypeStruct((nslots, dh), jnp.uint32),
        compiler_params=pltpu.CompilerParams(dimension_semantics=("arbitrary",),
                                             vmem_limit_bytes=VMEM_LIMIT),
        name="experts",
    )(block_expert, block_valid, buf, w1, w3, w2)


def _sc_mesh():
    return plsc.VectorSubcoreMesh(core_axis_name="core", subcore_axis_name="subcore")


def _dispatch_rows(rows, dest0, dest1, nslots):
    t, w = rows.shape
    win = SC_WINDOW
    idx_spec = pl.BlockSpec((1, win), lambda i: (0, i))

    @functools.partial(pl.kernel, mesh=_sc_mesh(), scratch_types=[],
                       out_type=jax.ShapeDtypeStruct((nslots, w), rows.dtype), name="dispatch_rows")
    def run(rows_hbm, i0_hbm, i1_hbm, out_hbm):
        def body(rows_vmem, i0_vmem, i1_vmem):
            pltpu.sync_copy(rows_vmem, out_hbm.at[i0_vmem.at[0]])
            pltpu.sync_copy(rows_vmem, out_hbm.at[i1_vmem.at[0]])

        pltpu.emit_pipeline(
            body, grid=(t // win,),
            in_specs=[pl.BlockSpec((win, w), lambda i: (i, 0)), idx_spec, idx_spec],
            out_specs=[],
            core_axis_name=("core", "subcore"),
            dimension_semantics=(pltpu.PARALLEL,),
        )(rows_hbm, i0_hbm, i1_hbm)

    return run(rows, dest0.reshape(1, t), dest1.reshape(1, t))


def _gather_rows(table, idx):
    n = idx.shape[0]
    w = table.shape[1]
    win = SC_WINDOW

    @functools.partial(pl.kernel, mesh=_sc_mesh(), scratch_types=[],
                       out_type=jax.ShapeDtypeStruct((n, w), table.dtype), name="gather_rows")
    def run(table_hbm, i_hbm, out_hbm):
        def body(i_vmem, out_vmem):
            pltpu.sync_copy(table_hbm.at[i_vmem.at[0]], out_vmem)

        pltpu.emit_pipeline(
            body, grid=(n // win,),
            in_specs=[pl.BlockSpec((1, win), lambda i: (0, i))],
            out_specs=[pl.BlockSpec((win, w), lambda i: (i, 0))],
            core_axis_name=("core", "subcore"),
            dimension_semantics=(pltpu.PARALLEL,),
        )(i_hbm, out_hbm)

    return run(table, idx.reshape(1, n))


def _combine_body(h_ref, g0_ref, g1_ref, rt_ref, gfin_ref, out_ref):
    rt = rt_ref[...]
    y = (h_ref[...] + rt[:, 4:5] * _unpack_bf16_pair(g0_ref[...])
         + rt[:, 5:6] * _unpack_bf16_pair(g1_ref[...]))
    out_ref[...] = _rms(y, gfin_ref[...])


def _combine(h, g01, rt, g_final):
    t, d = h.shape
    tm = TM_OUT
    tok = lambda i: (i, 0)
    second = lambda i: (i + t // tm, 0)
    return pl.pallas_call(
        _combine_body,
        grid=(t // tm,),
        in_specs=[pl.BlockSpec((tm, d), tok),
                  pl.BlockSpec((tm, d // 2), tok),
                  pl.BlockSpec((tm, d // 2), second),
                  pl.BlockSpec((tm, LANES), tok),
                  pl.BlockSpec((1, d), lambda i: (0, 0))],
        out_specs=pl.BlockSpec((tm, d), tok),
        out_shape=jax.ShapeDtypeStruct((t, d), F32),
        compiler_params=pltpu.CompilerParams(dimension_semantics=("arbitrary",),
                                             vmem_limit_bytes=VMEM_LIMIT),
        name="combine",
    )(h, g01, g01, rt, g_final)


def _layer(h, mem, g_mix, g_mem, w_in, w_gate, b_gate, lam_re, lam_im, log_dt, b_re, b_im,
           c_re, c_im, d_skip, w_glu, b_glu, g_sgu, w_spatial, b_spatial, w_kv, w_branch,
           w_out, g_ffn, w_group, b_group, w_router, b_router, w1, w3, w2, g_out):
    bsz, s, d = h.shape
    t = bsz * s
    row = lambda a: a.reshape(1, -1)

    k, v = _kv_proj(mem, row(g_mem), w_kv.astype(BF16))

    tril = jnp.tril(jnp.ones((CHUNK, CHUNK), dtype=bool))
    w_sp = jnp.where(tril, w_spatial, 0.0).astype(BF16)
    b_sp = jnp.broadcast_to(b_spatial[:, :, None], (SGU_HEADS, CHUNK, SGU_HEAD_DIM))
    x_ssm, y_b, y_c = _in_proj(h, row(g_mix), w_in.astype(BF16), row(g_sgu), w_sp, b_sp, k, v)

    nch = t // SSM_CHUNK
    u2 = x_ssm.reshape(nch, SSM_CHUNK, SSM_GROUPS, SSM_GROUP).transpose(2, 0, 1, 3)
    u2 = u2.reshape(SSM_GROUPS, nch, SSM_CHUNK * SSM_GROUP)
    y2 = _ssm(u2, *_ssm_params(lam_re, lam_im, log_dt, b_re, b_im, c_re, c_im, d_skip), n_seq=bsz)
    y_s = y2.reshape(SSM_GROUPS, nch, SSM_CHUNK, SSM_GROUP).transpose(1, 2, 0, 3).reshape(t, SSM_WIDTH)

    pad = LANES - N_GROUPS - N_EXPERTS
    w_rt = jnp.concatenate([w_group, w_router, jnp.zeros((d, pad), F32)], axis=1)
    b_rt = jnp.concatenate([b_group, b_router, jnp.zeros((pad,), F32)]).reshape(1, LANES)
    h2, xnp, rt, cnt = _merge_route(
        h.reshape(t, d), y_s, y_b.reshape(t, -1), y_c.reshape(t, -1), row(g_mix),
        w_gate.astype(BF16), row(b_gate), w_glu.astype(BF16), row(b_glu),
        w_branch.astype(BF16), w_out.astype(BF16), row(g_ffn), w_rt, b_rt)

    counts = cnt[0, ROUTE_LANE0:ROUTE_LANE0 + N_EXPERTS].astype(jnp.int32)
    padded = (counts + BM - 1) // BM * BM
    block_end = jnp.cumsum(padded)
    pstart = block_end - padded
    nb = (t * TOP_K) // BM + N_EXPERTS
    blk_row0 = jnp.arange(nb, dtype=jnp.int32) * BM
    block_expert = jnp.sum((block_end[None, :] <= blk_row0[:, None]).astype(jnp.int32), axis=1)
    block_expert = jnp.minimum(block_expert, N_EXPERTS - 1)
    block_valid = jnp.clip(counts[block_expert] - (blk_row0 - pstart[block_expert]), 0, BM)
    block_valid = jnp.where(blk_row0 < block_end[-1], block_valid, 0).astype(jnp.int32)
    e_ids = rt[:, 0:2].astype(jnp.int32)
    dest = pstart[e_ids] + rt[:, 2:4].astype(jnp.int32)

    piece = jnp.arange(SC_SPLIT, dtype=jnp.int32)
    dest_p = (dest.T[:, :, None] * SC_SPLIT + piece).reshape(TOP_K, t * SC_SPLIT)
    buf = _dispatch_rows(xnp.reshape(t * SC_SPLIT, SC_ROW), dest_p[0], dest_p[1], nb * BM * SC_SPLIT)
    yb = _experts(block_expert, block_valid, buf.reshape(nb * BM, d // 2), w1, w3, w2)
    g01 = _gather_rows(yb.reshape(nb * BM * SC_SPLIT, SC_ROW), dest_p.reshape(-1))
    out = _combine(h2, g01.reshape(TOP_K * t, d // 2), rt, row(g_out))
    return out.reshape(bsz, s, d)


def kernel(x, mem, g_mix, g_mem, w_in, w_gate, b_gate, lam_re, lam_im, log_dt, b_re, b_im, c_re,
           c_im, d_skip, w_glu, b_glu, g_sgu, w_spatial, b_spatial, w_kv, w_branch, w_out, g_ffn,
           w_group, b_group, w_router, b_router, w1, w3, w2, g_final):
    assert g_mix.shape[0] == 1, "single-layer stack"
    return _layer(x, mem, g_mix[0], g_mem[0], w_in[0], w_gate[0], b_gate[0], lam_re[0], lam_im[0],
                  log_dt[0], b_re[0], b_im[0], c_re[0], c_im[0], d_skip[0], w_glu[0], b_glu[0],
                  g_sgu[0], w_spatial[0], b_spatial[0], w_kv[0], w_branch[0], w_out[0], g_ffn[0],
                  w_group[0], b_group[0], w_router[0], b_router[0], w1[0], w3[0], w2[0], g_final)
```

```python
import functools
import math

import jax
import jax.numpy as jnp
from jax import lax
from jax.experimental import pallas as pl
from jax.experimental.pallas import tpu as pltpu
from jax.experimental.pallas import tpu_sc as plsc

F32 = jnp.float32
BF16 = jnp.bfloat16

EPS = 1e-6
D_MODEL = 1024
SSM_WIDTH = 512
SSM_GROUP = 16
SSM_GROUPS = 32
SSM_STATE = 64
SSM_CHUNK = 16
SGU_WIDTH = 512
SGU_HEADS = 4
SGU_HEAD_DIM = 128
CHUNK = 128
XA_HEADS = 4
XA_HEAD_DIM = 128
N_GROUPS = 8
EXPERTS_PER_GROUP = 8
N_EXPERTS = 64
TOP_K = 2
D_FF = 512
LANES = 128
ROUTE_LANE0 = N_GROUPS

TM_IN = 512
TM_MERGE = 256
TM_OUT = 512
BM = 256
SC_WINDOW = 128
SC_ROW = 256
SC_SPLIT = (D_MODEL // 2) // SC_ROW
VMEM_LIMIT = 56 * 1024 * 1024


def _rms(x, g):
    return x * lax.rsqrt(jnp.mean(x * x, axis=-1, keepdims=True) + EPS) * g


def _sigmoid(x):
    return 0.5 * (1.0 + jnp.tanh(0.5 * x))


def _gelu(x):
    c = math.sqrt(2.0 / math.pi)
    return 0.5 * x * (1.0 + jnp.tanh(c * (x + 0.044715 * (x * x * x))))


def _dot(a, b):
    return jnp.dot(a, b, preferred_element_type=F32)


def _pack_bf16_pair(x):
    n = x.shape[1] // 2
    lo = lax.bitcast_convert_type(x[:, :n].astype(BF16).astype(F32), jnp.uint32)
    hi = lax.bitcast_convert_type(x[:, n:].astype(BF16).astype(F32), jnp.uint32)
    return hi | (lo >> 16)


def _unpack_bf16_pair(p):
    lo = lax.bitcast_convert_type(p << 16, F32)
    hi = lax.bitcast_convert_type(p & jnp.uint32(0xFFFF0000), F32)
    return jnp.concatenate([lo, hi], axis=1)


GROUPS_PER_TILE = LANES // SSM_GROUP
POS_PER_TILE = LANES // SSM_GROUP


def _slot_masks(rows):
    lane = lax.broadcasted_iota(jnp.int32, (rows, LANES), 1)
    return [(lane >= i * SSM_GROUP) & (lane < (i + 1) * SSM_GROUP) for i in range(LANES // SSM_GROUP)]


def _tokens_to_chunks(tok_ref, out_ref):
    tm = tok_ref.shape[1]
    nc = tm // SSM_CHUNK
    masks = _slot_masks(nc)
    for k in range(SSM_WIDTH // LANES):
        for j in range(SSM_CHUNK // POS_PER_TILE):
            src = [tok_ref[k, pl.ds(j * POS_PER_TILE + p, nc, stride=SSM_CHUNK), :]
                   for p in range(POS_PER_TILE)]
            for gi in range(GROUPS_PER_TILE):
                acc = None
                for p in range(POS_PER_TILE):
                    shift = ((p - gi) * SSM_GROUP) % LANES
                    r = pltpu.roll(src[p], shift, 1) if shift else src[p]
                    acc = r if acc is None else jnp.where(masks[p], r, acc)
                out_ref[k * GROUPS_PER_TILE + gi, :, pl.ds(j * LANES, LANES)] = acc.astype(out_ref.dtype)


def _chunks_to_tokens(chunk_ref, tok_ref):
    tm = tok_ref.shape[1]
    nc = tm // SSM_CHUNK
    masks = _slot_masks(nc)
    for k in range(SSM_WIDTH // LANES):
        for j in range(SSM_CHUNK // POS_PER_TILE):
            src = [chunk_ref[k * GROUPS_PER_TILE + gi, :, pl.ds(j * LANES, LANES)].astype(F32)
                   for gi in range(GROUPS_PER_TILE)]
            for p in range(POS_PER_TILE):
                acc = None
                for gi in range(GROUPS_PER_TILE):
                    shift = ((gi - p) * SSM_GROUP) % LANES
                    r = pltpu.roll(src[gi], shift, 1) if shift else src[gi]
                    acc = r if acc is None else jnp.where(masks[gi], r, acc)
                tok_ref[k, pl.ds(j * POS_PER_TILE + p, nc, stride=SSM_CHUNK), :] = acc


def _kv_body(mem_ref, g_ref, w_ref, k_ref, v_ref):
    n = _rms(mem_ref[0], g_ref[...]).astype(BF16)
    kv = _dot(n, w_ref[...])
    k_ref[0] = kv[:, :SGU_WIDTH].astype(BF16)
    v_ref[0] = kv[:, SGU_WIDTH:].astype(BF16)


def _kv_proj(mem, g_mem, w_kv):
    b, m, d = mem.shape
    return pl.pallas_call(
        _kv_body,
        grid=(b,),
        in_specs=[pl.BlockSpec((1, m, d), lambda i: (i, 0, 0)),
                  pl.BlockSpec((1, d), lambda i: (0, 0)),
                  pl.BlockSpec((d, 2 * SGU_WIDTH), lambda i: (0, 0))],
        out_specs=[pl.BlockSpec((1, m, SGU_WIDTH), lambda i: (i, 0, 0)),
                   pl.BlockSpec((1, m, SGU_WIDTH), lambda i: (i, 0, 0))],
        out_shape=[jax.ShapeDtypeStruct((b, m, SGU_WIDTH), BF16),
                   jax.ShapeDtypeStruct((b, m, SGU_WIDTH), BF16)],
        compiler_params=pltpu.CompilerParams(dimension_semantics=("arbitrary",),
                                             vmem_limit_bytes=VMEM_LIMIT),
        name="kv_proj",
    )(mem, g_mem, w_kv)


def _in_body(x_ref, gmix_ref, win_ref, gsgu_ref, wsp_ref, bsp_ref, k_ref, v_ref,
             u2_ref, yb_ref, yc_ref, tok_ref):
    n = _rms(x_ref[0], gmix_ref[...]).astype(BF16)
    proj = _dot(n, win_ref[...])
    for k in range(SSM_WIDTH // LANES):
        tok_ref[k] = proj[:, k * LANES:(k + 1) * LANES]
    _tokens_to_chunks(tok_ref, u2_ref)

    uv = _gelu(proj[:, SSM_WIDTH:SSM_WIDTH + 2 * SGU_WIDTH])
    u = uv[:, :SGU_WIDTH]
    v = _rms(uv[:, SGU_WIDTH:], gsgu_ref[...]).astype(BF16)
    tm = u.shape[0]
    rows = []
    for c in range(tm // CHUNK):
        vc = v[c * CHUNK:(c + 1) * CHUNK]
        heads = []
        for h in range(SGU_HEADS):
            sl = slice(h * SGU_HEAD_DIM, (h + 1) * SGU_HEAD_DIM)
            heads.append(_dot(wsp_ref[h], vc[:, sl]) + bsp_ref[h])
        rows.append(jnp.concatenate(heads, axis=1))
    sv = jnp.concatenate(rows, axis=0)
    yb_ref[0] = (u * sv).astype(BF16)

    q = proj[:, SSM_WIDTH + 2 * SGU_WIDTH:].astype(BF16)
    kk = k_ref[0]
    vv = v_ref[0]
    outs = []
    for h in range(XA_HEADS):
        sl = slice(h * XA_HEAD_DIM, (h + 1) * XA_HEAD_DIM)
        s = lax.dot_general(q[:, sl], kk[:, sl], (((1,), (1,)), ((), ())),
                            preferred_element_type=F32) * (XA_HEAD_DIM ** -0.5)
        e = jnp.exp(s - jnp.max(s, axis=-1, keepdims=True))
        l = jnp.sum(e, axis=-1, keepdims=True)
        outs.append(_dot(e.astype(BF16), vv[:, sl]) / l)
    yc_ref[0] = jnp.concatenate(outs, axis=1).astype(BF16)


def _in_proj(x, g_mix, w_in, g_sgu, w_sp, b_sp, k, v):
    b, s, d = x.shape
    m = k.shape[1]
    const2 = lambda i, j: (0, 0)
    const3 = lambda i, j: (0, 0, 0)
    tok = lambda i, j: (i, j, 0)
    per_b = lambda i, j: (i, 0, 0)
    out = jax.ShapeDtypeStruct((b, s, SSM_WIDTH), BF16)
    nc = TM_IN // SSM_CHUNK
    tiles = s // TM_IN
    u2 = jax.ShapeDtypeStruct((SSM_GROUPS, b * s // SSM_CHUNK, SSM_CHUNK * SSM_GROUP), BF16)
    return pl.pallas_call(
        _in_body,
        grid=(b, s // TM_IN),
        in_specs=[pl.BlockSpec((1, TM_IN, d), tok),
                  pl.BlockSpec((1, d), const2),
                  pl.BlockSpec(w_in.shape, const2),
                  pl.BlockSpec((1, SGU_WIDTH), const2),
                  pl.BlockSpec(w_sp.shape, const3),
                  pl.BlockSpec(b_sp.shape, const3),
                  pl.BlockSpec((1, m, SGU_WIDTH), per_b),
                  pl.BlockSpec((1, m, SGU_WIDTH), per_b)],
        out_specs=[pl.BlockSpec((SSM_GROUPS, nc, SSM_CHUNK * SSM_GROUP), lambda i, j: (0, i * tiles + j, 0)),
                   pl.BlockSpec((1, TM_IN, SSM_WIDTH), tok),
                   pl.BlockSpec((1, TM_IN, SSM_WIDTH), tok)],
        out_shape=[u2, out, out],
        scratch_shapes=[pltpu.VMEM((SSM_WIDTH // LANES, TM_IN, LANES), F32)],
        compiler_params=pltpu.CompilerParams(dimension_semantics=("arbitrary", "arbitrary"),
                                             vmem_limit_bytes=VMEM_LIMIT),
        name="in_proj",
    )(x, g_mix, w_in, g_sgu, w_sp, b_sp, k, v)


def _ssm_params(lam_re, lam_im, log_dt, b_re, b_im, c_re, c_im, d_skip):
    hp = lax.Precision.HIGHEST
    g, p = lam_re.shape
    dt = jnp.exp(log_dt)[:, None]
    ar = lam_re * dt
    ai = lam_im * dt

    def cpow(j):
        mag = jnp.exp(ar[:, None, :] * j[None, :, None])
        ph = ai[:, None, :] * j[None, :, None]
        return mag * jnp.cos(ph), mag * jnp.sin(ph)

    pr, pi = cpow(jnp.arange(SSM_CHUNK + 1, dtype=F32))
    nr = pr[:, 1] - 1.0
    ni = pi[:, 1]
    den = lam_re * lam_re + lam_im * lam_im
    fr = (nr * lam_re + ni * lam_im) / den
    fi = (ni * lam_re - nr * lam_im) / den
    bbr = fr[..., None] * b_re - fi[..., None] * b_im
    bbi = fr[..., None] * b_im + fi[..., None] * b_re

    L = SSM_CHUNK
    cr = c_re[:, None]
    ci = c_im[:, None]
    cpr = cr * pr[:, :L, None, :] - ci * pi[:, :L, None, :]
    cpi = cr * pi[:, :L, None, :] + ci * pr[:, :L, None, :]
    kern = (jnp.einsum('glop,gpi->glio', cpr, bbr, precision=hp)
            - jnp.einsum('glop,gpi->glio', cpi, bbi, precision=hp))
    lag = jnp.arange(L)[None, :] - jnp.arange(L)[:, None]
    shift = (lag[None] == jnp.arange(L)[:, None, None]).astype(F32)
    toep = jnp.sum(shift[None, :, :, None, :, None] * kern[:, :, None, :, None, :], axis=1)
    toep = toep.reshape(g, L * SSM_GROUP, L * SSM_GROUP)

    rr = pr[:, L - 1::-1][:, :L]
    ri = pi[:, L - 1::-1][:, :L]
    bbr_t = bbr.transpose(0, 2, 1)[:, None]
    bbi_t = bbi.transpose(0, 2, 1)[:, None]
    n_re = rr[:, :, None, :] * bbr_t - ri[:, :, None, :] * bbi_t
    n_im = rr[:, :, None, :] * bbi_t + ri[:, :, None, :] * bbr_t
    nmat = jnp.concatenate([n_re, n_im], axis=-1).reshape(g, L * SSM_GROUP, 2 * p)

    clr = cr * pr[:, 1:, None, :] - ci * pi[:, 1:, None, :]
    cli = cr * pi[:, 1:, None, :] + ci * pr[:, 1:, None, :]
    mmat = jnp.concatenate([clr, -cli], axis=-1)
    mmat = mmat.transpose(0, 3, 1, 2).reshape(g, 2 * p, L * SSM_GROUP)

    qr, qi = cpow(jnp.asarray([float(L * 2 ** k) for k in range(8)], F32))
    lr = jnp.concatenate([qr, qr], axis=-1)
    li = jnp.concatenate([-qi, qi], axis=-1)
    d2 = jnp.tile(d_skip.reshape(g, 1, SSM_GROUP), (1, 1, L))
    return toep.astype(BF16), nmat.astype(BF16), mmat.astype(BF16), lr, li, d2


def _ssm_body(u_ref, toep_ref, n_ref, m_ref, lr_ref, li_ref, d2_ref, y_ref, *, n_seq):
    u = u_ref[0]
    rows = u.shape[0]
    per = rows // n_seq
    y = _dot(u, toep_ref[0])
    st = _dot(u, n_ref[0])
    row = lax.broadcasted_iota(jnp.int32, (per, LANES), 0)
    prev = []
    for b in range(n_seq):
        x = st[b * per:(b + 1) * per]
        k = 0
        while (1 << k) < per:
            d = 1 << k
            sh = jnp.where(row >= d, pltpu.roll(x, d, 0), 0.0)
            x = x + sh * lr_ref[0, k:k + 1, :] + pltpu.roll(sh, SSM_STATE, 1) * li_ref[0, k:k + 1, :]
            k += 1
        prev.append(jnp.where(row >= 1, pltpu.roll(x, 1, 0), 0.0))
    xp = jnp.concatenate(prev, axis=0).astype(BF16)
    y = y + _dot(xp, m_ref[0]) + d2_ref[0] * u.astype(F32)
    y_ref[0] = _gelu(y).astype(BF16)


def _ssm(u2, toep, nmat, mmat, lr, li, d2, n_seq):
    g, rows, w = u2.shape
    blk = lambda a: pl.BlockSpec((1,) + a.shape[1:], lambda i: (i, 0, 0))
    return pl.pallas_call(
        functools.partial(_ssm_body, n_seq=n_seq),
        grid=(g,),
        in_specs=[blk(u2), blk(toep), blk(nmat), blk(mmat), blk(lr), blk(li), blk(d2)],
        out_specs=blk(u2),
        out_shape=jax.ShapeDtypeStruct(u2.shape, BF16),
        compiler_params=pltpu.CompilerParams(dimension_semantics=("arbitrary",),
                                             vmem_limit_bytes=VMEM_LIMIT),
        name="ssm",
    )(u2, toep, nmat, mmat, lr, li, d2)


def _merge_body(x_ref, y2_ref, yb_ref, yc_ref, gmix_ref, wgate_ref, bgate_ref, wglu_ref, bglu_ref,
                wbr_ref, wout_ref, gffn_ref, wrt_ref, brt_ref,
                h_ref, xnp_ref, rt_ref, rtt_ref, cnt_ref, carry_ref, tok_ref):
    i = pl.program_id(0)

    @pl.when(i == 0)
    def _():
        carry_ref[...] = jnp.zeros_like(carry_ref)

    _chunks_to_tokens(y2_ref, tok_ref)
    x = x_ref[...]
    n = _rms(x, gmix_ref[...]).astype(BF16)
    gates = _sigmoid(_dot(n, wgate_ref[...]) + bgate_ref[...])
    ys = jnp.concatenate([tok_ref[k] for k in range(SSM_WIDTH // LANES)], axis=1).astype(BF16)
    glu = _dot(ys, wglu_ref[...]) + bglu_ref[...]
    ya = (glu[:, :SSM_WIDTH] * _sigmoid(glu[:, SSM_WIDTH:])).astype(BF16)
    merged = (gates[:, :D_MODEL] * _dot(ya, wbr_ref[0])
              + gates[:, D_MODEL:2 * D_MODEL] * _dot(yb_ref[...], wbr_ref[1])
              + gates[:, 2 * D_MODEL:] * _dot(yc_ref[...], wbr_ref[2]))
    h = x + _dot(merged.astype(BF16), wout_ref[...])
    h_ref[...] = h

    xn = _rms(h, gffn_ref[...])
    packed = _pack_bf16_pair(xn)
    for j in range(SC_SPLIT):
        xnp_ref[j] = packed[:, j * SC_ROW:(j + 1) * SC_ROW]

    logits = jnp.dot(xn, wrt_ref[...], precision=lax.Precision.HIGHEST,
                     preferred_element_type=F32) + brt_ref[...]
    tm = logits.shape[0]
    lane_i = lax.broadcasted_iota(jnp.int32, (tm, LANES), 1)
    lane = lane_i.astype(F32)
    neg = jnp.float32(-3.0e38)
    big = jnp.float32(LANES)
    gmask = lane_i < N_GROUPS
    gl = jnp.where(gmask, logits, neg)
    gmax = jnp.max(gl, axis=-1, keepdims=True)
    gidx = jnp.min(jnp.where(gl == gmax, lane, big), axis=-1, keepdims=True)
    gsum = jnp.sum(jnp.where(gmask, jnp.exp(gl - gmax), 0.0), axis=-1, keepdims=True)
    g_w = 1.0 / gsum
    e_lane = lane_i - ROUTE_LANE0
    lane_group = (e_lane >> 3).astype(F32)
    emask = (e_lane >= 0) & (e_lane < N_EXPERTS) & (lane_group == gidx)
    el = jnp.where(emask, logits, neg)
    m1 = jnp.max(el, axis=-1, keepdims=True)
    i1 = jnp.min(jnp.where(el == m1, lane, big), axis=-1, keepdims=True)
    el2 = jnp.where(lane == i1, neg, el)
    m2 = jnp.max(el2, axis=-1, keepdims=True)
    i2 = jnp.min(jnp.where(el2 == m2, lane, big), axis=-1, keepdims=True)
    t = jnp.exp(m2 - m1)
    w1 = g_w / (1.0 + t)
    w2 = g_w * t / (1.0 + t)

    sel1 = lane == i1
    sel2 = lane == i2
    onehot = jnp.where(sel1 | sel2, 1.0, 0.0)
    r_i = lax.broadcasted_iota(jnp.int32, (tm, tm), 0)
    c_i = lax.broadcasted_iota(jnp.int32, (tm, tm), 1)
    stril = jnp.where(c_i < r_i, 1.0, 0.0).astype(BF16)
    cum = _dot(stril, onehot.astype(BF16)) + carry_ref[0:1, :]
    rank1 = jnp.sum(jnp.where(sel1, cum, 0.0), axis=-1, keepdims=True)
    rank2 = jnp.sum(jnp.where(sel2, cum, 0.0), axis=-1, keepdims=True)
    carry_ref[...] = carry_ref[...] + jnp.sum(onehot, axis=0, keepdims=True)
    cnt_ref[...] = carry_ref[...]

    cols = (i1 - ROUTE_LANE0, i2 - ROUTE_LANE0, rank1, rank2, w1, w2)
    rt = jnp.zeros((tm, LANES), F32)
    for c, val in enumerate(cols):
        rt = jnp.where(lane_i == c, val, rt)
    rt_ref[...] = rt
    rtt_ref[...] = rt.T[:8]


def _merge_route(x, y2, yb, yc, g_mix, w_gate, b_gate, w_glu, b_glu, w_br, w_out, g_ffn, w_rt, b_rt):
    t, d = x.shape
    tm = TM_MERGE
    tok = lambda i: (i, 0)
    c2 = lambda i: (0, 0)
    c3 = lambda i: (0, 0, 0)
    full = lambda a: pl.BlockSpec(a.shape, c2 if a.ndim == 2 else c3)
    return pl.pallas_call(
        _merge_body,
        grid=(t // tm,),
        in_specs=[pl.BlockSpec((tm, d), tok),
                  pl.BlockSpec((SSM_GROUPS, tm // SSM_CHUNK, SSM_CHUNK * SSM_GROUP), lambda i: (0, i, 0)),
                  pl.BlockSpec((tm, SSM_WIDTH), tok),
                  pl.BlockSpec((tm, SSM_WIDTH), tok),
                  full(g_mix), full(w_gate), full(b_gate), full(w_glu), full(b_glu),
                  full(w_br), full(w_out), full(g_ffn), full(w_rt), full(b_rt)],
        out_specs=[pl.BlockSpec((tm, d), tok),
                   pl.BlockSpec((SC_SPLIT, tm, SC_ROW), lambda i: (0, i, 0)),
                   pl.BlockSpec((tm, LANES), tok),
                   pl.BlockSpec((8, tm), lambda i: (0, i)),
                   pl.BlockSpec((8, LANES), c2)],
        out_shape=[jax.ShapeDtypeStruct((t, d), F32),
                   jax.ShapeDtypeStruct((SC_SPLIT, t, SC_ROW), jnp.uint32),
                   jax.ShapeDtypeStruct((t, LANES), F32),
                   jax.ShapeDtypeStruct((8, t), F32),
                   jax.ShapeDtypeStruct((8, LANES), F32)],
        scratch_shapes=[pltpu.VMEM((8, LANES), F32),
                        pltpu.VMEM((SSM_WIDTH // LANES, tm, LANES), F32)],
        compiler_params=pltpu.CompilerParams(dimension_semantics=("arbitrary",),
                                             vmem_limit_bytes=VMEM_LIMIT),
        name="merge_route",
    )(x, y2, yb, yc, g_mix, w_gate, b_gate, w_glu, b_glu, w_br, w_out, g_ffn, w_rt, b_rt)


def _expert_body(be_ref, bv_ref, buf_ref, w1_ref, w3_ref, w2_ref, out_ref):
    i = pl.program_id(0)
    valid = bv_ref[i]

    @pl.when(valid > 0)
    def _():
        x = _unpack_bf16_pair(jnp.concatenate([buf_ref[j] for j in range(SC_SPLIT)], axis=1))
        row = lax.broadcasted_iota(jnp.int32, x.shape, 0)
        x = jnp.where(row < valid, x, 0.0).astype(BF16)
        h1 = _dot(x, w1_ref[0].astype(BF16))
        h3 = _dot(x, w3_ref[0].astype(BF16))
        a = (h1 * _sigmoid(h1) * h3).astype(BF16)
        packed = _pack_bf16_pair(_dot(a, w2_ref[0].astype(BF16)))
        for j in range(SC_SPLIT):
            out_ref[j] = packed[:, j * SC_ROW:(j + 1) * SC_ROW]

    @pl.when(valid <= 0)
    def _():
        out_ref[...] = jnp.zeros_like(out_ref)


def _experts(block_expert, block_valid, buf, w1, w3, w2):
    _, nslots, _ = buf.shape
    nb = nslots // BM
    d = w1.shape[1]
    rows = pl.BlockSpec((SC_SPLIT, BM, SC_ROW), lambda i, be, bv: (0, i, 0))
    grid_spec = pltpu.PrefetchScalarGridSpec(
        num_scalar_prefetch=2,
        grid=(nb,),
        in_specs=[rows,
                  pl.BlockSpec((1, d, D_FF), lambda i, be, bv: (be[i], 0, 0)),
                  pl.BlockSpec((1, d, D_FF), lambda i, be, bv: (be[i], 0, 0)),
                  pl.BlockSpec((1, D_FF, d), lambda i, be, bv: (be[i], 0, 0))],
        out_specs=rows,
    )
    return pl.pallas_call(
        _expert_body,
        grid_spec=grid_spec,
        out_shape=jax.ShapeDtypeStruct(buf.shape, jnp.uint32),
        compiler_params=pltpu.CompilerParams(dimension_semantics=("arbitrary",),
                                             vmem_limit_bytes=VMEM_LIMIT),
        name="experts",
    )(block_expert, block_valid, buf, w1, w3, w2)


def _sc_mesh():
    return plsc.VectorSubcoreMesh(core_axis_name="core", subcore_axis_name="subcore")


def _dispatch_rows(rows, dest0, dest1, nslots):
    t, w = rows.shape
    win = SC_WINDOW
    idx_spec = pl.BlockSpec((1, win), lambda i: (0, i))

    @functools.partial(pl.kernel, mesh=_sc_mesh(), scratch_types=[],
                       out_type=jax.ShapeDtypeStruct((nslots, w), rows.dtype), name="dispatch_rows")
    def run(rows_hbm, i0_hbm, i1_hbm, out_hbm):
        def body(rows_vmem, i0_vmem, i1_vmem):
            pltpu.sync_copy(rows_vmem, out_hbm.at[i0_vmem.at[0]])
            pltpu.sync_copy(rows_vmem, out_hbm.at[i1_vmem.at[0]])

        pltpu.emit_pipeline(
            body, grid=(t // win,),
            in_specs=[pl.BlockSpec((win, w), lambda i: (i, 0)), idx_spec, idx_spec],
            out_specs=[],
            core_axis_name=("core", "subcore"),
            dimension_semantics=(pltpu.PARALLEL,),
        )(rows_hbm, i0_hbm, i1_hbm)

    return run(rows, dest0.reshape(1, t), dest1.reshape(1, t))


def _gather_rows(table, idx):
    n = idx.shape[0]
    w = table.shape[1]
    win = SC_WINDOW

    @functools.partial(pl.kernel, mesh=_sc_mesh(), scratch_types=[],
                       out_type=jax.ShapeDtypeStruct((n, w), table.dtype), name="gather_rows")
    def run(table_hbm, i_hbm, out_hbm):
        def body(i_vmem, out_vmem):
            pltpu.sync_copy(table_hbm.at[i_vmem.at[0]], out_vmem)

        pltpu.emit_pipeline(
            body, grid=(n // win,),
            in_specs=[pl.BlockSpec((1, win), lambda i: (0, i))],
            out_specs=[pl.BlockSpec((win, w), lambda i: (i, 0))],
            core_axis_name=("core", "subcore"),
            dimension_semantics=(pltpu.PARALLEL,),
        )(i_hbm, out_hbm)

    return run(table, idx.reshape(1, n))


def _combine_body(h_ref, g_ref, rt_ref, gfin_ref, out_ref):
    rt = rt_ref[...]
    y = h_ref[...]
    for k in range(TOP_K):
        rows = jnp.concatenate([g_ref[k * SC_SPLIT + j] for j in range(SC_SPLIT)], axis=1)
        y = y + rt[:, 4 + k:5 + k] * _unpack_bf16_pair(rows)
    out_ref[...] = _rms(y, gfin_ref[...])


def _combine(h, g, rt, g_final):
    t, d = h.shape
    tm = TM_OUT
    tok = lambda i: (i, 0)
    return pl.pallas_call(
        _combine_body,
        grid=(t // tm,),
        in_specs=[pl.BlockSpec((tm, d), tok),
                  pl.BlockSpec((TOP_K * SC_SPLIT, tm, SC_ROW), lambda i: (0, i, 0)),
                  pl.BlockSpec((tm, LANES), tok),
                  pl.BlockSpec((1, d), lambda i: (0, 0))],
        out_specs=pl.BlockSpec((tm, d), tok),
        out_shape=jax.ShapeDtypeStruct((t, d), F32),
        compiler_params=pltpu.CompilerParams(dimension_semantics=("arbitrary",),
                                             vmem_limit_bytes=VMEM_LIMIT),
        name="combine",
    )(h, g, rt, g_final)


def _layer(h, mem, g_mix, g_mem, w_in, w_gate, b_gate, lam_re, lam_im, log_dt, b_re, b_im,
           c_re, c_im, d_skip, w_glu, b_glu, g_sgu, w_spatial, b_spatial, w_kv, w_branch,
           w_out, g_ffn, w_group, b_group, w_router, b_router, w1, w3, w2, g_out):
    bsz, s, d = h.shape
    t = bsz * s
    row = lambda a: a.reshape(1, -1)

    k, v = _kv_proj(mem, row(g_mem), w_kv.astype(BF16))

    tril = jnp.tril(jnp.ones((CHUNK, CHUNK), dtype=bool))
    w_sp = jnp.where(tril, w_spatial, 0.0).astype(BF16)
    b_sp = jnp.broadcast_to(b_spatial[:, :, None], (SGU_HEADS, CHUNK, SGU_HEAD_DIM))
    u2, y_b, y_c = _in_proj(h, row(g_mix), w_in.astype(BF16), row(g_sgu), w_sp, b_sp, k, v)
    y2 = _ssm(u2, *_ssm_params(lam_re, lam_im, log_dt, b_re, b_im, c_re, c_im, d_skip), n_seq=bsz)

    pad = LANES - N_GROUPS - N_EXPERTS
    w_rt = jnp.concatenate([w_group, w_router, jnp.zeros((d, pad), F32)], axis=1)
    b_rt = jnp.concatenate([b_group, b_router, jnp.zeros((pad,), F32)]).reshape(1, LANES)
    h2, xnp, rt, rtt, cnt = _merge_route(
        h.reshape(t, d), y2, y_b.reshape(t, -1), y_c.reshape(t, -1), row(g_mix),
        w_gate.astype(BF16), row(b_gate), w_glu.astype(BF16), row(b_glu),
        w_branch.astype(BF16), w_out.astype(BF16), row(g_ffn), w_rt, b_rt)

    counts = cnt[0, ROUTE_LANE0:ROUTE_LANE0 + N_EXPERTS].astype(jnp.int32)
    padded = (counts + BM - 1) // BM * BM
    block_end = jnp.cumsum(padded)
    pstart = block_end - padded
    nb = (t * TOP_K) // BM + N_EXPERTS
    blk_row0 = jnp.arange(nb, dtype=jnp.int32) * BM
    block_expert = jnp.sum((block_end[None, :] <= blk_row0[:, None]).astype(jnp.int32), axis=1)
    block_expert = jnp.minimum(block_expert, N_EXPERTS - 1)
    block_valid = jnp.clip(counts[block_expert] - (blk_row0 - pstart[block_expert]), 0, BM)
    block_valid = jnp.where(blk_row0 < block_end[-1], block_valid, 0).astype(jnp.int32)
    nslots = nb * BM
    dest = pstart[rtt[0:TOP_K].astype(jnp.int32)] + rtt[TOP_K:2 * TOP_K].astype(jnp.int32)
    plane = (jnp.arange(SC_SPLIT, dtype=jnp.int32) * nslots)[None, :, None]
    dest_p = (dest[:, None, :] + plane).reshape(TOP_K, SC_SPLIT * t)
    buf = _dispatch_rows(xnp.reshape(SC_SPLIT * t, SC_ROW), dest_p[0], dest_p[1], SC_SPLIT * nslots)
    yb = _experts(block_expert, block_valid, buf.reshape(SC_SPLIT, nslots, SC_ROW), w1, w3, w2)
    g = _gather_rows(yb.reshape(SC_SPLIT * nslots, SC_ROW), dest_p.reshape(-1))
    out = _combine(h2, g.reshape(TOP_K * SC_SPLIT, t, SC_ROW), rt, row(g_out))
    return out.reshape(bsz, s, d)


def kernel(x, mem, g_mix, g_mem, w_in, w_gate, b_gate, lam_re, lam_im, log_dt, b_re, b_im, c_re,
           c_im, d_skip, w_glu, b_glu, g_sgu, w_spatial, b_spatial, w_kv, w_branch, w_out, g_ffn,
           w_group, b_group, w_router, b_router, w1, w3, w2, g_final):
    assert g_mix.shape[0] == 1, "single-layer stack"
    return _layer(x, mem, g_mix[0], g_mem[0], w_in[0], w_gate[0], b_gate[0], lam_re[0], lam_im[0],
                  log_dt[0], b_re[0], b_im[0], c_re[0], c_im[0], d_skip[0], w_glu[0], b_glu[0],
                  g_sgu[0], w_spatial[0], b_spatial[0], w_kv[0], w_branch[0], w_out[0], g_ffn[0],
                  w_group[0], b_group[0], w_router[0], b_router[0], w1[0], w3[0], w2[0], g_final)
```

```python
import functools
import math

import jax
import jax.numpy as jnp
from jax import lax
from jax.experimental import pallas as pl
from jax.experimental.pallas import tpu as pltpu
from jax.experimental.pallas import tpu_sc as plsc

F32 = jnp.float32
BF16 = jnp.bfloat16

EPS = 1e-6
D_MODEL = 1024
SSM_WIDTH = 512
SSM_GROUP = 16
SSM_GROUPS = 32
SSM_STATE = 64
SSM_CHUNK = 16
SGU_WIDTH = 512
SGU_HEADS = 4
SGU_HEAD_DIM = 128
CHUNK = 128
XA_HEADS = 4
XA_HEAD_DIM = 128
N_GROUPS = 8
EXPERTS_PER_GROUP = 8
N_EXPERTS = 64
TOP_K = 2
D_FF = 512
LANES = 128
ROUTE_LANE0 = N_GROUPS

TM_IN = 512
TM_MERGE = 256
TM_OUT = 512
BM = 256
SC_WINDOW = 128
SC_ROW = 256
SC_SPLIT = (D_MODEL // 2) // SC_ROW
SLOT_LANES = 2048
VMEM_LIMIT = 56 * 1024 * 1024


def _rms(x, g):
    return x * lax.rsqrt(jnp.mean(x * x, axis=-1, keepdims=True) + EPS) * g


def _sigmoid(x):
    return 0.5 * (1.0 + jnp.tanh(0.5 * x))


def _gelu(x):
    c = math.sqrt(2.0 / math.pi)
    return 0.5 * x * (1.0 + jnp.tanh(c * (x + 0.044715 * (x * x * x))))


def _dot(a, b):
    return jnp.dot(a, b, preferred_element_type=F32)


def _pack_bf16_pair(x):
    n = x.shape[1] // 2
    lo = lax.bitcast_convert_type(x[:, :n].astype(BF16).astype(F32), jnp.uint32)
    hi = lax.bitcast_convert_type(x[:, n:].astype(BF16).astype(F32), jnp.uint32)
    return hi | (lo >> 16)


def _unpack_bf16_pair(p):
    lo = lax.bitcast_convert_type(p << 16, F32)
    hi = lax.bitcast_convert_type(p & jnp.uint32(0xFFFF0000), F32)
    return jnp.concatenate([lo, hi], axis=1)


GROUPS_PER_TILE = LANES // SSM_GROUP
POS_PER_TILE = LANES // SSM_GROUP


def _slot_masks(rows):
    lane = lax.broadcasted_iota(jnp.int32, (rows, LANES), 1)
    return [(lane >= i * SSM_GROUP) & (lane < (i + 1) * SSM_GROUP) for i in range(LANES // SSM_GROUP)]


def _tokens_to_chunks(tok_ref, out_ref):
    tm = tok_ref.shape[1]
    nc = tm // SSM_CHUNK
    masks = _slot_masks(nc)
    for k in range(SSM_WIDTH // LANES):
        for j in range(SSM_CHUNK // POS_PER_TILE):
            src = [tok_ref[k, pl.ds(j * POS_PER_TILE + p, nc, stride=SSM_CHUNK), :]
                   for p in range(POS_PER_TILE)]
            for gi in range(GROUPS_PER_TILE):
                acc = None
                for p in range(POS_PER_TILE):
                    shift = ((p - gi) * SSM_GROUP) % LANES
                    r = pltpu.roll(src[p], shift, 1) if shift else src[p]
                    acc = r if acc is None else jnp.where(masks[p], r, acc)
                out_ref[k * GROUPS_PER_TILE + gi, :, pl.ds(j * LANES, LANES)] = acc.astype(out_ref.dtype)


def _chunks_to_tokens(chunk_ref, tok_ref):
    tm = tok_ref.shape[1]
    nc = tm // SSM_CHUNK
    masks = _slot_masks(nc)
    for k in range(SSM_WIDTH // LANES):
        for j in range(SSM_CHUNK // POS_PER_TILE):
            src = [chunk_ref[k * GROUPS_PER_TILE + gi, :, pl.ds(j * LANES, LANES)].astype(F32)
                   for gi in range(GROUPS_PER_TILE)]
            for p in range(POS_PER_TILE):
                acc = None
                for gi in range(GROUPS_PER_TILE):
                    shift = ((gi - p) * SSM_GROUP) % LANES
                    r = pltpu.roll(src[gi], shift, 1) if shift else src[gi]
                    acc = r if acc is None else jnp.where(masks[gi], r, acc)
                tok_ref[k, pl.ds(j * POS_PER_TILE + p, nc, stride=SSM_CHUNK), :] = acc


def _kv_body(mem_ref, g_ref, w_ref, k_ref, v_ref):
    n = _rms(mem_ref[0], g_ref[...]).astype(BF16)
    kv = _dot(n, w_ref[...])
    k_ref[0] = kv[:, :SGU_WIDTH].astype(BF16)
    v_ref[0] = kv[:, SGU_WIDTH:].astype(BF16)


def _kv_proj(mem, g_mem, w_kv):
    b, m, d = mem.shape
    return pl.pallas_call(
        _kv_body,
        grid=(b,),
        in_specs=[pl.BlockSpec((1, m, d), lambda i: (i, 0, 0)),
                  pl.BlockSpec((1, d), lambda i: (0, 0)),
                  pl.BlockSpec((d, 2 * SGU_WIDTH), lambda i: (0, 0))],
        out_specs=[pl.BlockSpec((1, m, SGU_WIDTH), lambda i: (i, 0, 0)),
                   pl.BlockSpec((1, m, SGU_WIDTH), lambda i: (i, 0, 0))],
        out_shape=[jax.ShapeDtypeStruct((b, m, SGU_WIDTH), BF16),
                   jax.ShapeDtypeStruct((b, m, SGU_WIDTH), BF16)],
        compiler_params=pltpu.CompilerParams(dimension_semantics=("arbitrary",),
                                             vmem_limit_bytes=VMEM_LIMIT),
        name="kv_proj",
    )(mem, g_mem, w_kv)


def _in_body(x_ref, gmix_ref, win_ref, gsgu_ref, wsp_ref, bsp_ref, k_ref, v_ref,
             u2_ref, yb_ref, yc_ref, tok_ref):
    n = _rms(x_ref[0], gmix_ref[...]).astype(BF16)
    proj = _dot(n, win_ref[...])
    for k in range(SSM_WIDTH // LANES):
        tok_ref[k] = proj[:, k * LANES:(k + 1) * LANES]
    _tokens_to_chunks(tok_ref, u2_ref)

    uv = _gelu(proj[:, SSM_WIDTH:SSM_WIDTH + 2 * SGU_WIDTH])
    u = uv[:, :SGU_WIDTH]
    v = _rms(uv[:, SGU_WIDTH:], gsgu_ref[...]).astype(BF16)
    tm = u.shape[0]
    rows = []
    for c in range(tm // CHUNK):
        vc = v[c * CHUNK:(c + 1) * CHUNK]
        heads = []
        for h in range(SGU_HEADS):
            sl = slice(h * SGU_HEAD_DIM, (h + 1) * SGU_HEAD_DIM)
            heads.append(_dot(wsp_ref[h], vc[:, sl]) + bsp_ref[h])
        rows.append(jnp.concatenate(heads, axis=1))
    sv = jnp.concatenate(rows, axis=0)
    yb_ref[0] = (u * sv).astype(BF16)

    q = proj[:, SSM_WIDTH + 2 * SGU_WIDTH:].astype(BF16)
    kk = k_ref[0]
    vv = v_ref[0]
    outs = []
    for h in range(XA_HEADS):
        sl = slice(h * XA_HEAD_DIM, (h + 1) * XA_HEAD_DIM)
        s = lax.dot_general(q[:, sl], kk[:, sl], (((1,), (1,)), ((), ())),
                            preferred_element_type=F32) * (XA_HEAD_DIM ** -0.5)
        e = jnp.exp(s - jnp.max(s, axis=-1, keepdims=True))
        l = jnp.sum(e, axis=-1, keepdims=True)
        outs.append(_dot(e.astype(BF16), vv[:, sl]) / l)
    yc_ref[0] = jnp.concatenate(outs, axis=1).astype(BF16)


def _in_proj(x, g_mix, w_in, g_sgu, w_sp, b_sp, k, v):
    b, s, d = x.shape
    m = k.shape[1]
    const2 = lambda i, j: (0, 0)
    const3 = lambda i, j: (0, 0, 0)
    tok = lambda i, j: (i, j, 0)
    per_b = lambda i, j: (i, 0, 0)
    out = jax.ShapeDtypeStruct((b, s, SSM_WIDTH), BF16)
    nc = TM_IN // SSM_CHUNK
    tiles = s // TM_IN
    u2 = jax.ShapeDtypeStruct((SSM_GROUPS, b * s // SSM_CHUNK, SSM_CHUNK * SSM_GROUP), BF16)
    return pl.pallas_call(
        _in_body,
        grid=(b, s // TM_IN),
        in_specs=[pl.BlockSpec((1, TM_IN, d), tok),
                  pl.BlockSpec((1, d), const2),
                  pl.BlockSpec(w_in.shape, const2),
                  pl.BlockSpec((1, SGU_WIDTH), const2),
                  pl.BlockSpec(w_sp.shape, const3),
                  pl.BlockSpec(b_sp.shape, const3),
                  pl.BlockSpec((1, m, SGU_WIDTH), per_b),
                  pl.BlockSpec((1, m, SGU_WIDTH), per_b)],
        out_specs=[pl.BlockSpec((SSM_GROUPS, nc, SSM_CHUNK * SSM_GROUP), lambda i, j: (0, i * tiles + j, 0)),
                   pl.BlockSpec((1, TM_IN, SSM_WIDTH), tok),
                   pl.BlockSpec((1, TM_IN, SSM_WIDTH), tok)],
        out_shape=[u2, out, out],
        scratch_shapes=[pltpu.VMEM((SSM_WIDTH // LANES, TM_IN, LANES), F32)],
        compiler_params=pltpu.CompilerParams(dimension_semantics=("arbitrary", "arbitrary"),
                                             vmem_limit_bytes=VMEM_LIMIT),
        name="in_proj",
    )(x, g_mix, w_in, g_sgu, w_sp, b_sp, k, v)


def _ssm_params(lam_re, lam_im, log_dt, b_re, b_im, c_re, c_im, d_skip):
    g, p = lam_re.shape
    dt = jnp.exp(log_dt)[:, None]
    ar = lam_re * dt
    ai = lam_im * dt

    def cpow(j):
        mag = jnp.exp(ar[:, None, :] * j[None, :, None])
        ph = ai[:, None, :] * j[None, :, None]
        return mag * jnp.cos(ph), mag * jnp.sin(ph)

    pr, pi = cpow(jnp.arange(SSM_CHUNK + 1, dtype=F32))
    nr = pr[:, 1] - 1.0
    ni = pi[:, 1]
    den = lam_re * lam_re + lam_im * lam_im
    fr = (nr * lam_re + ni * lam_im) / den
    fi = (ni * lam_re - nr * lam_im) / den
    bbr = fr[..., None] * b_re - fi[..., None] * b_im
    bbi = fr[..., None] * b_im + fi[..., None] * b_re

    L = SSM_CHUNK
    cr = c_re[:, None]
    ci = c_im[:, None]
    cpr = cr * pr[:, :L, None, :] - ci * pi[:, :L, None, :]
    cpi = cr * pi[:, :L, None, :] + ci * pr[:, :L, None, :]
    ccat = jnp.concatenate([cpr, -cpi], axis=-1)
    ccat = ccat.transpose(0, 3, 1, 2).reshape(g, 2 * p, L * SSM_GROUP)
    bcat = jnp.concatenate([bbr, bbi], axis=1).transpose(0, 2, 1)

    rr = pr[:, L - 1::-1][:, :L]
    ri = pi[:, L - 1::-1][:, :L]
    bbr_t = bbr.transpose(0, 2, 1)[:, None]
    bbi_t = bbi.transpose(0, 2, 1)[:, None]
    n_re = rr[:, :, None, :] * bbr_t - ri[:, :, None, :] * bbi_t
    n_im = rr[:, :, None, :] * bbi_t + ri[:, :, None, :] * bbr_t
    nmat = jnp.concatenate([n_re, n_im], axis=-1).reshape(g, L * SSM_GROUP, 2 * p)

    clr = cr * pr[:, 1:, None, :] - ci * pi[:, 1:, None, :]
    cli = cr * pi[:, 1:, None, :] + ci * pr[:, 1:, None, :]
    mmat = jnp.concatenate([clr, -cli], axis=-1)
    mmat = mmat.transpose(0, 3, 1, 2).reshape(g, 2 * p, L * SSM_GROUP)

    qr, qi = cpow(jnp.asarray([float(L * 2 ** k) for k in range(8)], F32))
    lr = jnp.concatenate([qr, qr], axis=-1)
    li = jnp.concatenate([-qi, qi], axis=-1)
    d2 = jnp.tile(d_skip.reshape(g, 1, SSM_GROUP), (1, 1, L))
    return bcat, ccat, nmat.astype(BF16), mmat.astype(BF16), lr, li, d2


def _ssm_body(u_ref, bcat_ref, ccat_ref, n_ref, m_ref, lr_ref, li_ref, d2_ref, y_ref, toep_ref, *, n_seq):
    kern = jnp.dot(bcat_ref[0], ccat_ref[0], precision=lax.Precision.HIGHEST,
                   preferred_element_type=F32)
    col = lax.broadcasted_iota(jnp.int32, kern.shape, 1)
    for s in range(SSM_CHUNK):
        shifted = pltpu.roll(kern, s * SSM_GROUP, 1) if s else kern
        toep_ref[s * SSM_GROUP:(s + 1) * SSM_GROUP, :] = jnp.where(
            col >= s * SSM_GROUP, shifted, 0.0).astype(BF16)

    u = u_ref[0]
    rows = u.shape[0]
    per = rows // n_seq
    y = _dot(u, toep_ref[...])
    st = _dot(u, n_ref[0])
    row = lax.broadcasted_iota(jnp.int32, (per, LANES), 0)
    prev = []
    for b in range(n_seq):
        x = st[b * per:(b + 1) * per]
        k = 0
        while (1 << k) < per:
            d = 1 << k
            sh = jnp.where(row >= d, pltpu.roll(x, d, 0), 0.0)
            x = x + sh * lr_ref[0, k:k + 1, :] + pltpu.roll(sh, SSM_STATE, 1) * li_ref[0, k:k + 1, :]
            k += 1
        prev.append(jnp.where(row >= 1, pltpu.roll(x, 1, 0), 0.0))
    xp = jnp.concatenate(prev, axis=0).astype(BF16)
    y = y + _dot(xp, m_ref[0]) + d2_ref[0] * u.astype(F32)
    y_ref[0] = _gelu(y).astype(BF16)


def _ssm(u2, bcat, ccat, nmat, mmat, lr, li, d2, n_seq):
    g, rows, w = u2.shape
    blk = lambda a: pl.BlockSpec((1,) + a.shape[1:], lambda i: (i, 0, 0))
    return pl.pallas_call(
        functools.partial(_ssm_body, n_seq=n_seq),
        grid=(g,),
        in_specs=[blk(u2), blk(bcat), blk(ccat), blk(nmat), blk(mmat), blk(lr), blk(li), blk(d2)],
        out_specs=blk(u2),
        out_shape=jax.ShapeDtypeStruct(u2.shape, BF16),
        scratch_shapes=[pltpu.VMEM((w, w), BF16)],
        compiler_params=pltpu.CompilerParams(dimension_semantics=("arbitrary",),
                                             vmem_limit_bytes=VMEM_LIMIT),
        name="ssm",
    )(u2, bcat, ccat, nmat, mmat, lr, li, d2)


def _merge_body(x_ref, y2_ref, yb_ref, yc_ref, gmix_ref, wgate_ref, bgate_ref, wglu_ref, bglu_ref,
                wbr_ref, wout_ref, gffn_ref, wrt_ref, brt_ref,
                h_ref, xnp_ref, rt_ref, rtt_ref, cnt_ref, carry_ref, tok_ref):
    i = pl.program_id(0)

    @pl.when(i == 0)
    def _():
        carry_ref[...] = jnp.zeros_like(carry_ref)

    _chunks_to_tokens(y2_ref, tok_ref)
    x = x_ref[...]
    n = _rms(x, gmix_ref[...]).astype(BF16)
    gates = _sigmoid(_dot(n, wgate_ref[...]) + bgate_ref[...])
    ys = jnp.concatenate([tok_ref[k] for k in range(SSM_WIDTH // LANES)], axis=1).astype(BF16)
    glu = _dot(ys, wglu_ref[...]) + bglu_ref[...]
    ya = (glu[:, :SSM_WIDTH] * _sigmoid(glu[:, SSM_WIDTH:])).astype(BF16)
    merged = (gates[:, :D_MODEL] * _dot(ya, wbr_ref[0])
              + gates[:, D_MODEL:2 * D_MODEL] * _dot(yb_ref[...], wbr_ref[1])
              + gates[:, 2 * D_MODEL:] * _dot(yc_ref[...], wbr_ref[2]))
    h = x + _dot(merged.astype(BF16), wout_ref[...])
    h_ref[...] = h

    xn = _rms(h, gffn_ref[...])
    packed = _pack_bf16_pair(xn)
    for j in range(SC_SPLIT):
        xnp_ref[j] = packed[:, j * SC_ROW:(j + 1) * SC_ROW]

    logits = jnp.dot(xn, wrt_ref[...], precision=lax.Precision.HIGHEST,
                     preferred_element_type=F32) + brt_ref[...]
    tm = logits.shape[0]
    lane_i = lax.broadcasted_iota(jnp.int32, (tm, LANES), 1)
    lane = lane_i.astype(F32)
    neg = jnp.float32(-3.0e38)
    big = jnp.float32(LANES)
    gmask = lane_i < N_GROUPS
    gl = jnp.where(gmask, logits, neg)
    gmax = jnp.max(gl, axis=-1, keepdims=True)
    gidx = jnp.min(jnp.where(gl == gmax, lane, big), axis=-1, keepdims=True)
    gsum = jnp.sum(jnp.where(gmask, jnp.exp(gl - gmax), 0.0), axis=-1, keepdims=True)
    g_w = 1.0 / gsum
    e_lane = lane_i - ROUTE_LANE0
    lane_group = (e_lane >> 3).astype(F32)
    emask = (e_lane >= 0) & (e_lane < N_EXPERTS) & (lane_group == gidx)
    el = jnp.where(emask, logits, neg)
    m1 = jnp.max(el, axis=-1, keepdims=True)
    i1 = jnp.min(jnp.where(el == m1, lane, big), axis=-1, keepdims=True)
    el2 = jnp.where(lane == i1, neg, el)
    m2 = jnp.max(el2, axis=-1, keepdims=True)
    i2 = jnp.min(jnp.where(el2 == m2, lane, big), axis=-1, keepdims=True)
    t = jnp.exp(m2 - m1)
    w1 = g_w / (1.0 + t)
    w2 = g_w * t / (1.0 + t)

    sel1 = lane == i1
    sel2 = lane == i2
    onehot = jnp.where(sel1 | sel2, 1.0, 0.0)
    r_i = lax.broadcasted_iota(jnp.int32, (tm, tm), 0)
    c_i = lax.broadcasted_iota(jnp.int32, (tm, tm), 1)
    stril = jnp.where(c_i < r_i, 1.0, 0.0).astype(BF16)
    cum = _dot(stril, onehot.astype(BF16)) + carry_ref[0:1, :]
    rank1 = jnp.sum(jnp.where(sel1, cum, 0.0), axis=-1, keepdims=True)
    rank2 = jnp.sum(jnp.where(sel2, cum, 0.0), axis=-1, keepdims=True)
    carry_ref[...] = carry_ref[...] + jnp.sum(onehot, axis=0, keepdims=True)
    cnt_ref[...] = carry_ref[...]

    cols = (i1 - ROUTE_LANE0, i2 - ROUTE_LANE0, rank1, rank2, w1, w2)
    rt = jnp.zeros((tm, LANES), F32)
    for c, val in enumerate(cols):
        rt = jnp.where(lane_i == c, val, rt)
    rt_ref[...] = rt
    rtt_ref[...] = rt.T[:8]


def _merge_route(x, y2, yb, yc, g_mix, w_gate, b_gate, w_glu, b_glu, w_br, w_out, g_ffn, w_rt, b_rt):
    t, d = x.shape
    tm = TM_MERGE
    tok = lambda i: (i, 0)
    c2 = lambda i: (0, 0)
    c3 = lambda i: (0, 0, 0)
    full = lambda a: pl.BlockSpec(a.shape, c2 if a.ndim == 2 else c3)
    return pl.pallas_call(
        _merge_body,
        grid=(t // tm,),
        in_specs=[pl.BlockSpec((tm, d), tok),
                  pl.BlockSpec((SSM_GROUPS, tm // SSM_CHUNK, SSM_CHUNK * SSM_GROUP), lambda i: (0, i, 0)),
                  pl.BlockSpec((tm, SSM_WIDTH), tok),
                  pl.BlockSpec((tm, SSM_WIDTH), tok),
                  full(g_mix), full(w_gate), full(b_gate), full(w_glu), full(b_glu),
                  full(w_br), full(w_out), full(g_ffn), full(w_rt), full(b_rt)],
        out_specs=[pl.BlockSpec((tm, d), tok),
                   pl.BlockSpec((SC_SPLIT, tm, SC_ROW), lambda i: (0, i, 0)),
                   pl.BlockSpec((tm, LANES), tok),
                   pl.BlockSpec((8, tm), lambda i: (0, i)),
                   pl.BlockSpec((8, LANES), c2)],
        out_shape=[jax.ShapeDtypeStruct((t, d), F32),
                   jax.ShapeDtypeStruct((SC_SPLIT, t, SC_ROW), jnp.uint32),
                   jax.ShapeDtypeStruct((t, LANES), F32),
                   jax.ShapeDtypeStruct((8, t), F32),
                   jax.ShapeDtypeStruct((8, LANES), F32)],
        scratch_shapes=[pltpu.VMEM((8, LANES), F32),
                        pltpu.VMEM((SSM_WIDTH // LANES, tm, LANES), F32)],
        compiler_params=pltpu.CompilerParams(dimension_semantics=("arbitrary",),
                                             vmem_limit_bytes=VMEM_LIMIT),
        name="merge_route",
    )(x, y2, yb, yc, g_mix, w_gate, b_gate, w_glu, b_glu, w_br, w_out, g_ffn, w_rt, b_rt)


def _slot_body(rtt_ref, pstart_ref, out_ref, *, nslots):
    tl = rtt_ref.shape[1]
    expert = lax.broadcasted_iota(jnp.int32, (N_EXPERTS, tl), 0).astype(F32)
    for k in range(TOP_K):
        onehot = rtt_ref[k:k + 1, :] == expert
        start = jnp.sum(jnp.where(onehot, pstart_ref[...], 0.0), axis=0, keepdims=True)
        slot = (start + rtt_ref[TOP_K + k:TOP_K + k + 1, :]).astype(jnp.int32)
        for j in range(SC_SPLIT):
            out_ref[k * SC_SPLIT + j:k * SC_SPLIT + j + 1, :] = slot + j * nslots


def _slot_rows(rtt, pstart, nslots):
    t = rtt.shape[1]
    tl = SLOT_LANES
    return pl.pallas_call(
        functools.partial(_slot_body, nslots=nslots),
        grid=(t // tl,),
        in_specs=[pl.BlockSpec((8, tl), lambda i: (0, i)),
                  pl.BlockSpec((N_EXPERTS, 1), lambda i: (0, 0))],
        out_specs=pl.BlockSpec((TOP_K * SC_SPLIT, tl), lambda i: (0, i)),
        out_shape=jax.ShapeDtypeStruct((TOP_K * SC_SPLIT, t), jnp.int32),
        compiler_params=pltpu.CompilerParams(dimension_semantics=("arbitrary",)),
        name="slot_rows",
    )(rtt, pstart.astype(F32).reshape(N_EXPERTS, 1))


def _expert_body(be_ref, bv_ref, buf_ref, w1_ref, w3_ref, w2_ref, out_ref):
    i = pl.program_id(0)
    valid = bv_ref[i]

    @pl.when(valid > 0)
    def _():
        x = _unpack_bf16_pair(jnp.concatenate([buf_ref[j] for j in range(SC_SPLIT)], axis=1))
        row = lax.broadcasted_iota(jnp.int32, x.shape, 0)
        x = jnp.where(row < valid, x, 0.0).astype(BF16)
        h1 = _dot(x, w1_ref[0].astype(BF16))
        h3 = _dot(x, w3_ref[0].astype(BF16))
        a = (h1 * _sigmoid(h1) * h3).astype(BF16)
        packed = _pack_bf16_pair(_dot(a, w2_ref[0].astype(BF16)))
        for j in range(SC_SPLIT):
            out_ref[j] = packed[:, j * SC_ROW:(j + 1) * SC_ROW]

    @pl.when(valid <= 0)
    def _():
        out_ref[...] = jnp.zeros_like(out_ref)


def _experts(block_expert, block_valid, buf, w1, w3, w2):
    _, nslots, _ = buf.shape
    nb = nslots // BM
    d = w1.shape[1]
    rows = pl.BlockSpec((SC_SPLIT, BM, SC_ROW), lambda i, be, bv: (0, i, 0))
    grid_spec = pltpu.PrefetchScalarGridSpec(
        num_scalar_prefetch=2,
        grid=(nb,),
        in_specs=[rows,
                  pl.BlockSpec((1, d, D_FF), lambda i, be, bv: (be[i], 0, 0)),
                  pl.BlockSpec((1, d, D_FF), lambda i, be, bv: (be[i], 0, 0)),
                  pl.BlockSpec((1, D_FF, d), lambda i, be, bv: (be[i], 0, 0))],
        out_specs=rows,
    )
    return pl.pallas_call(
        _expert_body,
        grid_spec=grid_spec,
        out_shape=jax.ShapeDtypeStruct(buf.shape, jnp.uint32),
        compiler_params=pltpu.CompilerParams(dimension_semantics=("arbitrary",),
                                             vmem_limit_bytes=VMEM_LIMIT),
        name="experts",
    )(block_expert, block_valid, buf, w1, w3, w2)


def _sc_mesh():
    return plsc.VectorSubcoreMesh(core_axis_name="core", subcore_axis_name="subcore")


def _dispatch_rows(rows, dest0, dest1, nslots):
    t, w = rows.shape
    win = SC_WINDOW
    idx_spec = pl.BlockSpec((1, win), lambda i: (0, i))

    @functools.partial(pl.kernel, mesh=_sc_mesh(), scratch_types=[],
                       out_type=jax.ShapeDtypeStruct((nslots, w), rows.dtype), name="dispatch_rows")
    def run(rows_hbm, i0_hbm, i1_hbm, out_hbm):
        def body(rows_vmem, i0_vmem, i1_vmem):
            pltpu.sync_copy(rows_vmem, out_hbm.at[i0_vmem.at[0]])
            pltpu.sync_copy(rows_vmem, out_hbm.at[i1_vmem.at[0]])

        pltpu.emit_pipeline(
            body, grid=(t // win,),
            in_specs=[pl.BlockSpec((win, w), lambda i: (i, 0)), idx_spec, idx_spec],
            out_specs=[],
            core_axis_name=("core", "subcore"),
            dimension_semantics=(pltpu.PARALLEL,),
        )(rows_hbm, i0_hbm, i1_hbm)

    return run(rows, dest0.reshape(1, t), dest1.reshape(1, t))


def _gather_rows(table, idx):
    n = idx.shape[0]
    w = table.shape[1]
    win = SC_WINDOW

    @functools.partial(pl.kernel, mesh=_sc_mesh(), scratch_types=[],
                       out_type=jax.ShapeDtypeStruct((n, w), table.dtype), name="gather_rows")
    def run(table_hbm, i_hbm, out_hbm):
        def body(i_vmem, out_vmem):
            pltpu.sync_copy(table_hbm.at[i_vmem.at[0]], out_vmem)

        pltpu.emit_pipeline(
            body, grid=(n // win,),
            in_specs=[pl.BlockSpec((1, win), lambda i: (0, i))],
            out_specs=[pl.BlockSpec((win, w), lambda i: (i, 0))],
            core_axis_name=("core", "subcore"),
            dimension_semantics=(pltpu.PARALLEL,),
        )(i_hbm, out_hbm)

    return run(table, idx.reshape(1, n))


def _combine_body(h_ref, g_ref, rt_ref, gfin_ref, out_ref):
    rt = rt_ref[...]
    y = h_ref[...]
    for k in range(TOP_K):
        rows = jnp.concatenate([g_ref[k * SC_SPLIT + j] for j in range(SC_SPLIT)], axis=1)
        y = y + rt[:, 4 + k:5 + k] * _unpack_bf16_pair(rows)
    out_ref[...] = _rms(y, gfin_ref[...])


def _combine(h, g, rt, g_final):
    t, d = h.shape
    tm = TM_OUT
    tok = lambda i: (i, 0)
    return pl.pallas_call(
        _combine_body,
        grid=(t // tm,),
        in_specs=[pl.BlockSpec((tm, d), tok),
                  pl.BlockSpec((TOP_K * SC_SPLIT, tm, SC_ROW), lambda i: (0, i, 0)),
                  pl.BlockSpec((tm, LANES), tok),
                  pl.BlockSpec((1, d), lambda i: (0, 0))],
        out_specs=pl.BlockSpec((tm, d), tok),
        out_shape=jax.ShapeDtypeStruct((t, d), F32),
        compiler_params=pltpu.CompilerParams(dimension_semantics=("arbitrary",),
                                             vmem_limit_bytes=VMEM_LIMIT),
        name="combine",
    )(h, g, rt, g_final)


def _layer(h, mem, g_mix, g_mem, w_in, w_gate, b_gate, lam_re, lam_im, log_dt, b_re, b_im,
           c_re, c_im, d_skip, w_glu, b_glu, g_sgu, w_spatial, b_spatial, w_kv, w_branch,
           w_out, g_ffn, w_group, b_group, w_router, b_router, w1, w3, w2, g_out):
    bsz, s, d = h.shape
    t = bsz * s
    row = lambda a: a.reshape(1, -1)

    k, v = _kv_proj(mem, row(g_mem), w_kv.astype(BF16))

    tril = jnp.tril(jnp.ones((CHUNK, CHUNK), dtype=bool))
    w_sp = jnp.where(tril, w_spatial, 0.0).astype(BF16)
    b_sp = jnp.broadcast_to(b_spatial[:, :, None], (SGU_HEADS, CHUNK, SGU_HEAD_DIM))
    u2, y_b, y_c = _in_proj(h, row(g_mix), w_in.astype(BF16), row(g_sgu), w_sp, b_sp, k, v)
    y2 = _ssm(u2, *_ssm_params(lam_re, lam_im, log_dt, b_re, b_im, c_re, c_im, d_skip), n_seq=bsz)

    pad = LANES - N_GROUPS - N_EXPERTS
    w_rt = jnp.concatenate([w_group, w_router, jnp.zeros((d, pad), F32)], axis=1)
    b_rt = jnp.concatenate([b_group, b_router, jnp.zeros((pad,), F32)]).reshape(1, LANES)
    h2, xnp, rt, rtt, cnt = _merge_route(
        h.reshape(t, d), y2, y_b.reshape(t, -1), y_c.reshape(t, -1), row(g_mix),
        w_gate.astype(BF16), row(b_gate), w_glu.astype(BF16), row(b_glu),
        w_branch.astype(BF16), w_out.astype(BF16), row(g_ffn), w_rt, b_rt)

    counts = cnt[0, ROUTE_LANE0:ROUTE_LANE0 + N_EXPERTS].astype(jnp.int32)
    padded = (counts + BM - 1) // BM * BM
    block_end = jnp.cumsum(padded)
    pstart = block_end - padded
    nb = (t * TOP_K) // BM + N_EXPERTS
    blk_row0 = jnp.arange(nb, dtype=jnp.int32) * BM
    block_expert = jnp.sum((block_end[None, :] <= blk_row0[:, None]).astype(jnp.int32), axis=1)
    block_expert = jnp.minimum(block_expert, N_EXPERTS - 1)
    block_valid = jnp.clip(counts[block_expert] - (blk_row0 - pstart[block_expert]), 0, BM)
    block_valid = jnp.where(blk_row0 < block_end[-1], block_valid, 0).astype(jnp.int32)
    nslots = nb * BM
    dest_p = _slot_rows(rtt, pstart, nslots).reshape(TOP_K, SC_SPLIT * t)
    buf = _dispatch_rows(xnp.reshape(SC_SPLIT * t, SC_ROW), dest_p[0], dest_p[1], SC_SPLIT * nslots)
    yb = _experts(block_expert, block_valid, buf.reshape(SC_SPLIT, nslots, SC_ROW), w1, w3, w2)
    g = _gather_rows(yb.reshape(SC_SPLIT * nslots, SC_ROW), dest_p.reshape(-1))
    out = _combine(h2, g.reshape(TOP_K * SC_SPLIT, t, SC_ROW), rt, row(g_out))
    return out.reshape(bsz, s, d)


def kernel(x, mem, g_mix, g_mem, w_in, w_gate, b_gate, lam_re, lam_im, log_dt, b_re, b_im, c_re,
           c_im, d_skip, w_glu, b_glu, g_sgu, w_spatial, b_spatial, w_kv, w_branch, w_out, g_ffn,
           w_group, b_group, w_router, b_router, w1, w3, w2, g_final):
    assert g_mix.shape[0] == 1, "single-layer stack"
    return _layer(x, mem, g_mix[0], g_mem[0], w_in[0], w_gate[0], b_gate[0], lam_re[0], lam_im[0],
                  log_dt[0], b_re[0], b_im[0], c_re[0], c_im[0], d_skip[0], w_glu[0], b_glu[0],
                  g_sgu[0], w_spatial[0], b_spatial[0], w_kv[0], w_branch[0], w_out[0], g_ffn[0],
                  w_group[0], b_group[0], w_router[0], b_router[0], w1[0], w3[0], w2[0], g_final)
```

```python
import functools
import math

import jax
import jax.numpy as jnp
from jax import lax
from jax.experimental import pallas as pl
from jax.experimental.pallas import tpu as pltpu
from jax.experimental.pallas import tpu_sc as plsc

F32 = jnp.float32
BF16 = jnp.bfloat16

EPS = 1e-6
D_MODEL = 1024
SSM_WIDTH = 512
SSM_GROUP = 16
SSM_GROUPS = 32
SSM_STATE = 64
SSM_CHUNK = 16
SGU_WIDTH = 512
SGU_HEADS = 4
SGU_HEAD_DIM = 128
CHUNK = 128
XA_HEADS = 4
XA_HEAD_DIM = 128
N_GROUPS = 8
EXPERTS_PER_GROUP = 8
N_EXPERTS = 64
TOP_K = 2
D_FF = 512
LANES = 128
ROUTE_LANE0 = N_GROUPS

TM_IN = 512
TM_MERGE = 512
MERGE_COLS = 256
TM_OUT = 512
BM = 256
SC_WINDOW = 128
SC_ROW = 256
SC_SPLIT = (D_MODEL // 2) // SC_ROW
SLOT_LANES = 2048
VMEM_LIMIT = 56 * 1024 * 1024


def _rms(x, g):
    return x * lax.rsqrt(jnp.mean(x * x, axis=-1, keepdims=True) + EPS) * g


def _sigmoid(x):
    return 0.5 * (1.0 + jnp.tanh(0.5 * x))


def _gelu(x):
    c = math.sqrt(2.0 / math.pi)
    return 0.5 * x * (1.0 + jnp.tanh(c * (x + 0.044715 * (x * x * x))))


def _dot(a, b):
    return jnp.dot(a, b, preferred_element_type=F32)


def _pack_bf16_pair(x):
    n = x.shape[1] // 2
    lo = lax.bitcast_convert_type(x[:, :n].astype(BF16).astype(F32), jnp.uint32)
    hi = lax.bitcast_convert_type(x[:, n:].astype(BF16).astype(F32), jnp.uint32)
    return hi | (lo >> 16)


def _unpack_bf16_pair(p):
    lo = lax.bitcast_convert_type(p << 16, F32)
    hi = lax.bitcast_convert_type(p & jnp.uint32(0xFFFF0000), F32)
    return jnp.concatenate([lo, hi], axis=1)


GROUPS_PER_TILE = LANES // SSM_GROUP
POS_PER_TILE = LANES // SSM_GROUP


def _slot_masks(rows):
    lane = lax.broadcasted_iota(jnp.int32, (rows, LANES), 1)
    return [(lane >= i * SSM_GROUP) & (lane < (i + 1) * SSM_GROUP) for i in range(LANES // SSM_GROUP)]


def _tokens_to_chunks(tok_ref, out_ref):
    tm = tok_ref.shape[1]
    nc = tm // SSM_CHUNK
    masks = _slot_masks(nc)
    for k in range(SSM_WIDTH // LANES):
        for j in range(SSM_CHUNK // POS_PER_TILE):
            src = [tok_ref[k, pl.ds(j * POS_PER_TILE + p, nc, stride=SSM_CHUNK), :]
                   for p in range(POS_PER_TILE)]
            for gi in range(GROUPS_PER_TILE):
                acc = None
                for p in range(POS_PER_TILE):
                    shift = ((p - gi) * SSM_GROUP) % LANES
                    r = pltpu.roll(src[p], shift, 1) if shift else src[p]
                    acc = r if acc is None else jnp.where(masks[p], r, acc)
                out_ref[k * GROUPS_PER_TILE + gi, :, pl.ds(j * LANES, LANES)] = acc.astype(out_ref.dtype)


def _chunks_to_tokens(chunk_ref, tok_ref):
    tm = tok_ref.shape[1]
    nc = tm // SSM_CHUNK
    masks = _slot_masks(nc)
    for k in range(SSM_WIDTH // LANES):
        for j in range(SSM_CHUNK // POS_PER_TILE):
            src = [chunk_ref[k * GROUPS_PER_TILE + gi, :, pl.ds(j * LANES, LANES)].astype(F32)
                   for gi in range(GROUPS_PER_TILE)]
            for p in range(POS_PER_TILE):
                acc = None
                for gi in range(GROUPS_PER_TILE):
                    shift = ((gi - p) * SSM_GROUP) % LANES
                    r = pltpu.roll(src[gi], shift, 1) if shift else src[gi]
                    acc = r if acc is None else jnp.where(masks[gi], r, acc)
                tok_ref[k, pl.ds(j * POS_PER_TILE + p, nc, stride=SSM_CHUNK), :] = acc


def _kv_body(mem_ref, g_ref, w_ref, k_ref, v_ref):
    n = _rms(mem_ref[0], g_ref[...]).astype(BF16)
    kv = _dot(n, w_ref[...])
    k_ref[0] = kv[:, :SGU_WIDTH].astype(BF16)
    v_ref[0] = kv[:, SGU_WIDTH:].astype(BF16)


def _kv_proj(mem, g_mem, w_kv):
    b, m, d = mem.shape
    return pl.pallas_call(
        _kv_body,
        grid=(b,),
        in_specs=[pl.BlockSpec((1, m, d), lambda i: (i, 0, 0)),
                  pl.BlockSpec((1, d), lambda i: (0, 0)),
                  pl.BlockSpec((d, 2 * SGU_WIDTH), lambda i: (0, 0))],
        out_specs=[pl.BlockSpec((1, m, SGU_WIDTH), lambda i: (i, 0, 0)),
                   pl.BlockSpec((1, m, SGU_WIDTH), lambda i: (i, 0, 0))],
        out_shape=[jax.ShapeDtypeStruct((b, m, SGU_WIDTH), BF16),
                   jax.ShapeDtypeStruct((b, m, SGU_WIDTH), BF16)],
        compiler_params=pltpu.CompilerParams(dimension_semantics=("arbitrary",),
                                             vmem_limit_bytes=VMEM_LIMIT),
        name="kv_proj",
    )(mem, g_mem, w_kv)


def _in_body(x_ref, gmix_ref, win_ref, gsgu_ref, wsp_ref, bsp_ref, k_ref, v_ref,
             u2_ref, yb_ref, yc_ref, tok_ref):
    n = _rms(x_ref[0], gmix_ref[...]).astype(BF16)
    proj = _dot(n, win_ref[...])
    for k in range(SSM_WIDTH // LANES):
        tok_ref[k] = proj[:, k * LANES:(k + 1) * LANES]
    _tokens_to_chunks(tok_ref, u2_ref)

    uv = _gelu(proj[:, SSM_WIDTH:SSM_WIDTH + 2 * SGU_WIDTH])
    u = uv[:, :SGU_WIDTH]
    v = _rms(uv[:, SGU_WIDTH:], gsgu_ref[...]).astype(BF16)
    tm = u.shape[0]
    rows = []
    for c in range(tm // CHUNK):
        vc = v[c * CHUNK:(c + 1) * CHUNK]
        heads = []
        for h in range(SGU_HEADS):
            sl = slice(h * SGU_HEAD_DIM, (h + 1) * SGU_HEAD_DIM)
            heads.append(_dot(wsp_ref[h], vc[:, sl]) + bsp_ref[h])
        rows.append(jnp.concatenate(heads, axis=1))
    sv = jnp.concatenate(rows, axis=0)
    yb_ref[0] = (u * sv).astype(BF16)

    q = proj[:, SSM_WIDTH + 2 * SGU_WIDTH:].astype(BF16)
    kk = k_ref[0]
    vv = v_ref[0]
    outs = []
    for h in range(XA_HEADS):
        sl = slice(h * XA_HEAD_DIM, (h + 1) * XA_HEAD_DIM)
        s = lax.dot_general(q[:, sl], kk[:, sl], (((1,), (1,)), ((), ())),
                            preferred_element_type=F32) * (XA_HEAD_DIM ** -0.5)
        e = jnp.exp(s - jnp.max(s, axis=-1, keepdims=True))
        l = jnp.sum(e, axis=-1, keepdims=True)
        outs.append(_dot(e.astype(BF16), vv[:, sl]) / l)
    yc_ref[0] = jnp.concatenate(outs, axis=1).astype(BF16)


def _in_proj(x, g_mix, w_in, g_sgu, w_sp, b_sp, k, v):
    b, s, d = x.shape
    m = k.shape[1]
    const2 = lambda i, j: (0, 0)
    const3 = lambda i, j: (0, 0, 0)
    tok = lambda i, j: (i, j, 0)
    per_b = lambda i, j: (i, 0, 0)
    out = jax.ShapeDtypeStruct((b, s, SSM_WIDTH), BF16)
    nc = TM_IN // SSM_CHUNK
    tiles = s // TM_IN
    u2 = jax.ShapeDtypeStruct((SSM_GROUPS, b * s // SSM_CHUNK, SSM_CHUNK * SSM_GROUP), BF16)
    return pl.pallas_call(
        _in_body,
        grid=(b, s // TM_IN),
        in_specs=[pl.BlockSpec((1, TM_IN, d), tok),
                  pl.BlockSpec((1, d), const2),
                  pl.BlockSpec(w_in.shape, const2),
                  pl.BlockSpec((1, SGU_WIDTH), const2),
                  pl.BlockSpec(w_sp.shape, const3),
                  pl.BlockSpec(b_sp.shape, const3),
                  pl.BlockSpec((1, m, SGU_WIDTH), per_b),
                  pl.BlockSpec((1, m, SGU_WIDTH), per_b)],
        out_specs=[pl.BlockSpec((SSM_GROUPS, nc, SSM_CHUNK * SSM_GROUP), lambda i, j: (0, i * tiles + j, 0)),
                   pl.BlockSpec((1, TM_IN, SSM_WIDTH), tok),
                   pl.BlockSpec((1, TM_IN, SSM_WIDTH), tok)],
        out_shape=[u2, out, out],
        scratch_shapes=[pltpu.VMEM((SSM_WIDTH // LANES, TM_IN, LANES), F32)],
        compiler_params=pltpu.CompilerParams(dimension_semantics=("arbitrary", "arbitrary"),
                                             vmem_limit_bytes=VMEM_LIMIT),
        name="in_proj",
    )(x, g_mix, w_in, g_sgu, w_sp, b_sp, k, v)


def _ssm_params(lam_re, lam_im, log_dt, b_re, b_im, c_re, c_im, d_skip):
    g, p = lam_re.shape
    dt = jnp.exp(log_dt)[:, None]
    ar = lam_re * dt
    ai = lam_im * dt

    def cpow(j):
        mag = jnp.exp(ar[:, None, :] * j[None, :, None])
        ph = ai[:, None, :] * j[None, :, None]
        return mag * jnp.cos(ph), mag * jnp.sin(ph)

    pr, pi = cpow(jnp.arange(SSM_CHUNK + 1, dtype=F32))
    nr = pr[:, 1] - 1.0
    ni = pi[:, 1]
    den = lam_re * lam_re + lam_im * lam_im
    fr = (nr * lam_re + ni * lam_im) / den
    fi = (ni * lam_re - nr * lam_im) / den
    bbr = fr[..., None] * b_re - fi[..., None] * b_im
    bbi = fr[..., None] * b_im + fi[..., None] * b_re

    L = SSM_CHUNK
    cr = c_re[:, None]
    ci = c_im[:, None]
    cpr = cr * pr[:, :L, None, :] - ci * pi[:, :L, None, :]
    cpi = cr * pi[:, :L, None, :] + ci * pr[:, :L, None, :]
    ccat = jnp.concatenate([cpr, -cpi], axis=-1)
    ccat = ccat.transpose(0, 3, 1, 2).reshape(g, 2 * p, L * SSM_GROUP)
    bcat = jnp.concatenate([bbr, bbi], axis=1).transpose(0, 2, 1)

    rr = pr[:, L - 1::-1][:, :L]
    ri = pi[:, L - 1::-1][:, :L]
    bbr_t = bbr.transpose(0, 2, 1)[:, None]
    bbi_t = bbi.transpose(0, 2, 1)[:, None]
    n_re = rr[:, :, None, :] * bbr_t - ri[:, :, None, :] * bbi_t
    n_im = rr[:, :, None, :] * bbi_t + ri[:, :, None, :] * bbr_t
    nmat = jnp.concatenate([n_re, n_im], axis=-1).reshape(g, L * SSM_GROUP, 2 * p)

    clr = cr * pr[:, 1:, None, :] - ci * pi[:, 1:, None, :]
    cli = cr * pi[:, 1:, None, :] + ci * pr[:, 1:, None, :]
    mmat = jnp.concatenate([clr, -cli], axis=-1)
    mmat = mmat.transpose(0, 3, 1, 2).reshape(g, 2 * p, L * SSM_GROUP)

    qr, qi = cpow(jnp.asarray([float(L * 2 ** k) for k in range(8)], F32))
    lr = jnp.concatenate([qr, qr], axis=-1)
    li = jnp.concatenate([-qi, qi], axis=-1)
    d2 = jnp.tile(d_skip.reshape(g, 1, SSM_GROUP), (1, 1, L))
    return bcat, ccat, nmat.astype(BF16), mmat.astype(BF16), lr, li, d2


def _ssm_body(u_ref, bcat_ref, ccat_ref, n_ref, m_ref, lr_ref, li_ref, d2_ref, y_ref, toep_ref, *, n_seq):
    kern = jnp.dot(bcat_ref[0], ccat_ref[0], precision=lax.Precision.HIGHEST,
                   preferred_element_type=F32)
    col = lax.broadcasted_iota(jnp.int32, kern.shape, 1)
    for s in range(SSM_CHUNK):
        shifted = pltpu.roll(kern, s * SSM_GROUP, 1) if s else kern
        toep_ref[s * SSM_GROUP:(s + 1) * SSM_GROUP, :] = jnp.where(
            col >= s * SSM_GROUP, shifted, 0.0).astype(BF16)

    u = u_ref[0]
    rows = u.shape[0]
    per = rows // n_seq
    y = _dot(u, toep_ref[...])
    st = _dot(u, n_ref[0])
    row = lax.broadcasted_iota(jnp.int32, (per, LANES), 0)
    prev = []
    for b in range(n_seq):
        x = st[b * per:(b + 1) * per]
        k = 0
        while (1 << k) < per:
            d = 1 << k
            sh = jnp.where(row >= d, pltpu.roll(x, d, 0), 0.0)
            x = x + sh * lr_ref[0, k:k + 1, :] + pltpu.roll(sh, SSM_STATE, 1) * li_ref[0, k:k + 1, :]
            k += 1
        prev.append(jnp.where(row >= 1, pltpu.roll(x, 1, 0), 0.0))
    xp = jnp.concatenate(prev, axis=0).astype(BF16)
    y = y + _dot(xp, m_ref[0]) + d2_ref[0] * u.astype(F32)
    y_ref[0] = _gelu(y).astype(BF16)


def _ssm(u2, bcat, ccat, nmat, mmat, lr, li, d2, n_seq):
    g, rows, w = u2.shape
    blk = lambda a: pl.BlockSpec((1,) + a.shape[1:], lambda i: (i, 0, 0))
    return pl.pallas_call(
        functools.partial(_ssm_body, n_seq=n_seq),
        grid=(g,),
        in_specs=[blk(u2), blk(bcat), blk(ccat), blk(nmat), blk(mmat), blk(lr), blk(li), blk(d2)],
        out_specs=blk(u2),
        out_shape=jax.ShapeDtypeStruct(u2.shape, BF16),
        scratch_shapes=[pltpu.VMEM((w, w), BF16)],
        compiler_params=pltpu.CompilerParams(dimension_semantics=("arbitrary",),
                                             vmem_limit_bytes=VMEM_LIMIT),
        name="ssm",
    )(u2, bcat, ccat, nmat, mmat, lr, li, d2)


def _merge_body(x_ref, y2_ref, yb_ref, yc_ref, gmix_ref, wgate_ref, bgate_ref, wglu_ref, bglu_ref,
                wbr_ref, wout_ref, gffn_ref, wrt_ref, brt_ref,
                h_ref, xnp_ref, rt_ref, rtt_ref, cnt_ref, carry_ref, tok_ref):
    i = pl.program_id(0)

    @pl.when(i == 0)
    def _():
        carry_ref[...] = jnp.zeros_like(carry_ref)

    _chunks_to_tokens(y2_ref, tok_ref)
    x = x_ref[...]
    n = _rms(x, gmix_ref[...]).astype(BF16)
    ys = jnp.concatenate([tok_ref[k] for k in range(SSM_WIDTH // LANES)], axis=1).astype(BF16)
    glu = _dot(ys, wglu_ref[...]) + bglu_ref[...]
    ya = (glu[:, :SSM_WIDTH] * _sigmoid(glu[:, SSM_WIDTH:])).astype(BF16)
    branches = (ya, yb_ref[...], yc_ref[...])
    merged = []
    for c in range(D_MODEL // MERGE_COLS):
        acc = None
        for b, yb in enumerate(branches):
            cols = pl.ds(b * D_MODEL + c * MERGE_COLS, MERGE_COLS)
            gate = _sigmoid(_dot(n, wgate_ref[:, cols]) + bgate_ref[:, cols])
            term = gate * _dot(yb, wbr_ref[b, :, pl.ds(c * MERGE_COLS, MERGE_COLS)])
            acc = term if acc is None else acc + term
        merged.append(acc.astype(BF16))
    h = x + _dot(jnp.concatenate(merged, axis=1), wout_ref[...])
    h_ref[...] = h

    xn = _rms(h, gffn_ref[...])
    packed = _pack_bf16_pair(xn)
    for j in range(SC_SPLIT):
        xnp_ref[j] = packed[:, j * SC_ROW:(j + 1) * SC_ROW]

    x_hi = xn.astype(BF16)
    x_lo = (xn - x_hi.astype(F32)).astype(BF16)
    head = _dot(x_hi, wrt_ref[...])
    logits = (head[:, :LANES] + head[:, LANES:] + _dot(x_lo, wrt_ref[:, :LANES])) + brt_ref[...]
    tm = logits.shape[0]
    lane_i = lax.broadcasted_iota(jnp.int32, (tm, LANES), 1)
    lane = lane_i.astype(F32)
    neg = jnp.float32(-3.0e38)
    big = jnp.float32(LANES)
    gmask = lane_i < N_GROUPS
    gl = jnp.where(gmask, logits, neg)
    gmax = jnp.max(gl, axis=-1, keepdims=True)
    gidx = jnp.min(jnp.where(gl == gmax, lane, big), axis=-1, keepdims=True)
    gsum = jnp.sum(jnp.where(gmask, jnp.exp(gl - gmax), 0.0), axis=-1, keepdims=True)
    g_w = 1.0 / gsum
    e_lane = lane_i - ROUTE_LANE0
    lane_group = (e_lane >> 3).astype(F32)
    emask = (e_lane >= 0) & (e_lane < N_EXPERTS) & (lane_group == gidx)
    el = jnp.where(emask, logits, neg)
    m1 = jnp.max(el, axis=-1, keepdims=True)
    i1 = jnp.min(jnp.where(el == m1, lane, big), axis=-1, keepdims=True)
    el2 = jnp.where(lane == i1, neg, el)
    m2 = jnp.max(el2, axis=-1, keepdims=True)
    i2 = jnp.min(jnp.where(el2 == m2, lane, big), axis=-1, keepdims=True)
    t = jnp.exp(m2 - m1)
    w1 = g_w / (1.0 + t)
    w2 = g_w * t / (1.0 + t)

    sel1 = lane == i1
    sel2 = lane == i2
    onehot = jnp.where(sel1 | sel2, 1.0, 0.0)
    r_i = lax.broadcasted_iota(jnp.int32, (tm, tm), 0)
    c_i = lax.broadcasted_iota(jnp.int32, (tm, tm), 1)
    stril = jnp.where(c_i < r_i, 1.0, 0.0).astype(BF16)
    cum = _dot(stril, onehot.astype(BF16)) + carry_ref[0:1, :]
    rank1 = jnp.sum(jnp.where(sel1, cum, 0.0), axis=-1, keepdims=True)
    rank2 = jnp.sum(jnp.where(sel2, cum, 0.0), axis=-1, keepdims=True)
    carry_ref[...] = carry_ref[...] + jnp.sum(onehot, axis=0, keepdims=True)
    cnt_ref[...] = carry_ref[...]

    cols = (i1 - ROUTE_LANE0, i2 - ROUTE_LANE0, rank1, rank2, w1, w2)
    rt = jnp.zeros((tm, LANES), F32)
    for c, val in enumerate(cols):
        rt = jnp.where(lane_i == c, val, rt)
    rt_ref[...] = rt
    rtt_ref[...] = rt.T[:8]


def _merge_route(x, y2, yb, yc, g_mix, w_gate, b_gate, w_glu, b_glu, w_br, w_out, g_ffn, w_rt, b_rt):
    t, d = x.shape
    tm = TM_MERGE
    tok = lambda i: (i, 0)
    c2 = lambda i: (0, 0)
    c3 = lambda i: (0, 0, 0)
    full = lambda a: pl.BlockSpec(a.shape, c2 if a.ndim == 2 else c3)
    return pl.pallas_call(
        _merge_body,
        grid=(t // tm,),
        in_specs=[pl.BlockSpec((tm, d), tok),
                  pl.BlockSpec((SSM_GROUPS, tm // SSM_CHUNK, SSM_CHUNK * SSM_GROUP), lambda i: (0, i, 0)),
                  pl.BlockSpec((tm, SSM_WIDTH), tok),
                  pl.BlockSpec((tm, SSM_WIDTH), tok),
                  full(g_mix), full(w_gate), full(b_gate), full(w_glu), full(b_glu),
                  full(w_br), full(w_out), full(g_ffn), full(w_rt), full(b_rt)],
        out_specs=[pl.BlockSpec((tm, d), tok),
                   pl.BlockSpec((SC_SPLIT, tm, SC_ROW), lambda i: (0, i, 0)),
                   pl.BlockSpec((tm, LANES), tok),
                   pl.BlockSpec((8, tm), lambda i: (0, i)),
                   pl.BlockSpec((8, LANES), c2)],
        out_shape=[jax.ShapeDtypeStruct((t, d), F32),
                   jax.ShapeDtypeStruct((SC_SPLIT, t, SC_ROW), jnp.uint32),
                   jax.ShapeDtypeStruct((t, LANES), F32),
                   jax.ShapeDtypeStruct((8, t), F32),
                   jax.ShapeDtypeStruct((8, LANES), F32)],
        scratch_shapes=[pltpu.VMEM((8, LANES), F32),
                        pltpu.VMEM((SSM_WIDTH // LANES, tm, LANES), F32)],
        compiler_params=pltpu.CompilerParams(dimension_semantics=("arbitrary",),
                                             vmem_limit_bytes=VMEM_LIMIT),
        name="merge_route",
    )(x, y2, yb, yc, g_mix, w_gate, b_gate, w_glu, b_glu, w_br, w_out, g_ffn, w_rt, b_rt)


def _slot_body(rtt_ref, pstart_ref, out_ref, *, nslots):
    tl = rtt_ref.shape[1]
    expert = lax.broadcasted_iota(jnp.int32, (N_EXPERTS, tl), 0).astype(F32)
    for k in range(TOP_K):
        onehot = rtt_ref[k:k + 1, :] == expert
        start = jnp.sum(jnp.where(onehot, pstart_ref[...], 0.0), axis=0, keepdims=True)
        slot = (start + rtt_ref[TOP_K + k:TOP_K + k + 1, :]).astype(jnp.int32)
        for j in range(SC_SPLIT):
            out_ref[k * SC_SPLIT + j:k * SC_SPLIT + j + 1, :] = slot + j * nslots


def _slot_rows(rtt, pstart, nslots):
    t = rtt.shape[1]
    tl = SLOT_LANES
    return pl.pallas_call(
        functools.partial(_slot_body, nslots=nslots),
        grid=(t // tl,),
        in_specs=[pl.BlockSpec((8, tl), lambda i: (0, i)),
                  pl.BlockSpec((N_EXPERTS, 1), lambda i: (0, 0))],
        out_specs=pl.BlockSpec((TOP_K * SC_SPLIT, tl), lambda i: (0, i)),
        out_shape=jax.ShapeDtypeStruct((TOP_K * SC_SPLIT, t), jnp.int32),
        compiler_params=pltpu.CompilerParams(dimension_semantics=("arbitrary",)),
        name="slot_rows",
    )(rtt, pstart.astype(F32).reshape(N_EXPERTS, 1))


def _expert_body(be_ref, bv_ref, buf_ref, w1_ref, w3_ref, w2_ref, out_ref):
    i = pl.program_id(0)
    valid = bv_ref[i]

    @pl.when(valid > 0)
    def _():
        x = _unpack_bf16_pair(jnp.concatenate([buf_ref[j] for j in range(SC_SPLIT)], axis=1))
        row = lax.broadcasted_iota(jnp.int32, x.shape, 0)
        x = jnp.where(row < valid, x, 0.0).astype(BF16)
        h1 = _dot(x, w1_ref[0].astype(BF16))
        h3 = _dot(x, w3_ref[0].astype(BF16))
        a = (h1 * _sigmoid(h1) * h3).astype(BF16)
        packed = _pack_bf16_pair(_dot(a, w2_ref[0].astype(BF16)))
        for j in range(SC_SPLIT):
            out_ref[j] = packed[:, j * SC_ROW:(j + 1) * SC_ROW]

    @pl.when(valid <= 0)
    def _():
        out_ref[...] = jnp.zeros_like(out_ref)


def _experts(block_expert, block_valid, buf, w1, w3, w2):
    _, nslots, _ = buf.shape
    nb = nslots // BM
    d = w1.shape[1]
    rows = pl.BlockSpec((SC_SPLIT, BM, SC_ROW), lambda i, be, bv: (0, i, 0))
    grid_spec = pltpu.PrefetchScalarGridSpec(
        num_scalar_prefetch=2,
        grid=(nb,),
        in_specs=[rows,
                  pl.BlockSpec((1, d, D_FF), lambda i, be, bv: (be[i], 0, 0)),
                  pl.BlockSpec((1, d, D_FF), lambda i, be, bv: (be[i], 0, 0)),
                  pl.BlockSpec((1, D_FF, d), lambda i, be, bv: (be[i], 0, 0))],
        out_specs=rows,
    )
    return pl.pallas_call(
        _expert_body,
        grid_spec=grid_spec,
        out_shape=jax.ShapeDtypeStruct(buf.shape, jnp.uint32),
        compiler_params=pltpu.CompilerParams(dimension_semantics=("arbitrary",),
                                             vmem_limit_bytes=VMEM_LIMIT),
        name="experts",
    )(block_expert, block_valid, buf, w1, w3, w2)


def _sc_mesh():
    return plsc.VectorSubcoreMesh(core_axis_name="core", subcore_axis_name="subcore")


def _dispatch_rows(rows, dest0, dest1, nslots):
    t, w = rows.shape
    win = SC_WINDOW
    idx_spec = pl.BlockSpec((1, win), lambda i: (0, i))

    @functools.partial(pl.kernel, mesh=_sc_mesh(), scratch_types=[],
                       out_type=jax.ShapeDtypeStruct((nslots, w), rows.dtype), name="dispatch_rows")
    def run(rows_hbm, i0_hbm, i1_hbm, out_hbm):
        def body(rows_vmem, i0_vmem, i1_vmem):
            pltpu.sync_copy(rows_vmem, out_hbm.at[i0_vmem.at[0]])
            pltpu.sync_copy(rows_vmem, out_hbm.at[i1_vmem.at[0]])

        pltpu.emit_pipeline(
            body, grid=(t // win,),
            in_specs=[pl.BlockSpec((win, w), lambda i: (i, 0)), idx_spec, idx_spec],
            out_specs=[],
            core_axis_name=("core", "subcore"),
            dimension_semantics=(pltpu.PARALLEL,),
        )(rows_hbm, i0_hbm, i1_hbm)

    return run(rows, dest0.reshape(1, t), dest1.reshape(1, t))


def _gather_rows(table, idx):
    n = idx.shape[0]
    w = table.shape[1]
    win = SC_WINDOW

    @functools.partial(pl.kernel, mesh=_sc_mesh(), scratch_types=[],
                       out_type=jax.ShapeDtypeStruct((n, w), table.dtype), name="gather_rows")
    def run(table_hbm, i_hbm, out_hbm):
        def body(i_vmem, out_vmem):
            pltpu.sync_copy(table_hbm.at[i_vmem.at[0]], out_vmem)

        pltpu.emit_pipeline(
            body, grid=(n // win,),
            in_specs=[pl.BlockSpec((1, win), lambda i: (0, i))],
            out_specs=[pl.BlockSpec((win, w), lambda i: (i, 0))],
            core_axis_name=("core", "subcore"),
            dimension_semantics=(pltpu.PARALLEL,),
        )(i_hbm, out_hbm)

    return run(table, idx.reshape(1, n))


def _combine_body(h_ref, g_ref, rt_ref, gfin_ref, out_ref):
    rt = rt_ref[...]
    y = h_ref[...]
    for k in range(TOP_K):
        rows = jnp.concatenate([g_ref[k * SC_SPLIT + j] for j in range(SC_SPLIT)], axis=1)
        y = y + rt[:, 4 + k:5 + k] * _unpack_bf16_pair(rows)
    out_ref[...] = _rms(y, gfin_ref[...])


def _combine(h, g, rt, g_final):
    t, d = h.shape
    tm = TM_OUT
    tok = lambda i: (i, 0)
    return pl.pallas_call(
        _combine_body,
        grid=(t // tm,),
        in_specs=[pl.BlockSpec((tm, d), tok),
                  pl.BlockSpec((TOP_K * SC_SPLIT, tm, SC_ROW), lambda i: (0, i, 0)),
                  pl.BlockSpec((tm, LANES), tok),
                  pl.BlockSpec((1, d), lambda i: (0, 0))],
        out_specs=pl.BlockSpec((tm, d), tok),
        out_shape=jax.ShapeDtypeStruct((t, d), F32),
        compiler_params=pltpu.CompilerParams(dimension_semantics=("arbitrary",),
                                             vmem_limit_bytes=VMEM_LIMIT),
        name="combine",
    )(h, g, rt, g_final)


def _layer(h, mem, g_mix, g_mem, w_in, w_gate, b_gate, lam_re, lam_im, log_dt, b_re, b_im,
           c_re, c_im, d_skip, w_glu, b_glu, g_sgu, w_spatial, b_spatial, w_kv, w_branch,
           w_out, g_ffn, w_group, b_group, w_router, b_router, w1, w3, w2, g_out):
    bsz, s, d = h.shape
    t = bsz * s
    row = lambda a: a.reshape(1, -1)

    k, v = _kv_proj(mem, row(g_mem), w_kv.astype(BF16))

    tril = jnp.tril(jnp.ones((CHUNK, CHUNK), dtype=bool))
    w_sp = jnp.where(tril, w_spatial, 0.0).astype(BF16)
    b_sp = jnp.broadcast_to(b_spatial[:, :, None], (SGU_HEADS, CHUNK, SGU_HEAD_DIM))
    u2, y_b, y_c = _in_proj(h, row(g_mix), w_in.astype(BF16), row(g_sgu), w_sp, b_sp, k, v)
    y2 = _ssm(u2, *_ssm_params(lam_re, lam_im, log_dt, b_re, b_im, c_re, c_im, d_skip), n_seq=bsz)

    pad = LANES - N_GROUPS - N_EXPERTS
    w_rt = jnp.concatenate([w_group, w_router, jnp.zeros((d, pad), F32)], axis=1)
    w_rt_hi = w_rt.astype(BF16)
    w_rt = jnp.concatenate([w_rt_hi, (w_rt - w_rt_hi.astype(F32)).astype(BF16)], axis=1)
    b_rt =jnp.concatenate([b_group, b_router, jnp.zeros((pad,), F32)]).reshape(1, LANES)
    h2, xnp, rt, rtt, cnt = _merge_route(
        h.reshape(t, d), y2, y_b.reshape(t, -1), y_c.reshape(t, -1), row(g_mix),
        w_gate.astype(BF16), row(b_gate), w_glu.astype(BF16), row(b_glu),
        w_branch.astype(BF16), w_out.astype(BF16), row(g_ffn), w_rt, b_rt)

    counts = cnt[0, ROUTE_LANE0:ROUTE_LANE0 + N_EXPERTS].astype(jnp.int32)
    padded = (counts + BM - 1) // BM * BM
    block_end = jnp.cumsum(padded)
    pstart = block_end - padded
    nb = (t * TOP_K) // BM + N_EXPERTS
    blk_row0 = jnp.arange(nb, dtype=jnp.int32) * BM
    block_expert = jnp.sum((block_end[None, :] <= blk_row0[:, None]).astype(jnp.int32), axis=1)
    block_expert = jnp.minimum(block_expert, N_EXPERTS - 1)
    block_valid = jnp.clip(counts[block_expert] - (blk_row0 - pstart[block_expert]), 0, BM)
    block_valid = jnp.where(blk_row0 < block_end[-1], block_valid, 0).astype(jnp.int32)
    nslots = nb * BM
    dest_p = _slot_rows(rtt, pstart, nslots).reshape(TOP_K, SC_SPLIT * t)
    buf = _dispatch_rows(xnp.reshape(SC_SPLIT * t, SC_ROW), dest_p[0], dest_p[1], SC_SPLIT * nslots)
    yb = _experts(block_expert, block_valid, buf.reshape(SC_SPLIT, nslots, SC_ROW), w1, w3, w2)
    g = _gather_rows(yb.reshape(SC_SPLIT * nslots, SC_ROW), dest_p.reshape(-1))
    out = _combine(h2, g.reshape(TOP_K * SC_SPLIT, t, SC_ROW), rt, row(g_out))
    return out.reshape(bsz, s, d)


def kernel(x, mem, g_mix, g_mem, w_in, w_gate, b_gate, lam_re, lam_im, log_dt, b_re, b_im, c_re,
           c_im, d_skip, w_glu, b_glu, g_sgu, w_spatial, b_spatial, w_kv, w_branch, w_out, g_ffn,
           w_group, b_group, w_router, b_router, w1, w3, w2, g_final):
    assert g_mix.shape[0] == 1, "single-layer stack"
    return _layer(x, mem, g_mix[0], g_mem[0], w_in[0], w_gate[0], b_gate[0], lam_re[0], lam_im[0],
                  log_dt[0], b_re[0], b_im[0], c_re[0], c_im[0], d_skip[0], w_glu[0], b_glu[0],
                  g_sgu[0], w_spatial[0], b_spatial[0], w_kv[0], w_branch[0], w_out[0], g_ffn[0],
                  w_group[0], b_group[0], w_router[0], b_router[0], w1[0], w3[0], w2[0], g_final)
```

```python
import functools
import math

import jax
import jax.numpy as jnp
from jax import lax
from jax.experimental import pallas as pl
from jax.experimental.pallas import tpu as pltpu
from jax.experimental.pallas import tpu_sc as plsc

F32 = jnp.float32
BF16 = jnp.bfloat16

EPS = 1e-6
D_MODEL = 1024
SSM_WIDTH = 512
SSM_GROUP = 16
SSM_GROUPS = 32
SSM_STATE = 64
SSM_CHUNK = 16
SGU_WIDTH = 512
SGU_HEADS = 4
SGU_HEAD_DIM = 128
CHUNK = 128
XA_HEADS = 4
XA_HEAD_DIM = 128
N_GROUPS = 8
EXPERTS_PER_GROUP = 8
N_EXPERTS = 64
TOP_K = 2
D_FF = 512
LANES = 128
ROUTE_LANE0 = N_GROUPS

TM_IN = 512
TM_MERGE = 512
MERGE_COLS = 256
TM_OUT = 512
BM = 256
SC_WINDOW = 128
SC_ROW = 256
SC_SPLIT = (D_MODEL // 2) // SC_ROW
SLOT_LANES = 2048
VMEM_LIMIT = 56 * 1024 * 1024


def _rms(x, g):
    return x * lax.rsqrt(jnp.mean(x * x, axis=-1, keepdims=True) + EPS) * g


def _sigmoid(x):
    return 0.5 * (1.0 + jnp.tanh(0.5 * x))


def _gelu(x):
    c = math.sqrt(2.0 / math.pi)
    return 0.5 * x * (1.0 + jnp.tanh(c * (x + 0.044715 * (x * x * x))))


def _dot(a, b):
    return jnp.dot(a, b, preferred_element_type=F32)


def _pack_bf16_pair(x):
    n = x.shape[1] // 2
    lo = lax.bitcast_convert_type(x[:, :n].astype(BF16).astype(F32), jnp.uint32)
    hi = lax.bitcast_convert_type(x[:, n:].astype(BF16).astype(F32), jnp.uint32)
    return hi | (lo >> 16)


def _unpack_bf16_pair(p):
    lo = lax.bitcast_convert_type(p << 16, F32)
    hi = lax.bitcast_convert_type(p & jnp.uint32(0xFFFF0000), F32)
    return jnp.concatenate([lo, hi], axis=1)


GROUPS_PER_TILE = LANES // SSM_GROUP
POS_PER_TILE = LANES // SSM_GROUP


def _slot_masks(rows):
    lane = lax.broadcasted_iota(jnp.int32, (rows, LANES), 1)
    return [(lane >= i * SSM_GROUP) & (lane < (i + 1) * SSM_GROUP) for i in range(LANES // SSM_GROUP)]


def _tokens_to_chunks(tok_ref, out_ref):
    tm = tok_ref.shape[1]
    nc = tm // SSM_CHUNK
    masks = _slot_masks(nc)
    for k in range(SSM_WIDTH // LANES):
        for j in range(SSM_CHUNK // POS_PER_TILE):
            src = [tok_ref[k, pl.ds(j * POS_PER_TILE + p, nc, stride=SSM_CHUNK), :]
                   for p in range(POS_PER_TILE)]
            for gi in range(GROUPS_PER_TILE):
                acc = None
                for p in range(POS_PER_TILE):
                    shift = ((p - gi) * SSM_GROUP) % LANES
                    r = pltpu.roll(src[p], shift, 1) if shift else src[p]
                    acc = r if acc is None else jnp.where(masks[p], r, acc)
                out_ref[k * GROUPS_PER_TILE + gi, :, pl.ds(j * LANES, LANES)] = acc.astype(out_ref.dtype)


def _chunks_to_tokens(chunk_ref, tok_ref):
    tm = tok_ref.shape[1]
    nc = tm // SSM_CHUNK
    masks = _slot_masks(nc)
    for k in range(SSM_WIDTH // LANES):
        for j in range(SSM_CHUNK // POS_PER_TILE):
            src = [chunk_ref[k * GROUPS_PER_TILE + gi, :, pl.ds(j * LANES, LANES)].astype(F32)
                   for gi in range(GROUPS_PER_TILE)]
            for p in range(POS_PER_TILE):
                acc = None
                for gi in range(GROUPS_PER_TILE):
                    shift = ((gi - p) * SSM_GROUP) % LANES
                    r = pltpu.roll(src[gi], shift, 1) if shift else src[gi]
                    acc = r if acc is None else jnp.where(masks[gi], r, acc)
                tok_ref[k, pl.ds(j * POS_PER_TILE + p, nc, stride=SSM_CHUNK), :] = acc


def _kv_body(mem_ref, g_ref, w_ref, k_ref, v_ref):
    n = _rms(mem_ref[0], g_ref[...]).astype(BF16)
    kv = _dot(n, w_ref[...])
    k_ref[0] = kv[:, :SGU_WIDTH].astype(BF16)
    v_ref[0] = kv[:, SGU_WIDTH:].astype(BF16)


def _kv_proj(mem, g_mem, w_kv):
    b, m, d = mem.shape
    return pl.pallas_call(
        _kv_body,
        grid=(b,),
        in_specs=[pl.BlockSpec((1, m, d), lambda i: (i, 0, 0)),
                  pl.BlockSpec((1, d), lambda i: (0, 0)),
                  pl.BlockSpec((d, 2 * SGU_WIDTH), lambda i: (0, 0))],
        out_specs=[pl.BlockSpec((1, m, SGU_WIDTH), lambda i: (i, 0, 0)),
                   pl.BlockSpec((1, m, SGU_WIDTH), lambda i: (i, 0, 0))],
        out_shape=[jax.ShapeDtypeStruct((b, m, SGU_WIDTH), BF16),
                   jax.ShapeDtypeStruct((b, m, SGU_WIDTH), BF16)],
        compiler_params=pltpu.CompilerParams(dimension_semantics=("arbitrary",),
                                             vmem_limit_bytes=VMEM_LIMIT),
        name="kv_proj",
    )(mem, g_mem, w_kv)


def _in_body(x_ref, gmix_ref, win_ref, gsgu_ref, wsp_ref, bsp_ref, k_ref, v_ref,
             u2_ref, yb_ref, yc_ref, tok_ref):
    n = _rms(x_ref[0], gmix_ref[...]).astype(BF16)
    proj = _dot(n, win_ref[...])
    for k in range(SSM_WIDTH // LANES):
        tok_ref[k] = proj[:, k * LANES:(k + 1) * LANES]
    _tokens_to_chunks(tok_ref, u2_ref)

    uv = _gelu(proj[:, SSM_WIDTH:SSM_WIDTH + 2 * SGU_WIDTH])
    u = uv[:, :SGU_WIDTH]
    v = _rms(uv[:, SGU_WIDTH:], gsgu_ref[...]).astype(BF16)
    tm = u.shape[0]
    rows = []
    for c in range(tm // CHUNK):
        vc = v[c * CHUNK:(c + 1) * CHUNK]
        heads = []
        for h in range(SGU_HEADS):
            sl = slice(h * SGU_HEAD_DIM, (h + 1) * SGU_HEAD_DIM)
            heads.append(_dot(wsp_ref[h], vc[:, sl]) + bsp_ref[h])
        rows.append(jnp.concatenate(heads, axis=1))
    sv = jnp.concatenate(rows, axis=0)
    yb_ref[0] = (u * sv).astype(BF16)

    q = proj[:, SSM_WIDTH + 2 * SGU_WIDTH:].astype(BF16)
    kk = k_ref[0]
    vv = v_ref[0]
    outs = []
    for h in range(XA_HEADS):
        sl = slice(h * XA_HEAD_DIM, (h + 1) * XA_HEAD_DIM)
        s = lax.dot_general(q[:, sl], kk[:, sl], (((1,), (1,)), ((), ())),
                            preferred_element_type=F32) * (XA_HEAD_DIM ** -0.5)
        e = jnp.exp(s - jnp.max(s, axis=-1, keepdims=True))
        l = jnp.sum(e, axis=-1, keepdims=True)
        outs.append(_dot(e.astype(BF16), vv[:, sl]) / l)
    yc_ref[0] = jnp.concatenate(outs, axis=1).astype(BF16)


def _in_proj(x, g_mix, w_in, g_sgu, w_sp, b_sp, k, v):
    b, s, d = x.shape
    m = k.shape[1]
    const2 = lambda i, j: (0, 0)
    const3 = lambda i, j: (0, 0, 0)
    tok = lambda i, j: (i, j, 0)
    per_b = lambda i, j: (i, 0, 0)
    out = jax.ShapeDtypeStruct((b, s, SSM_WIDTH), BF16)
    nc = TM_IN // SSM_CHUNK
    tiles = s // TM_IN
    u2 = jax.ShapeDtypeStruct((SSM_GROUPS, b * s // SSM_CHUNK, SSM_CHUNK * SSM_GROUP), BF16)
    return pl.pallas_call(
        _in_body,
        grid=(b, s // TM_IN),
        in_specs=[pl.BlockSpec((1, TM_IN, d), tok),
                  pl.BlockSpec((1, d), const2),
                  pl.BlockSpec(w_in.shape, const2),
                  pl.BlockSpec((1, SGU_WIDTH), const2),
                  pl.BlockSpec(w_sp.shape, const3),
                  pl.BlockSpec(b_sp.shape, const3),
                  pl.BlockSpec((1, m, SGU_WIDTH), per_b),
                  pl.BlockSpec((1, m, SGU_WIDTH), per_b)],
        out_specs=[pl.BlockSpec((SSM_GROUPS, nc, SSM_CHUNK * SSM_GROUP), lambda i, j: (0, i * tiles + j, 0)),
                   pl.BlockSpec((1, TM_IN, SSM_WIDTH), tok),
                   pl.BlockSpec((1, TM_IN, SSM_WIDTH), tok)],
        out_shape=[u2, out, out],
        scratch_shapes=[pltpu.VMEM((SSM_WIDTH // LANES, TM_IN, LANES), F32)],
        compiler_params=pltpu.CompilerParams(dimension_semantics=("arbitrary", "arbitrary"),
                                             vmem_limit_bytes=VMEM_LIMIT),
        name="in_proj",
    )(x, g_mix, w_in, g_sgu, w_sp, b_sp, k, v)


def _ssm_params(lam_re, lam_im, log_dt, b_re, b_im, c_re, c_im, d_skip):
    g, p = lam_re.shape
    dt = jnp.exp(log_dt)[:, None]
    ar = lam_re * dt
    ai = lam_im * dt

    def cpow(j):
        mag = jnp.exp(ar[:, None, :] * j[None, :, None])
        ph = ai[:, None, :] * j[None, :, None]
        return mag * jnp.cos(ph), mag * jnp.sin(ph)

    pr, pi = cpow(jnp.arange(SSM_CHUNK + 1, dtype=F32))
    nr = pr[:, 1] - 1.0
    ni = pi[:, 1]
    den = lam_re * lam_re + lam_im * lam_im
    fr = (nr * lam_re + ni * lam_im) / den
    fi = (ni * lam_re - nr * lam_im) / den
    bbr = fr[..., None] * b_re - fi[..., None] * b_im
    bbi = fr[..., None] * b_im + fi[..., None] * b_re

    L = SSM_CHUNK
    cr = c_re[:, None]
    ci = c_im[:, None]
    cpr = cr * pr[:, :L, None, :] - ci * pi[:, :L, None, :]
    cpi = cr * pi[:, :L, None, :] + ci * pr[:, :L, None, :]
    ccat = jnp.concatenate([cpr, -cpi], axis=-1)
    ccat = ccat.transpose(0, 3, 1, 2).reshape(g, 2 * p, L * SSM_GROUP)
    bcat = jnp.concatenate([bbr, bbi], axis=1).transpose(0, 2, 1)

    rr = pr[:, L - 1::-1][:, :L]
    ri = pi[:, L - 1::-1][:, :L]
    bbr_t = bbr.transpose(0, 2, 1)[:, None]
    bbi_t = bbi.transpose(0, 2, 1)[:, None]
    n_re = rr[:, :, None, :] * bbr_t - ri[:, :, None, :] * bbi_t
    n_im = rr[:, :, None, :] * bbi_t + ri[:, :, None, :] * bbr_t
    nmat = jnp.concatenate([n_re, n_im], axis=-1).reshape(g, L * SSM_GROUP, 2 * p)

    clr = cr * pr[:, 1:, None, :] - ci * pi[:, 1:, None, :]
    cli = cr * pi[:, 1:, None, :] + ci * pr[:, 1:, None, :]
    mmat = jnp.concatenate([clr, -cli], axis=-1)
    mmat = mmat.transpose(0, 3, 1, 2).reshape(g, 2 * p, L * SSM_GROUP)

    qr, qi = cpow(jnp.asarray([float(L * 2 ** k) for k in range(8)], F32))
    lr = jnp.concatenate([qr, qr], axis=-1)
    li = jnp.concatenate([-qi, qi], axis=-1)
    d2 = jnp.tile(d_skip.reshape(g, 1, SSM_GROUP), (1, 1, L))
    return bcat, ccat, nmat.astype(BF16), mmat.astype(BF16), lr, li, d2


def _ssm_body(u_ref, bcat_ref, ccat_ref, n_ref, m_ref, lr_ref, li_ref, d2_ref, y_ref, toep_ref, *, n_seq):
    kern = jnp.dot(bcat_ref[0], ccat_ref[0], precision=lax.Precision.HIGHEST,
                   preferred_element_type=F32)
    col = lax.broadcasted_iota(jnp.int32, kern.shape, 1)
    for s in range(SSM_CHUNK):
        shifted = pltpu.roll(kern, s * SSM_GROUP, 1) if s else kern
        toep_ref[s * SSM_GROUP:(s + 1) * SSM_GROUP, :] = jnp.where(
            col >= s * SSM_GROUP, shifted, 0.0).astype(BF16)

    u = u_ref[0]
    rows = u.shape[0]
    per = rows // n_seq
    y = _dot(u, toep_ref[...])
    st = _dot(u, n_ref[0])
    row = lax.broadcasted_iota(jnp.int32, (per, LANES), 0)
    prev = []
    for b in range(n_seq):
        x = st[b * per:(b + 1) * per]
        k = 0
        while (1 << k) < per:
            d = 1 << k
            sh = jnp.where(row >= d, pltpu.roll(x, d, 0), 0.0)
            x = x + sh * lr_ref[0, k:k + 1, :] + pltpu.roll(sh, SSM_STATE, 1) * li_ref[0, k:k + 1, :]
            k += 1
        prev.append(jnp.where(row >= 1, pltpu.roll(x, 1, 0), 0.0))
    xp = jnp.concatenate(prev, axis=0).astype(BF16)
    y = y + _dot(xp, m_ref[0]) + d2_ref[0] * u.astype(F32)
    y_ref[0] = _gelu(y).astype(BF16)


def _ssm(u2, bcat, ccat, nmat, mmat, lr, li, d2, n_seq):
    g, rows, w = u2.shape
    blk = lambda a: pl.BlockSpec((1,) + a.shape[1:], lambda i: (i, 0, 0))
    return pl.pallas_call(
        functools.partial(_ssm_body, n_seq=n_seq),
        grid=(g,),
        in_specs=[blk(u2), blk(bcat), blk(ccat), blk(nmat), blk(mmat), blk(lr), blk(li), blk(d2)],
        out_specs=blk(u2),
        out_shape=jax.ShapeDtypeStruct(u2.shape, BF16),
        scratch_shapes=[pltpu.VMEM((w, w), BF16)],
        compiler_params=pltpu.CompilerParams(dimension_semantics=("arbitrary",),
                                             vmem_limit_bytes=VMEM_LIMIT),
        name="ssm",
    )(u2, bcat, ccat, nmat, mmat, lr, li, d2)


def _merge_body(x_ref, y2_ref, yb_ref, yc_ref, gmix_ref, wgate_ref, bgate_ref, wglu_ref, bglu_ref,
                wbr_ref, wout_ref, gffn_ref, wrt_ref, brt_ref,
                h_ref, xnp_ref, rt_ref, rtt_ref, cnt_ref, carry_ref, tok_ref):
    i = pl.program_id(0)

    @pl.when(i == 0)
    def _():
        carry_ref[...] = jnp.zeros_like(carry_ref)

    _chunks_to_tokens(y2_ref, tok_ref)
    x = x_ref[...]
    n = _rms(x, gmix_ref[...]).astype(BF16)
    ys = jnp.concatenate([tok_ref[k] for k in range(SSM_WIDTH // LANES)], axis=1).astype(BF16)
    glu = _dot(ys, wglu_ref[...]) + bglu_ref[...]
    ya = (glu[:, :SSM_WIDTH] * _sigmoid(glu[:, SSM_WIDTH:])).astype(BF16)
    branches = (ya, yb_ref[...], yc_ref[...])
    merged = []
    for c in range(D_MODEL // MERGE_COLS):
        acc = None
        for b, yb in enumerate(branches):
            cols = pl.ds(b * D_MODEL + c * MERGE_COLS, MERGE_COLS)
            gate = _sigmoid(_dot(n, wgate_ref[:, cols]) + bgate_ref[:, cols])
            term = gate * _dot(yb, wbr_ref[b, :, pl.ds(c * MERGE_COLS, MERGE_COLS)])
            acc = term if acc is None else acc + term
        merged.append(acc.astype(BF16))
    h = x + _dot(jnp.concatenate(merged, axis=1), wout_ref[...])
    h_ref[...] = h

    xn = _rms(h, gffn_ref[...])
    packed = _pack_bf16_pair(xn)
    for j in range(SC_SPLIT):
        xnp_ref[j] = packed[:, j * SC_ROW:(j + 1) * SC_ROW]

    x_hi = xn.astype(BF16)
    x_lo = (xn - x_hi.astype(F32)).astype(BF16)
    head = _dot(x_hi, wrt_ref[...])
    logits = (head[:, :LANES] + head[:, LANES:] + _dot(x_lo, wrt_ref[:, :LANES])) + brt_ref[...]
    tm = logits.shape[0]
    lane_i = lax.broadcasted_iota(jnp.int32, (tm, LANES), 1)
    lane = lane_i.astype(F32)
    neg = jnp.float32(-3.0e38)
    big = jnp.float32(LANES)
    gmask = lane_i < N_GROUPS
    gl = jnp.where(gmask, logits, neg)
    gmax = jnp.max(gl, axis=-1, keepdims=True)
    gidx = jnp.min(jnp.where(gl == gmax, lane, big), axis=-1, keepdims=True)
    gsum = jnp.sum(jnp.where(gmask, jnp.exp(gl - gmax), 0.0), axis=-1, keepdims=True)
    g_w = 1.0 / gsum
    e_lane = lane_i - ROUTE_LANE0
    lane_group = (e_lane >> 3).astype(F32)
    emask = (e_lane >= 0) & (e_lane < N_EXPERTS) & (lane_group == gidx)
    el = jnp.where(emask, logits, neg)
    m1 = jnp.max(el, axis=-1, keepdims=True)
    i1 = jnp.min(jnp.where(el == m1, lane, big), axis=-1, keepdims=True)
    el2 = jnp.where(lane == i1, neg, el)
    m2 = jnp.max(el2, axis=-1, keepdims=True)
    i2 = jnp.min(jnp.where(el2 == m2, lane, big), axis=-1, keepdims=True)
    t = jnp.exp(m2 - m1)
    w1 = g_w / (1.0 + t)
    w2 = g_w * t / (1.0 + t)

    sel1 = lane == i1
    sel2 = lane == i2
    onehot = jnp.where(sel1 | sel2, 1.0, 0.0)
    r_i = lax.broadcasted_iota(jnp.int32, (tm, tm), 0)
    c_i = lax.broadcasted_iota(jnp.int32, (tm, tm), 1)
    stril = jnp.where(c_i < r_i, 1.0, 0.0).astype(BF16)
    cum = _dot(stril, onehot.astype(BF16)) + carry_ref[0:1, :]
    rank1 = jnp.sum(jnp.where(sel1, cum, 0.0), axis=-1, keepdims=True)
    rank2 = jnp.sum(jnp.where(sel2, cum, 0.0), axis=-1, keepdims=True)
    carry_ref[...] = carry_ref[...] + jnp.sum(onehot, axis=0, keepdims=True)
    cnt_ref[...] = carry_ref[...]

    cols = (i1 - ROUTE_LANE0, i2 - ROUTE_LANE0, rank1, rank2, w1, w2)
    rt = jnp.zeros((tm, LANES), F32)
    for c, val in enumerate(cols):
        rt = jnp.where(lane_i == c, val, rt)
    rt_ref[...] = rt
    rtt_ref[...] = rt.T[:8]


def _merge_route(x, y2, yb, yc, g_mix, w_gate, b_gate, w_glu, b_glu, w_br, w_out, g_ffn, w_rt, b_rt):
    t, d = x.shape
    tm = TM_MERGE
    tok = lambda i: (i, 0)
    c2 = lambda i: (0, 0)
    c3 = lambda i: (0, 0, 0)
    full = lambda a: pl.BlockSpec(a.shape, c2 if a.ndim == 2 else c3)
    return pl.pallas_call(
        _merge_body,
        grid=(t // tm,),
        in_specs=[pl.BlockSpec((tm, d), tok),
                  pl.BlockSpec((SSM_GROUPS, tm // SSM_CHUNK, SSM_CHUNK * SSM_GROUP), lambda i: (0, i, 0)),
                  pl.BlockSpec((tm, SSM_WIDTH), tok),
                  pl.BlockSpec((tm, SSM_WIDTH), tok),
                  full(g_mix), full(w_gate), full(b_gate), full(w_glu), full(b_glu),
                  full(w_br), full(w_out), full(g_ffn), full(w_rt), full(b_rt)],
        out_specs=[pl.BlockSpec((tm, d), tok),
                   pl.BlockSpec((SC_SPLIT, tm, SC_ROW), lambda i: (0, i, 0)),
                   pl.BlockSpec((tm, LANES), tok),
                   pl.BlockSpec((8, tm), lambda i: (0, i)),
                   pl.BlockSpec((8, LANES), c2)],
        out_shape=[jax.ShapeDtypeStruct((t, d), F32),
                   jax.ShapeDtypeStruct((SC_SPLIT, t, SC_ROW), jnp.uint32),
                   jax.ShapeDtypeStruct((t, LANES), F32),
                   jax.ShapeDtypeStruct((8, t), F32),
                   jax.ShapeDtypeStruct((8, LANES), F32)],
        scratch_shapes=[pltpu.VMEM((8, LANES), F32),
                        pltpu.VMEM((SSM_WIDTH // LANES, tm, LANES), F32)],
        compiler_params=pltpu.CompilerParams(dimension_semantics=("arbitrary",),
                                             vmem_limit_bytes=VMEM_LIMIT),
        name="merge_route",
    )(x, y2, yb, yc, g_mix, w_gate, b_gate, w_glu, b_glu, w_br, w_out, g_ffn, w_rt, b_rt)


def _slot_body(rtt_ref, pstart_ref, out_ref, *, nslots):
    tl = rtt_ref.shape[1]
    expert = lax.broadcasted_iota(jnp.int32, (N_EXPERTS, tl), 0).astype(F32)
    for k in range(TOP_K):
        onehot = rtt_ref[k:k + 1, :] == expert
        start = jnp.sum(jnp.where(onehot, pstart_ref[...], 0.0), axis=0, keepdims=True)
        slot = (start + rtt_ref[TOP_K + k:TOP_K + k + 1, :]).astype(jnp.int32)
        for j in range(SC_SPLIT):
            out_ref[k * SC_SPLIT + j:k * SC_SPLIT + j + 1, :] = slot + j * nslots


def _slot_rows(rtt, pstart, nslots):
    t = rtt.shape[1]
    tl = SLOT_LANES
    return pl.pallas_call(
        functools.partial(_slot_body, nslots=nslots),
        grid=(t // tl,),
        in_specs=[pl.BlockSpec((8, tl), lambda i: (0, i)),
                  pl.BlockSpec((N_EXPERTS, 1), lambda i: (0, 0))],
        out_specs=pl.BlockSpec((TOP_K * SC_SPLIT, tl), lambda i: (0, i)),
        out_shape=jax.ShapeDtypeStruct((TOP_K * SC_SPLIT, t), jnp.int32),
        compiler_params=pltpu.CompilerParams(dimension_semantics=("arbitrary",)),
        name="slot_rows",
    )(rtt, pstart.astype(F32).reshape(N_EXPERTS, 1))


def _expert_body(meta_ref, buf_ref, w1_hbm, w3_hbm, w2_hbm, out_ref, w1_buf, w3_buf, w2_buf, sem):
    i = pl.program_id(0)
    expert = meta_ref[0, i]
    valid = meta_ref[1, i]
    slot = meta_ref[2, i]
    nxt = meta_ref[4, i]

    def weight_copies(e, s):
        return (pltpu.make_async_copy(w1_hbm.at[e], w1_buf.at[s], sem.at[s, 0]),
                pltpu.make_async_copy(w3_hbm.at[e], w3_buf.at[s], sem.at[s, 1]),
                pltpu.make_async_copy(w2_hbm.at[e], w2_buf.at[s], sem.at[s, 2]))

    @pl.when(i == 0)
    def _():
        for c in weight_copies(expert, slot):
            c.start()

    @pl.when(meta_ref[3, i] == 1)
    def _():
        for c in weight_copies(expert, slot):
            c.wait()

        @pl.when(nxt >= 0)
        def _():
            for c in weight_copies(nxt, 1 - slot):
                c.start()

    @pl.when(valid > 0)
    def _():
        x = _unpack_bf16_pair(jnp.concatenate([buf_ref[j] for j in range(SC_SPLIT)], axis=1))
        row = lax.broadcasted_iota(jnp.int32, x.shape, 0)
        x = jnp.where(row < valid, x, 0.0).astype(BF16)
        h1 = _dot(x, w1_buf[slot].astype(BF16))
        h3 = _dot(x, w3_buf[slot].astype(BF16))
        a = (h1 * _sigmoid(h1) * h3).astype(BF16)
        packed = _pack_bf16_pair(_dot(a, w2_buf[slot].astype(BF16)))
        for j in range(SC_SPLIT):
            out_ref[j] = packed[:, j * SC_ROW:(j + 1) * SC_ROW]

    @pl.when(valid <= 0)
    def _():
        out_ref[...] = jnp.zeros_like(out_ref)


def _expert_meta(block_expert, block_valid):
    nb = block_expert.shape[0]
    pos = jnp.arange(nb, dtype=jnp.int32)
    prev = jnp.concatenate([jnp.full((1,), -1, jnp.int32), block_expert[:-1]])
    first = (block_valid > 0) & (block_expert != prev)
    run = jnp.cumsum(first.astype(jnp.int32)) - 1
    start_pos = jnp.where(first, pos, nb)
    later = jnp.concatenate([start_pos[1:], jnp.full((1,), nb, jnp.int32)])
    nearest = lax.cummin(later, axis=0, reverse=True)
    nxt = jnp.where(nearest < nb, block_expert[jnp.minimum(nearest, nb - 1)], -1)
    return jnp.stack([block_expert, block_valid, run % 2, first.astype(jnp.int32), nxt]).astype(jnp.int32)


def _experts(block_expert, block_valid, buf, w1, w3, w2):
    _, nslots, _ = buf.shape
    nb = nslots // BM
    rows = pl.BlockSpec((SC_SPLIT, BM, SC_ROW), lambda i, meta: (0, i, 0))
    hbm = pl.BlockSpec(memory_space=pl.ANY)
    grid_spec = pltpu.PrefetchScalarGridSpec(
        num_scalar_prefetch=1,
        grid=(nb,),
        in_specs=[rows, hbm, hbm, hbm],
        out_specs=rows,
        scratch_shapes=[pltpu.VMEM((2,) + w1.shape[1:], w1.dtype),
                        pltpu.VMEM((2,) + w3.shape[1:], w3.dtype),
                        pltpu.VMEM((2,) + w2.shape[1:], w2.dtype),
                        pltpu.SemaphoreType.DMA((2, 3))],
    )
    return pl.pallas_call(
        _expert_body,
        grid_spec=grid_spec,
        out_shape=jax.ShapeDtypeStruct(buf.shape, jnp.uint32),
        compiler_params=pltpu.CompilerParams(dimension_semantics=("arbitrary",),
                                             vmem_limit_bytes=VMEM_LIMIT),
        name="experts",
    )(_expert_meta(block_expert, block_valid), buf, w1, w3, w2)


def _sc_mesh():
    return plsc.VectorSubcoreMesh(core_axis_name="core", subcore_axis_name="subcore")


def _dispatch_rows(rows, dest0, dest1, nslots):
    t, w = rows.shape
    win = SC_WINDOW
    idx_spec = pl.BlockSpec((1, win), lambda i: (0, i))

    @functools.partial(pl.kernel, mesh=_sc_mesh(), scratch_types=[],
                       out_type=jax.ShapeDtypeStruct((nslots, w), rows.dtype), name="dispatch_rows")
    def run(rows_hbm, i0_hbm, i1_hbm, out_hbm):
        def body(rows_vmem, i0_vmem, i1_vmem):
            pltpu.sync_copy(rows_vmem, out_hbm.at[i0_vmem.at[0]])
            pltpu.sync_copy(rows_vmem, out_hbm.at[i1_vmem.at[0]])

        pltpu.emit_pipeline(
            body, grid=(t // win,),
            in_specs=[pl.BlockSpec((win, w), lambda i: (i, 0)), idx_spec, idx_spec],
            out_specs=[],
            core_axis_name=("core", "subcore"),
            dimension_semantics=(pltpu.PARALLEL,),
        )(rows_hbm, i0_hbm, i1_hbm)

    return run(rows, dest0.reshape(1, t), dest1.reshape(1, t))


def _gather_rows(table, idx):
    n = idx.shape[0]
    w = table.shape[1]
    win = SC_WINDOW

    @functools.partial(pl.kernel, mesh=_sc_mesh(), scratch_types=[],
                       out_type=jax.ShapeDtypeStruct((n, w), table.dtype), name="gather_rows")
    def run(table_hbm, i_hbm, out_hbm):
        def body(i_vmem, out_vmem):
            pltpu.sync_copy(table_hbm.at[i_vmem.at[0]], out_vmem)

        pltpu.emit_pipeline(
            body, grid=(n // win,),
            in_specs=[pl.BlockSpec((1, win), lambda i: (0, i))],
            out_specs=[pl.BlockSpec((win, w), lambda i: (i, 0))],
            core_axis_name=("core", "subcore"),
            dimension_semantics=(pltpu.PARALLEL,),
        )(i_hbm, out_hbm)

    return run(table, idx.reshape(1, n))


def _combine_body(h_ref, g_ref, rt_ref, gfin_ref, out_ref):
    rt = rt_ref[...]
    y = h_ref[...]
    for k in range(TOP_K):
        rows = jnp.concatenate([g_ref[k * SC_SPLIT + j] for j in range(SC_SPLIT)], axis=1)
        y = y + rt[:, 4 + k:5 + k] * _unpack_bf16_pair(rows)
    out_ref[...] = _rms(y, gfin_ref[...])


def _combine(h, g, rt, g_final):
    t, d = h.shape
    tm = TM_OUT
    tok = lambda i: (i, 0)
    return pl.pallas_call(
        _combine_body,
        grid=(t // tm,),
        in_specs=[pl.BlockSpec((tm, d), tok),
                  pl.BlockSpec((TOP_K * SC_SPLIT, tm, SC_ROW), lambda i: (0, i, 0)),
                  pl.BlockSpec((tm, LANES), tok),
                  pl.BlockSpec((1, d), lambda i: (0, 0))],
        out_specs=pl.BlockSpec((tm, d), tok),
        out_shape=jax.ShapeDtypeStruct((t, d), F32),
        compiler_params=pltpu.CompilerParams(dimension_semantics=("arbitrary",),
                                             vmem_limit_bytes=VMEM_LIMIT),
        name="combine",
    )(h, g, rt, g_final)


def _layer(h, mem, g_mix, g_mem, w_in, w_gate, b_gate, lam_re, lam_im, log_dt, b_re, b_im,
           c_re, c_im, d_skip, w_glu, b_glu, g_sgu, w_spatial, b_spatial, w_kv, w_branch,
           w_out, g_ffn, w_group, b_group, w_router, b_router, w1, w3, w2, g_out):
    bsz, s, d = h.shape
    t = bsz * s
    row = lambda a: a.reshape(1, -1)

    k, v = _kv_proj(mem, row(g_mem), w_kv.astype(BF16))

    tril = jnp.tril(jnp.ones((CHUNK, CHUNK), dtype=bool))
    w_sp = jnp.where(tril, w_spatial, 0.0).astype(BF16)
    b_sp = jnp.broadcast_to(b_spatial[:, :, None], (SGU_HEADS, CHUNK, SGU_HEAD_DIM))
    u2, y_b, y_c = _in_proj(h, row(g_mix), w_in.astype(BF16), row(g_sgu), w_sp, b_sp, k, v)
    y2 = _ssm(u2, *_ssm_params(lam_re, lam_im, log_dt, b_re, b_im, c_re, c_im, d_skip), n_seq=bsz)

    pad = LANES - N_GROUPS - N_EXPERTS
    w_rt = jnp.concatenate([w_group, w_router, jnp.zeros((d, pad), F32)], axis=1)
    w_rt_hi = w_rt.astype(BF16)
    w_rt = jnp.concatenate([w_rt_hi, (w_rt - w_rt_hi.astype(F32)).astype(BF16)], axis=1)
    b_rt =jnp.concatenate([b_group, b_router, jnp.zeros((pad,), F32)]).reshape(1, LANES)
    h2, xnp, rt, rtt, cnt = _merge_route(
        h.reshape(t, d), y2, y_b.reshape(t, -1), y_c.reshape(t, -1), row(g_mix),
        w_gate.astype(BF16), row(b_gate), w_glu.astype(BF16), row(b_glu),
        w_branch.astype(BF16), w_out.astype(BF16), row(g_ffn), w_rt, b_rt)

    counts = cnt[0, ROUTE_LANE0:ROUTE_LANE0 + N_EXPERTS].astype(jnp.int32)
    padded = (counts + BM - 1) // BM * BM
    block_end = jnp.cumsum(padded)
    pstart = block_end - padded
    nb = (t * TOP_K) // BM + N_EXPERTS
    blk_row0 = jnp.arange(nb, dtype=jnp.int32) * BM
    block_expert = jnp.sum((block_end[None, :] <= blk_row0[:, None]).astype(jnp.int32), axis=1)
    block_expert = jnp.minimum(block_expert, N_EXPERTS - 1)
    block_valid = jnp.clip(counts[block_expert] - (blk_row0 - pstart[block_expert]), 0, BM)
    block_valid = jnp.where(blk_row0 < block_end[-1], block_valid, 0).astype(jnp.int32)
    nslots = nb * BM
    dest_p = _slot_rows(rtt, pstart, nslots).reshape(TOP_K, SC_SPLIT * t)
    buf = _dispatch_rows(xnp.reshape(SC_SPLIT * t, SC_ROW), dest_p[0], dest_p[1], SC_SPLIT * nslots)
    yb = _experts(block_expert, block_valid, buf.reshape(SC_SPLIT, nslots, SC_ROW), w1, w3, w2)
    g = _gather_rows(yb.reshape(SC_SPLIT * nslots, SC_ROW), dest_p.reshape(-1))
    out = _combine(h2, g.reshape(TOP_K * SC_SPLIT, t, SC_ROW), rt, row(g_out))
    return out.reshape(bsz, s, d)


def kernel(x, mem, g_mix, g_mem, w_in, w_gate, b_gate, lam_re, lam_im, log_dt, b_re, b_im, c_re,
           c_im, d_skip, w_glu, b_glu, g_sgu, w_spatial, b_spatial, w_kv, w_branch, w_out, g_ffn,
           w_group, b_group, w_router, b_router, w1, w3, w2, g_final):
    assert g_mix.shape[0] == 1, "single-layer stack"
    return _layer(x, mem, g_mix[0], g_mem[0], w_in[0], w_gate[0], b_gate[0], lam_re[0], lam_im[0],
                  log_dt[0], b_re[0], b_im[0], c_re[0], c_im[0], d_skip[0], w_glu[0], b_glu[0],
                  g_sgu[0], w_spatial[0], b_spatial[0], w_kv[0], w_branch[0], w_out[0], g_ffn[0],
                  w_group[0], b_group[0], w_router[0], b_router[0], w1[0], w3[0], w2[0], g_final)
```

```python
import functools
import math

import jax
import jax.numpy as jnp
from jax import lax
from jax.experimental import pallas as pl
from jax.experimental.pallas import tpu as pltpu
from jax.experimental.pallas import tpu_sc as plsc

F32 = jnp.float32
BF16 = jnp.bfloat16

EPS = 1e-6
D_MODEL = 1024
SSM_WIDTH = 512
SSM_GROUP = 16
SSM_GROUPS = 32
SSM_STATE = 64
SSM_CHUNK = 16
SGU_WIDTH = 512
SGU_HEADS = 4
SGU_HEAD_DIM = 128
CHUNK = 128
XA_HEADS = 4
XA_HEAD_DIM = 128
N_GROUPS = 8
EXPERTS_PER_GROUP = 8
N_EXPERTS = 64
TOP_K = 2
D_FF = 512
LANES = 128
ROUTE_LANE0 = N_GROUPS

TM_IN = 512
TM_MERGE = 512
MERGE_COLS = 256
TM_OUT = 512
BM = 256
WEIGHT_SLOTS = 3
SC_WINDOW = 128
SC_ROW = 256
SC_SPLIT = (D_MODEL // 2) // SC_ROW
SLOT_LANES = 2048
VMEM_LIMIT = 56 * 1024 * 1024


def _rms(x, g):
    return x * lax.rsqrt(jnp.mean(x * x, axis=-1, keepdims=True) + EPS) * g


def _sigmoid(x):
    return 0.5 * (1.0 + jnp.tanh(0.5 * x))


def _gelu(x):
    c = math.sqrt(2.0 / math.pi)
    return 0.5 * x * (1.0 + jnp.tanh(c * (x + 0.044715 * (x * x * x))))


def _dot(a, b):
    return jnp.dot(a, b, preferred_element_type=F32)


def _pack_bf16_pair(x):
    n = x.shape[1] // 2
    lo = lax.bitcast_convert_type(x[:, :n].astype(BF16).astype(F32), jnp.uint32)
    hi = lax.bitcast_convert_type(x[:, n:].astype(BF16).astype(F32), jnp.uint32)
    return hi | (lo >> 16)


def _unpack_bf16_pair(p):
    lo = lax.bitcast_convert_type(p << 16, F32)
    hi = lax.bitcast_convert_type(p & jnp.uint32(0xFFFF0000), F32)
    return jnp.concatenate([lo, hi], axis=1)


GROUPS_PER_TILE = LANES // SSM_GROUP
POS_PER_TILE = LANES // SSM_GROUP


def _slot_masks(rows):
    lane = lax.broadcasted_iota(jnp.int32, (rows, LANES), 1)
    return [(lane >= i * SSM_GROUP) & (lane < (i + 1) * SSM_GROUP) for i in range(LANES // SSM_GROUP)]


def _tokens_to_chunks(tok_ref, out_ref):
    tm = tok_ref.shape[1]
    nc = tm // SSM_CHUNK
    masks = _slot_masks(nc)
    for k in range(SSM_WIDTH // LANES):
        for j in range(SSM_CHUNK // POS_PER_TILE):
            src = [tok_ref[k, pl.ds(j * POS_PER_TILE + p, nc, stride=SSM_CHUNK), :]
                   for p in range(POS_PER_TILE)]
            for gi in range(GROUPS_PER_TILE):
                acc = None
                for p in range(POS_PER_TILE):
                    shift = ((p - gi) * SSM_GROUP) % LANES
                    r = pltpu.roll(src[p], shift, 1) if shift else src[p]
                    acc = r if acc is None else jnp.where(masks[p], r, acc)
                out_ref[k * GROUPS_PER_TILE + gi, :, pl.ds(j * LANES, LANES)] = acc.astype(out_ref.dtype)


def _chunks_to_tokens(chunk_ref, tok_ref):
    tm = tok_ref.shape[1]
    nc = tm // SSM_CHUNK
    masks = _slot_masks(nc)
    for k in range(SSM_WIDTH // LANES):
        for j in range(SSM_CHUNK // POS_PER_TILE):
            src = [chunk_ref[k * GROUPS_PER_TILE + gi, :, pl.ds(j * LANES, LANES)].astype(F32)
                   for gi in range(GROUPS_PER_TILE)]
            for p in range(POS_PER_TILE):
                acc = None
                for gi in range(GROUPS_PER_TILE):
                    shift = ((gi - p) * SSM_GROUP) % LANES
                    r = pltpu.roll(src[gi], shift, 1) if shift else src[gi]
                    acc = r if acc is None else jnp.where(masks[gi], r, acc)
                tok_ref[k, pl.ds(j * POS_PER_TILE + p, nc, stride=SSM_CHUNK), :] = acc


def _kv_body(mem_ref, g_ref, w_ref, k_ref, v_ref):
    n = _rms(mem_ref[0], g_ref[...]).astype(BF16)
    kv = _dot(n, w_ref[...])
    k_ref[0] = kv[:, :SGU_WIDTH].astype(BF16)
    v_ref[0] = kv[:, SGU_WIDTH:].astype(BF16)


def _kv_proj(mem, g_mem, w_kv):
    b, m, d = mem.shape
    return pl.pallas_call(
        _kv_body,
        grid=(b,),
        in_specs=[pl.BlockSpec((1, m, d), lambda i: (i, 0, 0)),
                  pl.BlockSpec((1, d), lambda i: (0, 0)),
                  pl.BlockSpec((d, 2 * SGU_WIDTH), lambda i: (0, 0))],
        out_specs=[pl.BlockSpec((1, m, SGU_WIDTH), lambda i: (i, 0, 0)),
                   pl.BlockSpec((1, m, SGU_WIDTH), lambda i: (i, 0, 0))],
        out_shape=[jax.ShapeDtypeStruct((b, m, SGU_WIDTH), BF16),
                   jax.ShapeDtypeStruct((b, m, SGU_WIDTH), BF16)],
        compiler_params=pltpu.CompilerParams(dimension_semantics=("arbitrary",),
                                             vmem_limit_bytes=VMEM_LIMIT),
        name="kv_proj",
    )(mem, g_mem, w_kv)


def _in_body(x_ref, gmix_ref, win_ref, gsgu_ref, wsp_ref, bsp_ref, k_ref, v_ref,
             u2_ref, yb_ref, yc_ref, tok_ref):
    n = _rms(x_ref[0], gmix_ref[...]).astype(BF16)
    proj = _dot(n, win_ref[...])
    for k in range(SSM_WIDTH // LANES):
        tok_ref[k] = proj[:, k * LANES:(k + 1) * LANES]
    _tokens_to_chunks(tok_ref, u2_ref)

    uv = _gelu(proj[:, SSM_WIDTH:SSM_WIDTH + 2 * SGU_WIDTH])
    u = uv[:, :SGU_WIDTH]
    v = _rms(uv[:, SGU_WIDTH:], gsgu_ref[...]).astype(BF16)
    tm = u.shape[0]
    rows = []
    for c in range(tm // CHUNK):
        vc = v[c * CHUNK:(c + 1) * CHUNK]
        heads = []
        for h in range(SGU_HEADS):
            sl = slice(h * SGU_HEAD_DIM, (h + 1) * SGU_HEAD_DIM)
            heads.append(_dot(wsp_ref[h], vc[:, sl]) + bsp_ref[h])
        rows.append(jnp.concatenate(heads, axis=1))
    sv = jnp.concatenate(rows, axis=0)
    yb_ref[0] = (u * sv).astype(BF16)

    q = proj[:, SSM_WIDTH + 2 * SGU_WIDTH:].astype(BF16)
    kk = k_ref[0]
    vv = v_ref[0]
    outs = []
    for h in range(XA_HEADS):
        sl = slice(h * XA_HEAD_DIM, (h + 1) * XA_HEAD_DIM)
        s = lax.dot_general(q[:, sl], kk[:, sl], (((1,), (1,)), ((), ())),
                            preferred_element_type=F32) * (XA_HEAD_DIM ** -0.5)
        e = jnp.exp(s - jnp.max(s, axis=-1, keepdims=True))
        l = jnp.sum(e, axis=-1, keepdims=True)
        outs.append(_dot(e.astype(BF16), vv[:, sl]) / l)
    yc_ref[0] = jnp.concatenate(outs, axis=1).astype(BF16)


def _in_proj(x, g_mix, w_in, g_sgu, w_sp, b_sp, k, v):
    b, s, d = x.shape
    m = k.shape[1]
    const2 = lambda i, j: (0, 0)
    const3 = lambda i, j: (0, 0, 0)
    tok = lambda i, j: (i, j, 0)
    per_b = lambda i, j: (i, 0, 0)
    out = jax.ShapeDtypeStruct((b, s, SSM_WIDTH), BF16)
    nc = TM_IN // SSM_CHUNK
    tiles = s // TM_IN
    u2 = jax.ShapeDtypeStruct((SSM_GROUPS, b * s // SSM_CHUNK, SSM_CHUNK * SSM_GROUP), BF16)
    return pl.pallas_call(
        _in_body,
        grid=(b, s // TM_IN),
        in_specs=[pl.BlockSpec((1, TM_IN, d), tok),
                  pl.BlockSpec((1, d), const2),
                  pl.BlockSpec(w_in.shape, const2),
                  pl.BlockSpec((1, SGU_WIDTH), const2),
                  pl.BlockSpec(w_sp.shape, const3),
                  pl.BlockSpec(b_sp.shape, const3),
                  pl.BlockSpec((1, m, SGU_WIDTH), per_b),
                  pl.BlockSpec((1, m, SGU_WIDTH), per_b)],
        out_specs=[pl.BlockSpec((SSM_GROUPS, nc, SSM_CHUNK * SSM_GROUP), lambda i, j: (0, i * tiles + j, 0)),
                   pl.BlockSpec((1, TM_IN, SSM_WIDTH), tok),
                   pl.BlockSpec((1, TM_IN, SSM_WIDTH), tok)],
        out_shape=[u2, out, out],
        scratch_shapes=[pltpu.VMEM((SSM_WIDTH // LANES, TM_IN, LANES), F32)],
        compiler_params=pltpu.CompilerParams(dimension_semantics=("arbitrary", "arbitrary"),
                                             vmem_limit_bytes=VMEM_LIMIT),
        name="in_proj",
    )(x, g_mix, w_in, g_sgu, w_sp, b_sp, k, v)


def _ssm_params(lam_re, lam_im, log_dt, b_re, b_im, c_re, c_im, d_skip):
    g, p = lam_re.shape
    dt = jnp.exp(log_dt)[:, None]
    ar = lam_re * dt
    ai = lam_im * dt

    def cpow(j):
        mag = jnp.exp(ar[:, None, :] * j[None, :, None])
        ph = ai[:, None, :] * j[None, :, None]
        return mag * jnp.cos(ph), mag * jnp.sin(ph)

    pr, pi = cpow(jnp.arange(SSM_CHUNK + 1, dtype=F32))
    nr = pr[:, 1] - 1.0
    ni = pi[:, 1]
    den = lam_re * lam_re + lam_im * lam_im
    fr = (nr * lam_re + ni * lam_im) / den
    fi = (ni * lam_re - nr * lam_im) / den
    bbr = fr[..., None] * b_re - fi[..., None] * b_im
    bbi = fr[..., None] * b_im + fi[..., None] * b_re

    L = SSM_CHUNK
    cr = c_re[:, None]
    ci = c_im[:, None]
    cpr = cr * pr[:, :L, None, :] - ci * pi[:, :L, None, :]
    cpi = cr * pi[:, :L, None, :] + ci * pr[:, :L, None, :]
    ccat = jnp.concatenate([cpr, -cpi], axis=-1)
    ccat = ccat.transpose(0, 3, 1, 2).reshape(g, 2 * p, L * SSM_GROUP)
    bcat = jnp.concatenate([bbr, bbi], axis=1).transpose(0, 2, 1)

    rr = pr[:, L - 1::-1][:, :L]
    ri = pi[:, L - 1::-1][:, :L]
    bbr_t = bbr.transpose(0, 2, 1)[:, None]
    bbi_t = bbi.transpose(0, 2, 1)[:, None]
    n_re = rr[:, :, None, :] * bbr_t - ri[:, :, None, :] * bbi_t
    n_im = rr[:, :, None, :] * bbi_t + ri[:, :, None, :] * bbr_t
    nmat = jnp.concatenate([n_re, n_im], axis=-1).reshape(g, L * SSM_GROUP, 2 * p)

    clr = cr * pr[:, 1:, None, :] - ci * pi[:, 1:, None, :]
    cli = cr * pi[:, 1:, None, :] + ci * pr[:, 1:, None, :]
    mmat = jnp.concatenate([clr, -cli], axis=-1)
    mmat = mmat.transpose(0, 3, 1, 2).reshape(g, 2 * p, L * SSM_GROUP)

    qr, qi = cpow(jnp.asarray([float(L * 2 ** k) for k in range(8)], F32))
    lr = jnp.concatenate([qr, qr], axis=-1)
    li = jnp.concatenate([-qi, qi], axis=-1)
    d2 = jnp.tile(d_skip.reshape(g, 1, SSM_GROUP), (1, 1, L))
    return bcat, ccat, nmat.astype(BF16), mmat.astype(BF16), lr, li, d2


def _ssm_body(u_ref, bcat_ref, ccat_ref, n_ref, m_ref, lr_ref, li_ref, d2_ref, y_ref, toep_ref, *, n_seq):
    kern = jnp.dot(bcat_ref[0], ccat_ref[0], precision=lax.Precision.HIGHEST,
                   preferred_element_type=F32)
    col = lax.broadcasted_iota(jnp.int32, kern.shape, 1)
    for s in range(SSM_CHUNK):
        shifted = pltpu.roll(kern, s * SSM_GROUP, 1) if s else kern
        toep_ref[s * SSM_GROUP:(s + 1) * SSM_GROUP, :] = jnp.where(
            col >= s * SSM_GROUP, shifted, 0.0).astype(BF16)

    u = u_ref[0]
    rows = u.shape[0]
    per = rows // n_seq
    y = _dot(u, toep_ref[...])
    st = _dot(u, n_ref[0])
    row = lax.broadcasted_iota(jnp.int32, (per, LANES), 0)
    prev = []
    for b in range(n_seq):
        x = st[b * per:(b + 1) * per]
        k = 0
        while (1 << k) < per:
            d = 1 << k
            sh = jnp.where(row >= d, pltpu.roll(x, d, 0), 0.0)
            x = x + sh * lr_ref[0, k:k + 1, :] + pltpu.roll(sh, SSM_STATE, 1) * li_ref[0, k:k + 1, :]
            k += 1
        prev.append(jnp.where(row >= 1, pltpu.roll(x, 1, 0), 0.0))
    xp = jnp.concatenate(prev, axis=0).astype(BF16)
    y = y + _dot(xp, m_ref[0]) + d2_ref[0] * u.astype(F32)
    y_ref[0] = _gelu(y).astype(BF16)


def _ssm(u2, bcat, ccat, nmat, mmat, lr, li, d2, n_seq):
    g, rows, w = u2.shape
    blk = lambda a: pl.BlockSpec((1,) + a.shape[1:], lambda i: (i, 0, 0))
    return pl.pallas_call(
        functools.partial(_ssm_body, n_seq=n_seq),
        grid=(g,),
        in_specs=[blk(u2), blk(bcat), blk(ccat), blk(nmat), blk(mmat), blk(lr), blk(li), blk(d2)],
        out_specs=blk(u2),
        out_shape=jax.ShapeDtypeStruct(u2.shape, BF16),
        scratch_shapes=[pltpu.VMEM((w, w), BF16)],
        compiler_params=pltpu.CompilerParams(dimension_semantics=("arbitrary",),
                                             vmem_limit_bytes=VMEM_LIMIT),
        name="ssm",
    )(u2, bcat, ccat, nmat, mmat, lr, li, d2)


def _merge_body(x_ref, y2_ref, yb_ref, yc_ref, gmix_ref, wgate_ref, bgate_ref, wglu_ref, bglu_ref,
                wbr_ref, wout_ref, gffn_ref, wrt_ref, brt_ref,
                h_ref, xnp_ref, rt_ref, rtt_ref, cnt_ref, carry_ref, tok_ref):
    i = pl.program_id(0)

    @pl.when(i == 0)
    def _():
        carry_ref[...] = jnp.zeros_like(carry_ref)

    _chunks_to_tokens(y2_ref, tok_ref)
    x = x_ref[...]
    n = _rms(x, gmix_ref[...]).astype(BF16)
    ys = jnp.concatenate([tok_ref[k] for k in range(SSM_WIDTH // LANES)], axis=1).astype(BF16)
    glu = _dot(ys, wglu_ref[...]) + bglu_ref[...]
    ya = (glu[:, :SSM_WIDTH] * _sigmoid(glu[:, SSM_WIDTH:])).astype(BF16)
    branches = (ya, yb_ref[...], yc_ref[...])
    merged = []
    for c in range(D_MODEL // MERGE_COLS):
        acc = None
        for b, yb in enumerate(branches):
            cols = pl.ds(b * D_MODEL + c * MERGE_COLS, MERGE_COLS)
            gate = _sigmoid(_dot(n, wgate_ref[:, cols]) + bgate_ref[:, cols])
            term = gate * _dot(yb, wbr_ref[b, :, pl.ds(c * MERGE_COLS, MERGE_COLS)])
            acc = term if acc is None else acc + term
        merged.append(acc.astype(BF16))
    h = x + _dot(jnp.concatenate(merged, axis=1), wout_ref[...])
    h_ref[...] = h

    xn = _rms(h, gffn_ref[...])
    packed = _pack_bf16_pair(xn)
    for j in range(SC_SPLIT):
        xnp_ref[j] = packed[:, j * SC_ROW:(j + 1) * SC_ROW]

    x_hi = xn.astype(BF16)
    x_lo = (xn - x_hi.astype(F32)).astype(BF16)
    head = _dot(x_hi, wrt_ref[...])
    logits = (head[:, :LANES] + head[:, LANES:] + _dot(x_lo, wrt_ref[:, :LANES])) + brt_ref[...]
    tm = logits.shape[0]
    lane_i = lax.broadcasted_iota(jnp.int32, (tm, LANES), 1)
    lane = lane_i.astype(F32)
    neg = jnp.float32(-3.0e38)
    big = jnp.float32(LANES)
    gmask = lane_i < N_GROUPS
    gl = jnp.where(gmask, logits, neg)
    gmax = jnp.max(gl, axis=-1, keepdims=True)
    gidx = jnp.min(jnp.where(gl == gmax, lane, big), axis=-1, keepdims=True)
    gsum = jnp.sum(jnp.where(gmask, jnp.exp(gl - gmax), 0.0), axis=-1, keepdims=True)
    g_w = 1.0 / gsum
    e_lane = lane_i - ROUTE_LANE0
    lane_group = (e_lane >> 3).astype(F32)
    emask = (e_lane >= 0) & (e_lane < N_EXPERTS) & (lane_group == gidx)
    el = jnp.where(emask, logits, neg)
    m1 = jnp.max(el, axis=-1, keepdims=True)
    i1 = jnp.min(jnp.where(el == m1, lane, big), axis=-1, keepdims=True)
    el2 = jnp.where(lane == i1, neg, el)
    m2 = jnp.max(el2, axis=-1, keepdims=True)
    i2 = jnp.min(jnp.where(el2 == m2, lane, big), axis=-1, keepdims=True)
    t = jnp.exp(m2 - m1)
    w1 = g_w / (1.0 + t)
    w2 = g_w * t / (1.0 + t)

    sel1 = lane == i1
    sel2 = lane == i2
    onehot = jnp.where(sel1 | sel2, 1.0, 0.0)
    r_i = lax.broadcasted_iota(jnp.int32, (tm, tm), 0)
    c_i = lax.broadcasted_iota(jnp.int32, (tm, tm), 1)
    stril = jnp.where(c_i < r_i, 1.0, 0.0).astype(BF16)
    cum = _dot(stril, onehot.astype(BF16)) + carry_ref[0:1, :]
    rank1 = jnp.sum(jnp.where(sel1, cum, 0.0), axis=-1, keepdims=True)
    rank2 = jnp.sum(jnp.where(sel2, cum, 0.0), axis=-1, keepdims=True)
    carry_ref[...] = carry_ref[...] + jnp.sum(onehot, axis=0, keepdims=True)
    cnt_ref[...] = carry_ref[...]

    cols = (i1 - ROUTE_LANE0, i2 - ROUTE_LANE0, rank1, rank2, w1, w2)
    rt = jnp.zeros((tm, LANES), F32)
    for c, val in enumerate(cols):
        rt = jnp.where(lane_i == c, val, rt)
    rt_ref[...] = rt
    rtt_ref[...] = rt.T[:8]


def _merge_route(x, y2, yb, yc, g_mix, w_gate, b_gate, w_glu, b_glu, w_br, w_out, g_ffn, w_rt, b_rt):
    t, d = x.shape
    tm = TM_MERGE
    tok = lambda i: (i, 0)
    c2 = lambda i: (0, 0)
    c3 = lambda i: (0, 0, 0)
    full = lambda a: pl.BlockSpec(a.shape, c2 if a.ndim == 2 else c3)
    return pl.pallas_call(
        _merge_body,
        grid=(t // tm,),
        in_specs=[pl.BlockSpec((tm, d), tok),
                  pl.BlockSpec((SSM_GROUPS, tm // SSM_CHUNK, SSM_CHUNK * SSM_GROUP), lambda i: (0, i, 0)),
                  pl.BlockSpec((tm, SSM_WIDTH), tok),
                  pl.BlockSpec((tm, SSM_WIDTH), tok),
                  full(g_mix), full(w_gate), full(b_gate), full(w_glu), full(b_glu),
                  full(w_br), full(w_out), full(g_ffn), full(w_rt), full(b_rt)],
        out_specs=[pl.BlockSpec((tm, d), tok),
                   pl.BlockSpec((SC_SPLIT, tm, SC_ROW), lambda i: (0, i, 0)),
                   pl.BlockSpec((tm, LANES), tok),
                   pl.BlockSpec((8, tm), lambda i: (0, i)),
                   pl.BlockSpec((8, LANES), c2)],
        out_shape=[jax.ShapeDtypeStruct((t, d), F32),
                   jax.ShapeDtypeStruct((SC_SPLIT, t, SC_ROW), jnp.uint32),
                   jax.ShapeDtypeStruct((t, LANES), F32),
                   jax.ShapeDtypeStruct((8, t), F32),
                   jax.ShapeDtypeStruct((8, LANES), F32)],
        scratch_shapes=[pltpu.VMEM((8, LANES), F32),
                        pltpu.VMEM((SSM_WIDTH // LANES, tm, LANES), F32)],
        compiler_params=pltpu.CompilerParams(dimension_semantics=("arbitrary",),
                                             vmem_limit_bytes=VMEM_LIMIT),
        name="merge_route",
    )(x, y2, yb, yc, g_mix, w_gate, b_gate, w_glu, b_glu, w_br, w_out, g_ffn, w_rt, b_rt)


def _slot_body(rtt_ref, pstart_ref, out_ref, *, nslots):
    tl = rtt_ref.shape[1]
    expert = lax.broadcasted_iota(jnp.int32, (N_EXPERTS, tl), 0).astype(F32)
    for k in range(TOP_K):
        onehot = rtt_ref[k:k + 1, :] == expert
        start = jnp.sum(jnp.where(onehot, pstart_ref[...], 0.0), axis=0, keepdims=True)
        slot = (start + rtt_ref[TOP_K + k:TOP_K + k + 1, :]).astype(jnp.int32)
        for j in range(SC_SPLIT):
            out_ref[k * SC_SPLIT + j:k * SC_SPLIT + j + 1, :] = slot + j * nslots


def _slot_rows(rtt, pstart, nslots):
    t = rtt.shape[1]
    tl = SLOT_LANES
    return pl.pallas_call(
        functools.partial(_slot_body, nslots=nslots),
        grid=(t // tl,),
        in_specs=[pl.BlockSpec((8, tl), lambda i: (0, i)),
                  pl.BlockSpec((N_EXPERTS, 1), lambda i: (0, 0))],
        out_specs=pl.BlockSpec((TOP_K * SC_SPLIT, tl), lambda i: (0, i)),
        out_shape=jax.ShapeDtypeStruct((TOP_K * SC_SPLIT, t), jnp.int32),
        compiler_params=pltpu.CompilerParams(dimension_semantics=("arbitrary",)),
        name="slot_rows",
    )(rtt, pstart.astype(F32).reshape(N_EXPERTS, 1))


def _expert_body(meta_ref, buf_ref, w1_hbm, w3_hbm, w2_hbm, out_ref, w1_buf, w3_buf, w2_buf, sem):
    i = pl.program_id(0)
    expert = meta_ref[0, i]
    valid = meta_ref[1, i]
    slot = meta_ref[2, i]
    ahead1 = meta_ref[4, i]
    ahead2 = meta_ref[5, i]

    def weight_copies(e, s):
        return (pltpu.make_async_copy(w1_hbm.at[e], w1_buf.at[s], sem.at[s, 0]),
                pltpu.make_async_copy(w3_hbm.at[e], w3_buf.at[s], sem.at[s, 1]),
                pltpu.make_async_copy(w2_hbm.at[e], w2_buf.at[s], sem.at[s, 2]))

    @pl.when(i == 0)
    def _():
        for c in weight_copies(expert, slot):
            c.start()

        @pl.when(ahead1 >= 0)
        def _():
            for c in weight_copies(ahead1, (slot + 1) % WEIGHT_SLOTS):
                c.start()

    @pl.when(meta_ref[3, i] == 1)
    def _():
        for c in weight_copies(expert, slot):
            c.wait()

        @pl.when(ahead2 >= 0)
        def _():
            for c in weight_copies(ahead2, (slot + 2) % WEIGHT_SLOTS):
                c.start()

    @pl.when(valid > 0)
    def _():
        x = _unpack_bf16_pair(jnp.concatenate([buf_ref[j] for j in range(SC_SPLIT)], axis=1))
        row = lax.broadcasted_iota(jnp.int32, x.shape, 0)
        x = jnp.where(row < valid, x, 0.0).astype(BF16)
        h1 = _dot(x, w1_buf[slot].astype(BF16))
        h3 = _dot(x, w3_buf[slot].astype(BF16))
        a = (h1 * _sigmoid(h1) * h3).astype(BF16)
        packed = _pack_bf16_pair(_dot(a, w2_buf[slot].astype(BF16)))
        for j in range(SC_SPLIT):
            out_ref[j] = packed[:, j * SC_ROW:(j + 1) * SC_ROW]

    @pl.when(valid <= 0)
    def _():
        out_ref[...] = jnp.zeros_like(out_ref)


def _expert_meta(block_expert, block_valid):
    nb = block_expert.shape[0]
    pos = jnp.arange(nb, dtype=jnp.int32)
    prev = jnp.concatenate([jnp.full((1,), -1, jnp.int32), block_expert[:-1]])
    first = (block_valid > 0) & (block_expert != prev)
    run = jnp.cumsum(first.astype(jnp.int32)) - 1
    start_pos = jnp.where(first, pos, nb)
    later = jnp.concatenate([start_pos[1:], jnp.full((2,), nb, jnp.int32)])
    nearest = lax.cummin(later, axis=0, reverse=True)
    n1 = nearest[:nb]
    n2 = nearest[n1]
    experts_ext = jnp.concatenate([block_expert, jnp.full((1,), -1, jnp.int32)])
    return jnp.stack([block_expert, block_valid, run % WEIGHT_SLOTS, first.astype(jnp.int32),
                      experts_ext[n1], experts_ext[n2]]).astype(jnp.int32)


def _experts(block_expert, block_valid, buf, w1, w3, w2):
    _, nslots, _ = buf.shape
    nb = nslots // BM
    rows = pl.BlockSpec((SC_SPLIT, BM, SC_ROW), lambda i, meta: (0, i, 0))
    hbm = pl.BlockSpec(memory_space=pl.ANY)
    grid_spec = pltpu.PrefetchScalarGridSpec(
        num_scalar_prefetch=1,
        grid=(nb,),
        in_specs=[rows, hbm, hbm, hbm],
        out_specs=rows,
        scratch_shapes=[pltpu.VMEM((WEIGHT_SLOTS,) + w1.shape[1:], w1.dtype),
                        pltpu.VMEM((WEIGHT_SLOTS,) + w3.shape[1:], w3.dtype),
                        pltpu.VMEM((WEIGHT_SLOTS,) + w2.shape[1:], w2.dtype),
                        pltpu.SemaphoreType.DMA((WEIGHT_SLOTS, 3))],
    )
    return pl.pallas_call(
        _expert_body,
        grid_spec=grid_spec,
        out_shape=jax.ShapeDtypeStruct(buf.shape, jnp.uint32),
        compiler_params=pltpu.CompilerParams(dimension_semantics=("arbitrary",),
                                             vmem_limit_bytes=VMEM_LIMIT),
        name="experts",
    )(_expert_meta(block_expert, block_valid), buf, w1, w3, w2)


def _sc_mesh():
    return plsc.VectorSubcoreMesh(core_axis_name="core", subcore_axis_name="subcore")


def _dispatch_rows(rows, dest0, dest1, nslots):
    t, w = rows.shape
    win = SC_WINDOW
    idx_spec = pl.BlockSpec((1, win), lambda i: (0, i))

    @functools.partial(pl.kernel, mesh=_sc_mesh(), scratch_types=[],
                       out_type=jax.ShapeDtypeStruct((nslots, w), rows.dtype), name="dispatch_rows")
    def run(rows_hbm, i0_hbm, i1_hbm, out_hbm):
        def body(rows_vmem, i0_vmem, i1_vmem):
            pltpu.sync_copy(rows_vmem, out_hbm.at[i0_vmem.at[0]])
            pltpu.sync_copy(rows_vmem, out_hbm.at[i1_vmem.at[0]])

        pltpu.emit_pipeline(
            body, grid=(t // win,),
            in_specs=[pl.BlockSpec((win, w), lambda i: (i, 0)), idx_spec, idx_spec],
            out_specs=[],
            core_axis_name=("core", "subcore"),
            dimension_semantics=(pltpu.PARALLEL,),
        )(rows_hbm, i0_hbm, i1_hbm)

    return run(rows, dest0.reshape(1, t), dest1.reshape(1, t))


def _gather_rows(table, idx):
    n = idx.shape[0]
    w = table.shape[1]
    win = SC_WINDOW

    @functools.partial(pl.kernel, mesh=_sc_mesh(), scratch_types=[],
                       out_type=jax.ShapeDtypeStruct((n, w), table.dtype), name="gather_rows")
    def run(table_hbm, i_hbm, out_hbm):
        def body(i_vmem, out_vmem):
            pltpu.sync_copy(table_hbm.at[i_vmem.at[0]], out_vmem)

        pltpu.emit_pipeline(
            body, grid=(n // win,),
            in_specs=[pl.BlockSpec((1, win), lambda i: (0, i))],
            out_specs=[pl.BlockSpec((win, w), lambda i: (i, 0))],
            core_axis_name=("core", "subcore"),
            dimension_semantics=(pltpu.PARALLEL,),
        )(i_hbm, out_hbm)

    return run(table, idx.reshape(1, n))


def _combine_body(h_ref, g_ref, rt_ref, gfin_ref, out_ref):
    rt = rt_ref[...]
    y = h_ref[...]
    for k in range(TOP_K):
        rows = jnp.concatenate([g_ref[k * SC_SPLIT + j] for j in range(SC_SPLIT)], axis=1)
        y = y + rt[:, 4 + k:5 + k] * _unpack_bf16_pair(rows)
    out_ref[...] = _rms(y, gfin_ref[...])


def _combine(h, g, rt, g_final):
    t, d = h.shape
    tm = TM_OUT
    tok = lambda i: (i, 0)
    return pl.pallas_call(
        _combine_body,
        grid=(t // tm,),
        in_specs=[pl.BlockSpec((tm, d), tok),
                  pl.BlockSpec((TOP_K * SC_SPLIT, tm, SC_ROW), lambda i: (0, i, 0)),
                  pl.BlockSpec((tm, LANES), tok),
                  pl.BlockSpec((1, d), lambda i: (0, 0))],
        out_specs=pl.BlockSpec((tm, d), tok),
        out_shape=jax.ShapeDtypeStruct((t, d), F32),
        compiler_params=pltpu.CompilerParams(dimension_semantics=("arbitrary",),
                                             vmem_limit_bytes=VMEM_LIMIT),
        name="combine",
    )(h, g, rt, g_final)


def _layer(h, mem, g_mix, g_mem, w_in, w_gate, b_gate, lam_re, lam_im, log_dt, b_re, b_im,
           c_re, c_im, d_skip, w_glu, b_glu, g_sgu, w_spatial, b_spatial, w_kv, w_branch,
           w_out, g_ffn, w_group, b_group, w_router, b_router, w1, w3, w2, g_out):
    bsz, s, d = h.shape
    t = bsz * s
    row = lambda a: a.reshape(1, -1)

    k, v = _kv_proj(mem, row(g_mem), w_kv.astype(BF16))

    tril = jnp.tril(jnp.ones((CHUNK, CHUNK), dtype=bool))
    w_sp = jnp.where(tril, w_spatial, 0.0).astype(BF16)
    b_sp = jnp.broadcast_to(b_spatial[:, :, None], (SGU_HEADS, CHUNK, SGU_HEAD_DIM))
    u2, y_b, y_c = _in_proj(h, row(g_mix), w_in.astype(BF16), row(g_sgu), w_sp, b_sp, k, v)
    y2 = _ssm(u2, *_ssm_params(lam_re, lam_im, log_dt, b_re, b_im, c_re, c_im, d_skip), n_seq=bsz)

    pad = LANES - N_GROUPS - N_EXPERTS
    w_rt = jnp.concatenate([w_group, w_router, jnp.zeros((d, pad), F32)], axis=1)
    w_rt_hi = w_rt.astype(BF16)
    w_rt = jnp.concatenate([w_rt_hi, (w_rt - w_rt_hi.astype(F32)).astype(BF16)], axis=1)
    b_rt =jnp.concatenate([b_group, b_router, jnp.zeros((pad,), F32)]).reshape(1, LANES)
    h2, xnp, rt, rtt, cnt = _merge_route(
        h.reshape(t, d), y2, y_b.reshape(t, -1), y_c.reshape(t, -1), row(g_mix),
        w_gate.astype(BF16), row(b_gate), w_glu.astype(BF16), row(b_glu),
        w_branch.astype(BF16), w_out.astype(BF16), row(g_ffn), w_rt, b_rt)

    counts = cnt[0, ROUTE_LANE0:ROUTE_LANE0 + N_EXPERTS].astype(jnp.int32)
    padded = (counts + BM - 1) // BM * BM
    block_end = jnp.cumsum(padded)
    pstart = block_end - padded
    nb = (t * TOP_K) // BM + N_EXPERTS
    blk_row0 = jnp.arange(nb, dtype=jnp.int32) * BM
    block_expert = jnp.sum((block_end[None, :] <= blk_row0[:, None]).astype(jnp.int32), axis=1)
    block_expert = jnp.minimum(block_expert, N_EXPERTS - 1)
    block_valid = jnp.clip(counts[block_expert] - (blk_row0 - pstart[block_expert]), 0, BM)
    block_valid = jnp.where(blk_row0 < block_end[-1], block_valid, 0).astype(jnp.int32)
    nslots = nb * BM
    dest_p = _slot_rows(rtt, pstart, nslots).reshape(TOP_K, SC_SPLIT * t)
    buf = _dispatch_rows(xnp.reshape(SC_SPLIT * t, SC_ROW), dest_p[0], dest_p[1], SC_SPLIT * nslots)
    yb = _experts(block_expert, block_valid, buf.reshape(SC_SPLIT, nslots, SC_ROW), w1, w3, w2)
    g = _gather_rows(yb.reshape(SC_SPLIT * nslots, SC_ROW), dest_p.reshape(-1))
    out = _combine(h2, g.reshape(TOP_K * SC_SPLIT, t, SC_ROW), rt, row(g_out))
    return out.reshape(bsz, s, d)


def kernel(x, mem, g_mix, g_mem, w_in, w_gate, b_gate, lam_re, lam_im, log_dt, b_re, b_im, c_re,
           c_im, d_skip, w_glu, b_glu, g_sgu, w_spatial, b_spatial, w_kv, w_branch, w_out, g_ffn,
           w_group, b_group, w_router, b_router, w1, w3, w2, g_final):
    assert g_mix.shape[0] == 1, "single-layer stack"
    return _layer(x, mem, g_mix[0], g_mem[0], w_in[0], w_gate[0], b_gate[0], lam_re[0], lam_im[0],
                  log_dt[0], b_re[0], b_im[0], c_re[0], c_im[0], d_skip[0], w_glu[0], b_glu[0],
                  g_sgu[0], w_spatial[0], b_spatial[0], w_kv[0], w_branch[0], w_out[0], g_ffn[0],
                  w_group[0], b_group[0], w_router[0], b_router[0], w1[0], w3[0], w2[0], g_final)
```

```python
import functools
import math

import jax
import jax.numpy as jnp
from jax import lax
from jax.experimental import pallas as pl
from jax.experimental.pallas import tpu as pltpu
from jax.experimental.pallas import tpu_sc as plsc

F32 = jnp.float32
BF16 = jnp.bfloat16

EPS = 1e-6
D_MODEL = 1024
SSM_WIDTH = 512
SSM_GROUP = 16
SSM_GROUPS = 32
SSM_STATE = 64
SSM_CHUNK = 16
SGU_WIDTH = 512
SGU_HEADS = 4
SGU_HEAD_DIM = 128
CHUNK = 128
XA_HEADS = 4
XA_HEAD_DIM = 128
N_GROUPS = 8
EXPERTS_PER_GROUP = 8
N_EXPERTS = 64
TOP_K = 2
D_FF = 512
LANES = 128
ROUTE_LANE0 = N_GROUPS

TM_IN = 512
TM_MERGE = 512
MERGE_COLS = 256
TM_OUT = 512
BM = 512
WEIGHT_SLOTS = 3
SC_WINDOW = 128
SC_ROW = 256
SC_SPLIT = (D_MODEL // 2) // SC_ROW
SLOT_LANES = 2048
VMEM_LIMIT = 56 * 1024 * 1024


def _rms(x, g):
    return x * lax.rsqrt(jnp.mean(x * x, axis=-1, keepdims=True) + EPS) * g


def _sigmoid(x):
    return 0.5 * (1.0 + jnp.tanh(0.5 * x))


def _gelu(x):
    c = math.sqrt(2.0 / math.pi)
    return 0.5 * x * (1.0 + jnp.tanh(c * (x + 0.044715 * (x * x * x))))


def _dot(a, b):
    return jnp.dot(a, b, preferred_element_type=F32)


def _pack_bf16_pair(x):
    n = x.shape[1] // 2
    lo = lax.bitcast_convert_type(x[:, :n].astype(BF16).astype(F32), jnp.uint32)
    hi = lax.bitcast_convert_type(x[:, n:].astype(BF16).astype(F32), jnp.uint32)
    return hi | (lo >> 16)


def _unpack_bf16_pair(p):
    lo = lax.bitcast_convert_type(p << 16, F32)
    hi = lax.bitcast_convert_type(p & jnp.uint32(0xFFFF0000), F32)
    return jnp.concatenate([lo, hi], axis=1)


GROUPS_PER_TILE = LANES // SSM_GROUP
POS_PER_TILE = LANES // SSM_GROUP


def _slot_masks(rows):
    lane = lax.broadcasted_iota(jnp.int32, (rows, LANES), 1)
    return [(lane >= i * SSM_GROUP) & (lane < (i + 1) * SSM_GROUP) for i in range(LANES // SSM_GROUP)]


def _tokens_to_chunks(tok_ref, out_ref):
    tm = tok_ref.shape[1]
    nc = tm // SSM_CHUNK
    masks = _slot_masks(nc)
    for k in range(SSM_WIDTH // LANES):
        for j in range(SSM_CHUNK // POS_PER_TILE):
            src = [tok_ref[k, pl.ds(j * POS_PER_TILE + p, nc, stride=SSM_CHUNK), :]
                   for p in range(POS_PER_TILE)]
            for gi in range(GROUPS_PER_TILE):
                acc = None
                for p in range(POS_PER_TILE):
                    shift = ((p - gi) * SSM_GROUP) % LANES
                    r = pltpu.roll(src[p], shift, 1) if shift else src[p]
                    acc = r if acc is None else jnp.where(masks[p], r, acc)
                out_ref[k * GROUPS_PER_TILE + gi, :, pl.ds(j * LANES, LANES)] = acc.astype(out_ref.dtype)


def _chunks_to_tokens(chunk_ref, tok_ref):
    tm = tok_ref.shape[1]
    nc = tm // SSM_CHUNK
    masks = _slot_masks(nc)
    for k in range(SSM_WIDTH // LANES):
        for j in range(SSM_CHUNK // POS_PER_TILE):
            src = [chunk_ref[k * GROUPS_PER_TILE + gi, :, pl.ds(j * LANES, LANES)].astype(F32)
                   for gi in range(GROUPS_PER_TILE)]
            for p in range(POS_PER_TILE):
                acc = None
                for gi in range(GROUPS_PER_TILE):
                    shift = ((gi - p) * SSM_GROUP) % LANES
                    r = pltpu.roll(src[gi], shift, 1) if shift else src[gi]
                    acc = r if acc is None else jnp.where(masks[gi], r, acc)
                tok_ref[k, pl.ds(j * POS_PER_TILE + p, nc, stride=SSM_CHUNK), :] = acc


def _kv_body(mem_ref, g_ref, w_ref, k_ref, v_ref):
    n = _rms(mem_ref[0], g_ref[...]).astype(BF16)
    kv = _dot(n, w_ref[...])
    k_ref[0] = kv[:, :SGU_WIDTH].astype(BF16)
    v_ref[0] = kv[:, SGU_WIDTH:].astype(BF16)


def _kv_proj(mem, g_mem, w_kv):
    b, m, d = mem.shape
    return pl.pallas_call(
        _kv_body,
        grid=(b,),
        in_specs=[pl.BlockSpec((1, m, d), lambda i: (i, 0, 0)),
                  pl.BlockSpec((1, d), lambda i: (0, 0)),
                  pl.BlockSpec((d, 2 * SGU_WIDTH), lambda i: (0, 0))],
        out_specs=[pl.BlockSpec((1, m, SGU_WIDTH), lambda i: (i, 0, 0)),
                   pl.BlockSpec((1, m, SGU_WIDTH), lambda i: (i, 0, 0))],
        out_shape=[jax.ShapeDtypeStruct((b, m, SGU_WIDTH), BF16),
                   jax.ShapeDtypeStruct((b, m, SGU_WIDTH), BF16)],
        compiler_params=pltpu.CompilerParams(dimension_semantics=("arbitrary",),
                                             vmem_limit_bytes=VMEM_LIMIT),
        name="kv_proj",
    )(mem, g_mem, w_kv)


def _in_body(x_ref, gmix_ref, win_ref, gsgu_ref, wsp_ref, bsp_ref, k_ref, v_ref,
             u2_ref, yb_ref, yc_ref, tok_ref):
    n = _rms(x_ref[0], gmix_ref[...]).astype(BF16)
    proj = _dot(n, win_ref[...])
    for k in range(SSM_WIDTH // LANES):
        tok_ref[k] = proj[:, k * LANES:(k + 1) * LANES]
    _tokens_to_chunks(tok_ref, u2_ref)

    u = _gelu(proj[:, SSM_WIDTH:SSM_WIDTH + SGU_WIDTH])
    v = _gelu(proj[:, SSM_WIDTH + SGU_WIDTH:SSM_WIDTH + 2 * SGU_WIDTH])
    v = _rms(v, gsgu_ref[...]).astype(BF16)
    tm = u.shape[0]
    rows = []
    for c in range(tm // CHUNK):
        vc = v[c * CHUNK:(c + 1) * CHUNK]
        heads = []
        for h in range(SGU_HEADS):
            sl = slice(h * SGU_HEAD_DIM, (h + 1) * SGU_HEAD_DIM)
            heads.append(_dot(wsp_ref[h], vc[:, sl]) + bsp_ref[h])
        rows.append(jnp.concatenate(heads, axis=1))
    sv = jnp.concatenate(rows, axis=0)
    yb_ref[0] = (u * sv).astype(BF16)

    q = proj[:, SSM_WIDTH + 2 * SGU_WIDTH:].astype(BF16)
    kk = k_ref[0]
    vv = v_ref[0]
    outs = []
    for h in range(XA_HEADS):
        sl = slice(h * XA_HEAD_DIM, (h + 1) * XA_HEAD_DIM)
        s = lax.dot_general(q[:, sl], kk[:, sl], (((1,), (1,)), ((), ())),
                            preferred_element_type=F32) * (XA_HEAD_DIM ** -0.5)
        e = jnp.exp(s - jnp.max(s, axis=-1, keepdims=True))
        l = jnp.sum(e, axis=-1, keepdims=True)
        outs.append(_dot(e.astype(BF16), vv[:, sl]) / l)
    yc_ref[0] = jnp.concatenate(outs, axis=1).astype(BF16)


def _in_proj(x, g_mix, w_in, g_sgu, w_sp, b_sp, k, v):
    b, s, d = x.shape
    m = k.shape[1]
    const2 = lambda i, j: (0, 0)
    const3 = lambda i, j: (0, 0, 0)
    tok = lambda i, j: (i, j, 0)
    per_b = lambda i, j: (i, 0, 0)
    out = jax.ShapeDtypeStruct((b, s, SSM_WIDTH), BF16)
    nc = TM_IN // SSM_CHUNK
    tiles = s // TM_IN
    u2 = jax.ShapeDtypeStruct((SSM_GROUPS, b * s // SSM_CHUNK, SSM_CHUNK * SSM_GROUP), BF16)
    return pl.pallas_call(
        _in_body,
        grid=(b, s // TM_IN),
        in_specs=[pl.BlockSpec((1, TM_IN, d), tok),
                  pl.BlockSpec((1, d), const2),
                  pl.BlockSpec(w_in.shape, const2),
                  pl.BlockSpec((1, SGU_WIDTH), const2),
                  pl.BlockSpec(w_sp.shape, const3),
                  pl.BlockSpec(b_sp.shape, const3),
                  pl.BlockSpec((1, m, SGU_WIDTH), per_b),
                  pl.BlockSpec((1, m, SGU_WIDTH), per_b)],
        out_specs=[pl.BlockSpec((SSM_GROUPS, nc, SSM_CHUNK * SSM_GROUP), lambda i, j: (0, i * tiles + j, 0)),
                   pl.BlockSpec((1, TM_IN, SSM_WIDTH), tok),
                   pl.BlockSpec((1, TM_IN, SSM_WIDTH), tok)],
        out_shape=[u2, out, out],
        scratch_shapes=[pltpu.VMEM((SSM_WIDTH // LANES, TM_IN, LANES), F32)],
        compiler_params=pltpu.CompilerParams(dimension_semantics=("arbitrary", "arbitrary"),
                                             vmem_limit_bytes=VMEM_LIMIT),
        name="in_proj",
    )(x, g_mix, w_in, g_sgu, w_sp, b_sp, k, v)


def _ssm_params(lam_re, lam_im, log_dt, b_re, b_im, c_re, c_im, d_skip):
    g, p = lam_re.shape
    dt = jnp.exp(log_dt)[:, None]
    ar = lam_re * dt
    ai = lam_im * dt

    def cpow(j):
        mag = jnp.exp(ar[:, None, :] * j[None, :, None])
        ph = ai[:, None, :] * j[None, :, None]
        return mag * jnp.cos(ph), mag * jnp.sin(ph)

    pr, pi = cpow(jnp.arange(SSM_CHUNK + 1, dtype=F32))
    nr = pr[:, 1] - 1.0
    ni = pi[:, 1]
    den = lam_re * lam_re + lam_im * lam_im
    fr = (nr * lam_re + ni * lam_im) / den
    fi = (ni * lam_re - nr * lam_im) / den
    bbr = fr[..., None] * b_re - fi[..., None] * b_im
    bbi = fr[..., None] * b_im + fi[..., None] * b_re

    L = SSM_CHUNK
    cr = c_re[:, None]
    ci = c_im[:, None]
    cpr = cr * pr[:, :L, None, :] - ci * pi[:, :L, None, :]
    cpi = cr * pi[:, :L, None, :] + ci * pr[:, :L, None, :]
    ccat = jnp.concatenate([cpr, -cpi], axis=-1)
    ccat = ccat.transpose(0, 3, 1, 2).reshape(g, 2 * p, L * SSM_GROUP)
    bcat = jnp.concatenate([bbr, bbi], axis=1).transpose(0, 2, 1)

    rr = pr[:, L - 1::-1][:, :L]
    ri = pi[:, L - 1::-1][:, :L]
    bbr_t = bbr.transpose(0, 2, 1)[:, None]
    bbi_t = bbi.transpose(0, 2, 1)[:, None]
    n_re = rr[:, :, None, :] * bbr_t - ri[:, :, None, :] * bbi_t
    n_im = rr[:, :, None, :] * bbi_t + ri[:, :, None, :] * bbr_t
    nmat = jnp.concatenate([n_re, n_im], axis=-1).reshape(g, L * SSM_GROUP, 2 * p)

    clr = cr * pr[:, 1:, None, :] - ci * pi[:, 1:, None, :]
    cli = cr * pi[:, 1:, None, :] + ci * pr[:, 1:, None, :]
    mmat = jnp.concatenate([clr, -cli], axis=-1)
    mmat = mmat.transpose(0, 3, 1, 2).reshape(g, 2 * p, L * SSM_GROUP)

    qr, qi = cpow(jnp.asarray([float(L * 2 ** k) for k in range(8)], F32))
    lr = jnp.concatenate([qr, qr], axis=-1)
    li = jnp.concatenate([-qi, qi], axis=-1)
    d2 = jnp.tile(d_skip.reshape(g, 1, SSM_GROUP), (1, 1, L))
    return bcat, ccat, nmat.astype(BF16), mmat.astype(BF16), lr, li, d2


def _ssm_body(u_ref, bcat_ref, ccat_ref, n_ref, m_ref, lr_ref, li_ref, d2_ref, y_ref, toep_ref, *, n_seq):
    kern = jnp.dot(bcat_ref[0], ccat_ref[0], precision=lax.Precision.HIGHEST,
                   preferred_element_type=F32)
    col = lax.broadcasted_iota(jnp.int32, kern.shape, 1)
    for s in range(SSM_CHUNK):
        shifted = pltpu.roll(kern, s * SSM_GROUP, 1) if s else kern
        toep_ref[s * SSM_GROUP:(s + 1) * SSM_GROUP, :] = jnp.where(
            col >= s * SSM_GROUP, shifted, 0.0).astype(BF16)

    u = u_ref[0]
    rows = u.shape[0]
    per = rows // n_seq
    y = _dot(u, toep_ref[...])
    st = _dot(u, n_ref[0])
    row = lax.broadcasted_iota(jnp.int32, (per, LANES), 0)
    prev = []
    for b in range(n_seq):
        x = st[b * per:(b + 1) * per]
        k = 0
        while (1 << k) < per:
            d = 1 << k
            sh = jnp.where(row >= d, pltpu.roll(x, d, 0), 0.0)
            x = x + sh * lr_ref[0, k:k + 1, :] + pltpu.roll(sh, SSM_STATE, 1) * li_ref[0, k:k + 1, :]
            k += 1
        prev.append(jnp.where(row >= 1, pltpu.roll(x, 1, 0), 0.0))
    xp = jnp.concatenate(prev, axis=0).astype(BF16)
    y = y + _dot(xp, m_ref[0]) + d2_ref[0] * u.astype(F32)
    y_ref[0] = _gelu(y).astype(BF16)


def _ssm(u2, bcat, ccat, nmat, mmat, lr, li, d2, n_seq):
    g, rows, w = u2.shape
    blk = lambda a: pl.BlockSpec((1,) + a.shape[1:], lambda i: (i, 0, 0))
    return pl.pallas_call(
        functools.partial(_ssm_body, n_seq=n_seq),
        grid=(g,),
        in_specs=[blk(u2), blk(bcat), blk(ccat), blk(nmat), blk(mmat), blk(lr), blk(li), blk(d2)],
        out_specs=blk(u2),
        out_shape=jax.ShapeDtypeStruct(u2.shape, BF16),
        scratch_shapes=[pltpu.VMEM((w, w), BF16)],
        compiler_params=pltpu.CompilerParams(dimension_semantics=("arbitrary",),
                                             vmem_limit_bytes=VMEM_LIMIT),
        name="ssm",
    )(u2, bcat, ccat, nmat, mmat, lr, li, d2)


def _merge_body(x_ref, y2_ref, yb_ref, yc_ref, gmix_ref, wgate_ref, bgate_ref, wglu_ref, bglu_ref,
                wbr_ref, wout_ref, gffn_ref, wrt_ref, brt_ref,
                h_ref, xnp_ref, rt_ref, rtt_ref, cnt_ref, carry_ref, tok_ref):
    i = pl.program_id(0)

    @pl.when(i == 0)
    def _():
        carry_ref[...] = jnp.zeros_like(carry_ref)

    _chunks_to_tokens(y2_ref, tok_ref)
    x = x_ref[...]
    n = _rms(x, gmix_ref[...]).astype(BF16)
    ys = jnp.concatenate([tok_ref[k] for k in range(SSM_WIDTH // LANES)], axis=1).astype(BF16)
    glu = _dot(ys, wglu_ref[...]) + bglu_ref[...]
    ya = (glu[:, :SSM_WIDTH] * _sigmoid(glu[:, SSM_WIDTH:])).astype(BF16)
    branches = (ya, yb_ref[...], yc_ref[...])
    merged = []
    for c in range(D_MODEL // MERGE_COLS):
        acc = None
        for b, yb in enumerate(branches):
            cols = pl.ds(b * D_MODEL + c * MERGE_COLS, MERGE_COLS)
            gate = _sigmoid(_dot(n, wgate_ref[:, cols]) + bgate_ref[:, cols])
            term = gate * _dot(yb, wbr_ref[b, :, pl.ds(c * MERGE_COLS, MERGE_COLS)])
            acc = term if acc is None else acc + term
        merged.append(acc.astype(BF16))
    h = x + _dot(jnp.concatenate(merged, axis=1), wout_ref[...])
    h_ref[...] = h

    xn = _rms(h, gffn_ref[...])
    packed = _pack_bf16_pair(xn)
    for j in range(SC_SPLIT):
        xnp_ref[j] = packed[:, j * SC_ROW:(j + 1) * SC_ROW]

    x_hi = xn.astype(BF16)
    x_lo = (xn - x_hi.astype(F32)).astype(BF16)
    head = _dot(x_hi, wrt_ref[...])
    logits = (head[:, :LANES] + head[:, LANES:] + _dot(x_lo, wrt_ref[:, :LANES])) + brt_ref[...]
    tm = logits.shape[0]
    lane_i = lax.broadcasted_iota(jnp.int32, (tm, LANES), 1)
    lane = lane_i.astype(F32)
    neg = jnp.float32(-3.0e38)
    big = jnp.float32(LANES)
    gmask = lane_i < N_GROUPS
    gl = jnp.where(gmask, logits, neg)
    gmax = jnp.max(gl, axis=-1, keepdims=True)
    gidx = jnp.min(jnp.where(gl == gmax, lane, big), axis=-1, keepdims=True)
    gsum = jnp.sum(jnp.where(gmask, jnp.exp(gl - gmax), 0.0), axis=-1, keepdims=True)
    g_w = 1.0 / gsum
    e_lane = lane_i - ROUTE_LANE0
    lane_group = (e_lane >> 3).astype(F32)
    emask = (e_lane >= 0) & (e_lane < N_EXPERTS) & (lane_group == gidx)
    el = jnp.where(emask, logits, neg)
    m1 = jnp.max(el, axis=-1, keepdims=True)
    i1 = jnp.min(jnp.where(el == m1, lane, big), axis=-1, keepdims=True)
    el2 = jnp.where(lane == i1, neg, el)
    m2 = jnp.max(el2, axis=-1, keepdims=True)
    i2 = jnp.min(jnp.where(el2 == m2, lane, big), axis=-1, keepdims=True)
    t = jnp.exp(m2 - m1)
    w1 = g_w / (1.0 + t)
    w2 = g_w * t / (1.0 + t)

    sel1 = lane == i1
    sel2 = lane == i2
    onehot = jnp.where(sel1 | sel2, 1.0, 0.0)
    r_i = lax.broadcasted_iota(jnp.int32, (tm, tm), 0)
    c_i = lax.broadcasted_iota(jnp.int32, (tm, tm), 1)
    stril = jnp.where(c_i < r_i, 1.0, 0.0).astype(BF16)
    cum = _dot(stril, onehot.astype(BF16)) + carry_ref[0:1, :]
    rank1 = jnp.sum(jnp.where(sel1, cum, 0.0), axis=-1, keepdims=True)
    rank2 = jnp.sum(jnp.where(sel2, cum, 0.0), axis=-1, keepdims=True)
    carry_ref[...] = carry_ref[...] + jnp.sum(onehot, axis=0, keepdims=True)
    cnt_ref[...] = carry_ref[...]

    cols = (i1 - ROUTE_LANE0, i2 - ROUTE_LANE0, rank1, rank2, w1, w2)
    rt = jnp.zeros((tm, LANES), F32)
    for c, val in enumerate(cols):
        rt = jnp.where(lane_i == c, val, rt)
    rt_ref[...] = rt
    rtt_ref[...] = rt.T[:8]


def _merge_route(x, y2, yb, yc, g_mix, w_gate, b_gate, w_glu, b_glu, w_br, w_out, g_ffn, w_rt, b_rt):
    t, d = x.shape
    tm = TM_MERGE
    tok = lambda i: (i, 0)
    c2 = lambda i: (0, 0)
    c3 = lambda i: (0, 0, 0)
    full = lambda a: pl.BlockSpec(a.shape, c2 if a.ndim == 2 else c3)
    return pl.pallas_call(
        _merge_body,
        grid=(t // tm,),
        in_specs=[pl.BlockSpec((tm, d), tok),
                  pl.BlockSpec((SSM_GROUPS, tm // SSM_CHUNK, SSM_CHUNK * SSM_GROUP), lambda i: (0, i, 0)),
                  pl.BlockSpec((tm, SSM_WIDTH), tok),
                  pl.BlockSpec((tm, SSM_WIDTH), tok),
                  full(g_mix), full(w_gate), full(b_gate), full(w_glu), full(b_glu),
                  full(w_br), full(w_out), full(g_ffn), full(w_rt), full(b_rt)],
        out_specs=[pl.BlockSpec((tm, d), tok),
                   pl.BlockSpec((SC_SPLIT, tm, SC_ROW), lambda i: (0, i, 0)),
                   pl.BlockSpec((tm, LANES), tok),
                   pl.BlockSpec((8, tm), lambda i: (0, i)),
                   pl.BlockSpec((8, LANES), c2)],
        out_shape=[jax.ShapeDtypeStruct((t, d), F32),
                   jax.ShapeDtypeStruct((SC_SPLIT, t, SC_ROW), jnp.uint32),
                   jax.ShapeDtypeStruct((t, LANES), F32),
                   jax.ShapeDtypeStruct((8, t), F32),
                   jax.ShapeDtypeStruct((8, LANES), F32)],
        scratch_shapes=[pltpu.VMEM((8, LANES), F32),
                        pltpu.VMEM((SSM_WIDTH // LANES, tm, LANES), F32)],
        compiler_params=pltpu.CompilerParams(dimension_semantics=("arbitrary",),
                                             vmem_limit_bytes=VMEM_LIMIT),
        name="merge_route",
    )(x, y2, yb, yc, g_mix, w_gate, b_gate, w_glu, b_glu, w_br, w_out, g_ffn, w_rt, b_rt)


def _slot_body(rtt_ref, pstart_ref, out_ref, *, nslots):
    tl = rtt_ref.shape[1]
    expert = lax.broadcasted_iota(jnp.int32, (N_EXPERTS, tl), 0).astype(F32)
    for k in range(TOP_K):
        onehot = rtt_ref[k:k + 1, :] == expert
        start = jnp.sum(jnp.where(onehot, pstart_ref[...], 0.0), axis=0, keepdims=True)
        slot = (start + rtt_ref[TOP_K + k:TOP_K + k + 1, :]).astype(jnp.int32)
        for j in range(SC_SPLIT):
            out_ref[k * SC_SPLIT + j:k * SC_SPLIT + j + 1, :] = slot + j * nslots


def _slot_rows(rtt, pstart, nslots):
    t = rtt.shape[1]
    tl = SLOT_LANES
    return pl.pallas_call(
        functools.partial(_slot_body, nslots=nslots),
        grid=(t // tl,),
        in_specs=[pl.BlockSpec((8, tl), lambda i: (0, i)),
                  pl.BlockSpec((N_EXPERTS, 1), lambda i: (0, 0))],
        out_specs=pl.BlockSpec((TOP_K * SC_SPLIT, tl), lambda i: (0, i)),
        out_shape=jax.ShapeDtypeStruct((TOP_K * SC_SPLIT, t), jnp.int32),
        compiler_params=pltpu.CompilerParams(dimension_semantics=("arbitrary",)),
        name="slot_rows",
    )(rtt, pstart.astype(F32).reshape(N_EXPERTS, 1))


def _expert_body(meta_ref, buf_ref, w1_hbm, w3_hbm, w2_hbm, out_ref, w1_buf, w3_buf, w2_buf, sem):
    i = pl.program_id(0)
    expert = meta_ref[0, i]
    valid = meta_ref[1, i]
    slot = meta_ref[2, i]
    ahead1 = meta_ref[4, i]
    ahead2 = meta_ref[5, i]

    def weight_copies(e, s):
        return (pltpu.make_async_copy(w1_hbm.at[e], w1_buf.at[s], sem.at[s, 0]),
                pltpu.make_async_copy(w3_hbm.at[e], w3_buf.at[s], sem.at[s, 1]),
                pltpu.make_async_copy(w2_hbm.at[e], w2_buf.at[s], sem.at[s, 2]))

    @pl.when(i == 0)
    def _():
        for c in weight_copies(expert, slot):
            c.start()

        @pl.when(ahead1 >= 0)
        def _():
            for c in weight_copies(ahead1, (slot + 1) % WEIGHT_SLOTS):
                c.start()

    @pl.when(meta_ref[3, i] == 1)
    def _():
        for c in weight_copies(expert, slot):
            c.wait()

        @pl.when(ahead2 >= 0)
        def _():
            for c in weight_copies(ahead2, (slot + 2) % WEIGHT_SLOTS):
                c.start()

    def mlp(rows):
        x = _unpack_bf16_pair(jnp.concatenate([buf_ref[j, :rows, :] for j in range(SC_SPLIT)], axis=1))
        row = lax.broadcasted_iota(jnp.int32, x.shape, 0)
        x = jnp.where(row < valid, x, 0.0).astype(BF16)
        h1 = _dot(x, w1_buf[slot].astype(BF16))
        h3 = _dot(x, w3_buf[slot].astype(BF16))
        a = (h1 * _sigmoid(h1) * h3).astype(BF16)
        packed = _pack_bf16_pair(_dot(a, w2_buf[slot].astype(BF16)))
        for j in range(SC_SPLIT):
            out_ref[j, :rows, :] = packed[:, j * SC_ROW:(j + 1) * SC_ROW]
            if rows < BM:
                out_ref[j, rows:, :] = jnp.zeros((BM - rows, SC_ROW), out_ref.dtype)

    @pl.when(valid > BM // 2)
    def _():
        mlp(BM)

    @pl.when((valid > 0) & (valid <= BM // 2))
    def _():
        mlp(BM // 2)

    @pl.when(valid <= 0)
    def _():
        out_ref[...] = jnp.zeros_like(out_ref)


def _expert_meta(block_expert, block_valid):
    nb = block_expert.shape[0]
    pos = jnp.arange(nb, dtype=jnp.int32)
    prev = jnp.concatenate([jnp.full((1,), -1, jnp.int32), block_expert[:-1]])
    first = (block_valid > 0) & (block_expert != prev)
    run = jnp.cumsum(first.astype(jnp.int32)) - 1
    start_pos = jnp.where(first, pos, nb)
    later = jnp.concatenate([start_pos[1:], jnp.full((2,), nb, jnp.int32)])
    nearest = lax.cummin(later, axis=0, reverse=True)
    n1 = nearest[:nb]
    n2 = nearest[n1]
    experts_ext = jnp.concatenate([block_expert, jnp.full((1,), -1, jnp.int32)])
    return jnp.stack([block_expert, block_valid, run % WEIGHT_SLOTS, first.astype(jnp.int32),
                      experts_ext[n1], experts_ext[n2]]).astype(jnp.int32)


def _experts(block_expert, block_valid, buf, w1, w3, w2):
    _, nslots, _ = buf.shape
    nb = nslots // BM
    rows = pl.BlockSpec((SC_SPLIT, BM, SC_ROW), lambda i, meta: (0, i, 0))
    hbm = pl.BlockSpec(memory_space=pl.ANY)
    grid_spec = pltpu.PrefetchScalarGridSpec(
        num_scalar_prefetch=1,
        grid=(nb,),
        in_specs=[rows, hbm, hbm, hbm],
        out_specs=rows,
        scratch_shapes=[pltpu.VMEM((WEIGHT_SLOTS,) + w1.shape[1:], w1.dtype),
                        pltpu.VMEM((WEIGHT_SLOTS,) + w3.shape[1:], w3.dtype),
                        pltpu.VMEM((WEIGHT_SLOTS,) + w2.shape[1:], w2.dtype),
                        pltpu.SemaphoreType.DMA((WEIGHT_SLOTS, 3))],
    )
    return pl.pallas_call(
        _expert_body,
        grid_spec=grid_spec,
        out_shape=jax.ShapeDtypeStruct(buf.shape, jnp.uint32),
        compiler_params=pltpu.CompilerParams(dimension_semantics=("arbitrary",),
                                             vmem_limit_bytes=VMEM_LIMIT),
        name="experts",
    )(_expert_meta(block_expert, block_valid), buf, w1, w3, w2)


def _sc_mesh():
    return plsc.VectorSubcoreMesh(core_axis_name="core", subcore_axis_name="subcore")


def _dispatch_rows(rows, dest0, dest1, nslots):
    t, w = rows.shape
    win = SC_WINDOW
    idx_spec = pl.BlockSpec((1, win), lambda i: (0, i))

    @functools.partial(pl.kernel, mesh=_sc_mesh(), scratch_types=[],
                       out_type=jax.ShapeDtypeStruct((nslots, w), rows.dtype), name="dispatch_rows")
    def run(rows_hbm, i0_hbm, i1_hbm, out_hbm):
        def body(rows_vmem, i0_vmem, i1_vmem):
            pltpu.sync_copy(rows_vmem, out_hbm.at[i0_vmem.at[0]])
            pltpu.sync_copy(rows_vmem, out_hbm.at[i1_vmem.at[0]])

        pltpu.emit_pipeline(
            body, grid=(t // win,),
            in_specs=[pl.BlockSpec((win, w), lambda i: (i, 0)), idx_spec, idx_spec],
            out_specs=[],
            core_axis_name=("core", "subcore"),
            dimension_semantics=(pltpu.PARALLEL,),
        )(rows_hbm, i0_hbm, i1_hbm)

    return run(rows, dest0.reshape(1, t), dest1.reshape(1, t))


def _gather_rows(table, idx):
    n = idx.shape[0]
    w = table.shape[1]
    win = SC_WINDOW

    @functools.partial(pl.kernel, mesh=_sc_mesh(), scratch_types=[],
                       out_type=jax.ShapeDtypeStruct((n, w), table.dtype), name="gather_rows")
    def run(table_hbm, i_hbm, out_hbm):
        def body(i_vmem, out_vmem):
            pltpu.sync_copy(table_hbm.at[i_vmem.at[0]], out_vmem)

        pltpu.emit_pipeline(
            body, grid=(n // win,),
            in_specs=[pl.BlockSpec((1, win), lambda i: (0, i))],
            out_specs=[pl.BlockSpec((win, w), lambda i: (i, 0))],
            core_axis_name=("core", "subcore"),
            dimension_semantics=(pltpu.PARALLEL,),
        )(i_hbm, out_hbm)

    return run(table, idx.reshape(1, n))


def _combine_body(h_ref, g_ref, rt_ref, gfin_ref, out_ref):
    rt = rt_ref[...]
    y = h_ref[...]
    for k in range(TOP_K):
        rows = jnp.concatenate([g_ref[k * SC_SPLIT + j] for j in range(SC_SPLIT)], axis=1)
        y = y + rt[:, 4 + k:5 + k] * _unpack_bf16_pair(rows)
    out_ref[...] = _rms(y, gfin_ref[...])


def _combine(h, g, rt, g_final):
    t, d = h.shape
    tm = TM_OUT
    tok = lambda i: (i, 0)
    return pl.pallas_call(
        _combine_body,
        grid=(t // tm,),
        in_specs=[pl.BlockSpec((tm, d), tok),
                  pl.BlockSpec((TOP_K * SC_SPLIT, tm, SC_ROW), lambda i: (0, i, 0)),
                  pl.BlockSpec((tm, LANES), tok),
                  pl.BlockSpec((1, d), lambda i: (0, 0))],
        out_specs=pl.BlockSpec((tm, d), tok),
        out_shape=jax.ShapeDtypeStruct((t, d), F32),
        compiler_params=pltpu.CompilerParams(dimension_semantics=("arbitrary",),
                                             vmem_limit_bytes=VMEM_LIMIT),
        name="combine",
    )(h, g, rt, g_final)


def _layer(h, mem, g_mix, g_mem, w_in, w_gate, b_gate, lam_re, lam_im, log_dt, b_re, b_im,
           c_re, c_im, d_skip, w_glu, b_glu, g_sgu, w_spatial, b_spatial, w_kv, w_branch,
           w_out, g_ffn, w_group, b_group, w_router, b_router, w1, w3, w2, g_out):
    bsz, s, d = h.shape
    t = bsz * s
    row = lambda a: a.reshape(1, -1)

    k, v = _kv_proj(mem, row(g_mem), w_kv.astype(BF16))

    tril = jnp.tril(jnp.ones((CHUNK, CHUNK), dtype=bool))
    w_sp = jnp.where(tril, w_spatial, 0.0).astype(BF16)
    b_sp = jnp.broadcast_to(b_spatial[:, :, None], (SGU_HEADS, CHUNK, SGU_HEAD_DIM))
    u2, y_b, y_c = _in_proj(h, row(g_mix), w_in.astype(BF16), row(g_sgu), w_sp, b_sp, k, v)
    y2 = _ssm(u2, *_ssm_params(lam_re, lam_im, log_dt, b_re, b_im, c_re, c_im, d_skip), n_seq=bsz)

    pad = LANES - N_GROUPS - N_EXPERTS
    w_rt = jnp.concatenate([w_group, w_router, jnp.zeros((d, pad), F32)], axis=1)
    w_rt_hi = w_rt.astype(BF16)
    w_rt = jnp.concatenate([w_rt_hi, (w_rt - w_rt_hi.astype(F32)).astype(BF16)], axis=1)
    b_rt =jnp.concatenate([b_group, b_router, jnp.zeros((pad,), F32)]).reshape(1, LANES)
    h2, xnp, rt, rtt, cnt = _merge_route(
        h.reshape(t, d), y2, y_b.reshape(t, -1), y_c.reshape(t, -1), row(g_mix),
        w_gate.astype(BF16), row(b_gate), w_glu.astype(BF16), row(b_glu),
        w_branch.astype(BF16), w_out.astype(BF16), row(g_ffn), w_rt, b_rt)

    counts = cnt[0, ROUTE_LANE0:ROUTE_LANE0 + N_EXPERTS].astype(jnp.int32)
    padded = (counts + BM - 1) // BM * BM
    block_end = jnp.cumsum(padded)
    pstart = block_end - padded
    nb = (t * TOP_K) // BM + N_EXPERTS
    blk_row0 = jnp.arange(nb, dtype=jnp.int32) * BM
    block_expert = jnp.sum((block_end[None, :] <= blk_row0[:, None]).astype(jnp.int32), axis=1)
    block_expert = jnp.minimum(block_expert, N_EXPERTS - 1)
    block_valid = jnp.clip(counts[block_expert] - (blk_row0 - pstart[block_expert]), 0, BM)
    block_valid = jnp.where(blk_row0 < block_end[-1], block_valid, 0).astype(jnp.int32)
    nslots = nb * BM
    dest_p = _slot_rows(rtt, pstart, nslots).reshape(TOP_K, SC_SPLIT * t)
    buf = _dispatch_rows(xnp.reshape(SC_SPLIT * t, SC_ROW), dest_p[0], dest_p[1], SC_SPLIT * nslots)
    yb = _experts(block_expert, block_valid, buf.reshape(SC_SPLIT, nslots, SC_ROW), w1, w3, w2)
    g = _gather_rows(yb.reshape(SC_SPLIT * nslots, SC_ROW), dest_p.reshape(-1))
    out = _combine(h2, g.reshape(TOP_K * SC_SPLIT, t, SC_ROW), rt, row(g_out))
    return out.reshape(bsz, s, d)


def kernel(x, mem, g_mix, g_mem, w_in, w_gate, b_gate, lam_re, lam_im, log_dt, b_re, b_im, c_re,
           c_im, d_skip, w_glu, b_glu, g_sgu, w_spatial, b_spatial, w_kv, w_branch, w_out, g_ffn,
           w_group, b_group, w_router, b_router, w1, w3, w2, g_final):
    assert g_mix.shape[0] == 1, "single-layer stack"
    return _layer(x, mem, g_mix[0], g_mem[0], w_in[0], w_gate[0], b_gate[0], lam_re[0], lam_im[0],
                  log_dt[0], b_re[0], b_im[0], c_re[0], c_im[0], d_skip[0], w_glu[0], b_glu[0],
                  g_sgu[0], w_spatial[0], b_spatial[0], w_kv[0], w_branch[0], w_out[0], g_ffn[0],
                  w_group[0], b_group[0], w_router[0], b_router[0], w1[0], w3[0], w2[0], g_final)
```

```python
import functools
import math

import jax
import jax.numpy as jnp
from jax import lax
from jax.experimental import pallas as pl
from jax.experimental.pallas import tpu as pltpu
from jax.experimental.pallas import tpu_sc as plsc

F32 = jnp.float32
BF16 = jnp.bfloat16

EPS = 1e-6
D_MODEL = 1024
SSM_WIDTH = 512
SSM_GROUP = 16
SSM_GROUPS = 32
SSM_STATE = 64
SSM_CHUNK = 16
SGU_WIDTH = 512
SGU_HEADS = 4
SGU_HEAD_DIM = 128
CHUNK = 128
XA_HEADS = 4
XA_HEAD_DIM = 128
N_GROUPS = 8
EXPERTS_PER_GROUP = 8
N_EXPERTS = 64
TOP_K = 2
D_FF = 512
LANES = 128
ROUTE_LANE0 = N_GROUPS

TM_IN = 512
TM_MERGE = 512
MERGE_COLS = 256
TM_OUT = 512
BM = 512
WEIGHT_SLOTS = 3
SC_WINDOW = 128
SC_ROW = 256
SC_SPLIT = (D_MODEL // 2) // SC_ROW
SLOT_LANES = 2048
VMEM_LIMIT = 56 * 1024 * 1024


def _rms(x, g):
    return x * lax.rsqrt(jnp.mean(x * x, axis=-1, keepdims=True) + EPS) * g


def _sigmoid(x):
    return 0.5 * (1.0 + jnp.tanh(0.5 * x))


def _gelu(x):
    c = math.sqrt(2.0 / math.pi)
    return 0.5 * x * (1.0 + jnp.tanh(c * (x + 0.044715 * (x * x * x))))


def _dot(a, b):
    return jnp.dot(a, b, preferred_element_type=F32)


_NT = (((1,), (1,)), ((), ()))


def _pack_bf16_pair(x):
    n = x.shape[1] // 2
    lo = lax.bitcast_convert_type(x[:, :n].astype(BF16).astype(F32), jnp.uint32)
    hi = lax.bitcast_convert_type(x[:, n:].astype(BF16).astype(F32), jnp.uint32)
    return hi | (lo >> 16)


def _unpack_bf16_pair(p):
    lo = lax.bitcast_convert_type(p << 16, F32)
    hi = lax.bitcast_convert_type(p & jnp.uint32(0xFFFF0000), F32)
    return jnp.concatenate([lo, hi], axis=1)


GROUPS_PER_TILE = LANES // SSM_GROUP
POS_PER_TILE = LANES // SSM_GROUP


def _slot_masks(rows):
    lane = lax.broadcasted_iota(jnp.int32, (rows, LANES), 1)
    return [(lane >= i * SSM_GROUP) & (lane < (i + 1) * SSM_GROUP) for i in range(LANES // SSM_GROUP)]


def _tokens_to_chunks(tok_ref, out_ref):
    tm = tok_ref.shape[1]
    nc = tm // SSM_CHUNK
    masks = _slot_masks(nc)
    for k in range(SSM_WIDTH // LANES):
        for j in range(SSM_CHUNK // POS_PER_TILE):
            src = [tok_ref[k, pl.ds(j * POS_PER_TILE + p, nc, stride=SSM_CHUNK), :]
                   for p in range(POS_PER_TILE)]
            for gi in range(GROUPS_PER_TILE):
                acc = None
                for p in range(POS_PER_TILE):
                    shift = ((p - gi) * SSM_GROUP) % LANES
                    r = pltpu.roll(src[p], shift, 1) if shift else src[p]
                    acc = r if acc is None else jnp.where(masks[p], r, acc)
                out_ref[k * GROUPS_PER_TILE + gi, :, pl.ds(j * LANES, LANES)] = acc.astype(out_ref.dtype)


def _chunks_to_tokens(chunk_ref, tok_ref):
    tm = tok_ref.shape[1]
    nc = tm // SSM_CHUNK
    masks = _slot_masks(nc)
    for k in range(SSM_WIDTH // LANES):
        for j in range(SSM_CHUNK // POS_PER_TILE):
            src = [chunk_ref[k * GROUPS_PER_TILE + gi, :, pl.ds(j * LANES, LANES)].astype(F32)
                   for gi in range(GROUPS_PER_TILE)]
            for p in range(POS_PER_TILE):
                acc = None
                for gi in range(GROUPS_PER_TILE):
                    shift = ((gi - p) * SSM_GROUP) % LANES
                    r = pltpu.roll(src[gi], shift, 1) if shift else src[gi]
                    acc = r if acc is None else jnp.where(masks[gi], r, acc)
                tok_ref[k, pl.ds(j * POS_PER_TILE + p, nc, stride=SSM_CHUNK), :] = acc


def _kv_body(mem_ref, g_ref, w_ref, k_ref, v_ref):
    n = _rms(mem_ref[0], g_ref[...]).astype(BF16)
    kv = _dot(n, w_ref[...])
    k_ref[0] = kv[:, :SGU_WIDTH].astype(BF16)
    v_ref[0] = kv[:, SGU_WIDTH:].astype(BF16)


def _kv_proj(mem, g_mem, w_kv):
    b, m, d = mem.shape
    return pl.pallas_call(
        _kv_body,
        grid=(b,),
        in_specs=[pl.BlockSpec((1, m, d), lambda i: (i, 0, 0)),
                  pl.BlockSpec((1, d), lambda i: (0, 0)),
                  pl.BlockSpec((d, 2 * SGU_WIDTH), lambda i: (0, 0))],
        out_specs=[pl.BlockSpec((1, m, SGU_WIDTH), lambda i: (i, 0, 0)),
                   pl.BlockSpec((1, m, SGU_WIDTH), lambda i: (i, 0, 0))],
        out_shape=[jax.ShapeDtypeStruct((b, m, SGU_WIDTH), BF16),
                   jax.ShapeDtypeStruct((b, m, SGU_WIDTH), BF16)],
        compiler_params=pltpu.CompilerParams(dimension_semantics=("arbitrary",),
                                             vmem_limit_bytes=VMEM_LIMIT),
        name="kv_proj",
    )(mem, g_mem, w_kv)


def _in_body(x_ref, gmix_ref, win_ref, gsgu_ref, wsp_ref, bsp_ref, k_ref, v_ref,
             u2_ref, yb_ref, yc_ref, tok_ref):
    n = _rms(x_ref[0], gmix_ref[...]).astype(BF16)
    proj = _dot(n, win_ref[...])
    for k in range(SSM_WIDTH // LANES):
        tok_ref[k] = proj[:, k * LANES:(k + 1) * LANES]
    _tokens_to_chunks(tok_ref, u2_ref)

    u = _gelu(proj[:, SSM_WIDTH:SSM_WIDTH + SGU_WIDTH])
    v = _gelu(proj[:, SSM_WIDTH + SGU_WIDTH:SSM_WIDTH + 2 * SGU_WIDTH])
    v = _rms(v, gsgu_ref[...]).astype(BF16)
    tm = u.shape[0]
    rows = []
    for c in range(tm // CHUNK):
        vc = v[c * CHUNK:(c + 1) * CHUNK]
        heads = []
        for h in range(SGU_HEADS):
            sl = slice(h * SGU_HEAD_DIM, (h + 1) * SGU_HEAD_DIM)
            heads.append(_dot(wsp_ref[h], vc[:, sl]) + bsp_ref[h])
        rows.append(jnp.concatenate(heads, axis=1))
    sv = jnp.concatenate(rows, axis=0)
    yb_ref[0] = (u * sv).astype(BF16)

    q = proj[:, SSM_WIDTH + 2 * SGU_WIDTH:].astype(BF16)
    kk = k_ref[0]
    vv = v_ref[0]
    outs = []
    for h in range(XA_HEADS):
        sl = slice(h * XA_HEAD_DIM, (h + 1) * XA_HEAD_DIM)
        s = lax.dot_general(q[:, sl], kk[:, sl], (((1,), (1,)), ((), ())),
                            preferred_element_type=F32) * (XA_HEAD_DIM ** -0.5)
        e = jnp.exp(s - jnp.max(s, axis=-1, keepdims=True))
        l = jnp.sum(e, axis=-1, keepdims=True)
        outs.append(_dot(e.astype(BF16), vv[:, sl]) / l)
    yc_ref[0] = jnp.concatenate(outs, axis=1).astype(BF16)


def _in_proj(x, g_mix, w_in, g_sgu, w_sp, b_sp, k, v):
    b, s, d = x.shape
    m = k.shape[1]
    const2 = lambda i, j: (0, 0)
    const3 = lambda i, j: (0, 0, 0)
    tok = lambda i, j: (i, j, 0)
    per_b = lambda i, j: (i, 0, 0)
    out = jax.ShapeDtypeStruct((b, s, SSM_WIDTH), BF16)
    nc = TM_IN // SSM_CHUNK
    tiles = s // TM_IN
    u2 = jax.ShapeDtypeStruct((SSM_GROUPS, b * s // SSM_CHUNK, SSM_CHUNK * SSM_GROUP), BF16)
    return pl.pallas_call(
        _in_body,
        grid=(b, s // TM_IN),
        in_specs=[pl.BlockSpec((1, TM_IN, d), tok),
                  pl.BlockSpec((1, d), const2),
                  pl.BlockSpec(w_in.shape, const2),
                  pl.BlockSpec((1, SGU_WIDTH), const2),
                  pl.BlockSpec(w_sp.shape, const3),
                  pl.BlockSpec(b_sp.shape, const3),
                  pl.BlockSpec((1, m, SGU_WIDTH), per_b),
                  pl.BlockSpec((1, m, SGU_WIDTH), per_b)],
        out_specs=[pl.BlockSpec((SSM_GROUPS, nc, SSM_CHUNK * SSM_GROUP), lambda i, j: (0, i * tiles + j, 0)),
                   pl.BlockSpec((1, TM_IN, SSM_WIDTH), tok),
                   pl.BlockSpec((1, TM_IN, SSM_WIDTH), tok)],
        out_shape=[u2, out, out],
        scratch_shapes=[pltpu.VMEM((SSM_WIDTH // LANES, TM_IN, LANES), F32)],
        compiler_params=pltpu.CompilerParams(dimension_semantics=("arbitrary", "arbitrary"),
                                             vmem_limit_bytes=VMEM_LIMIT),
        name="in_proj",
    )(x, g_mix, w_in, g_sgu, w_sp, b_sp, k, v)


def _ssm_params(lam_re, lam_im, log_dt, b_re, b_im, c_re, c_im, d_skip):
    g, p = lam_re.shape
    dt = jnp.exp(log_dt)[:, None]
    ar = lam_re * dt
    ai = lam_im * dt

    def cpow(j):
        mag = jnp.exp(ar[:, None, :] * j[None, :, None])
        ph = ai[:, None, :] * j[None, :, None]
        return mag * jnp.cos(ph), mag * jnp.sin(ph)

    pr, pi = cpow(jnp.arange(SSM_CHUNK + 1, dtype=F32))
    nr = pr[:, 1] - 1.0
    ni = pi[:, 1]
    den = lam_re * lam_re + lam_im * lam_im
    fr = (nr * lam_re + ni * lam_im) / den
    fi = (ni * lam_re - nr * lam_im) / den
    bbr = fr[..., None] * b_re - fi[..., None] * b_im
    bbi = fr[..., None] * b_im + fi[..., None] * b_re

    L = SSM_CHUNK
    cr = c_re[:, None]
    ci = c_im[:, None]
    cpr = cr * pr[:, :L, None, :] - ci * pi[:, :L, None, :]
    cpi = cr * pi[:, :L, None, :] + ci * pr[:, :L, None, :]
    ccat = jnp.concatenate([cpr, -cpi], axis=-1).reshape(g, L * SSM_GROUP, 2 * p)
    bcat = jnp.concatenate([bbr, bbi], axis=1).transpose(0, 2, 1)

    rr = pr[:, L - 1::-1][:, :L]
    ri = pi[:, L - 1::-1][:, :L]
    bbr_t = bbr.transpose(0, 2, 1)[:, None]
    bbi_t = bbi.transpose(0, 2, 1)[:, None]
    n_re = rr[:, :, None, :] * bbr_t - ri[:, :, None, :] * bbi_t
    n_im = rr[:, :, None, :] * bbi_t + ri[:, :, None, :] * bbr_t
    nmat = jnp.concatenate([n_re, n_im], axis=-1).reshape(g, L * SSM_GROUP, 2 * p)

    clr = cr * pr[:, 1:, None, :] - ci * pi[:, 1:, None, :]
    cli = cr * pi[:, 1:, None, :] + ci * pr[:, 1:, None, :]
    mmat = jnp.concatenate([clr, -cli], axis=-1).reshape(g, L * SSM_GROUP, 2 * p)

    qr, qi = cpow(jnp.asarray([float(L * 2 ** k) for k in range(8)], F32))
    lr = jnp.concatenate([qr, qr], axis=-1)
    li = jnp.concatenate([-qi, qi], axis=-1)
    d2 = jnp.tile(d_skip.reshape(g, 1, SSM_GROUP), (1, 1, L))
    return bcat, ccat, nmat.astype(BF16), mmat.astype(BF16), lr, li, d2


def _ssm_body(u_ref, bcat_ref, ccat_ref, n_ref, m_ref, lr_ref, li_ref, d2_ref, y_ref, toep_ref, *, n_seq):
    kern = lax.dot_general(bcat_ref[0], ccat_ref[0], _NT, precision=lax.Precision.HIGHEST,
                           preferred_element_type=F32)
    col = lax.broadcasted_iota(jnp.int32, kern.shape, 1)
    for s in range(SSM_CHUNK):
        shifted = pltpu.roll(kern, s * SSM_GROUP, 1) if s else kern
        toep_ref[s * SSM_GROUP:(s + 1) * SSM_GROUP, :] = jnp.where(
            col >= s * SSM_GROUP, shifted, 0.0).astype(BF16)

    u = u_ref[0]
    rows = u.shape[0]
    per = rows // n_seq
    y = _dot(u, toep_ref[...])
    st = _dot(u, n_ref[0])
    row = lax.broadcasted_iota(jnp.int32, (per, LANES), 0)
    prev = []
    for b in range(n_seq):
        x = st[b * per:(b + 1) * per]
        k = 0
        while (1 << k) < per:
            d = 1 << k
            sh = jnp.where(row >= d, pltpu.roll(x, d, 0), 0.0)
            x = x + sh * lr_ref[0, k:k + 1, :] + pltpu.roll(sh, SSM_STATE, 1) * li_ref[0, k:k + 1, :]
            k += 1
        prev.append(jnp.where(row >= 1, pltpu.roll(x, 1, 0), 0.0))
    xp = jnp.concatenate(prev, axis=0).astype(BF16)
    y = y + lax.dot_general(xp, m_ref[0], _NT, preferred_element_type=F32) + d2_ref[0] * u.astype(F32)
    y_ref[0] = _gelu(y).astype(BF16)


def _ssm(u2, bcat, ccat, nmat, mmat, lr, li, d2, n_seq):
    g, rows, w = u2.shape
    blk = lambda a: pl.BlockSpec((1,) + a.shape[1:], lambda i: (i, 0, 0))
    return pl.pallas_call(
        functools.partial(_ssm_body, n_seq=n_seq),
        grid=(g,),
        in_specs=[blk(u2), blk(bcat), blk(ccat), blk(nmat), blk(mmat), blk(lr), blk(li), blk(d2)],
        out_specs=blk(u2),
        out_shape=jax.ShapeDtypeStruct(u2.shape, BF16),
        scratch_shapes=[pltpu.VMEM((w, w), BF16)],
        compiler_params=pltpu.CompilerParams(dimension_semantics=("arbitrary",),
                                             vmem_limit_bytes=VMEM_LIMIT),
        name="ssm",
    )(u2, bcat, ccat, nmat, mmat, lr, li, d2)


def _mix_tile(x_ref, y2_ref, yb_ref, yc_ref, gmix_ref, wgate_ref, bgate_ref, wglu_ref, bglu_ref,
              wbr_ref, wout_ref, tok_ref, h_ref, hkeep_ref):
    x = x_ref[...]
    n = _rms(x, gmix_ref[...]).astype(BF16)

    def gated(b, c, y):
        cols = pl.ds(b * D_MODEL + c * MERGE_COLS, MERGE_COLS)
        gate = _sigmoid(_dot(n, wgate_ref[:, cols]) + bgate_ref[:, cols])
        return gate * _dot(y, wbr_ref[b, :, pl.ds(c * MERGE_COLS, MERGE_COLS)])

    n_blocks = D_MODEL // MERGE_COLS
    yb = yb_ref[...]
    yc = yc_ref[...]
    head = gated(1, 0, yb) + gated(2, 0, yc)
    yield
    _chunks_to_tokens(y2_ref, tok_ref)
    ys = jnp.concatenate([tok_ref[k] for k in range(SSM_WIDTH // LANES)], axis=1).astype(BF16)
    glu = _dot(ys, wglu_ref[...]) + bglu_ref[...]
    ya = (glu[:, :SSM_WIDTH] * _sigmoid(glu[:, SSM_WIDTH:])).astype(BF16)
    merged = [(head + gated(0, 0, ya)).astype(BF16)]
    for c in range(1, n_blocks):
        yield
        merged.append((gated(0, c, ya) + gated(1, c, yb) + gated(2, c, yc)).astype(BF16))
    yield
    h = x + _dot(jnp.concatenate(merged, axis=1), wout_ref[...])
    h_ref[...] = h
    hkeep_ref[...] = h


def _route_tile(hkeep_ref, gffn_ref, wrt_ref, brt_ref, xnp_ref, rt_ref, rtt_ref, cnt_ref, carry_ref):
    h = hkeep_ref[...]
    xn = _rms(h, gffn_ref[...])
    packed = _pack_bf16_pair(xn)
    for j in range(SC_SPLIT):
        xnp_ref[j] = packed[:, j * SC_ROW:(j + 1) * SC_ROW]
    yield

    x_hi = xn.astype(BF16)
    x_lo = (xn - x_hi.astype(F32)).astype(BF16)
    head = _dot(x_hi, wrt_ref[...])
    logits = (head[:, :LANES] + head[:, LANES:] + _dot(x_lo, wrt_ref[:, :LANES])) + brt_ref[...]
    yield
    tm = logits.shape[0]
    lane_i = lax.broadcasted_iota(jnp.int32, (tm, LANES), 1)
    lane = lane_i.astype(F32)
    neg = jnp.float32(-3.0e38)
    big = jnp.float32(LANES)
    gmask = lane_i < N_GROUPS
    gl = jnp.where(gmask, logits, neg)
    gmax = jnp.max(gl, axis=-1, keepdims=True)
    gidx = jnp.min(jnp.where(gl == gmax, lane, big), axis=-1, keepdims=True)
    gsum = jnp.sum(jnp.where(gmask, jnp.exp(gl - gmax), 0.0), axis=-1, keepdims=True)
    g_w = 1.0 / gsum
    e_lane = lane_i - ROUTE_LANE0
    lane_group = (e_lane >> 3).astype(F32)
    emask = (e_lane >= 0) & (e_lane < N_EXPERTS) & (lane_group == gidx)
    el = jnp.where(emask, logits, neg)
    m1 = jnp.max(el, axis=-1, keepdims=True)
    i1 = jnp.min(jnp.where(el == m1, lane, big), axis=-1, keepdims=True)
    el2 = jnp.where(lane == i1, neg, el)
    m2 = jnp.max(el2, axis=-1, keepdims=True)
    i2 = jnp.min(jnp.where(el2 == m2, lane, big), axis=-1, keepdims=True)
    t = jnp.exp(m2 - m1)
    w1 = g_w / (1.0 + t)
    w2 = g_w * t / (1.0 + t)
    yield

    sel1 = lane == i1
    sel2 = lane == i2
    onehot = jnp.where(sel1 | sel2, 1.0, 0.0)
    r_i = lax.broadcasted_iota(jnp.int32, (tm, tm), 0)
    c_i = lax.broadcasted_iota(jnp.int32, (tm, tm), 1)
    stril = jnp.where(c_i < r_i, 1.0, 0.0).astype(BF16)
    cum = _dot(stril, onehot.astype(BF16)) + carry_ref[0:1, :]
    rank1 = jnp.sum(jnp.where(sel1, cum, 0.0), axis=-1, keepdims=True)
    rank2 = jnp.sum(jnp.where(sel2, cum, 0.0), axis=-1, keepdims=True)
    carry_ref[...] = carry_ref[...] + jnp.sum(onehot, axis=0, keepdims=True)
    cnt_ref[...] = carry_ref[...]
    yield

    cols = (i1 - ROUTE_LANE0, i2 - ROUTE_LANE0, rank1, rank2, w1, w2)
    rt = jnp.zeros((tm, LANES), F32)
    for c, val in enumerate(cols):
        rt = jnp.where(lane_i == c, val, rt)
    rt_ref[...] = rt
    rtt_ref[...] = rt.T[:8]


def _merge_body(x_ref, y2_ref, yb_ref, yc_ref, gmix_ref, wgate_ref, bgate_ref, wglu_ref, bglu_ref,
                wbr_ref, wout_ref, gffn_ref, wrt_ref, brt_ref,
                h_ref, xnp_ref, rt_ref, rtt_ref, cnt_ref, carry_ref, tok_ref, hkeep_ref):
    i = pl.program_id(0)
    last = pl.num_programs(0) - 1
    cur = hkeep_ref.at[i % 2]
    prev = hkeep_ref.at[(i + 1) % 2]

    def mix():
        return _mix_tile(x_ref, y2_ref, yb_ref, yc_ref, gmix_ref, wgate_ref, bgate_ref, wglu_ref,
                         bglu_ref, wbr_ref, wout_ref, tok_ref, h_ref, cur)

    def route():
        return _route_tile(prev, gffn_ref, wrt_ref, brt_ref, xnp_ref, rt_ref, rtt_ref, cnt_ref, carry_ref)

    @pl.when(i == 0)
    def _():
        carry_ref[...] = jnp.zeros_like(carry_ref)
        for _ in mix():
            pass

    @pl.when((i > 0) & (i < last))
    def _():
        stages = [route(), mix()]
        while stages:
            stages = [s for s in stages if next(s, True) is None]

    @pl.when(i == last)
    def _():
        for _ in route():
            pass


def _merge_route(x, y2, yb, yc, g_mix, w_gate, b_gate, w_glu, b_glu, w_br, w_out, g_ffn, w_rt, b_rt):
    t, d = x.shape
    tm = TM_MERGE
    tiles = t // tm
    mixed = lambda i: jnp.minimum(i, tiles - 1)
    routed = lambda i: jnp.maximum(i - 1, 0)
    c2 = lambda i: (0, 0)
    c3 = lambda i: (0, 0, 0)
    full = lambda a: pl.BlockSpec(a.shape, c2 if a.ndim == 2 else c3)
    return pl.pallas_call(
        _merge_body,
        grid=(tiles + 1,),
        in_specs=[pl.BlockSpec((tm, d), lambda i: (mixed(i), 0)),
                  pl.BlockSpec((SSM_GROUPS, tm // SSM_CHUNK, SSM_CHUNK * SSM_GROUP),
                               lambda i: (0, mixed(i), 0)),
                  pl.BlockSpec((tm, SSM_WIDTH), lambda i: (mixed(i), 0)),
                  pl.BlockSpec((tm, SSM_WIDTH), lambda i: (mixed(i), 0)),
                  full(g_mix), full(w_gate), full(b_gate), full(w_glu), full(b_glu),
                  full(w_br), full(w_out), full(g_ffn), full(w_rt), full(b_rt)],
        out_specs=[pl.BlockSpec((tm, d), lambda i: (mixed(i), 0)),
                   pl.BlockSpec((SC_SPLIT, tm, SC_ROW), lambda i: (0, routed(i), 0)),
                   pl.BlockSpec((tm, LANES), lambda i: (routed(i), 0)),
                   pl.BlockSpec((8, tm), lambda i: (0, routed(i))),
                   pl.BlockSpec((8, LANES), c2)],
        out_shape=[jax.ShapeDtypeStruct((t, d), F32),
                   jax.ShapeDtypeStruct((SC_SPLIT, t, SC_ROW), jnp.uint32),
                   jax.ShapeDtypeStruct((t, LANES), F32),
                   jax.ShapeDtypeStruct((8, t), F32),
                   jax.ShapeDtypeStruct((8, LANES), F32)],
        scratch_shapes=[pltpu.VMEM((8, LANES), F32),
                        pltpu.VMEM((SSM_WIDTH // LANES, tm, LANES), F32),
                        pltpu.VMEM((2, tm, d), F32)],
        compiler_params=pltpu.CompilerParams(dimension_semantics=("arbitrary",),
                                             vmem_limit_bytes=VMEM_LIMIT),
        name="merge_route",
    )(x, y2, yb, yc, g_mix, w_gate, b_gate, w_glu, b_glu, w_br, w_out, g_ffn, w_rt, b_rt)


def _slot_body(rtt_ref, pstart_ref, out_ref, *, nslots):
    tl = rtt_ref.shape[1]
    expert = lax.broadcasted_iota(jnp.int32, (N_EXPERTS, tl), 0).astype(F32)
    for k in range(TOP_K):
        onehot = rtt_ref[k:k + 1, :] == expert
        start = jnp.sum(jnp.where(onehot, pstart_ref[...], 0.0), axis=0, keepdims=True)
        slot = (start + rtt_ref[TOP_K + k:TOP_K + k + 1, :]).astype(jnp.int32)
        for j in range(SC_SPLIT):
            out_ref[k * SC_SPLIT + j:k * SC_SPLIT + j + 1, :] = slot + j * nslots


def _slot_rows(rtt, pstart, nslots):
    t = rtt.shape[1]
    tl = SLOT_LANES
    return pl.pallas_call(
        functools.partial(_slot_body, nslots=nslots),
        grid=(t // tl,),
        in_specs=[pl.BlockSpec((8, tl), lambda i: (0, i)),
                  pl.BlockSpec((N_EXPERTS, 1), lambda i: (0, 0))],
        out_specs=pl.BlockSpec((TOP_K * SC_SPLIT, tl), lambda i: (0, i)),
        out_shape=jax.ShapeDtypeStruct((TOP_K * SC_SPLIT, t), jnp.int32),
        compiler_params=pltpu.CompilerParams(dimension_semantics=("arbitrary",)),
        name="slot_rows",
    )(rtt, pstart.astype(F32).reshape(N_EXPERTS, 1))


def _expert_body(meta_ref, buf_ref, w1_hbm, w3_hbm, w2_hbm, out_ref, w1_buf, w3_buf, w2_buf, sem):
    i = pl.program_id(0)
    expert = meta_ref[0, i]
    valid = meta_ref[1, i]
    slot = meta_ref[2, i]
    ahead1 = meta_ref[4, i]
    ahead2 = meta_ref[5, i]

    def weight_copies(e, s):
        return (pltpu.make_async_copy(w1_hbm.at[e], w1_buf.at[s], sem.at[s, 0]),
                pltpu.make_async_copy(w3_hbm.at[e], w3_buf.at[s], sem.at[s, 1]),
                pltpu.make_async_copy(w2_hbm.at[e], w2_buf.at[s], sem.at[s, 2]))

    @pl.when(i == 0)
    def _():
        for c in weight_copies(expert, slot):
            c.start()

        @pl.when(ahead1 >= 0)
        def _():
            for c in weight_copies(ahead1, (slot + 1) % WEIGHT_SLOTS):
                c.start()

    @pl.when(meta_ref[3, i] == 1)
    def _():
        for c in weight_copies(expert, slot):
            c.wait()

        @pl.when(ahead2 >= 0)
        def _():
            for c in weight_copies(ahead2, (slot + 2) % WEIGHT_SLOTS):
                c.start()

    def mlp(rows):
        x = _unpack_bf16_pair(jnp.concatenate([buf_ref[j, :rows, :] for j in range(SC_SPLIT)], axis=1))
        row = lax.broadcasted_iota(jnp.int32, x.shape, 0)
        x = jnp.where(row < valid, x, 0.0).astype(BF16)
        h1 = _dot(x, w1_buf[slot].astype(BF16))
        h3 = _dot(x, w3_buf[slot].astype(BF16))
        a = (h1 * _sigmoid(h1) * h3).astype(BF16)
        packed = _pack_bf16_pair(_dot(a, w2_buf[slot].astype(BF16)))
        for j in range(SC_SPLIT):
            out_ref[j, :rows, :] = packed[:, j * SC_ROW:(j + 1) * SC_ROW]
            if rows < BM:
                out_ref[j, rows:, :] = jnp.zeros((BM - rows, SC_ROW), out_ref.dtype)

    @pl.when(valid > BM // 2)
    def _():
        mlp(BM)

    @pl.when((valid > 0) & (valid <= BM // 2))
    def _():
        mlp(BM // 2)

    @pl.when(valid <= 0)
    def _():
        out_ref[...] = jnp.zeros_like(out_ref)


def _expert_meta(block_expert, block_valid):
    nb = block_expert.shape[0]
    pos = jnp.arange(nb, dtype=jnp.int32)
    prev = jnp.concatenate([jnp.full((1,), -1, jnp.int32), block_expert[:-1]])
    first = (block_valid > 0) & (block_expert != prev)
    run = jnp.cumsum(first.astype(jnp.int32)) - 1
    start_pos = jnp.where(first, pos, nb)
    later = jnp.concatenate([start_pos[1:], jnp.full((2,), nb, jnp.int32)])
    nearest = lax.cummin(later, axis=0, reverse=True)
    n1 = nearest[:nb]
    n2 = nearest[n1]
    experts_ext = jnp.concatenate([block_expert, jnp.full((1,), -1, jnp.int32)])
    return jnp.stack([block_expert, block_valid, run % WEIGHT_SLOTS, first.astype(jnp.int32),
                      experts_ext[n1], experts_ext[n2]]).astype(jnp.int32)


def _experts(block_expert, block_valid, buf, w1, w3, w2):
    _, nslots, _ = buf.shape
    nb = nslots // BM
    rows = pl.BlockSpec((SC_SPLIT, BM, SC_ROW), lambda i, meta: (0, i, 0))
    hbm = pl.BlockSpec(memory_space=pl.ANY)
    grid_spec = pltpu.PrefetchScalarGridSpec(
        num_scalar_prefetch=1,
        grid=(nb,),
        in_specs=[rows, hbm, hbm, hbm],
        out_specs=rows,
        scratch_shapes=[pltpu.VMEM((WEIGHT_SLOTS,) + w1.shape[1:], w1.dtype),
                        pltpu.VMEM((WEIGHT_SLOTS,) + w3.shape[1:], w3.dtype),
                        pltpu.VMEM((WEIGHT_SLOTS,) + w2.shape[1:], w2.dtype),
                        pltpu.SemaphoreType.DMA((WEIGHT_SLOTS, 3))],
    )
    return pl.pallas_call(
        _expert_body,
        grid_spec=grid_spec,
        out_shape=jax.ShapeDtypeStruct(buf.shape, jnp.uint32),
        compiler_params=pltpu.CompilerParams(dimension_semantics=("arbitrary",),
                                             vmem_limit_bytes=VMEM_LIMIT),
        name="experts",
    )(_expert_meta(block_expert, block_valid), buf, w1, w3, w2)


def _sc_mesh():
    return plsc.VectorSubcoreMesh(core_axis_name="core", subcore_axis_name="subcore")


def _dispatch_rows(rows, dest0, dest1, nslots):
    t, w = rows.shape
    win = SC_WINDOW
    idx_spec = pl.BlockSpec((1, win), lambda i: (0, i))

    @functools.partial(pl.kernel, mesh=_sc_mesh(), scratch_types=[],
                       out_type=jax.ShapeDtypeStruct((nslots, w), rows.dtype), name="dispatch_rows")
    def run(rows_hbm, i0_hbm, i1_hbm, out_hbm):
        def body(rows_vmem, i0_vmem, i1_vmem):
            pltpu.sync_copy(rows_vmem, out_hbm.at[i0_vmem.at[0]])
            pltpu.sync_copy(rows_vmem, out_hbm.at[i1_vmem.at[0]])

        pltpu.emit_pipeline(
            body, grid=(t // win,),
            in_specs=[pl.BlockSpec((win, w), lambda i: (i, 0)), idx_spec, idx_spec],
            out_specs=[],
            core_axis_name=("core", "subcore"),
            dimension_semantics=(pltpu.PARALLEL,),
        )(rows_hbm, i0_hbm, i1_hbm)

    return run(rows, dest0.reshape(1, t), dest1.reshape(1, t))


def _gather_rows(table, idx):
    n = idx.shape[0]
    w = table.shape[1]
    win = SC_WINDOW

    @functools.partial(pl.kernel, mesh=_sc_mesh(), scratch_types=[],
                       out_type=jax.ShapeDtypeStruct((n, w), table.dtype), name="gather_rows")
    def run(table_hbm, i_hbm, out_hbm):
        def body(i_vmem, out_vmem):
            pltpu.sync_copy(table_hbm.at[i_vmem.at[0]], out_vmem)

        pltpu.emit_pipeline(
            body, grid=(n // win,),
            in_specs=[pl.BlockSpec((1, win), lambda i: (0, i))],
            out_specs=[pl.BlockSpec((win, w), lambda i: (i, 0))],
            core_axis_name=("core", "subcore"),
            dimension_semantics=(pltpu.PARALLEL,),
        )(i_hbm, out_hbm)

    return run(table, idx.reshape(1, n))


def _combine_body(h_ref, g_ref, rt_ref, gfin_ref, out_ref):
    rt = rt_ref[...]
    y = h_ref[...]
    for k in range(TOP_K):
        rows = jnp.concatenate([g_ref[k * SC_SPLIT + j] for j in range(SC_SPLIT)], axis=1)
        y = y + rt[:, 4 + k:5 + k] * _unpack_bf16_pair(rows)
    out_ref[...] = _rms(y, gfin_ref[...])


def _combine(h, g, rt, g_final):
    t, d = h.shape
    tm = TM_OUT
    tok = lambda i: (i, 0)
    return pl.pallas_call(
        _combine_body,
        grid=(t // tm,),
        in_specs=[pl.BlockSpec((tm, d), tok),
                  pl.BlockSpec((TOP_K * SC_SPLIT, tm, SC_ROW), lambda i: (0, i, 0)),
                  pl.BlockSpec((tm, LANES), tok),
                  pl.BlockSpec((1, d), lambda i: (0, 0))],
        out_specs=pl.BlockSpec((tm, d), tok),
        out_shape=jax.ShapeDtypeStruct((t, d), F32),
        compiler_params=pltpu.CompilerParams(dimension_semantics=("arbitrary",),
                                             vmem_limit_bytes=VMEM_LIMIT),
        name="combine",
    )(h, g, rt, g_final)


def _layer(h, mem, g_mix, g_mem, w_in, w_gate, b_gate, lam_re, lam_im, log_dt, b_re, b_im,
           c_re, c_im, d_skip, w_glu, b_glu, g_sgu, w_spatial, b_spatial, w_kv, w_branch,
           w_out, g_ffn, w_group, b_group, w_router, b_router, w1, w3, w2, g_out):
    bsz, s, d = h.shape
    t = bsz * s
    row = lambda a: a.reshape(1, -1)

    k, v = _kv_proj(mem, row(g_mem), w_kv.astype(BF16))

    tril = jnp.tril(jnp.ones((CHUNK, CHUNK), dtype=bool))
    w_sp = jnp.where(tril, w_spatial, 0.0).astype(BF16)
    b_sp = jnp.broadcast_to(b_spatial[:, :, None], (SGU_HEADS, CHUNK, SGU_HEAD_DIM))
    u2, y_b, y_c = _in_proj(h, row(g_mix), w_in.astype(BF16), row(g_sgu), w_sp, b_sp, k, v)
    y2 = _ssm(u2, *_ssm_params(lam_re, lam_im, log_dt, b_re, b_im, c_re, c_im, d_skip), n_seq=bsz)

    pad = LANES - N_GROUPS - N_EXPERTS
    w_rt = jnp.concatenate([w_group, w_router, jnp.zeros((d, pad), F32)], axis=1)
    w_rt_hi = w_rt.astype(BF16)
    w_rt = jnp.concatenate([w_rt_hi, (w_rt - w_rt_hi.astype(F32)).astype(BF16)], axis=1)
    b_rt =jnp.concatenate([b_group, b_router, jnp.zeros((pad,), F32)]).reshape(1, LANES)
    h2, xnp, rt, rtt, cnt = _merge_route(
        h.reshape(t, d), y2, y_b.reshape(t, -1), y_c.reshape(t, -1), row(g_mix),
        w_gate.astype(BF16), row(b_gate), w_glu.astype(BF16), row(b_glu),
        w_branch.astype(BF16), w_out.astype(BF16), row(g_ffn), w_rt, b_rt)

    counts = cnt[0, ROUTE_LANE0:ROUTE_LANE0 + N_EXPERTS].astype(jnp.int32)
    padded = (counts + BM - 1) // BM * BM
    block_end = jnp.cumsum(padded)
    pstart = block_end - padded
    nb = (t * TOP_K) // BM + N_EXPERTS
    blk_row0 = jnp.arange(nb, dtype=jnp.int32) * BM
    block_expert = jnp.sum((block_end[None, :] <= blk_row0[:, None]).astype(jnp.int32), axis=1)
    block_expert = jnp.minimum(block_expert, N_EXPERTS - 1)
    block_valid = jnp.clip(counts[block_expert] - (blk_row0 - pstart[block_expert]), 0, BM)
    block_valid = jnp.where(blk_row0 < block_end[-1], block_valid, 0).astype(jnp.int32)
    nslots = nb * BM
    dest_p = _slot_rows(rtt, pstart, nslots).reshape(TOP_K, SC_SPLIT * t)
    buf = _dispatch_rows(xnp.reshape(SC_SPLIT * t, SC_ROW), dest_p[0], dest_p[1], SC_SPLIT * nslots)
    yb = _experts(block_expert, block_valid, buf.reshape(SC_SPLIT, nslots, SC_ROW), w1, w3, w2)
    g = _gather_rows(yb.reshape(SC_SPLIT * nslots, SC_ROW), dest_p.reshape(-1))
    out = _combine(h2, g.reshape(TOP_K * SC_SPLIT, t, SC_ROW), rt, row(g_out))
    return out.reshape(bsz, s, d)


def kernel(x, mem, g_mix, g_mem, w_in, w_gate, b_gate, lam_re, lam_im, log_dt, b_re, b_im, c_re,
           c_im, d_skip, w_glu, b_glu, g_sgu, w_spatial, b_spatial, w_kv, w_branch, w_out, g_ffn,
           w_group, b_group, w_router, b_router, w1, w3, w2, g_final):
    assert g_mix.shape[0] == 1, "single-layer stack"
    return _layer(x, mem, g_mix[0], g_mem[0], w_in[0], w_gate[0], b_gate[0], lam_re[0], lam_im[0],
                  log_dt[0], b_re[0], b_im[0], c_re[0], c_im[0], d_skip[0], w_glu[0], b_glu[0],
                  g_sgu[0], w_spatial[0], b_spatial[0], w_kv[0], w_branch[0], w_out[0], g_ffn[0],
                  w_group[0], b_group[0], w_router[0], b_router[0], w1[0], w3[0], w2[0], g_final)
```

```python
import functools
import math

import jax
import jax.numpy as jnp
from jax import lax
from jax.experimental import pallas as pl
from jax.experimental.pallas import tpu as pltpu
from jax.experimental.pallas import tpu_sc as plsc

F32 = jnp.float32
BF16 = jnp.bfloat16

EPS = 1e-6
D_MODEL = 1024
SSM_WIDTH = 512
SSM_GROUP = 16
SSM_GROUPS = 32
SSM_STATE = 64
SSM_CHUNK = 16
SGU_WIDTH = 512
SGU_HEADS = 4
SGU_HEAD_DIM = 128
CHUNK = 128
XA_HEADS = 4
XA_HEAD_DIM = 128
N_GROUPS = 8
EXPERTS_PER_GROUP = 8
N_EXPERTS = 64
TOP_K = 2
D_FF = 512
LANES = 128
ROUTE_LANE0 = N_GROUPS

TM_IN = 512
TM_MERGE = 512
MERGE_COLS = 256
TM_OUT = 512
BM = 512
WEIGHT_SLOTS = 3
SC_WINDOW = 128
SC_ROW = 256
SC_SPLIT = (D_MODEL // 2) // SC_ROW
SLOT_LANES = 2048
VMEM_LIMIT = 56 * 1024 * 1024


def _rms(x, g):
    return x * lax.rsqrt(jnp.mean(x * x, axis=-1, keepdims=True) + EPS) * g


def _sigmoid(x):
    return 0.5 * (1.0 + jnp.tanh(0.5 * x))


def _gelu(x):
    c = math.sqrt(2.0 / math.pi)
    return 0.5 * x * (1.0 + jnp.tanh(c * (x + 0.044715 * (x * x * x))))


def _dot(a, b):
    return jnp.dot(a, b, preferred_element_type=F32)


_NT = (((1,), (1,)), ((), ()))


def _pack_bf16_pair(x):
    n = x.shape[1] // 2
    lo = lax.bitcast_convert_type(x[:, :n].astype(BF16).astype(F32), jnp.uint32)
    hi = lax.bitcast_convert_type(x[:, n:].astype(BF16).astype(F32), jnp.uint32)
    return hi | (lo >> 16)


def _unpack_bf16_pair(p):
    lo = lax.bitcast_convert_type(p << 16, F32)
    hi = lax.bitcast_convert_type(p & jnp.uint32(0xFFFF0000), F32)
    return jnp.concatenate([lo, hi], axis=1)


GROUPS_PER_TILE = LANES // SSM_GROUP
POS_PER_TILE = LANES // SSM_GROUP


def _slot_masks(rows):
    lane = lax.broadcasted_iota(jnp.int32, (rows, LANES), 1)
    return [(lane >= i * SSM_GROUP) & (lane < (i + 1) * SSM_GROUP) for i in range(LANES // SSM_GROUP)]


def _tokens_to_chunks(tok_ref, out_ref):
    tm = tok_ref.shape[1]
    nc = tm // SSM_CHUNK
    masks = _slot_masks(nc)
    for k in range(SSM_WIDTH // LANES):
        for j in range(SSM_CHUNK // POS_PER_TILE):
            src = [tok_ref[k, pl.ds(j * POS_PER_TILE + p, nc, stride=SSM_CHUNK), :]
                   for p in range(POS_PER_TILE)]
            for gi in range(GROUPS_PER_TILE):
                acc = None
                for p in range(POS_PER_TILE):
                    shift = ((p - gi) * SSM_GROUP) % LANES
                    r = pltpu.roll(src[p], shift, 1) if shift else src[p]
                    acc = r if acc is None else jnp.where(masks[p], r, acc)
                out_ref[k * GROUPS_PER_TILE + gi, :, pl.ds(j * LANES, LANES)] = acc.astype(out_ref.dtype)


def _chunks_to_tokens(chunk_ref, tok_ref):
    tm = tok_ref.shape[1]
    nc = tm // SSM_CHUNK
    masks = _slot_masks(nc)
    for k in range(SSM_WIDTH // LANES):
        for j in range(SSM_CHUNK // POS_PER_TILE):
            src = [chunk_ref[k * GROUPS_PER_TILE + gi, :, pl.ds(j * LANES, LANES)].astype(F32)
                   for gi in range(GROUPS_PER_TILE)]
            for p in range(POS_PER_TILE):
                acc = None
                for gi in range(GROUPS_PER_TILE):
                    shift = ((gi - p) * SSM_GROUP) % LANES
                    r = pltpu.roll(src[gi], shift, 1) if shift else src[gi]
                    acc = r if acc is None else jnp.where(masks[gi], r, acc)
                tok_ref[k, pl.ds(j * POS_PER_TILE + p, nc, stride=SSM_CHUNK), :] = acc


def _kv_body(mem_ref, g_ref, w_ref, k_ref, v_ref):
    n = _rms(mem_ref[0], g_ref[...]).astype(BF16)
    kv = _dot(n, w_ref[...])
    k_ref[0] = kv[:, :SGU_WIDTH].astype(BF16)
    v_ref[0] = kv[:, SGU_WIDTH:].astype(BF16)


def _kv_proj(mem, g_mem, w_kv):
    b, m, d = mem.shape
    return pl.pallas_call(
        _kv_body,
        grid=(b,),
        in_specs=[pl.BlockSpec((1, m, d), lambda i: (i, 0, 0)),
                  pl.BlockSpec((1, d), lambda i: (0, 0)),
                  pl.BlockSpec((d, 2 * SGU_WIDTH), lambda i: (0, 0))],
        out_specs=[pl.BlockSpec((1, m, SGU_WIDTH), lambda i: (i, 0, 0)),
                   pl.BlockSpec((1, m, SGU_WIDTH), lambda i: (i, 0, 0))],
        out_shape=[jax.ShapeDtypeStruct((b, m, SGU_WIDTH), BF16),
                   jax.ShapeDtypeStruct((b, m, SGU_WIDTH), BF16)],
        compiler_params=pltpu.CompilerParams(dimension_semantics=("arbitrary",),
                                             vmem_limit_bytes=VMEM_LIMIT),
        name="kv_proj",
    )(mem, g_mem, w_kv)


def _in_body(x_ref, gmix_ref, win_ref, gsgu_ref, wsp_ref, bsp_ref, k_ref, v_ref,
             u2_ref, yb_ref, yc_ref, tok_ref):
    n = _rms(x_ref[0], gmix_ref[...]).astype(BF16)
    proj = _dot(n, win_ref[...])
    for k in range(SSM_WIDTH // LANES):
        tok_ref[k] = proj[:, k * LANES:(k + 1) * LANES]
    _tokens_to_chunks(tok_ref, u2_ref)

    u = _gelu(proj[:, SSM_WIDTH:SSM_WIDTH + SGU_WIDTH])
    v = _gelu(proj[:, SSM_WIDTH + SGU_WIDTH:SSM_WIDTH + 2 * SGU_WIDTH])
    v = _rms(v, gsgu_ref[...]).astype(BF16)
    tm = u.shape[0]
    rows = []
    for c in range(tm // CHUNK):
        vc = v[c * CHUNK:(c + 1) * CHUNK]
        heads = []
        for h in range(SGU_HEADS):
            sl = slice(h * SGU_HEAD_DIM, (h + 1) * SGU_HEAD_DIM)
            heads.append(_dot(wsp_ref[h], vc[:, sl]) + bsp_ref[h])
        rows.append(jnp.concatenate(heads, axis=1))
    sv = jnp.concatenate(rows, axis=0)
    yb_ref[0] = (u * sv).astype(BF16)

    q = proj[:, SSM_WIDTH + 2 * SGU_WIDTH:].astype(BF16)
    kk = k_ref[0]
    vv = v_ref[0]
    outs = []
    for h in range(XA_HEADS):
        sl = slice(h * XA_HEAD_DIM, (h + 1) * XA_HEAD_DIM)
        s = lax.dot_general(q[:, sl], kk[:, sl], (((1,), (1,)), ((), ())),
                            preferred_element_type=F32) * (XA_HEAD_DIM ** -0.5)
        e = jnp.exp(s - jnp.max(s, axis=-1, keepdims=True))
        l = jnp.sum(e, axis=-1, keepdims=True)
        outs.append(_dot(e.astype(BF16), vv[:, sl]) / l)
    yc_ref[0] = jnp.concatenate(outs, axis=1).astype(BF16)


def _in_proj(x, g_mix, w_in, g_sgu, w_sp, b_sp, k, v):
    b, s, d = x.shape
    m = k.shape[1]
    const2 = lambda i, j: (0, 0)
    const3 = lambda i, j: (0, 0, 0)
    tok = lambda i, j: (i, j, 0)
    per_b = lambda i, j: (i, 0, 0)
    out = jax.ShapeDtypeStruct((b, s, SSM_WIDTH), BF16)
    nc = TM_IN // SSM_CHUNK
    tiles = s // TM_IN
    u2 = jax.ShapeDtypeStruct((SSM_GROUPS, b * s // SSM_CHUNK, SSM_CHUNK * SSM_GROUP), BF16)
    return pl.pallas_call(
        _in_body,
        grid=(b, s // TM_IN),
        in_specs=[pl.BlockSpec((1, TM_IN, d), tok),
                  pl.BlockSpec((1, d), const2),
                  pl.BlockSpec(w_in.shape, const2),
                  pl.BlockSpec((1, SGU_WIDTH), const2),
                  pl.BlockSpec(w_sp.shape, const3),
                  pl.BlockSpec(b_sp.shape, const3),
                  pl.BlockSpec((1, m, SGU_WIDTH), per_b),
                  pl.BlockSpec((1, m, SGU_WIDTH), per_b)],
        out_specs=[pl.BlockSpec((SSM_GROUPS, nc, SSM_CHUNK * SSM_GROUP), lambda i, j: (0, i * tiles + j, 0)),
                   pl.BlockSpec((1, TM_IN, SSM_WIDTH), tok),
                   pl.BlockSpec((1, TM_IN, SSM_WIDTH), tok)],
        out_shape=[u2, out, out],
        scratch_shapes=[pltpu.VMEM((SSM_WIDTH // LANES, TM_IN, LANES), F32)],
        compiler_params=pltpu.CompilerParams(dimension_semantics=("arbitrary", "arbitrary"),
                                             vmem_limit_bytes=VMEM_LIMIT),
        name="in_proj",
    )(x, g_mix, w_in, g_sgu, w_sp, b_sp, k, v)


def _alternate(*stages):
    live = list(stages)
    while live:
        live = [s for s in live if next(s, True) is None]


def _ssm_params(lam_re, lam_im, log_dt, b_re, b_im, c_re, c_im, d_skip):
    g, p = lam_re.shape
    dup = lambda a: jnp.concatenate([a, a], axis=-1)
    lam = jnp.stack([dup(lam_re), dup(lam_im), jnp.broadcast_to(log_dt[:, None], (g, 2 * p))], axis=1)
    brt = b_re.transpose(0, 2, 1)
    bit = b_im.transpose(0, 2, 1)
    cat = lambda a, b: jnp.concatenate([a, b], axis=-1)
    bc = jnp.stack([cat(brt, bit), cat(bit, brt), cat(c_re, -c_im), cat(-c_im, -c_re)], axis=1)
    d2 = jnp.tile(d_skip.reshape(g, 1, SSM_GROUP), (1, 1, SSM_CHUNK))
    return lam, bc, d2


def _ssm_operators(lam_ref, bc_ref, ccat_ref, n_ref, m_ref):
    lam_re = lam_ref[0, 0:1, :]
    lam_im = lam_ref[0, 1:2, :]
    dt = jnp.exp(lam_ref[0, 2:3, :])
    ar = lam_re * dt
    ai = lam_im * dt
    lane = lax.broadcasted_iota(jnp.int32, (1, LANES), 1)
    sgn = jnp.where(lane >= SSM_STATE, 1.0, -1.0)

    def powers(j):
        mag = jnp.exp(ar * j)
        ph = ai * j
        return mag * jnp.cos(ph), mag * jnp.sin(ph)

    pos = lax.broadcasted_iota(jnp.int32, (SSM_CHUNK, 1), 0).astype(F32)
    p_re, p_im = powers(pos)
    r_re, r_im = powers((SSM_CHUNK - 1) - pos)
    one_re, one_im = powers(jnp.ones((1, 1), F32))
    q_re = p_re * one_re - p_im * one_im
    q_im = p_re * one_im + p_im * one_re
    step = lax.shift_left(jnp.full((8, 1), SSM_CHUNK, jnp.int32),
                          lax.broadcasted_iota(jnp.int32, (8, 1), 0)).astype(F32)
    s_re, s_im = powers(step)

    den = lam_re * lam_re + lam_im * lam_im
    f_re = ((one_re - 1.0) * lam_re + one_im * lam_im) / den
    f_im = (one_im * lam_re - (one_re - 1.0) * lam_im) / den
    b1, b2, ca, cb = bc_ref[0, 0], bc_ref[0, 1], bc_ref[0, 2], bc_ref[0, 3]
    bb1 = f_re * b1 + (sgn * f_im) * b2
    bb2 = f_re * b2 - (sgn * f_im) * b1
    r_ims = sgn * r_im
    for s in range(SSM_CHUNK):
        blk = pl.ds(s * SSM_GROUP, SSM_GROUP)
        ccat_ref[blk, :] = ca * p_re[s:s + 1, :] + cb * p_im[s:s + 1, :]
        m_ref[blk, :] = (ca * q_re[s:s + 1, :] + cb * q_im[s:s + 1, :]).astype(m_ref.dtype)
        n_ref[blk, :] = (bb1 * r_re[s:s + 1, :] + bb2 * r_ims[s:s + 1, :]).astype(n_ref.dtype)
    return bb1, s_re, sgn * s_im


def _ssm_body(u_ref, lam_ref, bc_ref, d2_ref, y_ref, toep_ref, ccat_ref, n_ref, m_ref, *, n_seq):
    bcat, lr, li = _ssm_operators(lam_ref, bc_ref, ccat_ref, n_ref, m_ref)
    kern = lax.dot_general(bcat, ccat_ref[...], _NT, precision=lax.Precision.HIGHEST,
                           preferred_element_type=F32)
    col = lax.broadcasted_iota(jnp.int32, kern.shape, 1)
    for s in range(SSM_CHUNK):
        shifted = pltpu.roll(kern, s * SSM_GROUP, 1) if s else kern
        toep_ref[s * SSM_GROUP:(s + 1) * SSM_GROUP, :] = jnp.where(
            col >= s * SSM_GROUP, shifted, 0.0).astype(BF16)

    u = u_ref[0]
    rows = u.shape[0]
    per = rows // n_seq
    y = _dot(u, toep_ref[...])
    st = _dot(u, n_ref[...])
    row = lax.broadcasted_iota(jnp.int32, (per, LANES), 0)
    prev = []
    for b in range(n_seq):
        x = st[b * per:(b + 1) * per]
        k = 0
        while (1 << k) < per:
            d = 1 << k
            sh = jnp.where(row >= d, pltpu.roll(x, d, 0), 0.0)
            x = x + sh * lr[k:k + 1, :] + pltpu.roll(sh, SSM_STATE, 1) * li[k:k + 1, :]
            k += 1
        prev.append(jnp.where(row >= 1, pltpu.roll(x, 1, 0), 0.0))
    xp = jnp.concatenate(prev, axis=0).astype(BF16)
    y = y + lax.dot_general(xp, m_ref[...], _NT, preferred_element_type=F32) + d2_ref[0] * u.astype(F32)
    y_ref[0] = _gelu(y).astype(BF16)


def _ssm(u2, lam, bc, d2, n_seq):
    g, rows, w = u2.shape
    assert rows // n_seq <= 1 << 8, "lam_bar^(16*2^k) is prepared for 8 scan steps"
    blk = lambda a: pl.BlockSpec((1,) + a.shape[1:], lambda i: (i,) + (0,) * (a.ndim - 1))
    return pl.pallas_call(
        functools.partial(_ssm_body, n_seq=n_seq),
        grid=(g,),
        in_specs=[blk(u2), blk(lam), blk(bc), blk(d2)],
        out_specs=blk(u2),
        out_shape=jax.ShapeDtypeStruct(u2.shape, BF16),
        scratch_shapes=[pltpu.VMEM((w, w), BF16),
                        pltpu.VMEM((w, 2 * SSM_STATE), F32),
                        pltpu.VMEM((w, 2 * SSM_STATE), BF16),
                        pltpu.VMEM((w, 2 * SSM_STATE), BF16)],
        compiler_params=pltpu.CompilerParams(dimension_semantics=("arbitrary",),
                                             vmem_limit_bytes=VMEM_LIMIT),
        name="ssm",
    )(u2, lam, bc, d2)


def _mix_tile(x_ref, y2_ref, yb_ref, yc_ref, gmix_ref, wgate_ref, bgate_ref, wglu_ref, bglu_ref,
              wbr_ref, wout_ref, tok_ref, h_ref, hkeep_ref):
    x = x_ref[...]
    n = _rms(x, gmix_ref[...]).astype(BF16)

    def gated(b, c, y):
        cols = pl.ds(b * D_MODEL + c * MERGE_COLS, MERGE_COLS)
        gate = _sigmoid(_dot(n, wgate_ref[:, cols]) + bgate_ref[:, cols])
        return gate * _dot(y, wbr_ref[b, :, pl.ds(c * MERGE_COLS, MERGE_COLS)])

    n_blocks = D_MODEL // MERGE_COLS
    yb = yb_ref[...]
    yc = yc_ref[...]
    head = gated(1, 0, yb) + gated(2, 0, yc)
    yield
    _chunks_to_tokens(y2_ref, tok_ref)
    ys = jnp.concatenate([tok_ref[k] for k in range(SSM_WIDTH // LANES)], axis=1).astype(BF16)
    glu = _dot(ys, wglu_ref[...]) + bglu_ref[...]
    ya = (glu[:, :SSM_WIDTH] * _sigmoid(glu[:, SSM_WIDTH:])).astype(BF16)
    merged = [(head + gated(0, 0, ya)).astype(BF16)]
    for c in range(1, n_blocks):
        yield
        merged.append((gated(0, c, ya) + gated(1, c, yb) + gated(2, c, yc)).astype(BF16))
    yield
    h = x + _dot(jnp.concatenate(merged, axis=1), wout_ref[...])
    h_ref[...] = h
    hkeep_ref[...] = h


def _route_tile(hkeep_ref, gffn_ref, wrt_ref, brt_ref, xnp_ref, rt_ref, rtt_ref, cnt_ref, carry_ref):
    h = hkeep_ref[...]
    xn = _rms(h, gffn_ref[...])
    packed = _pack_bf16_pair(xn)
    for j in range(SC_SPLIT):
        xnp_ref[j] = packed[:, j * SC_ROW:(j + 1) * SC_ROW]
    yield

    x_hi = xn.astype(BF16)
    x_lo = (xn - x_hi.astype(F32)).astype(BF16)
    head = _dot(x_hi, wrt_ref[...])
    logits = (head[:, :LANES] + head[:, LANES:] + _dot(x_lo, wrt_ref[:, :LANES])) + brt_ref[...]
    yield
    tm = logits.shape[0]
    lane_i = lax.broadcasted_iota(jnp.int32, (tm, LANES), 1)
    lane = lane_i.astype(F32)
    neg = jnp.float32(-3.0e38)
    big = jnp.float32(LANES)
    gmask = lane_i < N_GROUPS
    gl = jnp.where(gmask, logits, neg)
    gmax = jnp.max(gl, axis=-1, keepdims=True)
    gidx = jnp.min(jnp.where(gl == gmax, lane, big), axis=-1, keepdims=True)
    gsum = jnp.sum(jnp.where(gmask, jnp.exp(gl - gmax), 0.0), axis=-1, keepdims=True)
    g_w = 1.0 / gsum
    e_lane = lane_i - ROUTE_LANE0
    lane_group = (e_lane >> 3).astype(F32)
    emask = (e_lane >= 0) & (e_lane < N_EXPERTS) & (lane_group == gidx)
    el = jnp.where(emask, logits, neg)
    m1 = jnp.max(el, axis=-1, keepdims=True)
    i1 = jnp.min(jnp.where(el == m1, lane, big), axis=-1, keepdims=True)
    el2 = jnp.where(lane == i1, neg, el)
    m2 = jnp.max(el2, axis=-1, keepdims=True)
    i2 = jnp.min(jnp.where(el2 == m2, lane, big), axis=-1, keepdims=True)
    t = jnp.exp(m2 - m1)
    w1 = g_w / (1.0 + t)
    w2 = g_w * t / (1.0 + t)
    yield

    sel1 = lane == i1
    sel2 = lane == i2
    onehot = jnp.where(sel1 | sel2, 1.0, 0.0)
    r_i = lax.broadcasted_iota(jnp.int32, (tm, tm), 0)
    c_i = lax.broadcasted_iota(jnp.int32, (tm, tm), 1)
    stril = jnp.where(c_i < r_i, 1.0, 0.0).astype(BF16)
    cum = _dot(stril, onehot.astype(BF16)) + carry_ref[0:1, :]
    rank1 = jnp.sum(jnp.where(sel1, cum, 0.0), axis=-1, keepdims=True)
    rank2 = jnp.sum(jnp.where(sel2, cum, 0.0), axis=-1, keepdims=True)
    carry_ref[...] = carry_ref[...] + jnp.sum(onehot, axis=0, keepdims=True)
    cnt_ref[...] = carry_ref[...]
    yield

    cols = (i1 - ROUTE_LANE0, i2 - ROUTE_LANE0, rank1, rank2, w1, w2)
    rt = jnp.zeros((tm, LANES), F32)
    for c, val in enumerate(cols):
        rt = jnp.where(lane_i == c, val, rt)
    rt_ref[...] = rt
    rtt_ref[...] = rt.T[:8]


def _merge_body(x_ref, y2_ref, yb_ref, yc_ref, gmix_ref, wgate_ref, bgate_ref, wglu_ref, bglu_ref,
                wbr_ref, wout_ref, gffn_ref, wrt_ref, brt_ref,
                h_ref, xnp_ref, rt_ref, rtt_ref, cnt_ref, carry_ref, tok_ref, hkeep_ref):
    i = pl.program_id(0)
    last = pl.num_programs(0) - 1
    cur = hkeep_ref.at[i % 2]
    prev = hkeep_ref.at[(i + 1) % 2]

    def mix():
        return _mix_tile(x_ref, y2_ref, yb_ref, yc_ref, gmix_ref, wgate_ref, bgate_ref, wglu_ref,
                         bglu_ref, wbr_ref, wout_ref, tok_ref, h_ref, cur)

    def route():
        return _route_tile(prev, gffn_ref, wrt_ref, brt_ref, xnp_ref, rt_ref, rtt_ref, cnt_ref, carry_ref)

    @pl.when(i == 0)
    def _():
        carry_ref[...] = jnp.zeros_like(carry_ref)
        _alternate(mix())

    @pl.when((i > 0) & (i < last))
    def _():
        _alternate(route(), mix())

    @pl.when(i == last)
    def _():
        _alternate(route())


def _merge_route(x, y2, yb, yc, g_mix, w_gate, b_gate, w_glu, b_glu, w_br, w_out, g_ffn, w_rt, b_rt):
    t, d = x.shape
    tm = TM_MERGE
    tiles = t // tm
    mixed = lambda i: jnp.minimum(i, tiles - 1)
    routed = lambda i: jnp.maximum(i - 1, 0)
    c2 = lambda i: (0, 0)
    c3 = lambda i: (0, 0, 0)
    full = lambda a: pl.BlockSpec(a.shape, c2 if a.ndim == 2 else c3)
    return pl.pallas_call(
        _merge_body,
        grid=(tiles + 1,),
        in_specs=[pl.BlockSpec((tm, d), lambda i: (mixed(i), 0)),
                  pl.BlockSpec((SSM_GROUPS, tm // SSM_CHUNK, SSM_CHUNK * SSM_GROUP),
                               lambda i: (0, mixed(i), 0)),
                  pl.BlockSpec((tm, SSM_WIDTH), lambda i: (mixed(i), 0)),
                  pl.BlockSpec((tm, SSM_WIDTH), lambda i: (mixed(i), 0)),
                  full(g_mix), full(w_gate), full(b_gate), full(w_glu), full(b_glu),
                  full(w_br), full(w_out), full(g_ffn), full(w_rt), full(b_rt)],
        out_specs=[pl.BlockSpec((tm, d), lambda i: (mixed(i), 0)),
                   pl.BlockSpec((SC_SPLIT, tm, SC_ROW), lambda i: (0, routed(i), 0)),
                   pl.BlockSpec((tm, LANES), lambda i: (routed(i), 0)),
                   pl.BlockSpec((8, tm), lambda i: (0, routed(i))),
                   pl.BlockSpec((8, LANES), c2)],
        out_shape=[jax.ShapeDtypeStruct((t, d), F32),
                   jax.ShapeDtypeStruct((SC_SPLIT, t, SC_ROW), jnp.uint32),
                   jax.ShapeDtypeStruct((t, LANES), F32),
                   jax.ShapeDtypeStruct((8, t), F32),
                   jax.ShapeDtypeStruct((8, LANES), F32)],
        scratch_shapes=[pltpu.VMEM((8, LANES), F32),
                        pltpu.VMEM((SSM_WIDTH // LANES, tm, LANES), F32),
                        pltpu.VMEM((2, tm, d), F32)],
        compiler_params=pltpu.CompilerParams(dimension_semantics=("arbitrary",),
                                             vmem_limit_bytes=VMEM_LIMIT),
        name="merge_route",
    )(x, y2, yb, yc, g_mix, w_gate, b_gate, w_glu, b_glu, w_br, w_out, g_ffn, w_rt, b_rt)


def _slot_body(rtt_ref, pstart_ref, out_ref, *, nslots):
    tl = rtt_ref.shape[1]
    expert = lax.broadcasted_iota(jnp.int32, (N_EXPERTS, tl), 0).astype(F32)
    for k in range(TOP_K):
        onehot = rtt_ref[k:k + 1, :] == expert
        start = jnp.sum(jnp.where(onehot, pstart_ref[...], 0.0), axis=0, keepdims=True)
        slot = (start + rtt_ref[TOP_K + k:TOP_K + k + 1, :]).astype(jnp.int32)
        for j in range(SC_SPLIT):
            out_ref[k * SC_SPLIT + j:k * SC_SPLIT + j + 1, :] = slot + j * nslots


def _slot_rows(rtt, pstart, nslots):
    t = rtt.shape[1]
    tl = SLOT_LANES
    return pl.pallas_call(
        functools.partial(_slot_body, nslots=nslots),
        grid=(t // tl,),
        in_specs=[pl.BlockSpec((8, tl), lambda i: (0, i)),
                  pl.BlockSpec((N_EXPERTS, 1), lambda i: (0, 0))],
        out_specs=pl.BlockSpec((TOP_K * SC_SPLIT, tl), lambda i: (0, i)),
        out_shape=jax.ShapeDtypeStruct((TOP_K * SC_SPLIT, t), jnp.int32),
        compiler_params=pltpu.CompilerParams(dimension_semantics=("arbitrary",)),
        name="slot_rows",
    )(rtt, pstart.astype(F32).reshape(N_EXPERTS, 1))


def _expert_body(meta_ref, buf_ref, w1_hbm, w3_hbm, w2_hbm, out_ref, w1_buf, w3_buf, w2_buf, sem):
    i = pl.program_id(0)
    expert = meta_ref[0, i]
    valid = meta_ref[1, i]
    slot = meta_ref[2, i]
    ahead1 = meta_ref[4, i]
    ahead2 = meta_ref[5, i]

    def weight_copies(e, s):
        return (pltpu.make_async_copy(w1_hbm.at[e], w1_buf.at[s], sem.at[s, 0]),
                pltpu.make_async_copy(w3_hbm.at[e], w3_buf.at[s], sem.at[s, 1]),
                pltpu.make_async_copy(w2_hbm.at[e], w2_buf.at[s], sem.at[s, 2]))

    @pl.when(i == 0)
    def _():
        for c in weight_copies(expert, slot):
            c.start()

        @pl.when(ahead1 >= 0)
        def _():
            for c in weight_copies(ahead1, (slot + 1) % WEIGHT_SLOTS):
                c.start()

    @pl.when(meta_ref[3, i] == 1)
    def _():
        for c in weight_copies(expert, slot):
            c.wait()

        @pl.when(ahead2 >= 0)
        def _():
            for c in weight_copies(ahead2, (slot + 2) % WEIGHT_SLOTS):
                c.start()

    def mlp(rows):
        x = _unpack_bf16_pair(jnp.concatenate([buf_ref[j, :rows, :] for j in range(SC_SPLIT)], axis=1))
        row = lax.broadcasted_iota(jnp.int32, x.shape, 0)
        x = jnp.where(row < valid, x, 0.0).astype(BF16)
        h1 = _dot(x, w1_buf[slot].astype(BF16))
        h3 = _dot(x, w3_buf[slot].astype(BF16))
        a = (h1 * _sigmoid(h1) * h3).astype(BF16)
        packed = _pack_bf16_pair(_dot(a, w2_buf[slot].astype(BF16)))
        for j in range(SC_SPLIT):
            out_ref[j, :rows, :] = packed[:, j * SC_ROW:(j + 1) * SC_ROW]
            if rows < BM:
                out_ref[j, rows:, :] = jnp.zeros((BM - rows, SC_ROW), out_ref.dtype)

    @pl.when(valid > BM // 2)
    def _():
        mlp(BM)

    @pl.when((valid > 0) & (valid <= BM // 2))
    def _():
        mlp(BM // 2)

    @pl.when(valid <= 0)
    def _():
        out_ref[...] = jnp.zeros_like(out_ref)


def _expert_meta(block_expert, block_valid):
    nb = block_expert.shape[0]
    pos = jnp.arange(nb, dtype=jnp.int32)
    prev = jnp.concatenate([jnp.full((1,), -1, jnp.int32), block_expert[:-1]])
    first = (block_valid > 0) & (block_expert != prev)
    run = jnp.cumsum(first.astype(jnp.int32)) - 1
    start_pos = jnp.where(first, pos, nb)
    later = jnp.concatenate([start_pos[1:], jnp.full((2,), nb, jnp.int32)])
    nearest = lax.cummin(later, axis=0, reverse=True)
    n1 = nearest[:nb]
    n2 = nearest[n1]
    experts_ext = jnp.concatenate([block_expert, jnp.full((1,), -1, jnp.int32)])
    return jnp.stack([block_expert, block_valid, run % WEIGHT_SLOTS, first.astype(jnp.int32),
                      experts_ext[n1], experts_ext[n2]]).astype(jnp.int32)


def _experts(block_expert, block_valid, buf, w1, w3, w2):
    _, nslots, _ = buf.shape
    nb = nslots // BM
    rows = pl.BlockSpec((SC_SPLIT, BM, SC_ROW), lambda i, meta: (0, i, 0))
    hbm = pl.BlockSpec(memory_space=pl.ANY)
    grid_spec = pltpu.PrefetchScalarGridSpec(
        num_scalar_prefetch=1,
        grid=(nb,),
        in_specs=[rows, hbm, hbm, hbm],
        out_specs=rows,
        scratch_shapes=[pltpu.VMEM((WEIGHT_SLOTS,) + w1.shape[1:], w1.dtype),
                        pltpu.VMEM((WEIGHT_SLOTS,) + w3.shape[1:], w3.dtype),
                        pltpu.VMEM((WEIGHT_SLOTS,) + w2.shape[1:], w2.dtype),
                        pltpu.SemaphoreType.DMA((WEIGHT_SLOTS, 3))],
    )
    return pl.pallas_call(
        _expert_body,
        grid_spec=grid_spec,
        out_shape=jax.ShapeDtypeStruct(buf.shape, jnp.uint32),
        compiler_params=pltpu.CompilerParams(dimension_semantics=("arbitrary",),
                                             vmem_limit_bytes=VMEM_LIMIT),
        name="experts",
    )(_expert_meta(block_expert, block_valid), buf, w1, w3, w2)


def _sc_mesh():
    return plsc.VectorSubcoreMesh(core_axis_name="core", subcore_axis_name="subcore")


def _dispatch_rows(rows, dest0, dest1, nslots):
    t, w = rows.shape
    win = SC_WINDOW
    idx_spec = pl.BlockSpec((1, win), lambda i: (0, i))

    @functools.partial(pl.kernel, mesh=_sc_mesh(), scratch_types=[],
                       out_type=jax.ShapeDtypeStruct((nslots, w), rows.dtype), name="dispatch_rows")
    def run(rows_hbm, i0_hbm, i1_hbm, out_hbm):
        def body(rows_vmem, i0_vmem, i1_vmem):
            pltpu.sync_copy(rows_vmem, out_hbm.at[i0_vmem.at[0]])
            pltpu.sync_copy(rows_vmem, out_hbm.at[i1_vmem.at[0]])

        pltpu.emit_pipeline(
            body, grid=(t // win,),
            in_specs=[pl.BlockSpec((win, w), lambda i: (i, 0)), idx_spec, idx_spec],
            out_specs=[],
            core_axis_name=("core", "subcore"),
            dimension_semantics=(pltpu.PARALLEL,),
        )(rows_hbm, i0_hbm, i1_hbm)

    return run(rows, dest0.reshape(1, t), dest1.reshape(1, t))


def _gather_rows(table, idx):
    n = idx.shape[0]
    w = table.shape[1]
    win = SC_WINDOW

    @functools.partial(pl.kernel, mesh=_sc_mesh(), scratch_types=[],
                       out_type=jax.ShapeDtypeStruct((n, w), table.dtype), name="gather_rows")
    def run(table_hbm, i_hbm, out_hbm):
        def body(i_vmem, out_vmem):
            pltpu.sync_copy(table_hbm.at[i_vmem.at[0]], out_vmem)

        pltpu.emit_pipeline(
            body, grid=(n // win,),
            in_specs=[pl.BlockSpec((1, win), lambda i: (0, i))],
            out_specs=[pl.BlockSpec((win, w), lambda i: (i, 0))],
            core_axis_name=("core", "subcore"),
            dimension_semantics=(pltpu.PARALLEL,),
        )(i_hbm, out_hbm)

    return run(table, idx.reshape(1, n))


def _combine_body(h_ref, g_ref, rt_ref, gfin_ref, out_ref):
    rt = rt_ref[...]
    y = h_ref[...]
    for k in range(TOP_K):
        rows = jnp.concatenate([g_ref[k * SC_SPLIT + j] for j in range(SC_SPLIT)], axis=1)
        y = y + rt[:, 4 + k:5 + k] * _unpack_bf16_pair(rows)
    out_ref[...] = _rms(y, gfin_ref[...])


def _combine(h, g, rt, g_final):
    t, d = h.shape
    tm = TM_OUT
    tok = lambda i: (i, 0)
    return pl.pallas_call(
        _combine_body,
        grid=(t // tm,),
        in_specs=[pl.BlockSpec((tm, d), tok),
                  pl.BlockSpec((TOP_K * SC_SPLIT, tm, SC_ROW), lambda i: (0, i, 0)),
                  pl.BlockSpec((tm, LANES), tok),
                  pl.BlockSpec((1, d), lambda i: (0, 0))],
        out_specs=pl.BlockSpec((tm, d), tok),
        out_shape=jax.ShapeDtypeStruct((t, d), F32),
        compiler_params=pltpu.CompilerParams(dimension_semantics=("arbitrary",),
                                             vmem_limit_bytes=VMEM_LIMIT),
        name="combine",
    )(h, g, rt, g_final)


def _layer(h, mem, g_mix, g_mem, w_in, w_gate, b_gate, lam_re, lam_im, log_dt, b_re, b_im,
           c_re, c_im, d_skip, w_glu, b_glu, g_sgu, w_spatial, b_spatial, w_kv, w_branch,
           w_out, g_ffn, w_group, b_group, w_router, b_router, w1, w3, w2, g_out):
    bsz, s, d = h.shape
    t = bsz * s
    row = lambda a: a.reshape(1, -1)

    k, v = _kv_proj(mem, row(g_mem), w_kv.astype(BF16))

    tril = jnp.tril(jnp.ones((CHUNK, CHUNK), dtype=bool))
    w_sp = jnp.where(tril, w_spatial, 0.0).astype(BF16)
    b_sp = jnp.broadcast_to(b_spatial[:, :, None], (SGU_HEADS, CHUNK, SGU_HEAD_DIM))
    u2, y_b, y_c = _in_proj(h, row(g_mix), w_in.astype(BF16), row(g_sgu), w_sp, b_sp, k, v)
    y2 = _ssm(u2, *_ssm_params(lam_re, lam_im, log_dt, b_re, b_im, c_re, c_im, d_skip), n_seq=bsz)

    pad = LANES - N_GROUPS - N_EXPERTS
    w_rt = jnp.concatenate([w_group, w_router, jnp.zeros((d, pad), F32)], axis=1)
    w_rt_hi = w_rt.astype(BF16)
    w_rt = jnp.concatenate([w_rt_hi, (w_rt - w_rt_hi.astype(F32)).astype(BF16)], axis=1)
    b_rt =jnp.concatenate([b_group, b_router, jnp.zeros((pad,), F32)]).reshape(1, LANES)
    h2, xnp, rt, rtt, cnt = _merge_route(
        h.reshape(t, d), y2, y_b.reshape(t, -1), y_c.reshape(t, -1), row(g_mix),
        w_gate.astype(BF16), row(b_gate), w_glu.astype(BF16), row(b_glu),
        w_branch.astype(BF16), w_out.astype(BF16), row(g_ffn), w_rt, b_rt)

    counts = cnt[0, ROUTE_LANE0:ROUTE_LANE0 + N_EXPERTS].astype(jnp.int32)
    padded = (counts + BM - 1) // BM * BM
    block_end = jnp.cumsum(padded)
    pstart = block_end - padded
    nb = (t * TOP_K) // BM + N_EXPERTS
    blk_row0 = jnp.arange(nb, dtype=jnp.int32) * BM
    block_expert = jnp.sum((block_end[None, :] <= blk_row0[:, None]).astype(jnp.int32), axis=1)
    block_expert = jnp.minimum(block_expert, N_EXPERTS - 1)
    block_valid = jnp.clip(counts[block_expert] - (blk_row0 - pstart[block_expert]), 0, BM)
    block_valid = jnp.where(blk_row0 < block_end[-1], block_valid, 0).astype(jnp.int32)
    nslots = nb * BM
    dest_p = _slot_rows(rtt, pstart, nslots).reshape(TOP_K, SC_SPLIT * t)
    buf = _dispatch_rows(xnp.reshape(SC_SPLIT * t, SC_ROW), dest_p[0], dest_p[1], SC_SPLIT * nslots)
    yb = _experts(block_expert, block_valid, buf.reshape(SC_SPLIT, nslots, SC_ROW), w1, w3, w2)
    g = _gather_rows(yb.reshape(SC_SPLIT * nslots, SC_ROW), dest_p.reshape(-1))
    out = _combine(h2, g.reshape(TOP_K * SC_SPLIT, t, SC_ROW), rt, row(g_out))
    return out.reshape(bsz, s, d)


def kernel(x, mem, g_mix, g_mem, w_in, w_gate, b_gate, lam_re, lam_im, log_dt, b_re, b_im, c_re,
           c_im, d_skip, w_glu, b_glu, g_sgu, w_spatial, b_spatial, w_kv, w_branch, w_out, g_ffn,
           w_group, b_group, w_router, b_router, w1, w3, w2, g_final):
    assert g_mix.shape[0] == 1, "single-layer stack"
    return _layer(x, mem, g_mix[0], g_mem[0], w_in[0], w_gate[0], b_gate[0], lam_re[0], lam_im[0],
                  log_dt[0], b_re[0], b_im[0], c_re[0], c_im[0], d_skip[0], w_glu[0], b_glu[0],
                  g_sgu[0], w_spatial[0], b_spatial[0], w_kv[0], w_branch[0], w_out[0], g_ffn[0],
                  w_group[0], b_group[0], w_router[0], b_router[0], w1[0], w3[0], w2[0], g_final)
```

```python
import functools
import math

import jax
import jax.numpy as jnp
from jax import lax
from jax.experimental import pallas as pl
from jax.experimental.pallas import tpu as pltpu
from jax.experimental.pallas import tpu_sc as plsc

F32 = jnp.float32
BF16 = jnp.bfloat16

EPS = 1e-6
D_MODEL = 1024
SSM_WIDTH = 512
SSM_GROUP = 16
SSM_GROUPS = 32
SSM_STATE = 64
SSM_CHUNK = 16
SGU_WIDTH = 512
SGU_HEADS = 4
SGU_HEAD_DIM = 128
CHUNK = 128
XA_HEADS = 4
XA_HEAD_DIM = 128
N_GROUPS = 8
EXPERTS_PER_GROUP = 8
N_EXPERTS = 64
TOP_K = 2
D_FF = 512
LANES = 128
ROUTE_LANE0 = N_GROUPS

TM_IN = 512
TM_MERGE = 512
MERGE_COLS = 256
TM_OUT = 512
BM = 512
WEIGHT_SLOTS = 3
SC_WINDOW = 128
SC_ROW = 256
SC_SPLIT = (D_MODEL // 2) // SC_ROW
SLOT_LANES = 2048
COMBINE_PARTS = 2
VMEM_LIMIT = 56 * 1024 * 1024


def _rms(x, g):
    return x * lax.rsqrt(jnp.mean(x * x, axis=-1, keepdims=True) + EPS) * g


def _sigmoid(x):
    return 0.5 * (1.0 + jnp.tanh(0.5 * x))


def _gelu(x):
    c = math.sqrt(2.0 / math.pi)
    return 0.5 * x * (1.0 + jnp.tanh(c * (x + 0.044715 * (x * x * x))))


def _dot(a, b):
    return jnp.dot(a, b, preferred_element_type=F32)


_NT = (((1,), (1,)), ((), ()))


def _pack_bf16_pair(x):
    n = x.shape[1] // 2
    lo = lax.bitcast_convert_type(x[:, :n].astype(BF16).astype(F32), jnp.uint32)
    hi = lax.bitcast_convert_type(x[:, n:].astype(BF16).astype(F32), jnp.uint32)
    return hi | (lo >> 16)


def _unpack_bf16_pair(p):
    lo = lax.bitcast_convert_type(p << 16, F32)
    hi = lax.bitcast_convert_type(p & jnp.uint32(0xFFFF0000), F32)
    return jnp.concatenate([lo, hi], axis=1)


GROUPS_PER_TILE = LANES // SSM_GROUP
POS_PER_TILE = LANES // SSM_GROUP


def _slot_masks(rows):
    lane = lax.broadcasted_iota(jnp.int32, (rows, LANES), 1)
    return [(lane >= i * SSM_GROUP) & (lane < (i + 1) * SSM_GROUP) for i in range(LANES // SSM_GROUP)]


def _tokens_to_chunks(tok_ref, out_ref):
    tm = tok_ref.shape[1]
    nc = tm // SSM_CHUNK
    masks = _slot_masks(nc)
    for k in range(SSM_WIDTH // LANES):
        for j in range(SSM_CHUNK // POS_PER_TILE):
            src = [tok_ref[k, pl.ds(j * POS_PER_TILE + p, nc, stride=SSM_CHUNK), :]
                   for p in range(POS_PER_TILE)]
            for gi in range(GROUPS_PER_TILE):
                acc = None
                for p in range(POS_PER_TILE):
                    shift = ((p - gi) * SSM_GROUP) % LANES
                    r = pltpu.roll(src[p], shift, 1) if shift else src[p]
                    acc = r if acc is None else jnp.where(masks[p], r, acc)
                out_ref[k * GROUPS_PER_TILE + gi, :, pl.ds(j * LANES, LANES)] = acc.astype(out_ref.dtype)


def _chunks_to_tokens(chunk_ref, tok_ref):
    tm = tok_ref.shape[1]
    nc = tm // SSM_CHUNK
    masks = _slot_masks(nc)
    for k in range(SSM_WIDTH // LANES):
        for j in range(SSM_CHUNK // POS_PER_TILE):
            src = [chunk_ref[k * GROUPS_PER_TILE + gi, :, pl.ds(j * LANES, LANES)].astype(F32)
                   for gi in range(GROUPS_PER_TILE)]
            for p in range(POS_PER_TILE):
                acc = None
                for gi in range(GROUPS_PER_TILE):
                    shift = ((gi - p) * SSM_GROUP) % LANES
                    r = pltpu.roll(src[gi], shift, 1) if shift else src[gi]
                    acc = r if acc is None else jnp.where(masks[gi], r, acc)
                tok_ref[k, pl.ds(j * POS_PER_TILE + p, nc, stride=SSM_CHUNK), :] = acc


def _kv_body(mem_ref, g_ref, w_ref, k_ref, v_ref):
    n = _rms(mem_ref[0], g_ref[...]).astype(BF16)
    kv = _dot(n, w_ref[...])
    k_ref[0] = kv[:, :SGU_WIDTH].astype(BF16)
    v_ref[0] = kv[:, SGU_WIDTH:].astype(BF16)


def _kv_proj(mem, g_mem, w_kv):
    b, m, d = mem.shape
    return pl.pallas_call(
        _kv_body,
        grid=(b,),
        in_specs=[pl.BlockSpec((1, m, d), lambda i: (i, 0, 0)),
                  pl.BlockSpec((1, d), lambda i: (0, 0)),
                  pl.BlockSpec((d, 2 * SGU_WIDTH), lambda i: (0, 0))],
        out_specs=[pl.BlockSpec((1, m, SGU_WIDTH), lambda i: (i, 0, 0)),
                   pl.BlockSpec((1, m, SGU_WIDTH), lambda i: (i, 0, 0))],
        out_shape=[jax.ShapeDtypeStruct((b, m, SGU_WIDTH), BF16),
                   jax.ShapeDtypeStruct((b, m, SGU_WIDTH), BF16)],
        compiler_params=pltpu.CompilerParams(dimension_semantics=("arbitrary",),
                                             vmem_limit_bytes=VMEM_LIMIT),
        name="kv_proj",
    )(mem, g_mem, w_kv)


def _in_body(x_ref, gmix_ref, win_ref, gsgu_ref, wsp_ref, bsp_ref, k_ref, v_ref,
             u2_ref, yb_ref, yc_ref, tok_ref):
    n = _rms(x_ref[0], gmix_ref[...]).astype(BF16)
    proj = _dot(n, win_ref[...])
    for k in range(SSM_WIDTH // LANES):
        tok_ref[k] = proj[:, k * LANES:(k + 1) * LANES]
    _tokens_to_chunks(tok_ref, u2_ref)

    u = _gelu(proj[:, SSM_WIDTH:SSM_WIDTH + SGU_WIDTH])
    v = _gelu(proj[:, SSM_WIDTH + SGU_WIDTH:SSM_WIDTH + 2 * SGU_WIDTH])
    v = _rms(v, gsgu_ref[...]).astype(BF16)
    tm = u.shape[0]
    rows = []
    for c in range(tm // CHUNK):
        vc = v[c * CHUNK:(c + 1) * CHUNK]
        heads = []
        for h in range(SGU_HEADS):
            sl = slice(h * SGU_HEAD_DIM, (h + 1) * SGU_HEAD_DIM)
            heads.append(_dot(wsp_ref[h], vc[:, sl]) + bsp_ref[h])
        rows.append(jnp.concatenate(heads, axis=1))
    sv = jnp.concatenate(rows, axis=0)
    yb_ref[0] = (u * sv).astype(BF16)

    q = proj[:, SSM_WIDTH + 2 * SGU_WIDTH:].astype(BF16)
    kk = k_ref[0]
    vv = v_ref[0]
    outs = []
    for h in range(XA_HEADS):
        sl = slice(h * XA_HEAD_DIM, (h + 1) * XA_HEAD_DIM)
        s = lax.dot_general(q[:, sl], kk[:, sl], (((1,), (1,)), ((), ())),
                            preferred_element_type=F32) * (XA_HEAD_DIM ** -0.5)
        e = jnp.exp(s - jnp.max(s, axis=-1, keepdims=True))
        l = jnp.sum(e, axis=-1, keepdims=True)
        outs.append(_dot(e.astype(BF16), vv[:, sl]) / l)
    yc_ref[0] = jnp.concatenate(outs, axis=1).astype(BF16)


def _in_proj(x, g_mix, w_in, g_sgu, w_sp, b_sp, k, v):
    b, s, d = x.shape
    m = k.shape[1]
    const2 = lambda i, j: (0, 0)
    const3 = lambda i, j: (0, 0, 0)
    tok = lambda i, j: (i, j, 0)
    per_b = lambda i, j: (i, 0, 0)
    out = jax.ShapeDtypeStruct((b, s, SSM_WIDTH), BF16)
    nc = TM_IN // SSM_CHUNK
    tiles = s // TM_IN
    u2 = jax.ShapeDtypeStruct((SSM_GROUPS, b * s // SSM_CHUNK, SSM_CHUNK * SSM_GROUP), BF16)
    return pl.pallas_call(
        _in_body,
        grid=(b, s // TM_IN),
        in_specs=[pl.BlockSpec((1, TM_IN, d), tok),
                  pl.BlockSpec((1, d), const2),
                  pl.BlockSpec(w_in.shape, const2),
                  pl.BlockSpec((1, SGU_WIDTH), const2),
                  pl.BlockSpec(w_sp.shape, const3),
                  pl.BlockSpec(b_sp.shape, const3),
                  pl.BlockSpec((1, m, SGU_WIDTH), per_b),
                  pl.BlockSpec((1, m, SGU_WIDTH), per_b)],
        out_specs=[pl.BlockSpec((SSM_GROUPS, nc, SSM_CHUNK * SSM_GROUP), lambda i, j: (0, i * tiles + j, 0)),
                   pl.BlockSpec((1, TM_IN, SSM_WIDTH), tok),
                   pl.BlockSpec((1, TM_IN, SSM_WIDTH), tok)],
        out_shape=[u2, out, out],
        scratch_shapes=[pltpu.VMEM((SSM_WIDTH // LANES, TM_IN, LANES), F32)],
        compiler_params=pltpu.CompilerParams(dimension_semantics=("arbitrary", "arbitrary"),
                                             vmem_limit_bytes=VMEM_LIMIT),
        name="in_proj",
    )(x, g_mix, w_in, g_sgu, w_sp, b_sp, k, v)


def _alternate(*stages):
    live = list(stages)
    while live:
        live = [s for s in live if next(s, True) is None]


def _ssm_params(lam_re, lam_im, log_dt, b_re, b_im, c_re, c_im, d_skip):
    g, p = lam_re.shape
    dup = lambda a: jnp.concatenate([a, a], axis=-1)
    lam = jnp.stack([dup(lam_re), dup(lam_im), jnp.broadcast_to(log_dt[:, None], (g, 2 * p))], axis=1)
    brt = b_re.transpose(0, 2, 1)
    bit = b_im.transpose(0, 2, 1)
    cat = lambda a, b: jnp.concatenate([a, b], axis=-1)
    bc = jnp.stack([cat(brt, bit), cat(bit, brt), cat(c_re, -c_im), cat(-c_im, -c_re)], axis=1)
    d2 = jnp.tile(d_skip.reshape(g, 1, SSM_GROUP), (1, 1, SSM_CHUNK))
    return lam, bc, d2


def _ssm_operators(lam_ref, bc_ref, ccat_ref, n_ref, m_ref):
    lam_re = lam_ref[0, 0:1, :]
    lam_im = lam_ref[0, 1:2, :]
    dt = jnp.exp(lam_ref[0, 2:3, :])
    ar = lam_re * dt
    ai = lam_im * dt
    lane = lax.broadcasted_iota(jnp.int32, (1, LANES), 1)
    sgn = jnp.where(lane >= SSM_STATE, 1.0, -1.0)

    def powers(j):
        mag = jnp.exp(ar * j)
        ph = ai * j
        return mag * jnp.cos(ph), mag * jnp.sin(ph)

    pos = lax.broadcasted_iota(jnp.int32, (SSM_CHUNK, 1), 0).astype(F32)
    p_re, p_im = powers(pos)
    r_re, r_im = powers((SSM_CHUNK - 1) - pos)
    one_re, one_im = powers(jnp.ones((1, 1), F32))
    q_re = p_re * one_re - p_im * one_im
    q_im = p_re * one_im + p_im * one_re
    step = lax.shift_left(jnp.full((8, 1), SSM_CHUNK, jnp.int32),
                          lax.broadcasted_iota(jnp.int32, (8, 1), 0)).astype(F32)
    s_re, s_im = powers(step)

    den = lam_re * lam_re + lam_im * lam_im
    f_re = ((one_re - 1.0) * lam_re + one_im * lam_im) / den
    f_im = (one_im * lam_re - (one_re - 1.0) * lam_im) / den
    b1, b2, ca, cb = bc_ref[0, 0], bc_ref[0, 1], bc_ref[0, 2], bc_ref[0, 3]
    bb1 = f_re * b1 + (sgn * f_im) * b2
    bb2 = f_re * b2 - (sgn * f_im) * b1
    r_ims = sgn * r_im
    for s in range(SSM_CHUNK):
        blk = pl.ds(s * SSM_GROUP, SSM_GROUP)
        ccat_ref[blk, :] = ca * p_re[s:s + 1, :] + cb * p_im[s:s + 1, :]
        m_ref[blk, :] = (ca * q_re[s:s + 1, :] + cb * q_im[s:s + 1, :]).astype(m_ref.dtype)
        n_ref[blk, :] = (bb1 * r_re[s:s + 1, :] + bb2 * r_ims[s:s + 1, :]).astype(n_ref.dtype)
    return bb1, s_re, sgn * s_im


def _ssm_body(u_ref, lam_ref, bc_ref, d2_ref, y_ref, toep_ref, ccat_ref, n_ref, m_ref, *, n_seq):
    bcat, lr, li = _ssm_operators(lam_ref, bc_ref, ccat_ref, n_ref, m_ref)
    kern = lax.dot_general(bcat, ccat_ref[...], _NT, precision=lax.Precision.HIGHEST,
                           preferred_element_type=F32)
    col = lax.broadcasted_iota(jnp.int32, kern.shape, 1)
    for s in range(SSM_CHUNK):
        shifted = pltpu.roll(kern, s * SSM_GROUP, 1) if s else kern
        toep_ref[s * SSM_GROUP:(s + 1) * SSM_GROUP, :] = jnp.where(
            col >= s * SSM_GROUP, shifted, 0.0).astype(BF16)

    u = u_ref[0]
    rows = u.shape[0]
    per = rows // n_seq
    y = _dot(u, toep_ref[...])
    st = _dot(u, n_ref[...])
    row = lax.broadcasted_iota(jnp.int32, (per, LANES), 0)
    prev = []
    for b in range(n_seq):
        x = st[b * per:(b + 1) * per]
        k = 0
        while (1 << k) < per:
            d = 1 << k
            sh = jnp.where(row >= d, pltpu.roll(x, d, 0), 0.0)
            x = x + sh * lr[k:k + 1, :] + pltpu.roll(sh, SSM_STATE, 1) * li[k:k + 1, :]
            k += 1
        prev.append(jnp.where(row >= 1, pltpu.roll(x, 1, 0), 0.0))
    xp = jnp.concatenate(prev, axis=0).astype(BF16)
    y = y + lax.dot_general(xp, m_ref[...], _NT, preferred_element_type=F32) + d2_ref[0] * u.astype(F32)
    y_ref[0] = _gelu(y).astype(BF16)


def _ssm(u2, lam, bc, d2, n_seq):
    g, rows, w = u2.shape
    assert rows // n_seq <= 1 << 8, "lam_bar^(16*2^k) is prepared for 8 scan steps"
    blk = lambda a: pl.BlockSpec((1,) + a.shape[1:], lambda i: (i,) + (0,) * (a.ndim - 1))
    return pl.pallas_call(
        functools.partial(_ssm_body, n_seq=n_seq),
        grid=(g,),
        in_specs=[blk(u2), blk(lam), blk(bc), blk(d2)],
        out_specs=blk(u2),
        out_shape=jax.ShapeDtypeStruct(u2.shape, BF16),
        scratch_shapes=[pltpu.VMEM((w, w), BF16),
                        pltpu.VMEM((w, 2 * SSM_STATE), F32),
                        pltpu.VMEM((w, 2 * SSM_STATE), BF16),
                        pltpu.VMEM((w, 2 * SSM_STATE), BF16)],
        compiler_params=pltpu.CompilerParams(dimension_semantics=("arbitrary",),
                                             vmem_limit_bytes=VMEM_LIMIT),
        name="ssm",
    )(u2, lam, bc, d2)


def _mix_tile(x_ref, y2_ref, yb_ref, yc_ref, gmix_ref, wgate_ref, bgate_ref, wglu_ref, bglu_ref,
              wbr_ref, wout_ref, tok_ref, h_ref, hkeep_ref):
    x = x_ref[...]
    n = _rms(x, gmix_ref[...]).astype(BF16)

    def gated(b, c, y):
        cols = pl.ds(b * D_MODEL + c * MERGE_COLS, MERGE_COLS)
        gate = _sigmoid(_dot(n, wgate_ref[:, cols]) + bgate_ref[:, cols])
        return gate * _dot(y, wbr_ref[b, :, pl.ds(c * MERGE_COLS, MERGE_COLS)])

    n_blocks = D_MODEL // MERGE_COLS
    yb = yb_ref[...]
    yc = yc_ref[...]
    head = gated(1, 0, yb) + gated(2, 0, yc)
    yield
    _chunks_to_tokens(y2_ref, tok_ref)
    ys = jnp.concatenate([tok_ref[k] for k in range(SSM_WIDTH // LANES)], axis=1).astype(BF16)
    glu = _dot(ys, wglu_ref[...]) + bglu_ref[...]
    ya = (glu[:, :SSM_WIDTH] * _sigmoid(glu[:, SSM_WIDTH:])).astype(BF16)
    merged = [(head + gated(0, 0, ya)).astype(BF16)]
    for c in range(1, n_blocks):
        yield
        merged.append((gated(0, c, ya) + gated(1, c, yb) + gated(2, c, yc)).astype(BF16))
    yield
    h = x + _dot(jnp.concatenate(merged, axis=1), wout_ref[...])
    h_ref[...] = h
    hkeep_ref[...] = h


def _route_tile(hkeep_ref, gffn_ref, wrt_ref, brt_ref, xnp_ref, rt_ref, rtt_ref, cnt_ref, carry_ref):
    h = hkeep_ref[...]
    xn = _rms(h, gffn_ref[...])
    packed = _pack_bf16_pair(xn)
    for j in range(SC_SPLIT):
        xnp_ref[j] = packed[:, j * SC_ROW:(j + 1) * SC_ROW]
    yield

    x_hi = xn.astype(BF16)
    x_lo = (xn - x_hi.astype(F32)).astype(BF16)
    head = _dot(x_hi, wrt_ref[...])
    logits = (head[:, :LANES] + head[:, LANES:] + _dot(x_lo, wrt_ref[:, :LANES])) + brt_ref[...]
    yield
    tm = logits.shape[0]
    lane_i = lax.broadcasted_iota(jnp.int32, (tm, LANES), 1)
    lane = lane_i.astype(F32)
    neg = jnp.float32(-3.0e38)
    big = jnp.float32(LANES)
    gmask = lane_i < N_GROUPS
    gl = jnp.where(gmask, logits, neg)
    gmax = jnp.max(gl, axis=-1, keepdims=True)
    gidx = jnp.min(jnp.where(gl == gmax, lane, big), axis=-1, keepdims=True)
    gsum = jnp.sum(jnp.where(gmask, jnp.exp(gl - gmax), 0.0), axis=-1, keepdims=True)
    g_w = 1.0 / gsum
    e_lane = lane_i - ROUTE_LANE0
    lane_group = (e_lane >> 3).astype(F32)
    emask = (e_lane >= 0) & (e_lane < N_EXPERTS) & (lane_group == gidx)
    el = jnp.where(emask, logits, neg)
    m1 = jnp.max(el, axis=-1, keepdims=True)
    i1 = jnp.min(jnp.where(el == m1, lane, big), axis=-1, keepdims=True)
    el2 = jnp.where(lane == i1, neg, el)
    m2 = jnp.max(el2, axis=-1, keepdims=True)
    i2 = jnp.min(jnp.where(el2 == m2, lane, big), axis=-1, keepdims=True)
    t = jnp.exp(m2 - m1)
    w1 = g_w / (1.0 + t)
    w2 = g_w * t / (1.0 + t)
    yield

    sel1 = lane == i1
    sel2 = lane == i2
    onehot = jnp.where(sel1 | sel2, 1.0, 0.0)
    r_i = lax.broadcasted_iota(jnp.int32, (tm, tm), 0)
    c_i = lax.broadcasted_iota(jnp.int32, (tm, tm), 1)
    stril = jnp.where(c_i < r_i, 1.0, 0.0).astype(BF16)
    cum = _dot(stril, onehot.astype(BF16)) + carry_ref[0:1, :]
    rank1 = jnp.sum(jnp.where(sel1, cum, 0.0), axis=-1, keepdims=True)
    rank2 = jnp.sum(jnp.where(sel2, cum, 0.0), axis=-1, keepdims=True)
    carry_ref[...] = carry_ref[...] + jnp.sum(onehot, axis=0, keepdims=True)
    cnt_ref[...] = carry_ref[...]
    yield

    cols = (i1 - ROUTE_LANE0, i2 - ROUTE_LANE0, rank1, rank2, w1, w2)
    rt = jnp.zeros((tm, LANES), F32)
    for c, val in enumerate(cols):
        rt = jnp.where(lane_i == c, val, rt)
    rt_ref[...] = rt
    rtt_ref[...] = rt.T[:8]


def _merge_body(x_ref, y2_ref, yb_ref, yc_ref, gmix_ref, wgate_ref, bgate_ref, wglu_ref, bglu_ref,
                wbr_ref, wout_ref, gffn_ref, wrt_ref, brt_ref,
                h_ref, xnp_ref, rt_ref, rtt_ref, cnt_ref, carry_ref, tok_ref, hkeep_ref):
    i = pl.program_id(0)
    last = pl.num_programs(0) - 1
    cur = hkeep_ref.at[i % 2]
    prev = hkeep_ref.at[(i + 1) % 2]

    def mix():
        return _mix_tile(x_ref, y2_ref, yb_ref, yc_ref, gmix_ref, wgate_ref, bgate_ref, wglu_ref,
                         bglu_ref, wbr_ref, wout_ref, tok_ref, h_ref, cur)

    def route():
        return _route_tile(prev, gffn_ref, wrt_ref, brt_ref, xnp_ref, rt_ref, rtt_ref, cnt_ref, carry_ref)

    @pl.when(i == 0)
    def _():
        carry_ref[...] = jnp.zeros_like(carry_ref)
        _alternate(mix())

    @pl.when((i > 0) & (i < last))
    def _():
        _alternate(route(), mix())

    @pl.when(i == last)
    def _():
        _alternate(route())


def _merge_route(x, y2, yb, yc, g_mix, w_gate, b_gate, w_glu, b_glu, w_br, w_out, g_ffn, w_rt, b_rt):
    t, d = x.shape
    tm = TM_MERGE
    tiles = t // tm
    mixed = lambda i: jnp.minimum(i, tiles - 1)
    routed = lambda i: jnp.maximum(i - 1, 0)
    c2 = lambda i: (0, 0)
    c3 = lambda i: (0, 0, 0)
    full = lambda a: pl.BlockSpec(a.shape, c2 if a.ndim == 2 else c3)
    return pl.pallas_call(
        _merge_body,
        grid=(tiles + 1,),
        in_specs=[pl.BlockSpec((tm, d), lambda i: (mixed(i), 0)),
                  pl.BlockSpec((SSM_GROUPS, tm // SSM_CHUNK, SSM_CHUNK * SSM_GROUP),
                               lambda i: (0, mixed(i), 0)),
                  pl.BlockSpec((tm, SSM_WIDTH), lambda i: (mixed(i), 0)),
                  pl.BlockSpec((tm, SSM_WIDTH), lambda i: (mixed(i), 0)),
                  full(g_mix), full(w_gate), full(b_gate), full(w_glu), full(b_glu),
                  full(w_br), full(w_out), full(g_ffn), full(w_rt), full(b_rt)],
        out_specs=[pl.BlockSpec((tm, d), lambda i: (mixed(i), 0)),
                   pl.BlockSpec((SC_SPLIT, tm, SC_ROW), lambda i: (0, routed(i), 0)),
                   pl.BlockSpec((tm, LANES), lambda i: (routed(i), 0)),
                   pl.BlockSpec((8, tm), lambda i: (0, routed(i))),
                   pl.BlockSpec((8, LANES), c2)],
        out_shape=[jax.ShapeDtypeStruct((t, d), F32),
                   jax.ShapeDtypeStruct((SC_SPLIT, t, SC_ROW), jnp.uint32),
                   jax.ShapeDtypeStruct((t, LANES), F32),
                   jax.ShapeDtypeStruct((8, t), F32),
                   jax.ShapeDtypeStruct((8, LANES), F32)],
        scratch_shapes=[pltpu.VMEM((8, LANES), F32),
                        pltpu.VMEM((SSM_WIDTH // LANES, tm, LANES), F32),
                        pltpu.VMEM((2, tm, d), F32)],
        compiler_params=pltpu.CompilerParams(dimension_semantics=("arbitrary",),
                                             vmem_limit_bytes=VMEM_LIMIT),
        name="merge_route",
    )(x, y2, yb, yc, g_mix, w_gate, b_gate, w_glu, b_glu, w_br, w_out, g_ffn, w_rt, b_rt)


def _slot_body(rtt_ref, pstart_ref, out_ref, *, nslots):
    tl = rtt_ref.shape[1]
    expert = lax.broadcasted_iota(jnp.int32, (N_EXPERTS, tl), 0).astype(F32)
    for k in range(TOP_K):
        onehot = rtt_ref[k:k + 1, :] == expert
        start = jnp.sum(jnp.where(onehot, pstart_ref[...], 0.0), axis=0, keepdims=True)
        slot = (start + rtt_ref[TOP_K + k:TOP_K + k + 1, :]).astype(jnp.int32)
        for j in range(SC_SPLIT):
            out_ref[k * SC_SPLIT + j:k * SC_SPLIT + j + 1, :] = slot + j * nslots


def _slot_rows(rtt, pstart, nslots):
    t = rtt.shape[1]
    tl = SLOT_LANES
    return pl.pallas_call(
        functools.partial(_slot_body, nslots=nslots),
        grid=(t // tl,),
        in_specs=[pl.BlockSpec((8, tl), lambda i: (0, i)),
                  pl.BlockSpec((N_EXPERTS, 1), lambda i: (0, 0))],
        out_specs=pl.BlockSpec((TOP_K * SC_SPLIT, tl), lambda i: (0, i)),
        out_shape=jax.ShapeDtypeStruct((TOP_K * SC_SPLIT, t), jnp.int32),
        compiler_params=pltpu.CompilerParams(dimension_semantics=("arbitrary",)),
        name="slot_rows",
    )(rtt, pstart.astype(F32).reshape(N_EXPERTS, 1))


def _expert_body(meta_ref, buf_ref, w1_hbm, w3_hbm, w2_hbm, out_ref, w1_buf, w3_buf, w2_buf, sem):
    i = pl.program_id(0)
    expert = meta_ref[0, i]
    valid = meta_ref[1, i]
    slot = meta_ref[2, i]
    ahead1 = meta_ref[4, i]
    ahead2 = meta_ref[5, i]

    def weight_copies(e, s):
        return (pltpu.make_async_copy(w1_hbm.at[e], w1_buf.at[s], sem.at[s, 0]),
                pltpu.make_async_copy(w3_hbm.at[e], w3_buf.at[s], sem.at[s, 1]),
                pltpu.make_async_copy(w2_hbm.at[e], w2_buf.at[s], sem.at[s, 2]))

    @pl.when(i == 0)
    def _():
        for c in weight_copies(expert, slot):
            c.start()

        @pl.when(ahead1 >= 0)
        def _():
            for c in weight_copies(ahead1, (slot + 1) % WEIGHT_SLOTS):
                c.start()

    @pl.when(meta_ref[3, i] == 1)
    def _():
        for c in weight_copies(expert, slot):
            c.wait()

        @pl.when(ahead2 >= 0)
        def _():
            for c in weight_copies(ahead2, (slot + 2) % WEIGHT_SLOTS):
                c.start()

    def mlp(rows):
        x = _unpack_bf16_pair(jnp.concatenate([buf_ref[j, :rows, :] for j in range(SC_SPLIT)], axis=1))
        row = lax.broadcasted_iota(jnp.int32, x.shape, 0)
        x = jnp.where(row < valid, x, 0.0).astype(BF16)
        h1 = _dot(x, w1_buf[slot].astype(BF16))
        h3 = _dot(x, w3_buf[slot].astype(BF16))
        a = (h1 * _sigmoid(h1) * h3).astype(BF16)
        packed = _pack_bf16_pair(_dot(a, w2_buf[slot].astype(BF16)))
        for j in range(SC_SPLIT):
            out_ref[j, :rows, :] = packed[:, j * SC_ROW:(j + 1) * SC_ROW]
            if rows < BM:
                out_ref[j, rows:, :] = jnp.zeros((BM - rows, SC_ROW), out_ref.dtype)

    @pl.when(valid > BM // 2)
    def _():
        mlp(BM)

    @pl.when((valid > 0) & (valid <= BM // 2))
    def _():
        mlp(BM // 2)

    @pl.when(valid <= 0)
    def _():
        out_ref[...] = jnp.zeros_like(out_ref)


def _expert_meta(block_expert, block_valid):
    nb = block_expert.shape[0]
    pos = jnp.arange(nb, dtype=jnp.int32)
    prev = jnp.concatenate([jnp.full((1,), -1, jnp.int32), block_expert[:-1]])
    first = (block_valid > 0) & (block_expert != prev)
    run = jnp.cumsum(first.astype(jnp.int32)) - 1
    start_pos = jnp.where(first, pos, nb)
    later = jnp.concatenate([start_pos[1:], jnp.full((2,), nb, jnp.int32)])
    nearest = lax.cummin(later, axis=0, reverse=True)
    n1 = nearest[:nb]
    n2 = nearest[n1]
    experts_ext = jnp.concatenate([block_expert, jnp.full((1,), -1, jnp.int32)])
    return jnp.stack([block_expert, block_valid, run % WEIGHT_SLOTS, first.astype(jnp.int32),
                      experts_ext[n1], experts_ext[n2]]).astype(jnp.int32)


def _experts(block_expert, block_valid, buf, w1, w3, w2):
    _, nslots, _ = buf.shape
    nb = nslots // BM
    rows = pl.BlockSpec((SC_SPLIT, BM, SC_ROW), lambda i, meta: (0, i, 0))
    hbm = pl.BlockSpec(memory_space=pl.ANY)
    grid_spec = pltpu.PrefetchScalarGridSpec(
        num_scalar_prefetch=1,
        grid=(nb,),
        in_specs=[rows, hbm, hbm, hbm],
        out_specs=rows,
        scratch_shapes=[pltpu.VMEM((WEIGHT_SLOTS,) + w1.shape[1:], w1.dtype),
                        pltpu.VMEM((WEIGHT_SLOTS,) + w3.shape[1:], w3.dtype),
                        pltpu.VMEM((WEIGHT_SLOTS,) + w2.shape[1:], w2.dtype),
                        pltpu.SemaphoreType.DMA((WEIGHT_SLOTS, 3))],
    )
    return pl.pallas_call(
        _expert_body,
        grid_spec=grid_spec,
        out_shape=jax.ShapeDtypeStruct(buf.shape, jnp.uint32),
        compiler_params=pltpu.CompilerParams(dimension_semantics=("arbitrary",),
                                             vmem_limit_bytes=VMEM_LIMIT),
        name="experts",
    )(_expert_meta(block_expert, block_valid), buf, w1, w3, w2)


def _sc_mesh():
    return plsc.VectorSubcoreMesh(core_axis_name="core", subcore_axis_name="subcore")


def _dispatch_rows(rows, dest0, dest1, nslots):
    t, w = rows.shape
    win = SC_WINDOW
    idx_spec = pl.BlockSpec((1, win), lambda i: (0, i))

    @functools.partial(pl.kernel, mesh=_sc_mesh(), scratch_types=[],
                       out_type=jax.ShapeDtypeStruct((nslots, w), rows.dtype), name="dispatch_rows")
    def run(rows_hbm, i0_hbm, i1_hbm, out_hbm):
        def body(rows_vmem, i0_vmem, i1_vmem):
            pltpu.sync_copy(rows_vmem, out_hbm.at[i0_vmem.at[0]])
            pltpu.sync_copy(rows_vmem, out_hbm.at[i1_vmem.at[0]])

        pltpu.emit_pipeline(
            body, grid=(t // win,),
            in_specs=[pl.BlockSpec((win, w), lambda i: (i, 0)), idx_spec, idx_spec],
            out_specs=[],
            core_axis_name=("core", "subcore"),
            dimension_semantics=(pltpu.PARALLEL,),
        )(rows_hbm, i0_hbm, i1_hbm)

    return run(rows, dest0.reshape(1, t), dest1.reshape(1, t))


def _gather_rows(table, idx):
    n = idx.shape[0]
    w = table.shape[1]
    win = SC_WINDOW

    @functools.partial(pl.kernel, mesh=_sc_mesh(), scratch_types=[],
                       out_type=jax.ShapeDtypeStruct((n, w), table.dtype), name="gather_rows")
    def run(table_hbm, i_hbm, out_hbm):
        def body(i_vmem, out_vmem):
            pltpu.sync_copy(table_hbm.at[i_vmem.at[0]], out_vmem)

        pltpu.emit_pipeline(
            body, grid=(n // win,),
            in_specs=[pl.BlockSpec((1, win), lambda i: (0, i))],
            out_specs=[pl.BlockSpec((win, w), lambda i: (i, 0))],
            core_axis_name=("core", "subcore"),
            dimension_semantics=(pltpu.PARALLEL,),
        )(i_hbm, out_hbm)

    return run(table, idx.reshape(1, n))


def _combine_body(h_ref, g_ref, rt_ref, gfin_ref, out_ref):
    rt = rt_ref[...]
    y = h_ref[...]
    for k in range(TOP_K):
        rows = jnp.concatenate([g_ref[k * SC_SPLIT + j] for j in range(SC_SPLIT)], axis=1)
        y = y + rt[:, 4 + k:5 + k] * _unpack_bf16_pair(rows)
    out_ref[...] = _rms(y, gfin_ref[...])


def _combine(h, g, rt, g_final, part):
    t, d = h.shape
    tp = g.shape[1]
    tm = TM_OUT
    first = part * (tp // tm)
    tok = lambda i: (first + i, 0)
    return pl.pallas_call(
        _combine_body,
        grid=(tp // tm,),
        in_specs=[pl.BlockSpec((tm, d), tok),
                  pl.BlockSpec((TOP_K * SC_SPLIT, tm, SC_ROW), lambda i: (0, i, 0)),
                  pl.BlockSpec((tm, LANES), tok),
                  pl.BlockSpec((1, d), lambda i: (0, 0))],
        out_specs=pl.BlockSpec((tm, d), tok),
        out_shape=jax.ShapeDtypeStruct((t, d), F32),
        input_output_aliases={0: 0},
        compiler_params=pltpu.CompilerParams(dimension_semantics=("arbitrary",),
                                             vmem_limit_bytes=VMEM_LIMIT),
        name="combine",
    )(h, g, rt, g_final)


def _layer(h, mem, g_mix, g_mem, w_in, w_gate, b_gate, lam_re, lam_im, log_dt, b_re, b_im,
           c_re, c_im, d_skip, w_glu, b_glu, g_sgu, w_spatial, b_spatial, w_kv, w_branch,
           w_out, g_ffn, w_group, b_group, w_router, b_router, w1, w3, w2, g_out):
    bsz, s, d = h.shape
    t = bsz * s
    row = lambda a: a.reshape(1, -1)

    k, v = _kv_proj(mem, row(g_mem), w_kv.astype(BF16))

    tril = jnp.tril(jnp.ones((CHUNK, CHUNK), dtype=bool))
    w_sp = jnp.where(tril, w_spatial, 0.0).astype(BF16)
    b_sp = jnp.broadcast_to(b_spatial[:, :, None], (SGU_HEADS, CHUNK, SGU_HEAD_DIM))
    u2, y_b, y_c = _in_proj(h, row(g_mix), w_in.astype(BF16), row(g_sgu), w_sp, b_sp, k, v)
    y2 = _ssm(u2, *_ssm_params(lam_re, lam_im, log_dt, b_re, b_im, c_re, c_im, d_skip), n_seq=bsz)

    pad = LANES - N_GROUPS - N_EXPERTS
    w_rt = jnp.concatenate([w_group, w_router, jnp.zeros((d, pad), F32)], axis=1)
    w_rt_hi = w_rt.astype(BF16)
    w_rt = jnp.concatenate([w_rt_hi, (w_rt - w_rt_hi.astype(F32)).astype(BF16)], axis=1)
    b_rt =jnp.concatenate([b_group, b_router, jnp.zeros((pad,), F32)]).reshape(1, LANES)
    h2, xnp, rt, rtt, cnt = _merge_route(
        h.reshape(t, d), y2, y_b.reshape(t, -1), y_c.reshape(t, -1), row(g_mix),
        w_gate.astype(BF16), row(b_gate), w_glu.astype(BF16), row(b_glu),
        w_branch.astype(BF16), w_out.astype(BF16), row(g_ffn), w_rt, b_rt)

    counts = cnt[0, ROUTE_LANE0:ROUTE_LANE0 + N_EXPERTS].astype(jnp.int32)
    padded = (counts + BM - 1) // BM * BM
    block_end = jnp.cumsum(padded)
    pstart = block_end - padded
    nb = (t * TOP_K) // BM + N_EXPERTS
    blk_row0 = jnp.arange(nb, dtype=jnp.int32) * BM
    block_expert = jnp.sum((block_end[None, :] <= blk_row0[:, None]).astype(jnp.int32), axis=1)
    block_expert = jnp.minimum(block_expert, N_EXPERTS - 1)
    block_valid = jnp.clip(counts[block_expert] - (blk_row0 - pstart[block_expert]), 0, BM)
    block_valid = jnp.where(blk_row0 < block_end[-1], block_valid, 0).astype(jnp.int32)
    nslots = nb * BM
    dest = _slot_rows(rtt, pstart, nslots)
    dest_p = dest.reshape(TOP_K, SC_SPLIT * t)
    buf = _dispatch_rows(xnp.reshape(SC_SPLIT * t, SC_ROW), dest_p[0], dest_p[1], SC_SPLIT * nslots)
    yb = _experts(block_expert, block_valid, buf.reshape(SC_SPLIT, nslots, SC_ROW), w1, w3, w2)
    table = yb.reshape(SC_SPLIT * nslots, SC_ROW)
    tp = t // COMBINE_PARTS
    out = h2
    for part in range(COMBINE_PARTS):
        g = _gather_rows(table, dest[:, part * tp:(part + 1) * tp].reshape(-1))
        out = _combine(out, g.reshape(TOP_K * SC_SPLIT, tp, SC_ROW), rt, row(g_out), part)
    return out.reshape(bsz, s, d)


def kernel(x, mem, g_mix, g_mem, w_in, w_gate, b_gate, lam_re, lam_im, log_dt, b_re, b_im, c_re,
           c_im, d_skip, w_glu, b_glu, g_sgu, w_spatial, b_spatial, w_kv, w_branch, w_out, g_ffn,
           w_group, b_group, w_router, b_router, w1, w3, w2, g_final):
    assert g_mix.shape[0] == 1, "single-layer stack"
    return _layer(x, mem, g_mix[0], g_mem[0], w_in[0], w_gate[0], b_gate[0], lam_re[0], lam_im[0],
                  log_dt[0], b_re[0], b_im[0], c_re[0], c_im[0], d_skip[0], w_glu[0], b_glu[0],
                  g_sgu[0], w_spatial[0], b_spatial[0], w_kv[0], w_branch[0], w_out[0], g_ffn[0],
                  w_group[0], b_group[0], w_router[0], b_router[0], w1[0], w3[0], w2[0], g_final)
```

```python
import functools
import math

import jax
import jax.numpy as jnp
from jax import lax
from jax.experimental import pallas as pl
from jax.experimental.pallas import tpu as pltpu
from jax.experimental.pallas import tpu_sc as plsc

F32 = jnp.float32
BF16 = jnp.bfloat16

EPS = 1e-6
D_MODEL = 1024
SSM_WIDTH = 512
SSM_GROUP = 16
SSM_GROUPS = 32
SSM_STATE = 64
SSM_CHUNK = 16
SGU_WIDTH = 512
SGU_HEADS = 4
SGU_HEAD_DIM = 128
CHUNK = 128
XA_HEADS = 4
XA_HEAD_DIM = 128
N_GROUPS = 8
EXPERTS_PER_GROUP = 8
N_EXPERTS = 64
TOP_K = 2
D_FF = 512
LANES = 128
ROUTE_LANE0 = N_GROUPS

TM_IN = 512
TM_MERGE = 512
MERGE_COLS = 256
TM_OUT = 512
BM = 512
WEIGHT_SLOTS = 3
SC_WINDOW = 128
SC_ROW = 256
SC_SPLIT = (D_MODEL // 2) // SC_ROW
SLOT_LANES = 2048
VMEM_LIMIT = 56 * 1024 * 1024


def _rms(x, g):
    return x * lax.rsqrt(jnp.mean(x * x, axis=-1, keepdims=True) + EPS) * g


def _sigmoid(x):
    return 0.5 * (1.0 + jnp.tanh(0.5 * x))


def _gelu(x):
    c = math.sqrt(2.0 / math.pi)
    return 0.5 * x * (1.0 + jnp.tanh(c * (x + 0.044715 * (x * x * x))))


def _dot(a, b):
    return jnp.dot(a, b, preferred_element_type=F32)


_NT = (((1,), (1,)), ((), ()))


def _pack_bf16_pair(x):
    n = x.shape[1] // 2
    lo = lax.bitcast_convert_type(x[:, :n].astype(BF16).astype(F32), jnp.uint32)
    hi = lax.bitcast_convert_type(x[:, n:].astype(BF16).astype(F32), jnp.uint32)
    return hi | (lo >> 16)


def _unpack_bf16_pair(p):
    lo = lax.bitcast_convert_type(p << 16, F32)
    hi = lax.bitcast_convert_type(p & jnp.uint32(0xFFFF0000), F32)
    return jnp.concatenate([lo, hi], axis=1)


GROUPS_PER_TILE = LANES // SSM_GROUP
POS_PER_TILE = LANES // SSM_GROUP


def _slot_masks(rows):
    lane = lax.broadcasted_iota(jnp.int32, (rows, LANES), 1)
    return [(lane >= i * SSM_GROUP) & (lane < (i + 1) * SSM_GROUP) for i in range(LANES // SSM_GROUP)]


def _tokens_to_chunks(tok_ref, out_ref):
    tm = tok_ref.shape[1]
    nc = tm // SSM_CHUNK
    masks = _slot_masks(nc)
    for k in range(SSM_WIDTH // LANES):
        for j in range(SSM_CHUNK // POS_PER_TILE):
            src = [tok_ref[k, pl.ds(j * POS_PER_TILE + p, nc, stride=SSM_CHUNK), :]
                   for p in range(POS_PER_TILE)]
            for gi in range(GROUPS_PER_TILE):
                acc = None
                for p in range(POS_PER_TILE):
                    shift = ((p - gi) * SSM_GROUP) % LANES
                    r = pltpu.roll(src[p], shift, 1) if shift else src[p]
                    acc = r if acc is None else jnp.where(masks[p], r, acc)
                out_ref[k * GROUPS_PER_TILE + gi, :, pl.ds(j * LANES, LANES)] = acc.astype(out_ref.dtype)


def _chunks_to_tokens(chunk_ref, tok_ref):
    tm = tok_ref.shape[1]
    nc = tm // SSM_CHUNK
    masks = _slot_masks(nc)
    for k in range(SSM_WIDTH // LANES):
        for j in range(SSM_CHUNK // POS_PER_TILE):
            src = [chunk_ref[k * GROUPS_PER_TILE + gi, :, pl.ds(j * LANES, LANES)].astype(F32)
                   for gi in range(GROUPS_PER_TILE)]
            for p in range(POS_PER_TILE):
                acc = None
                for gi in range(GROUPS_PER_TILE):
                    shift = ((gi - p) * SSM_GROUP) % LANES
                    r = pltpu.roll(src[gi], shift, 1) if shift else src[gi]
                    acc = r if acc is None else jnp.where(masks[gi], r, acc)
                tok_ref[k, pl.ds(j * POS_PER_TILE + p, nc, stride=SSM_CHUNK), :] = acc


def _kv_body(mem_ref, g_ref, w_ref, k_ref, v_ref):
    n = _rms(mem_ref[0], g_ref[...]).astype(BF16)
    kv = _dot(n, w_ref[...])
    k_ref[0] = kv[:, :SGU_WIDTH].astype(BF16)
    v_ref[0] = kv[:, SGU_WIDTH:].astype(BF16)


def _kv_proj(mem, g_mem, w_kv):
    b, m, d = mem.shape
    return pl.pallas_call(
        _kv_body,
        grid=(b,),
        in_specs=[pl.BlockSpec((1, m, d), lambda i: (i, 0, 0)),
                  pl.BlockSpec((1, d), lambda i: (0, 0)),
                  pl.BlockSpec((d, 2 * SGU_WIDTH), lambda i: (0, 0))],
        out_specs=[pl.BlockSpec((1, m, SGU_WIDTH), lambda i: (i, 0, 0)),
                   pl.BlockSpec((1, m, SGU_WIDTH), lambda i: (i, 0, 0))],
        out_shape=[jax.ShapeDtypeStruct((b, m, SGU_WIDTH), BF16),
                   jax.ShapeDtypeStruct((b, m, SGU_WIDTH), BF16)],
        compiler_params=pltpu.CompilerParams(dimension_semantics=("arbitrary",),
                                             vmem_limit_bytes=VMEM_LIMIT),
        name="kv_proj",
    )(mem, g_mem, w_kv)


def _in_body(x_ref, gmix_ref, win_ref, gsgu_ref, wsp_ref, bsp_ref, k_ref, v_ref,
             u2_ref, yb_ref, yc_ref, tok_ref):
    n = _rms(x_ref[0], gmix_ref[...]).astype(BF16)
    proj = _dot(n, win_ref[...])
    for k in range(SSM_WIDTH // LANES):
        tok_ref[k] = proj[:, k * LANES:(k + 1) * LANES]
    _tokens_to_chunks(tok_ref, u2_ref)

    u = _gelu(proj[:, SSM_WIDTH:SSM_WIDTH + SGU_WIDTH])
    v = _gelu(proj[:, SSM_WIDTH + SGU_WIDTH:SSM_WIDTH + 2 * SGU_WIDTH])
    v = _rms(v, gsgu_ref[...]).astype(BF16)
    tm = u.shape[0]
    rows = []
    for c in range(tm // CHUNK):
        vc = v[c * CHUNK:(c + 1) * CHUNK]
        heads = []
        for h in range(SGU_HEADS):
            sl = slice(h * SGU_HEAD_DIM, (h + 1) * SGU_HEAD_DIM)
            heads.append(_dot(wsp_ref[h], vc[:, sl]) + bsp_ref[h])
        rows.append(jnp.concatenate(heads, axis=1))
    sv = jnp.concatenate(rows, axis=0)
    yb_ref[0] = (u * sv).astype(BF16)

    q = proj[:, SSM_WIDTH + 2 * SGU_WIDTH:].astype(BF16)
    kk = k_ref[0]
    vv = v_ref[0]
    outs = []
    for h in range(XA_HEADS):
        sl = slice(h * XA_HEAD_DIM, (h + 1) * XA_HEAD_DIM)
        s = lax.dot_general(q[:, sl], kk[:, sl], (((1,), (1,)), ((), ())),
                            preferred_element_type=F32) * (XA_HEAD_DIM ** -0.5)
        e = jnp.exp(s - jnp.max(s, axis=-1, keepdims=True))
        l = jnp.sum(e, axis=-1, keepdims=True)
        outs.append(_dot(e.astype(BF16), vv[:, sl]) / l)
    yc_ref[0] = jnp.concatenate(outs, axis=1).astype(BF16)


def _in_proj(x, g_mix, w_in, g_sgu, w_sp, b_sp, k, v):
    b, s, d = x.shape
    m = k.shape[1]
    const2 = lambda i, j: (0, 0)
    const3 = lambda i, j: (0, 0, 0)
    tok = lambda i, j: (i, j, 0)
    per_b = lambda i, j: (i, 0, 0)
    out = jax.ShapeDtypeStruct((b, s, SSM_WIDTH), BF16)
    nc = TM_IN // SSM_CHUNK
    tiles = s // TM_IN
    u2 = jax.ShapeDtypeStruct((SSM_GROUPS, b * s // SSM_CHUNK, SSM_CHUNK * SSM_GROUP), BF16)
    return pl.pallas_call(
        _in_body,
        grid=(b, s // TM_IN),
        in_specs=[pl.BlockSpec((1, TM_IN, d), tok),
                  pl.BlockSpec((1, d), const2),
                  pl.BlockSpec(w_in.shape, const2),
                  pl.BlockSpec((1, SGU_WIDTH), const2),
                  pl.BlockSpec(w_sp.shape, const3),
                  pl.BlockSpec(b_sp.shape, const3),
                  pl.BlockSpec((1, m, SGU_WIDTH), per_b),
                  pl.BlockSpec((1, m, SGU_WIDTH), per_b)],
        out_specs=[pl.BlockSpec((SSM_GROUPS, nc, SSM_CHUNK * SSM_GROUP), lambda i, j: (0, i * tiles + j, 0)),
                   pl.BlockSpec((1, TM_IN, SSM_WIDTH), tok),
                   pl.BlockSpec((1, TM_IN, SSM_WIDTH), tok)],
        out_shape=[u2, out, out],
        scratch_shapes=[pltpu.VMEM((SSM_WIDTH // LANES, TM_IN, LANES), F32)],
        compiler_params=pltpu.CompilerParams(dimension_semantics=("arbitrary", "arbitrary"),
                                             vmem_limit_bytes=VMEM_LIMIT),
        name="in_proj",
    )(x, g_mix, w_in, g_sgu, w_sp, b_sp, k, v)


def _alternate(*stages):
    live = list(stages)
    while live:
        live = [s for s in live if next(s, True) is None]


def _ssm_params(lam_re, lam_im, log_dt, b_re, b_im, c_re, c_im, d_skip):
    g, p = lam_re.shape
    dup = lambda a: jnp.concatenate([a, a], axis=-1)
    lam = jnp.stack([dup(lam_re), dup(lam_im), jnp.broadcast_to(log_dt[:, None], (g, 2 * p))], axis=1)
    brt = b_re.transpose(0, 2, 1)
    bit = b_im.transpose(0, 2, 1)
    cat = lambda a, b: jnp.concatenate([a, b], axis=-1)
    bc = jnp.stack([cat(brt, bit), cat(bit, brt), cat(c_re, -c_im), cat(-c_im, -c_re)], axis=1)
    d2 = jnp.tile(d_skip.reshape(g, 1, SSM_GROUP), (1, 1, SSM_CHUNK))
    return lam, bc, d2


def _ssm_operators(lam_ref, bc_ref, ccat_ref, n_ref, m_ref):
    lam_re = lam_ref[0, 0:1, :]
    lam_im = lam_ref[0, 1:2, :]
    dt = jnp.exp(lam_ref[0, 2:3, :])
    ar = lam_re * dt
    ai = lam_im * dt
    lane = lax.broadcasted_iota(jnp.int32, (1, LANES), 1)
    sgn = jnp.where(lane >= SSM_STATE, 1.0, -1.0)

    def powers(j):
        mag = jnp.exp(ar * j)
        ph = ai * j
        return mag * jnp.cos(ph), mag * jnp.sin(ph)

    pos = lax.broadcasted_iota(jnp.int32, (SSM_CHUNK, 1), 0).astype(F32)
    p_re, p_im = powers(pos)
    r_re, r_im = powers((SSM_CHUNK - 1) - pos)
    one_re, one_im = powers(jnp.ones((1, 1), F32))
    q_re = p_re * one_re - p_im * one_im
    q_im = p_re * one_im + p_im * one_re
    step = lax.shift_left(jnp.full((8, 1), SSM_CHUNK, jnp.int32),
                          lax.broadcasted_iota(jnp.int32, (8, 1), 0)).astype(F32)
    s_re, s_im = powers(step)

    den = lam_re * lam_re + lam_im * lam_im
    f_re = ((one_re - 1.0) * lam_re + one_im * lam_im) / den
    f_im = (one_im * lam_re - (one_re - 1.0) * lam_im) / den
    b1, b2, ca, cb = bc_ref[0, 0], bc_ref[0, 1], bc_ref[0, 2], bc_ref[0, 3]
    bb1 = f_re * b1 + (sgn * f_im) * b2
    bb2 = f_re * b2 - (sgn * f_im) * b1
    r_ims = sgn * r_im
    for s in range(SSM_CHUNK):
        blk = pl.ds(s * SSM_GROUP, SSM_GROUP)
        ccat_ref[blk, :] = ca * p_re[s:s + 1, :] + cb * p_im[s:s + 1, :]
        m_ref[blk, :] = (ca * q_re[s:s + 1, :] + cb * q_im[s:s + 1, :]).astype(m_ref.dtype)
        n_ref[blk, :] = (bb1 * r_re[s:s + 1, :] + bb2 * r_ims[s:s + 1, :]).astype(n_ref.dtype)
    return bb1, s_re, sgn * s_im


def _ssm_body(u_ref, lam_ref, bc_ref, d2_ref, y_ref, toep_ref, ccat_ref, n_ref, m_ref, *, n_seq):
    bcat, lr, li = _ssm_operators(lam_ref, bc_ref, ccat_ref, n_ref, m_ref)
    kern = lax.dot_general(bcat, ccat_ref[...], _NT, precision=lax.Precision.HIGHEST,
                           preferred_element_type=F32)
    col = lax.broadcasted_iota(jnp.int32, kern.shape, 1)
    for s in range(SSM_CHUNK):
        shifted = pltpu.roll(kern, s * SSM_GROUP, 1) if s else kern
        toep_ref[s * SSM_GROUP:(s + 1) * SSM_GROUP, :] = jnp.where(
            col >= s * SSM_GROUP, shifted, 0.0).astype(BF16)

    u = u_ref[0]
    rows = u.shape[0]
    per = rows // n_seq
    y = _dot(u, toep_ref[...])
    st = _dot(u, n_ref[...])
    row = lax.broadcasted_iota(jnp.int32, (per, LANES), 0)
    prev = []
    for b in range(n_seq):
        x = st[b * per:(b + 1) * per]
        k = 0
        while (1 << k) < per:
            d = 1 << k
            sh = jnp.where(row >= d, pltpu.roll(x, d, 0), 0.0)
            x = x + sh * lr[k:k + 1, :] + pltpu.roll(sh, SSM_STATE, 1) * li[k:k + 1, :]
            k += 1
        prev.append(jnp.where(row >= 1, pltpu.roll(x, 1, 0), 0.0))
    xp = jnp.concatenate(prev, axis=0).astype(BF16)
    y = y + lax.dot_general(xp, m_ref[...], _NT, preferred_element_type=F32) + d2_ref[0] * u.astype(F32)
    y_ref[0] = _gelu(y).astype(BF16)


def _ssm(u2, lam, bc, d2, n_seq):
    g, rows, w = u2.shape
    assert rows // n_seq <= 1 << 8, "lam_bar^(16*2^k) is prepared for 8 scan steps"
    blk = lambda a: pl.BlockSpec((1,) + a.shape[1:], lambda i: (i,) + (0,) * (a.ndim - 1))
    return pl.pallas_call(
        functools.partial(_ssm_body, n_seq=n_seq),
        grid=(g,),
        in_specs=[blk(u2), blk(lam), blk(bc), blk(d2)],
        out_specs=blk(u2),
        out_shape=jax.ShapeDtypeStruct(u2.shape, BF16),
        scratch_shapes=[pltpu.VMEM((w, w), BF16),
                        pltpu.VMEM((w, 2 * SSM_STATE), F32),
                        pltpu.VMEM((w, 2 * SSM_STATE), BF16),
                        pltpu.VMEM((w, 2 * SSM_STATE), BF16)],
        compiler_params=pltpu.CompilerParams(dimension_semantics=("arbitrary",),
                                             vmem_limit_bytes=VMEM_LIMIT),
        name="ssm",
    )(u2, lam, bc, d2)


def _mix_tile(x_ref, y2_ref, yb_ref, yc_ref, gmix_ref, wgate_ref, bgate_ref, wglu_ref, bglu_ref,
              wbr_ref, wout_ref, tok_ref, h_ref, hkeep_ref):
    x = x_ref[...]
    n = _rms(x, gmix_ref[...]).astype(BF16)

    def gated(b, c, y):
        cols = pl.ds(b * D_MODEL + c * MERGE_COLS, MERGE_COLS)
        gate = _sigmoid(_dot(n, wgate_ref[:, cols]) + bgate_ref[:, cols])
        return gate * _dot(y, wbr_ref[b, :, pl.ds(c * MERGE_COLS, MERGE_COLS)])

    n_blocks = D_MODEL // MERGE_COLS
    yb = yb_ref[...]
    yc = yc_ref[...]
    head = gated(1, 0, yb) + gated(2, 0, yc)
    yield
    _chunks_to_tokens(y2_ref, tok_ref)
    ys = jnp.concatenate([tok_ref[k] for k in range(SSM_WIDTH // LANES)], axis=1).astype(BF16)
    glu = _dot(ys, wglu_ref[...]) + bglu_ref[...]
    ya = (glu[:, :SSM_WIDTH] * _sigmoid(glu[:, SSM_WIDTH:])).astype(BF16)
    merged = [(head + gated(0, 0, ya)).astype(BF16)]
    for c in range(1, n_blocks):
        yield
        merged.append((gated(0, c, ya) + gated(1, c, yb) + gated(2, c, yc)).astype(BF16))
    yield
    h = x + _dot(jnp.concatenate(merged, axis=1), wout_ref[...])
    h_ref[...] = h
    hkeep_ref[...] = h


def _route_tile(hkeep_ref, gffn_ref, wrt_ref, brt_ref, xnp_ref, rt_ref, rtt_ref, cnt_ref, carry_ref):
    h = hkeep_ref[...]
    xn = _rms(h, gffn_ref[...])
    packed = _pack_bf16_pair(xn)
    for j in range(SC_SPLIT):
        xnp_ref[j] = packed[:, j * SC_ROW:(j + 1) * SC_ROW]
    yield

    x_hi = xn.astype(BF16)
    x_lo = (xn - x_hi.astype(F32)).astype(BF16)
    head = _dot(x_hi, wrt_ref[...])
    logits = (head[:, :LANES] + head[:, LANES:] + _dot(x_lo, wrt_ref[:, :LANES])) + brt_ref[...]
    yield
    tm = logits.shape[0]
    lane_i = lax.broadcasted_iota(jnp.int32, (tm, LANES), 1)
    lane = lane_i.astype(F32)
    neg = jnp.float32(-3.0e38)
    big = jnp.float32(LANES)
    gmask = lane_i < N_GROUPS
    gl = jnp.where(gmask, logits, neg)
    gmax = jnp.max(gl, axis=-1, keepdims=True)
    gidx = jnp.min(jnp.where(gl == gmax, lane, big), axis=-1, keepdims=True)
    gsum = jnp.sum(jnp.where(gmask, jnp.exp(gl - gmax), 0.0), axis=-1, keepdims=True)
    g_w = 1.0 / gsum
    e_lane = lane_i - ROUTE_LANE0
    lane_group = (e_lane >> 3).astype(F32)
    emask = (e_lane >= 0) & (e_lane < N_EXPERTS) & (lane_group == gidx)
    el = jnp.where(emask, logits, neg)
    m1 = jnp.max(el, axis=-1, keepdims=True)
    i1 = jnp.min(jnp.where(el == m1, lane, big), axis=-1, keepdims=True)
    el2 = jnp.where(lane == i1, neg, el)
    m2 = jnp.max(el2, axis=-1, keepdims=True)
    i2 = jnp.min(jnp.where(el2 == m2, lane, big), axis=-1, keepdims=True)
    t = jnp.exp(m2 - m1)
    w1 = g_w / (1.0 + t)
    w2 = g_w * t / (1.0 + t)
    yield

    sel1 = lane == i1
    sel2 = lane == i2
    onehot = jnp.where(sel1 | sel2, 1.0, 0.0)
    r_i = lax.broadcasted_iota(jnp.int32, (tm, tm), 0)
    c_i = lax.broadcasted_iota(jnp.int32, (tm, tm), 1)
    stril = jnp.where(c_i < r_i, 1.0, 0.0).astype(BF16)
    cum = _dot(stril, onehot.astype(BF16)) + carry_ref[0:1, :]
    rank1 = jnp.sum(jnp.where(sel1, cum, 0.0), axis=-1, keepdims=True)
    rank2 = jnp.sum(jnp.where(sel2, cum, 0.0), axis=-1, keepdims=True)
    carry_ref[...] = carry_ref[...] + jnp.sum(onehot, axis=0, keepdims=True)
    cnt_ref[...] = carry_ref[...]
    yield

    cols = (i1 - ROUTE_LANE0, i2 - ROUTE_LANE0, rank1, rank2, w1, w2)
    rt = jnp.zeros((tm, LANES), F32)
    for c, val in enumerate(cols):
        rt = jnp.where(lane_i == c, val, rt)
    rt_ref[...] = rt
    rtt_ref[...] = rt.T[:8]


def _merge_body(x_ref, y2_ref, yb_ref, yc_ref, gmix_ref, wgate_ref, bgate_ref, wglu_ref, bglu_ref,
                wbr_ref, wout_ref, gffn_ref, wrt_ref, brt_ref,
                h_ref, xnp_ref, rt_ref, rtt_ref, cnt_ref, carry_ref, tok_ref, hkeep_ref):
    i = pl.program_id(0)
    last = pl.num_programs(0) - 1
    cur = hkeep_ref.at[i % 2]
    prev = hkeep_ref.at[(i + 1) % 2]

    def mix():
        return _mix_tile(x_ref, y2_ref, yb_ref, yc_ref, gmix_ref, wgate_ref, bgate_ref, wglu_ref,
                         bglu_ref, wbr_ref, wout_ref, tok_ref, h_ref, cur)

    def route():
        return _route_tile(prev, gffn_ref, wrt_ref, brt_ref, xnp_ref, rt_ref, rtt_ref, cnt_ref, carry_ref)

    @pl.when(i == 0)
    def _():
        carry_ref[...] = jnp.zeros_like(carry_ref)
        _alternate(mix())

    @pl.when((i > 0) & (i < last))
    def _():
        _alternate(route(), mix())

    @pl.when(i == last)
    def _():
        _alternate(route())


def _merge_route(x, y2, yb, yc, g_mix, w_gate, b_gate, w_glu, b_glu, w_br, w_out, g_ffn, w_rt, b_rt):
    t, d = x.shape
    tm = TM_MERGE
    tiles = t // tm
    mixed = lambda i: jnp.minimum(i, tiles - 1)
    routed = lambda i: jnp.maximum(i - 1, 0)
    c2 = lambda i: (0, 0)
    c3 = lambda i: (0, 0, 0)
    full = lambda a: pl.BlockSpec(a.shape, c2 if a.ndim == 2 else c3)
    return pl.pallas_call(
        _merge_body,
        grid=(tiles + 1,),
        in_specs=[pl.BlockSpec((tm, d), lambda i: (mixed(i), 0)),
                  pl.BlockSpec((SSM_GROUPS, tm // SSM_CHUNK, SSM_CHUNK * SSM_GROUP),
                               lambda i: (0, mixed(i), 0)),
                  pl.BlockSpec((tm, SSM_WIDTH), lambda i: (mixed(i), 0)),
                  pl.BlockSpec((tm, SSM_WIDTH), lambda i: (mixed(i), 0)),
                  full(g_mix), full(w_gate), full(b_gate), full(w_glu), full(b_glu),
                  full(w_br), full(w_out), full(g_ffn), full(w_rt), full(b_rt)],
        out_specs=[pl.BlockSpec((tm, d), lambda i: (mixed(i), 0)),
                   pl.BlockSpec((SC_SPLIT, tm, SC_ROW), lambda i: (0, routed(i), 0)),
                   pl.BlockSpec((tm, LANES), lambda i: (routed(i), 0)),
                   pl.BlockSpec((8, tm), lambda i: (0, routed(i))),
                   pl.BlockSpec((8, LANES), c2)],
        out_shape=[jax.ShapeDtypeStruct((t, d), F32),
                   jax.ShapeDtypeStruct((SC_SPLIT, t, SC_ROW), jnp.uint32),
                   jax.ShapeDtypeStruct((t, LANES), F32),
                   jax.ShapeDtypeStruct((8, t), F32),
                   jax.ShapeDtypeStruct((8, LANES), F32)],
        scratch_shapes=[pltpu.VMEM((8, LANES), F32),
                        pltpu.VMEM((SSM_WIDTH // LANES, tm, LANES), F32),
                        pltpu.VMEM((2, tm, d), F32)],
        compiler_params=pltpu.CompilerParams(dimension_semantics=("arbitrary",),
                                             vmem_limit_bytes=VMEM_LIMIT),
        name="merge_route",
    )(x, y2, yb, yc, g_mix, w_gate, b_gate, w_glu, b_glu, w_br, w_out, g_ffn, w_rt, b_rt)


def _slot_body(rtt_ref, cnt_ref, out_ref, *, nslots):
    tl = rtt_ref.shape[1]
    padded = jnp.ceil(cnt_ref[...] * (1.0 / BM)) * BM
    k_i = lax.broadcasted_iota(jnp.int32, (LANES, LANES), 0)
    l_i = lax.broadcasted_iota(jnp.int32, (LANES, LANES), 1)
    before = jnp.where(k_i < l_i, 1.0, 0.0)
    pstart = jnp.dot(padded, before, precision=lax.Precision.HIGHEST, preferred_element_type=F32)
    lane_of = lax.broadcasted_iota(jnp.int32, (LANES, tl), 0).astype(F32) - ROUTE_LANE0
    for k in range(TOP_K):
        onehot = jnp.where(rtt_ref[k:k + 1, :] == lane_of, 1.0, 0.0)
        start = jnp.dot(pstart, onehot, precision=lax.Precision.HIGHEST, preferred_element_type=F32)
        slot = (start[0:1, :] + rtt_ref[TOP_K + k:TOP_K + k + 1, :]).astype(jnp.int32)
        for j in range(SC_SPLIT):
            out_ref[k * SC_SPLIT + j:k * SC_SPLIT + j + 1, :] = slot + j * nslots


def _slot_rows(rtt, cnt, nslots):
    t = rtt.shape[1]
    tl = SLOT_LANES
    return pl.pallas_call(
        functools.partial(_slot_body, nslots=nslots),
        grid=(t // tl,),
        in_specs=[pl.BlockSpec((8, tl), lambda i: (0, i)),
                  pl.BlockSpec((8, LANES), lambda i: (0, 0))],
        out_specs=pl.BlockSpec((TOP_K * SC_SPLIT, tl), lambda i: (0, i)),
        out_shape=jax.ShapeDtypeStruct((TOP_K * SC_SPLIT, t), jnp.int32),
        compiler_params=pltpu.CompilerParams(dimension_semantics=("arbitrary",)),
        name="slot_rows",
    )(rtt, cnt)


def _plan_blocks(counts_ref, blk_expert, blk_valid, blk_first, blk_run, run_expert):
    nb = blk_expert.shape[0]

    def per_expert(e, carry):
        cursor, run = carry
        count = counts_ref[e]
        n_blk = (count + (BM - 1)) // BM

        def per_block(b, cur):
            blk_expert[cur] = e
            blk_valid[cur] = jnp.minimum(count - b * BM, BM)
            blk_first[cur] = (b == 0).astype(jnp.int32)
            blk_run[cur] = run
            return cur + 1

        run_expert[run] = e
        return lax.fori_loop(0, n_blk, per_block, cursor), run + (n_blk > 0).astype(jnp.int32)

    cursor, runs = lax.fori_loop(0, counts_ref.shape[0], per_expert, (jnp.int32(0), jnp.int32(0)))
    run_expert[runs] = -1
    run_expert[runs + 1] = -1

    def empty(j, carry):
        blk_expert[j] = 0
        blk_valid[j] = 0
        blk_first[j] = 0
        blk_run[j] = runs - 1
        return carry

    lax.fori_loop(cursor, nb, empty, 0)


def _expert_body(counts_ref, buf_ref, w1_hbm, w3_hbm, w2_hbm, out_ref, w1_buf, w3_buf, w2_buf, sem,
                 blk_expert, blk_valid, blk_first, blk_run, run_expert):
    i = pl.program_id(0)

    @pl.when(i == 0)
    def _():
        _plan_blocks(counts_ref, blk_expert, blk_valid, blk_first, blk_run, run_expert)

    expert = blk_expert[i]
    valid = blk_valid[i]
    run = blk_run[i]
    slot = run % WEIGHT_SLOTS
    ahead1 = run_expert[run + 1]
    ahead2 = run_expert[run + 2]

    def weight_copies(e, s):
        return (pltpu.make_async_copy(w1_hbm.at[e], w1_buf.at[s], sem.at[s, 0]),
                pltpu.make_async_copy(w3_hbm.at[e], w3_buf.at[s], sem.at[s, 1]),
                pltpu.make_async_copy(w2_hbm.at[e], w2_buf.at[s], sem.at[s, 2]))

    @pl.when(i == 0)
    def _():
        for c in weight_copies(expert, slot):
            c.start()

        @pl.when(ahead1 >= 0)
        def _():
            for c in weight_copies(ahead1, (slot + 1) % WEIGHT_SLOTS):
                c.start()

    @pl.when(blk_first[i] == 1)
    def _():
        for c in weight_copies(expert, slot):
            c.wait()

        @pl.when(ahead2 >= 0)
        def _():
            for c in weight_copies(ahead2, (slot + 2) % WEIGHT_SLOTS):
                c.start()

    def mlp(rows):
        x = _unpack_bf16_pair(jnp.concatenate([buf_ref[j, :rows, :] for j in range(SC_SPLIT)], axis=1))
        row = lax.broadcasted_iota(jnp.int32, x.shape, 0)
        x = jnp.where(row < valid, x, 0.0).astype(BF16)
        h1 = _dot(x, w1_buf[slot].astype(BF16))
        h3 = _dot(x, w3_buf[slot].astype(BF16))
        a = (h1 * _sigmoid(h1) * h3).astype(BF16)
        packed = _pack_bf16_pair(_dot(a, w2_buf[slot].astype(BF16)))
        for j in range(SC_SPLIT):
            out_ref[j, :rows, :] = packed[:, j * SC_ROW:(j + 1) * SC_ROW]
            if rows < BM:
                out_ref[j, rows:, :] = jnp.zeros((BM - rows, SC_ROW), out_ref.dtype)

    @pl.when(valid > BM // 2)
    def _():
        mlp(BM)

    @pl.when((valid > 0) & (valid <= BM // 2))
    def _():
        mlp(BM // 2)

    @pl.when(valid <= 0)
    def _():
        out_ref[...] = jnp.zeros_like(out_ref)


def _experts(counts, buf, w1, w3, w2):
    _, nslots, _ = buf.shape
    nb = nslots // BM
    rows = pl.BlockSpec((SC_SPLIT, BM, SC_ROW), lambda i, counts: (0, i, 0))
    hbm = pl.BlockSpec(memory_space=pl.ANY)
    table = pltpu.SMEM((nb,), jnp.int32)
    grid_spec = pltpu.PrefetchScalarGridSpec(
        num_scalar_prefetch=1,
        grid=(nb,),
        in_specs=[rows, hbm, hbm, hbm],
        out_specs=rows,
        scratch_shapes=[pltpu.VMEM((WEIGHT_SLOTS,) + w1.shape[1:], w1.dtype),
                        pltpu.VMEM((WEIGHT_SLOTS,) + w3.shape[1:], w3.dtype),
                        pltpu.VMEM((WEIGHT_SLOTS,) + w2.shape[1:], w2.dtype),
                        pltpu.SemaphoreType.DMA((WEIGHT_SLOTS, 3)),
                        table, table, table, table,
                        pltpu.SMEM((counts.shape[0] + 2,), jnp.int32)],
    )
    return pl.pallas_call(
        _expert_body,
        grid_spec=grid_spec,
        out_shape=jax.ShapeDtypeStruct(buf.shape, jnp.uint32),
        compiler_params=pltpu.CompilerParams(dimension_semantics=("arbitrary",),
                                             vmem_limit_bytes=VMEM_LIMIT),
        name="experts",
    )(counts, buf, w1, w3, w2)


def _sc_mesh():
    return plsc.VectorSubcoreMesh(core_axis_name="core", subcore_axis_name="subcore")


def _dispatch_rows(rows, dest0, dest1, nslots):
    t, w = rows.shape
    win = SC_WINDOW
    idx_spec = pl.BlockSpec((1, win), lambda i: (0, i))

    @functools.partial(pl.kernel, mesh=_sc_mesh(), scratch_types=[],
                       out_type=jax.ShapeDtypeStruct((nslots, w), rows.dtype), name="dispatch_rows")
    def run(rows_hbm, i0_hbm, i1_hbm, out_hbm):
        def body(rows_vmem, i0_vmem, i1_vmem):
            pltpu.sync_copy(rows_vmem, out_hbm.at[i0_vmem.at[0]])
            pltpu.sync_copy(rows_vmem, out_hbm.at[i1_vmem.at[0]])

        pltpu.emit_pipeline(
            body, grid=(t // win,),
            in_specs=[pl.BlockSpec((win, w), lambda i: (i, 0)), idx_spec, idx_spec],
            out_specs=[],
            core_axis_name=("core", "subcore"),
            dimension_semantics=(pltpu.PARALLEL,),
        )(rows_hbm, i0_hbm, i1_hbm)

    return run(rows, dest0.reshape(1, t), dest1.reshape(1, t))


def _gather_rows(table, idx):
    n = idx.shape[0]
    w = table.shape[1]
    win = SC_WINDOW

    @functools.partial(pl.kernel, mesh=_sc_mesh(), scratch_types=[],
                       out_type=jax.ShapeDtypeStruct((n, w), table.dtype), name="gather_rows")
    def run(table_hbm, i_hbm, out_hbm):
        def body(i_vmem, out_vmem):
            pltpu.sync_copy(table_hbm.at[i_vmem.at[0]], out_vmem)

        pltpu.emit_pipeline(
            body, grid=(n // win,),
            in_specs=[pl.BlockSpec((1, win), lambda i: (0, i))],
            out_specs=[pl.BlockSpec((win, w), lambda i: (i, 0))],
            core_axis_name=("core", "subcore"),
            dimension_semantics=(pltpu.PARALLEL,),
        )(i_hbm, out_hbm)

    return run(table, idx.reshape(1, n))


def _combine_body(h_ref, g_ref, rt_ref, gfin_ref, out_ref):
    rt = rt_ref[...]
    y = h_ref[...]
    for k in range(TOP_K):
        rows = jnp.concatenate([g_ref[k * SC_SPLIT + j] for j in range(SC_SPLIT)], axis=1)
        y = y + rt[:, 4 + k:5 + k] * _unpack_bf16_pair(rows)
    out_ref[...] = _rms(y, gfin_ref[...])


def _combine(h, g, rt, g_final):
    t, d = h.shape
    tm = TM_OUT
    tok = lambda i: (i, 0)
    return pl.pallas_call(
        _combine_body,
        grid=(t // tm,),
        in_specs=[pl.BlockSpec((tm, d), tok),
                  pl.BlockSpec((TOP_K * SC_SPLIT, tm, SC_ROW), lambda i: (0, i, 0)),
                  pl.BlockSpec((tm, LANES), tok),
                  pl.BlockSpec((1, d), lambda i: (0, 0))],
        out_specs=pl.BlockSpec((tm, d), tok),
        out_shape=jax.ShapeDtypeStruct((t, d), F32),
        compiler_params=pltpu.CompilerParams(dimension_semantics=("arbitrary",),
                                             vmem_limit_bytes=VMEM_LIMIT),
        name="combine",
    )(h, g, rt, g_final)


def _layer(h, mem, g_mix, g_mem, w_in, w_gate, b_gate, lam_re, lam_im, log_dt, b_re, b_im,
           c_re, c_im, d_skip, w_glu, b_glu, g_sgu, w_spatial, b_spatial, w_kv, w_branch,
           w_out, g_ffn, w_group, b_group, w_router, b_router, w1, w3, w2, g_out):
    bsz, s, d = h.shape
    t = bsz * s
    row = lambda a: a.reshape(1, -1)

    k, v = _kv_proj(mem, row(g_mem), w_kv.astype(BF16))

    tril = jnp.tril(jnp.ones((CHUNK, CHUNK), dtype=bool))
    w_sp = jnp.where(tril, w_spatial, 0.0).astype(BF16)
    b_sp = jnp.broadcast_to(b_spatial[:, :, None], (SGU_HEADS, CHUNK, SGU_HEAD_DIM))
    u2, y_b, y_c = _in_proj(h, row(g_mix), w_in.astype(BF16), row(g_sgu), w_sp, b_sp, k, v)
    y2 = _ssm(u2, *_ssm_params(lam_re, lam_im, log_dt, b_re, b_im, c_re, c_im, d_skip), n_seq=bsz)

    pad = LANES - N_GROUPS - N_EXPERTS
    w_rt = jnp.concatenate([w_group, w_router, jnp.zeros((d, pad), F32)], axis=1)
    w_rt_hi = w_rt.astype(BF16)
    w_rt = jnp.concatenate([w_rt_hi, (w_rt - w_rt_hi.astype(F32)).astype(BF16)], axis=1)
    b_rt =jnp.concatenate([b_group, b_router, jnp.zeros((pad,), F32)]).reshape(1, LANES)
    h2, xnp, rt, rtt, cnt = _merge_route(
        h.reshape(t, d), y2, y_b.reshape(t, -1), y_c.reshape(t, -1), row(g_mix),
        w_gate.astype(BF16), row(b_gate), w_glu.astype(BF16), row(b_glu),
        w_branch.astype(BF16), w_out.astype(BF16), row(g_ffn), w_rt, b_rt)

    assert BM & (BM - 1) == 0, "block padding arithmetic assumes a power-of-two block"
    counts = cnt[0, ROUTE_LANE0:ROUTE_LANE0 + N_EXPERTS].astype(jnp.int32)
    nb = (t * TOP_K) // BM + N_EXPERTS
    nslots = nb * BM
    dest_p = _slot_rows(rtt, cnt, nslots).reshape(TOP_K, SC_SPLIT * t)
    buf = _dispatch_rows(xnp.reshape(SC_SPLIT * t, SC_ROW), dest_p[0], dest_p[1], SC_SPLIT * nslots)
    yb = _experts(counts, buf.reshape(SC_SPLIT, nslots, SC_ROW), w1, w3, w2)
    g = _gather_rows(yb.reshape(SC_SPLIT * nslots, SC_ROW), dest_p.reshape(-1))
    out = _combine(h2, g.reshape(TOP_K * SC_SPLIT, t, SC_ROW), rt, row(g_out))
    return out.reshape(bsz, s, d)


def kernel(x, mem, g_mix, g_mem, w_in, w_gate, b_gate, lam_re, lam_im, log_dt, b_re, b_im, c_re,
           c_im, d_skip, w_glu, b_glu, g_sgu, w_spatial, b_spatial, w_kv, w_branch, w_out, g_ffn,
           w_group, b_group, w_router, b_router, w1, w3, w2, g_final):
    assert g_mix.shape[0] == 1, "single-layer stack"
    return _layer(x, mem, g_mix[0], g_mem[0], w_in[0], w_gate[0], b_gate[0], lam_re[0], lam_im[0],
                  log_dt[0], b_re[0], b_im[0], c_re[0], c_im[0], d_skip[0], w_glu[0], b_glu[0],
                  g_sgu[0], w_spatial[0], b_spatial[0], w_kv[0], w_branch[0], w_out[0], g_ffn[0],
                  w_group[0], b_group[0], w_router[0], b_router[0], w1[0], w3[0], w2[0], g_final)
```

```python
import functools
import math

import jax
import jax.numpy as jnp
from jax import lax
from jax.experimental import pallas as pl
from jax.experimental.pallas import tpu as pltpu
from jax.experimental.pallas import tpu_sc as plsc

F32 = jnp.float32
BF16 = jnp.bfloat16

EPS = 1e-6
D_MODEL = 1024
SSM_WIDTH = 512
SSM_GROUP = 16
SSM_GROUPS = 32
SSM_STATE = 64
SSM_CHUNK = 16
SGU_WIDTH = 512
SGU_HEADS = 4
SGU_HEAD_DIM = 128
CHUNK = 128
XA_HEADS = 4
XA_HEAD_DIM = 128
N_GROUPS = 8
EXPERTS_PER_GROUP = 8
N_EXPERTS = 64
TOP_K = 2
D_FF = 512
LANES = 128
ROUTE_LANE0 = N_GROUPS

TM_IN = 1024
TM_MERGE = 512
MERGE_COLS = 256
TM_OUT = 1024
BM = 512
WEIGHT_SLOTS = 3
SC_WINDOW = 128
SC_ROW = 256
SC_SPLIT = (D_MODEL // 2) // SC_ROW
SLOT_LANES = 2048
VMEM_LIMIT = 56 * 1024 * 1024


def _rms(x, g):
    return x * lax.rsqrt(jnp.mean(x * x, axis=-1, keepdims=True) + EPS) * g


def _sigmoid(x):
    return 0.5 * (1.0 + jnp.tanh(0.5 * x))


def _gelu(x):
    c = math.sqrt(2.0 / math.pi)
    return 0.5 * x * (1.0 + jnp.tanh(c * (x + 0.044715 * (x * x * x))))


def _dot(a, b):
    return jnp.dot(a, b, preferred_element_type=F32)


_NT = (((1,), (1,)), ((), ()))


def _pack_bf16_pair(x):
    n = x.shape[1] // 2
    lo = lax.bitcast_convert_type(x[:, :n].astype(BF16).astype(F32), jnp.uint32)
    hi = lax.bitcast_convert_type(x[:, n:].astype(BF16).astype(F32), jnp.uint32)
    return hi | (lo >> 16)


def _unpack_bf16_pair(p):
    lo = lax.bitcast_convert_type(p << 16, F32)
    hi = lax.bitcast_convert_type(p & jnp.uint32(0xFFFF0000), F32)
    return jnp.concatenate([lo, hi], axis=1)


GROUPS_PER_TILE = LANES // SSM_GROUP
POS_PER_TILE = LANES // SSM_GROUP


def _slot_masks(rows):
    lane = lax.broadcasted_iota(jnp.int32, (rows, LANES), 1)
    return [(lane >= i * SSM_GROUP) & (lane < (i + 1) * SSM_GROUP) for i in range(LANES // SSM_GROUP)]


def _tokens_to_chunks(tok_ref, out_ref):
    tm = tok_ref.shape[1]
    nc = tm // SSM_CHUNK
    masks = _slot_masks(nc)
    for k in range(SSM_WIDTH // LANES):
        for j in range(SSM_CHUNK // POS_PER_TILE):
            src = [tok_ref[k, pl.ds(j * POS_PER_TILE + p, nc, stride=SSM_CHUNK), :]
                   for p in range(POS_PER_TILE)]
            for gi in range(GROUPS_PER_TILE):
                acc = None
                for p in range(POS_PER_TILE):
                    shift = ((p - gi) * SSM_GROUP) % LANES
                    r = pltpu.roll(src[p], shift, 1) if shift else src[p]
                    acc = r if acc is None else jnp.where(masks[p], r, acc)
                out_ref[k * GROUPS_PER_TILE + gi, :, pl.ds(j * LANES, LANES)] = acc.astype(out_ref.dtype)


def _chunks_to_tokens(chunk_ref, tok_ref):
    tm = tok_ref.shape[1]
    nc = tm // SSM_CHUNK
    masks = _slot_masks(nc)
    for k in range(SSM_WIDTH // LANES):
        for j in range(SSM_CHUNK // POS_PER_TILE):
            src = [chunk_ref[k * GROUPS_PER_TILE + gi, :, pl.ds(j * LANES, LANES)].astype(F32)
                   for gi in range(GROUPS_PER_TILE)]
            for p in range(POS_PER_TILE):
                acc = None
                for gi in range(GROUPS_PER_TILE):
                    shift = ((gi - p) * SSM_GROUP) % LANES
                    r = pltpu.roll(src[gi], shift, 1) if shift else src[gi]
                    acc = r if acc is None else jnp.where(masks[gi], r, acc)
                tok_ref[k, pl.ds(j * POS_PER_TILE + p, nc, stride=SSM_CHUNK), :] = acc


def _kv_body(mem_ref, g_ref, w_ref, k_ref, v_ref):
    n = _rms(mem_ref[0], g_ref[...]).astype(BF16)
    kv = _dot(n, w_ref[...])
    k_ref[0] = kv[:, :SGU_WIDTH].astype(BF16)
    v_ref[0] = kv[:, SGU_WIDTH:].astype(BF16)


def _kv_proj(mem, g_mem, w_kv):
    b, m, d = mem.shape
    return pl.pallas_call(
        _kv_body,
        grid=(b,),
        in_specs=[pl.BlockSpec((1, m, d), lambda i: (i, 0, 0)),
                  pl.BlockSpec((1, d), lambda i: (0, 0)),
                  pl.BlockSpec((d, 2 * SGU_WIDTH), lambda i: (0, 0))],
        out_specs=[pl.BlockSpec((1, m, SGU_WIDTH), lambda i: (i, 0, 0)),
                   pl.BlockSpec((1, m, SGU_WIDTH), lambda i: (i, 0, 0))],
        out_shape=[jax.ShapeDtypeStruct((b, m, SGU_WIDTH), BF16),
                   jax.ShapeDtypeStruct((b, m, SGU_WIDTH), BF16)],
        compiler_params=pltpu.CompilerParams(dimension_semantics=("arbitrary",),
                                             vmem_limit_bytes=VMEM_LIMIT),
        name="kv_proj",
    )(mem, g_mem, w_kv)


def _in_body(x_ref, gmix_ref, win_ref, gsgu_ref, wsp_ref, bsp_ref, k_ref, v_ref,
             u2_ref, yb_ref, yc_ref, tok_ref):
    n = _rms(x_ref[0], gmix_ref[...]).astype(BF16)
    proj = _dot(n, win_ref[...])
    for k in range(SSM_WIDTH // LANES):
        tok_ref[k] = proj[:, k * LANES:(k + 1) * LANES]
    _tokens_to_chunks(tok_ref, u2_ref)

    u = _gelu(proj[:, SSM_WIDTH:SSM_WIDTH + SGU_WIDTH])
    v = _gelu(proj[:, SSM_WIDTH + SGU_WIDTH:SSM_WIDTH + 2 * SGU_WIDTH])
    v = _rms(v, gsgu_ref[...]).astype(BF16)
    tm = u.shape[0]
    rows = []
    for c in range(tm // CHUNK):
        vc = v[c * CHUNK:(c + 1) * CHUNK]
        heads = []
        for h in range(SGU_HEADS):
            sl = slice(h * SGU_HEAD_DIM, (h + 1) * SGU_HEAD_DIM)
            heads.append(_dot(wsp_ref[h], vc[:, sl]) + bsp_ref[h])
        rows.append(jnp.concatenate(heads, axis=1))
    sv = jnp.concatenate(rows, axis=0)
    yb_ref[0] = (u * sv).astype(BF16)

    q = proj[:, SSM_WIDTH + 2 * SGU_WIDTH:].astype(BF16)
    kk = k_ref[0]
    vv = v_ref[0]
    outs = []
    for h in range(XA_HEADS):
        sl = slice(h * XA_HEAD_DIM, (h + 1) * XA_HEAD_DIM)
        s = lax.dot_general(q[:, sl], kk[:, sl], (((1,), (1,)), ((), ())),
                            preferred_element_type=F32) * (XA_HEAD_DIM ** -0.5)
        e = jnp.exp(s - jnp.max(s, axis=-1, keepdims=True))
        l = jnp.sum(e, axis=-1, keepdims=True)
        outs.append(_dot(e.astype(BF16), vv[:, sl]) / l)
    yc_ref[0] = jnp.concatenate(outs, axis=1).astype(BF16)


def _in_proj(x, g_mix, w_in, g_sgu, w_sp, b_sp, k, v):
    b, s, d = x.shape
    m = k.shape[1]
    const2 = lambda i, j: (0, 0)
    const3 = lambda i, j: (0, 0, 0)
    tok = lambda i, j: (i, j, 0)
    per_b = lambda i, j: (i, 0, 0)
    out = jax.ShapeDtypeStruct((b, s, SSM_WIDTH), BF16)
    nc = TM_IN // SSM_CHUNK
    tiles = s // TM_IN
    u2 = jax.ShapeDtypeStruct((SSM_GROUPS, b * s // SSM_CHUNK, SSM_CHUNK * SSM_GROUP), BF16)
    return pl.pallas_call(
        _in_body,
        grid=(b, s // TM_IN),
        in_specs=[pl.BlockSpec((1, TM_IN, d), tok),
                  pl.BlockSpec((1, d), const2),
                  pl.BlockSpec(w_in.shape, const2),
                  pl.BlockSpec((1, SGU_WIDTH), const2),
                  pl.BlockSpec(w_sp.shape, const3),
                  pl.BlockSpec(b_sp.shape, const3),
                  pl.BlockSpec((1, m, SGU_WIDTH), per_b),
                  pl.BlockSpec((1, m, SGU_WIDTH), per_b)],
        out_specs=[pl.BlockSpec((SSM_GROUPS, nc, SSM_CHUNK * SSM_GROUP), lambda i, j: (0, i * tiles + j, 0)),
                   pl.BlockSpec((1, TM_IN, SSM_WIDTH), tok),
                   pl.BlockSpec((1, TM_IN, SSM_WIDTH), tok)],
        out_shape=[u2, out, out],
        scratch_shapes=[pltpu.VMEM((SSM_WIDTH // LANES, TM_IN, LANES), F32)],
        compiler_params=pltpu.CompilerParams(dimension_semantics=("arbitrary", "arbitrary"),
                                             vmem_limit_bytes=VMEM_LIMIT),
        name="in_proj",
    )(x, g_mix, w_in, g_sgu, w_sp, b_sp, k, v)


def _alternate(*stages):
    live = list(stages)
    while live:
        live = [s for s in live if next(s, True) is None]


def _ssm_params(lam_re, lam_im, log_dt, b_re, b_im, c_re, c_im, d_skip):
    g, p = lam_re.shape
    dup = lambda a: jnp.concatenate([a, a], axis=-1)
    lam = jnp.stack([dup(lam_re), dup(lam_im), jnp.broadcast_to(log_dt[:, None], (g, 2 * p))], axis=1)
    brt = b_re.transpose(0, 2, 1)
    bit = b_im.transpose(0, 2, 1)
    cat = lambda a, b: jnp.concatenate([a, b], axis=-1)
    bc = jnp.stack([cat(brt, bit), cat(bit, brt), cat(c_re, -c_im), cat(-c_im, -c_re)], axis=1)
    d2 = jnp.tile(d_skip.reshape(g, 1, SSM_GROUP), (1, 1, SSM_CHUNK))
    return lam, bc, d2


def _ssm_operators(lam_ref, bc_ref, ccat_ref, n_ref, m_ref):
    lam_re = lam_ref[0, 0:1, :]
    lam_im = lam_ref[0, 1:2, :]
    dt = jnp.exp(lam_ref[0, 2:3, :])
    ar = lam_re * dt
    ai = lam_im * dt
    lane = lax.broadcasted_iota(jnp.int32, (1, LANES), 1)
    sgn = jnp.where(lane >= SSM_STATE, 1.0, -1.0)

    def powers(j):
        mag = jnp.exp(ar * j)
        ph = ai * j
        return mag * jnp.cos(ph), mag * jnp.sin(ph)

    pos = lax.broadcasted_iota(jnp.int32, (SSM_CHUNK, 1), 0).astype(F32)
    p_re, p_im = powers(pos)
    r_re, r_im = powers((SSM_CHUNK - 1) - pos)
    one_re, one_im = powers(jnp.ones((1, 1), F32))
    q_re = p_re * one_re - p_im * one_im
    q_im = p_re * one_im + p_im * one_re
    step = lax.shift_left(jnp.full((8, 1), SSM_CHUNK, jnp.int32),
                          lax.broadcasted_iota(jnp.int32, (8, 1), 0)).astype(F32)
    s_re, s_im = powers(step)

    den = lam_re * lam_re + lam_im * lam_im
    f_re = ((one_re - 1.0) * lam_re + one_im * lam_im) / den
    f_im = (one_im * lam_re - (one_re - 1.0) * lam_im) / den
    b1, b2, ca, cb = bc_ref[0, 0], bc_ref[0, 1], bc_ref[0, 2], bc_ref[0, 3]
    bb1 = f_re * b1 + (sgn * f_im) * b2
    bb2 = f_re * b2 - (sgn * f_im) * b1
    r_ims = sgn * r_im
    for s in range(SSM_CHUNK):
        blk = pl.ds(s * SSM_GROUP, SSM_GROUP)
        ccat_ref[blk, :] = ca * p_re[s:s + 1, :] + cb * p_im[s:s + 1, :]
        m_ref[blk, :] = (ca * q_re[s:s + 1, :] + cb * q_im[s:s + 1, :]).astype(m_ref.dtype)
        n_ref[blk, :] = (bb1 * r_re[s:s + 1, :] + bb2 * r_ims[s:s + 1, :]).astype(n_ref.dtype)
    return bb1, s_re, sgn * s_im


def _ssm_body(u_ref, lam_ref, bc_ref, d2_ref, y_ref, toep_ref, ccat_ref, n_ref, m_ref, *, n_seq):
    bcat, lr, li = _ssm_operators(lam_ref, bc_ref, ccat_ref, n_ref, m_ref)
    kern = lax.dot_general(bcat, ccat_ref[...], _NT, precision=lax.Precision.HIGHEST,
                           preferred_element_type=F32)
    col = lax.broadcasted_iota(jnp.int32, kern.shape, 1)
    for s in range(SSM_CHUNK):
        shifted = pltpu.roll(kern, s * SSM_GROUP, 1) if s else kern
        toep_ref[s * SSM_GROUP:(s + 1) * SSM_GROUP, :] = jnp.where(
            col >= s * SSM_GROUP, shifted, 0.0).astype(BF16)

    u = u_ref[0]
    rows = u.shape[0]
    per = rows // n_seq
    y = _dot(u, toep_ref[...])
    st = _dot(u, n_ref[...])
    row = lax.broadcasted_iota(jnp.int32, (per, LANES), 0)
    prev = []
    for b in range(n_seq):
        x = st[b * per:(b + 1) * per]
        k = 0
        while (1 << k) < per:
            d = 1 << k
            sh = jnp.where(row >= d, pltpu.roll(x, d, 0), 0.0)
            x = x + sh * lr[k:k + 1, :] + pltpu.roll(sh, SSM_STATE, 1) * li[k:k + 1, :]
            k += 1
        prev.append(jnp.where(row >= 1, pltpu.roll(x, 1, 0), 0.0))
    xp = jnp.concatenate(prev, axis=0).astype(BF16)
    y = y + lax.dot_general(xp, m_ref[...], _NT, preferred_element_type=F32) + d2_ref[0] * u.astype(F32)
    y_ref[0] = _gelu(y).astype(BF16)


def _ssm(u2, lam, bc, d2, n_seq):
    g, rows, w = u2.shape
    assert rows // n_seq <= 1 << 8, "lam_bar^(16*2^k) is prepared for 8 scan steps"
    blk = lambda a: pl.BlockSpec((1,) + a.shape[1:], lambda i: (i,) + (0,) * (a.ndim - 1))
    return pl.pallas_call(
        functools.partial(_ssm_body, n_seq=n_seq),
        grid=(g,),
        in_specs=[blk(u2), blk(lam), blk(bc), blk(d2)],
        out_specs=blk(u2),
        out_shape=jax.ShapeDtypeStruct(u2.shape, BF16),
        scratch_shapes=[pltpu.VMEM((w, w), BF16),
                        pltpu.VMEM((w, 2 * SSM_STATE), F32),
                        pltpu.VMEM((w, 2 * SSM_STATE), BF16),
                        pltpu.VMEM((w, 2 * SSM_STATE), BF16)],
        compiler_params=pltpu.CompilerParams(dimension_semantics=("arbitrary",),
                                             vmem_limit_bytes=VMEM_LIMIT),
        name="ssm",
    )(u2, lam, bc, d2)


def _mix_tile(x_ref, y2_ref, yb_ref, yc_ref, gmix_ref, wgate_ref, bgate_ref, wglu_ref, bglu_ref,
              wbr_ref, wout_ref, tok_ref, h_ref, hkeep_ref):
    x = x_ref[...]
    n = _rms(x, gmix_ref[...]).astype(BF16)

    def gated(b, c, y):
        cols = pl.ds(b * D_MODEL + c * MERGE_COLS, MERGE_COLS)
        gate = _sigmoid(_dot(n, wgate_ref[:, cols]) + bgate_ref[:, cols])
        return gate * _dot(y, wbr_ref[b, :, pl.ds(c * MERGE_COLS, MERGE_COLS)])

    n_blocks = D_MODEL // MERGE_COLS
    yb = yb_ref[...]
    yc = yc_ref[...]
    head = gated(1, 0, yb) + gated(2, 0, yc)
    yield
    _chunks_to_tokens(y2_ref, tok_ref)
    ys = jnp.concatenate([tok_ref[k] for k in range(SSM_WIDTH // LANES)], axis=1).astype(BF16)
    glu = _dot(ys, wglu_ref[...]) + bglu_ref[...]
    ya = (glu[:, :SSM_WIDTH] * _sigmoid(glu[:, SSM_WIDTH:])).astype(BF16)
    merged = [(head + gated(0, 0, ya)).astype(BF16)]
    for c in range(1, n_blocks):
        yield
        merged.append((gated(0, c, ya) + gated(1, c, yb) + gated(2, c, yc)).astype(BF16))
    yield
    h = x + _dot(jnp.concatenate(merged, axis=1), wout_ref[...])
    h_ref[...] = h
    hkeep_ref[...] = h


def _route_tile(hkeep_ref, gffn_ref, wrt_ref, brt_ref, xnp_ref, rt_ref, rtt_ref, cnt_ref, carry_ref):
    h = hkeep_ref[...]
    xn = _rms(h, gffn_ref[...])
    packed = _pack_bf16_pair(xn)
    for j in range(SC_SPLIT):
        xnp_ref[j] = packed[:, j * SC_ROW:(j + 1) * SC_ROW]
    yield

    x_hi = xn.astype(BF16)
    x_lo = (xn - x_hi.astype(F32)).astype(BF16)
    head = _dot(x_hi, wrt_ref[...])
    logits = (head[:, :LANES] + head[:, LANES:] + _dot(x_lo, wrt_ref[:, :LANES])) + brt_ref[...]
    yield
    tm = logits.shape[0]
    lane_i = lax.broadcasted_iota(jnp.int32, (tm, LANES), 1)
    lane = lane_i.astype(F32)
    neg = jnp.float32(-3.0e38)
    big = jnp.float32(LANES)
    gmask = lane_i < N_GROUPS
    gl = jnp.where(gmask, logits, neg)
    gmax = jnp.max(gl, axis=-1, keepdims=True)
    gidx = jnp.min(jnp.where(gl == gmax, lane, big), axis=-1, keepdims=True)
    gsum = jnp.sum(jnp.where(gmask, jnp.exp(gl - gmax), 0.0), axis=-1, keepdims=True)
    g_w = 1.0 / gsum
    e_lane = lane_i - ROUTE_LANE0
    lane_group = (e_lane >> 3).astype(F32)
    emask = (e_lane >= 0) & (e_lane < N_EXPERTS) & (lane_group == gidx)
    el = jnp.where(emask, logits, neg)
    m1 = jnp.max(el, axis=-1, keepdims=True)
    i1 = jnp.min(jnp.where(el == m1, lane, big), axis=-1, keepdims=True)
    el2 = jnp.where(lane == i1, neg, el)
    m2 = jnp.max(el2, axis=-1, keepdims=True)
    i2 = jnp.min(jnp.where(el2 == m2, lane, big), axis=-1, keepdims=True)
    t = jnp.exp(m2 - m1)
    w1 = g_w / (1.0 + t)
    w2 = g_w * t / (1.0 + t)
    yield

    sel1 = lane == i1
    sel2 = lane == i2
    onehot = jnp.where(sel1 | sel2, 1.0, 0.0)
    r_i = lax.broadcasted_iota(jnp.int32, (tm, tm), 0)
    c_i = lax.broadcasted_iota(jnp.int32, (tm, tm), 1)
    stril = jnp.where(c_i < r_i, 1.0, 0.0).astype(BF16)
    cum = _dot(stril, onehot.astype(BF16)) + carry_ref[0:1, :]
    rank1 = jnp.sum(jnp.where(sel1, cum, 0.0), axis=-1, keepdims=True)
    rank2 = jnp.sum(jnp.where(sel2, cum, 0.0), axis=-1, keepdims=True)
    carry_ref[...] = carry_ref[...] + jnp.sum(onehot, axis=0, keepdims=True)
    cnt_ref[...] = carry_ref[...]
    yield

    cols = (i1 - ROUTE_LANE0, i2 - ROUTE_LANE0, rank1, rank2, w1, w2)
    rt = jnp.zeros((tm, LANES), F32)
    for c, val in enumerate(cols):
        rt = jnp.where(lane_i == c, val, rt)
    rt_ref[...] = rt
    rtt_ref[...] = rt.T[:8]


def _merge_body(x_ref, y2_ref, yb_ref, yc_ref, gmix_ref, wgate_ref, bgate_ref, wglu_ref, bglu_ref,
                wbr_ref, wout_ref, gffn_ref, wrt_ref, brt_ref,
                h_ref, xnp_ref, rt_ref, rtt_ref, cnt_ref, carry_ref, tok_ref, hkeep_ref):
    i = pl.program_id(0)
    last = pl.num_programs(0) - 1
    cur = hkeep_ref.at[i % 2]
    prev = hkeep_ref.at[(i + 1) % 2]

    def mix():
        return _mix_tile(x_ref, y2_ref, yb_ref, yc_ref, gmix_ref, wgate_ref, bgate_ref, wglu_ref,
                         bglu_ref, wbr_ref, wout_ref, tok_ref, h_ref, cur)

    def route():
        return _route_tile(prev, gffn_ref, wrt_ref, brt_ref, xnp_ref, rt_ref, rtt_ref, cnt_ref, carry_ref)

    @pl.when(i == 0)
    def _():
        carry_ref[...] = jnp.zeros_like(carry_ref)
        _alternate(mix())

    @pl.when((i > 0) & (i < last))
    def _():
        _alternate(route(), mix())

    @pl.when(i == last)
    def _():
        _alternate(route())


def _merge_route(x, y2, yb, yc, g_mix, w_gate, b_gate, w_glu, b_glu, w_br, w_out, g_ffn, w_rt, b_rt):
    t, d = x.shape
    tm = TM_MERGE
    tiles = t // tm
    mixed = lambda i: jnp.minimum(i, tiles - 1)
    routed = lambda i: jnp.maximum(i - 1, 0)
    c2 = lambda i: (0, 0)
    c3 = lambda i: (0, 0, 0)
    full = lambda a: pl.BlockSpec(a.shape, c2 if a.ndim == 2 else c3)
    return pl.pallas_call(
        _merge_body,
        grid=(tiles + 1,),
        in_specs=[pl.BlockSpec((tm, d), lambda i: (mixed(i), 0)),
                  pl.BlockSpec((SSM_GROUPS, tm // SSM_CHUNK, SSM_CHUNK * SSM_GROUP),
                               lambda i: (0, mixed(i), 0)),
                  pl.BlockSpec((tm, SSM_WIDTH), lambda i: (mixed(i), 0)),
                  pl.BlockSpec((tm, SSM_WIDTH), lambda i: (mixed(i), 0)),
                  full(g_mix), full(w_gate), full(b_gate), full(w_glu), full(b_glu),
                  full(w_br), full(w_out), full(g_ffn), full(w_rt), full(b_rt)],
        out_specs=[pl.BlockSpec((tm, d), lambda i: (mixed(i), 0)),
                   pl.BlockSpec((SC_SPLIT, tm, SC_ROW), lambda i: (0, routed(i), 0)),
                   pl.BlockSpec((tm, LANES), lambda i: (routed(i), 0)),
                   pl.BlockSpec((8, tm), lambda i: (0, routed(i))),
                   pl.BlockSpec((8, LANES), c2)],
        out_shape=[jax.ShapeDtypeStruct((t, d), F32),
                   jax.ShapeDtypeStruct((SC_SPLIT, t, SC_ROW), jnp.uint32),
                   jax.ShapeDtypeStruct((t, LANES), F32),
                   jax.ShapeDtypeStruct((8, t), F32),
                   jax.ShapeDtypeStruct((8, LANES), F32)],
        scratch_shapes=[pltpu.VMEM((8, LANES), F32),
                        pltpu.VMEM((SSM_WIDTH // LANES, tm, LANES), F32),
                        pltpu.VMEM((2, tm, d), F32)],
        compiler_params=pltpu.CompilerParams(dimension_semantics=("arbitrary",),
                                             vmem_limit_bytes=VMEM_LIMIT),
        name="merge_route",
    )(x, y2, yb, yc, g_mix, w_gate, b_gate, w_glu, b_glu, w_br, w_out, g_ffn, w_rt, b_rt)


def _slot_body(rtt_ref, cnt_ref, out_ref, *, nslots):
    tl = rtt_ref.shape[1]
    padded = jnp.ceil(cnt_ref[...] * (1.0 / BM)) * BM
    k_i = lax.broadcasted_iota(jnp.int32, (LANES, LANES), 0)
    l_i = lax.broadcasted_iota(jnp.int32, (LANES, LANES), 1)
    before = jnp.where(k_i < l_i, 1.0, 0.0)
    pstart = jnp.dot(padded, before, precision=lax.Precision.HIGHEST, preferred_element_type=F32)
    lane_of = lax.broadcasted_iota(jnp.int32, (LANES, tl), 0).astype(F32) - ROUTE_LANE0
    for k in range(TOP_K):
        onehot = jnp.where(rtt_ref[k:k + 1, :] == lane_of, 1.0, 0.0)
        start = jnp.dot(pstart, onehot, precision=lax.Precision.HIGHEST, preferred_element_type=F32)
        slot = (start[0:1, :] + rtt_ref[TOP_K + k:TOP_K + k + 1, :]).astype(jnp.int32)
        for j in range(SC_SPLIT):
            out_ref[k * SC_SPLIT + j:k * SC_SPLIT + j + 1, :] = slot + j * nslots


def _slot_rows(rtt, cnt, nslots):
    t = rtt.shape[1]
    tl = SLOT_LANES
    return pl.pallas_call(
        functools.partial(_slot_body, nslots=nslots),
        grid=(t // tl,),
        in_specs=[pl.BlockSpec((8, tl), lambda i: (0, i)),
                  pl.BlockSpec((8, LANES), lambda i: (0, 0))],
        out_specs=pl.BlockSpec((TOP_K * SC_SPLIT, tl), lambda i: (0, i)),
        out_shape=jax.ShapeDtypeStruct((TOP_K * SC_SPLIT, t), jnp.int32),
        compiler_params=pltpu.CompilerParams(dimension_semantics=("arbitrary",)),
        name="slot_rows",
    )(rtt, cnt)


def _plan_blocks(counts_ref, blk_expert, blk_valid, blk_first, blk_run, run_expert):
    nb = blk_expert.shape[0]

    def per_expert(e, carry):
        cursor, run = carry
        count = counts_ref[e]
        n_blk = (count + (BM - 1)) // BM

        def per_block(b, cur):
            blk_expert[cur] = e
            blk_valid[cur] = jnp.minimum(count - b * BM, BM)
            blk_first[cur] = (b == 0).astype(jnp.int32)
            blk_run[cur] = run
            return cur + 1

        run_expert[run] = e
        return lax.fori_loop(0, n_blk, per_block, cursor), run + (n_blk > 0).astype(jnp.int32)

    cursor, runs = lax.fori_loop(0, counts_ref.shape[0], per_expert, (jnp.int32(0), jnp.int32(0)))
    run_expert[runs] = -1
    run_expert[runs + 1] = -1

    def empty(j, carry):
        blk_expert[j] = 0
        blk_valid[j] = 0
        blk_first[j] = 0
        blk_run[j] = runs - 1
        return carry

    lax.fori_loop(cursor, nb, empty, 0)


def _expert_body(counts_ref, buf_ref, w1_hbm, w3_hbm, w2_hbm, out_ref, w1_buf, w3_buf, w2_buf, sem,
                 blk_expert, blk_valid, blk_first, blk_run, run_expert):
    i = pl.program_id(0)

    @pl.when(i == 0)
    def _():
        _plan_blocks(counts_ref, blk_expert, blk_valid, blk_first, blk_run, run_expert)

    expert = blk_expert[i]
    valid = blk_valid[i]
    run = blk_run[i]
    slot = run % WEIGHT_SLOTS
    ahead1 = run_expert[run + 1]
    ahead2 = run_expert[run + 2]

    def weight_copies(e, s):
        return (pltpu.make_async_copy(w1_hbm.at[e], w1_buf.at[s], sem.at[s, 0]),
                pltpu.make_async_copy(w3_hbm.at[e], w3_buf.at[s], sem.at[s, 1]),
                pltpu.make_async_copy(w2_hbm.at[e], w2_buf.at[s], sem.at[s, 2]))

    @pl.when(i == 0)
    def _():
        for c in weight_copies(expert, slot):
            c.start()

        @pl.when(ahead1 >= 0)
        def _():
            for c in weight_copies(ahead1, (slot + 1) % WEIGHT_SLOTS):
                c.start()

    @pl.when(blk_first[i] == 1)
    def _():
        for c in weight_copies(expert, slot):
            c.wait()

        @pl.when(ahead2 >= 0)
        def _():
            for c in weight_copies(ahead2, (slot + 2) % WEIGHT_SLOTS):
                c.start()

    def mlp(rows):
        x = _unpack_bf16_pair(jnp.concatenate([buf_ref[j, :rows, :] for j in range(SC_SPLIT)], axis=1))
        row = lax.broadcasted_iota(jnp.int32, x.shape, 0)
        x = jnp.where(row < valid, x, 0.0).astype(BF16)
        h1 = _dot(x, w1_buf[slot].astype(BF16))
        h3 = _dot(x, w3_buf[slot].astype(BF16))
        a = (h1 * _sigmoid(h1) * h3).astype(BF16)
        packed = _pack_bf16_pair(_dot(a, w2_buf[slot].astype(BF16)))
        for j in range(SC_SPLIT):
            out_ref[j, :rows, :] = packed[:, j * SC_ROW:(j + 1) * SC_ROW]
            if rows < BM:
                out_ref[j, rows:, :] = jnp.zeros((BM - rows, SC_ROW), out_ref.dtype)

    @pl.when(valid > BM // 2)
    def _():
        mlp(BM)

    @pl.when((valid > 0) & (valid <= BM // 2))
    def _():
        mlp(BM // 2)

    @pl.when(valid <= 0)
    def _():
        out_ref[...] = jnp.zeros_like(out_ref)


def _experts(counts, buf, w1, w3, w2):
    _, nslots, _ = buf.shape
    nb = nslots // BM
    rows = pl.BlockSpec((SC_SPLIT, BM, SC_ROW), lambda i, counts: (0, i, 0))
    hbm = pl.BlockSpec(memory_space=pl.ANY)
    table = pltpu.SMEM((nb,), jnp.int32)
    grid_spec = pltpu.PrefetchScalarGridSpec(
        num_scalar_prefetch=1,
        grid=(nb,),
        in_specs=[rows, hbm, hbm, hbm],
        out_specs=rows,
        scratch_shapes=[pltpu.VMEM((WEIGHT_SLOTS,) + w1.shape[1:], w1.dtype),
                        pltpu.VMEM((WEIGHT_SLOTS,) + w3.shape[1:], w3.dtype),
                        pltpu.VMEM((WEIGHT_SLOTS,) + w2.shape[1:], w2.dtype),
                        pltpu.SemaphoreType.DMA((WEIGHT_SLOTS, 3)),
                        table, table, table, table,
                        pltpu.SMEM((counts.shape[0] + 2,), jnp.int32)],
    )
    return pl.pallas_call(
        _expert_body,
        grid_spec=grid_spec,
        out_shape=jax.ShapeDtypeStruct(buf.shape, jnp.uint32),
        compiler_params=pltpu.CompilerParams(dimension_semantics=("arbitrary",),
                                             vmem_limit_bytes=VMEM_LIMIT),
        name="experts",
    )(counts, buf, w1, w3, w2)


def _sc_mesh():
    return plsc.VectorSubcoreMesh(core_axis_name="core", subcore_axis_name="subcore")


def _dispatch_rows(rows, dest0, dest1, nslots):
    t, w = rows.shape
    win = SC_WINDOW
    idx_spec = pl.BlockSpec((1, win), lambda i: (0, i))

    @functools.partial(pl.kernel, mesh=_sc_mesh(), scratch_types=[],
                       out_type=jax.ShapeDtypeStruct((nslots, w), rows.dtype), name="dispatch_rows")
    def run(rows_hbm, i0_hbm, i1_hbm, out_hbm):
        def body(rows_vmem, i0_vmem, i1_vmem):
            pltpu.sync_copy(rows_vmem, out_hbm.at[i0_vmem.at[0]])
            pltpu.sync_copy(rows_vmem, out_hbm.at[i1_vmem.at[0]])

        pltpu.emit_pipeline(
            body, grid=(t // win,),
            in_specs=[pl.BlockSpec((win, w), lambda i: (i, 0)), idx_spec, idx_spec],
            out_specs=[],
            core_axis_name=("core", "subcore"),
            dimension_semantics=(pltpu.PARALLEL,),
        )(rows_hbm, i0_hbm, i1_hbm)

    return run(rows, dest0.reshape(1, t), dest1.reshape(1, t))


def _gather_rows(table, idx):
    n = idx.shape[0]
    w = table.shape[1]
    win = SC_WINDOW

    @functools.partial(pl.kernel, mesh=_sc_mesh(), scratch_types=[],
                       out_type=jax.ShapeDtypeStruct((n, w), table.dtype), name="gather_rows")
    def run(table_hbm, i_hbm, out_hbm):
        def body(i_vmem, out_vmem):
            pltpu.sync_copy(table_hbm.at[i_vmem.at[0]], out_vmem)

        pltpu.emit_pipeline(
            body, grid=(n // win,),
            in_specs=[pl.BlockSpec((1, win), lambda i: (0, i))],
            out_specs=[pl.BlockSpec((win, w), lambda i: (i, 0))],
            core_axis_name=("core", "subcore"),
            dimension_semantics=(pltpu.PARALLEL,),
        )(i_hbm, out_hbm)

    return run(table, idx.reshape(1, n))


def _combine_body(h_ref, g_ref, rt_ref, gfin_ref, out_ref):
    rt = rt_ref[...]
    y = h_ref[...]
    for k in range(TOP_K):
        rows = jnp.concatenate([g_ref[k * SC_SPLIT + j] for j in range(SC_SPLIT)], axis=1)
        y = y + rt[:, 4 + k:5 + k] * _unpack_bf16_pair(rows)
    out_ref[...] = _rms(y, gfin_ref[...])


def _combine(h, g, rt, g_final):
    t, d = h.shape
    tm = TM_OUT
    tok = lambda i: (i, 0)
    return pl.pallas_call(
        _combine_body,
        grid=(t // tm,),
        in_specs=[pl.BlockSpec((tm, d), tok),
                  pl.BlockSpec((TOP_K * SC_SPLIT, tm, SC_ROW), lambda i: (0, i, 0)),
                  pl.BlockSpec((tm, LANES), tok),
                  pl.BlockSpec((1, d), lambda i: (0, 0))],
        out_specs=pl.BlockSpec((tm, d), tok),
        out_shape=jax.ShapeDtypeStruct((t, d), F32),
        compiler_params=pltpu.CompilerParams(dimension_semantics=("arbitrary",),
                                             vmem_limit_bytes=VMEM_LIMIT),
        name="combine",
    )(h, g, rt, g_final)


def _layer(h, mem, g_mix, g_mem, w_in, w_gate, b_gate, lam_re, lam_im, log_dt, b_re, b_im,
           c_re, c_im, d_skip, w_glu, b_glu, g_sgu, w_spatial, b_spatial, w_kv, w_branch,
           w_out, g_ffn, w_group, b_group, w_router, b_router, w1, w3, w2, g_out):
    bsz, s, d = h.shape
    t = bsz * s
    row = lambda a: a.reshape(1, -1)

    k, v = _kv_proj(mem, row(g_mem), w_kv.astype(BF16))

    tril = jnp.tril(jnp.ones((CHUNK, CHUNK), dtype=bool))
    w_sp = jnp.where(tril, w_spatial, 0.0).astype(BF16)
    b_sp = jnp.broadcast_to(b_spatial[:, :, None], (SGU_HEADS, CHUNK, SGU_HEAD_DIM))
    u2, y_b, y_c = _in_proj(h, row(g_mix), w_in.astype(BF16), row(g_sgu), w_sp, b_sp, k, v)
    y2 = _ssm(u2, *_ssm_params(lam_re, lam_im, log_dt, b_re, b_im, c_re, c_im, d_skip), n_seq=bsz)

    pad = LANES - N_GROUPS - N_EXPERTS
    w_rt = jnp.concatenate([w_group, w_router, jnp.zeros((d, pad), F32)], axis=1)
    w_rt_hi = w_rt.astype(BF16)
    w_rt = jnp.concatenate([w_rt_hi, (w_rt - w_rt_hi.astype(F32)).astype(BF16)], axis=1)
    b_rt =jnp.concatenate([b_group, b_router, jnp.zeros((pad,), F32)]).reshape(1, LANES)
    h2, xnp, rt, rtt, cnt = _merge_route(
        h.reshape(t, d), y2, y_b.reshape(t, -1), y_c.reshape(t, -1), row(g_mix),
        w_gate.astype(BF16), row(b_gate), w_glu.astype(BF16), row(b_glu),
        w_branch.astype(BF16), w_out.astype(BF16), row(g_ffn), w_rt, b_rt)

    assert BM & (BM - 1) == 0, "block padding arithmetic assumes a power-of-two block"
    counts = cnt[0, ROUTE_LANE0:ROUTE_LANE0 + N_EXPERTS].astype(jnp.int32)
    nb = (t * TOP_K) // BM + N_EXPERTS
    nslots = nb * BM
    dest_p = _slot_rows(rtt, cnt, nslots).reshape(TOP_K, SC_SPLIT * t)
    buf = _dispatch_rows(xnp.reshape(SC_SPLIT * t, SC_ROW), dest_p[0], dest_p[1], SC_SPLIT * nslots)
    yb = _experts(counts, buf.reshape(SC_SPLIT, nslots, SC_ROW), w1, w3, w2)
    g = _gather_rows(yb.reshape(SC_SPLIT * nslots, SC_ROW), dest_p.reshape(-1))
    out = _combine(h2, g.reshape(TOP_K * SC_SPLIT, t, SC_ROW), rt, row(g_out))
    return out.reshape(bsz, s, d)


def kernel(x, mem, g_mix, g_mem, w_in, w_gate, b_gate, lam_re, lam_im, log_dt, b_re, b_im, c_re,
           c_im, d_skip, w_glu, b_glu, g_sgu, w_spatial, b_spatial, w_kv, w_branch, w_out, g_ffn,
           w_group, b_group, w_router, b_router, w1, w3, w2, g_final):
    assert g_mix.shape[0] == 1, "single-layer stack"
    return _layer(x, mem, g_mix[0], g_mem[0], w_in[0], w_gate[0], b_gate[0], lam_re[0], lam_im[0],
                  log_dt[0], b_re[0], b_im[0], c_re[0], c_im[0], d_skip[0], w_glu[0], b_glu[0],
                  g_sgu[0], w_spatial[0], b_spatial[0], w_kv[0], w_branch[0], w_out[0], g_ffn[0],
                  w_group[0], b_group[0], w_router[0], b_router[0], w1[0], w3[0], w2[0], g_final)
```

```python
import functools
import math

import jax
import jax.numpy as jnp
from jax import lax
from jax.experimental import pallas as pl
from jax.experimental.pallas import tpu as pltpu
from jax.experimental.pallas import tpu_sc as plsc

F32 = jnp.float32
BF16 = jnp.bfloat16

EPS = 1e-6
D_MODEL = 1024
SSM_WIDTH = 512
SSM_GROUP = 16
SSM_GROUPS = 32
SSM_STATE = 64
SSM_CHUNK = 16
SGU_WIDTH = 512
SGU_HEADS = 4
SGU_HEAD_DIM = 128
CHUNK = 128
XA_HEADS = 4
XA_HEAD_DIM = 128
N_GROUPS = 8
EXPERTS_PER_GROUP = 8
N_EXPERTS = 64
TOP_K = 2
D_FF = 512
LANES = 128
ROUTE_LANE0 = N_GROUPS

TM_IN = 1024
TM_MERGE = 512
MERGE_COLS = 256
TM_OUT = 1024
BM = 1024
BM_STEP = 256
WEIGHT_SLOTS = 3
SC_WINDOW = 128
SC_ROW = 256
SC_SPLIT = (D_MODEL // 2) // SC_ROW
SLOT_LANES = 2048
VMEM_LIMIT = 56 * 1024 * 1024


def _rms(x, g):
    return x * lax.rsqrt(jnp.mean(x * x, axis=-1, keepdims=True) + EPS) * g


def _sigmoid(x):
    return 0.5 * (1.0 + jnp.tanh(0.5 * x))


def _gelu(x):
    c = math.sqrt(2.0 / math.pi)
    return 0.5 * x * (1.0 + jnp.tanh(c * (x + 0.044715 * (x * x * x))))


def _dot(a, b):
    return jnp.dot(a, b, preferred_element_type=F32)


_NT = (((1,), (1,)), ((), ()))


def _pack_bf16_pair(x):
    n = x.shape[1] // 2
    lo = lax.bitcast_convert_type(x[:, :n].astype(BF16).astype(F32), jnp.uint32)
    hi = lax.bitcast_convert_type(x[:, n:].astype(BF16).astype(F32), jnp.uint32)
    return hi | (lo >> 16)


def _unpack_bf16_pair(p):
    lo = lax.bitcast_convert_type(p << 16, F32)
    hi = lax.bitcast_convert_type(p & jnp.uint32(0xFFFF0000), F32)
    return jnp.concatenate([lo, hi], axis=1)


GROUPS_PER_TILE = LANES // SSM_GROUP
POS_PER_TILE = LANES // SSM_GROUP


def _slot_masks(rows):
    lane = lax.broadcasted_iota(jnp.int32, (rows, LANES), 1)
    return [(lane >= i * SSM_GROUP) & (lane < (i + 1) * SSM_GROUP) for i in range(LANES // SSM_GROUP)]


def _tokens_to_chunks(tok_ref, out_ref):
    tm = tok_ref.shape[1]
    nc = tm // SSM_CHUNK
    masks = _slot_masks(nc)
    for k in range(SSM_WIDTH // LANES):
        for j in range(SSM_CHUNK // POS_PER_TILE):
            src = [tok_ref[k, pl.ds(j * POS_PER_TILE + p, nc, stride=SSM_CHUNK), :]
                   for p in range(POS_PER_TILE)]
            for gi in range(GROUPS_PER_TILE):
                acc = None
                for p in range(POS_PER_TILE):
                    shift = ((p - gi) * SSM_GROUP) % LANES
                    r = pltpu.roll(src[p], shift, 1) if shift else src[p]
                    acc = r if acc is None else jnp.where(masks[p], r, acc)
                out_ref[k * GROUPS_PER_TILE + gi, :, pl.ds(j * LANES, LANES)] = acc.astype(out_ref.dtype)


def _chunks_to_tokens(chunk_ref, tok_ref):
    tm = tok_ref.shape[1]
    nc = tm // SSM_CHUNK
    masks = _slot_masks(nc)
    for k in range(SSM_WIDTH // LANES):
        for j in range(SSM_CHUNK // POS_PER_TILE):
            src = [chunk_ref[k * GROUPS_PER_TILE + gi, :, pl.ds(j * LANES, LANES)].astype(F32)
                   for gi in range(GROUPS_PER_TILE)]
            for p in range(POS_PER_TILE):
                acc = None
                for gi in range(GROUPS_PER_TILE):
                    shift = ((gi - p) * SSM_GROUP) % LANES
                    r = pltpu.roll(src[gi], shift, 1) if shift else src[gi]
                    acc = r if acc is None else jnp.where(masks[gi], r, acc)
                tok_ref[k, pl.ds(j * POS_PER_TILE + p, nc, stride=SSM_CHUNK), :] = acc


def _kv_body(mem_ref, g_ref, w_ref, k_ref, v_ref):
    n = _rms(mem_ref[0], g_ref[...]).astype(BF16)
    kv = _dot(n, w_ref[...])
    k_ref[0] = kv[:, :SGU_WIDTH].astype(BF16)
    v_ref[0] = kv[:, SGU_WIDTH:].astype(BF16)


def _kv_proj(mem, g_mem, w_kv):
    b, m, d = mem.shape
    return pl.pallas_call(
        _kv_body,
        grid=(b,),
        in_specs=[pl.BlockSpec((1, m, d), lambda i: (i, 0, 0)),
                  pl.BlockSpec((1, d), lambda i: (0, 0)),
                  pl.BlockSpec((d, 2 * SGU_WIDTH), lambda i: (0, 0))],
        out_specs=[pl.BlockSpec((1, m, SGU_WIDTH), lambda i: (i, 0, 0)),
                   pl.BlockSpec((1, m, SGU_WIDTH), lambda i: (i, 0, 0))],
        out_shape=[jax.ShapeDtypeStruct((b, m, SGU_WIDTH), BF16),
                   jax.ShapeDtypeStruct((b, m, SGU_WIDTH), BF16)],
        compiler_params=pltpu.CompilerParams(dimension_semantics=("arbitrary",),
                                             vmem_limit_bytes=VMEM_LIMIT),
        name="kv_proj",
    )(mem, g_mem, w_kv)


def _in_body(x_ref, gmix_ref, win_ref, gsgu_ref, wsp_ref, bsp_ref, k_ref, v_ref,
             u2_ref, yb_ref, yc_ref, tok_ref):
    n = _rms(x_ref[0], gmix_ref[...]).astype(BF16)
    proj = _dot(n, win_ref[...])
    for k in range(SSM_WIDTH // LANES):
        tok_ref[k] = proj[:, k * LANES:(k + 1) * LANES]
    _tokens_to_chunks(tok_ref, u2_ref)

    u = _gelu(proj[:, SSM_WIDTH:SSM_WIDTH + SGU_WIDTH])
    v = _gelu(proj[:, SSM_WIDTH + SGU_WIDTH:SSM_WIDTH + 2 * SGU_WIDTH])
    v = _rms(v, gsgu_ref[...]).astype(BF16)
    tm = u.shape[0]
    rows = []
    for c in range(tm // CHUNK):
        vc = v[c * CHUNK:(c + 1) * CHUNK]
        heads = []
        for h in range(SGU_HEADS):
            sl = slice(h * SGU_HEAD_DIM, (h + 1) * SGU_HEAD_DIM)
            heads.append(_dot(wsp_ref[h], vc[:, sl]) + bsp_ref[h])
        rows.append(jnp.concatenate(heads, axis=1))
    sv = jnp.concatenate(rows, axis=0)
    yb_ref[0] = (u * sv).astype(BF16)

    q = proj[:, SSM_WIDTH + 2 * SGU_WIDTH:].astype(BF16)
    kk = k_ref[0]
    vv = v_ref[0]
    outs = []
    for h in range(XA_HEADS):
        sl = slice(h * XA_HEAD_DIM, (h + 1) * XA_HEAD_DIM)
        s = lax.dot_general(q[:, sl], kk[:, sl], (((1,), (1,)), ((), ())),
                            preferred_element_type=F32) * (XA_HEAD_DIM ** -0.5)
        e = jnp.exp(s - jnp.max(s, axis=-1, keepdims=True))
        l = jnp.sum(e, axis=-1, keepdims=True)
        outs.append(_dot(e.astype(BF16), vv[:, sl]) / l)
    yc_ref[0] = jnp.concatenate(outs, axis=1).astype(BF16)


def _in_proj(x, g_mix, w_in, g_sgu, w_sp, b_sp, k, v):
    b, s, d = x.shape
    m = k.shape[1]
    const2 = lambda i, j: (0, 0)
    const3 = lambda i, j: (0, 0, 0)
    tok = lambda i, j: (i, j, 0)
    per_b = lambda i, j: (i, 0, 0)
    out = jax.ShapeDtypeStruct((b, s, SSM_WIDTH), BF16)
    nc = TM_IN // SSM_CHUNK
    tiles = s // TM_IN
    u2 = jax.ShapeDtypeStruct((SSM_GROUPS, b * s // SSM_CHUNK, SSM_CHUNK * SSM_GROUP), BF16)
    return pl.pallas_call(
        _in_body,
        grid=(b, s // TM_IN),
        in_specs=[pl.BlockSpec((1, TM_IN, d), tok),
                  pl.BlockSpec((1, d), const2),
                  pl.BlockSpec(w_in.shape, const2),
                  pl.BlockSpec((1, SGU_WIDTH), const2),
                  pl.BlockSpec(w_sp.shape, const3),
                  pl.BlockSpec(b_sp.shape, const3),
                  pl.BlockSpec((1, m, SGU_WIDTH), per_b),
                  pl.BlockSpec((1, m, SGU_WIDTH), per_b)],
        out_specs=[pl.BlockSpec((SSM_GROUPS, nc, SSM_CHUNK * SSM_GROUP), lambda i, j: (0, i * tiles + j, 0)),
                   pl.BlockSpec((1, TM_IN, SSM_WIDTH), tok),
                   pl.BlockSpec((1, TM_IN, SSM_WIDTH), tok)],
        out_shape=[u2, out, out],
        scratch_shapes=[pltpu.VMEM((SSM_WIDTH // LANES, TM_IN, LANES), F32)],
        compiler_params=pltpu.CompilerParams(dimension_semantics=("arbitrary", "arbitrary"),
                                             vmem_limit_bytes=VMEM_LIMIT),
        name="in_proj",
    )(x, g_mix, w_in, g_sgu, w_sp, b_sp, k, v)


def _alternate(*stages):
    live = list(stages)
    while live:
        live = [s for s in live if next(s, True) is None]


def _ssm_params(lam_re, lam_im, log_dt, b_re, b_im, c_re, c_im, d_skip):
    g, p = lam_re.shape
    dup = lambda a: jnp.concatenate([a, a], axis=-1)
    lam = jnp.stack([dup(lam_re), dup(lam_im), jnp.broadcast_to(log_dt[:, None], (g, 2 * p))], axis=1)
    brt = b_re.transpose(0, 2, 1)
    bit = b_im.transpose(0, 2, 1)
    cat = lambda a, b: jnp.concatenate([a, b], axis=-1)
    bc = jnp.stack([cat(brt, bit), cat(bit, brt), cat(c_re, -c_im), cat(-c_im, -c_re)], axis=1)
    d2 = jnp.tile(d_skip.reshape(g, 1, SSM_GROUP), (1, 1, SSM_CHUNK))
    return lam, bc, d2


def _ssm_operators(lam_ref, bc_ref, ccat_ref, n_ref, m_ref):
    lam_re = lam_ref[0, 0:1, :]
    lam_im = lam_ref[0, 1:2, :]
    dt = jnp.exp(lam_ref[0, 2:3, :])
    ar = lam_re * dt
    ai = lam_im * dt
    lane = lax.broadcasted_iota(jnp.int32, (1, LANES), 1)
    sgn = jnp.where(lane >= SSM_STATE, 1.0, -1.0)

    def powers(j):
        mag = jnp.exp(ar * j)
        ph = ai * j
        return mag * jnp.cos(ph), mag * jnp.sin(ph)

    pos = lax.broadcasted_iota(jnp.int32, (SSM_CHUNK, 1), 0).astype(F32)
    p_re, p_im = powers(pos)
    r_re, r_im = powers((SSM_CHUNK - 1) - pos)
    one_re, one_im = powers(jnp.ones((1, 1), F32))
    q_re = p_re * one_re - p_im * one_im
    q_im = p_re * one_im + p_im * one_re
    step = lax.shift_left(jnp.full((8, 1), SSM_CHUNK, jnp.int32),
                          lax.broadcasted_iota(jnp.int32, (8, 1), 0)).astype(F32)
    s_re, s_im = powers(step)

    den = lam_re * lam_re + lam_im * lam_im
    f_re = ((one_re - 1.0) * lam_re + one_im * lam_im) / den
    f_im = (one_im * lam_re - (one_re - 1.0) * lam_im) / den
    b1, b2, ca, cb = bc_ref[0, 0], bc_ref[0, 1], bc_ref[0, 2], bc_ref[0, 3]
    bb1 = f_re * b1 + (sgn * f_im) * b2
    bb2 = f_re * b2 - (sgn * f_im) * b1
    r_ims = sgn * r_im
    for s in range(SSM_CHUNK):
        blk = pl.ds(s * SSM_GROUP, SSM_GROUP)
        ccat_ref[blk, :] = ca * p_re[s:s + 1, :] + cb * p_im[s:s + 1, :]
        m_ref[blk, :] = (ca * q_re[s:s + 1, :] + cb * q_im[s:s + 1, :]).astype(m_ref.dtype)
        n_ref[blk, :] = (bb1 * r_re[s:s + 1, :] + bb2 * r_ims[s:s + 1, :]).astype(n_ref.dtype)
    return bb1, s_re, sgn * s_im


def _ssm_body(u_ref, lam_ref, bc_ref, d2_ref, y_ref, toep_ref, ccat_ref, n_ref, m_ref, *, n_seq):
    bcat, lr, li = _ssm_operators(lam_ref, bc_ref, ccat_ref, n_ref, m_ref)
    kern = lax.dot_general(bcat, ccat_ref[...], _NT, precision=lax.Precision.HIGHEST,
                           preferred_element_type=F32)
    col = lax.broadcasted_iota(jnp.int32, kern.shape, 1)
    for s in range(SSM_CHUNK):
        shifted = pltpu.roll(kern, s * SSM_GROUP, 1) if s else kern
        toep_ref[s * SSM_GROUP:(s + 1) * SSM_GROUP, :] = jnp.where(
            col >= s * SSM_GROUP, shifted, 0.0).astype(BF16)

    u = u_ref[0]
    rows = u.shape[0]
    per = rows // n_seq
    y = _dot(u, toep_ref[...])
    st = _dot(u, n_ref[...])
    row = lax.broadcasted_iota(jnp.int32, (per, LANES), 0)
    prev = []
    for b in range(n_seq):
        x = st[b * per:(b + 1) * per]
        k = 0
        while (1 << k) < per:
            d = 1 << k
            sh = jnp.where(row >= d, pltpu.roll(x, d, 0), 0.0)
            x = x + sh * lr[k:k + 1, :] + pltpu.roll(sh, SSM_STATE, 1) * li[k:k + 1, :]
            k += 1
        prev.append(jnp.where(row >= 1, pltpu.roll(x, 1, 0), 0.0))
    xp = jnp.concatenate(prev, axis=0).astype(BF16)
    y = y + lax.dot_general(xp, m_ref[...], _NT, preferred_element_type=F32) + d2_ref[0] * u.astype(F32)
    y_ref[0] = _gelu(y).astype(BF16)


def _ssm(u2, lam, bc, d2, n_seq):
    g, rows, w = u2.shape
    assert rows // n_seq <= 1 << 8, "lam_bar^(16*2^k) is prepared for 8 scan steps"
    blk = lambda a: pl.BlockSpec((1,) + a.shape[1:], lambda i: (i,) + (0,) * (a.ndim - 1))
    return pl.pallas_call(
        functools.partial(_ssm_body, n_seq=n_seq),
        grid=(g,),
        in_specs=[blk(u2), blk(lam), blk(bc), blk(d2)],
        out_specs=blk(u2),
        out_shape=jax.ShapeDtypeStruct(u2.shape, BF16),
        scratch_shapes=[pltpu.VMEM((w, w), BF16),
                        pltpu.VMEM((w, 2 * SSM_STATE), F32),
                        pltpu.VMEM((w, 2 * SSM_STATE), BF16),
                        pltpu.VMEM((w, 2 * SSM_STATE), BF16)],
        compiler_params=pltpu.CompilerParams(dimension_semantics=("arbitrary",),
                                             vmem_limit_bytes=VMEM_LIMIT),
        name="ssm",
    )(u2, lam, bc, d2)


def _mix_tile(x_ref, y2_ref, yb_ref, yc_ref, gmix_ref, wgate_ref, bgate_ref, wglu_ref, bglu_ref,
              wbr_ref, wout_ref, tok_ref, h_ref, hkeep_ref):
    x = x_ref[...]
    n = _rms(x, gmix_ref[...]).astype(BF16)

    def gated(b, c, y):
        cols = pl.ds(b * D_MODEL + c * MERGE_COLS, MERGE_COLS)
        gate = _sigmoid(_dot(n, wgate_ref[:, cols]) + bgate_ref[:, cols])
        return gate * _dot(y, wbr_ref[b, :, pl.ds(c * MERGE_COLS, MERGE_COLS)])

    n_blocks = D_MODEL // MERGE_COLS
    yb = yb_ref[...]
    yc = yc_ref[...]
    head = gated(1, 0, yb) + gated(2, 0, yc)
    yield
    _chunks_to_tokens(y2_ref, tok_ref)
    ys = jnp.concatenate([tok_ref[k] for k in range(SSM_WIDTH // LANES)], axis=1).astype(BF16)
    glu = _dot(ys, wglu_ref[...]) + bglu_ref[...]
    ya = (glu[:, :SSM_WIDTH] * _sigmoid(glu[:, SSM_WIDTH:])).astype(BF16)
    merged = [(head + gated(0, 0, ya)).astype(BF16)]
    for c in range(1, n_blocks):
        yield
        merged.append((gated(0, c, ya) + gated(1, c, yb) + gated(2, c, yc)).astype(BF16))
    yield
    h = x + _dot(jnp.concatenate(merged, axis=1), wout_ref[...])
    h_ref[...] = h
    hkeep_ref[...] = h


def _route_tile(hkeep_ref, gffn_ref, wrt_ref, brt_ref, xnp_ref, rt_ref, rtt_ref, cnt_ref, carry_ref):
    h = hkeep_ref[...]
    xn = _rms(h, gffn_ref[...])
    packed = _pack_bf16_pair(xn)
    for j in range(SC_SPLIT):
        xnp_ref[j] = packed[:, j * SC_ROW:(j + 1) * SC_ROW]
    yield

    x_hi = xn.astype(BF16)
    x_lo = (xn - x_hi.astype(F32)).astype(BF16)
    head = _dot(x_hi, wrt_ref[...])
    logits = (head[:, :LANES] + head[:, LANES:] + _dot(x_lo, wrt_ref[:, :LANES])) + brt_ref[...]
    yield
    tm = logits.shape[0]
    lane_i = lax.broadcasted_iota(jnp.int32, (tm, LANES), 1)
    lane = lane_i.astype(F32)
    neg = jnp.float32(-3.0e38)
    big = jnp.float32(LANES)
    gmask = lane_i < N_GROUPS
    gl = jnp.where(gmask, logits, neg)
    gmax = jnp.max(gl, axis=-1, keepdims=True)
    gidx = jnp.min(jnp.where(gl == gmax, lane, big), axis=-1, keepdims=True)
    gsum = jnp.sum(jnp.where(gmask, jnp.exp(gl - gmax), 0.0), axis=-1, keepdims=True)
    g_w = 1.0 / gsum
    e_lane = lane_i - ROUTE_LANE0
    lane_group = (e_lane >> 3).astype(F32)
    emask = (e_lane >= 0) & (e_lane < N_EXPERTS) & (lane_group == gidx)
    el = jnp.where(emask, logits, neg)
    m1 = jnp.max(el, axis=-1, keepdims=True)
    i1 = jnp.min(jnp.where(el == m1, lane, big), axis=-1, keepdims=True)
    el2 = jnp.where(lane == i1, neg, el)
    m2 = jnp.max(el2, axis=-1, keepdims=True)
    i2 = jnp.min(jnp.where(el2 == m2, lane, big), axis=-1, keepdims=True)
    t = jnp.exp(m2 - m1)
    w1 = g_w / (1.0 + t)
    w2 = g_w * t / (1.0 + t)
    yield

    sel1 = lane == i1
    sel2 = lane == i2
    onehot = jnp.where(sel1 | sel2, 1.0, 0.0)
    r_i = lax.broadcasted_iota(jnp.int32, (tm, tm), 0)
    c_i = lax.broadcasted_iota(jnp.int32, (tm, tm), 1)
    stril = jnp.where(c_i < r_i, 1.0, 0.0).astype(BF16)
    cum = _dot(stril, onehot.astype(BF16)) + carry_ref[0:1, :]
    rank1 = jnp.sum(jnp.where(sel1, cum, 0.0), axis=-1, keepdims=True)
    rank2 = jnp.sum(jnp.where(sel2, cum, 0.0), axis=-1, keepdims=True)
    carry_ref[...] = carry_ref[...] + jnp.sum(onehot, axis=0, keepdims=True)
    cnt_ref[...] = carry_ref[...]
    yield

    cols = (i1 - ROUTE_LANE0, i2 - ROUTE_LANE0, rank1, rank2, w1, w2)
    rt = jnp.zeros((tm, LANES), F32)
    for c, val in enumerate(cols):
        rt = jnp.where(lane_i == c, val, rt)
    rt_ref[...] = rt
    rtt_ref[...] = rt.T[:8]


def _merge_body(x_ref, y2_ref, yb_ref, yc_ref, gmix_ref, wgate_ref, bgate_ref, wglu_ref, bglu_ref,
                wbr_ref, wout_ref, gffn_ref, wrt_ref, brt_ref,
                h_ref, xnp_ref, rt_ref, rtt_ref, cnt_ref, carry_ref, tok_ref, hkeep_ref):
    i = pl.program_id(0)
    last = pl.num_programs(0) - 1
    cur = hkeep_ref.at[i % 2]
    prev = hkeep_ref.at[(i + 1) % 2]

    def mix():
        return _mix_tile(x_ref, y2_ref, yb_ref, yc_ref, gmix_ref, wgate_ref, bgate_ref, wglu_ref,
                         bglu_ref, wbr_ref, wout_ref, tok_ref, h_ref, cur)

    def route():
        return _route_tile(prev, gffn_ref, wrt_ref, brt_ref, xnp_ref, rt_ref, rtt_ref, cnt_ref, carry_ref)

    @pl.when(i == 0)
    def _():
        carry_ref[...] = jnp.zeros_like(carry_ref)
        _alternate(mix())

    @pl.when((i > 0) & (i < last))
    def _():
        _alternate(route(), mix())

    @pl.when(i == last)
    def _():
        _alternate(route())


def _merge_route(x, y2, yb, yc, g_mix, w_gate, b_gate, w_glu, b_glu, w_br, w_out, g_ffn, w_rt, b_rt):
    t, d = x.shape
    tm = TM_MERGE
    tiles = t // tm
    mixed = lambda i: jnp.minimum(i, tiles - 1)
    routed = lambda i: jnp.maximum(i - 1, 0)
    c2 = lambda i: (0, 0)
    c3 = lambda i: (0, 0, 0)
    full = lambda a: pl.BlockSpec(a.shape, c2 if a.ndim == 2 else c3)
    return pl.pallas_call(
        _merge_body,
        grid=(tiles + 1,),
        in_specs=[pl.BlockSpec((tm, d), lambda i: (mixed(i), 0)),
                  pl.BlockSpec((SSM_GROUPS, tm // SSM_CHUNK, SSM_CHUNK * SSM_GROUP),
                               lambda i: (0, mixed(i), 0)),
                  pl.BlockSpec((tm, SSM_WIDTH), lambda i: (mixed(i), 0)),
                  pl.BlockSpec((tm, SSM_WIDTH), lambda i: (mixed(i), 0)),
                  full(g_mix), full(w_gate), full(b_gate), full(w_glu), full(b_glu),
                  full(w_br), full(w_out), full(g_ffn), full(w_rt), full(b_rt)],
        out_specs=[pl.BlockSpec((tm, d), lambda i: (mixed(i), 0)),
                   pl.BlockSpec((SC_SPLIT, tm, SC_ROW), lambda i: (0, routed(i), 0)),
                   pl.BlockSpec((tm, LANES), lambda i: (routed(i), 0)),
                   pl.BlockSpec((8, tm), lambda i: (0, routed(i))),
                   pl.BlockSpec((8, LANES), c2)],
        out_shape=[jax.ShapeDtypeStruct((t, d), F32),
                   jax.ShapeDtypeStruct((SC_SPLIT, t, SC_ROW), jnp.uint32),
                   jax.ShapeDtypeStruct((t, LANES), F32),
                   jax.ShapeDtypeStruct((8, t), F32),
                   jax.ShapeDtypeStruct((8, LANES), F32)],
        scratch_shapes=[pltpu.VMEM((8, LANES), F32),
                        pltpu.VMEM((SSM_WIDTH // LANES, tm, LANES), F32),
                        pltpu.VMEM((2, tm, d), F32)],
        compiler_params=pltpu.CompilerParams(dimension_semantics=("arbitrary",),
                                             vmem_limit_bytes=VMEM_LIMIT),
        name="merge_route",
    )(x, y2, yb, yc, g_mix, w_gate, b_gate, w_glu, b_glu, w_br, w_out, g_ffn, w_rt, b_rt)


def _slot_body(rtt_ref, cnt_ref, out_ref, *, nslots):
    tl = rtt_ref.shape[1]
    padded = jnp.ceil(cnt_ref[...] * (1.0 / BM)) * BM
    k_i = lax.broadcasted_iota(jnp.int32, (LANES, LANES), 0)
    l_i = lax.broadcasted_iota(jnp.int32, (LANES, LANES), 1)
    before = jnp.where(k_i < l_i, 1.0, 0.0)
    pstart = jnp.dot(padded, before, precision=lax.Precision.HIGHEST, preferred_element_type=F32)
    lane_of = lax.broadcasted_iota(jnp.int32, (LANES, tl), 0).astype(F32) - ROUTE_LANE0
    for k in range(TOP_K):
        onehot = jnp.where(rtt_ref[k:k + 1, :] == lane_of, 1.0, 0.0)
        start = jnp.dot(pstart, onehot, precision=lax.Precision.HIGHEST, preferred_element_type=F32)
        slot = (start[0:1, :] + rtt_ref[TOP_K + k:TOP_K + k + 1, :]).astype(jnp.int32)
        for j in range(SC_SPLIT):
            out_ref[k * SC_SPLIT + j:k * SC_SPLIT + j + 1, :] = slot + j * nslots


def _slot_rows(rtt, cnt, nslots):
    t = rtt.shape[1]
    tl = SLOT_LANES
    return pl.pallas_call(
        functools.partial(_slot_body, nslots=nslots),
        grid=(t // tl,),
        in_specs=[pl.BlockSpec((8, tl), lambda i: (0, i)),
                  pl.BlockSpec((8, LANES), lambda i: (0, 0))],
        out_specs=pl.BlockSpec((TOP_K * SC_SPLIT, tl), lambda i: (0, i)),
        out_shape=jax.ShapeDtypeStruct((TOP_K * SC_SPLIT, t), jnp.int32),
        compiler_params=pltpu.CompilerParams(dimension_semantics=("arbitrary",)),
        name="slot_rows",
    )(rtt, cnt)


def _plan_blocks(counts_ref, blk_expert, blk_valid, blk_first, blk_run, run_expert):
    nb = blk_expert.shape[0]

    def per_expert(e, carry):
        cursor, run = carry
        count = counts_ref[e]
        n_blk = (count + (BM - 1)) // BM

        def per_block(b, cur):
            blk_expert[cur] = e
            blk_valid[cur] = jnp.minimum(count - b * BM, BM)
            blk_first[cur] = (b == 0).astype(jnp.int32)
            blk_run[cur] = run
            return cur + 1

        run_expert[run] = e
        return lax.fori_loop(0, n_blk, per_block, cursor), run + (n_blk > 0).astype(jnp.int32)

    cursor, runs = lax.fori_loop(0, counts_ref.shape[0], per_expert, (jnp.int32(0), jnp.int32(0)))
    run_expert[runs] = -1
    run_expert[runs + 1] = -1

    def empty(j, carry):
        blk_expert[j] = 0
        blk_valid[j] = 0
        blk_first[j] = 0
        blk_run[j] = runs - 1
        return carry

    lax.fori_loop(cursor, nb, empty, 0)


def _expert_body(counts_ref, buf_ref, w1_hbm, w3_hbm, w2_hbm, out_ref, w1_buf, w3_buf, w2_buf, sem,
                 blk_expert, blk_valid, blk_first, blk_run, run_expert):
    i = pl.program_id(0)

    @pl.when(i == 0)
    def _():
        _plan_blocks(counts_ref, blk_expert, blk_valid, blk_first, blk_run, run_expert)

    expert = blk_expert[i]
    valid = blk_valid[i]
    run = blk_run[i]
    slot = run % WEIGHT_SLOTS
    ahead1 = run_expert[run + 1]
    ahead2 = run_expert[run + 2]

    def weight_copies(e, s):
        return (pltpu.make_async_copy(w1_hbm.at[e], w1_buf.at[s], sem.at[s, 0]),
                pltpu.make_async_copy(w3_hbm.at[e], w3_buf.at[s], sem.at[s, 1]),
                pltpu.make_async_copy(w2_hbm.at[e], w2_buf.at[s], sem.at[s, 2]))

    @pl.when(i == 0)
    def _():
        for c in weight_copies(expert, slot):
            c.start()

        @pl.when(ahead1 >= 0)
        def _():
            for c in weight_copies(ahead1, (slot + 1) % WEIGHT_SLOTS):
                c.start()

    @pl.when(blk_first[i] == 1)
    def _():
        for c in weight_copies(expert, slot):
            c.wait()

        @pl.when(ahead2 >= 0)
        def _():
            for c in weight_copies(ahead2, (slot + 2) % WEIGHT_SLOTS):
                c.start()

    def mlp(rows):
        x = _unpack_bf16_pair(jnp.concatenate([buf_ref[j, :rows, :] for j in range(SC_SPLIT)], axis=1))
        row = lax.broadcasted_iota(jnp.int32, x.shape, 0)
        x = jnp.where(row < valid, x, 0.0).astype(BF16)
        h1 = _dot(x, w1_buf[slot].astype(BF16))
        h3 = _dot(x, w3_buf[slot].astype(BF16))
        a = (h1 * _sigmoid(h1) * h3).astype(BF16)
        packed = _pack_bf16_pair(_dot(a, w2_buf[slot].astype(BF16)))
        for j in range(SC_SPLIT):
            out_ref[j, :rows, :] = packed[:, j * SC_ROW:(j + 1) * SC_ROW]
            if rows < BM:
                out_ref[j, rows:, :] = jnp.zeros((BM - rows, SC_ROW), out_ref.dtype)

    for rows in range(BM_STEP, BM + 1, BM_STEP):
        @pl.when((valid > rows - BM_STEP) & (valid <= rows))
        def _():
            mlp(rows)

    @pl.when(valid <= 0)
    def _():
        out_ref[...] = jnp.zeros_like(out_ref)


def _experts(counts, buf, w1, w3, w2):
    _, nslots, _ = buf.shape
    nb = nslots // BM
    rows = pl.BlockSpec((SC_SPLIT, BM, SC_ROW), lambda i, counts: (0, i, 0))
    hbm = pl.BlockSpec(memory_space=pl.ANY)
    table = pltpu.SMEM((nb,), jnp.int32)
    grid_spec = pltpu.PrefetchScalarGridSpec(
        num_scalar_prefetch=1,
        grid=(nb,),
        in_specs=[rows, hbm, hbm, hbm],
        out_specs=rows,
        scratch_shapes=[pltpu.VMEM((WEIGHT_SLOTS,) + w1.shape[1:], w1.dtype),
                        pltpu.VMEM((WEIGHT_SLOTS,) + w3.shape[1:], w3.dtype),
                        pltpu.VMEM((WEIGHT_SLOTS,) + w2.shape[1:], w2.dtype),
                        pltpu.SemaphoreType.DMA((WEIGHT_SLOTS, 3)),
                        table, table, table, table,
                        pltpu.SMEM((counts.shape[0] + 2,), jnp.int32)],
    )
    return pl.pallas_call(
        _expert_body,
        grid_spec=grid_spec,
        out_shape=jax.ShapeDtypeStruct(buf.shape, jnp.uint32),
        compiler_params=pltpu.CompilerParams(dimension_semantics=("arbitrary",),
                                             vmem_limit_bytes=VMEM_LIMIT),
        name="experts",
    )(counts, buf, w1, w3, w2)


def _sc_mesh():
    return plsc.VectorSubcoreMesh(core_axis_name="core", subcore_axis_name="subcore")


def _dispatch_rows(rows, dest0, dest1, nslots):
    t, w = rows.shape
    win = SC_WINDOW
    idx_spec = pl.BlockSpec((1, win), lambda i: (0, i))

    @functools.partial(pl.kernel, mesh=_sc_mesh(), scratch_types=[],
                       out_type=jax.ShapeDtypeStruct((nslots, w), rows.dtype), name="dispatch_rows")
    def run(rows_hbm, i0_hbm, i1_hbm, out_hbm):
        def body(rows_vmem, i0_vmem, i1_vmem):
            pltpu.sync_copy(rows_vmem, out_hbm.at[i0_vmem.at[0]])
            pltpu.sync_copy(rows_vmem, out_hbm.at[i1_vmem.at[0]])

        pltpu.emit_pipeline(
            body, grid=(t // win,),
            in_specs=[pl.BlockSpec((win, w), lambda i: (i, 0)), idx_spec, idx_spec],
            out_specs=[],
            core_axis_name=("core", "subcore"),
            dimension_semantics=(pltpu.PARALLEL,),
        )(rows_hbm, i0_hbm, i1_hbm)

    return run(rows, dest0.reshape(1, t), dest1.reshape(1, t))


def _gather_rows(table, idx):
    n = idx.shape[0]
    w = table.shape[1]
    win = SC_WINDOW

    @functools.partial(pl.kernel, mesh=_sc_mesh(), scratch_types=[],
                       out_type=jax.ShapeDtypeStruct((n, w), table.dtype), name="gather_rows")
    def run(table_hbm, i_hbm, out_hbm):
        def body(i_vmem, out_vmem):
            pltpu.sync_copy(table_hbm.at[i_vmem.at[0]], out_vmem)

        pltpu.emit_pipeline(
            body, grid=(n // win,),
            in_specs=[pl.BlockSpec((1, win), lambda i: (0, i))],
            out_specs=[pl.BlockSpec((win, w), lambda i: (i, 0))],
            core_axis_name=("core", "subcore"),
            dimension_semantics=(pltpu.PARALLEL,),
        )(i_hbm, out_hbm)

    return run(table, idx.reshape(1, n))


def _combine_body(h_ref, g_ref, rt_ref, gfin_ref, out_ref):
    rt = rt_ref[...]
    y = h_ref[...]
    for k in range(TOP_K):
        rows = jnp.concatenate([g_ref[k * SC_SPLIT + j] for j in range(SC_SPLIT)], axis=1)
        y = y + rt[:, 4 + k:5 + k] * _unpack_bf16_pair(rows)
    out_ref[...] = _rms(y, gfin_ref[...])


def _combine(h, g, rt, g_final):
    t, d = h.shape
    tm = TM_OUT
    tok = lambda i: (i, 0)
    return pl.pallas_call(
        _combine_body,
        grid=(t // tm,),
        in_specs=[pl.BlockSpec((tm, d), tok),
                  pl.BlockSpec((TOP_K * SC_SPLIT, tm, SC_ROW), lambda i: (0, i, 0)),
                  pl.BlockSpec((tm, LANES), tok),
                  pl.BlockSpec((1, d), lambda i: (0, 0))],
        out_specs=pl.BlockSpec((tm, d), tok),
        out_shape=jax.ShapeDtypeStruct((t, d), F32),
        compiler_params=pltpu.CompilerParams(dimension_semantics=("arbitrary",),
                                             vmem_limit_bytes=VMEM_LIMIT),
        name="combine",
    )(h, g, rt, g_final)


def _layer(h, mem, g_mix, g_mem, w_in, w_gate, b_gate, lam_re, lam_im, log_dt, b_re, b_im,
           c_re, c_im, d_skip, w_glu, b_glu, g_sgu, w_spatial, b_spatial, w_kv, w_branch,
           w_out, g_ffn, w_group, b_group, w_router, b_router, w1, w3, w2, g_out):
    bsz, s, d = h.shape
    t = bsz * s
    row = lambda a: a.reshape(1, -1)

    k, v = _kv_proj(mem, row(g_mem), w_kv.astype(BF16))

    tril = jnp.tril(jnp.ones((CHUNK, CHUNK), dtype=bool))
    w_sp = jnp.where(tril, w_spatial, 0.0).astype(BF16)
    b_sp = jnp.broadcast_to(b_spatial[:, :, None], (SGU_HEADS, CHUNK, SGU_HEAD_DIM))
    u2, y_b, y_c = _in_proj(h, row(g_mix), w_in.astype(BF16), row(g_sgu), w_sp, b_sp, k, v)
    y2 = _ssm(u2, *_ssm_params(lam_re, lam_im, log_dt, b_re, b_im, c_re, c_im, d_skip), n_seq=bsz)

    pad = LANES - N_GROUPS - N_EXPERTS
    w_rt = jnp.concatenate([w_group, w_router, jnp.zeros((d, pad), F32)], axis=1)
    w_rt_hi = w_rt.astype(BF16)
    w_rt = jnp.concatenate([w_rt_hi, (w_rt - w_rt_hi.astype(F32)).astype(BF16)], axis=1)
    b_rt =jnp.concatenate([b_group, b_router, jnp.zeros((pad,), F32)]).reshape(1, LANES)
    h2, xnp, rt, rtt, cnt = _merge_route(
        h.reshape(t, d), y2, y_b.reshape(t, -1), y_c.reshape(t, -1), row(g_mix),
        w_gate.astype(BF16), row(b_gate), w_glu.astype(BF16), row(b_glu),
        w_branch.astype(BF16), w_out.astype(BF16), row(g_ffn), w_rt, b_rt)

    assert BM & (BM - 1) == 0, "block padding arithmetic assumes a power-of-two block"
    counts = cnt[0, ROUTE_LANE0:ROUTE_LANE0 + N_EXPERTS].astype(jnp.int32)
    nb = (t * TOP_K) // BM + N_EXPERTS
    nslots = nb * BM
    dest_p = _slot_rows(rtt, cnt, nslots).reshape(TOP_K, SC_SPLIT * t)
    buf = _dispatch_rows(xnp.reshape(SC_SPLIT * t, SC_ROW), dest_p[0], dest_p[1], SC_SPLIT * nslots)
    yb = _experts(counts, buf.reshape(SC_SPLIT, nslots, SC_ROW), w1, w3, w2)
    g = _gather_rows(yb.reshape(SC_SPLIT * nslots, SC_ROW), dest_p.reshape(-1))
    out = _combine(h2, g.reshape(TOP_K * SC_SPLIT, t, SC_ROW), rt, row(g_out))
    return out.reshape(bsz, s, d)


def kernel(x, mem, g_mix, g_mem, w_in, w_gate, b_gate, lam_re, lam_im, log_dt, b_re, b_im, c_re,
           c_im, d_skip, w_glu, b_glu, g_sgu, w_spatial, b_spatial, w_kv, w_branch, w_out, g_ffn,
           w_group, b_group, w_router, b_router, w1, w3, w2, g_final):
    assert g_mix.shape[0] == 1, "single-layer stack"
    return _layer(x, mem, g_mix[0], g_mem[0], w_in[0], w_gate[0], b_gate[0], lam_re[0], lam_im[0],
                  log_dt[0], b_re[0], b_im[0], c_re[0], c_im[0], d_skip[0], w_glu[0], b_glu[0],
                  g_sgu[0], w_spatial[0], b_spatial[0], w_kv[0], w_branch[0], w_out[0], g_ffn[0],
                  w_group[0], b_group[0], w_router[0], b_router[0], w1[0], w3[0], w2[0], g_final)
```

```python
import functools
import math

import jax
import jax.numpy as jnp
from jax import lax
from jax.experimental import pallas as pl
from jax.experimental.pallas import tpu as pltpu
from jax.experimental.pallas import tpu_sc as plsc

F32 = jnp.float32
BF16 = jnp.bfloat16

EPS = 1e-6
D_MODEL = 1024
SSM_WIDTH = 512
SSM_GROUP = 16
SSM_GROUPS = 32
SSM_STATE = 64
SSM_CHUNK = 16
SGU_WIDTH = 512
SGU_HEADS = 4
SGU_HEAD_DIM = 128
CHUNK = 128
XA_HEADS = 4
XA_HEAD_DIM = 128
N_GROUPS = 8
EXPERTS_PER_GROUP = 8
N_EXPERTS = 64
TOP_K = 2
D_FF = 512
LANES = 128
ROUTE_LANE0 = N_GROUPS

TM_IN = 1024
TM_MERGE = 512
MERGE_COLS = 256
TM_OUT = 1024
BM = 512
BM_STEP = 128
WEIGHT_SLOTS = 3
SC_WINDOW = 128
SC_ROW = 256
SC_SPLIT = (D_MODEL // 2) // SC_ROW
SLOT_LANES = 2048
VMEM_LIMIT = 56 * 1024 * 1024


def _rms(x, g):
    return x * lax.rsqrt(jnp.mean(x * x, axis=-1, keepdims=True) + EPS) * g


def _sigmoid(x):
    return 0.5 * (1.0 + jnp.tanh(0.5 * x))


def _gelu(x):
    c = math.sqrt(2.0 / math.pi)
    return 0.5 * x * (1.0 + jnp.tanh(c * (x + 0.044715 * (x * x * x))))


def _dot(a, b):
    return jnp.dot(a, b, preferred_element_type=F32)


_NT = (((1,), (1,)), ((), ()))


def _pack_bf16_pair(x):
    n = x.shape[1] // 2
    lo = lax.bitcast_convert_type(x[:, :n].astype(BF16).astype(F32), jnp.uint32)
    hi = lax.bitcast_convert_type(x[:, n:].astype(BF16).astype(F32), jnp.uint32)
    return hi | (lo >> 16)


def _unpack_bf16_pair(p):
    lo = lax.bitcast_convert_type(p << 16, F32)
    hi = lax.bitcast_convert_type(p & jnp.uint32(0xFFFF0000), F32)
    return jnp.concatenate([lo, hi], axis=1)


GROUPS_PER_TILE = LANES // SSM_GROUP
POS_PER_TILE = LANES // SSM_GROUP


def _slot_masks(rows):
    lane = lax.broadcasted_iota(jnp.int32, (rows, LANES), 1)
    return [(lane >= i * SSM_GROUP) & (lane < (i + 1) * SSM_GROUP) for i in range(LANES // SSM_GROUP)]


def _tokens_to_chunks(tok_ref, out_ref):
    tm = tok_ref.shape[1]
    nc = tm // SSM_CHUNK
    masks = _slot_masks(nc)
    for k in range(SSM_WIDTH // LANES):
        for j in range(SSM_CHUNK // POS_PER_TILE):
            src = [tok_ref[k, pl.ds(j * POS_PER_TILE + p, nc, stride=SSM_CHUNK), :]
                   for p in range(POS_PER_TILE)]
            for gi in range(GROUPS_PER_TILE):
                acc = None
                for p in range(POS_PER_TILE):
                    shift = ((p - gi) * SSM_GROUP) % LANES
                    r = pltpu.roll(src[p], shift, 1) if shift else src[p]
                    acc = r if acc is None else jnp.where(masks[p], r, acc)
                out_ref[k * GROUPS_PER_TILE + gi, :, pl.ds(j * LANES, LANES)] = acc.astype(out_ref.dtype)


def _chunks_to_tokens(chunk_ref, tok_ref):
    tm = tok_ref.shape[1]
    nc = tm // SSM_CHUNK
    masks = _slot_masks(nc)
    for k in range(SSM_WIDTH // LANES):
        for j in range(SSM_CHUNK // POS_PER_TILE):
            src = [chunk_ref[k * GROUPS_PER_TILE + gi, :, pl.ds(j * LANES, LANES)].astype(F32)
                   for gi in range(GROUPS_PER_TILE)]
            for p in range(POS_PER_TILE):
                acc = None
                for gi in range(GROUPS_PER_TILE):
                    shift = ((gi - p) * SSM_GROUP) % LANES
                    r = pltpu.roll(src[gi], shift, 1) if shift else src[gi]
                    acc = r if acc is None else jnp.where(masks[gi], r, acc)
                tok_ref[k, pl.ds(j * POS_PER_TILE + p, nc, stride=SSM_CHUNK), :] = acc


def _kv_body(mem_ref, g_ref, w_ref, k_ref, v_ref):
    n = _rms(mem_ref[0], g_ref[...]).astype(BF16)
    kv = _dot(n, w_ref[...])
    k_ref[0] = kv[:, :SGU_WIDTH].astype(BF16)
    v_ref[0] = kv[:, SGU_WIDTH:].astype(BF16)


def _kv_proj(mem, g_mem, w_kv):
    b, m, d = mem.shape
    return pl.pallas_call(
        _kv_body,
        grid=(b,),
        in_specs=[pl.BlockSpec((1, m, d), lambda i: (i, 0, 0)),
                  pl.BlockSpec((1, d), lambda i: (0, 0)),
                  pl.BlockSpec((d, 2 * SGU_WIDTH), lambda i: (0, 0))],
        out_specs=[pl.BlockSpec((1, m, SGU_WIDTH), lambda i: (i, 0, 0)),
                   pl.BlockSpec((1, m, SGU_WIDTH), lambda i: (i, 0, 0))],
        out_shape=[jax.ShapeDtypeStruct((b, m, SGU_WIDTH), BF16),
                   jax.ShapeDtypeStruct((b, m, SGU_WIDTH), BF16)],
        compiler_params=pltpu.CompilerParams(dimension_semantics=("arbitrary",),
                                             vmem_limit_bytes=VMEM_LIMIT),
        name="kv_proj",
    )(mem, g_mem, w_kv)


def _in_body(x_ref, gmix_ref, win_ref, gsgu_ref, wsp_ref, bsp_ref, k_ref, v_ref,
             u2_ref, yb_ref, yc_ref, tok_ref):
    n = _rms(x_ref[0], gmix_ref[...]).astype(BF16)
    proj = _dot(n, win_ref[...])
    for k in range(SSM_WIDTH // LANES):
        tok_ref[k] = proj[:, k * LANES:(k + 1) * LANES]
    _tokens_to_chunks(tok_ref, u2_ref)

    u = _gelu(proj[:, SSM_WIDTH:SSM_WIDTH + SGU_WIDTH])
    v = _gelu(proj[:, SSM_WIDTH + SGU_WIDTH:SSM_WIDTH + 2 * SGU_WIDTH])
    v = _rms(v, gsgu_ref[...]).astype(BF16)
    tm = u.shape[0]
    rows = []
    for c in range(tm // CHUNK):
        vc = v[c * CHUNK:(c + 1) * CHUNK]
        heads = []
        for h in range(SGU_HEADS):
            sl = slice(h * SGU_HEAD_DIM, (h + 1) * SGU_HEAD_DIM)
            heads.append(_dot(wsp_ref[h], vc[:, sl]) + bsp_ref[h])
        rows.append(jnp.concatenate(heads, axis=1))
    sv = jnp.concatenate(rows, axis=0)
    yb_ref[0] = (u * sv).astype(BF16)

    q = proj[:, SSM_WIDTH + 2 * SGU_WIDTH:].astype(BF16)
    kk = k_ref[0]
    vv = v_ref[0]
    outs = []
    for h in range(XA_HEADS):
        sl = slice(h * XA_HEAD_DIM, (h + 1) * XA_HEAD_DIM)
        s = lax.dot_general(q[:, sl], kk[:, sl], (((1,), (1,)), ((), ())),
                            preferred_element_type=F32) * (XA_HEAD_DIM ** -0.5)
        e = jnp.exp(s - jnp.max(s, axis=-1, keepdims=True))
        l = jnp.sum(e, axis=-1, keepdims=True)
        outs.append(_dot(e.astype(BF16), vv[:, sl]) / l)
    yc_ref[0] = jnp.concatenate(outs, axis=1).astype(BF16)


def _in_proj(x, g_mix, w_in, g_sgu, w_sp, b_sp, k, v):
    b, s, d = x.shape
    m = k.shape[1]
    const2 = lambda i, j: (0, 0)
    const3 = lambda i, j: (0, 0, 0)
    tok = lambda i, j: (i, j, 0)
    per_b = lambda i, j: (i, 0, 0)
    out = jax.ShapeDtypeStruct((b, s, SSM_WIDTH), BF16)
    nc = TM_IN // SSM_CHUNK
    tiles = s // TM_IN
    u2 = jax.ShapeDtypeStruct((SSM_GROUPS, b * s // SSM_CHUNK, SSM_CHUNK * SSM_GROUP), BF16)
    return pl.pallas_call(
        _in_body,
        grid=(b, s // TM_IN),
        in_specs=[pl.BlockSpec((1, TM_IN, d), tok),
                  pl.BlockSpec((1, d), const2),
                  pl.BlockSpec(w_in.shape, const2),
                  pl.BlockSpec((1, SGU_WIDTH), const2),
                  pl.BlockSpec(w_sp.shape, const3),
                  pl.BlockSpec(b_sp.shape, const3),
                  pl.BlockSpec((1, m, SGU_WIDTH), per_b),
                  pl.BlockSpec((1, m, SGU_WIDTH), per_b)],
        out_specs=[pl.BlockSpec((SSM_GROUPS, nc, SSM_CHUNK * SSM_GROUP), lambda i, j: (0, i * tiles + j, 0)),
                   pl.BlockSpec((1, TM_IN, SSM_WIDTH), tok),
                   pl.BlockSpec((1, TM_IN, SSM_WIDTH), tok)],
        out_shape=[u2, out, out],
        scratch_shapes=[pltpu.VMEM((SSM_WIDTH // LANES, TM_IN, LANES), F32)],
        compiler_params=pltpu.CompilerParams(dimension_semantics=("arbitrary", "arbitrary"),
                                             vmem_limit_bytes=VMEM_LIMIT),
        name="in_proj",
    )(x, g_mix, w_in, g_sgu, w_sp, b_sp, k, v)


def _alternate(*stages):
    live = list(stages)
    while live:
        live = [s for s in live if next(s, True) is None]


def _ssm_params(lam_re, lam_im, log_dt, b_re, b_im, c_re, c_im, d_skip):
    g, p = lam_re.shape
    dup = lambda a: jnp.concatenate([a, a], axis=-1)
    lam = jnp.stack([dup(lam_re), dup(lam_im), jnp.broadcast_to(log_dt[:, None], (g, 2 * p))], axis=1)
    brt = b_re.transpose(0, 2, 1)
    bit = b_im.transpose(0, 2, 1)
    cat = lambda a, b: jnp.concatenate([a, b], axis=-1)
    bc = jnp.stack([cat(brt, bit), cat(bit, brt), cat(c_re, -c_im), cat(-c_im, -c_re)], axis=1)
    d2 = jnp.tile(d_skip.reshape(g, 1, SSM_GROUP), (1, 1, SSM_CHUNK))
    return lam, bc, d2


def _ssm_operators(lam_ref, bc_ref, ccat_ref, n_ref, m_ref):
    lam_re = lam_ref[0, 0:1, :]
    lam_im = lam_ref[0, 1:2, :]
    dt = jnp.exp(lam_ref[0, 2:3, :])
    ar = lam_re * dt
    ai = lam_im * dt
    lane = lax.broadcasted_iota(jnp.int32, (1, LANES), 1)
    sgn = jnp.where(lane >= SSM_STATE, 1.0, -1.0)

    def powers(j):
        mag = jnp.exp(ar * j)
        ph = ai * j
        return mag * jnp.cos(ph), mag * jnp.sin(ph)

    pos = lax.broadcasted_iota(jnp.int32, (SSM_CHUNK, 1), 0).astype(F32)
    p_re, p_im = powers(pos)
    r_re, r_im = powers((SSM_CHUNK - 1) - pos)
    one_re, one_im = powers(jnp.ones((1, 1), F32))
    q_re = p_re * one_re - p_im * one_im
    q_im = p_re * one_im + p_im * one_re
    step = lax.shift_left(jnp.full((8, 1), SSM_CHUNK, jnp.int32),
                          lax.broadcasted_iota(jnp.int32, (8, 1), 0)).astype(F32)
    s_re, s_im = powers(step)

    den = lam_re * lam_re + lam_im * lam_im
    f_re = ((one_re - 1.0) * lam_re + one_im * lam_im) / den
    f_im = (one_im * lam_re - (one_re - 1.0) * lam_im) / den
    b1, b2, ca, cb = bc_ref[0, 0], bc_ref[0, 1], bc_ref[0, 2], bc_ref[0, 3]
    bb1 = f_re * b1 + (sgn * f_im) * b2
    bb2 = f_re * b2 - (sgn * f_im) * b1
    r_ims = sgn * r_im
    for s in range(SSM_CHUNK):
        blk = pl.ds(s * SSM_GROUP, SSM_GROUP)
        ccat_ref[blk, :] = ca * p_re[s:s + 1, :] + cb * p_im[s:s + 1, :]
        m_ref[blk, :] = (ca * q_re[s:s + 1, :] + cb * q_im[s:s + 1, :]).astype(m_ref.dtype)
        n_ref[blk, :] = (bb1 * r_re[s:s + 1, :] + bb2 * r_ims[s:s + 1, :]).astype(n_ref.dtype)
    return bb1, s_re, sgn * s_im


def _ssm_body(u_ref, lam_ref, bc_ref, d2_ref, y_ref, toep_ref, ccat_ref, n_ref, m_ref, *, n_seq):
    bcat, lr, li = _ssm_operators(lam_ref, bc_ref, ccat_ref, n_ref, m_ref)
    kern = lax.dot_general(bcat, ccat_ref[...], _NT, precision=lax.Precision.HIGHEST,
                           preferred_element_type=F32)
    col = lax.broadcasted_iota(jnp.int32, kern.shape, 1)
    for s in range(SSM_CHUNK):
        shifted = pltpu.roll(kern, s * SSM_GROUP, 1) if s else kern
        toep_ref[s * SSM_GROUP:(s + 1) * SSM_GROUP, :] = jnp.where(
            col >= s * SSM_GROUP, shifted, 0.0).astype(BF16)

    u = u_ref[0]
    rows = u.shape[0]
    per = rows // n_seq
    y = _dot(u, toep_ref[...])
    st = _dot(u, n_ref[...])
    row = lax.broadcasted_iota(jnp.int32, (per, LANES), 0)
    prev = []
    for b in range(n_seq):
        x = st[b * per:(b + 1) * per]
        k = 0
        while (1 << k) < per:
            d = 1 << k
            sh = jnp.where(row >= d, pltpu.roll(x, d, 0), 0.0)
            x = x + sh * lr[k:k + 1, :] + pltpu.roll(sh, SSM_STATE, 1) * li[k:k + 1, :]
            k += 1
        prev.append(jnp.where(row >= 1, pltpu.roll(x, 1, 0), 0.0))
    xp = jnp.concatenate(prev, axis=0).astype(BF16)
    y = y + lax.dot_general(xp, m_ref[...], _NT, preferred_element_type=F32) + d2_ref[0] * u.astype(F32)
    y_ref[0] = _gelu(y).astype(BF16)


def _ssm(u2, lam, bc, d2, n_seq):
    g, rows, w = u2.shape
    assert rows // n_seq <= 1 << 8, "lam_bar^(16*2^k) is prepared for 8 scan steps"
    blk = lambda a: pl.BlockSpec((1,) + a.shape[1:], lambda i: (i,) + (0,) * (a.ndim - 1))
    return pl.pallas_call(
        functools.partial(_ssm_body, n_seq=n_seq),
        grid=(g,),
        in_specs=[blk(u2), blk(lam), blk(bc), blk(d2)],
        out_specs=blk(u2),
        out_shape=jax.ShapeDtypeStruct(u2.shape, BF16),
        scratch_shapes=[pltpu.VMEM((w, w), BF16),
                        pltpu.VMEM((w, 2 * SSM_STATE), F32),
                        pltpu.VMEM((w, 2 * SSM_STATE), BF16),
                        pltpu.VMEM((w, 2 * SSM_STATE), BF16)],
        compiler_params=pltpu.CompilerParams(dimension_semantics=("arbitrary",),
                                             vmem_limit_bytes=VMEM_LIMIT),
        name="ssm",
    )(u2, lam, bc, d2)


def _mix_tile(x_ref, y2_ref, yb_ref, yc_ref, gmix_ref, wgate_ref, bgate_ref, wglu_ref, bglu_ref,
              wbr_ref, wout_ref, tok_ref, h_ref, hkeep_ref):
    x = x_ref[...]
    n = _rms(x, gmix_ref[...]).astype(BF16)

    def gated(b, c, y):
        cols = pl.ds(b * D_MODEL + c * MERGE_COLS, MERGE_COLS)
        gate = _sigmoid(_dot(n, wgate_ref[:, cols]) + bgate_ref[:, cols])
        return gate * _dot(y, wbr_ref[b, :, pl.ds(c * MERGE_COLS, MERGE_COLS)])

    n_blocks = D_MODEL // MERGE_COLS
    yb = yb_ref[...]
    yc = yc_ref[...]
    head = gated(1, 0, yb) + gated(2, 0, yc)
    yield
    _chunks_to_tokens(y2_ref, tok_ref)
    ys = jnp.concatenate([tok_ref[k] for k in range(SSM_WIDTH // LANES)], axis=1).astype(BF16)
    glu = _dot(ys, wglu_ref[...]) + bglu_ref[...]
    ya = (glu[:, :SSM_WIDTH] * _sigmoid(glu[:, SSM_WIDTH:])).astype(BF16)
    merged = [(head + gated(0, 0, ya)).astype(BF16)]
    for c in range(1, n_blocks):
        yield
        merged.append((gated(0, c, ya) + gated(1, c, yb) + gated(2, c, yc)).astype(BF16))
    yield
    h = x + _dot(jnp.concatenate(merged, axis=1), wout_ref[...])
    h_ref[...] = h
    hkeep_ref[...] = h


def _route_tile(hkeep_ref, gffn_ref, wrt_ref, brt_ref, xnp_ref, rt_ref, rtt_ref, cnt_ref, carry_ref):
    h = hkeep_ref[...]
    xn = _rms(h, gffn_ref[...])
    packed = _pack_bf16_pair(xn)
    for j in range(SC_SPLIT):
        xnp_ref[j] = packed[:, j * SC_ROW:(j + 1) * SC_ROW]
    yield

    x_hi = xn.astype(BF16)
    x_lo = (xn - x_hi.astype(F32)).astype(BF16)
    head = _dot(x_hi, wrt_ref[...])
    logits = (head[:, :LANES] + head[:, LANES:] + _dot(x_lo, wrt_ref[:, :LANES])) + brt_ref[...]
    yield
    tm = logits.shape[0]
    lane_i = lax.broadcasted_iota(jnp.int32, (tm, LANES), 1)
    lane = lane_i.astype(F32)
    neg = jnp.float32(-3.0e38)
    big = jnp.float32(LANES)
    gmask = lane_i < N_GROUPS
    gl = jnp.where(gmask, logits, neg)
    gmax = jnp.max(gl, axis=-1, keepdims=True)
    gidx = jnp.min(jnp.where(gl == gmax, lane, big), axis=-1, keepdims=True)
    gsum = jnp.sum(jnp.where(gmask, jnp.exp(gl - gmax), 0.0), axis=-1, keepdims=True)
    g_w = 1.0 / gsum
    e_lane = lane_i - ROUTE_LANE0
    lane_group = (e_lane >> 3).astype(F32)
    emask = (e_lane >= 0) & (e_lane < N_EXPERTS) & (lane_group == gidx)
    el = jnp.where(emask, logits, neg)
    m1 = jnp.max(el, axis=-1, keepdims=True)
    i1 = jnp.min(jnp.where(el == m1, lane, big), axis=-1, keepdims=True)
    el2 = jnp.where(lane == i1, neg, el)
    m2 = jnp.max(el2, axis=-1, keepdims=True)
    i2 = jnp.min(jnp.where(el2 == m2, lane, big), axis=-1, keepdims=True)
    t = jnp.exp(m2 - m1)
    w1 = g_w / (1.0 + t)
    w2 = g_w * t / (1.0 + t)
    yield

    sel1 = lane == i1
    sel2 = lane == i2
    onehot = jnp.where(sel1 | sel2, 1.0, 0.0)
    r_i = lax.broadcasted_iota(jnp.int32, (tm, tm), 0)
    c_i = lax.broadcasted_iota(jnp.int32, (tm, tm), 1)
    stril = jnp.where(c_i < r_i, 1.0, 0.0).astype(BF16)
    cum = _dot(stril, onehot.astype(BF16)) + carry_ref[0:1, :]
    rank1 = jnp.sum(jnp.where(sel1, cum, 0.0), axis=-1, keepdims=True)
    rank2 = jnp.sum(jnp.where(sel2, cum, 0.0), axis=-1, keepdims=True)
    carry_ref[...] = carry_ref[...] + jnp.sum(onehot, axis=0, keepdims=True)
    cnt_ref[...] = carry_ref[...]
    yield

    cols = (i1 - ROUTE_LANE0, i2 - ROUTE_LANE0, rank1, rank2, w1, w2)
    rt = jnp.zeros((tm, LANES), F32)
    for c, val in enumerate(cols):
        rt = jnp.where(lane_i == c, val, rt)
    rt_ref[...] = rt
    rtt_ref[...] = rt.T[:8]


def _merge_body(x_ref, y2_ref, yb_ref, yc_ref, gmix_ref, wgate_ref, bgate_ref, wglu_ref, bglu_ref,
                wbr_ref, wout_ref, gffn_ref, wrt_ref, brt_ref,
                h_ref, xnp_ref, rt_ref, rtt_ref, cnt_ref, carry_ref, tok_ref, hkeep_ref):
    i = pl.program_id(0)
    last = pl.num_programs(0) - 1
    cur = hkeep_ref.at[i % 2]
    prev = hkeep_ref.at[(i + 1) % 2]

    def mix():
        return _mix_tile(x_ref, y2_ref, yb_ref, yc_ref, gmix_ref, wgate_ref, bgate_ref, wglu_ref,
                         bglu_ref, wbr_ref, wout_ref, tok_ref, h_ref, cur)

    def route():
        return _route_tile(prev, gffn_ref, wrt_ref, brt_ref, xnp_ref, rt_ref, rtt_ref, cnt_ref, carry_ref)

    @pl.when(i == 0)
    def _():
        carry_ref[...] = jnp.zeros_like(carry_ref)
        _alternate(mix())

    @pl.when((i > 0) & (i < last))
    def _():
        _alternate(route(), mix())

    @pl.when(i == last)
    def _():
        _alternate(route())


def _merge_route(x, y2, yb, yc, g_mix, w_gate, b_gate, w_glu, b_glu, w_br, w_out, g_ffn, w_rt, b_rt):
    t, d = x.shape
    tm = TM_MERGE
    tiles = t // tm
    mixed = lambda i: jnp.minimum(i, tiles - 1)
    routed = lambda i: jnp.maximum(i - 1, 0)
    c2 = lambda i: (0, 0)
    c3 = lambda i: (0, 0, 0)
    full = lambda a: pl.BlockSpec(a.shape, c2 if a.ndim == 2 else c3)
    return pl.pallas_call(
        _merge_body,
        grid=(tiles + 1,),
        in_specs=[pl.BlockSpec((tm, d), lambda i: (mixed(i), 0)),
                  pl.BlockSpec((SSM_GROUPS, tm // SSM_CHUNK, SSM_CHUNK * SSM_GROUP),
                               lambda i: (0, mixed(i), 0)),
                  pl.BlockSpec((tm, SSM_WIDTH), lambda i: (mixed(i), 0)),
                  pl.BlockSpec((tm, SSM_WIDTH), lambda i: (mixed(i), 0)),
                  full(g_mix), full(w_gate), full(b_gate), full(w_glu), full(b_glu),
                  full(w_br), full(w_out), full(g_ffn), full(w_rt), full(b_rt)],
        out_specs=[pl.BlockSpec((tm, d), lambda i: (mixed(i), 0)),
                   pl.BlockSpec((SC_SPLIT, tm, SC_ROW), lambda i: (0, routed(i), 0)),
                   pl.BlockSpec((tm, LANES), lambda i: (routed(i), 0)),
                   pl.BlockSpec((8, tm), lambda i: (0, routed(i))),
                   pl.BlockSpec((8, LANES), c2)],
        out_shape=[jax.ShapeDtypeStruct((t, d), F32),
                   jax.ShapeDtypeStruct((SC_SPLIT, t, SC_ROW), jnp.uint32),
                   jax.ShapeDtypeStruct((t, LANES), F32),
                   jax.ShapeDtypeStruct((8, t), F32),
                   jax.ShapeDtypeStruct((8, LANES), F32)],
        scratch_shapes=[pltpu.VMEM((8, LANES), F32),
                        pltpu.VMEM((SSM_WIDTH // LANES, tm, LANES), F32),
                        pltpu.VMEM((2, tm, d), F32)],
        compiler_params=pltpu.CompilerParams(dimension_semantics=("arbitrary",),
                                             vmem_limit_bytes=VMEM_LIMIT),
        name="merge_route",
    )(x, y2, yb, yc, g_mix, w_gate, b_gate, w_glu, b_glu, w_br, w_out, g_ffn, w_rt, b_rt)


def _slot_body(rtt_ref, cnt_ref, out_ref, *, nslots):
    tl = rtt_ref.shape[1]
    padded = jnp.ceil(cnt_ref[...] * (1.0 / BM)) * BM
    k_i = lax.broadcasted_iota(jnp.int32, (LANES, LANES), 0)
    l_i = lax.broadcasted_iota(jnp.int32, (LANES, LANES), 1)
    before = jnp.where(k_i < l_i, 1.0, 0.0)
    pstart = jnp.dot(padded, before, precision=lax.Precision.HIGHEST, preferred_element_type=F32)
    lane_of = lax.broadcasted_iota(jnp.int32, (LANES, tl), 0).astype(F32) - ROUTE_LANE0
    for k in range(TOP_K):
        onehot = jnp.where(rtt_ref[k:k + 1, :] == lane_of, 1.0, 0.0)
        start = jnp.dot(pstart, onehot, precision=lax.Precision.HIGHEST, preferred_element_type=F32)
        slot = (start[0:1, :] + rtt_ref[TOP_K + k:TOP_K + k + 1, :]).astype(jnp.int32)
        for j in range(SC_SPLIT):
            out_ref[k * SC_SPLIT + j:k * SC_SPLIT + j + 1, :] = slot + j * nslots


def _slot_rows(rtt, cnt, nslots):
    t = rtt.shape[1]
    tl = SLOT_LANES
    return pl.pallas_call(
        functools.partial(_slot_body, nslots=nslots),
        grid=(t // tl,),
        in_specs=[pl.BlockSpec((8, tl), lambda i: (0, i)),
                  pl.BlockSpec((8, LANES), lambda i: (0, 0))],
        out_specs=pl.BlockSpec((TOP_K * SC_SPLIT, tl), lambda i: (0, i)),
        out_shape=jax.ShapeDtypeStruct((TOP_K * SC_SPLIT, t), jnp.int32),
        compiler_params=pltpu.CompilerParams(dimension_semantics=("arbitrary",)),
        name="slot_rows",
    )(rtt, cnt)


def _plan_blocks(counts_ref, blk_expert, blk_valid, blk_first, blk_run, run_expert):
    nb = blk_expert.shape[0]

    def per_expert(e, carry):
        cursor, run = carry
        count = counts_ref[e]
        n_blk = (count + (BM - 1)) // BM

        def per_block(b, cur):
            blk_expert[cur] = e
            blk_valid[cur] = jnp.minimum(count - b * BM, BM)
            blk_first[cur] = (b == 0).astype(jnp.int32)
            blk_run[cur] = run
            return cur + 1

        run_expert[run] = e
        return lax.fori_loop(0, n_blk, per_block, cursor), run + (n_blk > 0).astype(jnp.int32)

    cursor, runs = lax.fori_loop(0, counts_ref.shape[0], per_expert, (jnp.int32(0), jnp.int32(0)))
    run_expert[runs] = -1
    run_expert[runs + 1] = -1

    def empty(j, carry):
        blk_expert[j] = 0
        blk_valid[j] = 0
        blk_first[j] = 0
        blk_run[j] = runs - 1
        return carry

    lax.fori_loop(cursor, nb, empty, 0)


def _expert_body(counts_ref, buf_ref, w1_hbm, w3_hbm, w2_hbm, out_ref, w1_buf, w3_buf, w2_buf, sem,
                 blk_expert, blk_valid, blk_first, blk_run, run_expert):
    i = pl.program_id(0)

    @pl.when(i == 0)
    def _():
        _plan_blocks(counts_ref, blk_expert, blk_valid, blk_first, blk_run, run_expert)

    expert = blk_expert[i]
    valid = blk_valid[i]
    run = blk_run[i]
    slot = run % WEIGHT_SLOTS
    ahead1 = run_expert[run + 1]
    ahead2 = run_expert[run + 2]

    def weight_copies(e, s):
        return (pltpu.make_async_copy(w1_hbm.at[e], w1_buf.at[s], sem.at[s, 0]),
                pltpu.make_async_copy(w3_hbm.at[e], w3_buf.at[s], sem.at[s, 1]),
                pltpu.make_async_copy(w2_hbm.at[e], w2_buf.at[s], sem.at[s, 2]))

    @pl.when(i == 0)
    def _():
        for c in weight_copies(expert, slot):
            c.start()

        @pl.when(ahead1 >= 0)
        def _():
            for c in weight_copies(ahead1, (slot + 1) % WEIGHT_SLOTS):
                c.start()

    @pl.when(blk_first[i] == 1)
    def _():
        for c in weight_copies(expert, slot):
            c.wait()

        @pl.when(ahead2 >= 0)
        def _():
            for c in weight_copies(ahead2, (slot + 2) % WEIGHT_SLOTS):
                c.start()

    def mlp(rows):
        x = _unpack_bf16_pair(jnp.concatenate([buf_ref[j, :rows, :] for j in range(SC_SPLIT)], axis=1))
        row = lax.broadcasted_iota(jnp.int32, x.shape, 0)
        x = jnp.where(row < valid, x, 0.0).astype(BF16)
        h1 = _dot(x, w1_buf[slot].astype(BF16))
        h3 = _dot(x, w3_buf[slot].astype(BF16))
        a = (h1 * _sigmoid(h1) * h3).astype(BF16)
        packed = _pack_bf16_pair(_dot(a, w2_buf[slot].astype(BF16)))
        for j in range(SC_SPLIT):
            out_ref[j, :rows, :] = packed[:, j * SC_ROW:(j + 1) * SC_ROW]
            if rows < BM:
                out_ref[j, rows:, :] = jnp.zeros((BM - rows, SC_ROW), out_ref.dtype)

    for rows in range(BM_STEP, BM + 1, BM_STEP):
        @pl.when((valid > rows - BM_STEP) & (valid <= rows))
        def _():
            mlp(rows)

    @pl.when(valid <= 0)
    def _():
        out_ref[...] = jnp.zeros_like(out_ref)


def _experts(counts, buf, w1, w3, w2):
    _, nslots, _ = buf.shape
    nb = nslots // BM
    rows = pl.BlockSpec((SC_SPLIT, BM, SC_ROW), lambda i, counts: (0, i, 0))
    hbm = pl.BlockSpec(memory_space=pl.ANY)
    table = pltpu.SMEM((nb,), jnp.int32)
    grid_spec = pltpu.PrefetchScalarGridSpec(
        num_scalar_prefetch=1,
        grid=(nb,),
        in_specs=[rows, hbm, hbm, hbm],
        out_specs=rows,
        scratch_shapes=[pltpu.VMEM((WEIGHT_SLOTS,) + w1.shape[1:], w1.dtype),
                        pltpu.VMEM((WEIGHT_SLOTS,) + w3.shape[1:], w3.dtype),
                        pltpu.VMEM((WEIGHT_SLOTS,) + w2.shape[1:], w2.dtype),
                        pltpu.SemaphoreType.DMA((WEIGHT_SLOTS, 3)),
                        table, table, table, table,
                        pltpu.SMEM((counts.shape[0] + 2,), jnp.int32)],
    )
    return pl.pallas_call(
        _expert_body,
        grid_spec=grid_spec,
        out_shape=jax.ShapeDtypeStruct(buf.shape, jnp.uint32),
        compiler_params=pltpu.CompilerParams(dimension_semantics=("arbitrary",),
                                             vmem_limit_bytes=VMEM_LIMIT),
        name="experts",
    )(counts, buf, w1, w3, w2)


def _sc_mesh():
    return plsc.VectorSubcoreMesh(core_axis_name="core", subcore_axis_name="subcore")


def _dispatch_rows(rows, dest0, dest1, nslots):
    t, w = rows.shape
    win = SC_WINDOW
    idx_spec = pl.BlockSpec((1, win), lambda i: (0, i))

    @functools.partial(pl.kernel, mesh=_sc_mesh(), scratch_types=[],
                       out_type=jax.ShapeDtypeStruct((nslots, w), rows.dtype), name="dispatch_rows")
    def run(rows_hbm, i0_hbm, i1_hbm, out_hbm):
        def body(rows_vmem, i0_vmem, i1_vmem):
            pltpu.sync_copy(rows_vmem, out_hbm.at[i0_vmem.at[0]])
            pltpu.sync_copy(rows_vmem, out_hbm.at[i1_vmem.at[0]])

        pltpu.emit_pipeline(
            body, grid=(t // win,),
            in_specs=[pl.BlockSpec((win, w), lambda i: (i, 0)), idx_spec, idx_spec],
            out_specs=[],
            core_axis_name=("core", "subcore"),
            dimension_semantics=(pltpu.PARALLEL,),
        )(rows_hbm, i0_hbm, i1_hbm)

    return run(rows, dest0.reshape(1, t), dest1.reshape(1, t))


def _gather_rows(table, idx):
    n = idx.shape[0]
    w = table.shape[1]
    win = SC_WINDOW

    @functools.partial(pl.kernel, mesh=_sc_mesh(), scratch_types=[],
                       out_type=jax.ShapeDtypeStruct((n, w), table.dtype), name="gather_rows")
    def run(table_hbm, i_hbm, out_hbm):
        def body(i_vmem, out_vmem):
            pltpu.sync_copy(table_hbm.at[i_vmem.at[0]], out_vmem)

        pltpu.emit_pipeline(
            body, grid=(n // win,),
            in_specs=[pl.BlockSpec((1, win), lambda i: (0, i))],
            out_specs=[pl.BlockSpec((win, w), lambda i: (i, 0))],
            core_axis_name=("core", "subcore"),
            dimension_semantics=(pltpu.PARALLEL,),
        )(i_hbm, out_hbm)

    return run(table, idx.reshape(1, n))


def _combine_body(h_ref, g_ref, rt_ref, gfin_ref, out_ref):
    rt = rt_ref[...]
    y = h_ref[...]
    for k in range(TOP_K):
        rows = jnp.concatenate([g_ref[k * SC_SPLIT + j] for j in range(SC_SPLIT)], axis=1)
        y = y + rt[:, 4 + k:5 + k] * _unpack_bf16_pair(rows)
    out_ref[...] = _rms(y, gfin_ref[...])


def _combine(h, g, rt, g_final):
    t, d = h.shape
    tm = TM_OUT
    tok = lambda i: (i, 0)
    return pl.pallas_call(
        _combine_body,
        grid=(t // tm,),
        in_specs=[pl.BlockSpec((tm, d), tok),
                  pl.BlockSpec((TOP_K * SC_SPLIT, tm, SC_ROW), lambda i: (0, i, 0)),
                  pl.BlockSpec((tm, LANES), tok),
                  pl.BlockSpec((1, d), lambda i: (0, 0))],
        out_specs=pl.BlockSpec((tm, d), tok),
        out_shape=jax.ShapeDtypeStruct((t, d), F32),
        compiler_params=pltpu.CompilerParams(dimension_semantics=("arbitrary",),
                                             vmem_limit_bytes=VMEM_LIMIT),
        name="combine",
    )(h, g, rt, g_final)


def _layer(h, mem, g_mix, g_mem, w_in, w_gate, b_gate, lam_re, lam_im, log_dt, b_re, b_im,
           c_re, c_im, d_skip, w_glu, b_glu, g_sgu, w_spatial, b_spatial, w_kv, w_branch,
           w_out, g_ffn, w_group, b_group, w_router, b_router, w1, w3, w2, g_out):
    bsz, s, d = h.shape
    t = bsz * s
    row = lambda a: a.reshape(1, -1)

    k, v = _kv_proj(mem, row(g_mem), w_kv.astype(BF16))

    tril = jnp.tril(jnp.ones((CHUNK, CHUNK), dtype=bool))
    w_sp = jnp.where(tril, w_spatial, 0.0).astype(BF16)
    b_sp = jnp.broadcast_to(b_spatial[:, :, None], (SGU_HEADS, CHUNK, SGU_HEAD_DIM))
    u2, y_b, y_c = _in_proj(h, row(g_mix), w_in.astype(BF16), row(g_sgu), w_sp, b_sp, k, v)
    y2 = _ssm(u2, *_ssm_params(lam_re, lam_im, log_dt, b_re, b_im, c_re, c_im, d_skip), n_seq=bsz)

    pad = LANES - N_GROUPS - N_EXPERTS
    w_rt = jnp.concatenate([w_group, w_router, jnp.zeros((d, pad), F32)], axis=1)
    w_rt_hi = w_rt.astype(BF16)
    w_rt = jnp.concatenate([w_rt_hi, (w_rt - w_rt_hi.astype(F32)).astype(BF16)], axis=1)
    b_rt =jnp.concatenate([b_group, b_router, jnp.zeros((pad,), F32)]).reshape(1, LANES)
    h2, xnp, rt, rtt, cnt = _merge_route(
        h.reshape(t, d), y2, y_b.reshape(t, -1), y_c.reshape(t, -1), row(g_mix),
        w_gate.astype(BF16), row(b_gate), w_glu.astype(BF16), row(b_glu),
        w_branch.astype(BF16), w_out.astype(BF16), row(g_ffn), w_rt, b_rt)

    assert BM & (BM - 1) == 0, "block padding arithmetic assumes a power-of-two block"
    counts = cnt[0, ROUTE_LANE0:ROUTE_LANE0 + N_EXPERTS].astype(jnp.int32)
    nb = (t * TOP_K) // BM + N_EXPERTS
    nslots = nb * BM
    dest_p = _slot_rows(rtt, cnt, nslots).reshape(TOP_K, SC_SPLIT * t)
    buf = _dispatch_rows(xnp.reshape(SC_SPLIT * t, SC_ROW), dest_p[0], dest_p[1], SC_SPLIT * nslots)
    yb = _experts(counts, buf.reshape(SC_SPLIT, nslots, SC_ROW), w1, w3, w2)
    g = _gather_rows(yb.reshape(SC_SPLIT * nslots, SC_ROW), dest_p.reshape(-1))
    out = _combine(h2, g.reshape(TOP_K * SC_SPLIT, t, SC_ROW), rt, row(g_out))
    return out.reshape(bsz, s, d)


def kernel(x, mem, g_mix, g_mem, w_in, w_gate, b_gate, lam_re, lam_im, log_dt, b_re, b_im, c_re,
           c_im, d_skip, w_glu, b_glu, g_sgu, w_spatial, b_spatial, w_kv, w_branch, w_out, g_ffn,
           w_group, b_group, w_router, b_router, w1, w3, w2, g_final):
    assert g_mix.shape[0] == 1, "single-layer stack"
    return _layer(x, mem, g_mix[0], g_mem[0], w_in[0], w_gate[0], b_gate[0], lam_re[0], lam_im[0],
                  log_dt[0], b_re[0], b_im[0], c_re[0], c_im[0], d_skip[0], w_glu[0], b_glu[0],
                  g_sgu[0], w_spatial[0], b_spatial[0], w_kv[0], w_branch[0], w_out[0], g_ffn[0],
                  w_group[0], b_group[0], w_router[0], b_router[0], w1[0], w3[0], w2[0], g_final)
```

```python
import functools
import math

import jax
import jax.numpy as jnp
from jax import lax
from jax.experimental import pallas as pl
from jax.experimental.pallas import tpu as pltpu
from jax.experimental.pallas import tpu_sc as plsc

F32 = jnp.float32
BF16 = jnp.bfloat16

EPS = 1e-6
D_MODEL = 1024
SSM_WIDTH = 512
SSM_GROUP = 16
SSM_GROUPS = 32
SSM_STATE = 64
SSM_CHUNK = 16
SSM_GROUPS_PER_STEP = 2
SGU_WIDTH = 512
SGU_HEADS = 4
SGU_HEAD_DIM = 128
CHUNK = 128
XA_HEADS = 4
XA_HEAD_DIM = 128
N_GROUPS = 8
EXPERTS_PER_GROUP = 8
N_EXPERTS = 64
TOP_K = 2
D_FF = 512
LANES = 128
ROUTE_LANE0 = N_GROUPS

TM_IN = 1024
TM_MERGE = 512
MERGE_COLS = 256
TM_OUT = 1024
BM = 512
BM_STEP = 128
WEIGHT_SLOTS = 3
SC_WINDOW = 128
SC_ROW = 256
SC_SPLIT = (D_MODEL // 2) // SC_ROW
SLOT_LANES = 2048
VMEM_LIMIT = 56 * 1024 * 1024


def _rms(x, g):
    return x * lax.rsqrt(jnp.mean(x * x, axis=-1, keepdims=True) + EPS) * g


def _sigmoid(x):
    return 0.5 * (1.0 + jnp.tanh(0.5 * x))


def _gelu(x):
    c = math.sqrt(2.0 / math.pi)
    return 0.5 * x * (1.0 + jnp.tanh(c * (x + 0.044715 * (x * x * x))))


def _dot(a, b):
    return jnp.dot(a, b, preferred_element_type=F32)


_NT = (((1,), (1,)), ((), ()))


def _pack_bf16_pair(x):
    n = x.shape[1] // 2
    lo = lax.bitcast_convert_type(x[:, :n].astype(BF16).astype(F32), jnp.uint32)
    hi = lax.bitcast_convert_type(x[:, n:].astype(BF16).astype(F32), jnp.uint32)
    return hi | (lo >> 16)


def _unpack_bf16_pair(p):
    lo = lax.bitcast_convert_type(p << 16, F32)
    hi = lax.bitcast_convert_type(p & jnp.uint32(0xFFFF0000), F32)
    return jnp.concatenate([lo, hi], axis=1)


GROUPS_PER_TILE = LANES // SSM_GROUP
POS_PER_TILE = LANES // SSM_GROUP


def _slot_masks(rows):
    lane = lax.broadcasted_iota(jnp.int32, (rows, LANES), 1)
    return [(lane >= i * SSM_GROUP) & (lane < (i + 1) * SSM_GROUP) for i in range(LANES // SSM_GROUP)]


def _tokens_to_chunks(tok_ref, out_ref):
    tm = tok_ref.shape[1]
    nc = tm // SSM_CHUNK
    masks = _slot_masks(nc)
    for k in range(SSM_WIDTH // LANES):
        for j in range(SSM_CHUNK // POS_PER_TILE):
            src = [tok_ref[k, pl.ds(j * POS_PER_TILE + p, nc, stride=SSM_CHUNK), :]
                   for p in range(POS_PER_TILE)]
            for gi in range(GROUPS_PER_TILE):
                acc = None
                for p in range(POS_PER_TILE):
                    shift = ((p - gi) * SSM_GROUP) % LANES
                    r = pltpu.roll(src[p], shift, 1) if shift else src[p]
                    acc = r if acc is None else jnp.where(masks[p], r, acc)
                out_ref[k * GROUPS_PER_TILE + gi, :, pl.ds(j * LANES, LANES)] = acc.astype(out_ref.dtype)


def _chunks_to_tokens(chunk_ref, tok_ref):
    tm = tok_ref.shape[1]
    nc = tm // SSM_CHUNK
    masks = _slot_masks(nc)
    for k in range(SSM_WIDTH // LANES):
        for j in range(SSM_CHUNK // POS_PER_TILE):
            src = [chunk_ref[k * GROUPS_PER_TILE + gi, :, pl.ds(j * LANES, LANES)].astype(F32)
                   for gi in range(GROUPS_PER_TILE)]
            for p in range(POS_PER_TILE):
                acc = None
                for gi in range(GROUPS_PER_TILE):
                    shift = ((gi - p) * SSM_GROUP) % LANES
                    r = pltpu.roll(src[gi], shift, 1) if shift else src[gi]
                    acc = r if acc is None else jnp.where(masks[gi], r, acc)
                tok_ref[k, pl.ds(j * POS_PER_TILE + p, nc, stride=SSM_CHUNK), :] = acc


def _kv_body(mem_ref, g_ref, w_ref, k_ref, v_ref):
    n = _rms(mem_ref[0], g_ref[...]).astype(BF16)
    kv = _dot(n, w_ref[...])
    k_ref[0] = kv[:, :SGU_WIDTH].astype(BF16)
    v_ref[0] = kv[:, SGU_WIDTH:].astype(BF16)


def _kv_proj(mem, g_mem, w_kv):
    b, m, d = mem.shape
    return pl.pallas_call(
        _kv_body,
        grid=(b,),
        in_specs=[pl.BlockSpec((1, m, d), lambda i: (i, 0, 0)),
                  pl.BlockSpec((1, d), lambda i: (0, 0)),
                  pl.BlockSpec((d, 2 * SGU_WIDTH), lambda i: (0, 0))],
        out_specs=[pl.BlockSpec((1, m, SGU_WIDTH), lambda i: (i, 0, 0)),
                   pl.BlockSpec((1, m, SGU_WIDTH), lambda i: (i, 0, 0))],
        out_shape=[jax.ShapeDtypeStruct((b, m, SGU_WIDTH), BF16),
                   jax.ShapeDtypeStruct((b, m, SGU_WIDTH), BF16)],
        compiler_params=pltpu.CompilerParams(dimension_semantics=("arbitrary",),
                                             vmem_limit_bytes=VMEM_LIMIT),
        name="kv_proj",
    )(mem, g_mem, w_kv)


def _in_body(x_ref, gmix_ref, win_ref, gsgu_ref, wsp_ref, bsp_ref, k_ref, v_ref,
             u2_ref, yb_ref, yc_ref, tok_ref):
    n = _rms(x_ref[0], gmix_ref[...]).astype(BF16)
    proj = _dot(n, win_ref[...])
    for k in range(SSM_WIDTH // LANES):
        tok_ref[k] = proj[:, k * LANES:(k + 1) * LANES]
    _tokens_to_chunks(tok_ref, u2_ref)

    u = _gelu(proj[:, SSM_WIDTH:SSM_WIDTH + SGU_WIDTH])
    v = _gelu(proj[:, SSM_WIDTH + SGU_WIDTH:SSM_WIDTH + 2 * SGU_WIDTH])
    v = _rms(v, gsgu_ref[...]).astype(BF16)
    tm = u.shape[0]
    rows = []
    for c in range(tm // CHUNK):
        vc = v[c * CHUNK:(c + 1) * CHUNK]
        heads = []
        for h in range(SGU_HEADS):
            sl = slice(h * SGU_HEAD_DIM, (h + 1) * SGU_HEAD_DIM)
            heads.append(_dot(wsp_ref[h], vc[:, sl]) + bsp_ref[h])
        rows.append(jnp.concatenate(heads, axis=1))
    sv = jnp.concatenate(rows, axis=0)
    yb_ref[0] = (u * sv).astype(BF16)

    q = proj[:, SSM_WIDTH + 2 * SGU_WIDTH:].astype(BF16)
    kk = k_ref[0]
    vv = v_ref[0]
    outs = []
    for h in range(XA_HEADS):
        sl = slice(h * XA_HEAD_DIM, (h + 1) * XA_HEAD_DIM)
        s = lax.dot_general(q[:, sl], kk[:, sl], (((1,), (1,)), ((), ())),
                            preferred_element_type=F32) * (XA_HEAD_DIM ** -0.5)
        e = jnp.exp(s - jnp.max(s, axis=-1, keepdims=True))
        l = jnp.sum(e, axis=-1, keepdims=True)
        outs.append(_dot(e.astype(BF16), vv[:, sl]) / l)
    yc_ref[0] = jnp.concatenate(outs, axis=1).astype(BF16)


def _in_proj(x, g_mix, w_in, g_sgu, w_sp, b_sp, k, v):
    b, s, d = x.shape
    m = k.shape[1]
    const2 = lambda i, j: (0, 0)
    const3 = lambda i, j: (0, 0, 0)
    tok = lambda i, j: (i, j, 0)
    per_b = lambda i, j: (i, 0, 0)
    out = jax.ShapeDtypeStruct((b, s, SSM_WIDTH), BF16)
    nc = TM_IN // SSM_CHUNK
    tiles = s // TM_IN
    u2 = jax.ShapeDtypeStruct((SSM_GROUPS, b * s // SSM_CHUNK, SSM_CHUNK * SSM_GROUP), BF16)
    return pl.pallas_call(
        _in_body,
        grid=(b, s // TM_IN),
        in_specs=[pl.BlockSpec((1, TM_IN, d), tok),
                  pl.BlockSpec((1, d), const2),
                  pl.BlockSpec(w_in.shape, const2),
                  pl.BlockSpec((1, SGU_WIDTH), const2),
                  pl.BlockSpec(w_sp.shape, const3),
                  pl.BlockSpec(b_sp.shape, const3),
                  pl.BlockSpec((1, m, SGU_WIDTH), per_b),
                  pl.BlockSpec((1, m, SGU_WIDTH), per_b)],
        out_specs=[pl.BlockSpec((SSM_GROUPS, nc, SSM_CHUNK * SSM_GROUP), lambda i, j: (0, i * tiles + j, 0)),
                   pl.BlockSpec((1, TM_IN, SSM_WIDTH), tok),
                   pl.BlockSpec((1, TM_IN, SSM_WIDTH), tok)],
        out_shape=[u2, out, out],
        scratch_shapes=[pltpu.VMEM((SSM_WIDTH // LANES, TM_IN, LANES), F32)],
        compiler_params=pltpu.CompilerParams(dimension_semantics=("arbitrary", "arbitrary"),
                                             vmem_limit_bytes=VMEM_LIMIT),
        name="in_proj",
    )(x, g_mix, w_in, g_sgu, w_sp, b_sp, k, v)


def _alternate(*stages):
    live = list(stages)
    while live:
        live = [s for s in live if next(s, True) is None]


def _ssm_params(lam_re, lam_im, log_dt, b_re, b_im, c_re, c_im, d_skip):
    g, p = lam_re.shape
    dup = lambda a: jnp.concatenate([a, a], axis=-1)
    lam = jnp.stack([dup(lam_re), dup(lam_im), jnp.broadcast_to(log_dt[:, None], (g, 2 * p))], axis=1)
    brt = b_re.transpose(0, 2, 1)
    bit = b_im.transpose(0, 2, 1)
    cat = lambda a, b: jnp.concatenate([a, b], axis=-1)
    bc = jnp.stack([cat(brt, bit), cat(bit, brt), cat(c_re, -c_im), cat(-c_im, -c_re)], axis=1)
    d2 = jnp.tile(d_skip.reshape(g, 1, SSM_GROUP), (1, 1, SSM_CHUNK))
    return lam, bc, d2


def _ssm_operators(lam_ref, bc_ref, ccat_ref, n_ref, m_ref):
    lam_re = lam_ref[0:1, :]
    lam_im = lam_ref[1:2, :]
    dt = jnp.exp(lam_ref[2:3, :])
    ar = lam_re * dt
    ai = lam_im * dt
    lane = lax.broadcasted_iota(jnp.int32, (1, LANES), 1)
    sgn = jnp.where(lane >= SSM_STATE, 1.0, -1.0)

    def powers(j):
        mag = jnp.exp(ar * j)
        ph = ai * j
        return mag * jnp.cos(ph), mag * jnp.sin(ph)

    pos = lax.broadcasted_iota(jnp.int32, (SSM_CHUNK, 1), 0).astype(F32)
    p_re, p_im = powers(pos)
    r_re, r_im = powers((SSM_CHUNK - 1) - pos)
    one_re, one_im = powers(jnp.ones((1, 1), F32))
    q_re = p_re * one_re - p_im * one_im
    q_im = p_re * one_im + p_im * one_re
    step = lax.shift_left(jnp.full((8, 1), SSM_CHUNK, jnp.int32),
                          lax.broadcasted_iota(jnp.int32, (8, 1), 0)).astype(F32)
    s_re, s_im = powers(step)

    den = lam_re * lam_re + lam_im * lam_im
    f_re = ((one_re - 1.0) * lam_re + one_im * lam_im) / den
    f_im = (one_im * lam_re - (one_re - 1.0) * lam_im) / den
    b1, b2, ca, cb = bc_ref[0], bc_ref[1], bc_ref[2], bc_ref[3]
    bb1 = f_re * b1 + (sgn * f_im) * b2
    bb2 = f_re * b2 - (sgn * f_im) * b1
    r_ims = sgn * r_im
    for s in range(SSM_CHUNK):
        blk = pl.ds(s * SSM_GROUP, SSM_GROUP)
        ccat_ref[blk, :] = ca * p_re[s:s + 1, :] + cb * p_im[s:s + 1, :]
        m_ref[blk, :] = (ca * q_re[s:s + 1, :] + cb * q_im[s:s + 1, :]).astype(m_ref.dtype)
        n_ref[blk, :] = (bb1 * r_re[s:s + 1, :] + bb2 * r_ims[s:s + 1, :]).astype(n_ref.dtype)
    return bb1, s_re, sgn * s_im


def _ssm_body(u_ref, lam_ref, bc_ref, d2_ref, y_ref, toep_ref, ccat_ref, n_ref, m_ref, *, n_seq):
    for g in range(u_ref.shape[0]):
        _ssm_group(u_ref.at[g], lam_ref.at[g], bc_ref.at[g], d2_ref.at[g], y_ref.at[g],
                   toep_ref.at[g], ccat_ref.at[g], n_ref.at[g], m_ref.at[g], n_seq)


def _ssm_group(u_ref, lam_ref, bc_ref, d2_ref, y_ref, toep_ref, ccat_ref, n_ref, m_ref, n_seq):
    bcat, lr, li = _ssm_operators(lam_ref, bc_ref, ccat_ref, n_ref, m_ref)
    kern = lax.dot_general(bcat, ccat_ref[...], _NT, precision=lax.Precision.HIGHEST,
                           preferred_element_type=F32)
    col = lax.broadcasted_iota(jnp.int32, kern.shape, 1)
    for s in range(SSM_CHUNK):
        shifted = pltpu.roll(kern, s * SSM_GROUP, 1) if s else kern
        toep_ref[s * SSM_GROUP:(s + 1) * SSM_GROUP, :] = jnp.where(
            col >= s * SSM_GROUP, shifted, 0.0).astype(BF16)

    u = u_ref[...]
    rows = u.shape[0]
    per = rows // n_seq
    y = _dot(u, toep_ref[...])
    st = _dot(u, n_ref[...])
    row = lax.broadcasted_iota(jnp.int32, (per, LANES), 0)
    prev = []
    for b in range(n_seq):
        x = st[b * per:(b + 1) * per]
        k = 0
        while (1 << k) < per:
            d = 1 << k
            sh = jnp.where(row >= d, pltpu.roll(x, d, 0), 0.0)
            x = x + sh * lr[k:k + 1, :] + pltpu.roll(sh, SSM_STATE, 1) * li[k:k + 1, :]
            k += 1
        prev.append(jnp.where(row >= 1, pltpu.roll(x, 1, 0), 0.0))
    xp = jnp.concatenate(prev, axis=0).astype(BF16)
    y = y + lax.dot_general(xp, m_ref[...], _NT, preferred_element_type=F32) + d2_ref[...] * u.astype(F32)
    y_ref[...] = _gelu(y).astype(BF16)


def _ssm(u2, lam, bc, d2, n_seq):
    g, rows, w = u2.shape
    assert rows // n_seq <= 1 << 8, "lam_bar^(16*2^k) is prepared for 8 scan steps"
    gs = SSM_GROUPS_PER_STEP
    blk = lambda a: pl.BlockSpec((gs,) + a.shape[1:], lambda i: (i,) + (0,) * (a.ndim - 1))
    return pl.pallas_call(
        functools.partial(_ssm_body, n_seq=n_seq),
        grid=(g // gs,),
        in_specs=[blk(u2), blk(lam), blk(bc), blk(d2)],
        out_specs=blk(u2),
        out_shape=jax.ShapeDtypeStruct(u2.shape, BF16),
        scratch_shapes=[pltpu.VMEM((gs, w, w), BF16),
                        pltpu.VMEM((gs, w, 2 * SSM_STATE), F32),
                        pltpu.VMEM((gs, w, 2 * SSM_STATE), BF16),
                        pltpu.VMEM((gs, w, 2 * SSM_STATE), BF16)],
        compiler_params=pltpu.CompilerParams(dimension_semantics=("arbitrary",),
                                             vmem_limit_bytes=VMEM_LIMIT),
        name="ssm",
    )(u2, lam, bc, d2)


def _mix_tile(x_ref, y2_ref, yb_ref, yc_ref, gmix_ref, wgate_ref, bgate_ref, wglu_ref, bglu_ref,
              wbr_ref, wout_ref, tok_ref, h_ref, hkeep_ref):
    x = x_ref[...]
    n = _rms(x, gmix_ref[...]).astype(BF16)

    def gated(b, c, y):
        cols = pl.ds(b * D_MODEL + c * MERGE_COLS, MERGE_COLS)
        gate = _sigmoid(_dot(n, wgate_ref[:, cols]) + bgate_ref[:, cols])
        return gate * _dot(y, wbr_ref[b, :, pl.ds(c * MERGE_COLS, MERGE_COLS)])

    n_blocks = D_MODEL // MERGE_COLS
    yb = yb_ref[...]
    yc = yc_ref[...]
    head = gated(1, 0, yb) + gated(2, 0, yc)
    yield
    _chunks_to_tokens(y2_ref, tok_ref)
    ys = jnp.concatenate([tok_ref[k] for k in range(SSM_WIDTH // LANES)], axis=1).astype(BF16)
    glu = _dot(ys, wglu_ref[...]) + bglu_ref[...]
    ya = (glu[:, :SSM_WIDTH] * _sigmoid(glu[:, SSM_WIDTH:])).astype(BF16)
    merged = [(head + gated(0, 0, ya)).astype(BF16)]
    for c in range(1, n_blocks):
        yield
        merged.append((gated(0, c, ya) + gated(1, c, yb) + gated(2, c, yc)).astype(BF16))
    yield
    h = x + _dot(jnp.concatenate(merged, axis=1), wout_ref[...])
    h_ref[...] = h
    hkeep_ref[...] = h


def _route_tile(hkeep_ref, gffn_ref, wrt_ref, brt_ref, xnp_ref, rt_ref, rtt_ref, cnt_ref, carry_ref):
    h = hkeep_ref[...]
    xn = _rms(h, gffn_ref[...])
    packed = _pack_bf16_pair(xn)
    for j in range(SC_SPLIT):
        xnp_ref[j] = packed[:, j * SC_ROW:(j + 1) * SC_ROW]
    yield

    x_hi = xn.astype(BF16)
    x_lo = (xn - x_hi.astype(F32)).astype(BF16)
    head = _dot(x_hi, wrt_ref[...])
    logits = (head[:, :LANES] + head[:, LANES:] + _dot(x_lo, wrt_ref[:, :LANES])) + brt_ref[...]
    yield
    tm = logits.shape[0]
    lane_i = lax.broadcasted_iota(jnp.int32, (tm, LANES), 1)
    lane = lane_i.astype(F32)
    neg = jnp.float32(-3.0e38)
    big = jnp.float32(LANES)
    gmask = lane_i < N_GROUPS
    gl = jnp.where(gmask, logits, neg)
    gmax = jnp.max(gl, axis=-1, keepdims=True)
    gidx = jnp.min(jnp.where(gl == gmax, lane, big), axis=-1, keepdims=True)
    gsum = jnp.sum(jnp.where(gmask, jnp.exp(gl - gmax), 0.0), axis=-1, keepdims=True)
    g_w = 1.0 / gsum
    e_lane = lane_i - ROUTE_LANE0
    lane_group = (e_lane >> 3).astype(F32)
    emask = (e_lane >= 0) & (e_lane < N_EXPERTS) & (lane_group == gidx)
    el = jnp.where(emask, logits, neg)
    m1 = jnp.max(el, axis=-1, keepdims=True)
    i1 = jnp.min(jnp.where(el == m1, lane, big), axis=-1, keepdims=True)
    el2 = jnp.where(lane == i1, neg, el)
    m2 = jnp.max(el2, axis=-1, keepdims=True)
    i2 = jnp.min(jnp.where(el2 == m2, lane, big), axis=-1, keepdims=True)
    t = jnp.exp(m2 - m1)
    w1 = g_w / (1.0 + t)
    w2 = g_w * t / (1.0 + t)
    yield

    sel1 = lane == i1
    sel2 = lane == i2
    onehot = jnp.where(sel1 | sel2, 1.0, 0.0)
    r_i = lax.broadcasted_iota(jnp.int32, (tm, tm), 0)
    c_i = lax.broadcasted_iota(jnp.int32, (tm, tm), 1)
    stril = jnp.where(c_i < r_i, 1.0, 0.0).astype(BF16)
    cum = _dot(stril, onehot.astype(BF16)) + carry_ref[0:1, :]
    rank1 = jnp.sum(jnp.where(sel1, cum, 0.0), axis=-1, keepdims=True)
    rank2 = jnp.sum(jnp.where(sel2, cum, 0.0), axis=-1, keepdims=True)
    carry_ref[...] = carry_ref[...] + jnp.sum(onehot, axis=0, keepdims=True)
    cnt_ref[...] = carry_ref[...]
    yield

    cols = (i1 - ROUTE_LANE0, i2 - ROUTE_LANE0, rank1, rank2, w1, w2)
    rt = jnp.zeros((tm, LANES), F32)
    for c, val in enumerate(cols):
        rt = jnp.where(lane_i == c, val, rt)
    rt_ref[...] = rt
    rtt_ref[...] = rt.T[:8]


def _merge_body(x_ref, y2_ref, yb_ref, yc_ref, gmix_ref, wgate_ref, bgate_ref, wglu_ref, bglu_ref,
                wbr_ref, wout_ref, gffn_ref, wrt_ref, brt_ref,
                h_ref, xnp_ref, rt_ref, rtt_ref, cnt_ref, carry_ref, tok_ref, hkeep_ref):
    i = pl.program_id(0)
    last = pl.num_programs(0) - 1
    cur = hkeep_ref.at[i % 2]
    prev = hkeep_ref.at[(i + 1) % 2]

    def mix():
        return _mix_tile(x_ref, y2_ref, yb_ref, yc_ref, gmix_ref, wgate_ref, bgate_ref, wglu_ref,
                         bglu_ref, wbr_ref, wout_ref, tok_ref, h_ref, cur)

    def route():
        return _route_tile(prev, gffn_ref, wrt_ref, brt_ref, xnp_ref, rt_ref, rtt_ref, cnt_ref, carry_ref)

    @pl.when(i == 0)
    def _():
        carry_ref[...] = jnp.zeros_like(carry_ref)
        _alternate(mix())

    @pl.when((i > 0) & (i < last))
    def _():
        _alternate(route(), mix())

    @pl.when(i == last)
    def _():
        _alternate(route())


def _merge_route(x, y2, yb, yc, g_mix, w_gate, b_gate, w_glu, b_glu, w_br, w_out, g_ffn, w_rt, b_rt):
    t, d = x.shape
    tm = TM_MERGE
    tiles = t // tm
    mixed = lambda i: jnp.minimum(i, tiles - 1)
    routed = lambda i: jnp.maximum(i - 1, 0)
    c2 = lambda i: (0, 0)
    c3 = lambda i: (0, 0, 0)
    full = lambda a: pl.BlockSpec(a.shape, c2 if a.ndim == 2 else c3)
    return pl.pallas_call(
        _merge_body,
        grid=(tiles + 1,),
        in_specs=[pl.BlockSpec((tm, d), lambda i: (mixed(i), 0)),
                  pl.BlockSpec((SSM_GROUPS, tm // SSM_CHUNK, SSM_CHUNK * SSM_GROUP),
                               lambda i: (0, mixed(i), 0)),
                  pl.BlockSpec((tm, SSM_WIDTH), lambda i: (mixed(i), 0)),
                  pl.BlockSpec((tm, SSM_WIDTH), lambda i: (mixed(i), 0)),
                  full(g_mix), full(w_gate), full(b_gate), full(w_glu), full(b_glu),
                  full(w_br), full(w_out), full(g_ffn), full(w_rt), full(b_rt)],
        out_specs=[pl.BlockSpec((tm, d), lambda i: (mixed(i), 0)),
                   pl.BlockSpec((SC_SPLIT, tm, SC_ROW), lambda i: (0, routed(i), 0)),
                   pl.BlockSpec((tm, LANES), lambda i: (routed(i), 0)),
                   pl.BlockSpec((8, tm), lambda i: (0, routed(i))),
                   pl.BlockSpec((8, LANES), c2)],
        out_shape=[jax.ShapeDtypeStruct((t, d), F32),
                   jax.ShapeDtypeStruct((SC_SPLIT, t, SC_ROW), jnp.uint32),
                   jax.ShapeDtypeStruct((t, LANES), F32),
                   jax.ShapeDtypeStruct((8, t), F32),
                   jax.ShapeDtypeStruct((8, LANES), F32)],
        scratch_shapes=[pltpu.VMEM((8, LANES), F32),
                        pltpu.VMEM((SSM_WIDTH // LANES, tm, LANES), F32),
                        pltpu.VMEM((2, tm, d), F32)],
        compiler_params=pltpu.CompilerParams(dimension_semantics=("arbitrary",),
                                             vmem_limit_bytes=VMEM_LIMIT),
        name="merge_route",
    )(x, y2, yb, yc, g_mix, w_gate, b_gate, w_glu, b_glu, w_br, w_out, g_ffn, w_rt, b_rt)


def _slot_body(rtt_ref, cnt_ref, out_ref, *, nslots):
    tl = rtt_ref.shape[1]
    blocks = jnp.ceil(cnt_ref[...] * (1.0 / BM))
    k_i = lax.broadcasted_iota(jnp.int32, (LANES, LANES), 0)
    l_i = lax.broadcasted_iota(jnp.int32, (LANES, LANES), 1)
    before = jnp.where(k_i < l_i, 1.0, 0.0).astype(BF16)
    first_blk = _dot(blocks.astype(BF16), before).astype(BF16)
    lane_of = lax.broadcasted_iota(jnp.int32, (LANES, tl), 0).astype(F32) - ROUTE_LANE0
    for k in range(TOP_K):
        onehot = jnp.where(rtt_ref[k:k + 1, :] == lane_of, 1.0, 0.0).astype(BF16)
        start = _dot(first_blk, onehot) * BM
        slot = (start[0:1, :] + rtt_ref[TOP_K + k:TOP_K + k + 1, :]).astype(jnp.int32)
        for j in range(SC_SPLIT):
            out_ref[k * SC_SPLIT + j:k * SC_SPLIT + j + 1, :] = slot + j * nslots


def _slot_rows(rtt, cnt, nslots):
    t = rtt.shape[1]
    tl = SLOT_LANES
    return pl.pallas_call(
        functools.partial(_slot_body, nslots=nslots),
        grid=(t // tl,),
        in_specs=[pl.BlockSpec((8, tl), lambda i: (0, i)),
                  pl.BlockSpec((8, LANES), lambda i: (0, 0))],
        out_specs=pl.BlockSpec((TOP_K * SC_SPLIT, tl), lambda i: (0, i)),
        out_shape=jax.ShapeDtypeStruct((TOP_K * SC_SPLIT, t), jnp.int32),
        compiler_params=pltpu.CompilerParams(dimension_semantics=("arbitrary",)),
        name="slot_rows",
    )(rtt, cnt)


def _plan_blocks(counts_ref, blk_expert, blk_valid, blk_first, blk_run, run_expert):
    nb = blk_expert.shape[0]

    def per_expert(e, carry):
        cursor, run = carry
        count = counts_ref[e]
        n_blk = (count + (BM - 1)) // BM

        def per_block(b, cur):
            blk_expert[cur] = e
            blk_valid[cur] = jnp.minimum(count - b * BM, BM)
            blk_first[cur] = (b == 0).astype(jnp.int32)
            blk_run[cur] = run
            return cur + 1

        run_expert[run] = e
        return lax.fori_loop(0, n_blk, per_block, cursor), run + (n_blk > 0).astype(jnp.int32)

    cursor, runs = lax.fori_loop(0, counts_ref.shape[0], per_expert, (jnp.int32(0), jnp.int32(0)))
    run_expert[runs] = -1
    run_expert[runs + 1] = -1

    def empty(j, carry):
        blk_expert[j] = 0
        blk_valid[j] = 0
        blk_first[j] = 0
        blk_run[j] = runs - 1
        return carry

    lax.fori_loop(cursor, nb, empty, 0)


def _expert_body(counts_ref, buf_ref, w1_hbm, w3_hbm, w2_hbm, out_ref, w1_buf, w3_buf, w2_buf, sem,
                 blk_expert, blk_valid, blk_first, blk_run, run_expert):
    i = pl.program_id(0)

    @pl.when(i == 0)
    def _():
        _plan_blocks(counts_ref, blk_expert, blk_valid, blk_first, blk_run, run_expert)

    expert = blk_expert[i]
    valid = blk_valid[i]
    run = blk_run[i]
    slot = run % WEIGHT_SLOTS
    ahead1 = run_expert[run + 1]
    ahead2 = run_expert[run + 2]

    def weight_copies(e, s):
        return (pltpu.make_async_copy(w1_hbm.at[e], w1_buf.at[s], sem.at[s, 0]),
                pltpu.make_async_copy(w3_hbm.at[e], w3_buf.at[s], sem.at[s, 1]),
                pltpu.make_async_copy(w2_hbm.at[e], w2_buf.at[s], sem.at[s, 2]))

    @pl.when(i == 0)
    def _():
        for c in weight_copies(expert, slot):
            c.start()

        @pl.when(ahead1 >= 0)
        def _():
            for c in weight_copies(ahead1, (slot + 1) % WEIGHT_SLOTS):
                c.start()

    @pl.when(blk_first[i] == 1)
    def _():
        for c in weight_copies(expert, slot):
            c.wait()

        @pl.when(ahead2 >= 0)
        def _():
            for c in weight_copies(ahead2, (slot + 2) % WEIGHT_SLOTS):
                c.start()

    def mlp(rows):
        x = _unpack_bf16_pair(jnp.concatenate([buf_ref[j, :rows, :] for j in range(SC_SPLIT)], axis=1))
        row = lax.broadcasted_iota(jnp.int32, x.shape, 0)
        x = jnp.where(row < valid, x, 0.0).astype(BF16)
        h1 = _dot(x, w1_buf[slot].astype(BF16))
        h3 = _dot(x, w3_buf[slot].astype(BF16))
        a = (h1 * _sigmoid(h1) * h3).astype(BF16)
        packed = _pack_bf16_pair(_dot(a, w2_buf[slot].astype(BF16)))
        for j in range(SC_SPLIT):
            out_ref[j, :rows, :] = packed[:, j * SC_ROW:(j + 1) * SC_ROW]
            if rows < BM:
                out_ref[j, rows:, :] = jnp.zeros((BM - rows, SC_ROW), out_ref.dtype)

    for rows in range(BM_STEP, BM + 1, BM_STEP):
        @pl.when((valid > rows - BM_STEP) & (valid <= rows))
        def _():
            mlp(rows)

    @pl.when(valid <= 0)
    def _():
        out_ref[...] = jnp.zeros_like(out_ref)


def _experts(counts, buf, w1, w3, w2):
    _, nslots, _ = buf.shape
    nb = nslots // BM
    rows = pl.BlockSpec((SC_SPLIT, BM, SC_ROW), lambda i, counts: (0, i, 0))
    hbm = pl.BlockSpec(memory_space=pl.ANY)
    table = pltpu.SMEM((nb,), jnp.int32)
    grid_spec = pltpu.PrefetchScalarGridSpec(
        num_scalar_prefetch=1,
        grid=(nb,),
        in_specs=[rows, hbm, hbm, hbm],
        out_specs=rows,
        scratch_shapes=[pltpu.VMEM((WEIGHT_SLOTS,) + w1.shape[1:], w1.dtype),
                        pltpu.VMEM((WEIGHT_SLOTS,) + w3.shape[1:], w3.dtype),
                        pltpu.VMEM((WEIGHT_SLOTS,) + w2.shape[1:], w2.dtype),
                        pltpu.SemaphoreType.DMA((WEIGHT_SLOTS, 3)),
                        table, table, table, table,
                        pltpu.SMEM((counts.shape[0] + 2,), jnp.int32)],
    )
    return pl.pallas_call(
        _expert_body,
        grid_spec=grid_spec,
        out_shape=jax.ShapeDtypeStruct(buf.shape, jnp.uint32),
        compiler_params=pltpu.CompilerParams(dimension_semantics=("arbitrary",),
                                             vmem_limit_bytes=VMEM_LIMIT),
        name="experts",
    )(counts, buf, w1, w3, w2)


def _sc_mesh():
    return plsc.VectorSubcoreMesh(core_axis_name="core", subcore_axis_name="subcore")


def _dispatch_rows(rows, dest0, dest1, nslots):
    t, w = rows.shape
    win = SC_WINDOW
    idx_spec = pl.BlockSpec((1, win), lambda i: (0, i))

    @functools.partial(pl.kernel, mesh=_sc_mesh(), scratch_types=[],
                       out_type=jax.ShapeDtypeStruct((nslots, w), rows.dtype), name="dispatch_rows")
    def run(rows_hbm, i0_hbm, i1_hbm, out_hbm):
        def body(rows_vmem, i0_vmem, i1_vmem):
            pltpu.sync_copy(rows_vmem, out_hbm.at[i0_vmem.at[0]])
            pltpu.sync_copy(rows_vmem, out_hbm.at[i1_vmem.at[0]])

        pltpu.emit_pipeline(
            body, grid=(t // win,),
            in_specs=[pl.BlockSpec((win, w), lambda i: (i, 0)), idx_spec, idx_spec],
            out_specs=[],
            core_axis_name=("core", "subcore"),
            dimension_semantics=(pltpu.PARALLEL,),
        )(rows_hbm, i0_hbm, i1_hbm)

    return run(rows, dest0.reshape(1, t), dest1.reshape(1, t))


def _gather_rows(table, idx):
    n = idx.shape[0]
    w = table.shape[1]
    win = SC_WINDOW

    @functools.partial(pl.kernel, mesh=_sc_mesh(), scratch_types=[],
                       out_type=jax.ShapeDtypeStruct((n, w), table.dtype), name="gather_rows")
    def run(table_hbm, i_hbm, out_hbm):
        def body(i_vmem, out_vmem):
            pltpu.sync_copy(table_hbm.at[i_vmem.at[0]], out_vmem)

        pltpu.emit_pipeline(
            body, grid=(n // win,),
            in_specs=[pl.BlockSpec((1, win), lambda i: (0, i))],
            out_specs=[pl.BlockSpec((win, w), lambda i: (i, 0))],
            core_axis_name=("core", "subcore"),
            dimension_semantics=(pltpu.PARALLEL,),
        )(i_hbm, out_hbm)

    return run(table, idx.reshape(1, n))


def _combine_body(h_ref, g_ref, rt_ref, gfin_ref, out_ref):
    rt = rt_ref[...]
    y = h_ref[...]
    for k in range(TOP_K):
        rows = jnp.concatenate([g_ref[k * SC_SPLIT + j] for j in range(SC_SPLIT)], axis=1)
        y = y + rt[:, 4 + k:5 + k] * _unpack_bf16_pair(rows)
    out_ref[...] = _rms(y, gfin_ref[...])


def _combine(h, g, rt, g_final):
    t, d = h.shape
    tm = TM_OUT
    tok = lambda i: (i, 0)
    return pl.pallas_call(
        _combine_body,
        grid=(t // tm,),
        in_specs=[pl.BlockSpec((tm, d), tok),
                  pl.BlockSpec((TOP_K * SC_SPLIT, tm, SC_ROW), lambda i: (0, i, 0)),
                  pl.BlockSpec((tm, LANES), tok),
                  pl.BlockSpec((1, d), lambda i: (0, 0))],
        out_specs=pl.BlockSpec((tm, d), tok),
        out_shape=jax.ShapeDtypeStruct((t, d), F32),
        compiler_params=pltpu.CompilerParams(dimension_semantics=("arbitrary",),
                                             vmem_limit_bytes=VMEM_LIMIT),
        name="combine",
    )(h, g, rt, g_final)


def _layer(h, mem, g_mix, g_mem, w_in, w_gate, b_gate, lam_re, lam_im, log_dt, b_re, b_im,
           c_re, c_im, d_skip, w_glu, b_glu, g_sgu, w_spatial, b_spatial, w_kv, w_branch,
           w_out, g_ffn, w_group, b_group, w_router, b_router, w1, w3, w2, g_out):
    bsz, s, d = h.shape
    t = bsz * s
    row = lambda a: a.reshape(1, -1)

    k, v = _kv_proj(mem, row(g_mem), w_kv.astype(BF16))

    tril = jnp.tril(jnp.ones((CHUNK, CHUNK), dtype=bool))
    w_sp = jnp.where(tril, w_spatial, 0.0).astype(BF16)
    b_sp = jnp.broadcast_to(b_spatial[:, :, None], (SGU_HEADS, CHUNK, SGU_HEAD_DIM))
    u2, y_b, y_c = _in_proj(h, row(g_mix), w_in.astype(BF16), row(g_sgu), w_sp, b_sp, k, v)
    y2 = _ssm(u2, *_ssm_params(lam_re, lam_im, log_dt, b_re, b_im, c_re, c_im, d_skip), n_seq=bsz)

    pad = LANES - N_GROUPS - N_EXPERTS
    w_rt = jnp.concatenate([w_group, w_router, jnp.zeros((d, pad), F32)], axis=1)
    w_rt_hi = w_rt.astype(BF16)
    w_rt = jnp.concatenate([w_rt_hi, (w_rt - w_rt_hi.astype(F32)).astype(BF16)], axis=1)
    b_rt =jnp.concatenate([b_group, b_router, jnp.zeros((pad,), F32)]).reshape(1, LANES)
    h2, xnp, rt, rtt, cnt = _merge_route(
        h.reshape(t, d), y2, y_b.reshape(t, -1), y_c.reshape(t, -1), row(g_mix),
        w_gate.astype(BF16), row(b_gate), w_glu.astype(BF16), row(b_glu),
        w_branch.astype(BF16), w_out.astype(BF16), row(g_ffn), w_rt, b_rt)

    assert BM & (BM - 1) == 0, "block padding arithmetic assumes a power-of-two block"
    counts = cnt[0, ROUTE_LANE0:ROUTE_LANE0 + N_EXPERTS].astype(jnp.int32)
    nb = (t * TOP_K) // BM + N_EXPERTS
    assert nb < 256, "slot_rows keeps block counts in bf16 matmul operands"
    nslots = nb * BM
    dest_p = _slot_rows(rtt, cnt, nslots).reshape(TOP_K, SC_SPLIT * t)
    buf = _dispatch_rows(xnp.reshape(SC_SPLIT * t, SC_ROW), dest_p[0], dest_p[1], SC_SPLIT * nslots)
    yb = _experts(counts, buf.reshape(SC_SPLIT, nslots, SC_ROW), w1, w3, w2)
    g = _gather_rows(yb.reshape(SC_SPLIT * nslots, SC_ROW), dest_p.reshape(-1))
    out = _combine(h2, g.reshape(TOP_K * SC_SPLIT, t, SC_ROW), rt, row(g_out))
    return out.reshape(bsz, s, d)


def kernel(x, mem, g_mix, g_mem, w_in, w_gate, b_gate, lam_re, lam_im, log_dt, b_re, b_im, c_re,
           c_im, d_skip, w_glu, b_glu, g_sgu, w_spatial, b_spatial, w_kv, w_branch, w_out, g_ffn,
           w_group, b_group, w_router, b_router, w1, w3, w2, g_final):
    assert g_mix.shape[0] == 1, "single-layer stack"
    return _layer(x, mem, g_mix[0], g_mem[0], w_in[0], w_gate[0], b_gate[0], lam_re[0], lam_im[0],
                  log_dt[0], b_re[0], b_im[0], c_re[0], c_im[0], d_skip[0], w_glu[0], b_glu[0],
                  g_sgu[0], w_spatial[0], b_spatial[0], w_kv[0], w_branch[0], w_out[0], g_ffn[0],
                  w_group[0], b_group[0], w_router[0], b_router[0], w1[0], w3[0], w2[0], g_final)
```

```python
import functools
import math

import jax
import jax.numpy as jnp
from jax import lax
from jax.experimental import pallas as pl
from jax.experimental.pallas import tpu as pltpu
from jax.experimental.pallas import tpu_sc as plsc

F32 = jnp.float32
BF16 = jnp.bfloat16

EPS = 1e-6
D_MODEL = 1024
SSM_WIDTH = 512
SSM_GROUP = 16
SSM_GROUPS = 32
SSM_STATE = 64
SSM_CHUNK = 16
SSM_GROUPS_PER_STEP = 1
SGU_WIDTH = 512
SGU_HEADS = 4
SGU_HEAD_DIM = 128
CHUNK = 128
XA_HEADS = 4
XA_HEAD_DIM = 128
N_GROUPS = 8
EXPERTS_PER_GROUP = 8
N_EXPERTS = 64
TOP_K = 2
D_FF = 512
LANES = 128
ROUTE_LANE0 = N_GROUPS

TM_IN = 1024
TM_MERGE = 512
MERGE_COLS = 256
TM_OUT = 1024
BM = 512
BM_STEP = 128
WEIGHT_SLOTS = 3
SC_WINDOW = 128
SC_ROW = 256
SC_SPLIT = (D_MODEL // 2) // SC_ROW
SLOT_LANES = 2048
VMEM_LIMIT = 56 * 1024 * 1024


def _rms(x, g):
    return x * lax.rsqrt(jnp.mean(x * x, axis=-1, keepdims=True) + EPS) * g


def _sigmoid(x):
    return 0.5 * (1.0 + jnp.tanh(0.5 * x))


def _gelu(x):
    c = math.sqrt(2.0 / math.pi)
    return 0.5 * x * (1.0 + jnp.tanh(c * (x + 0.044715 * (x * x * x))))


def _dot(a, b):
    return jnp.dot(a, b, preferred_element_type=F32)


_NT = (((1,), (1,)), ((), ()))


def _pack_bf16_pair(x):
    n = x.shape[1] // 2
    lo = lax.bitcast_convert_type(x[:, :n].astype(BF16).astype(F32), jnp.uint32)
    hi = lax.bitcast_convert_type(x[:, n:].astype(BF16).astype(F32), jnp.uint32)
    return hi | (lo >> 16)


def _unpack_bf16_pair(p):
    lo = lax.bitcast_convert_type(p << 16, F32)
    hi = lax.bitcast_convert_type(p & jnp.uint32(0xFFFF0000), F32)
    return jnp.concatenate([lo, hi], axis=1)


GROUPS_PER_TILE = LANES // SSM_GROUP
POS_PER_TILE = LANES // SSM_GROUP


def _slot_masks(rows):
    lane = lax.broadcasted_iota(jnp.int32, (rows, LANES), 1)
    return [(lane >= i * SSM_GROUP) & (lane < (i + 1) * SSM_GROUP) for i in range(LANES // SSM_GROUP)]


def _tokens_to_chunks(tok_ref, out_ref):
    tm = tok_ref.shape[1]
    nc = tm // SSM_CHUNK
    masks = _slot_masks(nc)
    for k in range(SSM_WIDTH // LANES):
        for j in range(SSM_CHUNK // POS_PER_TILE):
            src = [tok_ref[k, pl.ds(j * POS_PER_TILE + p, nc, stride=SSM_CHUNK), :]
                   for p in range(POS_PER_TILE)]
            for gi in range(GROUPS_PER_TILE):
                acc = None
                for p in range(POS_PER_TILE):
                    shift = ((p - gi) * SSM_GROUP) % LANES
                    r = pltpu.roll(src[p], shift, 1) if shift else src[p]
                    acc = r if acc is None else jnp.where(masks[p], r, acc)
                out_ref[k * GROUPS_PER_TILE + gi, :, pl.ds(j * LANES, LANES)] = acc.astype(out_ref.dtype)


def _chunks_to_tokens(chunk_ref, tok_ref):
    tm = tok_ref.shape[1]
    nc = tm // SSM_CHUNK
    masks = _slot_masks(nc)
    for k in range(SSM_WIDTH // LANES):
        for j in range(SSM_CHUNK // POS_PER_TILE):
            src = [chunk_ref[k * GROUPS_PER_TILE + gi, :, pl.ds(j * LANES, LANES)].astype(F32)
                   for gi in range(GROUPS_PER_TILE)]
            for p in range(POS_PER_TILE):
                acc = None
                for gi in range(GROUPS_PER_TILE):
                    shift = ((gi - p) * SSM_GROUP) % LANES
                    r = pltpu.roll(src[gi], shift, 1) if shift else src[gi]
                    acc = r if acc is None else jnp.where(masks[gi], r, acc)
                tok_ref[k, pl.ds(j * POS_PER_TILE + p, nc, stride=SSM_CHUNK), :] = acc


def _in_body(x_ref, gmix_ref, win_ref, gsgu_ref, wsp_ref, bsp_ref, mem_ref, gmem_ref, wkv_ref,
             u2_ref, yb_ref, yc_ref, tok_ref, k_ref, v_ref):
    @pl.when(pl.program_id(1) == 0)
    def _():
        kv = _dot(_rms(mem_ref[0], gmem_ref[...]).astype(BF16), wkv_ref[...].astype(BF16))
        k_ref[...] = kv[:, :XA_HEADS * XA_HEAD_DIM].astype(BF16)
        v_ref[...] = kv[:, XA_HEADS * XA_HEAD_DIM:].astype(BF16)

    n = _rms(x_ref[0], gmix_ref[...]).astype(BF16)
    proj = _dot(n, win_ref[...])
    for k in range(SSM_WIDTH // LANES):
        tok_ref[k] = proj[:, k * LANES:(k + 1) * LANES]
    _tokens_to_chunks(tok_ref, u2_ref)

    u = _gelu(proj[:, SSM_WIDTH:SSM_WIDTH + SGU_WIDTH])
    v = _gelu(proj[:, SSM_WIDTH + SGU_WIDTH:SSM_WIDTH + 2 * SGU_WIDTH])
    v = _rms(v, gsgu_ref[...]).astype(BF16)
    tm = u.shape[0]
    rows = []
    for c in range(tm // CHUNK):
        vc = v[c * CHUNK:(c + 1) * CHUNK]
        heads = []
        for h in range(SGU_HEADS):
            sl = slice(h * SGU_HEAD_DIM, (h + 1) * SGU_HEAD_DIM)
            heads.append(_dot(wsp_ref[h], vc[:, sl]) + bsp_ref[h])
        rows.append(jnp.concatenate(heads, axis=1))
    sv = jnp.concatenate(rows, axis=0)
    yb_ref[0] = (u * sv).astype(BF16)

    q = proj[:, SSM_WIDTH + 2 * SGU_WIDTH:].astype(BF16)
    kk = k_ref[...]
    vv = v_ref[...]
    outs = []
    for h in range(XA_HEADS):
        sl = slice(h * XA_HEAD_DIM, (h + 1) * XA_HEAD_DIM)
        s = lax.dot_general(q[:, sl], kk[:, sl], (((1,), (1,)), ((), ())),
                            preferred_element_type=F32) * (XA_HEAD_DIM ** -0.5)
        e = jnp.exp(s - jnp.max(s, axis=-1, keepdims=True))
        l = jnp.sum(e, axis=-1, keepdims=True)
        outs.append(_dot(e.astype(BF16), vv[:, sl]) / l)
    yc_ref[0] = jnp.concatenate(outs, axis=1).astype(BF16)


def _in_proj(x, g_mix, w_in, g_sgu, w_sp, b_sp, mem, g_mem, w_kv):
    b, s, d = x.shape
    m = mem.shape[1]
    xa = XA_HEADS * XA_HEAD_DIM
    const2 = lambda i, j: (0, 0)
    const3 = lambda i, j: (0, 0, 0)
    tok = lambda i, j: (i, j, 0)
    per_b = lambda i, j: (i, 0, 0)
    out = jax.ShapeDtypeStruct((b, s, SSM_WIDTH), BF16)
    nc = TM_IN // SSM_CHUNK
    tiles = s // TM_IN
    u2 = jax.ShapeDtypeStruct((SSM_GROUPS, b * s // SSM_CHUNK, SSM_CHUNK * SSM_GROUP), BF16)
    return pl.pallas_call(
        _in_body,
        grid=(b, s // TM_IN),
        in_specs=[pl.BlockSpec((1, TM_IN, d), tok),
                  pl.BlockSpec((1, d), const2),
                  pl.BlockSpec(w_in.shape, const2),
                  pl.BlockSpec((1, SGU_WIDTH), const2),
                  pl.BlockSpec(w_sp.shape, const3),
                  pl.BlockSpec(b_sp.shape, const3),
                  pl.BlockSpec((1, m, d), per_b),
                  pl.BlockSpec((1, d), const2),
                  pl.BlockSpec(w_kv.shape, const2, pipeline_mode=pl.Buffered(1))],
        out_specs=[pl.BlockSpec((SSM_GROUPS, nc, SSM_CHUNK * SSM_GROUP), lambda i, j: (0, i * tiles + j, 0)),
                   pl.BlockSpec((1, TM_IN, SSM_WIDTH), tok),
                   pl.BlockSpec((1, TM_IN, SSM_WIDTH), tok)],
        out_shape=[u2, out, out],
        scratch_shapes=[pltpu.VMEM((SSM_WIDTH // LANES, TM_IN, LANES), F32),
                        pltpu.VMEM((m, xa), BF16),
                        pltpu.VMEM((m, xa), BF16)],
        compiler_params=pltpu.CompilerParams(dimension_semantics=("arbitrary", "arbitrary"),
                                             vmem_limit_bytes=VMEM_LIMIT),
        name="in_proj",
    )(x, g_mix, w_in, g_sgu, w_sp, b_sp, mem, g_mem, w_kv)


def _alternate(*stages):
    live = list(stages)
    while live:
        live = [s for s in live if next(s, True) is None]


def _ssm_params(lam_re, lam_im, log_dt, b_re, b_im, c_re, c_im, d_skip):
    g, p = lam_re.shape
    dup = lambda a: jnp.concatenate([a, a], axis=-1)
    lam = jnp.stack([dup(lam_re), dup(lam_im), jnp.broadcast_to(log_dt[:, None], (g, 2 * p))], axis=1)
    brt = b_re.transpose(0, 2, 1)
    bit = b_im.transpose(0, 2, 1)
    cat = lambda a, b: jnp.concatenate([a, b], axis=-1)
    bc = jnp.stack([cat(brt, bit), cat(bit, brt), cat(c_re, -c_im), cat(-c_im, -c_re)], axis=1)
    d2 = jnp.tile(d_skip.reshape(g, 1, SSM_GROUP), (1, 1, SSM_CHUNK))
    return lam, bc, d2


def _ssm_operators(lam_ref, bc_ref, ccat_ref, n_ref, m_ref):
    lam_re = lam_ref[0:1, :]
    lam_im = lam_ref[1:2, :]
    dt = jnp.exp(lam_ref[2:3, :])
    ar = lam_re * dt
    ai = lam_im * dt
    lane = lax.broadcasted_iota(jnp.int32, (1, LANES), 1)
    sgn = jnp.where(lane >= SSM_STATE, 1.0, -1.0)

    def powers(j):
        mag = jnp.exp(ar * j)
        ph = ai * j
        return mag * jnp.cos(ph), mag * jnp.sin(ph)

    pos = lax.broadcasted_iota(jnp.int32, (SSM_CHUNK, 1), 0).astype(F32)
    p_re, p_im = powers(pos)
    r_re, r_im = powers((SSM_CHUNK - 1) - pos)
    one_re, one_im = powers(jnp.ones((1, 1), F32))
    q_re = p_re * one_re - p_im * one_im
    q_im = p_re * one_im + p_im * one_re
    step = lax.shift_left(jnp.full((8, 1), SSM_CHUNK, jnp.int32),
                          lax.broadcasted_iota(jnp.int32, (8, 1), 0)).astype(F32)
    s_re, s_im = powers(step)

    den = lam_re * lam_re + lam_im * lam_im
    f_re = ((one_re - 1.0) * lam_re + one_im * lam_im) / den
    f_im = (one_im * lam_re - (one_re - 1.0) * lam_im) / den
    b1, b2, ca, cb = bc_ref[0], bc_ref[1], bc_ref[2], bc_ref[3]
    bb1 = f_re * b1 + (sgn * f_im) * b2
    bb2 = f_re * b2 - (sgn * f_im) * b1
    r_ims = sgn * r_im
    for s in range(SSM_CHUNK):
        blk = pl.ds(s * SSM_GROUP, SSM_GROUP)
        ccat_ref[blk, :] = ca * p_re[s:s + 1, :] + cb * p_im[s:s + 1, :]
        m_ref[blk, :] = (ca * q_re[s:s + 1, :] + cb * q_im[s:s + 1, :]).astype(m_ref.dtype)
        n_ref[blk, :] = (bb1 * r_re[s:s + 1, :] + bb2 * r_ims[s:s + 1, :]).astype(n_ref.dtype)
    return bb1, s_re, sgn * s_im


def _ssm_body(u_ref, lam_ref, bc_ref, d2_ref, y_ref, toep_ref, ccat_ref, n_ref, m_ref, *, n_seq):
    for g in range(u_ref.shape[0]):
        _ssm_group(u_ref.at[g], lam_ref.at[g], bc_ref.at[g], d2_ref.at[g], y_ref.at[g],
                   toep_ref.at[g], ccat_ref.at[g], n_ref.at[g], m_ref.at[g], n_seq)


def _ssm_group(u_ref, lam_ref, bc_ref, d2_ref, y_ref, toep_ref, ccat_ref, n_ref, m_ref, n_seq):
    bcat, lr, li = _ssm_operators(lam_ref, bc_ref, ccat_ref, n_ref, m_ref)
    kern = lax.dot_general(bcat, ccat_ref[...], _NT, precision=lax.Precision.HIGHEST,
                           preferred_element_type=F32)
    col = lax.broadcasted_iota(jnp.int32, kern.shape, 1)
    for s in range(SSM_CHUNK):
        shifted = pltpu.roll(kern, s * SSM_GROUP, 1) if s else kern
        toep_ref[s * SSM_GROUP:(s + 1) * SSM_GROUP, :] = jnp.where(
            col >= s * SSM_GROUP, shifted, 0.0).astype(BF16)

    u = u_ref[...]
    rows = u.shape[0]
    per = rows // n_seq
    y = _dot(u, toep_ref[...])
    st = _dot(u, n_ref[...])
    row = lax.broadcasted_iota(jnp.int32, (per, LANES), 0)
    prev = []
    for b in range(n_seq):
        x = st[b * per:(b + 1) * per]
        k = 0
        while (1 << k) < per:
            d = 1 << k
            sh = jnp.where(row >= d, pltpu.roll(x, d, 0), 0.0)
            x = x + sh * lr[k:k + 1, :] + pltpu.roll(sh, SSM_STATE, 1) * li[k:k + 1, :]
            k += 1
        prev.append(jnp.where(row >= 1, pltpu.roll(x, 1, 0), 0.0))
    xp = jnp.concatenate(prev, axis=0).astype(BF16)
    y = y + lax.dot_general(xp, m_ref[...], _NT, preferred_element_type=F32) + d2_ref[...] * u.astype(F32)
    y_ref[...] = _gelu(y).astype(BF16)


def _ssm(u2, lam, bc, d2, n_seq):
    g, rows, w = u2.shape
    assert rows // n_seq <= 1 << 8, "lam_bar^(16*2^k) is prepared for 8 scan steps"
    gs = SSM_GROUPS_PER_STEP
    blk = lambda a: pl.BlockSpec((gs,) + a.shape[1:], lambda i: (i,) + (0,) * (a.ndim - 1))
    return pl.pallas_call(
        functools.partial(_ssm_body, n_seq=n_seq),
        grid=(g // gs,),
        in_specs=[blk(u2), blk(lam), blk(bc), blk(d2)],
        out_specs=blk(u2),
        out_shape=jax.ShapeDtypeStruct(u2.shape, BF16),
        scratch_shapes=[pltpu.VMEM((gs, w, w), BF16),
                        pltpu.VMEM((gs, w, 2 * SSM_STATE), F32),
                        pltpu.VMEM((gs, w, 2 * SSM_STATE), BF16),
                        pltpu.VMEM((gs, w, 2 * SSM_STATE), BF16)],
        compiler_params=pltpu.CompilerParams(dimension_semantics=("arbitrary",),
                                             vmem_limit_bytes=VMEM_LIMIT),
        name="ssm",
    )(u2, lam, bc, d2)


def _mix_tile(x_ref, y2_ref, yb_ref, yc_ref, gmix_ref, wgate_ref, bgate_ref, wglu_ref, bglu_ref,
              wbr_ref, wout_ref, tok_ref, h_ref, hkeep_ref):
    x = x_ref[...]
    n = _rms(x, gmix_ref[...]).astype(BF16)

    def gated(b, c, y):
        cols = pl.ds(b * D_MODEL + c * MERGE_COLS, MERGE_COLS)
        gate = _sigmoid(_dot(n, wgate_ref[:, cols].astype(BF16)) + bgate_ref[:, cols])
        return gate * _dot(y, wbr_ref[b, :, pl.ds(c * MERGE_COLS, MERGE_COLS)].astype(BF16))

    n_blocks = D_MODEL // MERGE_COLS
    yb = yb_ref[...]
    yc = yc_ref[...]
    head = gated(1, 0, yb) + gated(2, 0, yc)
    yield
    _chunks_to_tokens(y2_ref, tok_ref)
    ys = jnp.concatenate([tok_ref[k] for k in range(SSM_WIDTH // LANES)], axis=1).astype(BF16)
    glu = _dot(ys, wglu_ref[...].astype(BF16)) + bglu_ref[...]
    ya = (glu[:, :SSM_WIDTH] * _sigmoid(glu[:, SSM_WIDTH:])).astype(BF16)
    merged = [(head + gated(0, 0, ya)).astype(BF16)]
    for c in range(1, n_blocks):
        yield
        merged.append((gated(0, c, ya) + gated(1, c, yb) + gated(2, c, yc)).astype(BF16))
    yield
    h = x + _dot(jnp.concatenate(merged, axis=1), wout_ref[...].astype(BF16))
    h_ref[...] = h
    hkeep_ref[...] = h


def _route_tile(hkeep_ref, gffn_ref, wrt_ref, brt_ref, xnp_ref, rt_ref, rtt_ref, cnt_ref, carry_ref):
    h = hkeep_ref[...]
    xn = _rms(h, gffn_ref[...])
    packed = _pack_bf16_pair(xn)
    for j in range(SC_SPLIT):
        xnp_ref[j] = packed[:, j * SC_ROW:(j + 1) * SC_ROW]
    yield

    x_hi = xn.astype(BF16)
    x_lo = (xn - x_hi.astype(F32)).astype(BF16)
    head = _dot(x_hi, wrt_ref[...])
    logits = (head[:, :LANES] + head[:, LANES:] + _dot(x_lo, wrt_ref[:, :LANES])) + brt_ref[...]
    yield
    tm = logits.shape[0]
    lane_i = lax.broadcasted_iota(jnp.int32, (tm, LANES), 1)
    lane = lane_i.astype(F32)
    neg = jnp.float32(-3.0e38)
    big = jnp.float32(LANES)
    gmask = lane_i < N_GROUPS
    gl = jnp.where(gmask, logits, neg)
    gmax = jnp.max(gl, axis=-1, keepdims=True)
    gidx = jnp.min(jnp.where(gl == gmax, lane, big), axis=-1, keepdims=True)
    gsum = jnp.sum(jnp.where(gmask, jnp.exp(gl - gmax), 0.0), axis=-1, keepdims=True)
    g_w = 1.0 / gsum
    e_lane = lane_i - ROUTE_LANE0
    lane_group = (e_lane >> 3).astype(F32)
    emask = (e_lane >= 0) & (e_lane < N_EXPERTS) & (lane_group == gidx)
    el = jnp.where(emask, logits, neg)
    m1 = jnp.max(el, axis=-1, keepdims=True)
    i1 = jnp.min(jnp.where(el == m1, lane, big), axis=-1, keepdims=True)
    el2 = jnp.where(lane == i1, neg, el)
    m2 = jnp.max(el2, axis=-1, keepdims=True)
    i2 = jnp.min(jnp.where(el2 == m2, lane, big), axis=-1, keepdims=True)
    t = jnp.exp(m2 - m1)
    w1 = g_w / (1.0 + t)
    w2 = g_w * t / (1.0 + t)
    yield

    sel1 = lane == i1
    sel2 = lane == i2
    onehot = jnp.where(sel1 | sel2, 1.0, 0.0)
    r_i = lax.broadcasted_iota(jnp.int32, (tm, tm), 0)
    c_i = lax.broadcasted_iota(jnp.int32, (tm, tm), 1)
    stril = jnp.where(c_i < r_i, 1.0, 0.0).astype(BF16)
    cum = _dot(stril, onehot.astype(BF16)) + carry_ref[0:1, :]
    rank1 = jnp.sum(jnp.where(sel1, cum, 0.0), axis=-1, keepdims=True)
    rank2 = jnp.sum(jnp.where(sel2, cum, 0.0), axis=-1, keepdims=True)
    carry_ref[...] = carry_ref[...] + jnp.sum(onehot, axis=0, keepdims=True)
    cnt_ref[...] = carry_ref[...]
    yield

    cols = (i1 - ROUTE_LANE0, i2 - ROUTE_LANE0, rank1, rank2, w1, w2)
    rt = jnp.zeros((tm, LANES), F32)
    for c, val in enumerate(cols):
        rt = jnp.where(lane_i == c, val, rt)
    rt_ref[...] = rt
    rtt_ref[...] = rt.T[:8]


def _merge_body(x_ref, y2_ref, yb_ref, yc_ref, gmix_ref, wgate_ref, bgate_ref, wglu_ref, bglu_ref,
                wbr_ref, wout_ref, gffn_ref, wrt_ref, brt_ref,
                h_ref, xnp_ref, rt_ref, rtt_ref, cnt_ref, carry_ref, tok_ref, hkeep_ref):
    i = pl.program_id(0)
    last = pl.num_programs(0) - 1
    cur = hkeep_ref.at[i % 2]
    prev = hkeep_ref.at[(i + 1) % 2]

    def mix():
        return _mix_tile(x_ref, y2_ref, yb_ref, yc_ref, gmix_ref, wgate_ref, bgate_ref, wglu_ref,
                         bglu_ref, wbr_ref, wout_ref, tok_ref, h_ref, cur)

    def route():
        return _route_tile(prev, gffn_ref, wrt_ref, brt_ref, xnp_ref, rt_ref, rtt_ref, cnt_ref, carry_ref)

    @pl.when(i == 0)
    def _():
        carry_ref[...] = jnp.zeros_like(carry_ref)
        _alternate(mix())

    @pl.when((i > 0) & (i < last))
    def _():
        _alternate(route(), mix())

    @pl.when(i == last)
    def _():
        _alternate(route())


def _merge_route(x, y2, yb, yc, g_mix, w_gate, b_gate, w_glu, b_glu, w_br, w_out, g_ffn, w_rt, b_rt):
    t, d = x.shape
    tm = TM_MERGE
    tiles = t // tm
    mixed = lambda i: jnp.minimum(i, tiles - 1)
    routed = lambda i: jnp.maximum(i - 1, 0)
    c2 = lambda i: (0, 0)
    c3 = lambda i: (0, 0, 0)
    full = lambda a: pl.BlockSpec(a.shape, c2 if a.ndim == 2 else c3, pipeline_mode=pl.Buffered(1))
    return pl.pallas_call(
        _merge_body,
        grid=(tiles + 1,),
        in_specs=[pl.BlockSpec((tm, d), lambda i: (mixed(i), 0)),
                  pl.BlockSpec((SSM_GROUPS, tm // SSM_CHUNK, SSM_CHUNK * SSM_GROUP),
                               lambda i: (0, mixed(i), 0)),
                  pl.BlockSpec((tm, SSM_WIDTH), lambda i: (mixed(i), 0)),
                  pl.BlockSpec((tm, SSM_WIDTH), lambda i: (mixed(i), 0)),
                  full(g_mix), full(w_gate), full(b_gate), full(w_glu), full(b_glu),
                  full(w_br), full(w_out), full(g_ffn), full(w_rt), full(b_rt)],
        out_specs=[pl.BlockSpec((tm, d), lambda i: (mixed(i), 0)),
                   pl.BlockSpec((SC_SPLIT, tm, SC_ROW), lambda i: (0, routed(i), 0)),
                   pl.BlockSpec((tm, LANES), lambda i: (routed(i), 0)),
                   pl.BlockSpec((8, tm), lambda i: (0, routed(i))),
                   pl.BlockSpec((8, LANES), c2)],
        out_shape=[jax.ShapeDtypeStruct((t, d), F32),
                   jax.ShapeDtypeStruct((SC_SPLIT, t, SC_ROW), jnp.uint32),
                   jax.ShapeDtypeStruct((t, LANES), F32),
                   jax.ShapeDtypeStruct((8, t), F32),
                   jax.ShapeDtypeStruct((8, LANES), F32)],
        scratch_shapes=[pltpu.VMEM((8, LANES), F32),
                        pltpu.VMEM((SSM_WIDTH // LANES, tm, LANES), F32),
                        pltpu.VMEM((2, tm, d), F32)],
        compiler_params=pltpu.CompilerParams(dimension_semantics=("arbitrary",),
                                             vmem_limit_bytes=VMEM_LIMIT),
        name="merge_route",
    )(x, y2, yb, yc, g_mix, w_gate, b_gate, w_glu, b_glu, w_br, w_out, g_ffn, w_rt, b_rt)


def _slot_body(rtt_ref, cnt_ref, out_ref, *, nslots):
    tl = rtt_ref.shape[1]
    blocks = jnp.ceil(cnt_ref[...] * (1.0 / BM))
    k_i = lax.broadcasted_iota(jnp.int32, (LANES, LANES), 0)
    l_i = lax.broadcasted_iota(jnp.int32, (LANES, LANES), 1)
    before = jnp.where(k_i < l_i, 1.0, 0.0).astype(BF16)
    first_blk = _dot(blocks.astype(BF16), before).astype(BF16)
    lane_of = lax.broadcasted_iota(jnp.int32, (LANES, tl), 0).astype(F32) - ROUTE_LANE0
    for k in range(TOP_K):
        onehot = jnp.where(rtt_ref[k:k + 1, :] == lane_of, 1.0, 0.0).astype(BF16)
        start = _dot(first_blk, onehot) * BM
        slot = (start[0:1, :] + rtt_ref[TOP_K + k:TOP_K + k + 1, :]).astype(jnp.int32)
        for j in range(SC_SPLIT):
            out_ref[k * SC_SPLIT + j:k * SC_SPLIT + j + 1, :] = slot + j * nslots


def _slot_rows(rtt, cnt, nslots):
    t = rtt.shape[1]
    tl = SLOT_LANES
    return pl.pallas_call(
        functools.partial(_slot_body, nslots=nslots),
        grid=(t // tl,),
        in_specs=[pl.BlockSpec((8, tl), lambda i: (0, i)),
                  pl.BlockSpec((8, LANES), lambda i: (0, 0))],
        out_specs=pl.BlockSpec((TOP_K * SC_SPLIT, tl), lambda i: (0, i)),
        out_shape=jax.ShapeDtypeStruct((TOP_K * SC_SPLIT, t), jnp.int32),
        compiler_params=pltpu.CompilerParams(dimension_semantics=("arbitrary",)),
        name="slot_rows",
    )(rtt, cnt)


def _plan_blocks(counts_ref, blk_expert, blk_valid, blk_first, blk_run, run_expert):
    nb = blk_expert.shape[0]

    def per_expert(e, carry):
        cursor, run = carry
        count = counts_ref[e]
        n_blk = (count + (BM - 1)) // BM

        def per_block(b, cur):
            blk_expert[cur] = e
            blk_valid[cur] = jnp.minimum(count - b * BM, BM)
            blk_first[cur] = (b == 0).astype(jnp.int32)
            blk_run[cur] = run
            return cur + 1

        run_expert[run] = e
        return lax.fori_loop(0, n_blk, per_block, cursor), run + (n_blk > 0).astype(jnp.int32)

    cursor, runs = lax.fori_loop(0, counts_ref.shape[0], per_expert, (jnp.int32(0), jnp.int32(0)))
    run_expert[runs] = -1
    run_expert[runs + 1] = -1

    def empty(j, carry):
        blk_expert[j] = 0
        blk_valid[j] = 0
        blk_first[j] = 0
        blk_run[j] = runs - 1
        return carry

    lax.fori_loop(cursor, nb, empty, 0)


def _expert_body(counts_ref, buf_ref, w1_hbm, w3_hbm, w2_hbm, out_ref, w1_buf, w3_buf, w2_buf, sem,
                 blk_expert, blk_valid, blk_first, blk_run, run_expert):
    i = pl.program_id(0)

    @pl.when(i == 0)
    def _():
        _plan_blocks(counts_ref, blk_expert, blk_valid, blk_first, blk_run, run_expert)

    expert = blk_expert[i]
    valid = blk_valid[i]
    run = blk_run[i]
    slot = run % WEIGHT_SLOTS
    ahead1 = run_expert[run + 1]
    ahead2 = run_expert[run + 2]

    def weight_copies(e, s):
        return (pltpu.make_async_copy(w1_hbm.at[e], w1_buf.at[s], sem.at[s, 0]),
                pltpu.make_async_copy(w3_hbm.at[e], w3_buf.at[s], sem.at[s, 1]),
                pltpu.make_async_copy(w2_hbm.at[e], w2_buf.at[s], sem.at[s, 2]))

    @pl.when(i == 0)
    def _():
        for c in weight_copies(expert, slot):
            c.start()

        @pl.when(ahead1 >= 0)
        def _():
            for c in weight_copies(ahead1, (slot + 1) % WEIGHT_SLOTS):
                c.start()

    @pl.when(blk_first[i] == 1)
    def _():
        for c in weight_copies(expert, slot):
            c.wait()

        @pl.when(ahead2 >= 0)
        def _():
            for c in weight_copies(ahead2, (slot + 2) % WEIGHT_SLOTS):
                c.start()

    def mlp(rows):
        x = _unpack_bf16_pair(jnp.concatenate([buf_ref[j, :rows, :] for j in range(SC_SPLIT)], axis=1))
        row = lax.broadcasted_iota(jnp.int32, x.shape, 0)
        x = jnp.where(row < valid, x, 0.0).astype(BF16)
        h1 = _dot(x, w1_buf[slot].astype(BF16))
        h3 = _dot(x, w3_buf[slot].astype(BF16))
        a = (h1 * _sigmoid(h1) * h3).astype(BF16)
        packed = _pack_bf16_pair(_dot(a, w2_buf[slot].astype(BF16)))
        for j in range(SC_SPLIT):
            out_ref[j, :rows, :] = packed[:, j * SC_ROW:(j + 1) * SC_ROW]
            if rows < BM:
                out_ref[j, rows:, :] = jnp.zeros((BM - rows, SC_ROW), out_ref.dtype)

    for rows in range(BM_STEP, BM + 1, BM_STEP):
        @pl.when((valid > rows - BM_STEP) & (valid <= rows))
        def _():
            mlp(rows)

    @pl.when(valid <= 0)
    def _():
        out_ref[...] = jnp.zeros_like(out_ref)


def _experts(counts, buf, w1, w3, w2):
    _, nslots, _ = buf.shape
    nb = nslots // BM
    rows = pl.BlockSpec((SC_SPLIT, BM, SC_ROW), lambda i, counts: (0, i, 0))
    hbm = pl.BlockSpec(memory_space=pl.ANY)
    table = pltpu.SMEM((nb,), jnp.int32)
    grid_spec = pltpu.PrefetchScalarGridSpec(
        num_scalar_prefetch=1,
        grid=(nb,),
        in_specs=[rows, hbm, hbm, hbm],
        out_specs=rows,
        scratch_shapes=[pltpu.VMEM((WEIGHT_SLOTS,) + w1.shape[1:], w1.dtype),
                        pltpu.VMEM((WEIGHT_SLOTS,) + w3.shape[1:], w3.dtype),
                        pltpu.VMEM((WEIGHT_SLOTS,) + w2.shape[1:], w2.dtype),
                        pltpu.SemaphoreType.DMA((WEIGHT_SLOTS, 3)),
                        table, table, table, table,
                        pltpu.SMEM((counts.shape[0] + 2,), jnp.int32)],
    )
    return pl.pallas_call(
        _expert_body,
        grid_spec=grid_spec,
        out_shape=jax.ShapeDtypeStruct(buf.shape, jnp.uint32),
        compiler_params=pltpu.CompilerParams(dimension_semantics=("arbitrary",),
                                             vmem_limit_bytes=VMEM_LIMIT),
        name="experts",
    )(counts, buf, w1, w3, w2)


def _sc_mesh():
    return plsc.VectorSubcoreMesh(core_axis_name="core", subcore_axis_name="subcore")


def _dispatch_rows(rows, dest0, dest1, nslots):
    t, w = rows.shape
    win = SC_WINDOW
    idx_spec = pl.BlockSpec((1, win), lambda i: (0, i))

    @functools.partial(pl.kernel, mesh=_sc_mesh(), scratch_types=[],
                       out_type=jax.ShapeDtypeStruct((nslots, w), rows.dtype), name="dispatch_rows")
    def run(rows_hbm, i0_hbm, i1_hbm, out_hbm):
        def body(rows_vmem, i0_vmem, i1_vmem):
            pltpu.sync_copy(rows_vmem, out_hbm.at[i0_vmem.at[0]])
            pltpu.sync_copy(rows_vmem, out_hbm.at[i1_vmem.at[0]])

        pltpu.emit_pipeline(
            body, grid=(t // win,),
            in_specs=[pl.BlockSpec((win, w), lambda i: (i, 0)), idx_spec, idx_spec],
            out_specs=[],
            core_axis_name=("core", "subcore"),
            dimension_semantics=(pltpu.PARALLEL,),
        )(rows_hbm, i0_hbm, i1_hbm)

    return run(rows, dest0.reshape(1, t), dest1.reshape(1, t))


def _gather_rows(table, idx):
    n = idx.shape[0]
    w = table.shape[1]
    win = SC_WINDOW

    @functools.partial(pl.kernel, mesh=_sc_mesh(), scratch_types=[],
                       out_type=jax.ShapeDtypeStruct((n, w), table.dtype), name="gather_rows")
    def run(table_hbm, i_hbm, out_hbm):
        def body(i_vmem, out_vmem):
            pltpu.sync_copy(table_hbm.at[i_vmem.at[0]], out_vmem)

        pltpu.emit_pipeline(
            body, grid=(n // win,),
            in_specs=[pl.BlockSpec((1, win), lambda i: (0, i))],
            out_specs=[pl.BlockSpec((win, w), lambda i: (i, 0))],
            core_axis_name=("core", "subcore"),
            dimension_semantics=(pltpu.PARALLEL,),
        )(i_hbm, out_hbm)

    return run(table, idx.reshape(1, n))


def _combine_body(h_ref, g_ref, rt_ref, gfin_ref, out_ref):
    rt = rt_ref[...]
    y = h_ref[...]
    for k in range(TOP_K):
        rows = jnp.concatenate([g_ref[k * SC_SPLIT + j] for j in range(SC_SPLIT)], axis=1)
        y = y + rt[:, 4 + k:5 + k] * _unpack_bf16_pair(rows)
    out_ref[...] = _rms(y, gfin_ref[...])


def _combine(h, g, rt, g_final):
    t, d = h.shape
    tm = TM_OUT
    tok = lambda i: (i, 0)
    return pl.pallas_call(
        _combine_body,
        grid=(t // tm,),
        in_specs=[pl.BlockSpec((tm, d), tok),
                  pl.BlockSpec((TOP_K * SC_SPLIT, tm, SC_ROW), lambda i: (0, i, 0)),
                  pl.BlockSpec((tm, LANES), tok),
                  pl.BlockSpec((1, d), lambda i: (0, 0))],
        out_specs=pl.BlockSpec((tm, d), tok),
        out_shape=jax.ShapeDtypeStruct((t, d), F32),
        compiler_params=pltpu.CompilerParams(dimension_semantics=("arbitrary",),
                                             vmem_limit_bytes=VMEM_LIMIT),
        name="combine",
    )(h, g, rt, g_final)


def _layer(h, mem, g_mix, g_mem, w_in, w_gate, b_gate, lam_re, lam_im, log_dt, b_re, b_im,
           c_re, c_im, d_skip, w_glu, b_glu, g_sgu, w_spatial, b_spatial, w_kv, w_branch,
           w_out, g_ffn, w_group, b_group, w_router, b_router, w1, w3, w2, g_out):
    bsz, s, d = h.shape
    t = bsz * s
    row = lambda a: a.reshape(1, -1)

    tril = jnp.tril(jnp.ones((CHUNK, CHUNK), dtype=bool))
    w_sp = jnp.where(tril, w_spatial, 0.0).astype(BF16)
    b_sp = jnp.broadcast_to(b_spatial[:, :, None], (SGU_HEADS, CHUNK, SGU_HEAD_DIM))
    u2, y_b, y_c = _in_proj(h, row(g_mix), w_in.astype(BF16), row(g_sgu), w_sp, b_sp,
                            mem, row(g_mem), w_kv)
    y2 = _ssm(u2, *_ssm_params(lam_re, lam_im, log_dt, b_re, b_im, c_re, c_im, d_skip), n_seq=bsz)

    pad = LANES - N_GROUPS - N_EXPERTS
    w_rt = jnp.concatenate([w_group, w_router, jnp.zeros((d, pad), F32)], axis=1)
    w_rt_hi = w_rt.astype(BF16)
    w_rt = jnp.concatenate([w_rt_hi, (w_rt - w_rt_hi.astype(F32)).astype(BF16)], axis=1)
    b_rt =jnp.concatenate([b_group, b_router, jnp.zeros((pad,), F32)]).reshape(1, LANES)
    h2, xnp, rt, rtt, cnt = _merge_route(
        h.reshape(t, d), y2, y_b.reshape(t, -1), y_c.reshape(t, -1), row(g_mix),
        w_gate, row(b_gate), w_glu, row(b_glu), w_branch, w_out, row(g_ffn), w_rt, b_rt)

    assert BM & (BM - 1) == 0, "block padding arithmetic assumes a power-of-two block"
    counts = cnt[0, ROUTE_LANE0:ROUTE_LANE0 + N_EXPERTS].astype(jnp.int32)
    nb = (t * TOP_K) // BM + N_EXPERTS
    assert nb < 256, "slot_rows keeps block counts in bf16 matmul operands"
    nslots = nb * BM
    dest_p = _slot_rows(rtt, cnt, nslots).reshape(TOP_K, SC_SPLIT * t)
    buf = _dispatch_rows(xnp.reshape(SC_SPLIT * t, SC_ROW), dest_p[0], dest_p[1], SC_SPLIT * nslots)
    yb = _experts(counts, buf.reshape(SC_SPLIT, nslots, SC_ROW), w1, w3, w2)
    g = _gather_rows(yb.reshape(SC_SPLIT * nslots, SC_ROW), dest_p.reshape(-1))
    out = _combine(h2, g.reshape(TOP_K * SC_SPLIT, t, SC_ROW), rt, row(g_out))
    return out.reshape(bsz, s, d)


def kernel(x, mem, g_mix, g_mem, w_in, w_gate, b_gate, lam_re, lam_im, log_dt, b_re, b_im, c_re,
           c_im, d_skip, w_glu, b_glu, g_sgu, w_spatial, b_spatial, w_kv, w_branch, w_out, g_ffn,
           w_group, b_group, w_router, b_router, w1, w3, w2, g_final):
    assert g_mix.shape[0] == 1, "single-layer stack"
    return _layer(x, mem, g_mix[0], g_mem[0], w_in[0], w_gate[0], b_gate[0], lam_re[0], lam_im[0],
                  log_dt[0], b_re[0], b_im[0], c_re[0], c_im[0], d_skip[0], w_glu[0], b_glu[0],
                  g_sgu[0], w_spatial[0], b_spatial[0], w_kv[0], w_branch[0], w_out[0], g_ffn[0],
                  w_group[0], b_group[0], w_router[0], b_router[0], w1[0], w3[0], w2[0], g_final)
```

```python
import functools
import math

import jax
import jax.numpy as jnp
from jax import lax
from jax.experimental import pallas as pl
from jax.experimental.pallas import tpu as pltpu
from jax.experimental.pallas import tpu_sc as plsc

F32 = jnp.float32
BF16 = jnp.bfloat16

EPS = 1e-6
D_MODEL = 1024
SSM_WIDTH = 512
SSM_GROUP = 16
SSM_GROUPS = 32
SSM_STATE = 64
SSM_CHUNK = 16
SSM_GROUPS_PER_STEP = 1
SGU_WIDTH = 512
SGU_HEADS = 4
SGU_HEAD_DIM = 128
CHUNK = 128
XA_HEADS = 4
XA_HEAD_DIM = 128
N_GROUPS = 8
EXPERTS_PER_GROUP = 8
N_EXPERTS = 64
TOP_K = 2
D_FF = 512
LANES = 128
ROUTE_LANE0 = N_GROUPS

TM_IN = 1024
TM_MERGE = 512
MERGE_COLS = 256
TM_OUT = 1024
BM = 512
BM_STEP = 128
WEIGHT_SLOTS = 3
SC_WINDOW = 128
SC_ROW = 256
SC_SPLIT = (D_MODEL // 2) // SC_ROW
SLOT_LANES = 2048
VMEM_LIMIT = 56 * 1024 * 1024


def _rms(x, g):
    return x * lax.rsqrt(jnp.mean(x * x, axis=-1, keepdims=True) + EPS) * g


def _sigmoid(x):
    return 0.5 * (1.0 + jnp.tanh(0.5 * x))


def _gelu(x):
    c = math.sqrt(2.0 / math.pi)
    return 0.5 * x * (1.0 + jnp.tanh(c * (x + 0.044715 * (x * x * x))))


def _dot(a, b):
    return jnp.dot(a, b, preferred_element_type=F32)


_NT = (((1,), (1,)), ((), ()))


def _pack_bf16_pair(x):
    n = x.shape[1] // 2
    lo = lax.bitcast_convert_type(x[:, :n].astype(BF16).astype(F32), jnp.uint32)
    hi = lax.bitcast_convert_type(x[:, n:].astype(BF16).astype(F32), jnp.uint32)
    return hi | (lo >> 16)


def _unpack_bf16_pair(p):
    lo = lax.bitcast_convert_type(p << 16, F32)
    hi = lax.bitcast_convert_type(p & jnp.uint32(0xFFFF0000), F32)
    return jnp.concatenate([lo, hi], axis=1)


GROUPS_PER_TILE = LANES // SSM_GROUP
POS_PER_TILE = LANES // SSM_GROUP


def _slot_masks(rows):
    lane = lax.broadcasted_iota(jnp.int32, (rows, LANES), 1)
    return [(lane >= i * SSM_GROUP) & (lane < (i + 1) * SSM_GROUP) for i in range(LANES // SSM_GROUP)]


def _tokens_to_chunks(tok_ref, out_ref):
    tm = tok_ref.shape[1]
    nc = tm // SSM_CHUNK
    masks = _slot_masks(nc)
    for k in range(SSM_WIDTH // LANES):
        for j in range(SSM_CHUNK // POS_PER_TILE):
            src = [tok_ref[k, pl.ds(j * POS_PER_TILE + p, nc, stride=SSM_CHUNK), :]
                   for p in range(POS_PER_TILE)]
            for gi in range(GROUPS_PER_TILE):
                acc = None
                for p in range(POS_PER_TILE):
                    shift = ((p - gi) * SSM_GROUP) % LANES
                    r = pltpu.roll(src[p], shift, 1) if shift else src[p]
                    acc = r if acc is None else jnp.where(masks[p], r, acc)
                out_ref[k * GROUPS_PER_TILE + gi, :, pl.ds(j * LANES, LANES)] = acc.astype(out_ref.dtype)


def _chunks_to_tokens(chunk_ref, tok_ref):
    tm = tok_ref.shape[1]
    nc = tm // SSM_CHUNK
    masks = _slot_masks(nc)
    for k in range(SSM_WIDTH // LANES):
        for j in range(SSM_CHUNK // POS_PER_TILE):
            src = [chunk_ref[k * GROUPS_PER_TILE + gi, :, pl.ds(j * LANES, LANES)].astype(F32)
                   for gi in range(GROUPS_PER_TILE)]
            for p in range(POS_PER_TILE):
                acc = None
                for gi in range(GROUPS_PER_TILE):
                    shift = ((gi - p) * SSM_GROUP) % LANES
                    r = pltpu.roll(src[gi], shift, 1) if shift else src[gi]
                    acc = r if acc is None else jnp.where(masks[gi], r, acc)
                tok_ref[k, pl.ds(j * POS_PER_TILE + p, nc, stride=SSM_CHUNK), :] = acc


def _in_body(x_ref, gmix_ref, win_ref, gsgu_ref, wsp_ref, bsp_ref, mem_ref, gmem_ref, wkv_ref,
             u2_ref, yb_ref, yc_ref, tok_ref, k_ref, v_ref):
    @pl.when(pl.program_id(1) == 0)
    def _():
        kv = _dot(_rms(mem_ref[0], gmem_ref[...]).astype(BF16), wkv_ref[...].astype(BF16))
        k_ref[...] = kv[:, :XA_HEADS * XA_HEAD_DIM].astype(BF16)
        v_ref[...] = kv[:, XA_HEADS * XA_HEAD_DIM:].astype(BF16)

    n = _rms(x_ref[0], gmix_ref[...]).astype(BF16)
    proj = _dot(n, win_ref[...])
    for k in range(SSM_WIDTH // LANES):
        tok_ref[k] = proj[:, k * LANES:(k + 1) * LANES]
    _tokens_to_chunks(tok_ref, u2_ref)

    u = _gelu(proj[:, SSM_WIDTH:SSM_WIDTH + SGU_WIDTH])
    v = _gelu(proj[:, SSM_WIDTH + SGU_WIDTH:SSM_WIDTH + 2 * SGU_WIDTH])
    v = _rms(v, gsgu_ref[...]).astype(BF16)
    tm = u.shape[0]
    rows = []
    for c in range(tm // CHUNK):
        vc = v[c * CHUNK:(c + 1) * CHUNK]
        heads = []
        for h in range(SGU_HEADS):
            sl = slice(h * SGU_HEAD_DIM, (h + 1) * SGU_HEAD_DIM)
            heads.append(_dot(wsp_ref[h], vc[:, sl]) + bsp_ref[h])
        rows.append(jnp.concatenate(heads, axis=1))
    sv = jnp.concatenate(rows, axis=0)
    yb_ref[0] = (u * sv).astype(BF16)

    q = proj[:, SSM_WIDTH + 2 * SGU_WIDTH:].astype(BF16)
    kk = k_ref[...]
    vv = v_ref[...]
    outs = []
    for h in range(XA_HEADS):
        sl = slice(h * XA_HEAD_DIM, (h + 1) * XA_HEAD_DIM)
        s = lax.dot_general(q[:, sl], kk[:, sl], (((1,), (1,)), ((), ())),
                            preferred_element_type=F32) * (XA_HEAD_DIM ** -0.5)
        e = jnp.exp(s - jnp.max(s, axis=-1, keepdims=True))
        l = jnp.sum(e, axis=-1, keepdims=True)
        outs.append(_dot(e.astype(BF16), vv[:, sl]) / l)
    yc_ref[0] = jnp.concatenate(outs, axis=1).astype(BF16)


def _in_proj(x, g_mix, w_in, g_sgu, w_sp, b_sp, mem, g_mem, w_kv):
    b, s, d = x.shape
    m = mem.shape[1]
    xa = XA_HEADS * XA_HEAD_DIM
    const2 = lambda i, j: (0, 0)
    const3 = lambda i, j: (0, 0, 0)
    tok = lambda i, j: (i, j, 0)
    per_b = lambda i, j: (i, 0, 0)
    out = jax.ShapeDtypeStruct((b, s, SSM_WIDTH), BF16)
    nc = TM_IN // SSM_CHUNK
    tiles = s // TM_IN
    u2 = jax.ShapeDtypeStruct((SSM_GROUPS, b * s // SSM_CHUNK, SSM_CHUNK * SSM_GROUP), BF16)
    return pl.pallas_call(
        _in_body,
        grid=(b, s // TM_IN),
        in_specs=[pl.BlockSpec((1, TM_IN, d), tok),
                  pl.BlockSpec((1, d), const2),
                  pl.BlockSpec(w_in.shape, const2),
                  pl.BlockSpec((1, SGU_WIDTH), const2),
                  pl.BlockSpec(w_sp.shape, const3),
                  pl.BlockSpec(b_sp.shape, const3),
                  pl.BlockSpec((1, m, d), per_b),
                  pl.BlockSpec((1, d), const2),
                  pl.BlockSpec(w_kv.shape, const2, pipeline_mode=pl.Buffered(1))],
        out_specs=[pl.BlockSpec((SSM_GROUPS, nc, SSM_CHUNK * SSM_GROUP), lambda i, j: (0, i * tiles + j, 0)),
                   pl.BlockSpec((1, TM_IN, SSM_WIDTH), tok),
                   pl.BlockSpec((1, TM_IN, SSM_WIDTH), tok)],
        out_shape=[u2, out, out],
        scratch_shapes=[pltpu.VMEM((SSM_WIDTH // LANES, TM_IN, LANES), F32),
                        pltpu.VMEM((m, xa), BF16),
                        pltpu.VMEM((m, xa), BF16)],
        compiler_params=pltpu.CompilerParams(dimension_semantics=("arbitrary", "arbitrary"),
                                             vmem_limit_bytes=VMEM_LIMIT),
        name="in_proj",
    )(x, g_mix, w_in, g_sgu, w_sp, b_sp, mem, g_mem, w_kv)


def _alternate(*stages):
    live = list(stages)
    while live:
        live = [s for s in live if next(s, True) is None]


def _ssm_params(lam_re, lam_im, log_dt, b_re, b_im, c_re, c_im, d_skip):
    g, p = lam_re.shape
    dup = lambda a: jnp.concatenate([a, a], axis=-1)
    lam = jnp.stack([dup(lam_re), dup(lam_im), jnp.broadcast_to(log_dt[:, None], (g, 2 * p))], axis=1)
    brt = b_re.transpose(0, 2, 1)
    bit = b_im.transpose(0, 2, 1)
    cat = lambda a, b: jnp.concatenate([a, b], axis=-1)
    bc = jnp.stack([cat(brt, bit), cat(bit, brt), cat(c_re, -c_im), cat(-c_im, -c_re)], axis=1)
    d2 = jnp.tile(d_skip.reshape(g, 1, SSM_GROUP), (1, 1, SSM_CHUNK))
    return lam, bc, d2


def _ssm_operators(lam_ref, bc_ref, ccat_ref, n_ref, m_ref):
    lam_re = lam_ref[0:1, :]
    lam_im = lam_ref[1:2, :]
    dt = jnp.exp(lam_ref[2:3, :])
    ar = lam_re * dt
    ai = lam_im * dt
    lane = lax.broadcasted_iota(jnp.int32, (1, LANES), 1)
    sgn = jnp.where(lane >= SSM_STATE, 1.0, -1.0)

    def powers(j):
        mag = jnp.exp(ar * j)
        ph = ai * j
        return mag * jnp.cos(ph), mag * jnp.sin(ph)

    pos = lax.broadcasted_iota(jnp.int32, (SSM_CHUNK, 1), 0).astype(F32)
    p_re, p_im = powers(pos)
    r_re, r_im = powers((SSM_CHUNK - 1) - pos)
    one_re, one_im = powers(jnp.ones((1, 1), F32))
    q_re = p_re * one_re - p_im * one_im
    q_im = p_re * one_im + p_im * one_re
    step = lax.shift_left(jnp.full((8, 1), SSM_CHUNK, jnp.int32),
                          lax.broadcasted_iota(jnp.int32, (8, 1), 0)).astype(F32)
    s_re, s_im = powers(step)

    den = lam_re * lam_re + lam_im * lam_im
    f_re = ((one_re - 1.0) * lam_re + one_im * lam_im) / den
    f_im = (one_im * lam_re - (one_re - 1.0) * lam_im) / den
    b1, b2, ca, cb = bc_ref[0], bc_ref[1], bc_ref[2], bc_ref[3]
    bb1 = f_re * b1 + (sgn * f_im) * b2
    bb2 = f_re * b2 - (sgn * f_im) * b1
    r_ims = sgn * r_im
    for s in range(SSM_CHUNK):
        blk = pl.ds(s * SSM_GROUP, SSM_GROUP)
        ccat_ref[blk, :] = ca * p_re[s:s + 1, :] + cb * p_im[s:s + 1, :]
        m_ref[blk, :] = (ca * q_re[s:s + 1, :] + cb * q_im[s:s + 1, :]).astype(m_ref.dtype)
        n_ref[blk, :] = (bb1 * r_re[s:s + 1, :] + bb2 * r_ims[s:s + 1, :]).astype(n_ref.dtype)
    return bb1, s_re, sgn * s_im


def _ssm_body(u_ref, lam_ref, bc_ref, d2_ref, y_ref, toep_ref, ccat_ref, n_ref, m_ref, *, n_seq):
    for g in range(u_ref.shape[0]):
        _ssm_group(u_ref.at[g], lam_ref.at[g], bc_ref.at[g], d2_ref.at[g], y_ref.at[g],
                   toep_ref.at[g], ccat_ref.at[g], n_ref.at[g], m_ref.at[g], n_seq)


def _ssm_group(u_ref, lam_ref, bc_ref, d2_ref, y_ref, toep_ref, ccat_ref, n_ref, m_ref, n_seq):
    bcat, lr, li = _ssm_operators(lam_ref, bc_ref, ccat_ref, n_ref, m_ref)
    kern = lax.dot_general(bcat, ccat_ref[...], _NT, precision=lax.Precision.HIGHEST,
                           preferred_element_type=F32)
    col = lax.broadcasted_iota(jnp.int32, kern.shape, 1)
    for s in range(SSM_CHUNK):
        shifted = pltpu.roll(kern, s * SSM_GROUP, 1) if s else kern
        toep_ref[s * SSM_GROUP:(s + 1) * SSM_GROUP, :] = jnp.where(
            col >= s * SSM_GROUP, shifted, 0.0).astype(BF16)

    u = u_ref[...]
    rows = u.shape[0]
    per = rows // n_seq
    y = _dot(u, toep_ref[...])
    st = _dot(u, n_ref[...])
    row = lax.broadcasted_iota(jnp.int32, (per, LANES), 0)
    prev = []
    for b in range(n_seq):
        x = st[b * per:(b + 1) * per]
        k = 0
        while (1 << k) < per:
            d = 1 << k
            sh = jnp.where(row >= d, pltpu.roll(x, d, 0), 0.0)
            x = x + sh * lr[k:k + 1, :] + pltpu.roll(sh, SSM_STATE, 1) * li[k:k + 1, :]
            k += 1
        prev.append(jnp.where(row >= 1, pltpu.roll(x, 1, 0), 0.0))
    xp = jnp.concatenate(prev, axis=0).astype(BF16)
    y = y + lax.dot_general(xp, m_ref[...], _NT, preferred_element_type=F32) + d2_ref[...] * u.astype(F32)
    y_ref[...] = _gelu(y).astype(BF16)


def _ssm(u2, lam, bc, d2, n_seq):
    g, rows, w = u2.shape
    assert rows // n_seq <= 1 << 8, "lam_bar^(16*2^k) is prepared for 8 scan steps"
    gs = SSM_GROUPS_PER_STEP
    blk = lambda a: pl.BlockSpec((gs,) + a.shape[1:], lambda i: (i,) + (0,) * (a.ndim - 1))
    return pl.pallas_call(
        functools.partial(_ssm_body, n_seq=n_seq),
        grid=(g // gs,),
        in_specs=[blk(u2), blk(lam), blk(bc), blk(d2)],
        out_specs=blk(u2),
        out_shape=jax.ShapeDtypeStruct(u2.shape, BF16),
        scratch_shapes=[pltpu.VMEM((gs, w, w), BF16),
                        pltpu.VMEM((gs, w, 2 * SSM_STATE), F32),
                        pltpu.VMEM((gs, w, 2 * SSM_STATE), BF16),
                        pltpu.VMEM((gs, w, 2 * SSM_STATE), BF16)],
        compiler_params=pltpu.CompilerParams(dimension_semantics=("arbitrary",),
                                             vmem_limit_bytes=VMEM_LIMIT),
        name="ssm",
    )(u2, lam, bc, d2)


def _mix_tile(x_ref, y2_ref, yb_ref, yc_ref, gmix_ref, wgate_ref, bgate_ref, wglu_ref, bglu_ref,
              wbr_ref, wout_ref, tok_ref, h_ref, hkeep_ref):
    x = x_ref[...]
    n = _rms(x, gmix_ref[...]).astype(BF16)

    def gated(b, c, y):
        cols = pl.ds(b * D_MODEL + c * MERGE_COLS, MERGE_COLS)
        gate = _sigmoid(_dot(n, wgate_ref[:, cols].astype(BF16)) + bgate_ref[:, cols])
        return gate * _dot(y, wbr_ref[b, :, pl.ds(c * MERGE_COLS, MERGE_COLS)].astype(BF16))

    n_blocks = D_MODEL // MERGE_COLS
    yb = yb_ref[...]
    yc = yc_ref[...]
    head = gated(1, 0, yb) + gated(2, 0, yc)
    yield
    _chunks_to_tokens(y2_ref, tok_ref)
    ys = jnp.concatenate([tok_ref[k] for k in range(SSM_WIDTH // LANES)], axis=1).astype(BF16)
    glu = _dot(ys, wglu_ref[...].astype(BF16)) + bglu_ref[...]
    ya = (glu[:, :SSM_WIDTH] * _sigmoid(glu[:, SSM_WIDTH:])).astype(BF16)
    merged = [(head + gated(0, 0, ya)).astype(BF16)]
    for c in range(1, n_blocks):
        yield
        merged.append((gated(0, c, ya) + gated(1, c, yb) + gated(2, c, yc)).astype(BF16))
    yield
    h = x + _dot(jnp.concatenate(merged, axis=1), wout_ref[...].astype(BF16))
    h_ref[...] = h
    hkeep_ref[...] = h


def _route_tile(hkeep_ref, gffn_ref, wrt_ref, brt_ref, xnp_ref, rt_ref, rtt_ref, cnt_ref, carry_ref):
    h = hkeep_ref[...]
    xn = _rms(h, gffn_ref[...])
    packed = _pack_bf16_pair(xn)
    for j in range(SC_SPLIT):
        xnp_ref[j] = packed[:, j * SC_ROW:(j + 1) * SC_ROW]
    yield

    x_hi = xn.astype(BF16)
    x_lo = (xn - x_hi.astype(F32)).astype(BF16)
    head = _dot(x_hi, wrt_ref[...])
    logits = (head[:, :LANES] + head[:, LANES:] + _dot(x_lo, wrt_ref[:, :LANES])) + brt_ref[...]
    yield
    tm = logits.shape[0]
    lane_i = lax.broadcasted_iota(jnp.int32, (tm, LANES), 1)
    lane = lane_i.astype(F32)
    neg = jnp.float32(-3.0e38)
    big = jnp.float32(LANES)
    gmask = lane_i < N_GROUPS
    gl = jnp.where(gmask, logits, neg)
    gmax = jnp.max(gl, axis=-1, keepdims=True)
    gidx = jnp.min(jnp.where(gl == gmax, lane, big), axis=-1, keepdims=True)
    gsum = jnp.sum(jnp.where(gmask, jnp.exp(gl - gmax), 0.0), axis=-1, keepdims=True)
    g_w = 1.0 / gsum
    e_lane = lane_i - ROUTE_LANE0
    lane_group = (e_lane >> 3).astype(F32)
    emask = (e_lane >= 0) & (e_lane < N_EXPERTS) & (lane_group == gidx)
    el = jnp.where(emask, logits, neg)
    m1 = jnp.max(el, axis=-1, keepdims=True)
    i1 = jnp.min(jnp.where(el == m1, lane, big), axis=-1, keepdims=True)
    el2 = jnp.where(lane == i1, neg, el)
    m2 = jnp.max(el2, axis=-1, keepdims=True)
    i2 = jnp.min(jnp.where(el2 == m2, lane, big), axis=-1, keepdims=True)
    t = jnp.exp(m2 - m1)
    w1 = g_w / (1.0 + t)
    w2 = g_w * t / (1.0 + t)
    yield

    sel1 = lane == i1
    sel2 = lane == i2
    onehot = jnp.where(sel1 | sel2, 1.0, 0.0)
    r_i = lax.broadcasted_iota(jnp.int32, (tm, tm), 0)
    c_i = lax.broadcasted_iota(jnp.int32, (tm, tm), 1)
    stril = jnp.where(c_i < r_i, 1.0, 0.0).astype(BF16)
    cum = _dot(stril, onehot.astype(BF16)) + carry_ref[0:1, :]
    rank1 = jnp.sum(jnp.where(sel1, cum, 0.0), axis=-1, keepdims=True)
    rank2 = jnp.sum(jnp.where(sel2, cum, 0.0), axis=-1, keepdims=True)
    carry_ref[...] = carry_ref[...] + jnp.sum(onehot, axis=0, keepdims=True)
    cnt_ref[...] = carry_ref[...]
    yield

    cols = (i1 - ROUTE_LANE0, i2 - ROUTE_LANE0, rank1, rank2, w1, w2)
    rt = jnp.zeros((tm, LANES), F32)
    for c, val in enumerate(cols):
        rt = jnp.where(lane_i == c, val, rt)
    rt_ref[...] = rt
    rtt_ref[...] = rt.T[:8]


def _merge_body(x_ref, y2_ref, yb_ref, yc_ref, gmix_ref, wgate_ref, bgate_ref, wglu_ref, bglu_ref,
                wbr_ref, wout_ref, gffn_ref, wrt_ref, brt_ref,
                h_ref, xnp_ref, rt_ref, rtt_ref, cnt_ref, carry_ref, tok_ref, hkeep_ref):
    i = pl.program_id(0)
    last = pl.num_programs(0) - 1
    cur = hkeep_ref.at[i % 2]
    prev = hkeep_ref.at[(i + 1) % 2]

    def mix():
        return _mix_tile(x_ref, y2_ref, yb_ref, yc_ref, gmix_ref, wgate_ref, bgate_ref, wglu_ref,
                         bglu_ref, wbr_ref, wout_ref, tok_ref, h_ref, cur)

    def route():
        return _route_tile(prev, gffn_ref, wrt_ref, brt_ref, xnp_ref, rt_ref, rtt_ref, cnt_ref, carry_ref)

    @pl.when(i == 0)
    def _():
        carry_ref[...] = jnp.zeros_like(carry_ref)
        _alternate(mix())

    @pl.when((i > 0) & (i < last))
    def _():
        _alternate(route(), mix())

    @pl.when(i == last)
    def _():
        _alternate(route())


def _merge_route(x, y2, yb, yc, g_mix, w_gate, b_gate, w_glu, b_glu, w_br, w_out, g_ffn, w_rt, b_rt):
    t, d = x.shape
    tm = TM_MERGE
    tiles = t // tm
    mixed = lambda i: jnp.minimum(i, tiles - 1)
    routed = lambda i: jnp.maximum(i - 1, 0)
    c2 = lambda i: (0, 0)
    c3 = lambda i: (0, 0, 0)
    full = lambda a: pl.BlockSpec(a.shape, c2 if a.ndim == 2 else c3, pipeline_mode=pl.Buffered(1))
    return pl.pallas_call(
        _merge_body,
        grid=(tiles + 1,),
        in_specs=[pl.BlockSpec((tm, d), lambda i: (mixed(i), 0)),
                  pl.BlockSpec((SSM_GROUPS, tm // SSM_CHUNK, SSM_CHUNK * SSM_GROUP),
                               lambda i: (0, mixed(i), 0)),
                  pl.BlockSpec((tm, SSM_WIDTH), lambda i: (mixed(i), 0)),
                  pl.BlockSpec((tm, SSM_WIDTH), lambda i: (mixed(i), 0)),
                  full(g_mix), full(w_gate), full(b_gate), full(w_glu), full(b_glu),
                  full(w_br), full(w_out), full(g_ffn), full(w_rt), full(b_rt)],
        out_specs=[pl.BlockSpec((tm, d), lambda i: (mixed(i), 0)),
                   pl.BlockSpec((SC_SPLIT, tm, SC_ROW), lambda i: (0, routed(i), 0)),
                   pl.BlockSpec((tm, LANES), lambda i: (routed(i), 0)),
                   pl.BlockSpec((8, tm), lambda i: (0, routed(i))),
                   pl.BlockSpec((8, LANES), c2)],
        out_shape=[jax.ShapeDtypeStruct((t, d), F32),
                   jax.ShapeDtypeStruct((SC_SPLIT, t, SC_ROW), jnp.uint32),
                   jax.ShapeDtypeStruct((t, LANES), F32),
                   jax.ShapeDtypeStruct((8, t), F32),
                   jax.ShapeDtypeStruct((8, LANES), F32)],
        scratch_shapes=[pltpu.VMEM((8, LANES), F32),
                        pltpu.VMEM((SSM_WIDTH // LANES, tm, LANES), F32),
                        pltpu.VMEM((2, tm, d), F32)],
        compiler_params=pltpu.CompilerParams(dimension_semantics=("arbitrary",),
                                             vmem_limit_bytes=VMEM_LIMIT),
        name="merge_route",
    )(x, y2, yb, yc, g_mix, w_gate, b_gate, w_glu, b_glu, w_br, w_out, g_ffn, w_rt, b_rt)


def _slot_body(rtt_ref, cnt_ref, out_ref, *, nslots):
    tl = rtt_ref.shape[1]
    blocks = jnp.ceil(cnt_ref[...] * (1.0 / BM))
    k_i = lax.broadcasted_iota(jnp.int32, (LANES, LANES), 0)
    l_i = lax.broadcasted_iota(jnp.int32, (LANES, LANES), 1)
    before = jnp.where(k_i < l_i, 1.0, 0.0).astype(BF16)
    first_blk = _dot(blocks.astype(BF16), before).astype(BF16)
    lane_of = lax.broadcasted_iota(jnp.int32, (LANES, tl), 0).astype(F32) - ROUTE_LANE0
    for k in range(TOP_K):
        onehot = jnp.where(rtt_ref[k:k + 1, :] == lane_of, 1.0, 0.0).astype(BF16)
        start = _dot(first_blk, onehot) * BM
        slot = (start[0:1, :] + rtt_ref[TOP_K + k:TOP_K + k + 1, :]).astype(jnp.int32)
        for j in range(SC_SPLIT):
            out_ref[k * SC_SPLIT + j:k * SC_SPLIT + j + 1, :] = slot + j * nslots


def _slot_rows(rtt, cnt, nslots):
    t = rtt.shape[1]
    tl = SLOT_LANES
    return pl.pallas_call(
        functools.partial(_slot_body, nslots=nslots),
        grid=(t // tl,),
        in_specs=[pl.BlockSpec((8, tl), lambda i: (0, i)),
                  pl.BlockSpec((8, LANES), lambda i: (0, 0))],
        out_specs=pl.BlockSpec((TOP_K * SC_SPLIT, tl), lambda i: (0, i)),
        out_shape=jax.ShapeDtypeStruct((TOP_K * SC_SPLIT, t), jnp.int32),
        compiler_params=pltpu.CompilerParams(dimension_semantics=("arbitrary",)),
        name="slot_rows",
    )(rtt, cnt)


def _plan_blocks(counts_ref, blk_expert, blk_valid, blk_first, blk_run, run_expert):
    nb = blk_expert.shape[0]

    def per_expert(e, carry):
        cursor, run = carry
        count = counts_ref[e]
        n_blk = (count + (BM - 1)) // BM

        def per_block(b, cur):
            blk_expert[cur] = e
            blk_valid[cur] = jnp.minimum(count - b * BM, BM)
            blk_first[cur] = (b == 0).astype(jnp.int32)
            blk_run[cur] = run
            return cur + 1

        run_expert[run] = e
        return lax.fori_loop(0, n_blk, per_block, cursor), run + (n_blk > 0).astype(jnp.int32)

    cursor, runs = lax.fori_loop(0, counts_ref.shape[0], per_expert, (jnp.int32(0), jnp.int32(0)))
    run_expert[runs] = -1
    run_expert[runs + 1] = -1

    def empty(j, carry):
        blk_expert[j] = 0
        blk_valid[j] = 0
        blk_first[j] = 0
        blk_run[j] = runs - 1
        return carry

    lax.fori_loop(cursor, nb, empty, 0)


def _expert_body(counts_ref, buf_ref, w1_hbm, w3_hbm, w2_hbm, out_ref, w1_buf, w3_buf, w2_buf, sem,
                 blk_expert, blk_valid, blk_first, blk_run, run_expert):
    i = pl.program_id(0)

    @pl.when(i == 0)
    def _():
        _plan_blocks(counts_ref, blk_expert, blk_valid, blk_first, blk_run, run_expert)

    expert = blk_expert[i]
    valid = blk_valid[i]
    run = blk_run[i]
    slot = run % WEIGHT_SLOTS
    ahead1 = run_expert[run + 1]
    ahead2 = run_expert[run + 2]

    def weight_copies(e, s):
        return (pltpu.make_async_copy(w1_hbm.at[e], w1_buf.at[s], sem.at[s, 0]),
                pltpu.make_async_copy(w3_hbm.at[e], w3_buf.at[s], sem.at[s, 1]),
                pltpu.make_async_copy(w2_hbm.at[e], w2_buf.at[s], sem.at[s, 2]))

    @pl.when(i == 0)
    def _():
        for c in weight_copies(expert, slot):
            c.start()

        @pl.when(ahead1 >= 0)
        def _():
            for c in weight_copies(ahead1, (slot + 1) % WEIGHT_SLOTS):
                c.start()

    @pl.when(blk_first[i] == 1)
    def _():
        for c in weight_copies(expert, slot):
            c.wait()

        @pl.when(ahead2 >= 0)
        def _():
            for c in weight_copies(ahead2, (slot + 2) % WEIGHT_SLOTS):
                c.start()

    def mlp(rows):
        x = _unpack_bf16_pair(jnp.concatenate([buf_ref[j, :rows, :] for j in range(SC_SPLIT)], axis=1))
        row = lax.broadcasted_iota(jnp.int32, x.shape, 0)
        x = jnp.where(row < valid, x, 0.0).astype(BF16)
        h1 = _dot(x, w1_buf[slot].astype(BF16))
        h3 = _dot(x, w3_buf[slot].astype(BF16))
        a = (h1 * _sigmoid(h1) * h3).astype(BF16)
        packed = _pack_bf16_pair(_dot(a, w2_buf[slot].astype(BF16)))
        for j in range(SC_SPLIT):
            out_ref[j, :rows, :] = packed[:, j * SC_ROW:(j + 1) * SC_ROW]
            if rows < BM:
                out_ref[j, rows:, :] = jnp.zeros((BM - rows, SC_ROW), out_ref.dtype)

    for rows in range(BM_STEP, BM + 1, BM_STEP):
        @pl.when((valid > rows - BM_STEP) & (valid <= rows))
        def _():
            mlp(rows)

    @pl.when(valid <= 0)
    def _():
        out_ref[...] = jnp.zeros_like(out_ref)


def _experts(counts, buf, w1, w3, w2):
    _, nslots, _ = buf.shape
    nb = nslots // BM
    rows = pl.BlockSpec((SC_SPLIT, BM, SC_ROW), lambda i, counts: (0, i, 0))
    hbm = pl.BlockSpec(memory_space=pl.ANY)
    table = pltpu.SMEM((nb,), jnp.int32)
    grid_spec = pltpu.PrefetchScalarGridSpec(
        num_scalar_prefetch=1,
        grid=(nb,),
        in_specs=[rows, hbm, hbm, hbm],
        out_specs=rows,
        scratch_shapes=[pltpu.VMEM((WEIGHT_SLOTS,) + w1.shape[1:], w1.dtype),
                        pltpu.VMEM((WEIGHT_SLOTS,) + w3.shape[1:], w3.dtype),
                        pltpu.VMEM((WEIGHT_SLOTS,) + w2.shape[1:], w2.dtype),
                        pltpu.SemaphoreType.DMA((WEIGHT_SLOTS, 3)),
                        table, table, table, table,
                        pltpu.SMEM((counts.shape[0] + 2,), jnp.int32)],
    )
    return pl.pallas_call(
        _expert_body,
        grid_spec=grid_spec,
        out_shape=jax.ShapeDtypeStruct(buf.shape, jnp.uint32),
        compiler_params=pltpu.CompilerParams(dimension_semantics=("arbitrary",),
                                             vmem_limit_bytes=VMEM_LIMIT),
        name="experts",
    )(counts, buf, w1, w3, w2)


def _sc_mesh():
    return plsc.VectorSubcoreMesh(core_axis_name="core", subcore_axis_name="subcore")


def _dispatch_rows(rows, dest0, dest1, nslots):
    t, w = rows.shape
    win = SC_WINDOW
    idx_spec = pl.BlockSpec((1, win), lambda i: (0, i))

    @functools.partial(pl.kernel, mesh=_sc_mesh(),
                       scratch_types=[pltpu.SemaphoreType.DMA, pltpu.SemaphoreType.DMA],
                       out_type=jax.ShapeDtypeStruct((nslots, w), rows.dtype), name="dispatch_rows")
    def run(rows_hbm, i0_hbm, i1_hbm, out_hbm, sem0, sem1):
        def body(rows_vmem, i0_vmem, i1_vmem):
            first = pltpu.make_async_copy(rows_vmem, out_hbm.at[i0_vmem.at[0]], sem0)
            second = pltpu.make_async_copy(rows_vmem, out_hbm.at[i1_vmem.at[0]], sem1)
            first.start()
            second.start()
            first.wait()
            second.wait()

        pltpu.emit_pipeline(
            body, grid=(t // win,),
            in_specs=[pl.BlockSpec((win, w), lambda i: (i, 0)), idx_spec, idx_spec],
            out_specs=[],
            core_axis_name=("core", "subcore"),
            dimension_semantics=(pltpu.PARALLEL,),
        )(rows_hbm, i0_hbm, i1_hbm)

    return run(rows, dest0.reshape(1, t), dest1.reshape(1, t))


def _gather_rows(table, idx):
    n = idx.shape[0]
    w = table.shape[1]
    win = SC_WINDOW

    @functools.partial(pl.kernel, mesh=_sc_mesh(), scratch_types=[],
                       out_type=jax.ShapeDtypeStruct((n, w), table.dtype), name="gather_rows")
    def run(table_hbm, i_hbm, out_hbm):
        def body(i_vmem, out_vmem):
            pltpu.sync_copy(table_hbm.at[i_vmem.at[0]], out_vmem)

        pltpu.emit_pipeline(
            body, grid=(n // win,),
            in_specs=[pl.BlockSpec((1, win), lambda i: (0, i))],
            out_specs=[pl.BlockSpec((win, w), lambda i: (i, 0))],
            core_axis_name=("core", "subcore"),
            dimension_semantics=(pltpu.PARALLEL,),
        )(i_hbm, out_hbm)

    return run(table, idx.reshape(1, n))


def _combine_body(h_ref, g_ref, rt_ref, gfin_ref, out_ref):
    rt = rt_ref[...]
    y = h_ref[...]
    for k in range(TOP_K):
        rows = jnp.concatenate([g_ref[k * SC_SPLIT + j] for j in range(SC_SPLIT)], axis=1)
        y = y + rt[:, 4 + k:5 + k] * _unpack_bf16_pair(rows)
    out_ref[...] = _rms(y, gfin_ref[...])


def _combine(h, g, rt, g_final):
    t, d = h.shape
    tm = TM_OUT
    tok = lambda i: (i, 0)
    return pl.pallas_call(
        _combine_body,
        grid=(t // tm,),
        in_specs=[pl.BlockSpec((tm, d), tok),
                  pl.BlockSpec((TOP_K * SC_SPLIT, tm, SC_ROW), lambda i: (0, i, 0)),
                  pl.BlockSpec((tm, LANES), tok),
                  pl.BlockSpec((1, d), lambda i: (0, 0))],
        out_specs=pl.BlockSpec((tm, d), tok),
        out_shape=jax.ShapeDtypeStruct((t, d), F32),
        compiler_params=pltpu.CompilerParams(dimension_semantics=("arbitrary",),
                                             vmem_limit_bytes=VMEM_LIMIT),
        name="combine",
    )(h, g, rt, g_final)


def _layer(h, mem, g_mix, g_mem, w_in, w_gate, b_gate, lam_re, lam_im, log_dt, b_re, b_im,
           c_re, c_im, d_skip, w_glu, b_glu, g_sgu, w_spatial, b_spatial, w_kv, w_branch,
           w_out, g_ffn, w_group, b_group, w_router, b_router, w1, w3, w2, g_out):
    bsz, s, d = h.shape
    t = bsz * s
    row = lambda a: a.reshape(1, -1)

    tril = jnp.tril(jnp.ones((CHUNK, CHUNK), dtype=bool))
    w_sp = jnp.where(tril, w_spatial, 0.0).astype(BF16)
    b_sp = jnp.broadcast_to(b_spatial[:, :, None], (SGU_HEADS, CHUNK, SGU_HEAD_DIM))
    u2, y_b, y_c = _in_proj(h, row(g_mix), w_in.astype(BF16), row(g_sgu), w_sp, b_sp,
                            mem, row(g_mem), w_kv)
    y2 = _ssm(u2, *_ssm_params(lam_re, lam_im, log_dt, b_re, b_im, c_re, c_im, d_skip), n_seq=bsz)

    pad = LANES - N_GROUPS - N_EXPERTS
    w_rt = jnp.concatenate([w_group, w_router, jnp.zeros((d, pad), F32)], axis=1)
    w_rt_hi = w_rt.astype(BF16)
    w_rt = jnp.concatenate([w_rt_hi, (w_rt - w_rt_hi.astype(F32)).astype(BF16)], axis=1)
    b_rt =jnp.concatenate([b_group, b_router, jnp.zeros((pad,), F32)]).reshape(1, LANES)
    h2, xnp, rt, rtt, cnt = _merge_route(
        h.reshape(t, d), y2, y_b.reshape(t, -1), y_c.reshape(t, -1), row(g_mix),
        w_gate, row(b_gate), w_glu, row(b_glu), w_branch, w_out, row(g_ffn), w_rt, b_rt)

    assert BM & (BM - 1) == 0, "block padding arithmetic assumes a power-of-two block"
    counts = cnt[0, ROUTE_LANE0:ROUTE_LANE0 + N_EXPERTS].astype(jnp.int32)
    nb = (t * TOP_K) // BM + N_EXPERTS
    assert nb < 256, "slot_rows keeps block counts in bf16 matmul operands"
    nslots = nb * BM
    dest_p = _slot_rows(rtt, cnt, nslots).reshape(TOP_K, SC_SPLIT * t)
    buf = _dispatch_rows(xnp.reshape(SC_SPLIT * t, SC_ROW), dest_p[0], dest_p[1], SC_SPLIT * nslots)
    yb = _experts(counts, buf.reshape(SC_SPLIT, nslots, SC_ROW), w1, w3, w2)
    g = _gather_rows(yb.reshape(SC_SPLIT * nslots, SC_ROW), dest_p.reshape(-1))
    out = _combine(h2, g.reshape(TOP_K * SC_SPLIT, t, SC_ROW), rt, row(g_out))
    return out.reshape(bsz, s, d)


def kernel(x, mem, g_mix, g_mem, w_in, w_gate, b_gate, lam_re, lam_im, log_dt, b_re, b_im, c_re,
           c_im, d_skip, w_glu, b_glu, g_sgu, w_spatial, b_spatial, w_kv, w_branch, w_out, g_ffn,
           w_group, b_group, w_router, b_router, w1, w3, w2, g_final):
    assert g_mix.shape[0] == 1, "single-layer stack"
    return _layer(x, mem, g_mix[0], g_mem[0], w_in[0], w_gate[0], b_gate[0], lam_re[0], lam_im[0],
                  log_dt[0], b_re[0], b_im[0], c_re[0], c_im[0], d_skip[0], w_glu[0], b_glu[0],
                  g_sgu[0], w_spatial[0], b_spatial[0], w_kv[0], w_branch[0], w_out[0], g_ffn[0],
                  w_group[0], b_group[0], w_router[0], b_router[0], w1[0], w3[0], w2[0], g_final)
```

```python
import functools
import math

import jax
import jax.numpy as jnp
from jax import lax
from jax.experimental import pallas as pl
from jax.experimental.pallas import tpu as pltpu
from jax.experimental.pallas import tpu_sc as plsc

F32 = jnp.float32
BF16 = jnp.bfloat16

EPS = 1e-6
D_MODEL = 1024
SSM_WIDTH = 512
SSM_GROUP = 16
SSM_GROUPS = 32
SSM_STATE = 64
SSM_CHUNK = 16
SSM_GROUPS_PER_STEP = 1
SGU_WIDTH = 512
SGU_HEADS = 4
SGU_HEAD_DIM = 128
CHUNK = 128
XA_HEADS = 4
XA_HEAD_DIM = 128
N_GROUPS = 8
EXPERTS_PER_GROUP = 8
N_EXPERTS = 64
TOP_K = 2
D_FF = 512
LANES = 128
ROUTE_LANE0 = N_GROUPS

TM_IN = 1024
TM_MERGE = 512
MERGE_COLS = 256
TM_OUT = 1024
BM = 512
BM_STEP = 128
WEIGHT_SLOTS = 3
SC_WINDOW = 128
SC_ROW = 256
SC_SPLIT = (D_MODEL // 2) // SC_ROW
SLOT_LANES = 2048
VMEM_LIMIT = 56 * 1024 * 1024


def _rms(x, g):
    return x * lax.rsqrt(jnp.mean(x * x, axis=-1, keepdims=True) + EPS) * g


def _sigmoid(x):
    return 0.5 * (1.0 + jnp.tanh(0.5 * x))


def _gelu(x):
    c = math.sqrt(2.0 / math.pi)
    return 0.5 * x * (1.0 + jnp.tanh(c * (x + 0.044715 * (x * x * x))))


def _dot(a, b):
    return jnp.dot(a, b, preferred_element_type=F32)


_NT = (((1,), (1,)), ((), ()))


def _pack_bf16_pair(x):
    n = x.shape[1] // 2
    lo = lax.bitcast_convert_type(x[:, :n].astype(BF16).astype(F32), jnp.uint32)
    hi = lax.bitcast_convert_type(x[:, n:].astype(BF16).astype(F32), jnp.uint32)
    return hi | (lo >> 16)


def _unpack_bf16_pair(p):
    lo = lax.bitcast_convert_type(p << 16, F32)
    hi = lax.bitcast_convert_type(p & jnp.uint32(0xFFFF0000), F32)
    return jnp.concatenate([lo, hi], axis=1)


GROUPS_PER_TILE = LANES // SSM_GROUP
POS_PER_TILE = LANES // SSM_GROUP


def _slot_masks(rows):
    lane = lax.broadcasted_iota(jnp.int32, (rows, LANES), 1)
    return [(lane >= i * SSM_GROUP) & (lane < (i + 1) * SSM_GROUP) for i in range(LANES // SSM_GROUP)]


def _tokens_to_chunks(tok_ref, out_ref):
    tm = tok_ref.shape[1]
    nc = tm // SSM_CHUNK
    masks = _slot_masks(nc)
    for k in range(SSM_WIDTH // LANES):
        for j in range(SSM_CHUNK // POS_PER_TILE):
            src = [tok_ref[k, pl.ds(j * POS_PER_TILE + p, nc, stride=SSM_CHUNK), :]
                   for p in range(POS_PER_TILE)]
            for gi in range(GROUPS_PER_TILE):
                acc = None
                for p in range(POS_PER_TILE):
                    shift = ((p - gi) * SSM_GROUP) % LANES
                    r = pltpu.roll(src[p], shift, 1) if shift else src[p]
                    acc = r if acc is None else jnp.where(masks[p], r, acc)
                out_ref[k * GROUPS_PER_TILE + gi, :, pl.ds(j * LANES, LANES)] = acc.astype(out_ref.dtype)


def _chunks_to_tokens(chunk_ref, tok_ref):
    tm = tok_ref.shape[1]
    nc = tm // SSM_CHUNK
    masks = _slot_masks(nc)
    for k in range(SSM_WIDTH // LANES):
        for j in range(SSM_CHUNK // POS_PER_TILE):
            src = [chunk_ref[k * GROUPS_PER_TILE + gi, :, pl.ds(j * LANES, LANES)].astype(F32)
                   for gi in range(GROUPS_PER_TILE)]
            for p in range(POS_PER_TILE):
                acc = None
                for gi in range(GROUPS_PER_TILE):
                    shift = ((gi - p) * SSM_GROUP) % LANES
                    r = pltpu.roll(src[gi], shift, 1) if shift else src[gi]
                    acc = r if acc is None else jnp.where(masks[gi], r, acc)
                tok_ref[k, pl.ds(j * POS_PER_TILE + p, nc, stride=SSM_CHUNK), :] = acc


def _in_body(x_ref, gmix_ref, win_ref, gsgu_ref, wsp_ref, bsp_ref, mem_ref, gmem_ref, wkv_ref,
             u2_ref, yb_ref, yc_ref, tok_ref, k_ref, v_ref):
    @pl.when(pl.program_id(1) == 0)
    def _():
        kv = _dot(_rms(mem_ref[0], gmem_ref[...]).astype(BF16), wkv_ref[...].astype(BF16))
        k_ref[...] = kv[:, :XA_HEADS * XA_HEAD_DIM].astype(BF16)
        v_ref[...] = kv[:, XA_HEADS * XA_HEAD_DIM:].astype(BF16)

    n = _rms(x_ref[0], gmix_ref[...]).astype(BF16)
    proj = _dot(n, win_ref[...])
    for k in range(SSM_WIDTH // LANES):
        tok_ref[k] = proj[:, k * LANES:(k + 1) * LANES]
    _tokens_to_chunks(tok_ref, u2_ref)

    u = _gelu(proj[:, SSM_WIDTH:SSM_WIDTH + SGU_WIDTH])
    v = _gelu(proj[:, SSM_WIDTH + SGU_WIDTH:SSM_WIDTH + 2 * SGU_WIDTH])
    v = _rms(v, gsgu_ref[...]).astype(BF16)
    tm = u.shape[0]
    rows = []
    for c in range(tm // CHUNK):
        vc = v[c * CHUNK:(c + 1) * CHUNK]
        heads = []
        for h in range(SGU_HEADS):
            sl = slice(h * SGU_HEAD_DIM, (h + 1) * SGU_HEAD_DIM)
            heads.append(_dot(wsp_ref[h], vc[:, sl]) + bsp_ref[h])
        rows.append(jnp.concatenate(heads, axis=1))
    sv = jnp.concatenate(rows, axis=0)
    yb_ref[0] = (u * sv).astype(BF16)

    q = proj[:, SSM_WIDTH + 2 * SGU_WIDTH:].astype(BF16)
    kk = k_ref[...]
    vv = v_ref[...]
    outs = []
    for h in range(XA_HEADS):
        sl = slice(h * XA_HEAD_DIM, (h + 1) * XA_HEAD_DIM)
        s = lax.dot_general(q[:, sl], kk[:, sl], (((1,), (1,)), ((), ())),
                            preferred_element_type=F32) * (XA_HEAD_DIM ** -0.5)
        e = jnp.exp(s - jnp.max(s, axis=-1, keepdims=True))
        l = jnp.sum(e, axis=-1, keepdims=True)
        outs.append(_dot(e.astype(BF16), vv[:, sl]) / l)
    yc_ref[0] = jnp.concatenate(outs, axis=1).astype(BF16)


def _in_proj(x, g_mix, w_in, g_sgu, w_sp, b_sp, mem, g_mem, w_kv):
    b, s, d = x.shape
    m = mem.shape[1]
    xa = XA_HEADS * XA_HEAD_DIM
    const2 = lambda i, j: (0, 0)
    const3 = lambda i, j: (0, 0, 0)
    tok = lambda i, j: (i, j, 0)
    per_b = lambda i, j: (i, 0, 0)
    out = jax.ShapeDtypeStruct((b, s, SSM_WIDTH), BF16)
    nc = TM_IN // SSM_CHUNK
    tiles = s // TM_IN
    u2 = jax.ShapeDtypeStruct((SSM_GROUPS, b * s // SSM_CHUNK, SSM_CHUNK * SSM_GROUP), BF16)
    return pl.pallas_call(
        _in_body,
        grid=(b, s // TM_IN),
        in_specs=[pl.BlockSpec((1, TM_IN, d), tok),
                  pl.BlockSpec((1, d), const2),
                  pl.BlockSpec(w_in.shape, const2),
                  pl.BlockSpec((1, SGU_WIDTH), const2),
                  pl.BlockSpec(w_sp.shape, const3),
                  pl.BlockSpec(b_sp.shape, const3),
                  pl.BlockSpec((1, m, d), per_b),
                  pl.BlockSpec((1, d), const2),
                  pl.BlockSpec(w_kv.shape, const2, pipeline_mode=pl.Buffered(1))],
        out_specs=[pl.BlockSpec((SSM_GROUPS, nc, SSM_CHUNK * SSM_GROUP), lambda i, j: (0, i * tiles + j, 0)),
                   pl.BlockSpec((1, TM_IN, SSM_WIDTH), tok),
                   pl.BlockSpec((1, TM_IN, SSM_WIDTH), tok)],
        out_shape=[u2, out, out],
        scratch_shapes=[pltpu.VMEM((SSM_WIDTH // LANES, TM_IN, LANES), F32),
                        pltpu.VMEM((m, xa), BF16),
                        pltpu.VMEM((m, xa), BF16)],
        compiler_params=pltpu.CompilerParams(dimension_semantics=("arbitrary", "arbitrary"),
                                             vmem_limit_bytes=VMEM_LIMIT),
        name="in_proj",
    )(x, g_mix, w_in, g_sgu, w_sp, b_sp, mem, g_mem, w_kv)


def _alternate(*stages):
    live = list(stages)
    while live:
        live = [s for s in live if next(s, True) is None]


def _ssm_params(lam_re, lam_im, log_dt, b_re, b_im, c_re, c_im, d_skip):
    g, p = lam_re.shape
    dup = lambda a: jnp.concatenate([a, a], axis=-1)
    lam = jnp.stack([dup(lam_re), dup(lam_im), jnp.broadcast_to(log_dt[:, None], (g, 2 * p))], axis=1)
    brt = b_re.transpose(0, 2, 1)
    bit = b_im.transpose(0, 2, 1)
    cat = lambda a, b: jnp.concatenate([a, b], axis=-1)
    bc = jnp.stack([cat(brt, bit), cat(bit, brt), cat(c_re, -c_im), cat(-c_im, -c_re)], axis=1)
    d2 = jnp.tile(d_skip.reshape(g, 1, SSM_GROUP), (1, 1, SSM_CHUNK))
    return lam, bc, d2


def _ssm_operators(lam_ref, bc_ref, ccat_ref, n_ref, m_ref):
    lam_re = lam_ref[0:1, :]
    lam_im = lam_ref[1:2, :]
    dt = jnp.exp(lam_ref[2:3, :])
    ar = lam_re * dt
    ai = lam_im * dt
    lane = lax.broadcasted_iota(jnp.int32, (1, LANES), 1)
    sgn = jnp.where(lane >= SSM_STATE, 1.0, -1.0)

    def powers(j):
        mag = jnp.exp(ar * j)
        ph = ai * j
        return mag * jnp.cos(ph), mag * jnp.sin(ph)

    pos = lax.broadcasted_iota(jnp.int32, (SSM_CHUNK, 1), 0).astype(F32)
    p_re, p_im = powers(pos)
    r_re, r_im = powers((SSM_CHUNK - 1) - pos)
    one_re, one_im = powers(jnp.ones((1, 1), F32))
    q_re = p_re * one_re - p_im * one_im
    q_im = p_re * one_im + p_im * one_re
    step = lax.shift_left(jnp.full((8, 1), SSM_CHUNK, jnp.int32),
                          lax.broadcasted_iota(jnp.int32, (8, 1), 0)).astype(F32)
    s_re, s_im = powers(step)

    den = lam_re * lam_re + lam_im * lam_im
    f_re = ((one_re - 1.0) * lam_re + one_im * lam_im) / den
    f_im = (one_im * lam_re - (one_re - 1.0) * lam_im) / den
    b1, b2, ca, cb = bc_ref[0], bc_ref[1], bc_ref[2], bc_ref[3]
    bb1 = f_re * b1 + (sgn * f_im) * b2
    bb2 = f_re * b2 - (sgn * f_im) * b1
    r_ims = sgn * r_im
    for s in range(SSM_CHUNK):
        blk = pl.ds(s * SSM_GROUP, SSM_GROUP)
        ccat_ref[blk, :] = ca * p_re[s:s + 1, :] + cb * p_im[s:s + 1, :]
        m_ref[blk, :] = (ca * q_re[s:s + 1, :] + cb * q_im[s:s + 1, :]).astype(m_ref.dtype)
        n_ref[blk, :] = (bb1 * r_re[s:s + 1, :] + bb2 * r_ims[s:s + 1, :]).astype(n_ref.dtype)
    return bb1, s_re, sgn * s_im


def _ssm_body(u_ref, lam_ref, bc_ref, d2_ref, y_ref, toep_ref, ccat_ref, n_ref, m_ref, *, n_seq):
    for g in range(u_ref.shape[0]):
        _ssm_group(u_ref.at[g], lam_ref.at[g], bc_ref.at[g], d2_ref.at[g], y_ref.at[g],
                   toep_ref.at[g], ccat_ref.at[g], n_ref.at[g], m_ref.at[g], n_seq)


def _ssm_group(u_ref, lam_ref, bc_ref, d2_ref, y_ref, toep_ref, ccat_ref, n_ref, m_ref, n_seq):
    bcat, lr, li = _ssm_operators(lam_ref, bc_ref, ccat_ref, n_ref, m_ref)
    kern = lax.dot_general(bcat, ccat_ref[...], _NT, precision=lax.Precision.HIGHEST,
                           preferred_element_type=F32)
    col = lax.broadcasted_iota(jnp.int32, kern.shape, 1)
    for s in range(SSM_CHUNK):
        shifted = pltpu.roll(kern, s * SSM_GROUP, 1) if s else kern
        toep_ref[s * SSM_GROUP:(s + 1) * SSM_GROUP, :] = jnp.where(
            col >= s * SSM_GROUP, shifted, 0.0).astype(BF16)

    u = u_ref[...]
    rows = u.shape[0]
    per = rows // n_seq
    y = _dot(u, toep_ref[...])
    st = _dot(u, n_ref[...])
    row = lax.broadcasted_iota(jnp.int32, (per, LANES), 0)
    prev = []
    for b in range(n_seq):
        x = st[b * per:(b + 1) * per]
        k = 0
        while (1 << k) < per:
            d = 1 << k
            sh = jnp.where(row >= d, pltpu.roll(x, d, 0), 0.0)
            x = x + sh * lr[k:k + 1, :] + pltpu.roll(sh, SSM_STATE, 1) * li[k:k + 1, :]
            k += 1
        prev.append(jnp.where(row >= 1, pltpu.roll(x, 1, 0), 0.0))
    xp = jnp.concatenate(prev, axis=0).astype(BF16)
    y = y + lax.dot_general(xp, m_ref[...], _NT, preferred_element_type=F32) + d2_ref[...] * u.astype(F32)
    y_ref[...] = _gelu(y).astype(BF16)


def _ssm(u2, lam, bc, d2, n_seq):
    g, rows, w = u2.shape
    assert rows // n_seq <= 1 << 8, "lam_bar^(16*2^k) is prepared for 8 scan steps"
    gs = SSM_GROUPS_PER_STEP
    blk = lambda a: pl.BlockSpec((gs,) + a.shape[1:], lambda i: (i,) + (0,) * (a.ndim - 1))
    return pl.pallas_call(
        functools.partial(_ssm_body, n_seq=n_seq),
        grid=(g // gs,),
        in_specs=[blk(u2), blk(lam), blk(bc), blk(d2)],
        out_specs=blk(u2),
        out_shape=jax.ShapeDtypeStruct(u2.shape, BF16),
        scratch_shapes=[pltpu.VMEM((gs, w, w), BF16),
                        pltpu.VMEM((gs, w, 2 * SSM_STATE), F32),
                        pltpu.VMEM((gs, w, 2 * SSM_STATE), BF16),
                        pltpu.VMEM((gs, w, 2 * SSM_STATE), BF16)],
        compiler_params=pltpu.CompilerParams(dimension_semantics=("arbitrary",),
                                             vmem_limit_bytes=VMEM_LIMIT),
        name="ssm",
    )(u2, lam, bc, d2)


def _mix_tile(x_ref, y2_ref, yb_ref, yc_ref, gmix_ref, wgate_ref, bgate_ref, wglu_ref, bglu_ref,
              wbr_ref, wout_ref, tok_ref, h_ref, hkeep_ref):
    x = x_ref[...]
    n = _rms(x, gmix_ref[...]).astype(BF16)

    def gated(b, c, y):
        cols = pl.ds(b * D_MODEL + c * MERGE_COLS, MERGE_COLS)
        gate = _sigmoid(_dot(n, wgate_ref[:, cols].astype(BF16)) + bgate_ref[:, cols])
        return gate * _dot(y, wbr_ref[b, :, pl.ds(c * MERGE_COLS, MERGE_COLS)].astype(BF16))

    n_blocks = D_MODEL // MERGE_COLS
    yb = yb_ref[...]
    yc = yc_ref[...]
    head = gated(1, 0, yb) + gated(2, 0, yc)
    yield
    _chunks_to_tokens(y2_ref, tok_ref)
    ys = jnp.concatenate([tok_ref[k] for k in range(SSM_WIDTH // LANES)], axis=1).astype(BF16)
    glu = _dot(ys, wglu_ref[...].astype(BF16)) + bglu_ref[...]
    ya = (glu[:, :SSM_WIDTH] * _sigmoid(glu[:, SSM_WIDTH:])).astype(BF16)
    merged = [(head + gated(0, 0, ya)).astype(BF16)]
    for c in range(1, n_blocks):
        yield
        acc = gated(0, c, ya)
        yield
        acc = acc + gated(1, c, yb)
        yield
        merged.append((acc + gated(2, c, yc)).astype(BF16))
    yield
    h = x + _dot(jnp.concatenate(merged, axis=1), wout_ref[...].astype(BF16))
    h_ref[...] = h
    hkeep_ref[...] = h


def _route_tile(hkeep_ref, gffn_ref, wrt_ref, brt_ref, xnp_ref, rt_ref, rtt_ref, cnt_ref, carry_ref):
    h = hkeep_ref[...]
    xn = _rms(h, gffn_ref[...])
    yield
    packed = _pack_bf16_pair(xn)
    for j in range(SC_SPLIT):
        xnp_ref[j] = packed[:, j * SC_ROW:(j + 1) * SC_ROW]
    yield

    x_hi = xn.astype(BF16)
    x_lo = (xn - x_hi.astype(F32)).astype(BF16)
    head = _dot(x_hi, wrt_ref[...])
    logits = (head[:, :LANES] + head[:, LANES:] + _dot(x_lo, wrt_ref[:, :LANES])) + brt_ref[...]
    yield
    tm = logits.shape[0]
    lane_i = lax.broadcasted_iota(jnp.int32, (tm, LANES), 1)
    lane = lane_i.astype(F32)
    neg = jnp.float32(-3.0e38)
    big = jnp.float32(LANES)
    gmask = lane_i < N_GROUPS
    gl = jnp.where(gmask, logits, neg)
    gmax = jnp.max(gl, axis=-1, keepdims=True)
    gidx = jnp.min(jnp.where(gl == gmax, lane, big), axis=-1, keepdims=True)
    gsum = jnp.sum(jnp.where(gmask, jnp.exp(gl - gmax), 0.0), axis=-1, keepdims=True)
    g_w = 1.0 / gsum
    yield
    e_lane = lane_i - ROUTE_LANE0
    lane_group = (e_lane >> 3).astype(F32)
    emask = (e_lane >= 0) & (e_lane < N_EXPERTS) & (lane_group == gidx)
    el = jnp.where(emask, logits, neg)
    m1 = jnp.max(el, axis=-1, keepdims=True)
    i1 = jnp.min(jnp.where(el == m1, lane, big), axis=-1, keepdims=True)
    yield
    el2 = jnp.where(lane == i1, neg, el)
    m2 = jnp.max(el2, axis=-1, keepdims=True)
    i2 = jnp.min(jnp.where(el2 == m2, lane, big), axis=-1, keepdims=True)
    t = jnp.exp(m2 - m1)
    w1 = g_w / (1.0 + t)
    w2 = g_w * t / (1.0 + t)
    yield

    sel1 = lane == i1
    sel2 = lane == i2
    onehot = jnp.where(sel1 | sel2, 1.0, 0.0)
    r_i = lax.broadcasted_iota(jnp.int32, (tm, tm), 0)
    c_i = lax.broadcasted_iota(jnp.int32, (tm, tm), 1)
    stril = jnp.where(c_i < r_i, 1.0, 0.0).astype(BF16)
    cum = _dot(stril, onehot.astype(BF16)) + carry_ref[0:1, :]
    rank1 = jnp.sum(jnp.where(sel1, cum, 0.0), axis=-1, keepdims=True)
    rank2 = jnp.sum(jnp.where(sel2, cum, 0.0), axis=-1, keepdims=True)
    carry_ref[...] = carry_ref[...] + jnp.sum(onehot, axis=0, keepdims=True)
    cnt_ref[...] = carry_ref[...]
    yield

    cols = (i1 - ROUTE_LANE0, i2 - ROUTE_LANE0, rank1, rank2, w1, w2)
    rt = jnp.zeros((tm, LANES), F32)
    for c, val in enumerate(cols):
        rt = jnp.where(lane_i == c, val, rt)
    rt_ref[...] = rt
    rtt_ref[...] = rt.T[:8]


def _merge_body(x_ref, y2_ref, yb_ref, yc_ref, gmix_ref, wgate_ref, bgate_ref, wglu_ref, bglu_ref,
                wbr_ref, wout_ref, gffn_ref, wrt_ref, brt_ref,
                h_ref, xnp_ref, rt_ref, rtt_ref, cnt_ref, carry_ref, tok_ref, hkeep_ref):
    i = pl.program_id(0)
    last = pl.num_programs(0) - 1
    cur = hkeep_ref.at[i % 2]
    prev = hkeep_ref.at[(i + 1) % 2]

    def mix():
        return _mix_tile(x_ref, y2_ref, yb_ref, yc_ref, gmix_ref, wgate_ref, bgate_ref, wglu_ref,
                         bglu_ref, wbr_ref, wout_ref, tok_ref, h_ref, cur)

    def route():
        return _route_tile(prev, gffn_ref, wrt_ref, brt_ref, xnp_ref, rt_ref, rtt_ref, cnt_ref, carry_ref)

    @pl.when(i == 0)
    def _():
        carry_ref[...] = jnp.zeros_like(carry_ref)
        _alternate(mix())

    @pl.when((i > 0) & (i < last))
    def _():
        _alternate(route(), mix())

    @pl.when(i == last)
    def _():
        _alternate(route())


def _merge_route(x, y2, yb, yc, g_mix, w_gate, b_gate, w_glu, b_glu, w_br, w_out, g_ffn, w_rt, b_rt):
    t, d = x.shape
    tm = TM_MERGE
    tiles = t // tm
    mixed = lambda i: jnp.minimum(i, tiles - 1)
    routed = lambda i: jnp.maximum(i - 1, 0)
    c2 = lambda i: (0, 0)
    c3 = lambda i: (0, 0, 0)
    full = lambda a: pl.BlockSpec(a.shape, c2 if a.ndim == 2 else c3, pipeline_mode=pl.Buffered(1))
    return pl.pallas_call(
        _merge_body,
        grid=(tiles + 1,),
        in_specs=[pl.BlockSpec((tm, d), lambda i: (mixed(i), 0)),
                  pl.BlockSpec((SSM_GROUPS, tm // SSM_CHUNK, SSM_CHUNK * SSM_GROUP),
                               lambda i: (0, mixed(i), 0)),
                  pl.BlockSpec((tm, SSM_WIDTH), lambda i: (mixed(i), 0)),
                  pl.BlockSpec((tm, SSM_WIDTH), lambda i: (mixed(i), 0)),
                  full(g_mix), full(w_gate), full(b_gate), full(w_glu), full(b_glu),
                  full(w_br), full(w_out), full(g_ffn), full(w_rt), full(b_rt)],
        out_specs=[pl.BlockSpec((tm, d), lambda i: (mixed(i), 0)),
                   pl.BlockSpec((SC_SPLIT, tm, SC_ROW), lambda i: (0, routed(i), 0)),
                   pl.BlockSpec((tm, LANES), lambda i: (routed(i), 0)),
                   pl.BlockSpec((8, tm), lambda i: (0, routed(i))),
                   pl.BlockSpec((8, LANES), c2)],
        out_shape=[jax.ShapeDtypeStruct((t, d), F32),
                   jax.ShapeDtypeStruct((SC_SPLIT, t, SC_ROW), jnp.uint32),
                   jax.ShapeDtypeStruct((t, LANES), F32),
                   jax.ShapeDtypeStruct((8, t), F32),
                   jax.ShapeDtypeStruct((8, LANES), F32)],
        scratch_shapes=[pltpu.VMEM((8, LANES), F32),
                        pltpu.VMEM((SSM_WIDTH // LANES, tm, LANES), F32),
                        pltpu.VMEM((2, tm, d), F32)],
        compiler_params=pltpu.CompilerParams(dimension_semantics=("arbitrary",),
                                             vmem_limit_bytes=VMEM_LIMIT),
        name="merge_route",
    )(x, y2, yb, yc, g_mix, w_gate, b_gate, w_glu, b_glu, w_br, w_out, g_ffn, w_rt, b_rt)


def _slot_body(rtt_ref, cnt_ref, out_ref, *, nslots):
    tl = rtt_ref.shape[1]
    blocks = jnp.ceil(cnt_ref[...] * (1.0 / BM))
    k_i = lax.broadcasted_iota(jnp.int32, (LANES, LANES), 0)
    l_i = lax.broadcasted_iota(jnp.int32, (LANES, LANES), 1)
    before = jnp.where(k_i < l_i, 1.0, 0.0).astype(BF16)
    first_blk = _dot(blocks.astype(BF16), before).astype(BF16)
    lane_of = lax.broadcasted_iota(jnp.int32, (LANES, tl), 0).astype(F32) - ROUTE_LANE0
    for k in range(TOP_K):
        onehot = jnp.where(rtt_ref[k:k + 1, :] == lane_of, 1.0, 0.0).astype(BF16)
        start = _dot(first_blk, onehot) * BM
        slot = (start[0:1, :] + rtt_ref[TOP_K + k:TOP_K + k + 1, :]).astype(jnp.int32)
        for j in range(SC_SPLIT):
            out_ref[k * SC_SPLIT + j:k * SC_SPLIT + j + 1, :] = slot + j * nslots


def _slot_rows(rtt, cnt, nslots):
    t = rtt.shape[1]
    tl = SLOT_LANES
    return pl.pallas_call(
        functools.partial(_slot_body, nslots=nslots),
        grid=(t // tl,),
        in_specs=[pl.BlockSpec((8, tl), lambda i: (0, i)),
                  pl.BlockSpec((8, LANES), lambda i: (0, 0))],
        out_specs=pl.BlockSpec((TOP_K * SC_SPLIT, tl), lambda i: (0, i)),
        out_shape=jax.ShapeDtypeStruct((TOP_K * SC_SPLIT, t), jnp.int32),
        compiler_params=pltpu.CompilerParams(dimension_semantics=("arbitrary",)),
        name="slot_rows",
    )(rtt, cnt)


def _plan_blocks(counts_ref, blk_expert, blk_valid, blk_first, blk_run, run_expert):
    nb = blk_expert.shape[0]

    def per_expert(e, carry):
        cursor, run = carry
        count = counts_ref[e]
        n_blk = (count + (BM - 1)) // BM

        def per_block(b, cur):
            blk_expert[cur] = e
            blk_valid[cur] = jnp.minimum(count - b * BM, BM)
            blk_first[cur] = (b == 0).astype(jnp.int32)
            blk_run[cur] = run
            return cur + 1

        run_expert[run] = e
        return lax.fori_loop(0, n_blk, per_block, cursor), run + (n_blk > 0).astype(jnp.int32)

    cursor, runs = lax.fori_loop(0, counts_ref.shape[0], per_expert, (jnp.int32(0), jnp.int32(0)))
    run_expert[runs] = -1
    run_expert[runs + 1] = -1

    def empty(j, carry):
        blk_expert[j] = 0
        blk_valid[j] = 0
        blk_first[j] = 0
        blk_run[j] = runs - 1
        return carry

    lax.fori_loop(cursor, nb, empty, 0)


def _expert_body(counts_ref, buf_ref, w1_hbm, w3_hbm, w2_hbm, out_ref, w1_buf, w3_buf, w2_buf, sem,
                 blk_expert, blk_valid, blk_first, blk_run, run_expert):
    i = pl.program_id(0)

    @pl.when(i == 0)
    def _():
        _plan_blocks(counts_ref, blk_expert, blk_valid, blk_first, blk_run, run_expert)

    expert = blk_expert[i]
    valid = blk_valid[i]
    run = blk_run[i]
    slot = run % WEIGHT_SLOTS
    ahead1 = run_expert[run + 1]
    ahead2 = run_expert[run + 2]

    def weight_copies(e, s):
        return (pltpu.make_async_copy(w1_hbm.at[e], w1_buf.at[s], sem.at[s, 0]),
                pltpu.make_async_copy(w3_hbm.at[e], w3_buf.at[s], sem.at[s, 1]),
                pltpu.make_async_copy(w2_hbm.at[e], w2_buf.at[s], sem.at[s, 2]))

    @pl.when(i == 0)
    def _():
        for c in weight_copies(expert, slot):
            c.start()

        @pl.when(ahead1 >= 0)
        def _():
            for c in weight_copies(ahead1, (slot + 1) % WEIGHT_SLOTS):
                c.start()

    @pl.when(blk_first[i] == 1)
    def _():
        for c in weight_copies(expert, slot):
            c.wait()

        @pl.when(ahead2 >= 0)
        def _():
            for c in weight_copies(ahead2, (slot + 2) % WEIGHT_SLOTS):
                c.start()

    def mlp(rows):
        x = _unpack_bf16_pair(jnp.concatenate([buf_ref[j, :rows, :] for j in range(SC_SPLIT)], axis=1))
        row = lax.broadcasted_iota(jnp.int32, x.shape, 0)
        x = jnp.where(row < valid, x, 0.0).astype(BF16)
        h1 = _dot(x, w1_buf[slot].astype(BF16))
        h3 = _dot(x, w3_buf[slot].astype(BF16))
        a = (h1 * _sigmoid(h1) * h3).astype(BF16)
        packed = _pack_bf16_pair(_dot(a, w2_buf[slot].astype(BF16)))
        for j in range(SC_SPLIT):
            out_ref[j, :rows, :] = packed[:, j * SC_ROW:(j + 1) * SC_ROW]
            if rows < BM:
                out_ref[j, rows:, :] = jnp.zeros((BM - rows, SC_ROW), out_ref.dtype)

    for rows in range(BM_STEP, BM + 1, BM_STEP):
        @pl.when((valid > rows - BM_STEP) & (valid <= rows))
        def _():
            mlp(rows)

    @pl.when(valid <= 0)
    def _():
        out_ref[...] = jnp.zeros_like(out_ref)


def _experts(counts, buf, w1, w3, w2):
    _, nslots, _ = buf.shape
    nb = nslots // BM
    rows = pl.BlockSpec((SC_SPLIT, BM, SC_ROW), lambda i, counts: (0, i, 0))
    hbm = pl.BlockSpec(memory_space=pl.ANY)
    table = pltpu.SMEM((nb,), jnp.int32)
    grid_spec = pltpu.PrefetchScalarGridSpec(
        num_scalar_prefetch=1,
        grid=(nb,),
        in_specs=[rows, hbm, hbm, hbm],
        out_specs=rows,
        scratch_shapes=[pltpu.VMEM((WEIGHT_SLOTS,) + w1.shape[1:], w1.dtype),
                        pltpu.VMEM((WEIGHT_SLOTS,) + w3.shape[1:], w3.dtype),
                        pltpu.VMEM((WEIGHT_SLOTS,) + w2.shape[1:], w2.dtype),
                        pltpu.SemaphoreType.DMA((WEIGHT_SLOTS, 3)),
                        table, table, table, table,
                        pltpu.SMEM((counts.shape[0] + 2,), jnp.int32)],
    )
    return pl.pallas_call(
        _expert_body,
        grid_spec=grid_spec,
        out_shape=jax.ShapeDtypeStruct(buf.shape, jnp.uint32),
        compiler_params=pltpu.CompilerParams(dimension_semantics=("arbitrary",),
                                             vmem_limit_bytes=VMEM_LIMIT),
        name="experts",
    )(counts, buf, w1, w3, w2)


def _sc_mesh():
    return plsc.VectorSubcoreMesh(core_axis_name="core", subcore_axis_name="subcore")


def _dispatch_rows(rows, dest0, dest1, nslots):
    t, w = rows.shape
    win = SC_WINDOW
    idx_spec = pl.BlockSpec((1, win), lambda i: (0, i))

    @functools.partial(pl.kernel, mesh=_sc_mesh(), scratch_types=[],
                       out_type=jax.ShapeDtypeStruct((nslots, w), rows.dtype), name="dispatch_rows")
    def run(rows_hbm, i0_hbm, i1_hbm, out_hbm):
        def body(rows_vmem, i0_vmem, i1_vmem):
            pltpu.sync_copy(rows_vmem, out_hbm.at[i0_vmem.at[0]])
            pltpu.sync_copy(rows_vmem, out_hbm.at[i1_vmem.at[0]])

        pltpu.emit_pipeline(
            body, grid=(t // win,),
            in_specs=[pl.BlockSpec((win, w), lambda i: (i, 0)), idx_spec, idx_spec],
            out_specs=[],
            core_axis_name=("core", "subcore"),
            dimension_semantics=(pltpu.PARALLEL,),
        )(rows_hbm, i0_hbm, i1_hbm)

    return run(rows, dest0.reshape(1, t), dest1.reshape(1, t))


def _gather_rows(table, idx):
    n = idx.shape[0]
    w = table.shape[1]
    win = SC_WINDOW

    @functools.partial(pl.kernel, mesh=_sc_mesh(), scratch_types=[],
                       out_type=jax.ShapeDtypeStruct((n, w), table.dtype), name="gather_rows")
    def run(table_hbm, i_hbm, out_hbm):
        def body(i_vmem, out_vmem):
            pltpu.sync_copy(table_hbm.at[i_vmem.at[0]], out_vmem)

        pltpu.emit_pipeline(
            body, grid=(n // win,),
            in_specs=[pl.BlockSpec((1, win), lambda i: (0, i))],
            out_specs=[pl.BlockSpec((win, w), lambda i: (i, 0))],
            core_axis_name=("core", "subcore"),
            dimension_semantics=(pltpu.PARALLEL,),
        )(i_hbm, out_hbm)

    return run(table, idx.reshape(1, n))


def _combine_body(h_ref, g_ref, rt_ref, gfin_ref, out_ref):
    rt = rt_ref[...]
    y = h_ref[...]
    for k in range(TOP_K):
        rows = jnp.concatenate([g_ref[k * SC_SPLIT + j] for j in range(SC_SPLIT)], axis=1)
        y = y + rt[:, 4 + k:5 + k] * _unpack_bf16_pair(rows)
    out_ref[...] = _rms(y, gfin_ref[...])


def _combine(h, g, rt, g_final):
    t, d = h.shape
    tm = TM_OUT
    tok = lambda i: (i, 0)
    return pl.pallas_call(
        _combine_body,
        grid=(t // tm,),
        in_specs=[pl.BlockSpec((tm, d), tok),
                  pl.BlockSpec((TOP_K * SC_SPLIT, tm, SC_ROW), lambda i: (0, i, 0)),
                  pl.BlockSpec((tm, LANES), tok),
                  pl.BlockSpec((1, d), lambda i: (0, 0))],
        out_specs=pl.BlockSpec((tm, d), tok),
        out_shape=jax.ShapeDtypeStruct((t, d), F32),
        compiler_params=pltpu.CompilerParams(dimension_semantics=("arbitrary",),
                                             vmem_limit_bytes=VMEM_LIMIT),
        name="combine",
    )(h, g, rt, g_final)


def _layer(h, mem, g_mix, g_mem, w_in, w_gate, b_gate, lam_re, lam_im, log_dt, b_re, b_im,
           c_re, c_im, d_skip, w_glu, b_glu, g_sgu, w_spatial, b_spatial, w_kv, w_branch,
           w_out, g_ffn, w_group, b_group, w_router, b_router, w1, w3, w2, g_out):
    bsz, s, d = h.shape
    t = bsz * s
    row = lambda a: a.reshape(1, -1)

    tril = jnp.tril(jnp.ones((CHUNK, CHUNK), dtype=bool))
    w_sp = jnp.where(tril, w_spatial, 0.0).astype(BF16)
    b_sp = jnp.broadcast_to(b_spatial[:, :, None], (SGU_HEADS, CHUNK, SGU_HEAD_DIM))
    u2, y_b, y_c = _in_proj(h, row(g_mix), w_in.astype(BF16), row(g_sgu), w_sp, b_sp,
                            mem, row(g_mem), w_kv)
    y2 = _ssm(u2, *_ssm_params(lam_re, lam_im, log_dt, b_re, b_im, c_re, c_im, d_skip), n_seq=bsz)

    pad = LANES - N_GROUPS - N_EXPERTS
    w_rt = jnp.concatenate([w_group, w_router, jnp.zeros((d, pad), F32)], axis=1)
    w_rt_hi = w_rt.astype(BF16)
    w_rt = jnp.concatenate([w_rt_hi, (w_rt - w_rt_hi.astype(F32)).astype(BF16)], axis=1)
    b_rt =jnp.concatenate([b_group, b_router, jnp.zeros((pad,), F32)]).reshape(1, LANES)
    h2, xnp, rt, rtt, cnt = _merge_route(
        h.reshape(t, d), y2, y_b.reshape(t, -1), y_c.reshape(t, -1), row(g_mix),
        w_gate, row(b_gate), w_glu, row(b_glu), w_branch, w_out, row(g_ffn), w_rt, b_rt)

    assert BM & (BM - 1) == 0, "block padding arithmetic assumes a power-of-two block"
    counts = cnt[0, ROUTE_LANE0:ROUTE_LANE0 + N_EXPERTS].astype(jnp.int32)
    nb = (t * TOP_K) // BM + N_EXPERTS
    assert nb < 256, "slot_rows keeps block counts in bf16 matmul operands"
    nslots = nb * BM
    dest_p = _slot_rows(rtt, cnt, nslots).reshape(TOP_K, SC_SPLIT * t)
    buf = _dispatch_rows(xnp.reshape(SC_SPLIT * t, SC_ROW), dest_p[0], dest_p[1], SC_SPLIT * nslots)
    yb = _experts(counts, buf.reshape(SC_SPLIT, nslots, SC_ROW), w1, w3, w2)
    g = _gather_rows(yb.reshape(SC_SPLIT * nslots, SC_ROW), dest_p.reshape(-1))
    out = _combine(h2, g.reshape(TOP_K * SC_SPLIT, t, SC_ROW), rt, row(g_out))
    return out.reshape(bsz, s, d)


def kernel(x, mem, g_mix, g_mem, w_in, w_gate, b_gate, lam_re, lam_im, log_dt, b_re, b_im, c_re,
           c_im, d_skip, w_glu, b_glu, g_sgu, w_spatial, b_spatial, w_kv, w_branch, w_out, g_ffn,
           w_group, b_group, w_router, b_router, w1, w3, w2, g_final):
    assert g_mix.shape[0] == 1, "single-layer stack"
    return _layer(x, mem, g_mix[0], g_mem[0], w_in[0], w_gate[0], b_gate[0], lam_re[0], lam_im[0],
                  log_dt[0], b_re[0], b_im[0], c_re[0], c_im[0], d_skip[0], w_glu[0], b_glu[0],
                  g_sgu[0], w_spatial[0], b_spatial[0], w_kv[0], w_branch[0], w_out[0], g_ffn[0],
                  w_group[0], b_group[0], w_router[0], b_router[0], w1[0], w3[0], w2[0], g_final)
```

```python
import functools
import math

import jax
import jax.numpy as jnp
from jax import lax
from jax.experimental import pallas as pl
from jax.experimental.pallas import tpu as pltpu
from jax.experimental.pallas import tpu_sc as plsc

F32 = jnp.float32
BF16 = jnp.bfloat16

EPS = 1e-6
D_MODEL = 1024
SSM_WIDTH = 512
SSM_GROUP = 16
SSM_GROUPS = 32
SSM_STATE = 64
SSM_CHUNK = 16
SSM_GROUPS_PER_STEP = 1
SGU_WIDTH = 512
SGU_HEADS = 4
SGU_HEAD_DIM = 128
CHUNK = 128
XA_HEADS = 4
XA_HEAD_DIM = 128
N_GROUPS = 8
EXPERTS_PER_GROUP = 8
N_EXPERTS = 64
TOP_K = 2
D_FF = 512
LANES = 128
ROUTE_LANE0 = N_GROUPS

TM_IN = 1024
TM_MERGE = 512
MERGE_COLS = 256
TM_OUT = 1024
BM = 512
BM_STEP = 128
BLOCKS_PER_STEP = 2
WEIGHT_SLOTS = 3
SC_WINDOW = 128
SC_ROW = 256
SC_SPLIT = (D_MODEL // 2) // SC_ROW
SLOT_LANES = 2048
VMEM_LIMIT = 56 * 1024 * 1024


def _rms(x, g):
    return x * lax.rsqrt(jnp.mean(x * x, axis=-1, keepdims=True) + EPS) * g


def _sigmoid(x):
    return 0.5 * (1.0 + jnp.tanh(0.5 * x))


def _gelu(x):
    c = math.sqrt(2.0 / math.pi)
    return 0.5 * x * (1.0 + jnp.tanh(c * (x + 0.044715 * (x * x * x))))


def _dot(a, b):
    return jnp.dot(a, b, preferred_element_type=F32)


_NT = (((1,), (1,)), ((), ()))


def _pack_bf16_pair(x):
    n = x.shape[1] // 2
    lo = lax.bitcast_convert_type(x[:, :n].astype(BF16).astype(F32), jnp.uint32)
    hi = lax.bitcast_convert_type(x[:, n:].astype(BF16).astype(F32), jnp.uint32)
    return hi | (lo >> 16)


def _unpack_bf16_pair(p):
    lo = lax.bitcast_convert_type(p << 16, F32)
    hi = lax.bitcast_convert_type(p & jnp.uint32(0xFFFF0000), F32)
    return jnp.concatenate([lo, hi], axis=1)


GROUPS_PER_TILE = LANES // SSM_GROUP
POS_PER_TILE = LANES // SSM_GROUP


def _slot_masks(rows):
    lane = lax.broadcasted_iota(jnp.int32, (rows, LANES), 1)
    return [(lane >= i * SSM_GROUP) & (lane < (i + 1) * SSM_GROUP) for i in range(LANES // SSM_GROUP)]


def _tokens_to_chunks(tok_ref, out_ref):
    tm = tok_ref.shape[1]
    nc = tm // SSM_CHUNK
    masks = _slot_masks(nc)
    for k in range(SSM_WIDTH // LANES):
        for j in range(SSM_CHUNK // POS_PER_TILE):
            src = [tok_ref[k, pl.ds(j * POS_PER_TILE + p, nc, stride=SSM_CHUNK), :]
                   for p in range(POS_PER_TILE)]
            for gi in range(GROUPS_PER_TILE):
                acc = None
                for p in range(POS_PER_TILE):
                    shift = ((p - gi) * SSM_GROUP) % LANES
                    r = pltpu.roll(src[p], shift, 1) if shift else src[p]
                    acc = r if acc is None else jnp.where(masks[p], r, acc)
                out_ref[k * GROUPS_PER_TILE + gi, :, pl.ds(j * LANES, LANES)] = acc.astype(out_ref.dtype)


def _chunks_to_tokens(chunk_ref, tok_ref):
    tm = tok_ref.shape[1]
    nc = tm // SSM_CHUNK
    masks = _slot_masks(nc)
    for k in range(SSM_WIDTH // LANES):
        for j in range(SSM_CHUNK // POS_PER_TILE):
            src = [chunk_ref[k * GROUPS_PER_TILE + gi, :, pl.ds(j * LANES, LANES)].astype(F32)
                   for gi in range(GROUPS_PER_TILE)]
            for p in range(POS_PER_TILE):
                acc = None
                for gi in range(GROUPS_PER_TILE):
                    shift = ((gi - p) * SSM_GROUP) % LANES
                    r = pltpu.roll(src[gi], shift, 1) if shift else src[gi]
                    acc = r if acc is None else jnp.where(masks[gi], r, acc)
                tok_ref[k, pl.ds(j * POS_PER_TILE + p, nc, stride=SSM_CHUNK), :] = acc


def _in_body(x_ref, gmix_ref, win_ref, gsgu_ref, wsp_ref, bsp_ref, mem_ref, gmem_ref, wkv_ref,
             u2_ref, yb_ref, yc_ref, tok_ref, k_ref, v_ref):
    @pl.when(pl.program_id(1) == 0)
    def _():
        kv = _dot(_rms(mem_ref[0], gmem_ref[...]).astype(BF16), wkv_ref[...].astype(BF16))
        k_ref[...] = kv[:, :XA_HEADS * XA_HEAD_DIM].astype(BF16)
        v_ref[...] = kv[:, XA_HEADS * XA_HEAD_DIM:].astype(BF16)

    n = _rms(x_ref[0], gmix_ref[...]).astype(BF16)
    proj = _dot(n, win_ref[...])
    for k in range(SSM_WIDTH // LANES):
        tok_ref[k] = proj[:, k * LANES:(k + 1) * LANES]
    _tokens_to_chunks(tok_ref, u2_ref)

    u = _gelu(proj[:, SSM_WIDTH:SSM_WIDTH + SGU_WIDTH])
    v = _gelu(proj[:, SSM_WIDTH + SGU_WIDTH:SSM_WIDTH + 2 * SGU_WIDTH])
    v = _rms(v, gsgu_ref[...]).astype(BF16)
    tm = u.shape[0]
    rows = []
    for c in range(tm // CHUNK):
        vc = v[c * CHUNK:(c + 1) * CHUNK]
        heads = []
        for h in range(SGU_HEADS):
            sl = slice(h * SGU_HEAD_DIM, (h + 1) * SGU_HEAD_DIM)
            heads.append(_dot(wsp_ref[h], vc[:, sl]) + bsp_ref[h])
        rows.append(jnp.concatenate(heads, axis=1))
    sv = jnp.concatenate(rows, axis=0)
    yb_ref[0] = (u * sv).astype(BF16)

    q = proj[:, SSM_WIDTH + 2 * SGU_WIDTH:].astype(BF16)
    kk = k_ref[...]
    vv = v_ref[...]
    outs = []
    for h in range(XA_HEADS):
        sl = slice(h * XA_HEAD_DIM, (h + 1) * XA_HEAD_DIM)
        s = lax.dot_general(q[:, sl], kk[:, sl], (((1,), (1,)), ((), ())),
                            preferred_element_type=F32) * (XA_HEAD_DIM ** -0.5)
        e = jnp.exp(s - jnp.max(s, axis=-1, keepdims=True))
        l = jnp.sum(e, axis=-1, keepdims=True)
        outs.append(_dot(e.astype(BF16), vv[:, sl]) / l)
    yc_ref[0] = jnp.concatenate(outs, axis=1).astype(BF16)


def _in_proj(x, g_mix, w_in, g_sgu, w_sp, b_sp, mem, g_mem, w_kv):
    b, s, d = x.shape
    m = mem.shape[1]
    xa = XA_HEADS * XA_HEAD_DIM
    const2 = lambda i, j: (0, 0)
    const3 = lambda i, j: (0, 0, 0)
    tok = lambda i, j: (i, j, 0)
    per_b = lambda i, j: (i, 0, 0)
    out = jax.ShapeDtypeStruct((b, s, SSM_WIDTH), BF16)
    nc = TM_IN // SSM_CHUNK
    tiles = s // TM_IN
    u2 = jax.ShapeDtypeStruct((SSM_GROUPS, b * s // SSM_CHUNK, SSM_CHUNK * SSM_GROUP), BF16)
    return pl.pallas_call(
        _in_body,
        grid=(b, s // TM_IN),
        in_specs=[pl.BlockSpec((1, TM_IN, d), tok),
                  pl.BlockSpec((1, d), const2),
                  pl.BlockSpec(w_in.shape, const2),
                  pl.BlockSpec((1, SGU_WIDTH), const2),
                  pl.BlockSpec(w_sp.shape, const3),
                  pl.BlockSpec(b_sp.shape, const3),
                  pl.BlockSpec((1, m, d), per_b),
                  pl.BlockSpec((1, d), const2),
                  pl.BlockSpec(w_kv.shape, const2, pipeline_mode=pl.Buffered(1))],
        out_specs=[pl.BlockSpec((SSM_GROUPS, nc, SSM_CHUNK * SSM_GROUP), lambda i, j: (0, i * tiles + j, 0)),
                   pl.BlockSpec((1, TM_IN, SSM_WIDTH), tok),
                   pl.BlockSpec((1, TM_IN, SSM_WIDTH), tok)],
        out_shape=[u2, out, out],
        scratch_shapes=[pltpu.VMEM((SSM_WIDTH // LANES, TM_IN, LANES), F32),
                        pltpu.VMEM((m, xa), BF16),
                        pltpu.VMEM((m, xa), BF16)],
        compiler_params=pltpu.CompilerParams(dimension_semantics=("arbitrary", "arbitrary"),
                                             vmem_limit_bytes=VMEM_LIMIT),
        name="in_proj",
    )(x, g_mix, w_in, g_sgu, w_sp, b_sp, mem, g_mem, w_kv)


def _alternate(*stages):
    live = list(stages)
    while live:
        live = [s for s in live if next(s, True) is None]


def _ssm_params(lam_re, lam_im, log_dt, b_re, b_im, c_re, c_im, d_skip):
    g, p = lam_re.shape
    dup = lambda a: jnp.concatenate([a, a], axis=-1)
    lam = jnp.stack([dup(lam_re), dup(lam_im), jnp.broadcast_to(log_dt[:, None], (g, 2 * p))], axis=1)
    brt = b_re.transpose(0, 2, 1)
    bit = b_im.transpose(0, 2, 1)
    cat = lambda a, b: jnp.concatenate([a, b], axis=-1)
    bc = jnp.stack([cat(brt, bit), cat(bit, brt), cat(c_re, -c_im), cat(-c_im, -c_re)], axis=1)
    d2 = jnp.tile(d_skip.reshape(g, 1, SSM_GROUP), (1, 1, SSM_CHUNK))
    return lam, bc, d2


def _ssm_operators(lam_ref, bc_ref, ccat_ref, n_ref, m_ref):
    lam_re = lam_ref[0:1, :]
    lam_im = lam_ref[1:2, :]
    dt = jnp.exp(lam_ref[2:3, :])
    ar = lam_re * dt
    ai = lam_im * dt
    lane = lax.broadcasted_iota(jnp.int32, (1, LANES), 1)
    sgn = jnp.where(lane >= SSM_STATE, 1.0, -1.0)

    def powers(j):
        mag = jnp.exp(ar * j)
        ph = ai * j
        return mag * jnp.cos(ph), mag * jnp.sin(ph)

    pos = lax.broadcasted_iota(jnp.int32, (SSM_CHUNK, 1), 0).astype(F32)
    p_re, p_im = powers(pos)
    r_re, r_im = powers((SSM_CHUNK - 1) - pos)
    one_re, one_im = powers(jnp.ones((1, 1), F32))
    q_re = p_re * one_re - p_im * one_im
    q_im = p_re * one_im + p_im * one_re
    step = lax.shift_left(jnp.full((8, 1), SSM_CHUNK, jnp.int32),
                          lax.broadcasted_iota(jnp.int32, (8, 1), 0)).astype(F32)
    s_re, s_im = powers(step)

    den = lam_re * lam_re + lam_im * lam_im
    f_re = ((one_re - 1.0) * lam_re + one_im * lam_im) / den
    f_im = (one_im * lam_re - (one_re - 1.0) * lam_im) / den
    b1, b2, ca, cb = bc_ref[0], bc_ref[1], bc_ref[2], bc_ref[3]
    bb1 = f_re * b1 + (sgn * f_im) * b2
    bb2 = f_re * b2 - (sgn * f_im) * b1
    r_ims = sgn * r_im
    for s in range(SSM_CHUNK):
        blk = pl.ds(s * SSM_GROUP, SSM_GROUP)
        ccat_ref[blk, :] = ca * p_re[s:s + 1, :] + cb * p_im[s:s + 1, :]
        m_ref[blk, :] = (ca * q_re[s:s + 1, :] + cb * q_im[s:s + 1, :]).astype(m_ref.dtype)
        n_ref[blk, :] = (bb1 * r_re[s:s + 1, :] + bb2 * r_ims[s:s + 1, :]).astype(n_ref.dtype)
    return bb1, s_re, sgn * s_im


def _ssm_body(u_ref, lam_ref, bc_ref, d2_ref, y_ref, toep_ref, ccat_ref, n_ref, m_ref, *, n_seq):
    for g in range(u_ref.shape[0]):
        _ssm_group(u_ref.at[g], lam_ref.at[g], bc_ref.at[g], d2_ref.at[g], y_ref.at[g],
                   toep_ref.at[g], ccat_ref.at[g], n_ref.at[g], m_ref.at[g], n_seq)


def _ssm_group(u_ref, lam_ref, bc_ref, d2_ref, y_ref, toep_ref, ccat_ref, n_ref, m_ref, n_seq):
    bcat, lr, li = _ssm_operators(lam_ref, bc_ref, ccat_ref, n_ref, m_ref)
    kern = lax.dot_general(bcat, ccat_ref[...], _NT, precision=lax.Precision.HIGHEST,
                           preferred_element_type=F32)
    col = lax.broadcasted_iota(jnp.int32, kern.shape, 1)
    for s in range(SSM_CHUNK):
        shifted = pltpu.roll(kern, s * SSM_GROUP, 1) if s else kern
        toep_ref[s * SSM_GROUP:(s + 1) * SSM_GROUP, :] = jnp.where(
            col >= s * SSM_GROUP, shifted, 0.0).astype(BF16)

    u = u_ref[...]
    rows = u.shape[0]
    per = rows // n_seq
    y = _dot(u, toep_ref[...])
    st = _dot(u, n_ref[...])
    row = lax.broadcasted_iota(jnp.int32, (per, LANES), 0)
    prev = []
    for b in range(n_seq):
        x = st[b * per:(b + 1) * per]
        k = 0
        while (1 << k) < per:
            d = 1 << k
            sh = jnp.where(row >= d, pltpu.roll(x, d, 0), 0.0)
            x = x + sh * lr[k:k + 1, :] + pltpu.roll(sh, SSM_STATE, 1) * li[k:k + 1, :]
            k += 1
        prev.append(jnp.where(row >= 1, pltpu.roll(x, 1, 0), 0.0))
    xp = jnp.concatenate(prev, axis=0).astype(BF16)
    y = y + lax.dot_general(xp, m_ref[...], _NT, preferred_element_type=F32) + d2_ref[...] * u.astype(F32)
    y_ref[...] = _gelu(y).astype(BF16)


def _ssm(u2, lam, bc, d2, n_seq):
    g, rows, w = u2.shape
    assert rows // n_seq <= 1 << 8, "lam_bar^(16*2^k) is prepared for 8 scan steps"
    gs = SSM_GROUPS_PER_STEP
    blk = lambda a: pl.BlockSpec((gs,) + a.shape[1:], lambda i: (i,) + (0,) * (a.ndim - 1))
    return pl.pallas_call(
        functools.partial(_ssm_body, n_seq=n_seq),
        grid=(g // gs,),
        in_specs=[blk(u2), blk(lam), blk(bc), blk(d2)],
        out_specs=blk(u2),
        out_shape=jax.ShapeDtypeStruct(u2.shape, BF16),
        scratch_shapes=[pltpu.VMEM((gs, w, w), BF16),
                        pltpu.VMEM((gs, w, 2 * SSM_STATE), F32),
                        pltpu.VMEM((gs, w, 2 * SSM_STATE), BF16),
                        pltpu.VMEM((gs, w, 2 * SSM_STATE), BF16)],
        compiler_params=pltpu.CompilerParams(dimension_semantics=("arbitrary",),
                                             vmem_limit_bytes=VMEM_LIMIT),
        name="ssm",
    )(u2, lam, bc, d2)


def _mix_tile(x_ref, y2_ref, yb_ref, yc_ref, gmix_ref, wgate_ref, bgate_ref, wglu_ref, bglu_ref,
              wbr_ref, wout_ref, tok_ref, h_ref, hkeep_ref):
    x = x_ref[...]
    n = _rms(x, gmix_ref[...]).astype(BF16)

    def gated(b, c, y):
        cols = pl.ds(b * D_MODEL + c * MERGE_COLS, MERGE_COLS)
        gate = _sigmoid(_dot(n, wgate_ref[:, cols].astype(BF16)) + bgate_ref[:, cols])
        return gate * _dot(y, wbr_ref[b, :, pl.ds(c * MERGE_COLS, MERGE_COLS)].astype(BF16))

    n_blocks = D_MODEL // MERGE_COLS
    yb = yb_ref[...]
    yc = yc_ref[...]
    head = gated(1, 0, yb) + gated(2, 0, yc)
    yield
    _chunks_to_tokens(y2_ref, tok_ref)
    ys = jnp.concatenate([tok_ref[k] for k in range(SSM_WIDTH // LANES)], axis=1).astype(BF16)
    glu = _dot(ys, wglu_ref[...].astype(BF16)) + bglu_ref[...]
    ya = (glu[:, :SSM_WIDTH] * _sigmoid(glu[:, SSM_WIDTH:])).astype(BF16)
    merged = [(head + gated(0, 0, ya)).astype(BF16)]
    for c in range(1, n_blocks):
        yield
        acc = gated(0, c, ya)
        yield
        acc = acc + gated(1, c, yb)
        yield
        merged.append((acc + gated(2, c, yc)).astype(BF16))
    yield
    h = x + _dot(jnp.concatenate(merged, axis=1), wout_ref[...].astype(BF16))
    h_ref[...] = h
    hkeep_ref[...] = h


def _route_tile(hkeep_ref, gffn_ref, wrt_ref, brt_ref, xnp_ref, rt_ref, rtt_ref, cnt_ref, carry_ref):
    h = hkeep_ref[...]
    xn = _rms(h, gffn_ref[...])
    yield
    packed = _pack_bf16_pair(xn)
    for j in range(SC_SPLIT):
        xnp_ref[j] = packed[:, j * SC_ROW:(j + 1) * SC_ROW]
    yield

    x_hi = xn.astype(BF16)
    x_lo = (xn - x_hi.astype(F32)).astype(BF16)
    head = _dot(x_hi, wrt_ref[...])
    logits = (head[:, :LANES] + head[:, LANES:] + _dot(x_lo, wrt_ref[:, :LANES])) + brt_ref[...]
    yield
    tm = logits.shape[0]
    lane_i = lax.broadcasted_iota(jnp.int32, (tm, LANES), 1)
    lane = lane_i.astype(F32)
    neg = jnp.float32(-3.0e38)
    big = jnp.float32(LANES)
    gmask = lane_i < N_GROUPS
    gl = jnp.where(gmask, logits, neg)
    gmax = jnp.max(gl, axis=-1, keepdims=True)
    gidx = jnp.min(jnp.where(gl == gmax, lane, big), axis=-1, keepdims=True)
    gsum = jnp.sum(jnp.where(gmask, jnp.exp(gl - gmax), 0.0), axis=-1, keepdims=True)
    g_w = 1.0 / gsum
    yield
    e_lane = lane_i - ROUTE_LANE0
    lane_group = (e_lane >> 3).astype(F32)
    emask = (e_lane >= 0) & (e_lane < N_EXPERTS) & (lane_group == gidx)
    el = jnp.where(emask, logits, neg)
    m1 = jnp.max(el, axis=-1, keepdims=True)
    i1 = jnp.min(jnp.where(el == m1, lane, big), axis=-1, keepdims=True)
    yield
    el2 = jnp.where(lane == i1, neg, el)
    m2 = jnp.max(el2, axis=-1, keepdims=True)
    i2 = jnp.min(jnp.where(el2 == m2, lane, big), axis=-1, keepdims=True)
    t = jnp.exp(m2 - m1)
    w1 = g_w / (1.0 + t)
    w2 = g_w * t / (1.0 + t)
    yield

    sel1 = lane == i1
    sel2 = lane == i2
    onehot = jnp.where(sel1 | sel2, 1.0, 0.0)
    r_i = lax.broadcasted_iota(jnp.int32, (tm, tm), 0)
    c_i = lax.broadcasted_iota(jnp.int32, (tm, tm), 1)
    stril = jnp.where(c_i < r_i, 1.0, 0.0).astype(BF16)
    cum = _dot(stril, onehot.astype(BF16)) + carry_ref[0:1, :]
    rank1 = jnp.sum(jnp.where(sel1, cum, 0.0), axis=-1, keepdims=True)
    rank2 = jnp.sum(jnp.where(sel2, cum, 0.0), axis=-1, keepdims=True)
    carry_ref[...] = carry_ref[...] + jnp.sum(onehot, axis=0, keepdims=True)
    cnt_ref[...] = carry_ref[...]
    yield

    cols = (i1 - ROUTE_LANE0, i2 - ROUTE_LANE0, rank1, rank2, w1, w2)
    rt = jnp.zeros((tm, LANES), F32)
    for c, val in enumerate(cols):
        rt = jnp.where(lane_i == c, val, rt)
    rt_ref[...] = rt
    rtt_ref[...] = rt.T[:8]


def _merge_body(x_ref, y2_ref, yb_ref, yc_ref, gmix_ref, wgate_ref, bgate_ref, wglu_ref, bglu_ref,
                wbr_ref, wout_ref, gffn_ref, wrt_ref, brt_ref,
                h_ref, xnp_ref, rt_ref, rtt_ref, cnt_ref, carry_ref, tok_ref, hkeep_ref):
    i = pl.program_id(0)
    last = pl.num_programs(0) - 1
    cur = hkeep_ref.at[i % 2]
    prev = hkeep_ref.at[(i + 1) % 2]

    def mix():
        return _mix_tile(x_ref, y2_ref, yb_ref, yc_ref, gmix_ref, wgate_ref, bgate_ref, wglu_ref,
                         bglu_ref, wbr_ref, wout_ref, tok_ref, h_ref, cur)

    def route():
        return _route_tile(prev, gffn_ref, wrt_ref, brt_ref, xnp_ref, rt_ref, rtt_ref, cnt_ref, carry_ref)

    @pl.when(i == 0)
    def _():
        carry_ref[...] = jnp.zeros_like(carry_ref)
        _alternate(mix())

    @pl.when((i > 0) & (i < last))
    def _():
        _alternate(route(), mix())

    @pl.when(i == last)
    def _():
        _alternate(route())


def _merge_route(x, y2, yb, yc, g_mix, w_gate, b_gate, w_glu, b_glu, w_br, w_out, g_ffn, w_rt, b_rt):
    t, d = x.shape
    tm = TM_MERGE
    tiles = t // tm
    mixed = lambda i: jnp.minimum(i, tiles - 1)
    routed = lambda i: jnp.maximum(i - 1, 0)
    c2 = lambda i: (0, 0)
    c3 = lambda i: (0, 0, 0)
    full = lambda a: pl.BlockSpec(a.shape, c2 if a.ndim == 2 else c3, pipeline_mode=pl.Buffered(1))
    return pl.pallas_call(
        _merge_body,
        grid=(tiles + 1,),
        in_specs=[pl.BlockSpec((tm, d), lambda i: (mixed(i), 0)),
                  pl.BlockSpec((SSM_GROUPS, tm // SSM_CHUNK, SSM_CHUNK * SSM_GROUP),
                               lambda i: (0, mixed(i), 0)),
                  pl.BlockSpec((tm, SSM_WIDTH), lambda i: (mixed(i), 0)),
                  pl.BlockSpec((tm, SSM_WIDTH), lambda i: (mixed(i), 0)),
                  full(g_mix), full(w_gate), full(b_gate), full(w_glu), full(b_glu),
                  full(w_br), full(w_out), full(g_ffn), full(w_rt), full(b_rt)],
        out_specs=[pl.BlockSpec((tm, d), lambda i: (mixed(i), 0)),
                   pl.BlockSpec((SC_SPLIT, tm, SC_ROW), lambda i: (0, routed(i), 0)),
                   pl.BlockSpec((tm, LANES), lambda i: (routed(i), 0)),
                   pl.BlockSpec((8, tm), lambda i: (0, routed(i))),
                   pl.BlockSpec((8, LANES), c2)],
        out_shape=[jax.ShapeDtypeStruct((t, d), F32),
                   jax.ShapeDtypeStruct((SC_SPLIT, t, SC_ROW), jnp.uint32),
                   jax.ShapeDtypeStruct((t, LANES), F32),
                   jax.ShapeDtypeStruct((8, t), F32),
                   jax.ShapeDtypeStruct((8, LANES), F32)],
        scratch_shapes=[pltpu.VMEM((8, LANES), F32),
                        pltpu.VMEM((SSM_WIDTH // LANES, tm, LANES), F32),
                        pltpu.VMEM((2, tm, d), F32)],
        compiler_params=pltpu.CompilerParams(dimension_semantics=("arbitrary",),
                                             vmem_limit_bytes=VMEM_LIMIT),
        name="merge_route",
    )(x, y2, yb, yc, g_mix, w_gate, b_gate, w_glu, b_glu, w_br, w_out, g_ffn, w_rt, b_rt)


def _slot_body(rtt_ref, cnt_ref, out_ref, *, nslots):
    tl = rtt_ref.shape[1]
    blocks = jnp.ceil(cnt_ref[...] * (1.0 / BM))
    k_i = lax.broadcasted_iota(jnp.int32, (LANES, LANES), 0)
    l_i = lax.broadcasted_iota(jnp.int32, (LANES, LANES), 1)
    before = jnp.where(k_i < l_i, 1.0, 0.0).astype(BF16)
    first_blk = _dot(blocks.astype(BF16), before).astype(BF16)
    lane_of = lax.broadcasted_iota(jnp.int32, (LANES, tl), 0).astype(F32) - ROUTE_LANE0
    for k in range(TOP_K):
        onehot = jnp.where(rtt_ref[k:k + 1, :] == lane_of, 1.0, 0.0).astype(BF16)
        start = _dot(first_blk, onehot) * BM
        slot = (start[0:1, :] + rtt_ref[TOP_K + k:TOP_K + k + 1, :]).astype(jnp.int32)
        for j in range(SC_SPLIT):
            out_ref[k * SC_SPLIT + j:k * SC_SPLIT + j + 1, :] = slot + j * nslots


def _slot_rows(rtt, cnt, nslots):
    t = rtt.shape[1]
    tl = SLOT_LANES
    return pl.pallas_call(
        functools.partial(_slot_body, nslots=nslots),
        grid=(t // tl,),
        in_specs=[pl.BlockSpec((8, tl), lambda i: (0, i)),
                  pl.BlockSpec((8, LANES), lambda i: (0, 0))],
        out_specs=pl.BlockSpec((TOP_K * SC_SPLIT, tl), lambda i: (0, i)),
        out_shape=jax.ShapeDtypeStruct((TOP_K * SC_SPLIT, t), jnp.int32),
        compiler_params=pltpu.CompilerParams(dimension_semantics=("arbitrary",)),
        name="slot_rows",
    )(rtt, cnt)


def _plan_blocks(counts_ref, blk_expert, blk_valid, blk_first, blk_run, run_expert):
    nb = blk_expert.shape[0]

    def per_expert(e, carry):
        cursor, run = carry
        count = counts_ref[e]
        n_blk = (count + (BM - 1)) // BM

        def per_block(b, cur):
            blk_expert[cur] = e
            blk_valid[cur] = jnp.minimum(count - b * BM, BM)
            blk_first[cur] = (b == 0).astype(jnp.int32)
            blk_run[cur] = run
            return cur + 1

        run_expert[run] = e
        return lax.fori_loop(0, n_blk, per_block, cursor), run + (n_blk > 0).astype(jnp.int32)

    cursor, runs = lax.fori_loop(0, counts_ref.shape[0], per_expert, (jnp.int32(0), jnp.int32(0)))
    run_expert[runs] = -1
    run_expert[runs + 1] = -1

    def empty(j, carry):
        blk_expert[j] = 0
        blk_valid[j] = 0
        blk_first[j] = 0
        blk_run[j] = runs - 1
        return carry

    lax.fori_loop(cursor, nb, empty, 0)


def _expert_body(counts_ref, buf_ref, w1_hbm, w3_hbm, w2_hbm, out_ref, w1_buf, w3_buf, w2_buf, sem,
                 blk_expert, blk_valid, blk_first, blk_run, run_expert):
    step = pl.program_id(0)

    @pl.when(step == 0)
    def _():
        _plan_blocks(counts_ref, blk_expert, blk_valid, blk_first, blk_run, run_expert)

    def weight_copies(e, s):
        return (pltpu.make_async_copy(w1_hbm.at[e], w1_buf.at[s], sem.at[s, 0]),
                pltpu.make_async_copy(w3_hbm.at[e], w3_buf.at[s], sem.at[s, 1]),
                pltpu.make_async_copy(w2_hbm.at[e], w2_buf.at[s], sem.at[s, 2]))

    for sub in range(BLOCKS_PER_STEP):
        i = step * BLOCKS_PER_STEP + sub
        block = pl.ds(sub * BM, BM)
        _expert_block(i, buf_ref.at[:, block, :], out_ref.at[:, block, :], weight_copies,
                      (w1_buf, w3_buf, w2_buf), blk_expert, blk_valid, blk_first, blk_run, run_expert)


def _expert_block(i, buf_ref, out_ref, weight_copies, weights, blk_expert, blk_valid, blk_first, blk_run,
                  run_expert):
    w1_buf, w3_buf, w2_buf = weights
    expert = blk_expert[i]
    valid = blk_valid[i]
    run = blk_run[i]
    slot = run % WEIGHT_SLOTS
    ahead1 = run_expert[run + 1]
    ahead2 = run_expert[run + 2]

    @pl.when(i == 0)
    def _():
        for c in weight_copies(expert, slot):
            c.start()

        @pl.when(ahead1 >= 0)
        def _():
            for c in weight_copies(ahead1, (slot + 1) % WEIGHT_SLOTS):
                c.start()

    @pl.when(blk_first[i] == 1)
    def _():
        for c in weight_copies(expert, slot):
            c.wait()

        @pl.when(ahead2 >= 0)
        def _():
            for c in weight_copies(ahead2, (slot + 2) % WEIGHT_SLOTS):
                c.start()

    def mlp(rows):
        x = _unpack_bf16_pair(jnp.concatenate([buf_ref[j, :rows, :] for j in range(SC_SPLIT)], axis=1))
        row = lax.broadcasted_iota(jnp.int32, x.shape, 0)
        x = jnp.where(row < valid, x, 0.0).astype(BF16)
        h1 = _dot(x, w1_buf[slot].astype(BF16))
        h3 = _dot(x, w3_buf[slot].astype(BF16))
        a = (h1 * _sigmoid(h1) * h3).astype(BF16)
        packed = _pack_bf16_pair(_dot(a, w2_buf[slot].astype(BF16)))
        for j in range(SC_SPLIT):
            out_ref[j, :rows, :] = packed[:, j * SC_ROW:(j + 1) * SC_ROW]
            if rows < BM:
                out_ref[j, rows:, :] = jnp.zeros((BM - rows, SC_ROW), out_ref.dtype)

    for rows in range(BM_STEP, BM + 1, BM_STEP):
        @pl.when((valid > rows - BM_STEP) & (valid <= rows))
        def _():
            mlp(rows)

    @pl.when(valid <= 0)
    def _():
        out_ref[...] = jnp.zeros_like(out_ref)


def _experts(counts, buf, w1, w3, w2):
    _, nslots, _ = buf.shape
    nb = nslots // BM
    rows = pl.BlockSpec((SC_SPLIT, BLOCKS_PER_STEP * BM, SC_ROW), lambda i, counts: (0, i, 0))
    hbm = pl.BlockSpec(memory_space=pl.ANY)
    table = pltpu.SMEM((nb,), jnp.int32)
    grid_spec = pltpu.PrefetchScalarGridSpec(
        num_scalar_prefetch=1,
        grid=(nb // BLOCKS_PER_STEP,),
        in_specs=[rows, hbm, hbm, hbm],
        out_specs=rows,
        scratch_shapes=[pltpu.VMEM((WEIGHT_SLOTS,) + w1.shape[1:], w1.dtype),
                        pltpu.VMEM((WEIGHT_SLOTS,) + w3.shape[1:], w3.dtype),
                        pltpu.VMEM((WEIGHT_SLOTS,) + w2.shape[1:], w2.dtype),
                        pltpu.SemaphoreType.DMA((WEIGHT_SLOTS, 3)),
                        table, table, table, table,
                        pltpu.SMEM((counts.shape[0] + 2,), jnp.int32)],
    )
    return pl.pallas_call(
        _expert_body,
        grid_spec=grid_spec,
        out_shape=jax.ShapeDtypeStruct(buf.shape, jnp.uint32),
        compiler_params=pltpu.CompilerParams(dimension_semantics=("arbitrary",),
                                             vmem_limit_bytes=VMEM_LIMIT),
        name="experts",
    )(counts, buf, w1, w3, w2)


def _sc_mesh():
    return plsc.VectorSubcoreMesh(core_axis_name="core", subcore_axis_name="subcore")


def _dispatch_rows(rows, dest0, dest1, nslots):
    t, w = rows.shape
    win = SC_WINDOW
    idx_spec = pl.BlockSpec((1, win), lambda i: (0, i))

    @functools.partial(pl.kernel, mesh=_sc_mesh(), scratch_types=[],
                       out_type=jax.ShapeDtypeStruct((nslots, w), rows.dtype), name="dispatch_rows")
    def run(rows_hbm, i0_hbm, i1_hbm, out_hbm):
        def body(rows_vmem, i0_vmem, i1_vmem):
            pltpu.sync_copy(rows_vmem, out_hbm.at[i0_vmem.at[0]])
            pltpu.sync_copy(rows_vmem, out_hbm.at[i1_vmem.at[0]])

        pltpu.emit_pipeline(
            body, grid=(t // win,),
            in_specs=[pl.BlockSpec((win, w), lambda i: (i, 0)), idx_spec, idx_spec],
            out_specs=[],
            core_axis_name=("core", "subcore"),
            dimension_semantics=(pltpu.PARALLEL,),
        )(rows_hbm, i0_hbm, i1_hbm)

    return run(rows, dest0.reshape(1, t), dest1.reshape(1, t))


def _gather_rows(table, idx):
    n = idx.shape[0]
    w = table.shape[1]
    win = SC_WINDOW

    @functools.partial(pl.kernel, mesh=_sc_mesh(), scratch_types=[],
                       out_type=jax.ShapeDtypeStruct((n, w), table.dtype), name="gather_rows")
    def run(table_hbm, i_hbm, out_hbm):
        def body(i_vmem, out_vmem):
            pltpu.sync_copy(table_hbm.at[i_vmem.at[0]], out_vmem)

        pltpu.emit_pipeline(
            body, grid=(n // win,),
            in_specs=[pl.BlockSpec((1, win), lambda i: (0, i))],
            out_specs=[pl.BlockSpec((win, w), lambda i: (i, 0))],
            core_axis_name=("core", "subcore"),
            dimension_semantics=(pltpu.PARALLEL,),
        )(i_hbm, out_hbm)

    return run(table, idx.reshape(1, n))


def _combine_body(h_ref, g_ref, rt_ref, gfin_ref, out_ref):
    rt = rt_ref[...]
    y = h_ref[...]
    for k in range(TOP_K):
        rows = jnp.concatenate([g_ref[k * SC_SPLIT + j] for j in range(SC_SPLIT)], axis=1)
        y = y + rt[:, 4 + k:5 + k] * _unpack_bf16_pair(rows)
    out_ref[...] = _rms(y, gfin_ref[...])


def _combine(h, g, rt, g_final):
    t, d = h.shape
    tm = TM_OUT
    tok = lambda i: (i, 0)
    return pl.pallas_call(
        _combine_body,
        grid=(t // tm,),
        in_specs=[pl.BlockSpec((tm, d), tok),
                  pl.BlockSpec((TOP_K * SC_SPLIT, tm, SC_ROW), lambda i: (0, i, 0)),
                  pl.BlockSpec((tm, LANES), tok),
                  pl.BlockSpec((1, d), lambda i: (0, 0))],
        out_specs=pl.BlockSpec((tm, d), tok),
        out_shape=jax.ShapeDtypeStruct((t, d), F32),
        compiler_params=pltpu.CompilerParams(dimension_semantics=("arbitrary",),
                                             vmem_limit_bytes=VMEM_LIMIT),
        name="combine",
    )(h, g, rt, g_final)


def _layer(h, mem, g_mix, g_mem, w_in, w_gate, b_gate, lam_re, lam_im, log_dt, b_re, b_im,
           c_re, c_im, d_skip, w_glu, b_glu, g_sgu, w_spatial, b_spatial, w_kv, w_branch,
           w_out, g_ffn, w_group, b_group, w_router, b_router, w1, w3, w2, g_out):
    bsz, s, d = h.shape
    t = bsz * s
    row = lambda a: a.reshape(1, -1)

    tril = jnp.tril(jnp.ones((CHUNK, CHUNK), dtype=bool))
    w_sp = jnp.where(tril, w_spatial, 0.0).astype(BF16)
    b_sp = jnp.broadcast_to(b_spatial[:, :, None], (SGU_HEADS, CHUNK, SGU_HEAD_DIM))
    u2, y_b, y_c = _in_proj(h, row(g_mix), w_in.astype(BF16), row(g_sgu), w_sp, b_sp,
                            mem, row(g_mem), w_kv)
    y2 = _ssm(u2, *_ssm_params(lam_re, lam_im, log_dt, b_re, b_im, c_re, c_im, d_skip), n_seq=bsz)

    pad = LANES - N_GROUPS - N_EXPERTS
    w_rt = jnp.concatenate([w_group, w_router, jnp.zeros((d, pad), F32)], axis=1)
    w_rt_hi = w_rt.astype(BF16)
    w_rt = jnp.concatenate([w_rt_hi, (w_rt - w_rt_hi.astype(F32)).astype(BF16)], axis=1)
    b_rt =jnp.concatenate([b_group, b_router, jnp.zeros((pad,), F32)]).reshape(1, LANES)
    h2, xnp, rt, rtt, cnt = _merge_route(
        h.reshape(t, d), y2, y_b.reshape(t, -1), y_c.reshape(t, -1), row(g_mix),
        w_gate, row(b_gate), w_glu, row(b_glu), w_branch, w_out, row(g_ffn), w_rt, b_rt)

    assert BM & (BM - 1) == 0, "block padding arithmetic assumes a power-of-two block"
    counts = cnt[0, ROUTE_LANE0:ROUTE_LANE0 + N_EXPERTS].astype(jnp.int32)
    nb = (t * TOP_K) // BM + N_EXPERTS
    assert nb < 256, "slot_rows keeps block counts in bf16 matmul operands"
    nslots = nb * BM
    dest_p = _slot_rows(rtt, cnt, nslots).reshape(TOP_K, SC_SPLIT * t)
    buf = _dispatch_rows(xnp.reshape(SC_SPLIT * t, SC_ROW), dest_p[0], dest_p[1], SC_SPLIT * nslots)
    yb = _experts(counts, buf.reshape(SC_SPLIT, nslots, SC_ROW), w1, w3, w2)
    g = _gather_rows(yb.reshape(SC_SPLIT * nslots, SC_ROW), dest_p.reshape(-1))
    out = _combine(h2, g.reshape(TOP_K * SC_SPLIT, t, SC_ROW), rt, row(g_out))
    return out.reshape(bsz, s, d)


def kernel(x, mem, g_mix, g_mem, w_in, w_gate, b_gate, lam_re, lam_im, log_dt, b_re, b_im, c_re,
           c_im, d_skip, w_glu, b_glu, g_sgu, w_spatial, b_spatial, w_kv, w_branch, w_out, g_ffn,
           w_group, b_group, w_router, b_router, w1, w3, w2, g_final):
    assert g_mix.shape[0] == 1, "single-layer stack"
    return _layer(x, mem, g_mix[0], g_mem[0], w_in[0], w_gate[0], b_gate[0], lam_re[0], lam_im[0],
                  log_dt[0], b_re[0], b_im[0], c_re[0], c_im[0], d_skip[0], w_glu[0], b_glu[0],
                  g_sgu[0], w_spatial[0], b_spatial[0], w_kv[0], w_branch[0], w_out[0], g_ffn[0],
                  w_group[0], b_group[0], w_router[0], b_router[0], w1[0], w3[0], w2[0], g_final)
```

```python
import functools
import math

import jax
import jax.numpy as jnp
from jax import lax
from jax.experimental import pallas as pl
from jax.experimental.pallas import tpu as pltpu
from jax.experimental.pallas import tpu_sc as plsc

F32 = jnp.float32
BF16 = jnp.bfloat16

EPS = 1e-6
D_MODEL = 1024
SSM_WIDTH = 512
SSM_GROUP = 16
SSM_GROUPS = 32
SSM_STATE = 64
SSM_CHUNK = 16
SSM_GROUPS_PER_STEP = 1
SGU_WIDTH = 512
SGU_HEADS = 4
SGU_HEAD_DIM = 128
CHUNK = 128
XA_HEADS = 4
XA_HEAD_DIM = 128
N_GROUPS = 8
EXPERTS_PER_GROUP = 8
N_EXPERTS = 64
TOP_K = 2
D_FF = 512
LANES = 128
ROUTE_LANE0 = N_GROUPS

TM_IN = 1024
TM_MERGE = 512
MERGE_COLS = 256
TM_OUT = 1024
BM = 512
BM_STEP = 128
BLOCKS_PER_STEP = 4
WEIGHT_SLOTS = 3
SC_WINDOW = 128
SC_ROW = 256
SC_SPLIT = (D_MODEL // 2) // SC_ROW
SLOT_LANES = 2048
VMEM_LIMIT = 56 * 1024 * 1024


def _rms(x, g):
    return x * lax.rsqrt(jnp.mean(x * x, axis=-1, keepdims=True) + EPS) * g


def _sigmoid(x):
    return 0.5 * (1.0 + jnp.tanh(0.5 * x))


def _gelu(x):
    c = math.sqrt(2.0 / math.pi)
    return 0.5 * x * (1.0 + jnp.tanh(c * (x + 0.044715 * (x * x * x))))


def _dot(a, b):
    return jnp.dot(a, b, preferred_element_type=F32)


_NT = (((1,), (1,)), ((), ()))


def _pack_bf16_pair(x):
    n = x.shape[1] // 2
    lo = lax.bitcast_convert_type(x[:, :n].astype(BF16).astype(F32), jnp.uint32)
    hi = lax.bitcast_convert_type(x[:, n:].astype(BF16).astype(F32), jnp.uint32)
    return hi | (lo >> 16)


def _unpack_bf16_pair(p):
    lo = lax.bitcast_convert_type(p << 16, F32)
    hi = lax.bitcast_convert_type(p & jnp.uint32(0xFFFF0000), F32)
    return jnp.concatenate([lo, hi], axis=1)


GROUPS_PER_TILE = LANES // SSM_GROUP
POS_PER_TILE = LANES // SSM_GROUP


def _slot_masks(rows):
    lane = lax.broadcasted_iota(jnp.int32, (rows, LANES), 1)
    return [(lane >= i * SSM_GROUP) & (lane < (i + 1) * SSM_GROUP) for i in range(LANES // SSM_GROUP)]


def _tokens_to_chunks(tok_ref, out_ref):
    tm = tok_ref.shape[1]
    nc = tm // SSM_CHUNK
    masks = _slot_masks(nc)
    for k in range(SSM_WIDTH // LANES):
        for j in range(SSM_CHUNK // POS_PER_TILE):
            src = [tok_ref[k, pl.ds(j * POS_PER_TILE + p, nc, stride=SSM_CHUNK), :]
                   for p in range(POS_PER_TILE)]
            for gi in range(GROUPS_PER_TILE):
                acc = None
                for p in range(POS_PER_TILE):
                    shift = ((p - gi) * SSM_GROUP) % LANES
                    r = pltpu.roll(src[p], shift, 1) if shift else src[p]
                    acc = r if acc is None else jnp.where(masks[p], r, acc)
                out_ref[k * GROUPS_PER_TILE + gi, :, pl.ds(j * LANES, LANES)] = acc.astype(out_ref.dtype)


def _chunks_to_tokens(chunk_ref, tok_ref):
    tm = tok_ref.shape[1]
    nc = tm // SSM_CHUNK
    masks = _slot_masks(nc)
    for k in range(SSM_WIDTH // LANES):
        for j in range(SSM_CHUNK // POS_PER_TILE):
            src = [chunk_ref[k * GROUPS_PER_TILE + gi, :, pl.ds(j * LANES, LANES)].astype(F32)
                   for gi in range(GROUPS_PER_TILE)]
            for p in range(POS_PER_TILE):
                acc = None
                for gi in range(GROUPS_PER_TILE):
                    shift = ((gi - p) * SSM_GROUP) % LANES
                    r = pltpu.roll(src[gi], shift, 1) if shift else src[gi]
                    acc = r if acc is None else jnp.where(masks[gi], r, acc)
                tok_ref[k, pl.ds(j * POS_PER_TILE + p, nc, stride=SSM_CHUNK), :] = acc


def _in_body(x_ref, gmix_ref, win_ref, gsgu_ref, wsp_ref, bsp_ref, mem_ref, gmem_ref, wkv_ref,
             u2_ref, yb_ref, yc_ref, tok_ref, k_ref, v_ref):
    @pl.when(pl.program_id(1) == 0)
    def _():
        kv = _dot(_rms(mem_ref[0], gmem_ref[...]).astype(BF16), wkv_ref[...].astype(BF16))
        k_ref[...] = kv[:, :XA_HEADS * XA_HEAD_DIM].astype(BF16)
        v_ref[...] = kv[:, XA_HEADS * XA_HEAD_DIM:].astype(BF16)

    n = _rms(x_ref[0], gmix_ref[...]).astype(BF16)
    proj = _dot(n, win_ref[...])
    for k in range(SSM_WIDTH // LANES):
        tok_ref[k] = proj[:, k * LANES:(k + 1) * LANES]
    _tokens_to_chunks(tok_ref, u2_ref)

    u = _gelu(proj[:, SSM_WIDTH:SSM_WIDTH + SGU_WIDTH])
    v = _gelu(proj[:, SSM_WIDTH + SGU_WIDTH:SSM_WIDTH + 2 * SGU_WIDTH])
    v = _rms(v, gsgu_ref[...]).astype(BF16)
    tm = u.shape[0]
    rows = []
    for c in range(tm // CHUNK):
        vc = v[c * CHUNK:(c + 1) * CHUNK]
        heads = []
        for h in range(SGU_HEADS):
            sl = slice(h * SGU_HEAD_DIM, (h + 1) * SGU_HEAD_DIM)
            heads.append(_dot(wsp_ref[h], vc[:, sl]) + bsp_ref[h])
        rows.append(jnp.concatenate(heads, axis=1))
    sv = jnp.concatenate(rows, axis=0)
    yb_ref[0] = (u * sv).astype(BF16)

    q = proj[:, SSM_WIDTH + 2 * SGU_WIDTH:].astype(BF16)
    kk = k_ref[...]
    vv = v_ref[...]
    outs = []
    for h in range(XA_HEADS):
        sl = slice(h * XA_HEAD_DIM, (h + 1) * XA_HEAD_DIM)
        s = lax.dot_general(q[:, sl], kk[:, sl], (((1,), (1,)), ((), ())),
                            preferred_element_type=F32) * (XA_HEAD_DIM ** -0.5)
        e = jnp.exp(s - jnp.max(s, axis=-1, keepdims=True))
        l = jnp.sum(e, axis=-1, keepdims=True)
        outs.append(_dot(e.astype(BF16), vv[:, sl]) / l)
    yc_ref[0] = jnp.concatenate(outs, axis=1).astype(BF16)


def _in_proj(x, g_mix, w_in, g_sgu, w_sp, b_sp, mem, g_mem, w_kv):
    b, s, d = x.shape
    m = mem.shape[1]
    xa = XA_HEADS * XA_HEAD_DIM
    const2 = lambda i, j: (0, 0)
    const3 = lambda i, j: (0, 0, 0)
    tok = lambda i, j: (i, j, 0)
    per_b = lambda i, j: (i, 0, 0)
    out = jax.ShapeDtypeStruct((b, s, SSM_WIDTH), BF16)
    nc = TM_IN // SSM_CHUNK
    tiles = s // TM_IN
    u2 = jax.ShapeDtypeStruct((SSM_GROUPS, b * s // SSM_CHUNK, SSM_CHUNK * SSM_GROUP), BF16)
    return pl.pallas_call(
        _in_body,
        grid=(b, s // TM_IN),
        in_specs=[pl.BlockSpec((1, TM_IN, d), tok),
                  pl.BlockSpec((1, d), const2),
                  pl.BlockSpec(w_in.shape, const2),
                  pl.BlockSpec((1, SGU_WIDTH), const2),
                  pl.BlockSpec(w_sp.shape, const3),
                  pl.BlockSpec(b_sp.shape, const3),
                  pl.BlockSpec((1, m, d), per_b),
                  pl.BlockSpec((1, d), const2),
                  pl.BlockSpec(w_kv.shape, const2, pipeline_mode=pl.Buffered(1))],
        out_specs=[pl.BlockSpec((SSM_GROUPS, nc, SSM_CHUNK * SSM_GROUP), lambda i, j: (0, i * tiles + j, 0)),
                   pl.BlockSpec((1, TM_IN, SSM_WIDTH), tok),
                   pl.BlockSpec((1, TM_IN, SSM_WIDTH), tok)],
        out_shape=[u2, out, out],
        scratch_shapes=[pltpu.VMEM((SSM_WIDTH // LANES, TM_IN, LANES), F32),
                        pltpu.VMEM((m, xa), BF16),
                        pltpu.VMEM((m, xa), BF16)],
        compiler_params=pltpu.CompilerParams(dimension_semantics=("arbitrary", "arbitrary"),
                                             vmem_limit_bytes=VMEM_LIMIT),
        name="in_proj",
    )(x, g_mix, w_in, g_sgu, w_sp, b_sp, mem, g_mem, w_kv)


def _alternate(*stages):
    live = list(stages)
    while live:
        live = [s for s in live if next(s, True) is None]


def _ssm_params(lam_re, lam_im, log_dt, b_re, b_im, c_re, c_im, d_skip):
    g, p = lam_re.shape
    dup = lambda a: jnp.concatenate([a, a], axis=-1)
    lam = jnp.stack([dup(lam_re), dup(lam_im), jnp.broadcast_to(log_dt[:, None], (g, 2 * p))], axis=1)
    brt = b_re.transpose(0, 2, 1)
    bit = b_im.transpose(0, 2, 1)
    cat = lambda a, b: jnp.concatenate([a, b], axis=-1)
    bc = jnp.stack([cat(brt, bit), cat(bit, brt), cat(c_re, -c_im), cat(-c_im, -c_re)], axis=1)
    d2 = jnp.tile(d_skip.reshape(g, 1, SSM_GROUP), (1, 1, SSM_CHUNK))
    return lam, bc, d2


def _ssm_operators(lam_ref, bc_ref, ccat_ref, n_ref, m_ref):
    lam_re = lam_ref[0:1, :]
    lam_im = lam_ref[1:2, :]
    dt = jnp.exp(lam_ref[2:3, :])
    ar = lam_re * dt
    ai = lam_im * dt
    lane = lax.broadcasted_iota(jnp.int32, (1, LANES), 1)
    sgn = jnp.where(lane >= SSM_STATE, 1.0, -1.0)

    def powers(j):
        mag = jnp.exp(ar * j)
        ph = ai * j
        return mag * jnp.cos(ph), mag * jnp.sin(ph)

    pos = lax.broadcasted_iota(jnp.int32, (SSM_CHUNK, 1), 0).astype(F32)
    p_re, p_im = powers(pos)
    r_re, r_im = powers((SSM_CHUNK - 1) - pos)
    one_re, one_im = powers(jnp.ones((1, 1), F32))
    q_re = p_re * one_re - p_im * one_im
    q_im = p_re * one_im + p_im * one_re
    step = lax.shift_left(jnp.full((8, 1), SSM_CHUNK, jnp.int32),
                          lax.broadcasted_iota(jnp.int32, (8, 1), 0)).astype(F32)
    s_re, s_im = powers(step)

    den = lam_re * lam_re + lam_im * lam_im
    f_re = ((one_re - 1.0) * lam_re + one_im * lam_im) / den
    f_im = (one_im * lam_re - (one_re - 1.0) * lam_im) / den
    b1, b2, ca, cb = bc_ref[0], bc_ref[1], bc_ref[2], bc_ref[3]
    bb1 = f_re * b1 + (sgn * f_im) * b2
    bb2 = f_re * b2 - (sgn * f_im) * b1
    r_ims = sgn * r_im
    for s in range(SSM_CHUNK):
        blk = pl.ds(s * SSM_GROUP, SSM_GROUP)
        ccat_ref[blk, :] = ca * p_re[s:s + 1, :] + cb * p_im[s:s + 1, :]
        m_ref[blk, :] = (ca * q_re[s:s + 1, :] + cb * q_im[s:s + 1, :]).astype(m_ref.dtype)
        n_ref[blk, :] = (bb1 * r_re[s:s + 1, :] + bb2 * r_ims[s:s + 1, :]).astype(n_ref.dtype)
    return bb1, s_re, sgn * s_im


def _ssm_body(u_ref, lam_ref, bc_ref, d2_ref, y_ref, toep_ref, ccat_ref, n_ref, m_ref, *, n_seq):
    for g in range(u_ref.shape[0]):
        _ssm_group(u_ref.at[g], lam_ref.at[g], bc_ref.at[g], d2_ref.at[g], y_ref.at[g],
                   toep_ref.at[g], ccat_ref.at[g], n_ref.at[g], m_ref.at[g], n_seq)


def _ssm_group(u_ref, lam_ref, bc_ref, d2_ref, y_ref, toep_ref, ccat_ref, n_ref, m_ref, n_seq):
    bcat, lr, li = _ssm_operators(lam_ref, bc_ref, ccat_ref, n_ref, m_ref)
    kern = lax.dot_general(bcat, ccat_ref[...], _NT, precision=lax.Precision.HIGHEST,
                           preferred_element_type=F32)
    col = lax.broadcasted_iota(jnp.int32, kern.shape, 1)
    for s in range(SSM_CHUNK):
        shifted = pltpu.roll(kern, s * SSM_GROUP, 1) if s else kern
        toep_ref[s * SSM_GROUP:(s + 1) * SSM_GROUP, :] = jnp.where(
            col >= s * SSM_GROUP, shifted, 0.0).astype(BF16)

    u = u_ref[...]
    rows = u.shape[0]
    per = rows // n_seq
    y = _dot(u, toep_ref[...])
    st = _dot(u, n_ref[...])
    row = lax.broadcasted_iota(jnp.int32, (per, LANES), 0)
    prev = []
    for b in range(n_seq):
        x = st[b * per:(b + 1) * per]
        k = 0
        while (1 << k) < per:
            d = 1 << k
            sh = jnp.where(row >= d, pltpu.roll(x, d, 0), 0.0)
            x = x + sh * lr[k:k + 1, :] + pltpu.roll(sh, SSM_STATE, 1) * li[k:k + 1, :]
            k += 1
        prev.append(jnp.where(row >= 1, pltpu.roll(x, 1, 0), 0.0))
    xp = jnp.concatenate(prev, axis=0).astype(BF16)
    y = y + lax.dot_general(xp, m_ref[...], _NT, preferred_element_type=F32) + d2_ref[...] * u.astype(F32)
    y_ref[...] = _gelu(y).astype(BF16)


def _ssm(u2, lam, bc, d2, n_seq):
    g, rows, w = u2.shape
    assert rows // n_seq <= 1 << 8, "lam_bar^(16*2^k) is prepared for 8 scan steps"
    gs = SSM_GROUPS_PER_STEP
    blk = lambda a: pl.BlockSpec((gs,) + a.shape[1:], lambda i: (i,) + (0,) * (a.ndim - 1))
    return pl.pallas_call(
        functools.partial(_ssm_body, n_seq=n_seq),
        grid=(g // gs,),
        in_specs=[blk(u2), blk(lam), blk(bc), blk(d2)],
        out_specs=blk(u2),
        out_shape=jax.ShapeDtypeStruct(u2.shape, BF16),
        scratch_shapes=[pltpu.VMEM((gs, w, w), BF16),
                        pltpu.VMEM((gs, w, 2 * SSM_STATE), F32),
                        pltpu.VMEM((gs, w, 2 * SSM_STATE), BF16),
                        pltpu.VMEM((gs, w, 2 * SSM_STATE), BF16)],
        compiler_params=pltpu.CompilerParams(dimension_semantics=("arbitrary",),
                                             vmem_limit_bytes=VMEM_LIMIT),
        name="ssm",
    )(u2, lam, bc, d2)


def _mix_tile(x_ref, y2_ref, yb_ref, yc_ref, gmix_ref, wgate_ref, bgate_ref, wglu_ref, bglu_ref,
              wbr_ref, wout_ref, tok_ref, h_ref, hkeep_ref):
    x = x_ref[...]
    n = _rms(x, gmix_ref[...]).astype(BF16)

    def gated(b, c, y):
        cols = pl.ds(b * D_MODEL + c * MERGE_COLS, MERGE_COLS)
        gate = _sigmoid(_dot(n, wgate_ref[:, cols].astype(BF16)) + bgate_ref[:, cols])
        return gate * _dot(y, wbr_ref[b, :, pl.ds(c * MERGE_COLS, MERGE_COLS)].astype(BF16))

    n_blocks = D_MODEL // MERGE_COLS
    yb = yb_ref[...]
    yc = yc_ref[...]
    head = gated(1, 0, yb) + gated(2, 0, yc)
    yield
    _chunks_to_tokens(y2_ref, tok_ref)
    ys = jnp.concatenate([tok_ref[k] for k in range(SSM_WIDTH // LANES)], axis=1).astype(BF16)
    glu = _dot(ys, wglu_ref[...].astype(BF16)) + bglu_ref[...]
    ya = (glu[:, :SSM_WIDTH] * _sigmoid(glu[:, SSM_WIDTH:])).astype(BF16)
    merged = [(head + gated(0, 0, ya)).astype(BF16)]
    for c in range(1, n_blocks):
        yield
        acc = gated(0, c, ya)
        yield
        acc = acc + gated(1, c, yb)
        yield
        merged.append((acc + gated(2, c, yc)).astype(BF16))
    yield
    h = x + _dot(jnp.concatenate(merged, axis=1), wout_ref[...].astype(BF16))
    h_ref[...] = h
    hkeep_ref[...] = h


def _route_tile(hkeep_ref, gffn_ref, wrt_ref, brt_ref, xnp_ref, rt_ref, rtt_ref, cnt_ref, carry_ref):
    h = hkeep_ref[...]
    xn = _rms(h, gffn_ref[...])
    yield
    packed = _pack_bf16_pair(xn)
    for j in range(SC_SPLIT):
        xnp_ref[j] = packed[:, j * SC_ROW:(j + 1) * SC_ROW]
    yield

    x_hi = xn.astype(BF16)
    x_lo = (xn - x_hi.astype(F32)).astype(BF16)
    head = _dot(x_hi, wrt_ref[...])
    logits = (head[:, :LANES] + head[:, LANES:] + _dot(x_lo, wrt_ref[:, :LANES])) + brt_ref[...]
    yield
    tm = logits.shape[0]
    lane_i = lax.broadcasted_iota(jnp.int32, (tm, LANES), 1)
    lane = lane_i.astype(F32)
    neg = jnp.float32(-3.0e38)
    big = jnp.float32(LANES)
    gmask = lane_i < N_GROUPS
    gl = jnp.where(gmask, logits, neg)
    gmax = jnp.max(gl, axis=-1, keepdims=True)
    gidx = jnp.min(jnp.where(gl == gmax, lane, big), axis=-1, keepdims=True)
    gsum = jnp.sum(jnp.where(gmask, jnp.exp(gl - gmax), 0.0), axis=-1, keepdims=True)
    g_w = 1.0 / gsum
    yield
    e_lane = lane_i - ROUTE_LANE0
    lane_group = (e_lane >> 3).astype(F32)
    emask = (e_lane >= 0) & (e_lane < N_EXPERTS) & (lane_group == gidx)
    el = jnp.where(emask, logits, neg)
    m1 = jnp.max(el, axis=-1, keepdims=True)
    i1 = jnp.min(jnp.where(el == m1, lane, big), axis=-1, keepdims=True)
    yield
    el2 = jnp.where(lane == i1, neg, el)
    m2 = jnp.max(el2, axis=-1, keepdims=True)
    i2 = jnp.min(jnp.where(el2 == m2, lane, big), axis=-1, keepdims=True)
    t = jnp.exp(m2 - m1)
    w1 = g_w / (1.0 + t)
    w2 = g_w * t / (1.0 + t)
    yield

    sel1 = lane == i1
    sel2 = lane == i2
    onehot = jnp.where(sel1 | sel2, 1.0, 0.0)
    r_i = lax.broadcasted_iota(jnp.int32, (tm, tm), 0)
    c_i = lax.broadcasted_iota(jnp.int32, (tm, tm), 1)
    stril = jnp.where(c_i < r_i, 1.0, 0.0).astype(BF16)
    cum = _dot(stril, onehot.astype(BF16)) + carry_ref[0:1, :]
    rank1 = jnp.sum(jnp.where(sel1, cum, 0.0), axis=-1, keepdims=True)
    rank2 = jnp.sum(jnp.where(sel2, cum, 0.0), axis=-1, keepdims=True)
    carry_ref[...] = carry_ref[...] + jnp.sum(onehot, axis=0, keepdims=True)
    cnt_ref[...] = carry_ref[...]
    yield

    cols = (i1 - ROUTE_LANE0, i2 - ROUTE_LANE0, rank1, rank2, w1, w2)
    rt = jnp.zeros((tm, LANES), F32)
    for c, val in enumerate(cols):
        rt = jnp.where(lane_i == c, val, rt)
    rt_ref[...] = rt
    rtt_ref[...] = rt.T[:8]


def _merge_body(x_ref, y2_ref, yb_ref, yc_ref, gmix_ref, wgate_ref, bgate_ref, wglu_ref, bglu_ref,
                wbr_ref, wout_ref, gffn_ref, wrt_ref, brt_ref,
                h_ref, xnp_ref, rt_ref, rtt_ref, cnt_ref, carry_ref, tok_ref, hkeep_ref):
    i = pl.program_id(0)
    last = pl.num_programs(0) - 1
    cur = hkeep_ref.at[i % 2]
    prev = hkeep_ref.at[(i + 1) % 2]

    def mix():
        return _mix_tile(x_ref, y2_ref, yb_ref, yc_ref, gmix_ref, wgate_ref, bgate_ref, wglu_ref,
                         bglu_ref, wbr_ref, wout_ref, tok_ref, h_ref, cur)

    def route():
        return _route_tile(prev, gffn_ref, wrt_ref, brt_ref, xnp_ref, rt_ref, rtt_ref, cnt_ref, carry_ref)

    @pl.when(i == 0)
    def _():
        carry_ref[...] = jnp.zeros_like(carry_ref)
        _alternate(mix())

    @pl.when((i > 0) & (i < last))
    def _():
        _alternate(route(), mix())

    @pl.when(i == last)
    def _():
        _alternate(route())


def _merge_route(x, y2, yb, yc, g_mix, w_gate, b_gate, w_glu, b_glu, w_br, w_out, g_ffn, w_rt, b_rt):
    t, d = x.shape
    tm = TM_MERGE
    tiles = t // tm
    mixed = lambda i: jnp.minimum(i, tiles - 1)
    routed = lambda i: jnp.maximum(i - 1, 0)
    c2 = lambda i: (0, 0)
    c3 = lambda i: (0, 0, 0)
    full = lambda a: pl.BlockSpec(a.shape, c2 if a.ndim == 2 else c3, pipeline_mode=pl.Buffered(1))
    return pl.pallas_call(
        _merge_body,
        grid=(tiles + 1,),
        in_specs=[pl.BlockSpec((tm, d), lambda i: (mixed(i), 0)),
                  pl.BlockSpec((SSM_GROUPS, tm // SSM_CHUNK, SSM_CHUNK * SSM_GROUP),
                               lambda i: (0, mixed(i), 0)),
                  pl.BlockSpec((tm, SSM_WIDTH), lambda i: (mixed(i), 0)),
                  pl.BlockSpec((tm, SSM_WIDTH), lambda i: (mixed(i), 0)),
                  full(g_mix), full(w_gate), full(b_gate), full(w_glu), full(b_glu),
                  full(w_br), full(w_out), full(g_ffn), full(w_rt), full(b_rt)],
        out_specs=[pl.BlockSpec((tm, d), lambda i: (mixed(i), 0)),
                   pl.BlockSpec((SC_SPLIT, tm, SC_ROW), lambda i: (0, routed(i), 0)),
                   pl.BlockSpec((tm, LANES), lambda i: (routed(i), 0)),
                   pl.BlockSpec((8, tm), lambda i: (0, routed(i))),
                   pl.BlockSpec((8, LANES), c2)],
        out_shape=[jax.ShapeDtypeStruct((t, d), F32),
                   jax.ShapeDtypeStruct((SC_SPLIT, t, SC_ROW), jnp.uint32),
                   jax.ShapeDtypeStruct((t, LANES), F32),
                   jax.ShapeDtypeStruct((8, t), F32),
                   jax.ShapeDtypeStruct((8, LANES), F32)],
        scratch_shapes=[pltpu.VMEM((8, LANES), F32),
                        pltpu.VMEM((SSM_WIDTH // LANES, tm, LANES), F32),
                        pltpu.VMEM((2, tm, d), F32)],
        compiler_params=pltpu.CompilerParams(dimension_semantics=("arbitrary",),
                                             vmem_limit_bytes=VMEM_LIMIT),
        name="merge_route",
    )(x, y2, yb, yc, g_mix, w_gate, b_gate, w_glu, b_glu, w_br, w_out, g_ffn, w_rt, b_rt)


def _slot_body(rtt_ref, cnt_ref, out_ref, *, nslots):
    tl = rtt_ref.shape[1]
    blocks = jnp.ceil(cnt_ref[...] * (1.0 / BM))
    k_i = lax.broadcasted_iota(jnp.int32, (LANES, LANES), 0)
    l_i = lax.broadcasted_iota(jnp.int32, (LANES, LANES), 1)
    before = jnp.where(k_i < l_i, 1.0, 0.0).astype(BF16)
    first_blk = _dot(blocks.astype(BF16), before).astype(BF16)
    lane_of = lax.broadcasted_iota(jnp.int32, (LANES, tl), 0).astype(F32) - ROUTE_LANE0
    for k in range(TOP_K):
        onehot = jnp.where(rtt_ref[k:k + 1, :] == lane_of, 1.0, 0.0).astype(BF16)
        start = _dot(first_blk, onehot) * BM
        slot = (start[0:1, :] + rtt_ref[TOP_K + k:TOP_K + k + 1, :]).astype(jnp.int32)
        for j in range(SC_SPLIT):
            out_ref[k * SC_SPLIT + j:k * SC_SPLIT + j + 1, :] = slot + j * nslots


def _slot_rows(rtt, cnt, nslots):
    t = rtt.shape[1]
    tl = SLOT_LANES
    return pl.pallas_call(
        functools.partial(_slot_body, nslots=nslots),
        grid=(t // tl,),
        in_specs=[pl.BlockSpec((8, tl), lambda i: (0, i)),
                  pl.BlockSpec((8, LANES), lambda i: (0, 0))],
        out_specs=pl.BlockSpec((TOP_K * SC_SPLIT, tl), lambda i: (0, i)),
        out_shape=jax.ShapeDtypeStruct((TOP_K * SC_SPLIT, t), jnp.int32),
        compiler_params=pltpu.CompilerParams(dimension_semantics=("arbitrary",)),
        name="slot_rows",
    )(rtt, cnt)


def _plan_blocks(counts_ref, blk_expert, blk_valid, blk_first, blk_run, run_expert):
    nb = blk_expert.shape[0]

    def per_expert(e, carry):
        cursor, run = carry
        count = counts_ref[e]
        n_blk = (count + (BM - 1)) // BM

        def per_block(b, cur):
            blk_expert[cur] = e
            blk_valid[cur] = jnp.minimum(count - b * BM, BM)
            blk_first[cur] = (b == 0).astype(jnp.int32)
            blk_run[cur] = run
            return cur + 1

        run_expert[run] = e
        return lax.fori_loop(0, n_blk, per_block, cursor), run + (n_blk > 0).astype(jnp.int32)

    cursor, runs = lax.fori_loop(0, counts_ref.shape[0], per_expert, (jnp.int32(0), jnp.int32(0)))
    run_expert[runs] = -1
    run_expert[runs + 1] = -1

    def empty(j, carry):
        blk_expert[j] = 0
        blk_valid[j] = 0
        blk_first[j] = 0
        blk_run[j] = runs - 1
        return carry

    lax.fori_loop(cursor, nb, empty, 0)


def _expert_body(counts_ref, buf_ref, w1_hbm, w3_hbm, w2_hbm, out_ref, w1_buf, w3_buf, w2_buf, sem,
                 blk_expert, blk_valid, blk_first, blk_run, run_expert):
    step = pl.program_id(0)

    @pl.when(step == 0)
    def _():
        _plan_blocks(counts_ref, blk_expert, blk_valid, blk_first, blk_run, run_expert)

    def weight_copies(e, s):
        return (pltpu.make_async_copy(w1_hbm.at[e], w1_buf.at[s], sem.at[s, 0]),
                pltpu.make_async_copy(w3_hbm.at[e], w3_buf.at[s], sem.at[s, 1]),
                pltpu.make_async_copy(w2_hbm.at[e], w2_buf.at[s], sem.at[s, 2]))

    for sub in range(BLOCKS_PER_STEP):
        i = step * BLOCKS_PER_STEP + sub
        block = pl.ds(sub * BM, BM)
        _expert_block(i, buf_ref.at[:, block, :], out_ref.at[:, block, :], weight_copies,
                      (w1_buf, w3_buf, w2_buf), blk_expert, blk_valid, blk_first, blk_run, run_expert)


def _expert_block(i, buf_ref, out_ref, weight_copies, weights, blk_expert, blk_valid, blk_first, blk_run,
                  run_expert):
    w1_buf, w3_buf, w2_buf = weights
    expert = blk_expert[i]
    valid = blk_valid[i]
    run = blk_run[i]
    slot = run % WEIGHT_SLOTS
    ahead1 = run_expert[run + 1]
    ahead2 = run_expert[run + 2]

    @pl.when(i == 0)
    def _():
        for c in weight_copies(expert, slot):
            c.start()

        @pl.when(ahead1 >= 0)
        def _():
            for c in weight_copies(ahead1, (slot + 1) % WEIGHT_SLOTS):
                c.start()

    @pl.when(blk_first[i] == 1)
    def _():
        for c in weight_copies(expert, slot):
            c.wait()

        @pl.when(ahead2 >= 0)
        def _():
            for c in weight_copies(ahead2, (slot + 2) % WEIGHT_SLOTS):
                c.start()

    def mlp(rows):
        x = _unpack_bf16_pair(jnp.concatenate([buf_ref[j, :rows, :] for j in range(SC_SPLIT)], axis=1))
        row = lax.broadcasted_iota(jnp.int32, x.shape, 0)
        x = jnp.where(row < valid, x, 0.0).astype(BF16)
        h1 = _dot(x, w1_buf[slot].astype(BF16))
        h3 = _dot(x, w3_buf[slot].astype(BF16))
        a = (h1 * _sigmoid(h1) * h3).astype(BF16)
        packed = _pack_bf16_pair(_dot(a, w2_buf[slot].astype(BF16)))
        for j in range(SC_SPLIT):
            out_ref[j, :rows, :] = packed[:, j * SC_ROW:(j + 1) * SC_ROW]
            if rows < BM:
                out_ref[j, rows:, :] = jnp.zeros((BM - rows, SC_ROW), out_ref.dtype)

    for rows in range(BM_STEP, BM + 1, BM_STEP):
        @pl.when((valid > rows - BM_STEP) & (valid <= rows))
        def _():
            mlp(rows)

    @pl.when(valid <= 0)
    def _():
        out_ref[...] = jnp.zeros_like(out_ref)


def _experts(counts, buf, w1, w3, w2):
    _, nslots, _ = buf.shape
    nb = nslots // BM
    rows = pl.BlockSpec((SC_SPLIT, BLOCKS_PER_STEP * BM, SC_ROW), lambda i, counts: (0, i, 0))
    hbm = pl.BlockSpec(memory_space=pl.ANY)
    table = pltpu.SMEM((nb,), jnp.int32)
    grid_spec = pltpu.PrefetchScalarGridSpec(
        num_scalar_prefetch=1,
        grid=(nb // BLOCKS_PER_STEP,),
        in_specs=[rows, hbm, hbm, hbm],
        out_specs=rows,
        scratch_shapes=[pltpu.VMEM((WEIGHT_SLOTS,) + w1.shape[1:], w1.dtype),
                        pltpu.VMEM((WEIGHT_SLOTS,) + w3.shape[1:], w3.dtype),
                        pltpu.VMEM((WEIGHT_SLOTS,) + w2.shape[1:], w2.dtype),
                        pltpu.SemaphoreType.DMA((WEIGHT_SLOTS, 3)),
                        table, table, table, table,
                        pltpu.SMEM((counts.shape[0] + 2,), jnp.int32)],
    )
    return pl.pallas_call(
        _expert_body,
        grid_spec=grid_spec,
        out_shape=jax.ShapeDtypeStruct(buf.shape, jnp.uint32),
        compiler_params=pltpu.CompilerParams(dimension_semantics=("arbitrary",),
                                             vmem_limit_bytes=VMEM_LIMIT),
        name="experts",
    )(counts, buf, w1, w3, w2)


def _sc_mesh():
    return plsc.VectorSubcoreMesh(core_axis_name="core", subcore_axis_name="subcore")


def _dispatch_rows(rows, dest0, dest1, nslots):
    t, w = rows.shape
    win = SC_WINDOW
    idx_spec = pl.BlockSpec((1, win), lambda i: (0, i))

    @functools.partial(pl.kernel, mesh=_sc_mesh(), scratch_types=[],
                       out_type=jax.ShapeDtypeStruct((nslots, w), rows.dtype), name="dispatch_rows")
    def run(rows_hbm, i0_hbm, i1_hbm, out_hbm):
        def body(rows_vmem, i0_vmem, i1_vmem):
            pltpu.sync_copy(rows_vmem, out_hbm.at[i0_vmem.at[0]])
            pltpu.sync_copy(rows_vmem, out_hbm.at[i1_vmem.at[0]])

        pltpu.emit_pipeline(
            body, grid=(t // win,),
            in_specs=[pl.BlockSpec((win, w), lambda i: (i, 0)), idx_spec, idx_spec],
            out_specs=[],
            core_axis_name=("core", "subcore"),
            dimension_semantics=(pltpu.PARALLEL,),
        )(rows_hbm, i0_hbm, i1_hbm)

    return run(rows, dest0.reshape(1, t), dest1.reshape(1, t))


def _gather_rows(table, idx):
    n = idx.shape[0]
    w = table.shape[1]
    win = SC_WINDOW

    @functools.partial(pl.kernel, mesh=_sc_mesh(), scratch_types=[],
                       out_type=jax.ShapeDtypeStruct((n, w), table.dtype), name="gather_rows")
    def run(table_hbm, i_hbm, out_hbm):
        def body(i_vmem, out_vmem):
            pltpu.sync_copy(table_hbm.at[i_vmem.at[0]], out_vmem)

        pltpu.emit_pipeline(
            body, grid=(n // win,),
            in_specs=[pl.BlockSpec((1, win), lambda i: (0, i))],
            out_specs=[pl.BlockSpec((win, w), lambda i: (i, 0))],
            core_axis_name=("core", "subcore"),
            dimension_semantics=(pltpu.PARALLEL,),
        )(i_hbm, out_hbm)

    return run(table, idx.reshape(1, n))


def _combine_body(h_ref, g_ref, rt_ref, gfin_ref, out_ref):
    rt = rt_ref[...]
    y = h_ref[...]
    for k in range(TOP_K):
        rows = jnp.concatenate([g_ref[k * SC_SPLIT + j] for j in range(SC_SPLIT)], axis=1)
        y = y + rt[:, 4 + k:5 + k] * _unpack_bf16_pair(rows)
    out_ref[...] = _rms(y, gfin_ref[...])


def _combine(h, g, rt, g_final):
    t, d = h.shape
    tm = TM_OUT
    tok = lambda i: (i, 0)
    return pl.pallas_call(
        _combine_body,
        grid=(t // tm,),
        in_specs=[pl.BlockSpec((tm, d), tok),
                  pl.BlockSpec((TOP_K * SC_SPLIT, tm, SC_ROW), lambda i: (0, i, 0)),
                  pl.BlockSpec((tm, LANES), tok),
                  pl.BlockSpec((1, d), lambda i: (0, 0))],
        out_specs=pl.BlockSpec((tm, d), tok),
        out_shape=jax.ShapeDtypeStruct((t, d), F32),
        compiler_params=pltpu.CompilerParams(dimension_semantics=("arbitrary",),
                                             vmem_limit_bytes=VMEM_LIMIT),
        name="combine",
    )(h, g, rt, g_final)


def _layer(h, mem, g_mix, g_mem, w_in, w_gate, b_gate, lam_re, lam_im, log_dt, b_re, b_im,
           c_re, c_im, d_skip, w_glu, b_glu, g_sgu, w_spatial, b_spatial, w_kv, w_branch,
           w_out, g_ffn, w_group, b_group, w_router, b_router, w1, w3, w2, g_out):
    bsz, s, d = h.shape
    t = bsz * s
    row = lambda a: a.reshape(1, -1)

    tril = jnp.tril(jnp.ones((CHUNK, CHUNK), dtype=bool))
    w_sp = jnp.where(tril, w_spatial, 0.0).astype(BF16)
    b_sp = jnp.broadcast_to(b_spatial[:, :, None], (SGU_HEADS, CHUNK, SGU_HEAD_DIM))
    u2, y_b, y_c = _in_proj(h, row(g_mix), w_in.astype(BF16), row(g_sgu), w_sp, b_sp,
                            mem, row(g_mem), w_kv)
    y2 = _ssm(u2, *_ssm_params(lam_re, lam_im, log_dt, b_re, b_im, c_re, c_im, d_skip), n_seq=bsz)

    pad = LANES - N_GROUPS - N_EXPERTS
    w_rt = jnp.concatenate([w_group, w_router, jnp.zeros((d, pad), F32)], axis=1)
    w_rt_hi = w_rt.astype(BF16)
    w_rt = jnp.concatenate([w_rt_hi, (w_rt - w_rt_hi.astype(F32)).astype(BF16)], axis=1)
    b_rt =jnp.concatenate([b_group, b_router, jnp.zeros((pad,), F32)]).reshape(1, LANES)
    h2, xnp, rt, rtt, cnt = _merge_route(
        h.reshape(t, d), y2, y_b.reshape(t, -1), y_c.reshape(t, -1), row(g_mix),
        w_gate, row(b_gate), w_glu, row(b_glu), w_branch, w_out, row(g_ffn), w_rt, b_rt)

    assert BM & (BM - 1) == 0, "block padding arithmetic assumes a power-of-two block"
    counts = cnt[0, ROUTE_LANE0:ROUTE_LANE0 + N_EXPERTS].astype(jnp.int32)
    nb = (t * TOP_K) // BM + N_EXPERTS
    assert nb < 256, "slot_rows keeps block counts in bf16 matmul operands"
    nslots = nb * BM
    dest_p = _slot_rows(rtt, cnt, nslots).reshape(TOP_K, SC_SPLIT * t)
    buf = _dispatch_rows(xnp.reshape(SC_SPLIT * t, SC_ROW), dest_p[0], dest_p[1], SC_SPLIT * nslots)
    yb = _experts(counts, buf.reshape(SC_SPLIT, nslots, SC_ROW), w1, w3, w2)
    g = _gather_rows(yb.reshape(SC_SPLIT * nslots, SC_ROW), dest_p.reshape(-1))
    out = _combine(h2, g.reshape(TOP_K * SC_SPLIT, t, SC_ROW), rt, row(g_out))
    return out.reshape(bsz, s, d)


def kernel(x, mem, g_mix, g_mem, w_in, w_gate, b_gate, lam_re, lam_im, log_dt, b_re, b_im, c_re,
           c_im, d_skip, w_glu, b_glu, g_sgu, w_spatial, b_spatial, w_kv, w_branch, w_out, g_ffn,
           w_group, b_group, w_router, b_router, w1, w3, w2, g_final):
    assert g_mix.shape[0] == 1, "single-layer stack"
    return _layer(x, mem, g_mix[0], g_mem[0], w_in[0], w_gate[0], b_gate[0], lam_re[0], lam_im[0],
                  log_dt[0], b_re[0], b_im[0], c_re[0], c_im[0], d_skip[0], w_glu[0], b_glu[0],
                  g_sgu[0], w_spatial[0], b_spatial[0], w_kv[0], w_branch[0], w_out[0], g_ffn[0],
                  w_group[0], b_group[0], w_router[0], b_router[0], w1[0], w3[0], w2[0], g_final)
```

```python
import functools
import math

import jax
import jax.numpy as jnp
from jax import lax
from jax.experimental import pallas as pl
from jax.experimental.pallas import tpu as pltpu
from jax.experimental.pallas import tpu_sc as plsc

F32 = jnp.float32
BF16 = jnp.bfloat16

EPS = 1e-6
D_MODEL = 1024
SSM_WIDTH = 512
SSM_GROUP = 16
SSM_GROUPS = 32
SSM_STATE = 64
SSM_CHUNK = 16
SSM_GROUPS_PER_STEP = 1
SGU_WIDTH = 512
SGU_HEADS = 4
SGU_HEAD_DIM = 128
CHUNK = 128
XA_HEADS = 4
XA_HEAD_DIM = 128
N_GROUPS = 8
EXPERTS_PER_GROUP = 8
N_EXPERTS = 64
TOP_K = 2
D_FF = 512
LANES = 128
ROUTE_LANE0 = N_GROUPS

TM_IN = 1024
TM_MERGE = 512
MERGE_COLS = 256
TM_OUT = 1024
BM = 512
BM_STEP = 128
BLOCKS_PER_STEP = 2
WEIGHT_SLOTS = 3
SC_WINDOW = 128
SC_ROW = 256
SC_SPLIT = (D_MODEL // 2) // SC_ROW
SLOT_LANES = 2048
VMEM_LIMIT = 56 * 1024 * 1024


def _rms(x, g):
    return x * lax.rsqrt(jnp.mean(x * x, axis=-1, keepdims=True) + EPS) * g


def _sigmoid(x):
    return 0.5 * (1.0 + jnp.tanh(0.5 * x))


def _gelu(x):
    c = math.sqrt(2.0 / math.pi)
    return 0.5 * x * (1.0 + jnp.tanh(c * (x + 0.044715 * (x * x * x))))


def _dot(a, b):
    return jnp.dot(a, b, preferred_element_type=F32)


_NT = (((1,), (1,)), ((), ()))


def _pack_bf16_pair(x):
    n = x.shape[1] // 2
    lo = lax.bitcast_convert_type(x[:, :n].astype(BF16).astype(F32), jnp.uint32)
    hi = lax.bitcast_convert_type(x[:, n:].astype(BF16).astype(F32), jnp.uint32)
    return hi | (lo >> 16)


def _unpack_bf16_pair(p):
    lo = lax.bitcast_convert_type(p << 16, F32)
    hi = lax.bitcast_convert_type(p & jnp.uint32(0xFFFF0000), F32)
    return jnp.concatenate([lo, hi], axis=1)


GROUPS_PER_TILE = LANES // SSM_GROUP
POS_PER_TILE = LANES // SSM_GROUP


def _slot_masks(rows):
    lane = lax.broadcasted_iota(jnp.int32, (rows, LANES), 1)
    return [(lane >= i * SSM_GROUP) & (lane < (i + 1) * SSM_GROUP) for i in range(LANES // SSM_GROUP)]


def _tokens_to_chunks(tok_ref, out_ref):
    tm = tok_ref.shape[1]
    nc = tm // SSM_CHUNK
    masks = _slot_masks(nc)
    for k in range(SSM_WIDTH // LANES):
        for j in range(SSM_CHUNK // POS_PER_TILE):
            src = [tok_ref[k, pl.ds(j * POS_PER_TILE + p, nc, stride=SSM_CHUNK), :]
                   for p in range(POS_PER_TILE)]
            for gi in range(GROUPS_PER_TILE):
                acc = None
                for p in range(POS_PER_TILE):
                    shift = ((p - gi) * SSM_GROUP) % LANES
                    r = pltpu.roll(src[p], shift, 1) if shift else src[p]
                    acc = r if acc is None else jnp.where(masks[p], r, acc)
                out_ref[k * GROUPS_PER_TILE + gi, :, pl.ds(j * LANES, LANES)] = acc.astype(out_ref.dtype)


def _chunks_to_tokens(chunk_ref, tok_ref):
    tm = tok_ref.shape[1]
    nc = tm // SSM_CHUNK
    masks = _slot_masks(nc)
    for k in range(SSM_WIDTH // LANES):
        for j in range(SSM_CHUNK // POS_PER_TILE):
            src = [chunk_ref[k * GROUPS_PER_TILE + gi, :, pl.ds(j * LANES, LANES)].astype(F32)
                   for gi in range(GROUPS_PER_TILE)]
            for p in range(POS_PER_TILE):
                acc = None
                for gi in range(GROUPS_PER_TILE):
                    shift = ((gi - p) * SSM_GROUP) % LANES
                    r = pltpu.roll(src[gi], shift, 1) if shift else src[gi]
                    acc = r if acc is None else jnp.where(masks[gi], r, acc)
                tok_ref[k, pl.ds(j * POS_PER_TILE + p, nc, stride=SSM_CHUNK), :] = acc


def _in_body(x_ref, gmix_ref, win_ref, gsgu_ref, wsp_ref, bsp_ref, mem_ref, gmem_ref, wkv_ref,
             u2_ref, yb_ref, yc_ref, tok_ref, k_ref, v_ref):
    @pl.when(pl.program_id(1) == 0)
    def _():
        kv = _dot(_rms(mem_ref[0], gmem_ref[...]).astype(BF16), wkv_ref[...].astype(BF16))
        k_ref[...] = kv[:, :XA_HEADS * XA_HEAD_DIM].astype(BF16)
        v_ref[...] = kv[:, XA_HEADS * XA_HEAD_DIM:].astype(BF16)

    n = _rms(x_ref[0], gmix_ref[...]).astype(BF16)
    proj = _dot(n, win_ref[...])
    for k in range(SSM_WIDTH // LANES):
        tok_ref[k] = proj[:, k * LANES:(k + 1) * LANES]
    _tokens_to_chunks(tok_ref, u2_ref)

    u = _gelu(proj[:, SSM_WIDTH:SSM_WIDTH + SGU_WIDTH])
    v = _gelu(proj[:, SSM_WIDTH + SGU_WIDTH:SSM_WIDTH + 2 * SGU_WIDTH])
    v = _rms(v, gsgu_ref[...]).astype(BF16)
    tm = u.shape[0]
    rows = []
    for c in range(tm // CHUNK):
        vc = v[c * CHUNK:(c + 1) * CHUNK]
        heads = []
        for h in range(SGU_HEADS):
            sl = slice(h * SGU_HEAD_DIM, (h + 1) * SGU_HEAD_DIM)
            heads.append(_dot(wsp_ref[h], vc[:, sl]) + bsp_ref[h])
        rows.append(jnp.concatenate(heads, axis=1))
    sv = jnp.concatenate(rows, axis=0)
    yb_ref[0] = (u * sv).astype(BF16)

    q = proj[:, SSM_WIDTH + 2 * SGU_WIDTH:].astype(BF16)
    kk = k_ref[...]
    vv = v_ref[...]
    outs = []
    for h in range(XA_HEADS):
        sl = slice(h * XA_HEAD_DIM, (h + 1) * XA_HEAD_DIM)
        s = lax.dot_general(q[:, sl], kk[:, sl], (((1,), (1,)), ((), ())),
                            preferred_element_type=F32) * (XA_HEAD_DIM ** -0.5)
        e = jnp.exp(s - jnp.max(s, axis=-1, keepdims=True))
        l = jnp.sum(e, axis=-1, keepdims=True)
        outs.append(_dot(e.astype(BF16), vv[:, sl]) / l)
    yc_ref[0] = jnp.concatenate(outs, axis=1).astype(BF16)


def _in_proj(x, g_mix, w_in, g_sgu, w_sp, b_sp, mem, g_mem, w_kv):
    b, s, d = x.shape
    m = mem.shape[1]
    xa = XA_HEADS * XA_HEAD_DIM
    const2 = lambda i, j: (0, 0)
    const3 = lambda i, j: (0, 0, 0)
    tok = lambda i, j: (i, j, 0)
    per_b = lambda i, j: (i, 0, 0)
    out = jax.ShapeDtypeStruct((b, s, SSM_WIDTH), BF16)
    nc = TM_IN // SSM_CHUNK
    tiles = s // TM_IN
    u2 = jax.ShapeDtypeStruct((SSM_GROUPS, b * s // SSM_CHUNK, SSM_CHUNK * SSM_GROUP), BF16)
    return pl.pallas_call(
        _in_body,
        grid=(b, s // TM_IN),
        in_specs=[pl.BlockSpec((1, TM_IN, d), tok),
                  pl.BlockSpec((1, d), const2),
                  pl.BlockSpec(w_in.shape, const2),
                  pl.BlockSpec((1, SGU_WIDTH), const2),
                  pl.BlockSpec(w_sp.shape, const3),
                  pl.BlockSpec(b_sp.shape, const3),
                  pl.BlockSpec((1, m, d), per_b),
                  pl.BlockSpec((1, d), const2),
                  pl.BlockSpec(w_kv.shape, const2, pipeline_mode=pl.Buffered(1))],
        out_specs=[pl.BlockSpec((SSM_GROUPS, nc, SSM_CHUNK * SSM_GROUP), lambda i, j: (0, i * tiles + j, 0)),
                   pl.BlockSpec((1, TM_IN, SSM_WIDTH), tok),
                   pl.BlockSpec((1, TM_IN, SSM_WIDTH), tok)],
        out_shape=[u2, out, out],
        scratch_shapes=[pltpu.VMEM((SSM_WIDTH // LANES, TM_IN, LANES), F32),
                        pltpu.VMEM((m, xa), BF16),
                        pltpu.VMEM((m, xa), BF16)],
        compiler_params=pltpu.CompilerParams(dimension_semantics=("arbitrary", "arbitrary"),
                                             vmem_limit_bytes=VMEM_LIMIT),
        name="in_proj",
    )(x, g_mix, w_in, g_sgu, w_sp, b_sp, mem, g_mem, w_kv)


def _alternate(*stages):
    live = list(stages)
    while live:
        live = [s for s in live if next(s, True) is None]


def _ssm_params(lam_re, lam_im, log_dt, b_re, b_im, c_re, c_im, d_skip):
    g, p = lam_re.shape
    dup = lambda a: jnp.concatenate([a, a], axis=-1)
    lam = jnp.stack([dup(lam_re), dup(lam_im), jnp.broadcast_to(log_dt[:, None], (g, 2 * p))], axis=1)
    brt = b_re.transpose(0, 2, 1)
    bit = b_im.transpose(0, 2, 1)
    cat = lambda a, b: jnp.concatenate([a, b], axis=-1)
    bc = jnp.stack([cat(brt, bit), cat(bit, brt), cat(c_re, -c_im), cat(-c_im, -c_re)], axis=1)
    d2 = jnp.tile(d_skip.reshape(g, 1, SSM_GROUP), (1, 1, SSM_CHUNK))
    return lam, bc, d2


def _ssm_operators(lam_ref, bc_ref, ccat_ref, n_ref, m_ref):
    lam_re = lam_ref[0:1, :]
    lam_im = lam_ref[1:2, :]
    dt = jnp.exp(lam_ref[2:3, :])
    ar = lam_re * dt
    ai = lam_im * dt
    lane = lax.broadcasted_iota(jnp.int32, (1, LANES), 1)
    sgn = jnp.where(lane >= SSM_STATE, 1.0, -1.0)

    def powers(j):
        mag = jnp.exp(ar * j)
        ph = ai * j
        return mag * jnp.cos(ph), mag * jnp.sin(ph)

    pos = lax.broadcasted_iota(jnp.int32, (SSM_CHUNK, 1), 0).astype(F32)
    p_re, p_im = powers(pos)
    r_re, r_im = powers((SSM_CHUNK - 1) - pos)
    one_re, one_im = powers(jnp.ones((1, 1), F32))
    q_re = p_re * one_re - p_im * one_im
    q_im = p_re * one_im + p_im * one_re
    step = lax.shift_left(jnp.full((8, 1), SSM_CHUNK, jnp.int32),
                          lax.broadcasted_iota(jnp.int32, (8, 1), 0)).astype(F32)
    s_re, s_im = powers(step)

    den = lam_re * lam_re + lam_im * lam_im
    f_re = ((one_re - 1.0) * lam_re + one_im * lam_im) / den
    f_im = (one_im * lam_re - (one_re - 1.0) * lam_im) / den
    b1, b2, ca, cb = bc_ref[0], bc_ref[1], bc_ref[2], bc_ref[3]
    bb1 = f_re * b1 + (sgn * f_im) * b2
    bb2 = f_re * b2 - (sgn * f_im) * b1
    r_ims = sgn * r_im
    for s in range(SSM_CHUNK):
        blk = pl.ds(s * SSM_GROUP, SSM_GROUP)
        ccat_ref[blk, :] = ca * p_re[s:s + 1, :] + cb * p_im[s:s + 1, :]
        m_ref[blk, :] = (ca * q_re[s:s + 1, :] + cb * q_im[s:s + 1, :]).astype(m_ref.dtype)
        n_ref[blk, :] = (bb1 * r_re[s:s + 1, :] + bb2 * r_ims[s:s + 1, :]).astype(n_ref.dtype)
    return bb1, s_re, sgn * s_im


def _ssm_body(u_ref, lam_ref, bc_ref, d2_ref, y_ref, toep_ref, ccat_ref, n_ref, m_ref, *, n_seq):
    for g in range(u_ref.shape[0]):
        _ssm_group(u_ref.at[g], lam_ref.at[g], bc_ref.at[g], d2_ref.at[g], y_ref.at[g],
                   toep_ref.at[g], ccat_ref.at[g], n_ref.at[g], m_ref.at[g], n_seq)


def _ssm_group(u_ref, lam_ref, bc_ref, d2_ref, y_ref, toep_ref, ccat_ref, n_ref, m_ref, n_seq):
    bcat, lr, li = _ssm_operators(lam_ref, bc_ref, ccat_ref, n_ref, m_ref)
    kern = lax.dot_general(bcat, ccat_ref[...], _NT, precision=lax.Precision.HIGHEST,
                           preferred_element_type=F32)
    col = lax.broadcasted_iota(jnp.int32, kern.shape, 1)
    for s in range(SSM_CHUNK):
        shifted = pltpu.roll(kern, s * SSM_GROUP, 1) if s else kern
        toep_ref[s * SSM_GROUP:(s + 1) * SSM_GROUP, :] = jnp.where(
            col >= s * SSM_GROUP, shifted, 0.0).astype(BF16)

    u = u_ref[...]
    rows = u.shape[0]
    per = rows // n_seq
    y = _dot(u, toep_ref[...])
    st = _dot(u, n_ref[...])
    row = lax.broadcasted_iota(jnp.int32, (per, LANES), 0)
    prev = []
    for b in range(n_seq):
        x = st[b * per:(b + 1) * per]
        k = 0
        while (1 << k) < per:
            d = 1 << k
            sh = jnp.where(row >= d, pltpu.roll(x, d, 0), 0.0)
            x = x + sh * lr[k:k + 1, :] + pltpu.roll(sh, SSM_STATE, 1) * li[k:k + 1, :]
            k += 1
        prev.append(jnp.where(row >= 1, pltpu.roll(x, 1, 0), 0.0))
    xp = jnp.concatenate(prev, axis=0).astype(BF16)
    y = y + lax.dot_general(xp, m_ref[...], _NT, preferred_element_type=F32) + d2_ref[...] * u.astype(F32)
    y_ref[...] = _gelu(y).astype(BF16)


def _ssm(u2, lam, bc, d2, n_seq):
    g, rows, w = u2.shape
    assert rows // n_seq <= 1 << 8, "lam_bar^(16*2^k) is prepared for 8 scan steps"
    gs = SSM_GROUPS_PER_STEP
    blk = lambda a: pl.BlockSpec((gs,) + a.shape[1:], lambda i: (i,) + (0,) * (a.ndim - 1))
    return pl.pallas_call(
        functools.partial(_ssm_body, n_seq=n_seq),
        grid=(g // gs,),
        in_specs=[blk(u2), blk(lam), blk(bc), blk(d2)],
        out_specs=blk(u2),
        out_shape=jax.ShapeDtypeStruct(u2.shape, BF16),
        scratch_shapes=[pltpu.VMEM((gs, w, w), BF16),
                        pltpu.VMEM((gs, w, 2 * SSM_STATE), F32),
                        pltpu.VMEM((gs, w, 2 * SSM_STATE), BF16),
                        pltpu.VMEM((gs, w, 2 * SSM_STATE), BF16)],
        compiler_params=pltpu.CompilerParams(dimension_semantics=("arbitrary",),
                                             vmem_limit_bytes=VMEM_LIMIT),
        name="ssm",
    )(u2, lam, bc, d2)


def _mix_tile(x_ref, y2_ref, yb_ref, yc_ref, gmix_ref, wgate_ref, bgate_ref, wglu_ref, bglu_ref,
              wbr_ref, wout_ref, tok_ref, h_ref, hkeep_ref):
    x = x_ref[...]
    n = _rms(x, gmix_ref[...]).astype(BF16)

    def gated(b, c, y):
        cols = pl.ds(b * D_MODEL + c * MERGE_COLS, MERGE_COLS)
        gate = _sigmoid(_dot(n, wgate_ref[:, cols].astype(BF16)) + bgate_ref[:, cols])
        return gate * _dot(y, wbr_ref[b, :, pl.ds(c * MERGE_COLS, MERGE_COLS)].astype(BF16))

    n_blocks = D_MODEL // MERGE_COLS
    yb = yb_ref[...]
    yc = yc_ref[...]
    head = gated(1, 0, yb) + gated(2, 0, yc)
    yield
    _chunks_to_tokens(y2_ref, tok_ref)
    ys = jnp.concatenate([tok_ref[k] for k in range(SSM_WIDTH // LANES)], axis=1).astype(BF16)
    glu = _dot(ys, wglu_ref[...].astype(BF16)) + bglu_ref[...]
    ya = (glu[:, :SSM_WIDTH] * _sigmoid(glu[:, SSM_WIDTH:])).astype(BF16)
    merged = [(head + gated(0, 0, ya)).astype(BF16)]
    for c in range(1, n_blocks):
        yield
        acc = gated(0, c, ya)
        yield
        acc = acc + gated(1, c, yb)
        yield
        merged.append((acc + gated(2, c, yc)).astype(BF16))
    yield
    h = x + _dot(jnp.concatenate(merged, axis=1), wout_ref[...].astype(BF16))
    h_ref[...] = h
    hkeep_ref[...] = h


def _route_tile(hkeep_ref, gffn_ref, wrt_ref, brt_ref, xnp_ref, rt_ref, rtt_ref, cnt_ref, carry_ref):
    h = hkeep_ref[...]
    xn = _rms(h, gffn_ref[...])
    yield
    packed = _pack_bf16_pair(xn)
    for j in range(SC_SPLIT):
        xnp_ref[j] = packed[:, j * SC_ROW:(j + 1) * SC_ROW]
    yield

    x_hi = xn.astype(BF16)
    x_lo = (xn - x_hi.astype(F32)).astype(BF16)
    head = _dot(x_hi, wrt_ref[...])
    logits = (head[:, :LANES] + head[:, LANES:] + _dot(x_lo, wrt_ref[:, :LANES])) + brt_ref[...]
    yield
    tm = logits.shape[0]
    lane_i = lax.broadcasted_iota(jnp.int32, (tm, LANES), 1)
    lane = lane_i.astype(F32)
    neg = jnp.float32(-3.0e38)
    big = jnp.float32(LANES)
    gmask = lane_i < N_GROUPS
    gl = jnp.where(gmask, logits, neg)
    gmax = jnp.max(gl, axis=-1, keepdims=True)
    gidx = jnp.min(jnp.where(gl == gmax, lane, big), axis=-1, keepdims=True)
    gsum = jnp.sum(jnp.where(gmask, jnp.exp(gl - gmax), 0.0), axis=-1, keepdims=True)
    g_w = 1.0 / gsum
    yield
    e_lane = lane_i - ROUTE_LANE0
    lane_group = (e_lane // EXPERTS_PER_GROUP).astype(F32)
    emask = (e_lane >= 0) & (e_lane < N_EXPERTS) & (lane_group == gidx)
    el = jnp.where(emask, logits, neg)
    m1 = jnp.max(el, axis=-1, keepdims=True)
    i1 = jnp.min(jnp.where(el == m1, lane, big), axis=-1, keepdims=True)
    yield
    el2 = jnp.where(lane == i1, neg, el)
    m2 = jnp.max(el2, axis=-1, keepdims=True)
    i2 = jnp.min(jnp.where(el2 == m2, lane, big), axis=-1, keepdims=True)
    t = jnp.exp(m2 - m1)
    w1 = g_w / (1.0 + t)
    w2 = g_w * t / (1.0 + t)
    yield

    sel1 = lane == i1
    sel2 = lane == i2
    onehot = jnp.where(sel1 | sel2, 1.0, 0.0)
    r_i = lax.broadcasted_iota(jnp.int32, (tm, tm), 0)
    c_i = lax.broadcasted_iota(jnp.int32, (tm, tm), 1)
    stril = jnp.where(c_i < r_i, 1.0, 0.0).astype(BF16)
    cum = _dot(stril, onehot.astype(BF16)) + carry_ref[0:1, :]
    rank1 = jnp.sum(jnp.where(sel1, cum, 0.0), axis=-1, keepdims=True)
    rank2 = jnp.sum(jnp.where(sel2, cum, 0.0), axis=-1, keepdims=True)
    carry_ref[...] = carry_ref[...] + jnp.sum(onehot, axis=0, keepdims=True)
    cnt_ref[...] = carry_ref[...]
    yield

    cols = (i1 - ROUTE_LANE0, i2 - ROUTE_LANE0, rank1, rank2, w1, w2)
    rt = jnp.zeros((tm, LANES), F32)
    for c, val in enumerate(cols):
        rt = jnp.where(lane_i == c, val, rt)
    rt_ref[...] = rt
    rtt_ref[...] = rt.T[:8]


def _merge_body(x_ref, y2_ref, yb_ref, yc_ref, gmix_ref, wgate_ref, bgate_ref, wglu_ref, bglu_ref,
                wbr_ref, wout_ref, gffn_ref, wrt_ref, brt_ref,
                h_ref, xnp_ref, rt_ref, rtt_ref, cnt_ref, carry_ref, tok_ref, hkeep_ref):
    i = pl.program_id(0)
    last = pl.num_programs(0) - 1
    cur = hkeep_ref.at[i % 2]
    prev = hkeep_ref.at[(i + 1) % 2]

    def mix():
        return _mix_tile(x_ref, y2_ref, yb_ref, yc_ref, gmix_ref, wgate_ref, bgate_ref, wglu_ref,
                         bglu_ref, wbr_ref, wout_ref, tok_ref, h_ref, cur)

    def route():
        return _route_tile(prev, gffn_ref, wrt_ref, brt_ref, xnp_ref, rt_ref, rtt_ref, cnt_ref, carry_ref)

    @pl.when(i == 0)
    def _():
        carry_ref[...] = jnp.zeros_like(carry_ref)
        _alternate(mix())

    @pl.when((i > 0) & (i < last))
    def _():
        _alternate(route(), mix())

    @pl.when(i == last)
    def _():
        _alternate(route())


def _merge_route(x, y2, yb, yc, g_mix, w_gate, b_gate, w_glu, b_glu, w_br, w_out, g_ffn, w_rt, b_rt):
    t, d = x.shape
    tm = TM_MERGE
    tiles = t // tm
    mixed = lambda i: jnp.minimum(i, tiles - 1)
    routed = lambda i: jnp.maximum(i - 1, 0)
    c2 = lambda i: (0, 0)
    c3 = lambda i: (0, 0, 0)
    full = lambda a: pl.BlockSpec(a.shape, c2 if a.ndim == 2 else c3, pipeline_mode=pl.Buffered(1))
    return pl.pallas_call(
        _merge_body,
        grid=(tiles + 1,),
        in_specs=[pl.BlockSpec((tm, d), lambda i: (mixed(i), 0)),
                  pl.BlockSpec((SSM_GROUPS, tm // SSM_CHUNK, SSM_CHUNK * SSM_GROUP),
                               lambda i: (0, mixed(i), 0)),
                  pl.BlockSpec((tm, SSM_WIDTH), lambda i: (mixed(i), 0)),
                  pl.BlockSpec((tm, SSM_WIDTH), lambda i: (mixed(i), 0)),
                  full(g_mix), full(w_gate), full(b_gate), full(w_glu), full(b_glu),
                  full(w_br), full(w_out), full(g_ffn), full(w_rt), full(b_rt)],
        out_specs=[pl.BlockSpec((tm, d), lambda i: (mixed(i), 0)),
                   pl.BlockSpec((SC_SPLIT, tm, SC_ROW), lambda i: (0, routed(i), 0)),
                   pl.BlockSpec((tm, LANES), lambda i: (routed(i), 0)),
                   pl.BlockSpec((8, tm), lambda i: (0, routed(i))),
                   pl.BlockSpec((8, LANES), c2)],
        out_shape=[jax.ShapeDtypeStruct((t, d), F32),
                   jax.ShapeDtypeStruct((SC_SPLIT, t, SC_ROW), jnp.uint32),
                   jax.ShapeDtypeStruct((t, LANES), F32),
                   jax.ShapeDtypeStruct((8, t), F32),
                   jax.ShapeDtypeStruct((8, LANES), F32)],
        scratch_shapes=[pltpu.VMEM((8, LANES), F32),
                        pltpu.VMEM((SSM_WIDTH // LANES, tm, LANES), F32),
                        pltpu.VMEM((2, tm, d), F32)],
        compiler_params=pltpu.CompilerParams(dimension_semantics=("arbitrary",),
                                             vmem_limit_bytes=VMEM_LIMIT),
        name="merge_route",
    )(x, y2, yb, yc, g_mix, w_gate, b_gate, w_glu, b_glu, w_br, w_out, g_ffn, w_rt, b_rt)


def _slot_body(rtt_ref, cnt_ref, out_ref, *, nslots):
    tl = rtt_ref.shape[1]
    blocks = jnp.ceil(cnt_ref[...] * (1.0 / BM))
    k_i = lax.broadcasted_iota(jnp.int32, (LANES, LANES), 0)
    l_i = lax.broadcasted_iota(jnp.int32, (LANES, LANES), 1)
    before = jnp.where(k_i < l_i, 1.0, 0.0).astype(BF16)
    first_blk = _dot(blocks.astype(BF16), before).astype(BF16)
    lane_of = lax.broadcasted_iota(jnp.int32, (LANES, tl), 0).astype(F32) - ROUTE_LANE0
    for k in range(TOP_K):
        onehot = jnp.where(rtt_ref[k:k + 1, :] == lane_of, 1.0, 0.0).astype(BF16)
        start = _dot(first_blk, onehot) * BM
        slot = (start[0:1, :] + rtt_ref[TOP_K + k:TOP_K + k + 1, :]).astype(jnp.int32)
        for j in range(SC_SPLIT):
            out_ref[k * SC_SPLIT + j:k * SC_SPLIT + j + 1, :] = slot + j * nslots


def _slot_rows(rtt, cnt, nslots):
    t = rtt.shape[1]
    tl = SLOT_LANES
    return pl.pallas_call(
        functools.partial(_slot_body, nslots=nslots),
        grid=(t // tl,),
        in_specs=[pl.BlockSpec((8, tl), lambda i: (0, i)),
                  pl.BlockSpec((8, LANES), lambda i: (0, 0))],
        out_specs=pl.BlockSpec((TOP_K * SC_SPLIT, tl), lambda i: (0, i)),
        out_shape=jax.ShapeDtypeStruct((TOP_K * SC_SPLIT, t), jnp.int32),
        compiler_params=pltpu.CompilerParams(dimension_semantics=("arbitrary",)),
        name="slot_rows",
    )(rtt, cnt)


def _plan_blocks(counts_ref, blk_expert, blk_valid, blk_first, blk_run, run_expert):
    nb = blk_expert.shape[0]

    def per_expert(e, carry):
        cursor, run = carry
        count = counts_ref[e]
        n_blk = (count + (BM - 1)) // BM

        def per_block(b, cur):
            blk_expert[cur] = e
            blk_valid[cur] = jnp.minimum(count - b * BM, BM)
            blk_first[cur] = (b == 0).astype(jnp.int32)
            blk_run[cur] = run
            return cur + 1

        run_expert[run] = e
        return lax.fori_loop(0, n_blk, per_block, cursor), run + (n_blk > 0).astype(jnp.int32)

    cursor, runs = lax.fori_loop(0, counts_ref.shape[0], per_expert, (jnp.int32(0), jnp.int32(0)))
    run_expert[runs] = -1
    run_expert[runs + 1] = -1

    def empty(j, carry):
        blk_expert[j] = 0
        blk_valid[j] = 0
        blk_first[j] = 0
        blk_run[j] = runs - 1
        return carry

    lax.fori_loop(cursor, nb, empty, 0)


def _expert_body(counts_ref, buf_ref, w1_hbm, w3_hbm, w2_hbm, out_ref, w1_buf, w3_buf, w2_buf, sem,
                 blk_expert, blk_valid, blk_first, blk_run, run_expert):
    step = pl.program_id(0)

    @pl.when(step == 0)
    def _():
        _plan_blocks(counts_ref, blk_expert, blk_valid, blk_first, blk_run, run_expert)

    def weight_copies(e, s):
        return (pltpu.make_async_copy(w1_hbm.at[e], w1_buf.at[s], sem.at[s, 0]),
                pltpu.make_async_copy(w3_hbm.at[e], w3_buf.at[s], sem.at[s, 1]),
                pltpu.make_async_copy(w2_hbm.at[e], w2_buf.at[s], sem.at[s, 2]))

    for sub in range(BLOCKS_PER_STEP):
        i = step * BLOCKS_PER_STEP + sub
        block = pl.ds(sub * BM, BM)
        _expert_block(i, buf_ref.at[:, block, :], out_ref.at[:, block, :], weight_copies,
                      (w1_buf, w3_buf, w2_buf), blk_expert, blk_valid, blk_first, blk_run, run_expert)


def _expert_block(i, buf_ref, out_ref, weight_copies, weights, blk_expert, blk_valid, blk_first, blk_run,
                  run_expert):
    w1_buf, w3_buf, w2_buf = weights
    expert = blk_expert[i]
    valid = blk_valid[i]
    run = blk_run[i]
    slot = run % WEIGHT_SLOTS
    ahead1 = run_expert[run + 1]
    ahead2 = run_expert[run + 2]

    @pl.when(i == 0)
    def _():
        for c in weight_copies(expert, slot):
            c.start()

        @pl.when(ahead1 >= 0)
        def _():
            for c in weight_copies(ahead1, (slot + 1) % WEIGHT_SLOTS):
                c.start()

    @pl.when(blk_first[i] == 1)
    def _():
        for c in weight_copies(expert, slot):
            c.wait()

        @pl.when(ahead2 >= 0)
        def _():
            for c in weight_copies(ahead2, (slot + 2) % WEIGHT_SLOTS):
                c.start()

    def mlp(rows):
        x = _unpack_bf16_pair(jnp.concatenate([buf_ref[j, :rows, :] for j in range(SC_SPLIT)], axis=1))
        row = lax.broadcasted_iota(jnp.int32, x.shape, 0)
        x = jnp.where(row < valid, x, 0.0).astype(BF16)
        h1 = _dot(x, w1_buf[slot].astype(BF16))
        h3 = _dot(x, w3_buf[slot].astype(BF16))
        a = (h1 * _sigmoid(h1) * h3).astype(BF16)
        packed = _pack_bf16_pair(_dot(a, w2_buf[slot].astype(BF16)))
        for j in range(SC_SPLIT):
            out_ref[j, :rows, :] = packed[:, j * SC_ROW:(j + 1) * SC_ROW]
            if rows < BM:
                out_ref[j, rows:, :] = jnp.zeros((BM - rows, SC_ROW), out_ref.dtype)

    for rows in range(BM_STEP, BM + 1, BM_STEP):
        @pl.when((valid > rows - BM_STEP) & (valid <= rows))
        def _():
            mlp(rows)

    @pl.when(valid <= 0)
    def _():
        out_ref[...] = jnp.zeros_like(out_ref)


def _experts(counts, buf, w1, w3, w2):
    _, nslots, _ = buf.shape
    nb = nslots // BM
    rows = pl.BlockSpec((SC_SPLIT, BLOCKS_PER_STEP * BM, SC_ROW), lambda i, counts: (0, i, 0))
    hbm = pl.BlockSpec(memory_space=pl.ANY)
    table = pltpu.SMEM((nb,), jnp.int32)
    grid_spec = pltpu.PrefetchScalarGridSpec(
        num_scalar_prefetch=1,
        grid=(nb // BLOCKS_PER_STEP,),
        in_specs=[rows, hbm, hbm, hbm],
        out_specs=rows,
        scratch_shapes=[pltpu.VMEM((WEIGHT_SLOTS,) + w1.shape[1:], w1.dtype),
                        pltpu.VMEM((WEIGHT_SLOTS,) + w3.shape[1:], w3.dtype),
                        pltpu.VMEM((WEIGHT_SLOTS,) + w2.shape[1:], w2.dtype),
                        pltpu.SemaphoreType.DMA((WEIGHT_SLOTS, 3)),
                        table, table, table, table,
                        pltpu.SMEM((counts.shape[0] + 2,), jnp.int32)],
    )
    return pl.pallas_call(
        _expert_body,
        grid_spec=grid_spec,
        out_shape=jax.ShapeDtypeStruct(buf.shape, jnp.uint32),
        compiler_params=pltpu.CompilerParams(dimension_semantics=("arbitrary",),
                                             vmem_limit_bytes=VMEM_LIMIT),
        name="experts",
    )(counts, buf, w1, w3, w2)


def _sc_mesh():
    return plsc.VectorSubcoreMesh(core_axis_name="core", subcore_axis_name="subcore")


def _dispatch_rows(rows, dest0, dest1, nslots):
    t, w = rows.shape
    win = SC_WINDOW
    idx_spec = pl.BlockSpec((1, win), lambda i: (0, i))

    @functools.partial(pl.kernel, mesh=_sc_mesh(), scratch_types=[],
                       out_type=jax.ShapeDtypeStruct((nslots, w), rows.dtype), name="dispatch_rows")
    def run(rows_hbm, i0_hbm, i1_hbm, out_hbm):
        def body(rows_vmem, i0_vmem, i1_vmem):
            pltpu.sync_copy(rows_vmem, out_hbm.at[i0_vmem.at[0]])
            pltpu.sync_copy(rows_vmem, out_hbm.at[i1_vmem.at[0]])

        pltpu.emit_pipeline(
            body, grid=(t // win,),
            in_specs=[pl.BlockSpec((win, w), lambda i: (i, 0)), idx_spec, idx_spec],
            out_specs=[],
            core_axis_name=("core", "subcore"),
            dimension_semantics=(pltpu.PARALLEL,),
        )(rows_hbm, i0_hbm, i1_hbm)

    return run(rows, dest0.reshape(1, t), dest1.reshape(1, t))


def _gather_rows(table, idx):
    n = idx.shape[0]
    w = table.shape[1]
    win = SC_WINDOW

    @functools.partial(pl.kernel, mesh=_sc_mesh(), scratch_types=[],
                       out_type=jax.ShapeDtypeStruct((n, w), table.dtype), name="gather_rows")
    def run(table_hbm, i_hbm, out_hbm):
        def body(i_vmem, out_vmem):
            pltpu.sync_copy(table_hbm.at[i_vmem.at[0]], out_vmem)

        pltpu.emit_pipeline(
            body, grid=(n // win,),
            in_specs=[pl.BlockSpec((1, win), lambda i: (0, i))],
            out_specs=[pl.BlockSpec((win, w), lambda i: (i, 0))],
            core_axis_name=("core", "subcore"),
            dimension_semantics=(pltpu.PARALLEL,),
        )(i_hbm, out_hbm)

    return run(table, idx.reshape(1, n))


def _combine_body(h_ref, g_ref, rt_ref, gfin_ref, out_ref):
    rt = rt_ref[...]
    y = h_ref[...]
    for k in range(TOP_K):
        rows = jnp.concatenate([g_ref[k * SC_SPLIT + j] for j in range(SC_SPLIT)], axis=1)
        y = y + rt[:, 4 + k:5 + k] * _unpack_bf16_pair(rows)
    out_ref[...] = _rms(y, gfin_ref[...])


def _combine(h, g, rt, g_final):
    t, d = h.shape
    tm = TM_OUT
    tok = lambda i: (i, 0)
    return pl.pallas_call(
        _combine_body,
        grid=(t // tm,),
        in_specs=[pl.BlockSpec((tm, d), tok),
                  pl.BlockSpec((TOP_K * SC_SPLIT, tm, SC_ROW), lambda i: (0, i, 0)),
                  pl.BlockSpec((tm, LANES), tok),
                  pl.BlockSpec((1, d), lambda i: (0, 0))],
        out_specs=pl.BlockSpec((tm, d), tok),
        out_shape=jax.ShapeDtypeStruct((t, d), F32),
        compiler_params=pltpu.CompilerParams(dimension_semantics=("arbitrary",),
                                             vmem_limit_bytes=VMEM_LIMIT),
        name="combine",
    )(h, g, rt, g_final)


def _layer(h, mem, g_mix, g_mem, w_in, w_gate, b_gate, lam_re, lam_im, log_dt, b_re, b_im,
           c_re, c_im, d_skip, w_glu, b_glu, g_sgu, w_spatial, b_spatial, w_kv, w_branch,
           w_out, g_ffn, w_group, b_group, w_router, b_router, w1, w3, w2, g_out):
    bsz, s, d = h.shape
    t = bsz * s
    row = lambda a: a.reshape(1, -1)

    tril = jnp.tril(jnp.ones((CHUNK, CHUNK), dtype=bool))
    w_sp = jnp.where(tril, w_spatial, 0.0).astype(BF16)
    b_sp = jnp.broadcast_to(b_spatial[:, :, None], (SGU_HEADS, CHUNK, SGU_HEAD_DIM))
    u2, y_b, y_c = _in_proj(h, row(g_mix), w_in.astype(BF16), row(g_sgu), w_sp, b_sp,
                            mem, row(g_mem), w_kv)
    y2 = _ssm(u2, *_ssm_params(lam_re, lam_im, log_dt, b_re, b_im, c_re, c_im, d_skip), n_seq=bsz)

    pad = LANES - N_GROUPS - N_EXPERTS
    w_rt = jnp.concatenate([w_group, w_router, jnp.zeros((d, pad), F32)], axis=1)
    w_rt_hi = w_rt.astype(BF16)
    w_rt = jnp.concatenate([w_rt_hi, (w_rt - w_rt_hi.astype(F32)).astype(BF16)], axis=1)
    b_rt =jnp.concatenate([b_group, b_router, jnp.zeros((pad,), F32)]).reshape(1, LANES)
    h2, xnp, rt, rtt, cnt = _merge_route(
        h.reshape(t, d), y2, y_b.reshape(t, -1), y_c.reshape(t, -1), row(g_mix),
        w_gate, row(b_gate), w_glu, row(b_glu), w_branch, w_out, row(g_ffn), w_rt, b_rt)

    assert BM & (BM - 1) == 0, "block padding arithmetic assumes a power-of-two block"
    counts = cnt[0, ROUTE_LANE0:ROUTE_LANE0 + N_EXPERTS].astype(jnp.int32)
    nb = (t * TOP_K) // BM + N_EXPERTS
    assert nb < 256, "slot_rows keeps block counts in bf16 matmul operands"
    nslots = nb * BM
    dest_p = _slot_rows(rtt, cnt, nslots).reshape(TOP_K, SC_SPLIT * t)
    buf = _dispatch_rows(xnp.reshape(SC_SPLIT * t, SC_ROW), dest_p[0], dest_p[1], SC_SPLIT * nslots)
    yb = _experts(counts, buf.reshape(SC_SPLIT, nslots, SC_ROW), w1, w3, w2)
    g = _gather_rows(yb.reshape(SC_SPLIT * nslots, SC_ROW), dest_p.reshape(-1))
    out = _combine(h2, g.reshape(TOP_K * SC_SPLIT, t, SC_ROW), rt, row(g_out))
    return out.reshape(bsz, s, d)


def kernel(x, mem, g_mix, g_mem, w_in, w_gate, b_gate, lam_re, lam_im, log_dt, b_re, b_im, c_re,
           c_im, d_skip, w_glu, b_glu, g_sgu, w_spatial, b_spatial, w_kv, w_branch, w_out, g_ffn,
           w_group, b_group, w_router, b_router, w1, w3, w2, g_final):
    assert g_mix.shape[0] == 1, "single-layer stack"
    return _layer(x, mem, g_mix[0], g_mem[0], w_in[0], w_gate[0], b_gate[0], lam_re[0], lam_im[0],
                  log_dt[0], b_re[0], b_im[0], c_re[0], c_im[0], d_skip[0], w_glu[0], b_glu[0],
                  g_sgu[0], w_spatial[0], b_spatial[0], w_kv[0], w_branch[0], w_out[0], g_ffn[0],
                  w_group[0], b_group[0], w_router[0], b_router[0], w1[0], w3[0], w2[0], g_final)
```

```python
import functools
import math

import jax
import jax.numpy as jnp
from jax import lax
from jax.experimental import pallas as pl
from jax.experimental.pallas import tpu as pltpu
from jax.experimental.pallas import tpu_sc as plsc

F32 = jnp.float32
BF16 = jnp.bfloat16

EPS = 1e-6
D_MODEL = 1024
SSM_WIDTH = 512
SSM_GROUP = 16
SSM_GROUPS = 32
SSM_STATE = 64
SSM_CHUNK = 16
SSM_GROUPS_PER_STEP = 1
SGU_WIDTH = 512
SGU_HEADS = 4
SGU_HEAD_DIM = 128
CHUNK = 128
XA_HEADS = 4
XA_HEAD_DIM = 128
N_GROUPS = 8
EXPERTS_PER_GROUP = 8
N_EXPERTS = 64
TOP_K = 2
LANES = 128
ROUTE_LANE0 = N_GROUPS

TM_IN = 1024
TM_MERGE = 512
MERGE_COLS = 256
TM_OUT = 1024
BM = 512
BM_STEP = 128
BLOCKS_PER_STEP = 2
WEIGHT_SLOTS = 3
SC_WINDOW = 128
SC_ROW = 512
SC_SPLIT = (D_MODEL // 2) // SC_ROW
SLOT_LANES = 2048
VMEM_LIMIT = 56 * 1024 * 1024


def _rms(x, g):
    return x * lax.rsqrt(jnp.mean(x * x, axis=-1, keepdims=True) + EPS) * g


def _sigmoid(x):
    return 0.5 * (1.0 + jnp.tanh(0.5 * x))


def _gelu(x):
    c = math.sqrt(2.0 / math.pi)
    return 0.5 * x * (1.0 + jnp.tanh(c * (x + 0.044715 * (x * x * x))))


def _dot(a, b):
    return jnp.dot(a, b, preferred_element_type=F32)


_NT = (((1,), (1,)), ((), ()))


def _pack_bf16_pair(x):
    n = x.shape[1] // 2
    lo = lax.bitcast_convert_type(x[:, :n].astype(BF16).astype(F32), jnp.uint32)
    hi = lax.bitcast_convert_type(x[:, n:].astype(BF16).astype(F32), jnp.uint32)
    return hi | (lo >> 16)


def _unpack_bf16_pair(p):
    lo = lax.bitcast_convert_type(p << 16, F32)
    hi = lax.bitcast_convert_type(p & jnp.uint32(0xFFFF0000), F32)
    return jnp.concatenate([lo, hi], axis=1)


GROUPS_PER_TILE = LANES // SSM_GROUP
POS_PER_TILE = LANES // SSM_GROUP


def _slot_masks(rows):
    lane = lax.broadcasted_iota(jnp.int32, (rows, LANES), 1)
    return [(lane >= i * SSM_GROUP) & (lane < (i + 1) * SSM_GROUP) for i in range(LANES // SSM_GROUP)]


def _tokens_to_chunks(tok_ref, out_ref):
    tm = tok_ref.shape[1]
    nc = tm // SSM_CHUNK
    masks = _slot_masks(nc)
    for k in range(SSM_WIDTH // LANES):
        for j in range(SSM_CHUNK // POS_PER_TILE):
            src = [tok_ref[k, pl.ds(j * POS_PER_TILE + p, nc, stride=SSM_CHUNK), :]
                   for p in range(POS_PER_TILE)]
            for gi in range(GROUPS_PER_TILE):
                acc = None
                for p in range(POS_PER_TILE):
                    shift = ((p - gi) * SSM_GROUP) % LANES
                    r = pltpu.roll(src[p], shift, 1) if shift else src[p]
                    acc = r if acc is None else jnp.where(masks[p], r, acc)
                out_ref[k * GROUPS_PER_TILE + gi, :, pl.ds(j * LANES, LANES)] = acc.astype(out_ref.dtype)


def _chunks_to_tokens(chunk_ref, tok_ref):
    tm = tok_ref.shape[1]
    nc = tm // SSM_CHUNK
    masks = _slot_masks(nc)
    for k in range(SSM_WIDTH // LANES):
        for j in range(SSM_CHUNK // POS_PER_TILE):
            src = [chunk_ref[k * GROUPS_PER_TILE + gi, :, pl.ds(j * LANES, LANES)].astype(F32)
                   for gi in range(GROUPS_PER_TILE)]
            for p in range(POS_PER_TILE):
                acc = None
                for gi in range(GROUPS_PER_TILE):
                    shift = ((gi - p) * SSM_GROUP) % LANES
                    r = pltpu.roll(src[gi], shift, 1) if shift else src[gi]
                    acc = r if acc is None else jnp.where(masks[gi], r, acc)
                tok_ref[k, pl.ds(j * POS_PER_TILE + p, nc, stride=SSM_CHUNK), :] = acc


def _in_body(x_ref, gmix_ref, win_ref, gsgu_ref, wsp_ref, bsp_ref, mem_ref, gmem_ref, wkv_ref,
             u2_ref, yb_ref, yc_ref, tok_ref, k_ref, v_ref):
    @pl.when(pl.program_id(1) == 0)
    def _():
        kv = _dot(_rms(mem_ref[0], gmem_ref[...]).astype(BF16), wkv_ref[...].astype(BF16))
        k_ref[...] = kv[:, :XA_HEADS * XA_HEAD_DIM].astype(BF16)
        v_ref[...] = kv[:, XA_HEADS * XA_HEAD_DIM:].astype(BF16)

    n = _rms(x_ref[0], gmix_ref[...]).astype(BF16)
    proj = _dot(n, win_ref[...])
    for k in range(SSM_WIDTH // LANES):
        tok_ref[k] = proj[:, k * LANES:(k + 1) * LANES]
    _tokens_to_chunks(tok_ref, u2_ref)

    u = _gelu(proj[:, SSM_WIDTH:SSM_WIDTH + SGU_WIDTH])
    v = _gelu(proj[:, SSM_WIDTH + SGU_WIDTH:SSM_WIDTH + 2 * SGU_WIDTH])
    v = _rms(v, gsgu_ref[...]).astype(BF16)
    tm = u.shape[0]
    rows = []
    for c in range(tm // CHUNK):
        vc = v[c * CHUNK:(c + 1) * CHUNK]
        heads = []
        for h in range(SGU_HEADS):
            sl = slice(h * SGU_HEAD_DIM, (h + 1) * SGU_HEAD_DIM)
            heads.append(_dot(wsp_ref[h], vc[:, sl]) + bsp_ref[h])
        rows.append(jnp.concatenate(heads, axis=1))
    sv = jnp.concatenate(rows, axis=0)
    yb_ref[0] = (u * sv).astype(BF16)

    q = proj[:, SSM_WIDTH + 2 * SGU_WIDTH:].astype(BF16)
    kk = k_ref[...]
    vv = v_ref[...]
    outs = []
    for h in range(XA_HEADS):
        sl = slice(h * XA_HEAD_DIM, (h + 1) * XA_HEAD_DIM)
        s = lax.dot_general(q[:, sl], kk[:, sl], (((1,), (1,)), ((), ())),
                            preferred_element_type=F32) * (XA_HEAD_DIM ** -0.5)
        e = jnp.exp(s - jnp.max(s, axis=-1, keepdims=True))
        l = jnp.sum(e, axis=-1, keepdims=True)
        outs.append(_dot(e.astype(BF16), vv[:, sl]) / l)
    yc_ref[0] = jnp.concatenate(outs, axis=1).astype(BF16)


def _in_proj(x, g_mix, w_in, g_sgu, w_sp, b_sp, mem, g_mem, w_kv):
    b, s, d = x.shape
    m = mem.shape[1]
    xa = XA_HEADS * XA_HEAD_DIM
    const2 = lambda i, j: (0, 0)
    const3 = lambda i, j: (0, 0, 0)
    tok = lambda i, j: (i, j, 0)
    per_b = lambda i, j: (i, 0, 0)
    out = jax.ShapeDtypeStruct((b, s, SSM_WIDTH), BF16)
    nc = TM_IN // SSM_CHUNK
    tiles = s // TM_IN
    u2 = jax.ShapeDtypeStruct((SSM_GROUPS, b * s // SSM_CHUNK, SSM_CHUNK * SSM_GROUP), BF16)
    return pl.pallas_call(
        _in_body,
        grid=(b, s // TM_IN),
        in_specs=[pl.BlockSpec((1, TM_IN, d), tok),
                  pl.BlockSpec((1, d), const2),
                  pl.BlockSpec(w_in.shape, const2),
                  pl.BlockSpec((1, SGU_WIDTH), const2),
                  pl.BlockSpec(w_sp.shape, const3),
                  pl.BlockSpec(b_sp.shape, const3),
                  pl.BlockSpec((1, m, d), per_b),
                  pl.BlockSpec((1, d), const2),
                  pl.BlockSpec(w_kv.shape, const2, pipeline_mode=pl.Buffered(1))],
        out_specs=[pl.BlockSpec((SSM_GROUPS, nc, SSM_CHUNK * SSM_GROUP), lambda i, j: (0, i * tiles + j, 0)),
                   pl.BlockSpec((1, TM_IN, SSM_WIDTH), tok),
                   pl.BlockSpec((1, TM_IN, SSM_WIDTH), tok)],
        out_shape=[u2, out, out],
        scratch_shapes=[pltpu.VMEM((SSM_WIDTH // LANES, TM_IN, LANES), F32),
                        pltpu.VMEM((m, xa), BF16),
                        pltpu.VMEM((m, xa), BF16)],
        compiler_params=pltpu.CompilerParams(dimension_semantics=("arbitrary", "arbitrary"),
                                             vmem_limit_bytes=VMEM_LIMIT),
        name="in_proj",
    )(x, g_mix, w_in, g_sgu, w_sp, b_sp, mem, g_mem, w_kv)


def _alternate(*stages):
    live = list(stages)
    while live:
        live = [s for s in live if next(s, True) is None]


def _ssm_params(lam_re, lam_im, log_dt, b_re, b_im, c_re, c_im, d_skip):
    g, p = lam_re.shape
    dup = lambda a: jnp.concatenate([a, a], axis=-1)
    lam = jnp.stack([dup(lam_re), dup(lam_im), jnp.broadcast_to(log_dt[:, None], (g, 2 * p))], axis=1)
    brt = b_re.transpose(0, 2, 1)
    bit = b_im.transpose(0, 2, 1)
    cat = lambda a, b: jnp.concatenate([a, b], axis=-1)
    bc = jnp.stack([cat(brt, bit), cat(bit, brt), cat(c_re, -c_im), cat(-c_im, -c_re)], axis=1)
    d2 = jnp.tile(d_skip.reshape(g, 1, SSM_GROUP), (1, 1, SSM_CHUNK))
    return lam, bc, d2


def _ssm_operators(lam_ref, bc_ref, ccat_ref, n_ref, m_ref):
    lam_re = lam_ref[0:1, :]
    lam_im = lam_ref[1:2, :]
    dt = jnp.exp(lam_ref[2:3, :])
    ar = lam_re * dt
    ai = lam_im * dt
    lane = lax.broadcasted_iota(jnp.int32, (1, LANES), 1)
    sgn = jnp.where(lane >= SSM_STATE, 1.0, -1.0)

    def powers(j):
        mag = jnp.exp(ar * j)
        ph = ai * j
        return mag * jnp.cos(ph), mag * jnp.sin(ph)

    pos = lax.broadcasted_iota(jnp.int32, (SSM_CHUNK, 1), 0).astype(F32)
    p_re, p_im = powers(pos)
    r_re, r_im = powers((SSM_CHUNK - 1) - pos)
    one_re, one_im = powers(jnp.ones((1, 1), F32))
    q_re = p_re * one_re - p_im * one_im
    q_im = p_re * one_im + p_im * one_re
    step = lax.shift_left(jnp.full((8, 1), SSM_CHUNK, jnp.int32),
                          lax.broadcasted_iota(jnp.int32, (8, 1), 0)).astype(F32)
    s_re, s_im = powers(step)

    den = lam_re * lam_re + lam_im * lam_im
    f_re = ((one_re - 1.0) * lam_re + one_im * lam_im) / den
    f_im = (one_im * lam_re - (one_re - 1.0) * lam_im) / den
    b1, b2, ca, cb = bc_ref[0], bc_ref[1], bc_ref[2], bc_ref[3]
    bb1 = f_re * b1 + (sgn * f_im) * b2
    bb2 = f_re * b2 - (sgn * f_im) * b1
    r_ims = sgn * r_im
    for s in range(SSM_CHUNK):
        blk = pl.ds(s * SSM_GROUP, SSM_GROUP)
        ccat_ref[blk, :] = ca * p_re[s:s + 1, :] + cb * p_im[s:s + 1, :]
        m_ref[blk, :] = (ca * q_re[s:s + 1, :] + cb * q_im[s:s + 1, :]).astype(m_ref.dtype)
        n_ref[blk, :] = (bb1 * r_re[s:s + 1, :] + bb2 * r_ims[s:s + 1, :]).astype(n_ref.dtype)
    return bb1, s_re, sgn * s_im


def _ssm_body(u_ref, lam_ref, bc_ref, d2_ref, y_ref, toep_ref, ccat_ref, n_ref, m_ref, *, n_seq):
    for g in range(u_ref.shape[0]):
        _ssm_group(u_ref.at[g], lam_ref.at[g], bc_ref.at[g], d2_ref.at[g], y_ref.at[g],
                   toep_ref.at[g], ccat_ref.at[g], n_ref.at[g], m_ref.at[g], n_seq)


def _ssm_group(u_ref, lam_ref, bc_ref, d2_ref, y_ref, toep_ref, ccat_ref, n_ref, m_ref, n_seq):
    bcat, lr, li = _ssm_operators(lam_ref, bc_ref, ccat_ref, n_ref, m_ref)
    kern = lax.dot_general(bcat, ccat_ref[...], _NT, precision=lax.Precision.HIGHEST,
                           preferred_element_type=F32)
    col = lax.broadcasted_iota(jnp.int32, kern.shape, 1)
    for s in range(SSM_CHUNK):
        shifted = pltpu.roll(kern, s * SSM_GROUP, 1) if s else kern
        toep_ref[s * SSM_GROUP:(s + 1) * SSM_GROUP, :] = jnp.where(
            col >= s * SSM_GROUP, shifted, 0.0).astype(BF16)

    u = u_ref[...]
    rows = u.shape[0]
    per = rows // n_seq
    y = _dot(u, toep_ref[...])
    st = _dot(u, n_ref[...])
    row = lax.broadcasted_iota(jnp.int32, (per, LANES), 0)
    prev = []
    for b in range(n_seq):
        x = st[b * per:(b + 1) * per]
        k = 0
        while (1 << k) < per:
            d = 1 << k
            sh = jnp.where(row >= d, pltpu.roll(x, d, 0), 0.0)
            x = x + sh * lr[k:k + 1, :] + pltpu.roll(sh, SSM_STATE, 1) * li[k:k + 1, :]
            k += 1
        prev.append(jnp.where(row >= 1, pltpu.roll(x, 1, 0), 0.0))
    xp = jnp.concatenate(prev, axis=0).astype(BF16)
    y = y + lax.dot_general(xp, m_ref[...], _NT, preferred_element_type=F32) + d2_ref[...] * u.astype(F32)
    y_ref[...] = _gelu(y).astype(BF16)


def _ssm(u2, lam, bc, d2, n_seq):
    g, rows, w = u2.shape
    assert rows // n_seq <= 1 << 8, "lam_bar^(16*2^k) is prepared for 8 scan steps"
    gs = SSM_GROUPS_PER_STEP
    blk = lambda a: pl.BlockSpec((gs,) + a.shape[1:], lambda i: (i,) + (0,) * (a.ndim - 1))
    return pl.pallas_call(
        functools.partial(_ssm_body, n_seq=n_seq),
        grid=(g // gs,),
        in_specs=[blk(u2), blk(lam), blk(bc), blk(d2)],
        out_specs=blk(u2),
        out_shape=jax.ShapeDtypeStruct(u2.shape, BF16),
        scratch_shapes=[pltpu.VMEM((gs, w, w), BF16),
                        pltpu.VMEM((gs, w, 2 * SSM_STATE), F32),
                        pltpu.VMEM((gs, w, 2 * SSM_STATE), BF16),
                        pltpu.VMEM((gs, w, 2 * SSM_STATE), BF16)],
        compiler_params=pltpu.CompilerParams(dimension_semantics=("arbitrary",),
                                             vmem_limit_bytes=VMEM_LIMIT),
        name="ssm",
    )(u2, lam, bc, d2)


def _mix_tile(x_ref, y2_ref, yb_ref, yc_ref, gmix_ref, wgate_ref, bgate_ref, wglu_ref, bglu_ref,
              wbr_ref, wout_ref, tok_ref, h_ref, hkeep_ref):
    x = x_ref[...]
    n = _rms(x, gmix_ref[...]).astype(BF16)

    def gated(b, c, y):
        cols = pl.ds(b * D_MODEL + c * MERGE_COLS, MERGE_COLS)
        gate = _sigmoid(_dot(n, wgate_ref[:, cols].astype(BF16)) + bgate_ref[:, cols])
        return gate * _dot(y, wbr_ref[b, :, pl.ds(c * MERGE_COLS, MERGE_COLS)].astype(BF16))

    n_blocks = D_MODEL // MERGE_COLS
    yb = yb_ref[...]
    yc = yc_ref[...]
    head = gated(1, 0, yb) + gated(2, 0, yc)
    yield
    _chunks_to_tokens(y2_ref, tok_ref)
    ys = jnp.concatenate([tok_ref[k] for k in range(SSM_WIDTH // LANES)], axis=1).astype(BF16)
    glu = _dot(ys, wglu_ref[...].astype(BF16)) + bglu_ref[...]
    ya = (glu[:, :SSM_WIDTH] * _sigmoid(glu[:, SSM_WIDTH:])).astype(BF16)
    merged = [(head + gated(0, 0, ya)).astype(BF16)]
    for c in range(1, n_blocks):
        yield
        acc = gated(0, c, ya)
        yield
        acc = acc + gated(1, c, yb)
        yield
        merged.append((acc + gated(2, c, yc)).astype(BF16))
    yield
    h = x + _dot(jnp.concatenate(merged, axis=1), wout_ref[...].astype(BF16))
    h_ref[...] = h
    hkeep_ref[...] = h


def _route_tile(hkeep_ref, gffn_ref, wrt_ref, brt_ref, xnp_ref, rt_ref, rtt_ref, cnt_ref, carry_ref):
    h = hkeep_ref[...]
    xn = _rms(h, gffn_ref[...])
    yield
    packed = _pack_bf16_pair(xn)
    for j in range(SC_SPLIT):
        xnp_ref[j] = packed[:, j * SC_ROW:(j + 1) * SC_ROW]
    yield

    x_hi = xn.astype(BF16)
    x_lo = (xn - x_hi.astype(F32)).astype(BF16)
    head = _dot(x_hi, wrt_ref[...])
    logits = (head[:, :LANES] + head[:, LANES:] + _dot(x_lo, wrt_ref[:, :LANES])) + brt_ref[...]
    yield
    tm = logits.shape[0]
    lane_i = lax.broadcasted_iota(jnp.int32, (tm, LANES), 1)
    lane = lane_i.astype(F32)
    neg = jnp.float32(-3.0e38)
    big = jnp.float32(LANES)
    gmask = lane_i < N_GROUPS
    gl = jnp.where(gmask, logits, neg)
    gmax = jnp.max(gl, axis=-1, keepdims=True)
    gidx = jnp.min(jnp.where(gl == gmax, lane, big), axis=-1, keepdims=True)
    gsum = jnp.sum(jnp.where(gmask, jnp.exp(gl - gmax), 0.0), axis=-1, keepdims=True)
    g_w = 1.0 / gsum
    yield
    e_lane = lane_i - ROUTE_LANE0
    lane_group = (e_lane // EXPERTS_PER_GROUP).astype(F32)
    emask = (e_lane >= 0) & (e_lane < N_EXPERTS) & (lane_group == gidx)
    el = jnp.where(emask, logits, neg)
    m1 = jnp.max(el, axis=-1, keepdims=True)
    i1 = jnp.min(jnp.where(el == m1, lane, big), axis=-1, keepdims=True)
    yield
    el2 = jnp.where(lane == i1, neg, el)
    m2 = jnp.max(el2, axis=-1, keepdims=True)
    i2 = jnp.min(jnp.where(el2 == m2, lane, big), axis=-1, keepdims=True)
    t = jnp.exp(m2 - m1)
    w1 = g_w / (1.0 + t)
    w2 = g_w * t / (1.0 + t)
    yield

    sel1 = lane == i1
    sel2 = lane == i2
    onehot = jnp.where(sel1 | sel2, 1.0, 0.0)
    r_i = lax.broadcasted_iota(jnp.int32, (tm, tm), 0)
    c_i = lax.broadcasted_iota(jnp.int32, (tm, tm), 1)
    stril = jnp.where(c_i < r_i, 1.0, 0.0).astype(BF16)
    cum = _dot(stril, onehot.astype(BF16)) + carry_ref[0:1, :]
    rank1 = jnp.sum(jnp.where(sel1, cum, 0.0), axis=-1, keepdims=True)
    rank2 = jnp.sum(jnp.where(sel2, cum, 0.0), axis=-1, keepdims=True)
    carry_ref[...] = carry_ref[...] + jnp.sum(onehot, axis=0, keepdims=True)
    cnt_ref[...] = carry_ref[...]
    yield

    cols = (i1 - ROUTE_LANE0, i2 - ROUTE_LANE0, rank1, rank2, w1, w2)
    rt = jnp.zeros((tm, LANES), F32)
    for c, val in enumerate(cols):
        rt = jnp.where(lane_i == c, val, rt)
    rt_ref[...] = rt
    rtt_ref[...] = rt.T[:8]


def _merge_body(x_ref, y2_ref, yb_ref, yc_ref, gmix_ref, wgate_ref, bgate_ref, wglu_ref, bglu_ref,
                wbr_ref, wout_ref, gffn_ref, wrt_ref, brt_ref,
                h_ref, xnp_ref, rt_ref, rtt_ref, cnt_ref, carry_ref, tok_ref, hkeep_ref):
    i = pl.program_id(0)
    last = pl.num_programs(0) - 1
    cur = hkeep_ref.at[i % 2]
    prev = hkeep_ref.at[(i + 1) % 2]

    def mix():
        return _mix_tile(x_ref, y2_ref, yb_ref, yc_ref, gmix_ref, wgate_ref, bgate_ref, wglu_ref,
                         bglu_ref, wbr_ref, wout_ref, tok_ref, h_ref, cur)

    def route():
        return _route_tile(prev, gffn_ref, wrt_ref, brt_ref, xnp_ref, rt_ref, rtt_ref, cnt_ref, carry_ref)

    @pl.when(i == 0)
    def _():
        carry_ref[...] = jnp.zeros_like(carry_ref)
        _alternate(mix())

    @pl.when((i > 0) & (i < last))
    def _():
        _alternate(route(), mix())

    @pl.when(i == last)
    def _():
        _alternate(route())


def _merge_route(x, y2, yb, yc, g_mix, w_gate, b_gate, w_glu, b_glu, w_br, w_out, g_ffn, w_rt, b_rt):
    t, d = x.shape
    tm = TM_MERGE
    tiles = t // tm
    mixed = lambda i: jnp.minimum(i, tiles - 1)
    routed = lambda i: jnp.maximum(i - 1, 0)
    c2 = lambda i: (0, 0)
    c3 = lambda i: (0, 0, 0)
    full = lambda a: pl.BlockSpec(a.shape, c2 if a.ndim == 2 else c3, pipeline_mode=pl.Buffered(1))
    return pl.pallas_call(
        _merge_body,
        grid=(tiles + 1,),
        in_specs=[pl.BlockSpec((tm, d), lambda i: (mixed(i), 0)),
                  pl.BlockSpec((SSM_GROUPS, tm // SSM_CHUNK, SSM_CHUNK * SSM_GROUP),
                               lambda i: (0, mixed(i), 0)),
                  pl.BlockSpec((tm, SSM_WIDTH), lambda i: (mixed(i), 0)),
                  pl.BlockSpec((tm, SSM_WIDTH), lambda i: (mixed(i), 0)),
                  full(g_mix), full(w_gate), full(b_gate), full(w_glu), full(b_glu),
                  full(w_br), full(w_out), full(g_ffn), full(w_rt), full(b_rt)],
        out_specs=[pl.BlockSpec((tm, d), lambda i: (mixed(i), 0)),
                   pl.BlockSpec((SC_SPLIT, tm, SC_ROW), lambda i: (0, routed(i), 0)),
                   pl.BlockSpec((tm, LANES), lambda i: (routed(i), 0)),
                   pl.BlockSpec((8, tm), lambda i: (0, routed(i))),
                   pl.BlockSpec((8, LANES), c2)],
        out_shape=[jax.ShapeDtypeStruct((t, d), F32),
                   jax.ShapeDtypeStruct((SC_SPLIT, t, SC_ROW), jnp.uint32),
                   jax.ShapeDtypeStruct((t, LANES), F32),
                   jax.ShapeDtypeStruct((8, t), F32),
                   jax.ShapeDtypeStruct((8, LANES), F32)],
        scratch_shapes=[pltpu.VMEM((8, LANES), F32),
                        pltpu.VMEM((SSM_WIDTH // LANES, tm, LANES), F32),
                        pltpu.VMEM((2, tm, d), F32)],
        compiler_params=pltpu.CompilerParams(dimension_semantics=("arbitrary",),
                                             vmem_limit_bytes=VMEM_LIMIT),
        name="merge_route",
    )(x, y2, yb, yc, g_mix, w_gate, b_gate, w_glu, b_glu, w_br, w_out, g_ffn, w_rt, b_rt)


def _slot_body(rtt_ref, cnt_ref, out_ref, *, nslots):
    tl = rtt_ref.shape[1]
    blocks = jnp.ceil(cnt_ref[...] * (1.0 / BM))
    k_i = lax.broadcasted_iota(jnp.int32, (LANES, LANES), 0)
    l_i = lax.broadcasted_iota(jnp.int32, (LANES, LANES), 1)
    before = jnp.where(k_i < l_i, 1.0, 0.0).astype(BF16)
    first_blk = _dot(blocks.astype(BF16), before).astype(BF16)
    lane_of = lax.broadcasted_iota(jnp.int32, (LANES, tl), 0).astype(F32) - ROUTE_LANE0
    for k in range(TOP_K):
        onehot = jnp.where(rtt_ref[k:k + 1, :] == lane_of, 1.0, 0.0).astype(BF16)
        start = _dot(first_blk, onehot) * BM
        slot = (start[0:1, :] + rtt_ref[TOP_K + k:TOP_K + k + 1, :]).astype(jnp.int32)
        for j in range(SC_SPLIT):
            out_ref[k * SC_SPLIT + j:k * SC_SPLIT + j + 1, :] = slot + j * nslots


def _slot_rows(rtt, cnt, nslots):
    t = rtt.shape[1]
    tl = SLOT_LANES
    return pl.pallas_call(
        functools.partial(_slot_body, nslots=nslots),
        grid=(t // tl,),
        in_specs=[pl.BlockSpec((8, tl), lambda i: (0, i)),
                  pl.BlockSpec((8, LANES), lambda i: (0, 0))],
        out_specs=pl.BlockSpec((TOP_K * SC_SPLIT, tl), lambda i: (0, i)),
        out_shape=jax.ShapeDtypeStruct((TOP_K * SC_SPLIT, t), jnp.int32),
        compiler_params=pltpu.CompilerParams(dimension_semantics=("arbitrary",)),
        name="slot_rows",
    )(rtt, cnt)


def _plan_blocks(counts_ref, blk_expert, blk_valid, blk_first, blk_run, run_expert):
    nb = blk_expert.shape[0]

    def per_expert(e, carry):
        cursor, run = carry
        count = counts_ref[e]
        n_blk = (count + (BM - 1)) // BM

        def per_block(b, cur):
            blk_expert[cur] = e
            blk_valid[cur] = jnp.minimum(count - b * BM, BM)
            blk_first[cur] = (b == 0).astype(jnp.int32)
            blk_run[cur] = run
            return cur + 1

        run_expert[run] = e
        return lax.fori_loop(0, n_blk, per_block, cursor), run + (n_blk > 0).astype(jnp.int32)

    cursor, runs = lax.fori_loop(0, counts_ref.shape[0], per_expert, (jnp.int32(0), jnp.int32(0)))
    run_expert[runs] = -1
    run_expert[runs + 1] = -1

    def empty(j, carry):
        blk_expert[j] = 0
        blk_valid[j] = 0
        blk_first[j] = 0
        blk_run[j] = runs - 1
        return carry

    lax.fori_loop(cursor, nb, empty, 0)


def _expert_body(counts_ref, buf_ref, w1_hbm, w3_hbm, w2_hbm, out_ref, w1_buf, w3_buf, w2_buf, sem,
                 blk_expert, blk_valid, blk_first, blk_run, run_expert):
    step = pl.program_id(0)

    @pl.when(step == 0)
    def _():
        _plan_blocks(counts_ref, blk_expert, blk_valid, blk_first, blk_run, run_expert)

    def weight_copies(e, s):
        return (pltpu.make_async_copy(w1_hbm.at[e], w1_buf.at[s], sem.at[s, 0]),
                pltpu.make_async_copy(w3_hbm.at[e], w3_buf.at[s], sem.at[s, 1]),
                pltpu.make_async_copy(w2_hbm.at[e], w2_buf.at[s], sem.at[s, 2]))

    for sub in range(BLOCKS_PER_STEP):
        i = step * BLOCKS_PER_STEP + sub
        block = pl.ds(sub * BM, BM)
        _expert_block(i, buf_ref.at[:, block, :], out_ref.at[:, block, :], weight_copies,
                      (w1_buf, w3_buf, w2_buf), blk_expert, blk_valid, blk_first, blk_run, run_expert)


def _expert_block(i, buf_ref, out_ref, weight_copies, weights, blk_expert, blk_valid, blk_first, blk_run,
                  run_expert):
    w1_buf, w3_buf, w2_buf = weights
    expert = blk_expert[i]
    valid = blk_valid[i]
    run = blk_run[i]
    slot = run % WEIGHT_SLOTS
    ahead1 = run_expert[run + 1]
    ahead2 = run_expert[run + 2]

    @pl.when(i == 0)
    def _():
        for c in weight_copies(expert, slot):
            c.start()

        @pl.when(ahead1 >= 0)
        def _():
            for c in weight_copies(ahead1, (slot + 1) % WEIGHT_SLOTS):
                c.start()

    @pl.when(blk_first[i] == 1)
    def _():
        for c in weight_copies(expert, slot):
            c.wait()

        @pl.when(ahead2 >= 0)
        def _():
            for c in weight_copies(ahead2, (slot + 2) % WEIGHT_SLOTS):
                c.start()

    def mlp(rows):
        x = _unpack_bf16_pair(jnp.concatenate([buf_ref[j, :rows, :] for j in range(SC_SPLIT)], axis=1))
        row = lax.broadcasted_iota(jnp.int32, x.shape, 0)
        x = jnp.where(row < valid, x, 0.0).astype(BF16)
        h1 = _dot(x, w1_buf[slot].astype(BF16))
        h3 = _dot(x, w3_buf[slot].astype(BF16))
        a = (h1 * _sigmoid(h1) * h3).astype(BF16)
        packed = _pack_bf16_pair(_dot(a, w2_buf[slot].astype(BF16)))
        for j in range(SC_SPLIT):
            out_ref[j, :rows, :] = packed[:, j * SC_ROW:(j + 1) * SC_ROW]
            if rows < BM:
                out_ref[j, rows:, :] = jnp.zeros((BM - rows, SC_ROW), out_ref.dtype)

    for rows in range(BM_STEP, BM + 1, BM_STEP):
        @pl.when((valid > rows - BM_STEP) & (valid <= rows))
        def _():
            mlp(rows)

    @pl.when(valid <= 0)
    def _():
        out_ref[...] = jnp.zeros_like(out_ref)


def _experts(counts, buf, w1, w3, w2):
    _, nslots, _ = buf.shape
    nb = nslots // BM
    rows = pl.BlockSpec((SC_SPLIT, BLOCKS_PER_STEP * BM, SC_ROW), lambda i, counts: (0, i, 0))
    hbm = pl.BlockSpec(memory_space=pl.ANY)
    table = pltpu.SMEM((nb,), jnp.int32)
    grid_spec = pltpu.PrefetchScalarGridSpec(
        num_scalar_prefetch=1,
        grid=(nb // BLOCKS_PER_STEP,),
        in_specs=[rows, hbm, hbm, hbm],
        out_specs=rows,
        scratch_shapes=[pltpu.VMEM((WEIGHT_SLOTS,) + w1.shape[1:], w1.dtype),
                        pltpu.VMEM((WEIGHT_SLOTS,) + w3.shape[1:], w3.dtype),
                        pltpu.VMEM((WEIGHT_SLOTS,) + w2.shape[1:], w2.dtype),
                        pltpu.SemaphoreType.DMA((WEIGHT_SLOTS, 3)),
                        table, table, table, table,
                        pltpu.SMEM((counts.shape[0] + 2,), jnp.int32)],
    )
    return pl.pallas_call(
        _expert_body,
        grid_spec=grid_spec,
        out_shape=jax.ShapeDtypeStruct(buf.shape, jnp.uint32),
        compiler_params=pltpu.CompilerParams(dimension_semantics=("arbitrary",),
                                             vmem_limit_bytes=VMEM_LIMIT),
        name="experts",
    )(counts, buf, w1, w3, w2)


def _sc_mesh():
    return plsc.VectorSubcoreMesh(core_axis_name="core", subcore_axis_name="subcore")


def _dispatch_rows(rows, dest0, dest1, nslots):
    t, w = rows.shape
    win = SC_WINDOW
    idx_spec = pl.BlockSpec((1, win), lambda i: (0, i))

    @functools.partial(pl.kernel, mesh=_sc_mesh(), scratch_types=[],
                       out_type=jax.ShapeDtypeStruct((nslots, w), rows.dtype), name="dispatch_rows")
    def run(rows_hbm, i0_hbm, i1_hbm, out_hbm):
        def body(rows_vmem, i0_vmem, i1_vmem):
            pltpu.sync_copy(rows_vmem, out_hbm.at[i0_vmem.at[0]])
            pltpu.sync_copy(rows_vmem, out_hbm.at[i1_vmem.at[0]])

        pltpu.emit_pipeline(
            body, grid=(t // win,),
            in_specs=[pl.BlockSpec((win, w), lambda i: (i, 0), pipeline_mode=pl.Buffered(1)),
                      idx_spec, idx_spec],
            out_specs=[],
            core_axis_name=("core", "subcore"),
            dimension_semantics=(pltpu.PARALLEL,),
        )(rows_hbm, i0_hbm, i1_hbm)

    return run(rows, dest0.reshape(1, t), dest1.reshape(1, t))


def _gather_rows(table, idx):
    n = idx.shape[0]
    w = table.shape[1]
    win = SC_WINDOW

    @functools.partial(pl.kernel, mesh=_sc_mesh(), scratch_types=[],
                       out_type=jax.ShapeDtypeStruct((n, w), table.dtype), name="gather_rows")
    def run(table_hbm, i_hbm, out_hbm):
        def body(i_vmem, out_vmem):
            pltpu.sync_copy(table_hbm.at[i_vmem.at[0]], out_vmem)

        pltpu.emit_pipeline(
            body, grid=(n // win,),
            in_specs=[pl.BlockSpec((1, win), lambda i: (0, i))],
            out_specs=[pl.BlockSpec((win, w), lambda i: (i, 0), pipeline_mode=pl.Buffered(1))],
            core_axis_name=("core", "subcore"),
            dimension_semantics=(pltpu.PARALLEL,),
        )(i_hbm, out_hbm)

    return run(table, idx.reshape(1, n))


def _combine_body(h_ref, g_ref, rt_ref, gfin_ref, out_ref):
    rt = rt_ref[...]
    y = h_ref[...]
    for k in range(TOP_K):
        rows = jnp.concatenate([g_ref[k * SC_SPLIT + j] for j in range(SC_SPLIT)], axis=1)
        y = y + rt[:, 4 + k:5 + k] * _unpack_bf16_pair(rows)
    out_ref[...] = _rms(y, gfin_ref[...])


def _combine(h, g, rt, g_final):
    t, d = h.shape
    tm = TM_OUT
    tok = lambda i: (i, 0)
    return pl.pallas_call(
        _combine_body,
        grid=(t // tm,),
        in_specs=[pl.BlockSpec((tm, d), tok),
                  pl.BlockSpec((TOP_K * SC_SPLIT, tm, SC_ROW), lambda i: (0, i, 0)),
                  pl.BlockSpec((tm, LANES), tok),
                  pl.BlockSpec((1, d), lambda i: (0, 0))],
        out_specs=pl.BlockSpec((tm, d), tok),
        out_shape=jax.ShapeDtypeStruct((t, d), F32),
        compiler_params=pltpu.CompilerParams(dimension_semantics=("arbitrary",),
                                             vmem_limit_bytes=VMEM_LIMIT),
        name="combine",
    )(h, g, rt, g_final)


def _layer(h, mem, g_mix, g_mem, w_in, w_gate, b_gate, lam_re, lam_im, log_dt, b_re, b_im,
           c_re, c_im, d_skip, w_glu, b_glu, g_sgu, w_spatial, b_spatial, w_kv, w_branch,
           w_out, g_ffn, w_group, b_group, w_router, b_router, w1, w3, w2, g_out):
    bsz, s, d = h.shape
    t = bsz * s
    row = lambda a: a.reshape(1, -1)

    tril = jnp.tril(jnp.ones((CHUNK, CHUNK), dtype=bool))
    w_sp = jnp.where(tril, w_spatial, 0.0).astype(BF16)
    b_sp = jnp.broadcast_to(b_spatial[:, :, None], (SGU_HEADS, CHUNK, SGU_HEAD_DIM))
    u2, y_b, y_c = _in_proj(h, row(g_mix), w_in.astype(BF16), row(g_sgu), w_sp, b_sp,
                            mem, row(g_mem), w_kv)
    y2 = _ssm(u2, *_ssm_params(lam_re, lam_im, log_dt, b_re, b_im, c_re, c_im, d_skip), n_seq=bsz)

    pad = LANES - N_GROUPS - N_EXPERTS
    w_rt = jnp.concatenate([w_group, w_router, jnp.zeros((d, pad), F32)], axis=1)
    w_rt_hi = w_rt.astype(BF16)
    w_rt = jnp.concatenate([w_rt_hi, (w_rt - w_rt_hi.astype(F32)).astype(BF16)], axis=1)
    b_rt =jnp.concatenate([b_group, b_router, jnp.zeros((pad,), F32)]).reshape(1, LANES)
    h2, xnp, rt, rtt, cnt = _merge_route(
        h.reshape(t, d), y2, y_b.reshape(t, -1), y_c.reshape(t, -1), row(g_mix),
        w_gate, row(b_gate), w_glu, row(b_glu), w_branch, w_out, row(g_ffn), w_rt, b_rt)

    assert BM & (BM - 1) == 0, "block padding arithmetic assumes a power-of-two block"
    counts = cnt[0, ROUTE_LANE0:ROUTE_LANE0 + N_EXPERTS].astype(jnp.int32)
    nb = (t * TOP_K) // BM + N_EXPERTS
    assert nb < 256, "slot_rows keeps block counts in bf16 matmul operands"
    nslots = nb * BM
    dest_p = _slot_rows(rtt, cnt, nslots).reshape(TOP_K, SC_SPLIT * t)
    buf = _dispatch_rows(xnp.reshape(SC_SPLIT * t, SC_ROW), dest_p[0], dest_p[1], SC_SPLIT * nslots)
    yb = _experts(counts, buf.reshape(SC_SPLIT, nslots, SC_ROW), w1, w3, w2)
    g = _gather_rows(yb.reshape(SC_SPLIT * nslots, SC_ROW), dest_p.reshape(-1))
    out = _combine(h2, g.reshape(TOP_K * SC_SPLIT, t, SC_ROW), rt, row(g_out))
    return out.reshape(bsz, s, d)


def kernel(x, mem, g_mix, g_mem, w_in, w_gate, b_gate, lam_re, lam_im, log_dt, b_re, b_im, c_re,
           c_im, d_skip, w_glu, b_glu, g_sgu, w_spatial, b_spatial, w_kv, w_branch, w_out, g_ffn,
           w_group, b_group, w_router, b_router, w1, w3, w2, g_final):
    assert g_mix.shape[0] == 1, "single-layer stack"
    return _layer(x, mem, g_mix[0], g_mem[0], w_in[0], w_gate[0], b_gate[0], lam_re[0], lam_im[0],
                  log_dt[0], b_re[0], b_im[0], c_re[0], c_im[0], d_skip[0], w_glu[0], b_glu[0],
                  g_sgu[0], w_spatial[0], b_spatial[0], w_kv[0], w_branch[0], w_out[0], g_ffn[0],
                  w_group[0], b_group[0], w_router[0], b_router[0], w1[0], w3[0], w2[0], g_final)
```

```python
import functools
import math

import jax
import jax.numpy as jnp
from jax import lax
from jax.experimental import pallas as pl
from jax.experimental.pallas import tpu as pltpu
from jax.experimental.pallas import tpu_sc as plsc

F32 = jnp.float32
BF16 = jnp.bfloat16

EPS = 1e-6
D_MODEL = 1024
SSM_WIDTH = 512
SSM_GROUP = 16
SSM_GROUPS = 32
SSM_STATE = 64
SSM_CHUNK = 16
SSM_GROUPS_PER_STEP = 1
SGU_WIDTH = 512
SGU_HEADS = 4
SGU_HEAD_DIM = 128
CHUNK = 128
XA_HEADS = 4
XA_HEAD_DIM = 128
N_GROUPS = 8
EXPERTS_PER_GROUP = 8
N_EXPERTS = 64
TOP_K = 2
LANES = 128
ROUTE_LANE0 = N_GROUPS

TM_IN = 1024
TM_MERGE = 512
MERGE_COLS = 256
TM_OUT = 1024
BM = 512
BM_STEP = 128
BLOCKS_PER_STEP = 2
WEIGHT_SLOTS = 3
SC_WINDOW = 128
SC_ROW = 512
SC_SPLIT = (D_MODEL // 2) // SC_ROW
SLOT_LANES = 2048
VMEM_LIMIT = 56 * 1024 * 1024


def _rms(x, g):
    return x * lax.rsqrt(jnp.mean(x * x, axis=-1, keepdims=True) + EPS) * g


def _sigmoid(x):
    return 0.5 * (1.0 + jnp.tanh(0.5 * x))


def _gelu(x):
    c = math.sqrt(2.0 / math.pi)
    return 0.5 * x * (1.0 + jnp.tanh(c * (x + 0.044715 * (x * x * x))))


def _dot(a, b):
    return jnp.dot(a, b, preferred_element_type=F32)


_NT = (((1,), (1,)), ((), ()))


def _pack_bf16_pair(x):
    n = x.shape[1] // 2
    lo = lax.bitcast_convert_type(x[:, :n].astype(BF16).astype(F32), jnp.uint32)
    hi = lax.bitcast_convert_type(x[:, n:].astype(BF16).astype(F32), jnp.uint32)
    return hi | (lo >> 16)


def _unpack_bf16_pair(p):
    lo = lax.bitcast_convert_type(p << 16, F32)
    hi = lax.bitcast_convert_type(p & jnp.uint32(0xFFFF0000), F32)
    return jnp.concatenate([lo, hi], axis=1)


GROUPS_PER_TILE = LANES // SSM_GROUP
POS_PER_TILE = LANES // SSM_GROUP


def _slot_masks(rows):
    lane = lax.broadcasted_iota(jnp.int32, (rows, LANES), 1)
    return [(lane >= i * SSM_GROUP) & (lane < (i + 1) * SSM_GROUP) for i in range(LANES // SSM_GROUP)]


def _tokens_to_chunks(tok_ref, out_ref):
    tm = tok_ref.shape[1]
    nc = tm // SSM_CHUNK
    masks = _slot_masks(nc)
    for k in range(SSM_WIDTH // LANES):
        for j in range(SSM_CHUNK // POS_PER_TILE):
            src = [tok_ref[k, pl.ds(j * POS_PER_TILE + p, nc, stride=SSM_CHUNK), :]
                   for p in range(POS_PER_TILE)]
            for gi in range(GROUPS_PER_TILE):
                acc = None
                for p in range(POS_PER_TILE):
                    shift = ((p - gi) * SSM_GROUP) % LANES
                    r = pltpu.roll(src[p], shift, 1) if shift else src[p]
                    acc = r if acc is None else jnp.where(masks[p], r, acc)
                out_ref[k * GROUPS_PER_TILE + gi, :, pl.ds(j * LANES, LANES)] = acc.astype(out_ref.dtype)


def _chunks_to_tokens(chunk_ref, tok_ref):
    tm = tok_ref.shape[1]
    nc = tm // SSM_CHUNK
    masks = _slot_masks(nc)
    for k in range(SSM_WIDTH // LANES):
        for j in range(SSM_CHUNK // POS_PER_TILE):
            src = [chunk_ref[k * GROUPS_PER_TILE + gi, :, pl.ds(j * LANES, LANES)].astype(F32)
                   for gi in range(GROUPS_PER_TILE)]
            for p in range(POS_PER_TILE):
                acc = None
                for gi in range(GROUPS_PER_TILE):
                    shift = ((gi - p) * SSM_GROUP) % LANES
                    r = pltpu.roll(src[gi], shift, 1) if shift else src[gi]
                    acc = r if acc is None else jnp.where(masks[gi], r, acc)
                tok_ref[k, pl.ds(j * POS_PER_TILE + p, nc, stride=SSM_CHUNK), :] = acc


def _in_body(x_ref, gmix_ref, win_ref, gsgu_ref, wsp_ref, bsp_ref, mem_ref, gmem_ref, wkv_ref,
             u2_ref, yb_ref, yc_ref, tok_ref, k_ref, v_ref):
    @pl.when(pl.program_id(1) == 0)
    def _():
        kv = _dot(_rms(mem_ref[0], gmem_ref[...]).astype(BF16), wkv_ref[...].astype(BF16))
        k_ref[...] = kv[:, :XA_HEADS * XA_HEAD_DIM].astype(BF16)
        v_ref[...] = kv[:, XA_HEADS * XA_HEAD_DIM:].astype(BF16)

    n = _rms(x_ref[0], gmix_ref[...]).astype(BF16)
    proj = _dot(n, win_ref[...])
    for k in range(SSM_WIDTH // LANES):
        tok_ref[k] = proj[:, k * LANES:(k + 1) * LANES]
    _tokens_to_chunks(tok_ref, u2_ref)

    u = _gelu(proj[:, SSM_WIDTH:SSM_WIDTH + SGU_WIDTH])
    v = _gelu(proj[:, SSM_WIDTH + SGU_WIDTH:SSM_WIDTH + 2 * SGU_WIDTH])
    v = _rms(v, gsgu_ref[...]).astype(BF16)
    tm = u.shape[0]
    rows = []
    for c in range(tm // CHUNK):
        vc = v[c * CHUNK:(c + 1) * CHUNK]
        heads = []
        for h in range(SGU_HEADS):
            sl = slice(h * SGU_HEAD_DIM, (h + 1) * SGU_HEAD_DIM)
            heads.append(_dot(wsp_ref[h], vc[:, sl]) + bsp_ref[h])
        rows.append(jnp.concatenate(heads, axis=1))
    sv = jnp.concatenate(rows, axis=0)
    yb_ref[0] = (u * sv).astype(BF16)

    q = proj[:, SSM_WIDTH + 2 * SGU_WIDTH:].astype(BF16)
    kk = k_ref[...]
    vv = v_ref[...]
    outs = []
    for h in range(XA_HEADS):
        sl = slice(h * XA_HEAD_DIM, (h + 1) * XA_HEAD_DIM)
        s = lax.dot_general(q[:, sl], kk[:, sl], (((1,), (1,)), ((), ())),
                            preferred_element_type=F32) * (XA_HEAD_DIM ** -0.5)
        e = jnp.exp(s - jnp.max(s, axis=-1, keepdims=True))
        l = jnp.sum(e, axis=-1, keepdims=True)
        outs.append(_dot(e.astype(BF16), vv[:, sl]) / l)
    yc_ref[0] = jnp.concatenate(outs, axis=1).astype(BF16)


def _in_proj(x, g_mix, w_in, g_sgu, w_sp, b_sp, mem, g_mem, w_kv):
    b, s, d = x.shape
    m = mem.shape[1]
    xa = XA_HEADS * XA_HEAD_DIM
    const2 = lambda i, j: (0, 0)
    const3 = lambda i, j: (0, 0, 0)
    tok = lambda i, j: (i, j, 0)
    per_b = lambda i, j: (i, 0, 0)
    out = jax.ShapeDtypeStruct((b, s, SSM_WIDTH), BF16)
    nc = TM_IN // SSM_CHUNK
    tiles = s // TM_IN
    u2 = jax.ShapeDtypeStruct((SSM_GROUPS, b * s // SSM_CHUNK, SSM_CHUNK * SSM_GROUP), BF16)
    return pl.pallas_call(
        _in_body,
        grid=(b, s // TM_IN),
        in_specs=[pl.BlockSpec((1, TM_IN, d), tok),
                  pl.BlockSpec((1, d), const2),
                  pl.BlockSpec(w_in.shape, const2),
                  pl.BlockSpec((1, SGU_WIDTH), const2),
                  pl.BlockSpec(w_sp.shape, const3),
                  pl.BlockSpec(b_sp.shape, const3),
                  pl.BlockSpec((1, m, d), per_b),
                  pl.BlockSpec((1, d), const2),
                  pl.BlockSpec(w_kv.shape, const2, pipeline_mode=pl.Buffered(1))],
        out_specs=[pl.BlockSpec((SSM_GROUPS, nc, SSM_CHUNK * SSM_GROUP), lambda i, j: (0, i * tiles + j, 0)),
                   pl.BlockSpec((1, TM_IN, SSM_WIDTH), tok),
                   pl.BlockSpec((1, TM_IN, SSM_WIDTH), tok)],
        out_shape=[u2, out, out],
        scratch_shapes=[pltpu.VMEM((SSM_WIDTH // LANES, TM_IN, LANES), F32),
                        pltpu.VMEM((m, xa), BF16),
                        pltpu.VMEM((m, xa), BF16)],
        compiler_params=pltpu.CompilerParams(dimension_semantics=("arbitrary", "arbitrary"),
                                             vmem_limit_bytes=VMEM_LIMIT),
        name="in_proj",
    )(x, g_mix, w_in, g_sgu, w_sp, b_sp, mem, g_mem, w_kv)


def _alternate(*stages):
    live = list(stages)
    while live:
        live = [s for s in live if next(s, True) is None]


def _ssm_params(lam_re, lam_im, log_dt, b_re, b_im, c_re, c_im, d_skip):
    g, p = lam_re.shape
    dup = lambda a: jnp.concatenate([a, a], axis=-1)
    lam = jnp.stack([dup(lam_re), dup(lam_im), jnp.broadcast_to(log_dt[:, None], (g, 2 * p))], axis=1)
    brt = b_re.transpose(0, 2, 1)
    bit = b_im.transpose(0, 2, 1)
    cat = lambda a, b: jnp.concatenate([a, b], axis=-1)
    bc = jnp.stack([cat(brt, bit), cat(bit, brt), cat(c_re, -c_im), cat(-c_im, -c_re)], axis=1)
    d2 = jnp.tile(d_skip.reshape(g, 1, SSM_GROUP), (1, 1, SSM_CHUNK))
    return lam, bc, d2


def _ssm_operators(lam_ref, bc_ref, ccat_ref, n_ref, m_ref):
    lam_re = lam_ref[0:1, :]
    lam_im = lam_ref[1:2, :]
    dt = jnp.exp(lam_ref[2:3, :])
    ar = lam_re * dt
    ai = lam_im * dt
    lane = lax.broadcasted_iota(jnp.int32, (1, LANES), 1)
    sgn = jnp.where(lane >= SSM_STATE, 1.0, -1.0)

    def powers(j):
        mag = jnp.exp(ar * j)
        ph = ai * j
        return mag * jnp.cos(ph), mag * jnp.sin(ph)

    pos = lax.broadcasted_iota(jnp.int32, (SSM_CHUNK, 1), 0).astype(F32)
    p_re, p_im = powers(pos)
    r_re, r_im = powers((SSM_CHUNK - 1) - pos)
    one_re, one_im = powers(jnp.ones((1, 1), F32))
    q_re = p_re * one_re - p_im * one_im
    q_im = p_re * one_im + p_im * one_re
    step = lax.shift_left(jnp.full((8, 1), SSM_CHUNK, jnp.int32),
                          lax.broadcasted_iota(jnp.int32, (8, 1), 0)).astype(F32)
    s_re, s_im = powers(step)

    den = lam_re * lam_re + lam_im * lam_im
    f_re = ((one_re - 1.0) * lam_re + one_im * lam_im) / den
    f_im = (one_im * lam_re - (one_re - 1.0) * lam_im) / den
    b1, b2, ca, cb = bc_ref[0], bc_ref[1], bc_ref[2], bc_ref[3]
    bb1 = f_re * b1 + (sgn * f_im) * b2
    bb2 = f_re * b2 - (sgn * f_im) * b1
    r_ims = sgn * r_im
    for s in range(SSM_CHUNK):
        blk = pl.ds(s * SSM_GROUP, SSM_GROUP)
        ccat_ref[blk, :] = ca * p_re[s:s + 1, :] + cb * p_im[s:s + 1, :]
        m_ref[blk, :] = (ca * q_re[s:s + 1, :] + cb * q_im[s:s + 1, :]).astype(m_ref.dtype)
        n_ref[blk, :] = (bb1 * r_re[s:s + 1, :] + bb2 * r_ims[s:s + 1, :]).astype(n_ref.dtype)
    return bb1, s_re, sgn * s_im


def _ssm_body(u_ref, lam_ref, bc_ref, d2_ref, y_ref, toep_ref, ccat_ref, n_ref, m_ref, *, n_seq):
    for g in range(u_ref.shape[0]):
        _ssm_group(u_ref.at[g], lam_ref.at[g], bc_ref.at[g], d2_ref.at[g], y_ref.at[g],
                   toep_ref.at[g], ccat_ref.at[g], n_ref.at[g], m_ref.at[g], n_seq)


def _ssm_group(u_ref, lam_ref, bc_ref, d2_ref, y_ref, toep_ref, ccat_ref, n_ref, m_ref, n_seq):
    bcat, lr, li = _ssm_operators(lam_ref, bc_ref, ccat_ref, n_ref, m_ref)
    kern = lax.dot_general(bcat, ccat_ref[...], _NT, precision=lax.Precision.HIGHEST,
                           preferred_element_type=F32)
    col = lax.broadcasted_iota(jnp.int32, kern.shape, 1)
    for s in range(SSM_CHUNK):
        shifted = pltpu.roll(kern, s * SSM_GROUP, 1) if s else kern
        toep_ref[s * SSM_GROUP:(s + 1) * SSM_GROUP, :] = jnp.where(
            col >= s * SSM_GROUP, shifted, 0.0).astype(BF16)

    u = u_ref[...]
    rows = u.shape[0]
    per = rows // n_seq
    y = _dot(u, toep_ref[...])
    st = _dot(u, n_ref[...])
    row = lax.broadcasted_iota(jnp.int32, (per, LANES), 0)
    prev = []
    for b in range(n_seq):
        x = st[b * per:(b + 1) * per]
        k = 0
        while (1 << k) < per:
            d = 1 << k
            sh = jnp.where(row >= d, pltpu.roll(x, d, 0), 0.0)
            x = x + sh * lr[k:k + 1, :] + pltpu.roll(sh, SSM_STATE, 1) * li[k:k + 1, :]
            k += 1
        prev.append(jnp.where(row >= 1, pltpu.roll(x, 1, 0), 0.0))
    xp = jnp.concatenate(prev, axis=0).astype(BF16)
    y = y + lax.dot_general(xp, m_ref[...], _NT, preferred_element_type=F32) + d2_ref[...] * u.astype(F32)
    y_ref[...] = _gelu(y).astype(BF16)


def _ssm(u2, lam, bc, d2, n_seq):
    g, rows, w = u2.shape
    assert rows // n_seq <= 1 << 8, "lam_bar^(16*2^k) is prepared for 8 scan steps"
    gs = SSM_GROUPS_PER_STEP
    blk = lambda a: pl.BlockSpec((gs,) + a.shape[1:], lambda i: (i,) + (0,) * (a.ndim - 1))
    return pl.pallas_call(
        functools.partial(_ssm_body, n_seq=n_seq),
        grid=(g // gs,),
        in_specs=[blk(u2), blk(lam), blk(bc), blk(d2)],
        out_specs=blk(u2),
        out_shape=jax.ShapeDtypeStruct(u2.shape, BF16),
        scratch_shapes=[pltpu.VMEM((gs, w, w), BF16),
                        pltpu.VMEM((gs, w, 2 * SSM_STATE), F32),
                        pltpu.VMEM((gs, w, 2 * SSM_STATE), BF16),
                        pltpu.VMEM((gs, w, 2 * SSM_STATE), BF16)],
        compiler_params=pltpu.CompilerParams(dimension_semantics=("arbitrary",),
                                             vmem_limit_bytes=VMEM_LIMIT),
        name="ssm",
    )(u2, lam, bc, d2)


def _mix_tile(x_ref, y2_ref, yb_ref, yc_ref, gmix_ref, wgate_ref, bgate_ref, wglu_ref, bglu_ref,
              wbr_ref, wout_ref, tok_ref, h_ref, hkeep_ref):
    x = x_ref[...]
    n = _rms(x, gmix_ref[...]).astype(BF16)

    def gated(b, c, y):
        cols = pl.ds(b * D_MODEL + c * MERGE_COLS, MERGE_COLS)
        gate = _sigmoid(_dot(n, wgate_ref[:, cols].astype(BF16)) + bgate_ref[:, cols])
        return gate * _dot(y, wbr_ref[b, :, pl.ds(c * MERGE_COLS, MERGE_COLS)].astype(BF16))

    n_blocks = D_MODEL // MERGE_COLS
    yb = yb_ref[...]
    yc = yc_ref[...]
    head = gated(1, 0, yb) + gated(2, 0, yc)
    yield
    _chunks_to_tokens(y2_ref, tok_ref)
    ys = jnp.concatenate([tok_ref[k] for k in range(SSM_WIDTH // LANES)], axis=1).astype(BF16)
    glu = _dot(ys, wglu_ref[...].astype(BF16)) + bglu_ref[...]
    ya = (glu[:, :SSM_WIDTH] * _sigmoid(glu[:, SSM_WIDTH:])).astype(BF16)
    merged = [(head + gated(0, 0, ya)).astype(BF16)]
    for c in range(1, n_blocks):
        yield
        acc = gated(0, c, ya)
        yield
        acc = acc + gated(1, c, yb)
        yield
        merged.append((acc + gated(2, c, yc)).astype(BF16))
    yield
    h = x + _dot(jnp.concatenate(merged, axis=1), wout_ref[...].astype(BF16))
    h_ref[...] = h
    hkeep_ref[...] = h


def _route_tile(hkeep_ref, gffn_ref, wrt_ref, brt_ref, xnp_ref, rt_ref, rtt_ref, cnt_ref, carry_ref):
    h = hkeep_ref[...]
    xn = _rms(h, gffn_ref[...])
    yield
    packed = _pack_bf16_pair(xn)
    for j in range(SC_SPLIT):
        xnp_ref[j] = packed[:, j * SC_ROW:(j + 1) * SC_ROW]
    yield

    x_hi = xn.astype(BF16)
    x_lo = (xn - x_hi.astype(F32)).astype(BF16)
    head = _dot(x_hi, wrt_ref[...])
    logits = (head[:, :LANES] + head[:, LANES:] + _dot(x_lo, wrt_ref[:, :LANES])) + brt_ref[...]
    yield
    tm = logits.shape[0]
    lane_i = lax.broadcasted_iota(jnp.int32, (tm, LANES), 1)
    lane = lane_i.astype(F32)
    neg = jnp.float32(-3.0e38)
    big = jnp.float32(LANES)
    gmask = lane_i < N_GROUPS
    gl = jnp.where(gmask, logits, neg)
    gmax = jnp.max(gl, axis=-1, keepdims=True)
    gidx = jnp.min(jnp.where(gl == gmax, lane, big), axis=-1, keepdims=True)
    gsum = jnp.sum(jnp.where(gmask, jnp.exp(gl - gmax), 0.0), axis=-1, keepdims=True)
    g_w = 1.0 / gsum
    yield
    e_lane = lane_i - ROUTE_LANE0
    lane_group = (e_lane // EXPERTS_PER_GROUP).astype(F32)
    emask = (e_lane >= 0) & (e_lane < N_EXPERTS) & (lane_group == gidx)
    el = jnp.where(emask, logits, neg)
    m1 = jnp.max(el, axis=-1, keepdims=True)
    i1 = jnp.min(jnp.where(el == m1, lane, big), axis=-1, keepdims=True)
    yield
    el2 = jnp.where(lane == i1, neg, el)
    m2 = jnp.max(el2, axis=-1, keepdims=True)
    i2 = jnp.min(jnp.where(el2 == m2, lane, big), axis=-1, keepdims=True)
    t = jnp.exp(m2 - m1)
    w1 = g_w / (1.0 + t)
    w2 = g_w * t / (1.0 + t)
    yield

    sel1 = lane == i1
    sel2 = lane == i2
    onehot = jnp.where(sel1 | sel2, 1.0, 0.0)
    r_i = lax.broadcasted_iota(jnp.int32, (tm, tm), 0)
    c_i = lax.broadcasted_iota(jnp.int32, (tm, tm), 1)
    stril = jnp.where(c_i < r_i, 1.0, 0.0).astype(BF16)
    cum = _dot(stril, onehot.astype(BF16)) + carry_ref[0:1, :]
    rank1 = jnp.sum(jnp.where(sel1, cum, 0.0), axis=-1, keepdims=True)
    rank2 = jnp.sum(jnp.where(sel2, cum, 0.0), axis=-1, keepdims=True)
    carry_ref[...] = carry_ref[...] + jnp.sum(onehot, axis=0, keepdims=True)
    cnt_ref[...] = carry_ref[...]
    yield

    cols = (i1 - ROUTE_LANE0, i2 - ROUTE_LANE0, rank1, rank2, w1, w2)
    rt = jnp.zeros((tm, LANES), F32)
    for c, val in enumerate(cols):
        rt = jnp.where(lane_i == c, val, rt)
    rt_ref[...] = rt
    rtt_ref[...] = rt.T[:8]


def _merge_body(x_ref, y2_ref, yb_ref, yc_ref, gmix_ref, wgate_ref, bgate_ref, wglu_ref, bglu_ref,
                wbr_ref, wout_ref, gffn_ref, wrt_ref, brt_ref,
                h_ref, xnp_ref, rt_ref, rtt_ref, cnt_ref, carry_ref, tok_ref, hkeep_ref):
    i = pl.program_id(0)
    last = pl.num_programs(0) - 1
    cur = hkeep_ref.at[i % 2]
    prev = hkeep_ref.at[(i + 1) % 2]

    def mix():
        return _mix_tile(x_ref, y2_ref, yb_ref, yc_ref, gmix_ref, wgate_ref, bgate_ref, wglu_ref,
                         bglu_ref, wbr_ref, wout_ref, tok_ref, h_ref, cur)

    def route():
        return _route_tile(prev, gffn_ref, wrt_ref, brt_ref, xnp_ref, rt_ref, rtt_ref, cnt_ref, carry_ref)

    @pl.when(i == 0)
    def _():
        carry_ref[...] = jnp.zeros_like(carry_ref)
        _alternate(mix())

    @pl.when((i > 0) & (i < last))
    def _():
        _alternate(route(), mix())

    @pl.when(i == last)
    def _():
        _alternate(route())


def _merge_route(x, y2, yb, yc, g_mix, w_gate, b_gate, w_glu, b_glu, w_br, w_out, g_ffn, w_rt, b_rt):
    t, d = x.shape
    tm = TM_MERGE
    tiles = t // tm
    mixed = lambda i: jnp.minimum(i, tiles - 1)
    routed = lambda i: jnp.maximum(i - 1, 0)
    c2 = lambda i: (0, 0)
    c3 = lambda i: (0, 0, 0)
    full = lambda a: pl.BlockSpec(a.shape, c2 if a.ndim == 2 else c3, pipeline_mode=pl.Buffered(1))
    return pl.pallas_call(
        _merge_body,
        grid=(tiles + 1,),
        in_specs=[pl.BlockSpec((tm, d), lambda i: (mixed(i), 0)),
                  pl.BlockSpec((SSM_GROUPS, tm // SSM_CHUNK, SSM_CHUNK * SSM_GROUP),
                               lambda i: (0, mixed(i), 0)),
                  pl.BlockSpec((tm, SSM_WIDTH), lambda i: (mixed(i), 0)),
                  pl.BlockSpec((tm, SSM_WIDTH), lambda i: (mixed(i), 0)),
                  full(g_mix), full(w_gate), full(b_gate), full(w_glu), full(b_glu),
                  full(w_br), full(w_out), full(g_ffn), full(w_rt), full(b_rt)],
        out_specs=[pl.BlockSpec((tm, d), lambda i: (mixed(i), 0)),
                   pl.BlockSpec((SC_SPLIT, tm, SC_ROW), lambda i: (0, routed(i), 0)),
                   pl.BlockSpec((tm, LANES), lambda i: (routed(i), 0)),
                   pl.BlockSpec((8, tm), lambda i: (0, routed(i))),
                   pl.BlockSpec((8, LANES), c2)],
        out_shape=[jax.ShapeDtypeStruct((t, d), F32),
                   jax.ShapeDtypeStruct((SC_SPLIT, t, SC_ROW), jnp.uint32),
                   jax.ShapeDtypeStruct((t, LANES), F32),
                   jax.ShapeDtypeStruct((8, t), F32),
                   jax.ShapeDtypeStruct((8, LANES), F32)],
        scratch_shapes=[pltpu.VMEM((8, LANES), F32),
                        pltpu.VMEM((SSM_WIDTH // LANES, tm, LANES), F32),
                        pltpu.VMEM((2, tm, d), F32)],
        compiler_params=pltpu.CompilerParams(dimension_semantics=("arbitrary",),
                                             vmem_limit_bytes=VMEM_LIMIT),
        name="merge_route",
    )(x, y2, yb, yc, g_mix, w_gate, b_gate, w_glu, b_glu, w_br, w_out, g_ffn, w_rt, b_rt)


def _slot_body(rtt_ref, cnt_ref, out_ref, *, nslots):
    tl = rtt_ref.shape[1]
    blocks = jnp.ceil(cnt_ref[...] * (1.0 / BM))
    k_i = lax.broadcasted_iota(jnp.int32, (LANES, LANES), 0)
    l_i = lax.broadcasted_iota(jnp.int32, (LANES, LANES), 1)
    before = jnp.where(k_i < l_i, 1.0, 0.0).astype(BF16)
    first_blk = _dot(blocks.astype(BF16), before).astype(BF16)
    lane_of = lax.broadcasted_iota(jnp.int32, (LANES, tl), 0).astype(F32) - ROUTE_LANE0
    for k in range(TOP_K):
        onehot = jnp.where(rtt_ref[k:k + 1, :] == lane_of, 1.0, 0.0).astype(BF16)
        start = _dot(first_blk, onehot) * BM
        slot = (start[0:1, :] + rtt_ref[TOP_K + k:TOP_K + k + 1, :]).astype(jnp.int32)
        for j in range(SC_SPLIT):
            out_ref[k * SC_SPLIT + j:k * SC_SPLIT + j + 1, :] = slot + j * nslots


def _slot_rows(rtt, cnt, nslots):
    t = rtt.shape[1]
    tl = SLOT_LANES
    return pl.pallas_call(
        functools.partial(_slot_body, nslots=nslots),
        grid=(t // tl,),
        in_specs=[pl.BlockSpec((8, tl), lambda i: (0, i)),
                  pl.BlockSpec((8, LANES), lambda i: (0, 0))],
        out_specs=pl.BlockSpec((TOP_K * SC_SPLIT, tl), lambda i: (0, i)),
        out_shape=jax.ShapeDtypeStruct((TOP_K * SC_SPLIT, t), jnp.int32),
        compiler_params=pltpu.CompilerParams(dimension_semantics=("arbitrary",)),
        name="slot_rows",
    )(rtt, cnt)


def _plan_blocks(counts_ref, blk_expert, blk_valid, blk_first, blk_run, run_expert):
    nb = blk_expert.shape[0]

    def per_expert(e, carry):
        cursor, run = carry
        count = counts_ref[e]
        n_blk = (count + (BM - 1)) // BM

        def per_block(b, cur):
            blk_expert[cur] = e
            blk_valid[cur] = jnp.minimum(count - b * BM, BM)
            blk_first[cur] = (b == 0).astype(jnp.int32)
            blk_run[cur] = run
            return cur + 1

        run_expert[run] = e
        return lax.fori_loop(0, n_blk, per_block, cursor), run + (n_blk > 0).astype(jnp.int32)

    cursor, runs = lax.fori_loop(0, counts_ref.shape[0], per_expert, (jnp.int32(0), jnp.int32(0)))
    run_expert[runs] = -1
    run_expert[runs + 1] = -1

    def empty(j, carry):
        blk_expert[j] = 0
        blk_valid[j] = 0
        blk_first[j] = 0
        blk_run[j] = runs - 1
        return carry

    lax.fori_loop(cursor, nb, empty, 0)


def _expert_body(counts_ref, buf_ref, w1_hbm, w3_hbm, w2_hbm, out_ref, w1_buf, w3_buf, w2_buf, sem,
                 blk_expert, blk_valid, blk_first, blk_run, run_expert):
    step = pl.program_id(0)

    @pl.when(step == 0)
    def _():
        _plan_blocks(counts_ref, blk_expert, blk_valid, blk_first, blk_run, run_expert)

    def weight_copies(e, s):
        return (pltpu.make_async_copy(w1_hbm.at[e], w1_buf.at[s], sem.at[s, 0]),
                pltpu.make_async_copy(w3_hbm.at[e], w3_buf.at[s], sem.at[s, 1]),
                pltpu.make_async_copy(w2_hbm.at[e], w2_buf.at[s], sem.at[s, 2]))

    for sub in range(BLOCKS_PER_STEP):
        i = step * BLOCKS_PER_STEP + sub
        block = pl.ds(sub * BM, BM)
        _expert_block(i, buf_ref.at[:, block, :], out_ref.at[:, block, :], weight_copies,
                      (w1_buf, w3_buf, w2_buf), blk_expert, blk_valid, blk_first, blk_run, run_expert)


def _expert_block(i, buf_ref, out_ref, weight_copies, weights, blk_expert, blk_valid, blk_first, blk_run,
                  run_expert):
    w1_buf, w3_buf, w2_buf = weights
    expert = blk_expert[i]
    valid = blk_valid[i]
    run = blk_run[i]
    slot = run % WEIGHT_SLOTS
    ahead1 = run_expert[run + 1]
    ahead2 = run_expert[run + 2]

    @pl.when(i == 0)
    def _():
        for c in weight_copies(expert, slot):
            c.start()

        @pl.when(ahead1 >= 0)
        def _():
            for c in weight_copies(ahead1, (slot + 1) % WEIGHT_SLOTS):
                c.start()

    @pl.when(blk_first[i] == 1)
    def _():
        for c in weight_copies(expert, slot):
            c.wait()

        @pl.when(ahead2 >= 0)
        def _():
            for c in weight_copies(ahead2, (slot + 2) % WEIGHT_SLOTS):
                c.start()

    def mlp(rows):
        x = _unpack_bf16_pair(jnp.concatenate([buf_ref[j, :rows, :] for j in range(SC_SPLIT)], axis=1))
        row = lax.broadcasted_iota(jnp.int32, x.shape, 0)
        x = jnp.where(row < valid, x, 0.0).astype(BF16)
        h1 = _dot(x, w1_buf[slot].astype(BF16))
        h3 = _dot(x, w3_buf[slot].astype(BF16))
        a = (h1 * _sigmoid(h1) * h3).astype(BF16)
        packed = _pack_bf16_pair(_dot(a, w2_buf[slot].astype(BF16)))
        for j in range(SC_SPLIT):
            out_ref[j, :rows, :] = packed[:, j * SC_ROW:(j + 1) * SC_ROW]
            if rows < BM:
                out_ref[j, rows:, :] = jnp.zeros((BM - rows, SC_ROW), out_ref.dtype)

    for rows in range(BM_STEP, BM + 1, BM_STEP):
        @pl.when((valid > rows - BM_STEP) & (valid <= rows))
        def _():
            mlp(rows)

    @pl.when(valid <= 0)
    def _():
        out_ref[...] = jnp.zeros_like(out_ref)


def _experts(counts, buf, w1, w3, w2):
    _, nslots, _ = buf.shape
    nb = nslots // BM
    rows = pl.BlockSpec((SC_SPLIT, BLOCKS_PER_STEP * BM, SC_ROW), lambda i, counts: (0, i, 0))
    hbm = pl.BlockSpec(memory_space=pl.ANY)
    table = pltpu.SMEM((nb,), jnp.int32)
    grid_spec = pltpu.PrefetchScalarGridSpec(
        num_scalar_prefetch=1,
        grid=(nb // BLOCKS_PER_STEP,),
        in_specs=[rows, hbm, hbm, hbm],
        out_specs=rows,
        scratch_shapes=[pltpu.VMEM((WEIGHT_SLOTS,) + w1.shape[1:], w1.dtype),
                        pltpu.VMEM((WEIGHT_SLOTS,) + w3.shape[1:], w3.dtype),
                        pltpu.VMEM((WEIGHT_SLOTS,) + w2.shape[1:], w2.dtype),
                        pltpu.SemaphoreType.DMA((WEIGHT_SLOTS, 3)),
                        table, table, table, table,
                        pltpu.SMEM((counts.shape[0] + 2,), jnp.int32)],
    )
    return pl.pallas_call(
        _expert_body,
        grid_spec=grid_spec,
        out_shape=jax.ShapeDtypeStruct(buf.shape, jnp.uint32),
        compiler_params=pltpu.CompilerParams(dimension_semantics=("arbitrary",),
                                             vmem_limit_bytes=VMEM_LIMIT),
        name="experts",
    )(counts, buf, w1, w3, w2)


def _sc_mesh():
    return plsc.VectorSubcoreMesh(core_axis_name="core", subcore_axis_name="subcore")


def _dispatch_rows(rows, dest0, dest1, nslots):
    t, w = rows.shape
    win = SC_WINDOW
    idx_spec = pl.BlockSpec((1, win), lambda i: (0, i))

    @functools.partial(pl.kernel, mesh=_sc_mesh(),
                       scratch_types=[pltpu.SemaphoreType.DMA, pltpu.SemaphoreType.DMA],
                       out_type=jax.ShapeDtypeStruct((nslots, w), rows.dtype), name="dispatch_rows")
    def run(rows_hbm, i0_hbm, i1_hbm, out_hbm, sem0, sem1):
        def body(rows_vmem, i0_vmem, i1_vmem):
            first = pltpu.make_async_copy(rows_vmem, out_hbm.at[i0_vmem.at[0]], sem0)
            second = pltpu.make_async_copy(rows_vmem, out_hbm.at[i1_vmem.at[0]], sem1)
            first.start()
            second.start()
            first.wait()
            second.wait()

        pltpu.emit_pipeline(
            body, grid=(t // win,),
            in_specs=[pl.BlockSpec((win, w), lambda i: (i, 0), pipeline_mode=pl.Buffered(1)),
                      idx_spec, idx_spec],
            out_specs=[],
            core_axis_name=("core", "subcore"),
            dimension_semantics=(pltpu.PARALLEL,),
        )(rows_hbm, i0_hbm, i1_hbm)

    return run(rows, dest0.reshape(1, t), dest1.reshape(1, t))


def _gather_rows(table, idx):
    n = idx.shape[0]
    w = table.shape[1]
    win = SC_WINDOW

    @functools.partial(pl.kernel, mesh=_sc_mesh(), scratch_types=[],
                       out_type=jax.ShapeDtypeStruct((n, w), table.dtype), name="gather_rows")
    def run(table_hbm, i_hbm, out_hbm):
        def body(i_vmem, out_vmem):
            pltpu.sync_copy(table_hbm.at[i_vmem.at[0]], out_vmem)

        pltpu.emit_pipeline(
            body, grid=(n // win,),
            in_specs=[pl.BlockSpec((1, win), lambda i: (0, i))],
            out_specs=[pl.BlockSpec((win, w), lambda i: (i, 0), pipeline_mode=pl.Buffered(1))],
            core_axis_name=("core", "subcore"),
            dimension_semantics=(pltpu.PARALLEL,),
        )(i_hbm, out_hbm)

    return run(table, idx.reshape(1, n))


def _combine_body(h_ref, g_ref, rt_ref, gfin_ref, out_ref):
    rt = rt_ref[...]
    y = h_ref[...]
    for k in range(TOP_K):
        rows = jnp.concatenate([g_ref[k * SC_SPLIT + j] for j in range(SC_SPLIT)], axis=1)
        y = y + rt[:, 4 + k:5 + k] * _unpack_bf16_pair(rows)
    out_ref[...] = _rms(y, gfin_ref[...])


def _combine(h, g, rt, g_final):
    t, d = h.shape
    tm = TM_OUT
    tok = lambda i: (i, 0)
    return pl.pallas_call(
        _combine_body,
        grid=(t // tm,),
        in_specs=[pl.BlockSpec((tm, d), tok),
                  pl.BlockSpec((TOP_K * SC_SPLIT, tm, SC_ROW), lambda i: (0, i, 0)),
                  pl.BlockSpec((tm, LANES), tok),
                  pl.BlockSpec((1, d), lambda i: (0, 0))],
        out_specs=pl.BlockSpec((tm, d), tok),
        out_shape=jax.ShapeDtypeStruct((t, d), F32),
        compiler_params=pltpu.CompilerParams(dimension_semantics=("arbitrary",),
                                             vmem_limit_bytes=VMEM_LIMIT),
        name="combine",
    )(h, g, rt, g_final)


def _layer(h, mem, g_mix, g_mem, w_in, w_gate, b_gate, lam_re, lam_im, log_dt, b_re, b_im,
           c_re, c_im, d_skip, w_glu, b_glu, g_sgu, w_spatial, b_spatial, w_kv, w_branch,
           w_out, g_ffn, w_group, b_group, w_router, b_router, w1, w3, w2, g_out):
    bsz, s, d = h.shape
    t = bsz * s
    row = lambda a: a.reshape(1, -1)

    tril = jnp.tril(jnp.ones((CHUNK, CHUNK), dtype=bool))
    w_sp = jnp.where(tril, w_spatial, 0.0).astype(BF16)
    b_sp = jnp.broadcast_to(b_spatial[:, :, None], (SGU_HEADS, CHUNK, SGU_HEAD_DIM))
    u2, y_b, y_c = _in_proj(h, row(g_mix), w_in.astype(BF16), row(g_sgu), w_sp, b_sp,
                            mem, row(g_mem), w_kv)
    y2 = _ssm(u2, *_ssm_params(lam_re, lam_im, log_dt, b_re, b_im, c_re, c_im, d_skip), n_seq=bsz)

    pad = LANES - N_GROUPS - N_EXPERTS
    w_rt = jnp.concatenate([w_group, w_router, jnp.zeros((d, pad), F32)], axis=1)
    w_rt_hi = w_rt.astype(BF16)
    w_rt = jnp.concatenate([w_rt_hi, (w_rt - w_rt_hi.astype(F32)).astype(BF16)], axis=1)
    b_rt =jnp.concatenate([b_group, b_router, jnp.zeros((pad,), F32)]).reshape(1, LANES)
    h2, xnp, rt, rtt, cnt = _merge_route(
        h.reshape(t, d), y2, y_b.reshape(t, -1), y_c.reshape(t, -1), row(g_mix),
        w_gate, row(b_gate), w_glu, row(b_glu), w_branch, w_out, row(g_ffn), w_rt, b_rt)

    assert BM & (BM - 1) == 0, "block padding arithmetic assumes a power-of-two block"
    counts = cnt[0, ROUTE_LANE0:ROUTE_LANE0 + N_EXPERTS].astype(jnp.int32)
    nb = (t * TOP_K) // BM + N_EXPERTS
    assert nb < 256, "slot_rows keeps block counts in bf16 matmul operands"
    nslots = nb * BM
    dest_p = _slot_rows(rtt, cnt, nslots).reshape(TOP_K, SC_SPLIT * t)
    buf = _dispatch_rows(xnp.reshape(SC_SPLIT * t, SC_ROW), dest_p[0], dest_p[1], SC_SPLIT * nslots)
    yb = _experts(counts, buf.reshape(SC_SPLIT, nslots, SC_ROW), w1, w3, w2)
    g = _gather_rows(yb.reshape(SC_SPLIT * nslots, SC_ROW), dest_p.reshape(-1))
    out = _combine(h2, g.reshape(TOP_K * SC_SPLIT, t, SC_ROW), rt, row(g_out))
    return out.reshape(bsz, s, d)


def kernel(x, mem, g_mix, g_mem, w_in, w_gate, b_gate, lam_re, lam_im, log_dt, b_re, b_im, c_re,
           c_im, d_skip, w_glu, b_glu, g_sgu, w_spatial, b_spatial, w_kv, w_branch, w_out, g_ffn,
           w_group, b_group, w_router, b_router, w1, w3, w2, g_final):
    assert g_mix.shape[0] == 1, "single-layer stack"
    return _layer(x, mem, g_mix[0], g_mem[0], w_in[0], w_gate[0], b_gate[0], lam_re[0], lam_im[0],
                  log_dt[0], b_re[0], b_im[0], c_re[0], c_im[0], d_skip[0], w_glu[0], b_glu[0],
                  g_sgu[0], w_spatial[0], b_spatial[0], w_kv[0], w_branch[0], w_out[0], g_ffn[0],
                  w_group[0], b_group[0], w_router[0], b_router[0], w1[0], w3[0], w2[0], g_final)
```

```python
import functools
import math

import jax
import jax.numpy as jnp
from jax import lax
from jax.experimental import pallas as pl
from jax.experimental.pallas import tpu as pltpu
from jax.experimental.pallas import tpu_sc as plsc

F32 = jnp.float32
BF16 = jnp.bfloat16

EPS = 1e-6
D_MODEL = 1024
SSM_WIDTH = 512
SSM_GROUP = 16
SSM_GROUPS = 32
SSM_STATE = 64
SSM_CHUNK = 16
SSM_GROUPS_PER_STEP = 1
SGU_WIDTH = 512
SGU_HEADS = 4
SGU_HEAD_DIM = 128
CHUNK = 128
XA_HEADS = 4
XA_HEAD_DIM = 128
N_GROUPS = 8
EXPERTS_PER_GROUP = 8
N_EXPERTS = 64
TOP_K = 2
LANES = 128
ROUTE_LANE0 = N_GROUPS

TM_IN = 1024
TM_MERGE = 512
MERGE_COLS = 256
TM_OUT = 1024
BM = 512
BM_STEP = 128
BLOCKS_PER_STEP = 2
WEIGHT_SLOTS = 3
SC_WINDOW = 128
SC_ROW = 512
SC_SPLIT = (D_MODEL // 2) // SC_ROW
SLOT_LANES = 2048
VMEM_LIMIT = 56 * 1024 * 1024


def _rms(x, g):
    return x * lax.rsqrt(jnp.mean(x * x, axis=-1, keepdims=True) + EPS) * g


def _sigmoid(x):
    return 0.5 * (1.0 + jnp.tanh(0.5 * x))


def _gelu(x):
    c = math.sqrt(2.0 / math.pi)
    return 0.5 * x * (1.0 + jnp.tanh(c * (x + 0.044715 * (x * x * x))))


def _dot(a, b):
    return jnp.dot(a, b, preferred_element_type=F32)


_NT = (((1,), (1,)), ((), ()))


def _pack_bf16_pair(x):
    n = x.shape[1] // 2
    lo = lax.bitcast_convert_type(x[:, :n].astype(BF16).astype(F32), jnp.uint32)
    hi = lax.bitcast_convert_type(x[:, n:].astype(BF16).astype(F32), jnp.uint32)
    return hi | (lo >> 16)


def _unpack_bf16_pair(p):
    lo = lax.bitcast_convert_type(p << 16, F32)
    hi = lax.bitcast_convert_type(p & jnp.uint32(0xFFFF0000), F32)
    return jnp.concatenate([lo, hi], axis=1)


GROUPS_PER_TILE = LANES // SSM_GROUP
POS_PER_TILE = LANES // SSM_GROUP


def _slot_masks(rows):
    lane = lax.broadcasted_iota(jnp.int32, (rows, LANES), 1)
    return [(lane >= i * SSM_GROUP) & (lane < (i + 1) * SSM_GROUP) for i in range(LANES // SSM_GROUP)]


def _tokens_to_chunks(tok_ref, out_ref):
    tm = tok_ref.shape[1]
    nc = tm // SSM_CHUNK
    masks = _slot_masks(nc)
    for k in range(SSM_WIDTH // LANES):
        for j in range(SSM_CHUNK // POS_PER_TILE):
            src = [tok_ref[k, pl.ds(j * POS_PER_TILE + p, nc, stride=SSM_CHUNK), :]
                   for p in range(POS_PER_TILE)]
            for gi in range(GROUPS_PER_TILE):
                acc = None
                for p in range(POS_PER_TILE):
                    shift = ((p - gi) * SSM_GROUP) % LANES
                    r = pltpu.roll(src[p], shift, 1) if shift else src[p]
                    acc = r if acc is None else jnp.where(masks[p], r, acc)
                out_ref[k * GROUPS_PER_TILE + gi, :, pl.ds(j * LANES, LANES)] = acc.astype(out_ref.dtype)


def _chunks_to_tokens(chunk_ref, tok_ref):
    tm = tok_ref.shape[1]
    nc = tm // SSM_CHUNK
    masks = _slot_masks(nc)
    for k in range(SSM_WIDTH // LANES):
        for j in range(SSM_CHUNK // POS_PER_TILE):
            src = [chunk_ref[k * GROUPS_PER_TILE + gi, :, pl.ds(j * LANES, LANES)].astype(F32)
                   for gi in range(GROUPS_PER_TILE)]
            for p in range(POS_PER_TILE):
                acc = None
                for gi in range(GROUPS_PER_TILE):
                    shift = ((gi - p) * SSM_GROUP) % LANES
                    r = pltpu.roll(src[gi], shift, 1) if shift else src[gi]
                    acc = r if acc is None else jnp.where(masks[gi], r, acc)
                tok_ref[k, pl.ds(j * POS_PER_TILE + p, nc, stride=SSM_CHUNK), :] = acc


def _in_body(x_ref, gmix_ref, win_ref, gsgu_ref, wsp_ref, bsp_ref, mem_ref, gmem_ref, wkv_ref,
             u2_ref, yb_ref, yc_ref, tok_ref, k_ref, v_ref):
    @pl.when(pl.program_id(1) == 0)
    def _():
        kv = _dot(_rms(mem_ref[0], gmem_ref[...]).astype(BF16), wkv_ref[...].astype(BF16))
        k_ref[...] = kv[:, :XA_HEADS * XA_HEAD_DIM].astype(BF16)
        v_ref[...] = kv[:, XA_HEADS * XA_HEAD_DIM:].astype(BF16)

    n = _rms(x_ref[0], gmix_ref[...]).astype(BF16)
    proj = _dot(n, win_ref[...])
    for k in range(SSM_WIDTH // LANES):
        tok_ref[k] = proj[:, k * LANES:(k + 1) * LANES]
    _tokens_to_chunks(tok_ref, u2_ref)

    u = _gelu(proj[:, SSM_WIDTH:SSM_WIDTH + SGU_WIDTH])
    v = _gelu(proj[:, SSM_WIDTH + SGU_WIDTH:SSM_WIDTH + 2 * SGU_WIDTH])
    v = _rms(v, gsgu_ref[...]).astype(BF16)
    tm = u.shape[0]
    rows = []
    for c in range(tm // CHUNK):
        vc = v[c * CHUNK:(c + 1) * CHUNK]
        heads = []
        for h in range(SGU_HEADS):
            sl = slice(h * SGU_HEAD_DIM, (h + 1) * SGU_HEAD_DIM)
            heads.append(_dot(wsp_ref[h], vc[:, sl]) + bsp_ref[h])
        rows.append(jnp.concatenate(heads, axis=1))
    sv = jnp.concatenate(rows, axis=0)
    yb_ref[0] = (u * sv).astype(BF16)

    q = proj[:, SSM_WIDTH + 2 * SGU_WIDTH:].astype(BF16)
    kk = k_ref[...]
    vv = v_ref[...]
    outs = []
    for h in range(XA_HEADS):
        sl = slice(h * XA_HEAD_DIM, (h + 1) * XA_HEAD_DIM)
        s = lax.dot_general(q[:, sl], kk[:, sl], (((1,), (1,)), ((), ())),
                            preferred_element_type=F32) * (XA_HEAD_DIM ** -0.5)
        e = jnp.exp(s - jnp.max(s, axis=-1, keepdims=True))
        l = jnp.sum(e, axis=-1, keepdims=True)
        outs.append(_dot(e.astype(BF16), vv[:, sl]) / l)
    yc_ref[0] = jnp.concatenate(outs, axis=1).astype(BF16)


def _in_proj(x, g_mix, w_in, g_sgu, w_sp, b_sp, mem, g_mem, w_kv):
    b, s, d = x.shape
    m = mem.shape[1]
    xa = XA_HEADS * XA_HEAD_DIM
    const2 = lambda i, j: (0, 0)
    const3 = lambda i, j: (0, 0, 0)
    tok = lambda i, j: (i, j, 0)
    per_b = lambda i, j: (i, 0, 0)
    out = jax.ShapeDtypeStruct((b, s, SSM_WIDTH), BF16)
    nc = TM_IN // SSM_CHUNK
    tiles = s // TM_IN
    u2 = jax.ShapeDtypeStruct((SSM_GROUPS, b * s // SSM_CHUNK, SSM_CHUNK * SSM_GROUP), BF16)
    return pl.pallas_call(
        _in_body,
        grid=(b, s // TM_IN),
        in_specs=[pl.BlockSpec((1, TM_IN, d), tok),
                  pl.BlockSpec((1, d), const2),
                  pl.BlockSpec(w_in.shape, const2),
                  pl.BlockSpec((1, SGU_WIDTH), const2),
                  pl.BlockSpec(w_sp.shape, const3),
                  pl.BlockSpec(b_sp.shape, const3),
                  pl.BlockSpec((1, m, d), per_b),
                  pl.BlockSpec((1, d), const2),
                  pl.BlockSpec(w_kv.shape, const2, pipeline_mode=pl.Buffered(1))],
        out_specs=[pl.BlockSpec((SSM_GROUPS, nc, SSM_CHUNK * SSM_GROUP), lambda i, j: (0, i * tiles + j, 0)),
                   pl.BlockSpec((1, TM_IN, SSM_WIDTH), tok),
                   pl.BlockSpec((1, TM_IN, SSM_WIDTH), tok)],
        out_shape=[u2, out, out],
        scratch_shapes=[pltpu.VMEM((SSM_WIDTH // LANES, TM_IN, LANES), F32),
                        pltpu.VMEM((m, xa), BF16),
                        pltpu.VMEM((m, xa), BF16)],
        compiler_params=pltpu.CompilerParams(dimension_semantics=("arbitrary", "arbitrary"),
                                             vmem_limit_bytes=VMEM_LIMIT),
        name="in_proj",
    )(x, g_mix, w_in, g_sgu, w_sp, b_sp, mem, g_mem, w_kv)


def _alternate(*stages):
    live = list(stages)
    while live:
        live = [s for s in live if next(s, True) is None]


def _ssm_params(lam_re, lam_im, log_dt, b_re, b_im, c_re, c_im, d_skip):
    g, p = lam_re.shape
    dup = lambda a: jnp.concatenate([a, a], axis=-1)
    lam = jnp.stack([dup(lam_re), dup(lam_im), jnp.broadcast_to(log_dt[:, None], (g, 2 * p))], axis=1)
    brt = b_re.transpose(0, 2, 1)
    bit = b_im.transpose(0, 2, 1)
    cat = lambda a, b: jnp.concatenate([a, b], axis=-1)
    bc = jnp.stack([cat(brt, bit), cat(bit, brt), cat(c_re, -c_im), cat(-c_im, -c_re)], axis=1)
    d2 = jnp.tile(d_skip.reshape(g, 1, SSM_GROUP), (1, 1, SSM_CHUNK))
    return lam, bc, d2


def _ssm_operators(lam_ref, bc_ref, ccat_ref, n_ref, m_ref):
    lam_re = lam_ref[0:1, :]
    lam_im = lam_ref[1:2, :]
    dt = jnp.exp(lam_ref[2:3, :])
    ar = lam_re * dt
    ai = lam_im * dt
    lane = lax.broadcasted_iota(jnp.int32, (1, LANES), 1)
    sgn = jnp.where(lane >= SSM_STATE, 1.0, -1.0)

    def powers(j):
        mag = jnp.exp(ar * j)
        ph = ai * j
        return mag * jnp.cos(ph), mag * jnp.sin(ph)

    pos = lax.broadcasted_iota(jnp.int32, (SSM_CHUNK, 1), 0).astype(F32)
    p_re, p_im = powers(pos)
    r_re, r_im = powers((SSM_CHUNK - 1) - pos)
    one_re, one_im = powers(jnp.ones((1, 1), F32))
    q_re = p_re * one_re - p_im * one_im
    q_im = p_re * one_im + p_im * one_re
    step = lax.shift_left(jnp.full((8, 1), SSM_CHUNK, jnp.int32),
                          lax.broadcasted_iota(jnp.int32, (8, 1), 0)).astype(F32)
    s_re, s_im = powers(step)

    den = lam_re * lam_re + lam_im * lam_im
    f_re = ((one_re - 1.0) * lam_re + one_im * lam_im) / den
    f_im = (one_im * lam_re - (one_re - 1.0) * lam_im) / den
    b1, b2, ca, cb = bc_ref[0], bc_ref[1], bc_ref[2], bc_ref[3]
    bb1 = f_re * b1 + (sgn * f_im) * b2
    bb2 = f_re * b2 - (sgn * f_im) * b1
    r_ims = sgn * r_im
    for s in range(SSM_CHUNK):
        blk = pl.ds(s * SSM_GROUP, SSM_GROUP)
        ccat_ref[blk, :] = ca * p_re[s:s + 1, :] + cb * p_im[s:s + 1, :]
        m_ref[blk, :] = (ca * q_re[s:s + 1, :] + cb * q_im[s:s + 1, :]).astype(m_ref.dtype)
        n_ref[blk, :] = (bb1 * r_re[s:s + 1, :] + bb2 * r_ims[s:s + 1, :]).astype(n_ref.dtype)
    return bb1, s_re, sgn * s_im


def _ssm_body(u_ref, lam_ref, bc_ref, d2_ref, y_ref, toep_ref, ccat_ref, n_ref, m_ref, *, n_seq):
    for g in range(u_ref.shape[0]):
        _ssm_group(u_ref.at[g], lam_ref.at[g], bc_ref.at[g], d2_ref.at[g], y_ref.at[g],
                   toep_ref.at[g], ccat_ref.at[g], n_ref.at[g], m_ref.at[g], n_seq)


def _ssm_group(u_ref, lam_ref, bc_ref, d2_ref, y_ref, toep_ref, ccat_ref, n_ref, m_ref, n_seq):
    bcat, lr, li = _ssm_operators(lam_ref, bc_ref, ccat_ref, n_ref, m_ref)
    kern = lax.dot_general(bcat, ccat_ref[...], _NT, precision=lax.Precision.HIGHEST,
                           preferred_element_type=F32)
    col = lax.broadcasted_iota(jnp.int32, kern.shape, 1)
    for s in range(SSM_CHUNK):
        shifted = pltpu.roll(kern, s * SSM_GROUP, 1) if s else kern
        toep_ref[s * SSM_GROUP:(s + 1) * SSM_GROUP, :] = jnp.where(
            col >= s * SSM_GROUP, shifted, 0.0).astype(BF16)

    u = u_ref[...]
    rows = u.shape[0]
    per = rows // n_seq
    y = _dot(u, toep_ref[...])
    st = _dot(u, n_ref[...])
    row = lax.broadcasted_iota(jnp.int32, (per, LANES), 0)
    prev = []
    for b in range(n_seq):
        x = st[b * per:(b + 1) * per]
        k = 0
        while (1 << k) < per:
            d = 1 << k
            sh = jnp.where(row >= d, pltpu.roll(x, d, 0), 0.0)
            x = x + sh * lr[k:k + 1, :] + pltpu.roll(sh, SSM_STATE, 1) * li[k:k + 1, :]
            k += 1
        prev.append(jnp.where(row >= 1, pltpu.roll(x, 1, 0), 0.0))
    xp = jnp.concatenate(prev, axis=0).astype(BF16)
    y = y + lax.dot_general(xp, m_ref[...], _NT, preferred_element_type=F32) + d2_ref[...] * u.astype(F32)
    y_ref[...] = y.astype(BF16)


def _ssm(u2, lam, bc, d2, n_seq):
    g, rows, w = u2.shape
    assert rows // n_seq <= 1 << 8, "lam_bar^(16*2^k) is prepared for 8 scan steps"
    gs = SSM_GROUPS_PER_STEP
    blk = lambda a: pl.BlockSpec((gs,) + a.shape[1:], lambda i: (i,) + (0,) * (a.ndim - 1))
    return pl.pallas_call(
        functools.partial(_ssm_body, n_seq=n_seq),
        grid=(g // gs,),
        in_specs=[blk(u2), blk(lam), blk(bc), blk(d2)],
        out_specs=blk(u2),
        out_shape=jax.ShapeDtypeStruct(u2.shape, BF16),
        scratch_shapes=[pltpu.VMEM((gs, w, w), BF16),
                        pltpu.VMEM((gs, w, 2 * SSM_STATE), F32),
                        pltpu.VMEM((gs, w, 2 * SSM_STATE), BF16),
                        pltpu.VMEM((gs, w, 2 * SSM_STATE), BF16)],
        compiler_params=pltpu.CompilerParams(dimension_semantics=("arbitrary",),
                                             vmem_limit_bytes=VMEM_LIMIT),
        name="ssm",
    )(u2, lam, bc, d2)


def _mix_tile(x_ref, y2_ref, yb_ref, yc_ref, gmix_ref, wgate_ref, bgate_ref, wglu_ref, bglu_ref,
              wbr_ref, wout_ref, tok_ref, h_ref, hkeep_ref):
    x = x_ref[...]
    n = _rms(x, gmix_ref[...]).astype(BF16)

    def gated(b, c, y):
        cols = pl.ds(b * D_MODEL + c * MERGE_COLS, MERGE_COLS)
        gate = _sigmoid(_dot(n, wgate_ref[:, cols].astype(BF16)) + bgate_ref[:, cols])
        return gate * _dot(y, wbr_ref[b, :, pl.ds(c * MERGE_COLS, MERGE_COLS)].astype(BF16))

    n_blocks = D_MODEL // MERGE_COLS
    yb = yb_ref[...]
    yc = yc_ref[...]
    head = gated(1, 0, yb) + gated(2, 0, yc)
    yield
    _chunks_to_tokens(y2_ref, tok_ref)
    ys = _gelu(jnp.concatenate([tok_ref[k] for k in range(SSM_WIDTH // LANES)], axis=1)).astype(BF16)
    glu = _dot(ys, wglu_ref[...].astype(BF16)) + bglu_ref[...]
    ya = (glu[:, :SSM_WIDTH] * _sigmoid(glu[:, SSM_WIDTH:])).astype(BF16)
    merged = [(head + gated(0, 0, ya)).astype(BF16)]
    for c in range(1, n_blocks):
        yield
        acc = gated(0, c, ya)
        yield
        acc = acc + gated(1, c, yb)
        yield
        merged.append((acc + gated(2, c, yc)).astype(BF16))
    yield
    h = x + _dot(jnp.concatenate(merged, axis=1), wout_ref[...].astype(BF16))
    h_ref[...] = h
    hkeep_ref[...] = h


def _route_tile(hkeep_ref, gffn_ref, wrt_ref, brt_ref, xnp_ref, rt_ref, rtt_ref, cnt_ref, carry_ref):
    h = hkeep_ref[...]
    xn = _rms(h, gffn_ref[...])
    yield
    packed = _pack_bf16_pair(xn)
    for j in range(SC_SPLIT):
        xnp_ref[j] = packed[:, j * SC_ROW:(j + 1) * SC_ROW]
    yield

    x_hi = xn.astype(BF16)
    x_lo = (xn - x_hi.astype(F32)).astype(BF16)
    head = _dot(x_hi, wrt_ref[...])
    logits = (head[:, :LANES] + head[:, LANES:] + _dot(x_lo, wrt_ref[:, :LANES])) + brt_ref[...]
    yield
    tm = logits.shape[0]
    lane_i = lax.broadcasted_iota(jnp.int32, (tm, LANES), 1)
    lane = lane_i.astype(F32)
    neg = jnp.float32(-3.0e38)
    big = jnp.float32(LANES)
    gmask = lane_i < N_GROUPS
    gl = jnp.where(gmask, logits, neg)
    gmax = jnp.max(gl, axis=-1, keepdims=True)
    gidx = jnp.min(jnp.where(gl == gmax, lane, big), axis=-1, keepdims=True)
    gsum = jnp.sum(jnp.where(gmask, jnp.exp(gl - gmax), 0.0), axis=-1, keepdims=True)
    g_w = 1.0 / gsum
    yield
    e_lane = lane_i - ROUTE_LANE0
    lane_group = (e_lane // EXPERTS_PER_GROUP).astype(F32)
    emask = (e_lane >= 0) & (e_lane < N_EXPERTS) & (lane_group == gidx)
    el = jnp.where(emask, logits, neg)
    m1 = jnp.max(el, axis=-1, keepdims=True)
    i1 = jnp.min(jnp.where(el == m1, lane, big), axis=-1, keepdims=True)
    yield
    el2 = jnp.where(lane == i1, neg, el)
    m2 = jnp.max(el2, axis=-1, keepdims=True)
    i2 = jnp.min(jnp.where(el2 == m2, lane, big), axis=-1, keepdims=True)
    t = jnp.exp(m2 - m1)
    w1 = g_w / (1.0 + t)
    w2 = g_w * t / (1.0 + t)
    yield

    sel1 = lane == i1
    sel2 = lane == i2
    onehot = jnp.where(sel1 | sel2, 1.0, 0.0)
    r_i = lax.broadcasted_iota(jnp.int32, (tm, tm), 0)
    c_i = lax.broadcasted_iota(jnp.int32, (tm, tm), 1)
    stril = jnp.where(c_i < r_i, 1.0, 0.0).astype(BF16)
    cum = _dot(stril, onehot.astype(BF16)) + carry_ref[0:1, :]
    rank1 = jnp.sum(jnp.where(sel1, cum, 0.0), axis=-1, keepdims=True)
    rank2 = jnp.sum(jnp.where(sel2, cum, 0.0), axis=-1, keepdims=True)
    carry_ref[...] = carry_ref[...] + jnp.sum(onehot, axis=0, keepdims=True)
    cnt_ref[...] = carry_ref[...]
    yield

    cols = (i1 - ROUTE_LANE0, i2 - ROUTE_LANE0, rank1, rank2, w1, w2)
    rt = jnp.zeros((tm, LANES), F32)
    for c, val in enumerate(cols):
        rt = jnp.where(lane_i == c, val, rt)
    rt_ref[...] = rt
    rtt_ref[...] = rt.T[:8]


def _merge_body(x_ref, y2_ref, yb_ref, yc_ref, gmix_ref, wgate_ref, bgate_ref, wglu_ref, bglu_ref,
                wbr_ref, wout_ref, gffn_ref, wrt_ref, brt_ref,
                h_ref, xnp_ref, rt_ref, rtt_ref, cnt_ref, carry_ref, tok_ref, hkeep_ref):
    i = pl.program_id(0)
    last = pl.num_programs(0) - 1
    cur = hkeep_ref.at[i % 2]
    prev = hkeep_ref.at[(i + 1) % 2]

    def mix():
        return _mix_tile(x_ref, y2_ref, yb_ref, yc_ref, gmix_ref, wgate_ref, bgate_ref, wglu_ref,
                         bglu_ref, wbr_ref, wout_ref, tok_ref, h_ref, cur)

    def route():
        return _route_tile(prev, gffn_ref, wrt_ref, brt_ref, xnp_ref, rt_ref, rtt_ref, cnt_ref, carry_ref)

    @pl.when(i == 0)
    def _():
        carry_ref[...] = jnp.zeros_like(carry_ref)
        _alternate(mix())

    @pl.when((i > 0) & (i < last))
    def _():
        _alternate(route(), mix())

    @pl.when(i == last)
    def _():
        _alternate(route())


def _merge_route(x, y2, yb, yc, g_mix, w_gate, b_gate, w_glu, b_glu, w_br, w_out, g_ffn, w_rt, b_rt):
    t, d = x.shape
    tm = TM_MERGE
    tiles = t // tm
    mixed = lambda i: jnp.minimum(i, tiles - 1)
    routed = lambda i: jnp.maximum(i - 1, 0)
    c2 = lambda i: (0, 0)
    c3 = lambda i: (0, 0, 0)
    full = lambda a: pl.BlockSpec(a.shape, c2 if a.ndim == 2 else c3, pipeline_mode=pl.Buffered(1))
    return pl.pallas_call(
        _merge_body,
        grid=(tiles + 1,),
        in_specs=[pl.BlockSpec((tm, d), lambda i: (mixed(i), 0)),
                  pl.BlockSpec((SSM_GROUPS, tm // SSM_CHUNK, SSM_CHUNK * SSM_GROUP),
                               lambda i: (0, mixed(i), 0)),
                  pl.BlockSpec((tm, SSM_WIDTH), lambda i: (mixed(i), 0)),
                  pl.BlockSpec((tm, SSM_WIDTH), lambda i: (mixed(i), 0)),
                  full(g_mix), full(w_gate), full(b_gate), full(w_glu), full(b_glu),
                  full(w_br), full(w_out), full(g_ffn), full(w_rt), full(b_rt)],
        out_specs=[pl.BlockSpec((tm, d), lambda i: (mixed(i), 0)),
                   pl.BlockSpec((SC_SPLIT, tm, SC_ROW), lambda i: (0, routed(i), 0)),
                   pl.BlockSpec((tm, LANES), lambda i: (routed(i), 0)),
                   pl.BlockSpec((8, tm), lambda i: (0, routed(i))),
                   pl.BlockSpec((8, LANES), c2)],
        out_shape=[jax.ShapeDtypeStruct((t, d), F32),
                   jax.ShapeDtypeStruct((SC_SPLIT, t, SC_ROW), jnp.uint32),
                   jax.ShapeDtypeStruct((t, LANES), F32),
                   jax.ShapeDtypeStruct((8, t), F32),
                   jax.ShapeDtypeStruct((8, LANES), F32)],
        scratch_shapes=[pltpu.VMEM((8, LANES), F32),
                        pltpu.VMEM((SSM_WIDTH // LANES, tm, LANES), F32),
                        pltpu.VMEM((2, tm, d), F32)],
        compiler_params=pltpu.CompilerParams(dimension_semantics=("arbitrary",),
                                             vmem_limit_bytes=VMEM_LIMIT),
        name="merge_route",
    )(x, y2, yb, yc, g_mix, w_gate, b_gate, w_glu, b_glu, w_br, w_out, g_ffn, w_rt, b_rt)


def _slot_body(rtt_ref, cnt_ref, out_ref, *, nslots):
    tl = rtt_ref.shape[1]
    blocks = jnp.ceil(cnt_ref[...] * (1.0 / BM))
    k_i = lax.broadcasted_iota(jnp.int32, (LANES, LANES), 0)
    l_i = lax.broadcasted_iota(jnp.int32, (LANES, LANES), 1)
    before = jnp.where(k_i < l_i, 1.0, 0.0).astype(BF16)
    first_blk = _dot(blocks.astype(BF16), before).astype(BF16)
    lane_of = lax.broadcasted_iota(jnp.int32, (LANES, tl), 0).astype(F32) - ROUTE_LANE0
    for k in range(TOP_K):
        onehot = jnp.where(rtt_ref[k:k + 1, :] == lane_of, 1.0, 0.0).astype(BF16)
        start = _dot(first_blk, onehot) * BM
        slot = (start[0:1, :] + rtt_ref[TOP_K + k:TOP_K + k + 1, :]).astype(jnp.int32)
        for j in range(SC_SPLIT):
            out_ref[k * SC_SPLIT + j:k * SC_SPLIT + j + 1, :] = slot + j * nslots


def _slot_rows(rtt, cnt, nslots):
    t = rtt.shape[1]
    tl = SLOT_LANES
    return pl.pallas_call(
        functools.partial(_slot_body, nslots=nslots),
        grid=(t // tl,),
        in_specs=[pl.BlockSpec((8, tl), lambda i: (0, i)),
                  pl.BlockSpec((8, LANES), lambda i: (0, 0))],
        out_specs=pl.BlockSpec((TOP_K * SC_SPLIT, tl), lambda i: (0, i)),
        out_shape=jax.ShapeDtypeStruct((TOP_K * SC_SPLIT, t), jnp.int32),
        compiler_params=pltpu.CompilerParams(dimension_semantics=("arbitrary",)),
        name="slot_rows",
    )(rtt, cnt)


def _plan_blocks(counts_ref, blk_expert, blk_valid, blk_first, blk_run, run_expert):
    nb = blk_expert.shape[0]

    def per_expert(e, carry):
        cursor, run = carry
        count = counts_ref[e]
        n_blk = (count + (BM - 1)) // BM

        def per_block(b, cur):
            blk_expert[cur] = e
            blk_valid[cur] = jnp.minimum(count - b * BM, BM)
            blk_first[cur] = (b == 0).astype(jnp.int32)
            blk_run[cur] = run
            return cur + 1

        run_expert[run] = e
        return lax.fori_loop(0, n_blk, per_block, cursor), run + (n_blk > 0).astype(jnp.int32)

    cursor, runs = lax.fori_loop(0, counts_ref.shape[0], per_expert, (jnp.int32(0), jnp.int32(0)))
    run_expert[runs] = -1
    run_expert[runs + 1] = -1

    def empty(j, carry):
        blk_expert[j] = 0
        blk_valid[j] = 0
        blk_first[j] = 0
        blk_run[j] = runs - 1
        return carry

    lax.fori_loop(cursor, nb, empty, 0)


def _expert_body(counts_ref, buf_ref, w1_hbm, w3_hbm, w2_hbm, out_ref, w1_buf, w3_buf, w2_buf, sem,
                 blk_expert, blk_valid, blk_first, blk_run, run_expert):
    step = pl.program_id(0)

    @pl.when(step == 0)
    def _():
        _plan_blocks(counts_ref, blk_expert, blk_valid, blk_first, blk_run, run_expert)

    def weight_copies(e, s):
        return (pltpu.make_async_copy(w1_hbm.at[e], w1_buf.at[s], sem.at[s, 0]),
                pltpu.make_async_copy(w3_hbm.at[e], w3_buf.at[s], sem.at[s, 1]),
                pltpu.make_async_copy(w2_hbm.at[e], w2_buf.at[s], sem.at[s, 2]))

    for sub in range(BLOCKS_PER_STEP):
        i = step * BLOCKS_PER_STEP + sub
        block = pl.ds(sub * BM, BM)
        _expert_block(i, buf_ref.at[:, block, :], out_ref.at[:, block, :], weight_copies,
                      (w1_buf, w3_buf, w2_buf), blk_expert, blk_valid, blk_first, blk_run, run_expert)


def _expert_block(i, buf_ref, out_ref, weight_copies, weights, blk_expert, blk_valid, blk_first, blk_run,
                  run_expert):
    w1_buf, w3_buf, w2_buf = weights
    expert = blk_expert[i]
    valid = blk_valid[i]
    run = blk_run[i]
    slot = run % WEIGHT_SLOTS
    ahead1 = run_expert[run + 1]
    ahead2 = run_expert[run + 2]

    @pl.when(i == 0)
    def _():
        for c in weight_copies(expert, slot):
            c.start()

        @pl.when(ahead1 >= 0)
        def _():
            for c in weight_copies(ahead1, (slot + 1) % WEIGHT_SLOTS):
                c.start()

    @pl.when(blk_first[i] == 1)
    def _():
        for c in weight_copies(expert, slot):
            c.wait()

        @pl.when(ahead2 >= 0)
        def _():
            for c in weight_copies(ahead2, (slot + 2) % WEIGHT_SLOTS):
                c.start()

    def mlp(rows):
        x = _unpack_bf16_pair(jnp.concatenate([buf_ref[j, :rows, :] for j in range(SC_SPLIT)], axis=1))
        row = lax.broadcasted_iota(jnp.int32, x.shape, 0)
        x = jnp.where(row < valid, x, 0.0).astype(BF16)
        h1 = _dot(x, w1_buf[slot].astype(BF16))
        h3 = _dot(x, w3_buf[slot].astype(BF16))
        a = (h1 * _sigmoid(h1) * h3).astype(BF16)
        packed = _pack_bf16_pair(_dot(a, w2_buf[slot].astype(BF16)))
        for j in range(SC_SPLIT):
            out_ref[j, :rows, :] = packed[:, j * SC_ROW:(j + 1) * SC_ROW]
            if rows < BM:
                out_ref[j, rows:, :] = jnp.zeros((BM - rows, SC_ROW), out_ref.dtype)

    for rows in range(BM_STEP, BM + 1, BM_STEP):
        @pl.when((valid > rows - BM_STEP) & (valid <= rows))
        def _():
            mlp(rows)

    @pl.when(valid <= 0)
    def _():
        out_ref[...] = jnp.zeros_like(out_ref)


def _experts(counts, buf, w1, w3, w2):
    _, nslots, _ = buf.shape
    nb = nslots // BM
    rows = pl.BlockSpec((SC_SPLIT, BLOCKS_PER_STEP * BM, SC_ROW), lambda i, counts: (0, i, 0))
    hbm = pl.BlockSpec(memory_space=pl.ANY)
    table = pltpu.SMEM((nb,), jnp.int32)
    grid_spec = pltpu.PrefetchScalarGridSpec(
        num_scalar_prefetch=1,
        grid=(nb // BLOCKS_PER_STEP,),
        in_specs=[rows, hbm, hbm, hbm],
        out_specs=rows,
        scratch_shapes=[pltpu.VMEM((WEIGHT_SLOTS,) + w1.shape[1:], w1.dtype),
                        pltpu.VMEM((WEIGHT_SLOTS,) + w3.shape[1:], w3.dtype),
                        pltpu.VMEM((WEIGHT_SLOTS,) + w2.shape[1:], w2.dtype),
                        pltpu.SemaphoreType.DMA((WEIGHT_SLOTS, 3)),
                        table, table, table, table,
                        pltpu.SMEM((counts.shape[0] + 2,), jnp.int32)],
    )
    return pl.pallas_call(
        _expert_body,
        grid_spec=grid_spec,
        out_shape=jax.ShapeDtypeStruct(buf.shape, jnp.uint32),
        compiler_params=pltpu.CompilerParams(dimension_semantics=("arbitrary",),
                                             vmem_limit_bytes=VMEM_LIMIT),
        name="experts",
    )(counts, buf, w1, w3, w2)


def _sc_mesh():
    return plsc.VectorSubcoreMesh(core_axis_name="core", subcore_axis_name="subcore")


def _dispatch_rows(rows, dest0, dest1, nslots):
    t, w = rows.shape
    win = SC_WINDOW
    idx_spec = pl.BlockSpec((1, win), lambda i: (0, i))

    @functools.partial(pl.kernel, mesh=_sc_mesh(), scratch_types=[],
                       out_type=jax.ShapeDtypeStruct((nslots, w), rows.dtype), name="dispatch_rows")
    def run(rows_hbm, i0_hbm, i1_hbm, out_hbm):
        def body(rows_vmem, i0_vmem, i1_vmem):
            pltpu.sync_copy(rows_vmem, out_hbm.at[i0_vmem.at[0]])
            pltpu.sync_copy(rows_vmem, out_hbm.at[i1_vmem.at[0]])

        pltpu.emit_pipeline(
            body, grid=(t // win,),
            in_specs=[pl.BlockSpec((win, w), lambda i: (i, 0), pipeline_mode=pl.Buffered(1)),
                      idx_spec, idx_spec],
            out_specs=[],
            core_axis_name=("core", "subcore"),
            dimension_semantics=(pltpu.PARALLEL,),
        )(rows_hbm, i0_hbm, i1_hbm)

    return run(rows, dest0.reshape(1, t), dest1.reshape(1, t))


def _gather_rows(table, idx):
    n = idx.shape[0]
    w = table.shape[1]
    win = SC_WINDOW

    @functools.partial(pl.kernel, mesh=_sc_mesh(), scratch_types=[],
                       out_type=jax.ShapeDtypeStruct((n, w), table.dtype), name="gather_rows")
    def run(table_hbm, i_hbm, out_hbm):
        def body(i_vmem, out_vmem):
            pltpu.sync_copy(table_hbm.at[i_vmem.at[0]], out_vmem)

        pltpu.emit_pipeline(
            body, grid=(n // win,),
            in_specs=[pl.BlockSpec((1, win), lambda i: (0, i))],
            out_specs=[pl.BlockSpec((win, w), lambda i: (i, 0), pipeline_mode=pl.Buffered(1))],
            core_axis_name=("core", "subcore"),
            dimension_semantics=(pltpu.PARALLEL,),
        )(i_hbm, out_hbm)

    return run(table, idx.reshape(1, n))


def _combine_body(h_ref, g_ref, rt_ref, gfin_ref, out_ref):
    rt = rt_ref[...]
    y = h_ref[...]
    for k in range(TOP_K):
        rows = jnp.concatenate([g_ref[k * SC_SPLIT + j] for j in range(SC_SPLIT)], axis=1)
        y = y + rt[:, 4 + k:5 + k] * _unpack_bf16_pair(rows)
    out_ref[...] = _rms(y, gfin_ref[...])


def _combine(h, g, rt, g_final):
    t, d = h.shape
    tm = TM_OUT
    tok = lambda i: (i, 0)
    return pl.pallas_call(
        _combine_body,
        grid=(t // tm,),
        in_specs=[pl.BlockSpec((tm, d), tok),
                  pl.BlockSpec((TOP_K * SC_SPLIT, tm, SC_ROW), lambda i: (0, i, 0)),
                  pl.BlockSpec((tm, LANES), tok),
                  pl.BlockSpec((1, d), lambda i: (0, 0))],
        out_specs=pl.BlockSpec((tm, d), tok),
        out_shape=jax.ShapeDtypeStruct((t, d), F32),
        compiler_params=pltpu.CompilerParams(dimension_semantics=("arbitrary",),
                                             vmem_limit_bytes=VMEM_LIMIT),
        name="combine",
    )(h, g, rt, g_final)


def _layer(h, mem, g_mix, g_mem, w_in, w_gate, b_gate, lam_re, lam_im, log_dt, b_re, b_im,
           c_re, c_im, d_skip, w_glu, b_glu, g_sgu, w_spatial, b_spatial, w_kv, w_branch,
           w_out, g_ffn, w_group, b_group, w_router, b_router, w1, w3, w2, g_out):
    bsz, s, d = h.shape
    t = bsz * s
    row = lambda a: a.reshape(1, -1)

    tril = jnp.tril(jnp.ones((CHUNK, CHUNK), dtype=bool))
    w_sp = jnp.where(tril, w_spatial, 0.0).astype(BF16)
    b_sp = jnp.broadcast_to(b_spatial[:, :, None], (SGU_HEADS, CHUNK, SGU_HEAD_DIM))
    u2, y_b, y_c = _in_proj(h, row(g_mix), w_in.astype(BF16), row(g_sgu), w_sp, b_sp,
                            mem, row(g_mem), w_kv)
    y2 = _ssm(u2, *_ssm_params(lam_re, lam_im, log_dt, b_re, b_im, c_re, c_im, d_skip), n_seq=bsz)

    pad = LANES - N_GROUPS - N_EXPERTS
    w_rt = jnp.concatenate([w_group, w_router, jnp.zeros((d, pad), F32)], axis=1)
    w_rt_hi = w_rt.astype(BF16)
    w_rt = jnp.concatenate([w_rt_hi, (w_rt - w_rt_hi.astype(F32)).astype(BF16)], axis=1)
    b_rt =jnp.concatenate([b_group, b_router, jnp.zeros((pad,), F32)]).reshape(1, LANES)
    h2, xnp, rt, rtt, cnt = _merge_route(
        h.reshape(t, d), y2, y_b.reshape(t, -1), y_c.reshape(t, -1), row(g_mix),
        w_gate, row(b_gate), w_glu, row(b_glu), w_branch, w_out, row(g_ffn), w_rt, b_rt)

    assert BM & (BM - 1) == 0, "block padding arithmetic assumes a power-of-two block"
    counts = cnt[0, ROUTE_LANE0:ROUTE_LANE0 + N_EXPERTS].astype(jnp.int32)
    nb = (t * TOP_K) // BM + N_EXPERTS
    assert nb < 256, "slot_rows keeps block counts in bf16 matmul operands"
    nslots = nb * BM
    dest_p = _slot_rows(rtt, cnt, nslots).reshape(TOP_K, SC_SPLIT * t)
    buf = _dispatch_rows(xnp.reshape(SC_SPLIT * t, SC_ROW), dest_p[0], dest_p[1], SC_SPLIT * nslots)
    yb = _experts(counts, buf.reshape(SC_SPLIT, nslots, SC_ROW), w1, w3, w2)
    g = _gather_rows(yb.reshape(SC_SPLIT * nslots, SC_ROW), dest_p.reshape(-1))
    out = _combine(h2, g.reshape(TOP_K * SC_SPLIT, t, SC_ROW), rt, row(g_out))
    return out.reshape(bsz, s, d)


def kernel(x, mem, g_mix, g_mem, w_in, w_gate, b_gate, lam_re, lam_im, log_dt, b_re, b_im, c_re,
           c_im, d_skip, w_glu, b_glu, g_sgu, w_spatial, b_spatial, w_kv, w_branch, w_out, g_ffn,
           w_group, b_group, w_router, b_router, w1, w3, w2, g_final):
    assert g_mix.shape[0] == 1, "single-layer stack"
    return _layer(x, mem, g_mix[0], g_mem[0], w_in[0], w_gate[0], b_gate[0], lam_re[0], lam_im[0],
                  log_dt[0], b_re[0], b_im[0], c_re[0], c_im[0], d_skip[0], w_glu[0], b_glu[0],
                  g_sgu[0], w_spatial[0], b_spatial[0], w_kv[0], w_branch[0], w_out[0], g_ffn[0],
                  w_group[0], b_group[0], w_router[0], b_router[0], w1[0], w3[0], w2[0], g_final)
```

```python
import functools
import math

import jax
import jax.numpy as jnp
from jax import lax
from jax.experimental import pallas as pl
from jax.experimental.pallas import tpu as pltpu
from jax.experimental.pallas import tpu_sc as plsc

F32 = jnp.float32
BF16 = jnp.bfloat16

EPS = 1e-6
D_MODEL = 1024
SSM_WIDTH = 512
SSM_GROUP = 16
SSM_GROUPS = 32
SSM_STATE = 64
SSM_CHUNK = 16
SSM_GROUPS_PER_STEP = 1
SGU_WIDTH = 512
SGU_HEADS = 4
SGU_HEAD_DIM = 128
CHUNK = 128
XA_HEADS = 4
XA_HEAD_DIM = 128
N_GROUPS = 8
EXPERTS_PER_GROUP = 8
N_EXPERTS = 64
TOP_K = 2
LANES = 128
ROUTE_LANE0 = N_GROUPS

TM_IN = 1024
TM_MERGE = 512
MERGE_COLS = 256
TM_OUT = 1024
BM = 512
BM_STEP = 128
BLOCKS_PER_STEP = 2
WEIGHT_SLOTS = 3
WEIGHT_DMA_PRIORITY = 1
SC_WINDOW = 128
SC_ROW = 512
SC_SPLIT = (D_MODEL // 2) // SC_ROW
SLOT_LANES = 2048
VMEM_LIMIT = 56 * 1024 * 1024


def _rms(x, g):
    return x * lax.rsqrt(jnp.mean(x * x, axis=-1, keepdims=True) + EPS) * g


def _sigmoid(x):
    return 0.5 * (1.0 + jnp.tanh(0.5 * x))


def _gelu(x):
    c = math.sqrt(2.0 / math.pi)
    return 0.5 * x * (1.0 + jnp.tanh(c * (x + 0.044715 * (x * x * x))))


def _dot(a, b):
    return jnp.dot(a, b, preferred_element_type=F32)


_NT = (((1,), (1,)), ((), ()))


def _pack_bf16_pair(x):
    n = x.shape[1] // 2
    lo = lax.bitcast_convert_type(x[:, :n].astype(BF16).astype(F32), jnp.uint32)
    hi = lax.bitcast_convert_type(x[:, n:].astype(BF16).astype(F32), jnp.uint32)
    return hi | (lo >> 16)


def _unpack_bf16_pair(p):
    lo = lax.bitcast_convert_type(p << 16, F32)
    hi = lax.bitcast_convert_type(p & jnp.uint32(0xFFFF0000), F32)
    return jnp.concatenate([lo, hi], axis=1)


GROUPS_PER_TILE = LANES // SSM_GROUP
POS_PER_TILE = LANES // SSM_GROUP


def _slot_masks(rows):
    lane = lax.broadcasted_iota(jnp.int32, (rows, LANES), 1)
    return [(lane >= i * SSM_GROUP) & (lane < (i + 1) * SSM_GROUP) for i in range(LANES // SSM_GROUP)]


def _tokens_to_chunks(tok_ref, out_ref):
    tm = tok_ref.shape[1]
    nc = tm // SSM_CHUNK
    masks = _slot_masks(nc)
    for k in range(SSM_WIDTH // LANES):
        for j in range(SSM_CHUNK // POS_PER_TILE):
            src = [tok_ref[k, pl.ds(j * POS_PER_TILE + p, nc, stride=SSM_CHUNK), :]
                   for p in range(POS_PER_TILE)]
            for gi in range(GROUPS_PER_TILE):
                acc = None
                for p in range(POS_PER_TILE):
                    shift = ((p - gi) * SSM_GROUP) % LANES
                    r = pltpu.roll(src[p], shift, 1) if shift else src[p]
                    acc = r if acc is None else jnp.where(masks[p], r, acc)
                out_ref[k * GROUPS_PER_TILE + gi, :, pl.ds(j * LANES, LANES)] = acc.astype(out_ref.dtype)


def _chunks_to_tokens(chunk_ref, tok_ref):
    tm = tok_ref.shape[1]
    nc = tm // SSM_CHUNK
    masks = _slot_masks(nc)
    for k in range(SSM_WIDTH // LANES):
        for j in range(SSM_CHUNK // POS_PER_TILE):
            src = [chunk_ref[k * GROUPS_PER_TILE + gi, :, pl.ds(j * LANES, LANES)].astype(F32)
                   for gi in range(GROUPS_PER_TILE)]
            for p in range(POS_PER_TILE):
                acc = None
                for gi in range(GROUPS_PER_TILE):
                    shift = ((gi - p) * SSM_GROUP) % LANES
                    r = pltpu.roll(src[gi], shift, 1) if shift else src[gi]
                    acc = r if acc is None else jnp.where(masks[gi], r, acc)
                tok_ref[k, pl.ds(j * POS_PER_TILE + p, nc, stride=SSM_CHUNK), :] = acc


def _in_body(x_ref, gmix_ref, win_ref, gsgu_ref, wsp_ref, bsp_ref, mem_ref, gmem_ref, wkv_ref,
             u2_ref, yb_ref, yc_ref, tok_ref, k_ref, v_ref):
    @pl.when(pl.program_id(1) == 0)
    def _():
        kv = _dot(_rms(mem_ref[0], gmem_ref[...]).astype(BF16), wkv_ref[...].astype(BF16))
        k_ref[...] = kv[:, :XA_HEADS * XA_HEAD_DIM].astype(BF16)
        v_ref[...] = kv[:, XA_HEADS * XA_HEAD_DIM:].astype(BF16)

    n = _rms(x_ref[0], gmix_ref[...]).astype(BF16)
    proj = _dot(n, win_ref[...])
    for k in range(SSM_WIDTH // LANES):
        tok_ref[k] = proj[:, k * LANES:(k + 1) * LANES]
    _tokens_to_chunks(tok_ref, u2_ref)

    u = _gelu(proj[:, SSM_WIDTH:SSM_WIDTH + SGU_WIDTH])
    v = _gelu(proj[:, SSM_WIDTH + SGU_WIDTH:SSM_WIDTH + 2 * SGU_WIDTH])
    v = _rms(v, gsgu_ref[...]).astype(BF16)
    tm = u.shape[0]
    rows = []
    for c in range(tm // CHUNK):
        vc = v[c * CHUNK:(c + 1) * CHUNK]
        heads = []
        for h in range(SGU_HEADS):
            sl = slice(h * SGU_HEAD_DIM, (h + 1) * SGU_HEAD_DIM)
            heads.append(_dot(wsp_ref[h], vc[:, sl]) + bsp_ref[h])
        rows.append(jnp.concatenate(heads, axis=1))
    sv = jnp.concatenate(rows, axis=0)
    yb_ref[0] = (u * sv).astype(BF16)

    q = proj[:, SSM_WIDTH + 2 * SGU_WIDTH:].astype(BF16)
    kk = k_ref[...]
    vv = v_ref[...]
    outs = []
    for h in range(XA_HEADS):
        sl = slice(h * XA_HEAD_DIM, (h + 1) * XA_HEAD_DIM)
        s = lax.dot_general(q[:, sl], kk[:, sl], (((1,), (1,)), ((), ())),
                            preferred_element_type=F32) * (XA_HEAD_DIM ** -0.5)
        e = jnp.exp(s - jnp.max(s, axis=-1, keepdims=True))
        l = jnp.sum(e, axis=-1, keepdims=True)
        outs.append(_dot(e.astype(BF16), vv[:, sl]) / l)
    yc_ref[0] = jnp.concatenate(outs, axis=1).astype(BF16)


def _in_proj(x, g_mix, w_in, g_sgu, w_sp, b_sp, mem, g_mem, w_kv):
    b, s, d = x.shape
    m = mem.shape[1]
    xa = XA_HEADS * XA_HEAD_DIM
    const2 = lambda i, j: (0, 0)
    const3 = lambda i, j: (0, 0, 0)
    tok = lambda i, j: (i, j, 0)
    per_b = lambda i, j: (i, 0, 0)
    out = jax.ShapeDtypeStruct((b, s, SSM_WIDTH), BF16)
    nc = TM_IN // SSM_CHUNK
    tiles = s // TM_IN
    u2 = jax.ShapeDtypeStruct((SSM_GROUPS, b * s // SSM_CHUNK, SSM_CHUNK * SSM_GROUP), BF16)
    return pl.pallas_call(
        _in_body,
        grid=(b, s // TM_IN),
        in_specs=[pl.BlockSpec((1, TM_IN, d), tok),
                  pl.BlockSpec((1, d), const2),
                  pl.BlockSpec(w_in.shape, const2),
                  pl.BlockSpec((1, SGU_WIDTH), const2),
                  pl.BlockSpec(w_sp.shape, const3),
                  pl.BlockSpec(b_sp.shape, const3),
                  pl.BlockSpec((1, m, d), per_b),
                  pl.BlockSpec((1, d), const2),
                  pl.BlockSpec(w_kv.shape, const2, pipeline_mode=pl.Buffered(1))],
        out_specs=[pl.BlockSpec((SSM_GROUPS, nc, SSM_CHUNK * SSM_GROUP), lambda i, j: (0, i * tiles + j, 0)),
                   pl.BlockSpec((1, TM_IN, SSM_WIDTH), tok),
                   pl.BlockSpec((1, TM_IN, SSM_WIDTH), tok)],
        out_shape=[u2, out, out],
        scratch_shapes=[pltpu.VMEM((SSM_WIDTH // LANES, TM_IN, LANES), F32),
                        pltpu.VMEM((m, xa), BF16),
                        pltpu.VMEM((m, xa), BF16)],
        compiler_params=pltpu.CompilerParams(dimension_semantics=("arbitrary", "arbitrary"),
                                             vmem_limit_bytes=VMEM_LIMIT),
        name="in_proj",
    )(x, g_mix, w_in, g_sgu, w_sp, b_sp, mem, g_mem, w_kv)


def _alternate(*stages):
    live = list(stages)
    while live:
        live = [s for s in live if next(s, True) is None]


def _ssm_params(lam_re, lam_im, log_dt, b_re, b_im, c_re, c_im, d_skip):
    g, p = lam_re.shape
    dup = lambda a: jnp.concatenate([a, a], axis=-1)
    lam = jnp.stack([dup(lam_re), dup(lam_im), jnp.broadcast_to(log_dt[:, None], (g, 2 * p))], axis=1)
    brt = b_re.transpose(0, 2, 1)
    bit = b_im.transpose(0, 2, 1)
    cat = lambda a, b: jnp.concatenate([a, b], axis=-1)
    bc = jnp.stack([cat(brt, bit), cat(bit, brt), cat(c_re, -c_im), cat(-c_im, -c_re)], axis=1)
    d2 = jnp.tile(d_skip.reshape(g, 1, SSM_GROUP), (1, 1, SSM_CHUNK))
    return lam, bc, d2


def _ssm_operators(lam_ref, bc_ref, ccat_ref, n_ref, m_ref):
    lam_re = lam_ref[0:1, :]
    lam_im = lam_ref[1:2, :]
    dt = jnp.exp(lam_ref[2:3, :])
    ar = lam_re * dt
    ai = lam_im * dt
    lane = lax.broadcasted_iota(jnp.int32, (1, LANES), 1)
    sgn = jnp.where(lane >= SSM_STATE, 1.0, -1.0)

    def powers(j):
        mag = jnp.exp(ar * j)
        ph = ai * j
        return mag * jnp.cos(ph), mag * jnp.sin(ph)

    pos = lax.broadcasted_iota(jnp.int32, (SSM_CHUNK, 1), 0).astype(F32)
    p_re, p_im = powers(pos)
    r_re, r_im = powers((SSM_CHUNK - 1) - pos)
    one_re, one_im = powers(jnp.ones((1, 1), F32))
    q_re = p_re * one_re - p_im * one_im
    q_im = p_re * one_im + p_im * one_re
    step = lax.shift_left(jnp.full((8, 1), SSM_CHUNK, jnp.int32),
                          lax.broadcasted_iota(jnp.int32, (8, 1), 0)).astype(F32)
    s_re, s_im = powers(step)

    den = lam_re * lam_re + lam_im * lam_im
    f_re = ((one_re - 1.0) * lam_re + one_im * lam_im) / den
    f_im = (one_im * lam_re - (one_re - 1.0) * lam_im) / den
    b1, b2, ca, cb = bc_ref[0], bc_ref[1], bc_ref[2], bc_ref[3]
    bb1 = f_re * b1 + (sgn * f_im) * b2
    bb2 = f_re * b2 - (sgn * f_im) * b1
    r_ims = sgn * r_im
    for s in range(SSM_CHUNK):
        blk = pl.ds(s * SSM_GROUP, SSM_GROUP)
        ccat_ref[blk, :] = ca * p_re[s:s + 1, :] + cb * p_im[s:s + 1, :]
        m_ref[blk, :] = (ca * q_re[s:s + 1, :] + cb * q_im[s:s + 1, :]).astype(m_ref.dtype)
        n_ref[blk, :] = (bb1 * r_re[s:s + 1, :] + bb2 * r_ims[s:s + 1, :]).astype(n_ref.dtype)
    return bb1, s_re, sgn * s_im


def _ssm_body(u_ref, lam_ref, bc_ref, d2_ref, y_ref, toep_ref, ccat_ref, n_ref, m_ref, *, n_seq):
    for g in range(u_ref.shape[0]):
        _ssm_group(u_ref.at[g], lam_ref.at[g], bc_ref.at[g], d2_ref.at[g], y_ref.at[g],
                   toep_ref.at[g], ccat_ref.at[g], n_ref.at[g], m_ref.at[g], n_seq)


def _ssm_group(u_ref, lam_ref, bc_ref, d2_ref, y_ref, toep_ref, ccat_ref, n_ref, m_ref, n_seq):
    bcat, lr, li = _ssm_operators(lam_ref, bc_ref, ccat_ref, n_ref, m_ref)
    kern = lax.dot_general(bcat, ccat_ref[...], _NT, precision=lax.Precision.HIGHEST,
                           preferred_element_type=F32)
    col = lax.broadcasted_iota(jnp.int32, kern.shape, 1)
    for s in range(SSM_CHUNK):
        shifted = pltpu.roll(kern, s * SSM_GROUP, 1) if s else kern
        toep_ref[s * SSM_GROUP:(s + 1) * SSM_GROUP, :] = jnp.where(
            col >= s * SSM_GROUP, shifted, 0.0).astype(BF16)

    u = u_ref[...]
    rows = u.shape[0]
    per = rows // n_seq
    y = _dot(u, toep_ref[...])
    st = _dot(u, n_ref[...])
    row = lax.broadcasted_iota(jnp.int32, (per, LANES), 0)
    prev = []
    for b in range(n_seq):
        x = st[b * per:(b + 1) * per]
        k = 0
        while (1 << k) < per:
            d = 1 << k
            sh = jnp.where(row >= d, pltpu.roll(x, d, 0), 0.0)
            x = x + sh * lr[k:k + 1, :] + pltpu.roll(sh, SSM_STATE, 1) * li[k:k + 1, :]
            k += 1
        prev.append(jnp.where(row >= 1, pltpu.roll(x, 1, 0), 0.0))
    xp = jnp.concatenate(prev, axis=0).astype(BF16)
    y = y + lax.dot_general(xp, m_ref[...], _NT, preferred_element_type=F32) + d2_ref[...] * u.astype(F32)
    y_ref[...] = _gelu(y).astype(BF16)


def _ssm(u2, lam, bc, d2, n_seq):
    g, rows, w = u2.shape
    assert rows // n_seq <= 1 << 8, "lam_bar^(16*2^k) is prepared for 8 scan steps"
    gs = SSM_GROUPS_PER_STEP
    blk = lambda a: pl.BlockSpec((gs,) + a.shape[1:], lambda i: (i,) + (0,) * (a.ndim - 1))
    return pl.pallas_call(
        functools.partial(_ssm_body, n_seq=n_seq),
        grid=(g // gs,),
        in_specs=[blk(u2), blk(lam), blk(bc), blk(d2)],
        out_specs=blk(u2),
        out_shape=jax.ShapeDtypeStruct(u2.shape, BF16),
        scratch_shapes=[pltpu.VMEM((gs, w, w), BF16),
                        pltpu.VMEM((gs, w, 2 * SSM_STATE), F32),
                        pltpu.VMEM((gs, w, 2 * SSM_STATE), BF16),
                        pltpu.VMEM((gs, w, 2 * SSM_STATE), BF16)],
        compiler_params=pltpu.CompilerParams(dimension_semantics=("arbitrary",),
                                             vmem_limit_bytes=VMEM_LIMIT),
        name="ssm",
    )(u2, lam, bc, d2)


def _mix_tile(x_ref, y2_ref, yb_ref, yc_ref, gmix_ref, wgate_ref, bgate_ref, wglu_ref, bglu_ref,
              wbr_ref, wout_ref, tok_ref, h_ref, hkeep_ref):
    x = x_ref[...]
    n = _rms(x, gmix_ref[...]).astype(BF16)

    def gated(b, c, y):
        cols = pl.ds(b * D_MODEL + c * MERGE_COLS, MERGE_COLS)
        gate = _sigmoid(_dot(n, wgate_ref[:, cols].astype(BF16)) + bgate_ref[:, cols])
        return gate * _dot(y, wbr_ref[b, :, pl.ds(c * MERGE_COLS, MERGE_COLS)].astype(BF16))

    n_blocks = D_MODEL // MERGE_COLS
    yb = yb_ref[...]
    yc = yc_ref[...]
    head = gated(1, 0, yb) + gated(2, 0, yc)
    yield
    _chunks_to_tokens(y2_ref, tok_ref)
    ys = jnp.concatenate([tok_ref[k] for k in range(SSM_WIDTH // LANES)], axis=1).astype(BF16)
    glu = _dot(ys, wglu_ref[...].astype(BF16)) + bglu_ref[...]
    ya = (glu[:, :SSM_WIDTH] * _sigmoid(glu[:, SSM_WIDTH:])).astype(BF16)
    merged = [(head + gated(0, 0, ya)).astype(BF16)]
    for c in range(1, n_blocks):
        yield
        acc = gated(0, c, ya)
        yield
        acc = acc + gated(1, c, yb)
        yield
        merged.append((acc + gated(2, c, yc)).astype(BF16))
    yield
    h = x + _dot(jnp.concatenate(merged, axis=1), wout_ref[...].astype(BF16))
    h_ref[...] = h
    hkeep_ref[...] = h


def _route_tile(hkeep_ref, gffn_ref, wrt_ref, brt_ref, xnp_ref, rt_ref, rtt_ref, cnt_ref, carry_ref):
    h = hkeep_ref[...]
    xn = _rms(h, gffn_ref[...])
    yield
    packed = _pack_bf16_pair(xn)
    for j in range(SC_SPLIT):
        xnp_ref[j] = packed[:, j * SC_ROW:(j + 1) * SC_ROW]
    yield

    x_hi = xn.astype(BF16)
    x_lo = (xn - x_hi.astype(F32)).astype(BF16)
    head = _dot(x_hi, wrt_ref[...])
    logits = (head[:, :LANES] + head[:, LANES:] + _dot(x_lo, wrt_ref[:, :LANES])) + brt_ref[...]
    yield
    tm = logits.shape[0]
    lane_i = lax.broadcasted_iota(jnp.int32, (tm, LANES), 1)
    lane = lane_i.astype(F32)
    neg = jnp.float32(-3.0e38)
    big = jnp.float32(LANES)
    gmask = lane_i < N_GROUPS
    gl = jnp.where(gmask, logits, neg)
    gmax = jnp.max(gl, axis=-1, keepdims=True)
    gidx = jnp.min(jnp.where(gl == gmax, lane, big), axis=-1, keepdims=True)
    gsum = jnp.sum(jnp.where(gmask, jnp.exp(gl - gmax), 0.0), axis=-1, keepdims=True)
    g_w = 1.0 / gsum
    yield
    e_lane = lane_i - ROUTE_LANE0
    lane_group = (e_lane // EXPERTS_PER_GROUP).astype(F32)
    emask = (e_lane >= 0) & (e_lane < N_EXPERTS) & (lane_group == gidx)
    el = jnp.where(emask, logits, neg)
    m1 = jnp.max(el, axis=-1, keepdims=True)
    i1 = jnp.min(jnp.where(el == m1, lane, big), axis=-1, keepdims=True)
    yield
    el2 = jnp.where(lane == i1, neg, el)
    m2 = jnp.max(el2, axis=-1, keepdims=True)
    i2 = jnp.min(jnp.where(el2 == m2, lane, big), axis=-1, keepdims=True)
    t = jnp.exp(m2 - m1)
    w1 = g_w / (1.0 + t)
    w2 = g_w * t / (1.0 + t)
    yield

    sel1 = lane == i1
    sel2 = lane == i2
    onehot = jnp.where(sel1 | sel2, 1.0, 0.0)
    r_i = lax.broadcasted_iota(jnp.int32, (tm, tm), 0)
    c_i = lax.broadcasted_iota(jnp.int32, (tm, tm), 1)
    stril = jnp.where(c_i < r_i, 1.0, 0.0).astype(BF16)
    cum = _dot(stril, onehot.astype(BF16)) + carry_ref[0:1, :]
    rank1 = jnp.sum(jnp.where(sel1, cum, 0.0), axis=-1, keepdims=True)
    rank2 = jnp.sum(jnp.where(sel2, cum, 0.0), axis=-1, keepdims=True)
    carry_ref[...] = carry_ref[...] + jnp.sum(onehot, axis=0, keepdims=True)
    cnt_ref[...] = carry_ref[...]
    yield

    cols = (i1 - ROUTE_LANE0, i2 - ROUTE_LANE0, rank1, rank2, w1, w2)
    rt = jnp.zeros((tm, LANES), F32)
    for c, val in enumerate(cols):
        rt = jnp.where(lane_i == c, val, rt)
    rt_ref[...] = rt
    rtt_ref[...] = rt.T[:8]


def _merge_body(x_ref, y2_ref, yb_ref, yc_ref, gmix_ref, wgate_ref, bgate_ref, wglu_ref, bglu_ref,
                wbr_ref, wout_ref, gffn_ref, wrt_ref, brt_ref,
                h_ref, xnp_ref, rt_ref, rtt_ref, cnt_ref, carry_ref, tok_ref, hkeep_ref):
    i = pl.program_id(0)
    last = pl.num_programs(0) - 1
    cur = hkeep_ref.at[i % 2]
    prev = hkeep_ref.at[(i + 1) % 2]

    def mix():
        return _mix_tile(x_ref, y2_ref, yb_ref, yc_ref, gmix_ref, wgate_ref, bgate_ref, wglu_ref,
                         bglu_ref, wbr_ref, wout_ref, tok_ref, h_ref, cur)

    def route():
        return _route_tile(prev, gffn_ref, wrt_ref, brt_ref, xnp_ref, rt_ref, rtt_ref, cnt_ref, carry_ref)

    @pl.when(i == 0)
    def _():
        carry_ref[...] = jnp.zeros_like(carry_ref)
        _alternate(mix())

    @pl.when((i > 0) & (i < last))
    def _():
        _alternate(route(), mix())

    @pl.when(i == last)
    def _():
        _alternate(route())


def _merge_route(x, y2, yb, yc, g_mix, w_gate, b_gate, w_glu, b_glu, w_br, w_out, g_ffn, w_rt, b_rt):
    t, d = x.shape
    tm = TM_MERGE
    tiles = t // tm
    mixed = lambda i: jnp.minimum(i, tiles - 1)
    routed = lambda i: jnp.maximum(i - 1, 0)
    c2 = lambda i: (0, 0)
    c3 = lambda i: (0, 0, 0)
    full = lambda a: pl.BlockSpec(a.shape, c2 if a.ndim == 2 else c3, pipeline_mode=pl.Buffered(1))
    return pl.pallas_call(
        _merge_body,
        grid=(tiles + 1,),
        in_specs=[pl.BlockSpec((tm, d), lambda i: (mixed(i), 0)),
                  pl.BlockSpec((SSM_GROUPS, tm // SSM_CHUNK, SSM_CHUNK * SSM_GROUP),
                               lambda i: (0, mixed(i), 0)),
                  pl.BlockSpec((tm, SSM_WIDTH), lambda i: (mixed(i), 0)),
                  pl.BlockSpec((tm, SSM_WIDTH), lambda i: (mixed(i), 0)),
                  full(g_mix), full(w_gate), full(b_gate), full(w_glu), full(b_glu),
                  full(w_br), full(w_out), full(g_ffn), full(w_rt), full(b_rt)],
        out_specs=[pl.BlockSpec((tm, d), lambda i: (mixed(i), 0)),
                   pl.BlockSpec((SC_SPLIT, tm, SC_ROW), lambda i: (0, routed(i), 0)),
                   pl.BlockSpec((tm, LANES), lambda i: (routed(i), 0)),
                   pl.BlockSpec((8, tm), lambda i: (0, routed(i))),
                   pl.BlockSpec((8, LANES), c2)],
        out_shape=[jax.ShapeDtypeStruct((t, d), F32),
                   jax.ShapeDtypeStruct((SC_SPLIT, t, SC_ROW), jnp.uint32),
                   jax.ShapeDtypeStruct((t, LANES), F32),
                   jax.ShapeDtypeStruct((8, t), F32),
                   jax.ShapeDtypeStruct((8, LANES), F32)],
        scratch_shapes=[pltpu.VMEM((8, LANES), F32),
                        pltpu.VMEM((SSM_WIDTH // LANES, tm, LANES), F32),
                        pltpu.VMEM((2, tm, d), F32)],
        compiler_params=pltpu.CompilerParams(dimension_semantics=("arbitrary",),
                                             vmem_limit_bytes=VMEM_LIMIT),
        name="merge_route",
    )(x, y2, yb, yc, g_mix, w_gate, b_gate, w_glu, b_glu, w_br, w_out, g_ffn, w_rt, b_rt)


def _slot_body(rtt_ref, cnt_ref, out_ref, *, nslots):
    tl = rtt_ref.shape[1]
    blocks = jnp.ceil(cnt_ref[...] * (1.0 / BM))
    k_i = lax.broadcasted_iota(jnp.int32, (LANES, LANES), 0)
    l_i = lax.broadcasted_iota(jnp.int32, (LANES, LANES), 1)
    before = jnp.where(k_i < l_i, 1.0, 0.0).astype(BF16)
    first_blk = _dot(blocks.astype(BF16), before).astype(BF16)
    lane_of = lax.broadcasted_iota(jnp.int32, (LANES, tl), 0).astype(F32) - ROUTE_LANE0
    for k in range(TOP_K):
        onehot = jnp.where(rtt_ref[k:k + 1, :] == lane_of, 1.0, 0.0).astype(BF16)
        start = _dot(first_blk, onehot) * BM
        slot = (start[0:1, :] + rtt_ref[TOP_K + k:TOP_K + k + 1, :]).astype(jnp.int32)
        for j in range(SC_SPLIT):
            out_ref[k * SC_SPLIT + j:k * SC_SPLIT + j + 1, :] = slot + j * nslots


def _slot_rows(rtt, cnt, nslots):
    t = rtt.shape[1]
    tl = SLOT_LANES
    return pl.pallas_call(
        functools.partial(_slot_body, nslots=nslots),
        grid=(t // tl,),
        in_specs=[pl.BlockSpec((8, tl), lambda i: (0, i)),
                  pl.BlockSpec((8, LANES), lambda i: (0, 0))],
        out_specs=pl.BlockSpec((TOP_K * SC_SPLIT, tl), lambda i: (0, i)),
        out_shape=jax.ShapeDtypeStruct((TOP_K * SC_SPLIT, t), jnp.int32),
        compiler_params=pltpu.CompilerParams(dimension_semantics=("arbitrary",)),
        name="slot_rows",
    )(rtt, cnt)


def _plan_blocks(counts_ref, blk_expert, blk_valid, blk_first, blk_run, run_expert):
    nb = blk_expert.shape[0]

    def per_expert(e, carry):
        cursor, run = carry
        count = counts_ref[e]
        n_blk = (count + (BM - 1)) // BM

        def per_block(b, cur):
            blk_expert[cur] = e
            blk_valid[cur] = jnp.minimum(count - b * BM, BM)
            blk_first[cur] = (b == 0).astype(jnp.int32)
            blk_run[cur] = run
            return cur + 1

        run_expert[run] = e
        return lax.fori_loop(0, n_blk, per_block, cursor), run + (n_blk > 0).astype(jnp.int32)

    cursor, runs = lax.fori_loop(0, counts_ref.shape[0], per_expert, (jnp.int32(0), jnp.int32(0)))
    run_expert[runs] = -1
    run_expert[runs + 1] = -1

    def empty(j, carry):
        blk_expert[j] = 0
        blk_valid[j] = 0
        blk_first[j] = 0
        blk_run[j] = runs - 1
        return carry

    lax.fori_loop(cursor, nb, empty, 0)


def _expert_body(counts_ref, buf_ref, w1_hbm, w3_hbm, w2_hbm, out_ref, w1_buf, w3_buf, w2_buf, sem,
                 blk_expert, blk_valid, blk_first, blk_run, run_expert):
    step = pl.program_id(0)

    @pl.when(step == 0)
    def _():
        _plan_blocks(counts_ref, blk_expert, blk_valid, blk_first, blk_run, run_expert)

    def weight_copies(e, s):
        return (pltpu.make_async_copy(w1_hbm.at[e], w1_buf.at[s], sem.at[s, 0]),
                pltpu.make_async_copy(w3_hbm.at[e], w3_buf.at[s], sem.at[s, 1]),
                pltpu.make_async_copy(w2_hbm.at[e], w2_buf.at[s], sem.at[s, 2]))

    for sub in range(BLOCKS_PER_STEP):
        i = step * BLOCKS_PER_STEP + sub
        block = pl.ds(sub * BM, BM)
        _expert_block(i, buf_ref.at[:, block, :], out_ref.at[:, block, :], weight_copies,
                      (w1_buf, w3_buf, w2_buf), blk_expert, blk_valid, blk_first, blk_run, run_expert)


def _expert_block(i, buf_ref, out_ref, weight_copies, weights, blk_expert, blk_valid, blk_first, blk_run,
                  run_expert):
    w1_buf, w3_buf, w2_buf = weights
    expert = blk_expert[i]
    valid = blk_valid[i]
    run = blk_run[i]
    slot = run % WEIGHT_SLOTS
    ahead1 = run_expert[run + 1]
    ahead2 = run_expert[run + 2]

    @pl.when(i == 0)
    def _():
        for c in weight_copies(expert, slot):
            c.start(priority=WEIGHT_DMA_PRIORITY)

        @pl.when(ahead1 >= 0)
        def _():
            for c in weight_copies(ahead1, (slot + 1) % WEIGHT_SLOTS):
                c.start(priority=WEIGHT_DMA_PRIORITY)

    @pl.when(blk_first[i] == 1)
    def _():
        for c in weight_copies(expert, slot):
            c.wait()

        @pl.when(ahead2 >= 0)
        def _():
            for c in weight_copies(ahead2, (slot + 2) % WEIGHT_SLOTS):
                c.start(priority=WEIGHT_DMA_PRIORITY)

    def mlp(rows):
        x = _unpack_bf16_pair(jnp.concatenate([buf_ref[j, :rows, :] for j in range(SC_SPLIT)], axis=1))
        row = lax.broadcasted_iota(jnp.int32, x.shape, 0)
        x = jnp.where(row < valid, x, 0.0).astype(BF16)
        h1 = _dot(x, w1_buf[slot].astype(BF16))
        h3 = _dot(x, w3_buf[slot].astype(BF16))
        a = (h1 * _sigmoid(h1) * h3).astype(BF16)
        packed = _pack_bf16_pair(_dot(a, w2_buf[slot].astype(BF16)))
        for j in range(SC_SPLIT):
            out_ref[j, :rows, :] = packed[:, j * SC_ROW:(j + 1) * SC_ROW]
            if rows < BM:
                out_ref[j, rows:, :] = jnp.zeros((BM - rows, SC_ROW), out_ref.dtype)

    for rows in range(BM_STEP, BM + 1, BM_STEP):
        @pl.when((valid > rows - BM_STEP) & (valid <= rows))
        def _():
            mlp(rows)

    @pl.when(valid <= 0)
    def _():
        out_ref[...] = jnp.zeros_like(out_ref)


def _experts(counts, buf, w1, w3, w2):
    _, nslots, _ = buf.shape
    nb = nslots // BM
    rows = pl.BlockSpec((SC_SPLIT, BLOCKS_PER_STEP * BM, SC_ROW), lambda i, counts: (0, i, 0))
    hbm = pl.BlockSpec(memory_space=pl.ANY)
    table = pltpu.SMEM((nb,), jnp.int32)
    grid_spec = pltpu.PrefetchScalarGridSpec(
        num_scalar_prefetch=1,
        grid=(nb // BLOCKS_PER_STEP,),
        in_specs=[rows, hbm, hbm, hbm],
        out_specs=rows,
        scratch_shapes=[pltpu.VMEM((WEIGHT_SLOTS,) + w1.shape[1:], w1.dtype),
                        pltpu.VMEM((WEIGHT_SLOTS,) + w3.shape[1:], w3.dtype),
                        pltpu.VMEM((WEIGHT_SLOTS,) + w2.shape[1:], w2.dtype),
                        pltpu.SemaphoreType.DMA((WEIGHT_SLOTS, 3)),
                        table, table, table, table,
                        pltpu.SMEM((counts.shape[0] + 2,), jnp.int32)],
    )
    return pl.pallas_call(
        _expert_body,
        grid_spec=grid_spec,
        out_shape=jax.ShapeDtypeStruct(buf.shape, jnp.uint32),
        compiler_params=pltpu.CompilerParams(dimension_semantics=("arbitrary",),
                                             vmem_limit_bytes=VMEM_LIMIT),
        name="experts",
    )(counts, buf, w1, w3, w2)


def _sc_mesh():
    return plsc.VectorSubcoreMesh(core_axis_name="core", subcore_axis_name="subcore")


def _dispatch_rows(rows, dest0, dest1, nslots):
    t, w = rows.shape
    win = SC_WINDOW
    idx_spec = pl.BlockSpec((1, win), lambda i: (0, i))

    @functools.partial(pl.kernel, mesh=_sc_mesh(), scratch_types=[],
                       out_type=jax.ShapeDtypeStruct((nslots, w), rows.dtype), name="dispatch_rows")
    def run(rows_hbm, i0_hbm, i1_hbm, out_hbm):
        def body(rows_vmem, i0_vmem, i1_vmem):
            pltpu.sync_copy(rows_vmem, out_hbm.at[i0_vmem.at[0]])
            pltpu.sync_copy(rows_vmem, out_hbm.at[i1_vmem.at[0]])

        pltpu.emit_pipeline(
            body, grid=(t // win,),
            in_specs=[pl.BlockSpec((win, w), lambda i: (i, 0), pipeline_mode=pl.Buffered(1)),
                      idx_spec, idx_spec],
            out_specs=[],
            core_axis_name=("core", "subcore"),
            dimension_semantics=(pltpu.PARALLEL,),
        )(rows_hbm, i0_hbm, i1_hbm)

    return run(rows, dest0.reshape(1, t), dest1.reshape(1, t))


def _gather_rows(table, idx):
    n = idx.shape[0]
    w = table.shape[1]
    win = SC_WINDOW

    @functools.partial(pl.kernel, mesh=_sc_mesh(), scratch_types=[],
                       out_type=jax.ShapeDtypeStruct((n, w), table.dtype), name="gather_rows")
    def run(table_hbm, i_hbm, out_hbm):
        def body(i_vmem, out_vmem):
            pltpu.sync_copy(table_hbm.at[i_vmem.at[0]], out_vmem)

        pltpu.emit_pipeline(
            body, grid=(n // win,),
            in_specs=[pl.BlockSpec((1, win), lambda i: (0, i))],
            out_specs=[pl.BlockSpec((win, w), lambda i: (i, 0), pipeline_mode=pl.Buffered(1))],
            core_axis_name=("core", "subcore"),
            dimension_semantics=(pltpu.PARALLEL,),
        )(i_hbm, out_hbm)

    return run(table, idx.reshape(1, n))


def _combine_body(h_ref, g_ref, rt_ref, gfin_ref, out_ref):
    rt = rt_ref[...]
    y = h_ref[...]
    for k in range(TOP_K):
        rows = jnp.concatenate([g_ref[k * SC_SPLIT + j] for j in range(SC_SPLIT)], axis=1)
        y = y + rt[:, 4 + k:5 + k] * _unpack_bf16_pair(rows)
    out_ref[...] = _rms(y, gfin_ref[...])


def _combine(h, g, rt, g_final):
    t, d = h.shape
    tm = TM_OUT
    tok = lambda i: (i, 0)
    return pl.pallas_call(
        _combine_body,
        grid=(t // tm,),
        in_specs=[pl.BlockSpec((tm, d), tok),
                  pl.BlockSpec((TOP_K * SC_SPLIT, tm, SC_ROW), lambda i: (0, i, 0)),
                  pl.BlockSpec((tm, LANES), tok),
                  pl.BlockSpec((1, d), lambda i: (0, 0))],
        out_specs=pl.BlockSpec((tm, d), tok),
        out_shape=jax.ShapeDtypeStruct((t, d), F32),
        compiler_params=pltpu.CompilerParams(dimension_semantics=("arbitrary",),
                                             vmem_limit_bytes=VMEM_LIMIT),
        name="combine",
    )(h, g, rt, g_final)


def _layer(h, mem, g_mix, g_mem, w_in, w_gate, b_gate, lam_re, lam_im, log_dt, b_re, b_im,
           c_re, c_im, d_skip, w_glu, b_glu, g_sgu, w_spatial, b_spatial, w_kv, w_branch,
           w_out, g_ffn, w_group, b_group, w_router, b_router, w1, w3, w2, g_out):
    bsz, s, d = h.shape
    t = bsz * s
    row = lambda a: a.reshape(1, -1)

    tril = jnp.tril(jnp.ones((CHUNK, CHUNK), dtype=bool))
    w_sp = jnp.where(tril, w_spatial, 0.0).astype(BF16)
    b_sp = jnp.broadcast_to(b_spatial[:, :, None], (SGU_HEADS, CHUNK, SGU_HEAD_DIM))
    u2, y_b, y_c = _in_proj(h, row(g_mix), w_in.astype(BF16), row(g_sgu), w_sp, b_sp,
                            mem, row(g_mem), w_kv)
    y2 = _ssm(u2, *_ssm_params(lam_re, lam_im, log_dt, b_re, b_im, c_re, c_im, d_skip), n_seq=bsz)

    pad = LANES - N_GROUPS - N_EXPERTS
    w_rt = jnp.concatenate([w_group, w_router, jnp.zeros((d, pad), F32)], axis=1)
    w_rt_hi = w_rt.astype(BF16)
    w_rt = jnp.concatenate([w_rt_hi, (w_rt - w_rt_hi.astype(F32)).astype(BF16)], axis=1)
    b_rt =jnp.concatenate([b_group, b_router, jnp.zeros((pad,), F32)]).reshape(1, LANES)
    h2, xnp, rt, rtt, cnt = _merge_route(
        h.reshape(t, d), y2, y_b.reshape(t, -1), y_c.reshape(t, -1), row(g_mix),
        w_gate, row(b_gate), w_glu, row(b_glu), w_branch, w_out, row(g_ffn), w_rt, b_rt)

    assert BM & (BM - 1) == 0, "block padding arithmetic assumes a power-of-two block"
    counts = cnt[0, ROUTE_LANE0:ROUTE_LANE0 + N_EXPERTS].astype(jnp.int32)
    nb = (t * TOP_K) // BM + N_EXPERTS
    assert nb < 256, "slot_rows keeps block counts in bf16 matmul operands"
    nslots = nb * BM
    dest_p = _slot_rows(rtt, cnt, nslots).reshape(TOP_K, SC_SPLIT * t)
    buf = _dispatch_rows(xnp.reshape(SC_SPLIT * t, SC_ROW), dest_p[0], dest_p[1], SC_SPLIT * nslots)
    yb = _experts(counts, buf.reshape(SC_SPLIT, nslots, SC_ROW), w1, w3, w2)
    g = _gather_rows(yb.reshape(SC_SPLIT * nslots, SC_ROW), dest_p.reshape(-1))
    out = _combine(h2, g.reshape(TOP_K * SC_SPLIT, t, SC_ROW), rt, row(g_out))
    return out.reshape(bsz, s, d)


def kernel(x, mem, g_mix, g_mem, w_in, w_gate, b_gate, lam_re, lam_im, log_dt, b_re, b_im, c_re,
           c_im, d_skip, w_glu, b_glu, g_sgu, w_spatial, b_spatial, w_kv, w_branch, w_out, g_ffn,
           w_group, b_group, w_router, b_router, w1, w3, w2, g_final):
    assert g_mix.shape[0] == 1, "single-layer stack"
    return _layer(x, mem, g_mix[0], g_mem[0], w_in[0], w_gate[0], b_gate[0], lam_re[0], lam_im[0],
                  log_dt[0], b_re[0], b_im[0], c_re[0], c_im[0], d_skip[0], w_glu[0], b_glu[0],
                  g_sgu[0], w_spatial[0], b_spatial[0], w_kv[0], w_branch[0], w_out[0], g_ffn[0],
                  w_group[0], b_group[0], w_router[0], b_router[0], w1[0], w3[0], w2[0], g_final)
```
